```python
import math
import jax
import jax.numpy as jnp
from jax import lax
import numpy as np

D_MODEL = 1024
BATCH = 32
SEQ = 256
DEPTH = 1
DEC_BATCH = 4
DEC_SEQ = 4096
PAST_LEN = 256

GRID_W = 64
D_MIX = D_MODEL
D_RNN = D_MIX // 2
D_SGU = D_MIX - D_RNN
N_HEADS_RNN = 8
HEAD_RNN = D_RNN // N_HEADS_RNN
N_HEADS_SGU = 8
HEAD_SGU = D_SGU // N_HEADS_SGU
CHUNK = 128
CONV_W = 4
CONV_LEFT = 2
RG_C = 8.0
N_GROUPS = 4
EXPERTS_PER_GROUP = 4
N_EXPERTS = N_GROUPS * EXPERTS_PER_GROUP
TOP_K = 2
D_EXPERT = 512
EPS = 1e-6
POS_BASE = 10000.0

kernel_name = "hybrid_rglru_sgu_hmoe_flow_step"


def rmsnorm(x, g):
    xf = x.astype(jnp.float32)
    y = xf * lax.rsqrt(jnp.mean(xf * xf, axis=-1, keepdims=True) + EPS) * g.astype(jnp.float32)
    return y.astype(x.dtype)


def grid_pos_embedding(n_tokens, dtype):
    rows = n_tokens // GRID_W
    row = jnp.broadcast_to(jnp.arange(rows, dtype=jnp.float32)[:, None], (rows, GRID_W)).reshape(-1)
    col = jnp.broadcast_to(jnp.arange(GRID_W, dtype=jnp.float32)[None, :], (rows, GRID_W)).reshape(-1)
    n_freq = D_MODEL // 4
    freq = jnp.exp(-math.log(POS_BASE) * jnp.arange(n_freq, dtype=jnp.float32) / n_freq)
    ar = row[:, None] * freq
    ac = col[:, None] * freq
    emb = jnp.concatenate([jnp.sin(ar), jnp.cos(ar), jnp.sin(ac), jnp.cos(ac)], axis=-1)
    return emb.astype(dtype)


def depthwise_conv(x, w, b):
    L = x.shape[1]
    xp = jnp.pad(x, ((0, 0), (CONV_LEFT, CONV_W - 1 - CONV_LEFT), (0, 0)))
    out = b
    for k in range(CONV_W):
        out = out + xp[:, k:k + L] * w[k]
    return out


def _linear_combine(left, right):
    a_l, b_l = left
    a_r, b_r = right
    return a_l * a_r, a_r * b_l + b_r


def rglru_scan(xc, wa, ba, wx, bx, lam, h0, reverse):
    B, L, _ = xc.shape
    xh = xc.reshape(B, L, N_HEADS_RNN, HEAD_RNN)
    r = jax.nn.sigmoid(jnp.einsum('blhi,hij->blhj', xh, wa.astype(jnp.float32)).reshape(B, L, D_RNN)
                       + ba.astype(jnp.float32))
    i = jax.nn.sigmoid(jnp.einsum('blhi,hij->blhj', xh, wx.astype(jnp.float32)).reshape(B, L, D_RNN)
                       + bx.astype(jnp.float32))
    log_a = -RG_C * r * jax.nn.softplus(-lam.astype(jnp.float32))
    a = jnp.exp(log_a)
    b = jnp.sqrt(-jnp.expm1(2.0 * log_a)) * (i * xc)
    edge = L - 1 if reverse else 0
    b = b.at[:, edge].add(a[:, edge] * h0)
    _, h = lax.associative_scan(_linear_combine, (a, b), axis=1, reverse=reverse)
    return h


def spatial_gate(u, v, sgu_g, sgu_w, sgu_b):
    B, L, _ = v.shape
    vh = rmsnorm(v, sgu_g).reshape(B, L // CHUNK, CHUNK, N_HEADS_SGU, HEAD_SGU)
    s = jnp.einsum('hpq,bnqhc->bnphc', sgu_w, vh) + sgu_b.T[None, None, :, :, None]
    return u * s.reshape(B, L, D_SGU)


def token_mixer(h, h0, w_in, conv_w, conv_b, rg_wa, rg_ba, rg_wx, rg_bx, rg_lambda,
                sgu_g, sgu_w, sgu_b, w_out):
    z = h @ w_in
    xr, gate, u, v = jnp.split(z, [D_RNN, 2 * D_RNN, 2 * D_RNN + D_SGU], axis=-1)
    xc = depthwise_conv(xr, conv_w, conv_b).astype(jnp.float32)
    hf = rglru_scan(xc, rg_wa[0], rg_ba[0], rg_wx[0], rg_bx[0], rg_lambda[0], h0[:, 0], False)
    hb = rglru_scan(xc, rg_wa[1], rg_ba[1], rg_wx[1], rg_bx[1], rg_lambda[1], h0[:, 1], True)
    y_rnn = ((hf + hb) * jax.nn.gelu(gate.astype(jnp.float32))).astype(h.dtype)
    y_sgu = spatial_gate(u, v, sgu_g, sgu_w, sgu_b)
    y = jnp.concatenate([y_rnn, y_sgu], axis=-1) @ w_out
    final_state = jnp.stack([hf[:, -1], hb[:, 0]], axis=1)
    return y, final_state


def hier_moe(h, router_g_w, router_g_b, router_e_w, router_e_b, exp_w_gate, exp_w_up, exp_w_down):
    B, L, _ = h.shape
    g_logits = (h @ router_g_w + router_g_b).astype(jnp.float32)
    g_prob = jax.nn.softmax(g_logits, axis=-1)
    g_idx = jnp.argmax(g_logits, axis=-1)
    g_w = jnp.take_along_axis(g_prob, g_idx[..., None], axis=-1)
    e_logits = (h @ router_e_w + router_e_b).astype(jnp.float32).reshape(B, L, N_GROUPS, EXPERTS_PER_GROUP)
    e_logits = jnp.take_along_axis(e_logits, g_idx[..., None, None], axis=2)[:, :, 0]
    e_prob = jax.nn.softmax(e_logits, axis=-1)
    top_w, top_i = lax.top_k(e_prob, TOP_K)
    top_w = top_w / jnp.sum(top_w, axis=-1, keepdims=True) * g_w
    expert_id = g_idx[..., None] * EXPERTS_PER_GROUP + top_i
    combine = jnp.einsum('blk,blke->ble', top_w,
                         jax.nn.one_hot(expert_id, N_EXPERTS, dtype=jnp.float32)).astype(h.dtype)
    gate = jnp.einsum('bld,edf->blef', h, exp_w_gate)
    up = jnp.einsum('bld,edf->blef', h, exp_w_up)
    act = jax.nn.silu(gate) * up * combine[..., None]
    return jnp.einsum('blef,efd->bld', act, exp_w_down)


def trunk_layer(x, cond, h0, w_mod, b_mod, g_pre_mix, g_post_mix, g_pre_ffn, g_post_ffn,
                w_in, conv_w, conv_b, rg_wa, rg_ba, rg_wx, rg_bx, rg_lambda,
                sgu_g, sgu_w, sgu_b, w_out,
                router_g_w, router_g_b, router_e_w, router_e_b, exp_w_gate, exp_w_up, exp_w_down):
    mod = (jax.nn.silu(cond) @ w_mod + b_mod)[:, None, :]
    shift1, scale1, gate1, shift2, scale2, gate2 = jnp.split(mod, 6, axis=-1)
    hn = rmsnorm(x, g_pre_mix) * (1 + scale1) + shift1
    y, final_state = token_mixer(hn, h0, w_in, conv_w, conv_b, rg_wa, rg_ba, rg_wx, rg_bx,
                                 rg_lambda, sgu_g, sgu_w, sgu_b, w_out)
    x = x + gate1 * rmsnorm(y, g_post_mix)
    hn = rmsnorm(x, g_pre_ffn) * (1 + scale2) + shift2
    y = hier_moe(hn, router_g_w, router_g_b, router_e_w, router_e_b, exp_w_gate, exp_w_up, exp_w_down)
    x = x + gate2 * rmsnorm(y, g_post_ffn)
    return x, final_state


def setup_inputs(seed: int = 0) -> dict:
    key = jax.random.key(seed)
    ks = jax.random.split(key, 40)

    def nrm(k, shape, s):
        return jax.random.normal(k, shape, jnp.float32) * s

    a_pow = jax.random.uniform(ks[20], (DEPTH, 2, D_RNN), jnp.float32, minval=0.9, maxval=0.999)
    a0 = a_pow ** (1.0 / RG_C)
    return {
        "x_prompt": nrm(ks[0], (BATCH, SEQ, D_MODEL), 1.0),
        "x_sample": nrm(ks[1], (DEC_BATCH, DEC_SEQ, D_MODEL), 1.0),
        "state_rglru": nrm(ks[2], (DEC_BATCH, DEPTH, 2, D_RNN), 0.5),
        "c": nrm(ks[3], (DEC_BATCH, D_MODEL), 1.0),
        "c_ctx": nrm(ks[4], (D_MODEL,), 1.0),
        "w_mod": nrm(ks[5], (DEPTH, D_MODEL, 6 * D_MODEL), 0.5 * D_MODEL ** -0.5),
        "b_mod": nrm(ks[6], (DEPTH, 6 * D_MODEL), 0.02),
        "g_pre_mix": 1.0 + nrm(ks[7], (DEPTH, D_MODEL), 0.05),
        "g_post_mix": 1.0 + nrm(ks[8], (DEPTH, D_MODEL), 0.05),
        "g_pre_ffn": 1.0 + nrm(ks[9], (DEPTH, D_MODEL), 0.05),
        "g_post_ffn": 1.0 + nrm(ks[10], (DEPTH, D_MODEL), 0.05),
        "w_in": nrm(ks[11], (DEPTH, D_MODEL, 2 * D_RNN + 2 * D_SGU), D_MODEL ** -0.5),
        "conv_w": nrm(ks[12], (DEPTH, CONV_W, D_RNN), CONV_W ** -0.5),
        "conv_b": nrm(ks[13], (DEPTH, D_RNN), 0.01),
        "rg_wa": nrm(ks[14], (DEPTH, 2, N_HEADS_RNN, HEAD_RNN, HEAD_RNN), HEAD_RNN ** -0.5),
        "rg_ba": nrm(ks[15], (DEPTH, 2, D_RNN), 0.01),
        "rg_wx": nrm(ks[16], (DEPTH, 2, N_HEADS_RNN, HEAD_RNN, HEAD_RNN), HEAD_RNN ** -0.5),
        "rg_bx": nrm(ks[17], (DEPTH, 2, D_RNN), 0.01),
        "rg_lambda": jnp.log(a0) - jnp.log1p(-a0),
        "sgu_g": 1.0 + nrm(ks[18], (DEPTH, D_SGU), 0.05),
        "sgu_w": nrm(ks[19], (DEPTH, N_HEADS_SGU, CHUNK, CHUNK), CHUNK ** -0.5),
        "sgu_b": 1.0 + nrm(ks[21], (DEPTH, N_HEADS_SGU, CHUNK), 0.1),
        "w_out": nrm(ks[22], (DEPTH, D_MIX, D_MODEL), D_MIX ** -0.5),
        "router_g_w": nrm(ks[23], (DEPTH, D_MODEL, N_GROUPS), D_MODEL ** -0.5),
        "router_g_b": nrm(ks[24], (DEPTH, N_GROUPS), 0.01),
        "router_e_w": nrm(ks[25], (DEPTH, D_MODEL, N_EXPERTS), D_MODEL ** -0.5),
        "router_e_b": nrm(ks[26], (DEPTH, N_EXPERTS), 0.01),
        "exp_w_gate": nrm(ks[27], (DEPTH, N_EXPERTS, D_MODEL, D_EXPERT), D_MODEL ** -0.5),
        "exp_w_up": nrm(ks[28], (DEPTH, N_EXPERTS, D_MODEL, D_EXPERT), D_MODEL ** -0.5),
        "exp_w_down": nrm(ks[29], (DEPTH, N_EXPERTS, D_EXPERT, D_MODEL), D_EXPERT ** -0.5),
    }


def reference(x_prompt, x_sample, state_rglru, c, c_ctx, w_mod, b_mod, g_pre_mix, g_post_mix,
              g_pre_ffn, g_post_ffn, w_in, conv_w, conv_b, rg_wa, rg_ba, rg_wx, rg_bx, rg_lambda,
              sgu_g, sgu_w, sgu_b, w_out, router_g_w, router_g_b, router_e_w, router_e_b,
              exp_w_gate, exp_w_up, exp_w_down):
    def layer_params(l):
        return (w_mod[l], b_mod[l], g_pre_mix[l], g_post_mix[l], g_pre_ffn[l], g_post_ffn[l],
                w_in[l], conv_w[l], conv_b[l], rg_wa[l], rg_ba[l], rg_wx[l], rg_bx[l], rg_lambda[l],
                sgu_g[l], sgu_w[l], sgu_b[l], w_out[l],
                router_g_w[l], router_g_b[l], router_e_w[l], router_e_b[l],
                exp_w_gate[l], exp_w_up[l], exp_w_down[l])

    n_ctx = x_prompt.shape[0]
    cond_ctx = jnp.broadcast_to(c_ctx, (n_ctx, D_MODEL))
    h0_ctx = jnp.zeros((n_ctx, 2, D_RNN), jnp.float32)
    xp = x_prompt
    states = []
    for l in range(DEPTH):
        xp, st = trunk_layer(xp, cond_ctx, h0_ctx, *layer_params(l))
        states.append(st.astype(state_rglru.dtype))
    new_state_rglru = jnp.stack(states, axis=1)

    xs = x_sample + grid_pos_embedding(x_sample.shape[1], x_sample.dtype)[None]
    for l in range(DEPTH):
        xs, _ = trunk_layer(xs, c, state_rglru[:, l].astype(jnp.float32), *layer_params(l))

    return (xp, xs, new_state_rglru)
```

```python
import functools
import math

import jax
import jax.numpy as jnp
from jax import lax
from jax.experimental import pallas as pl
from jax.experimental.pallas import tpu as pltpu

D_MODEL = 1024
D_RNN = 512
D_SGU = 512
N_HEADS_RNN = 8
HEAD_RNN = D_RNN // N_HEADS_RNN
N_HEADS_SGU = 8
HEAD_SGU = D_SGU // N_HEADS_SGU
CHUNK = 128
GRID_W = 64
RG_C = 8.0
N_GROUPS = 4
EXPERTS_PER_GROUP = 4
N_EXPERTS = N_GROUPS * EXPERTS_PER_GROUP
D_EXPERT = 512
EPS = 1e-6
POS_BASE = 10000.0

LANES = 128
SUBLANES = 8
RNN_HALF = D_RNN // 2
ROUTER_LANES = LANES
E_LANE0 = N_GROUPS

TM = 256
TM_MOE = 1024
SCAN_CHUNK = 128
VMEM_LIMIT = 56 * 1024 * 1024

F32 = jnp.float32
BF16 = jnp.bfloat16


def _params(sem):
    return pltpu.CompilerParams(dimension_semantics=sem, vmem_limit_bytes=VMEM_LIMIT)


def _rms(x):
    return x * lax.rsqrt(jnp.mean(x * x, axis=-1, keepdims=True) + EPS)


def _sigmoid(x):
    return 1.0 / (1.0 + jnp.exp(-x))


def _mod_kernel(cond_ref, w_ref, b_ref, o_ref):
    c = cond_ref[...]
    s = c * _sigmoid(c)
    o_ref[...] = jnp.dot(s.astype(BF16), w_ref[...].astype(BF16),
                         preferred_element_type=F32) + b_ref[...]


def _modulation(cond, w_mod, b_mod):
    n = w_mod.shape[1]
    bn = 1024
    return pl.pallas_call(
        _mod_kernel,
        out_shape=jax.ShapeDtypeStruct((cond.shape[0], n), F32),
        grid=(n // bn,),
        in_specs=[pl.BlockSpec(cond.shape, lambda j: (0, 0)),
                  pl.BlockSpec((D_MODEL, bn), lambda j: (0, j)),
                  pl.BlockSpec((1, bn), lambda j: (0, j))],
        out_specs=pl.BlockSpec((cond.shape[0], bn), lambda j: (0, j)),
        compiler_params=_params(("arbitrary",)),
        name="modulation",
    )(cond, w_mod, b_mod.reshape(1, n))


def _pos_kernel(o_ref):
    n_freq = D_MODEL // 4
    k = lax.broadcasted_iota(jnp.int32, (GRID_W, n_freq), 1).astype(F32)
    p = lax.broadcasted_iota(jnp.int32, (GRID_W, n_freq), 0).astype(F32)
    freq = jnp.exp(-math.log(POS_BASE) * k / n_freq)
    ang = p * freq
    o_ref[:, 0:n_freq] = jnp.sin(ang)
    o_ref[:, n_freq:2 * n_freq] = jnp.cos(ang)


def _pos_table():
    return pl.pallas_call(
        _pos_kernel,
        out_shape=jax.ShapeDtypeStruct((GRID_W, D_MODEL // 2), F32),
        name="pos_table",
    )()


def _tile_pos(rows_ref, cols_ref):
    reps = TM // GRID_W
    rpart = jnp.concatenate(
        [jnp.broadcast_to(rows_ref[q:q + 1, :], (GRID_W, D_MODEL // 2)) for q in range(reps)], axis=0)
    cpart = jnp.concatenate([cols_ref[...]] * reps, axis=0)
    return rpart, cpart


def _load_x(x_ref, pos_refs):
    x = x_ref[...]
    if pos_refs is None:
        return x
    rpart, cpart = _tile_pos(*pos_refs)
    return jnp.concatenate([x[:, :D_MODEL // 2] + rpart, x[:, D_MODEL // 2:] + cpart], axis=1)


def _premix_kernel(*refs, add_pos):
    if add_pos:
        (x_ref, rows_ref, cols_ref, mod_ref, g_ref, win_ref, sgug_ref, sguw_ref, sgub_ref,
         xr_ref, gg_ref, ys_ref) = refs
        pos_refs = (rows_ref, cols_ref)
    else:
        (x_ref, mod_ref, g_ref, win_ref, sgug_ref, sguw_ref, sgub_ref,
         xr_ref, gg_ref, ys_ref) = refs
        pos_refs = None
    x = _load_x(x_ref, pos_refs)
    shift = mod_ref[0, :, 0:D_MODEL]
    scale = mod_ref[0, :, D_MODEL:2 * D_MODEL]
    hn = _rms(x) * g_ref[...] * (1.0 + scale) + shift
    z = jnp.dot(hn.astype(BF16), win_ref[...], preferred_element_type=F32)
    xr_ref[...] = z[:, 0:D_RNN]
    gg_ref[...] = jax.nn.gelu(z[:, D_RNN:2 * D_RNN])
    u = z[:, 2 * D_RNN:2 * D_RNN + D_SGU]
    v = z[:, 2 * D_RNN + D_SGU:]
    vn = (_rms(v) * sgug_ref[...]).astype(BF16)
    half = D_SGU // 2
    heads_per_half = N_HEADS_SGU // 2
    lane_head = lax.broadcasted_iota(jnp.int32, (CHUNK, half), 1) // HEAD_SGU
    for c in range(TM // CHUNK):
        rows = slice(c * CHUNK, (c + 1) * CHUNK)
        parts = []
        for hf in range(2):
            r = jnp.dot(sguw_ref[hf], vn[rows, hf * half:(hf + 1) * half],
                        preferred_element_type=F32)
            s = jnp.zeros((CHUNK, half), F32)
            for h in range(heads_per_half):
                s = jnp.where(lane_head == h, r[h * CHUNK:(h + 1) * CHUNK], s)
            parts.append(s)
        s = jnp.concatenate(parts, axis=1) + sgub_ref[...]
        ys_ref[rows, :] = (u[rows] * s).astype(BF16)


def _premix(x, mod3, cond_of_tile, g_pre, w_in_b, sgu_g, sgu_w_b, sgu_bias_tile, pos_tab):
    n_tok = x.shape[0]
    n_tiles = n_tok // TM
    add_pos = pos_tab is not None
    tok = lambda i: (i, 0)
    const2 = lambda i: (0, 0)
    in_specs = [pl.BlockSpec((TM, D_MODEL), tok)]
    args = [x]
    if add_pos:
        reps = TM // GRID_W
        tiles_per_seq = GRID_W // reps
        in_specs += [pl.BlockSpec((None, reps, D_MODEL // 2), lambda i: (i % tiles_per_seq, 0, 0)),
                     pl.BlockSpec((GRID_W, D_MODEL // 2), const2)]
        args += [pos_tab.reshape(tiles_per_seq, reps, D_MODEL // 2), pos_tab]
    in_specs += [pl.BlockSpec((1, 1, 6 * D_MODEL), lambda i: (cond_of_tile(i), 0, 0)),
                 pl.BlockSpec((1, D_MODEL), const2),
                 pl.BlockSpec((D_MODEL, 2 * D_RNN + 2 * D_SGU), const2),
                 pl.BlockSpec((1, D_SGU), const2),
                 pl.BlockSpec((2, 4 * CHUNK, CHUNK), lambda i: (0, 0, 0)),
                 pl.BlockSpec((CHUNK, D_SGU), const2)]
    args += [mod3, g_pre, w_in_b, sgu_g, sgu_w_b, sgu_bias_tile]
    return pl.pallas_call(
        functools.partial(_premix_kernel, add_pos=add_pos),
        out_shape=(jax.ShapeDtypeStruct((n_tok, D_RNN), F32),
                   jax.ShapeDtypeStruct((n_tok, D_RNN), F32),
                   jax.ShapeDtypeStruct((n_tok, D_SGU), BF16)),
        grid=(n_tiles,),
        in_specs=in_specs,
        out_specs=(pl.BlockSpec((TM, D_RNN), tok),
                   pl.BlockSpec((TM, D_RNN), tok),
                   pl.BlockSpec((TM, D_SGU), tok)),
        compiler_params=_params(("parallel",)),
        name="premix",
    )(*args)


def _block_scan(a, b, h, reverse):
    n, w = a.shape
    nblk = n // SUBLANES
    a = a.reshape(nblk, SUBLANES, w)
    b = b.reshape(nblk, SUBLANES, w)
    sub = lax.broadcasted_iota(jnp.int32, (1, SUBLANES, w), 1)
    for s in (1, 2, 4):
        if reverse:
            keep = sub < SUBLANES - s
            shift = SUBLANES - s
        else:
            keep = sub >= s
            shift = s
        a_sh = jnp.where(keep, pltpu.roll(a, shift, 1), 1.0)
        b_sh = jnp.where(keep, pltpu.roll(b, shift, 1), 0.0)
        b = a * b_sh + b
        a = a * a_sh
    out = [None] * nblk
    order = range(nblk - 1, -1, -1) if reverse else range(nblk)
    edge = 0 if reverse else SUBLANES - 1
    for j in order:
        hj = b[j] + a[j] * h
        out[j] = hj
        h = hj[edge:edge + 1, :]
    return jnp.concatenate(out, axis=0), h


def _rglru_kernel(xr_ref, gg_ref, h0_ref, convw_ref, convb_ref, wg_ref, bg_ref, lam_ref,
                  y_ref, fs_ref, hf_ref, *, nb, seq_len):
    n_chunks = seq_len // SCAN_CHUNK
    pad = SUBLANES
    win_rows = SCAN_CHUNK + 2 * pad

    def gates(s, c, d):
        t0 = pl.multiple_of(c * SCAN_CHUNK, SCAN_CHUNK)
        before = pl.multiple_of(jnp.maximum(t0 - pad, 0), pad)
        after = pl.multiple_of(jnp.minimum(t0 + SCAN_CHUNK, seq_len - pad), pad)
        head = jnp.where(c > 0, xr_ref[s, pl.ds(before, pad), :], 0.0)
        tail = jnp.where(c < n_chunks - 1, xr_ref[s, pl.ds(after, pad), :], 0.0)
        win = jnp.concatenate([head, xr_ref[s, pl.ds(t0, SCAN_CHUNK), :], tail], axis=0)
        taps = [pltpu.roll(win, 2, 0), pltpu.roll(win, 1, 0), win, pltpu.roll(win, win_rows - 1, 0)]
        xc = convb_ref[...]
        for k in range(4):
            xc = xc + taps[k][pad:pad + SCAN_CHUNK] * convw_ref[k:k + 1, :]
        g = jnp.dot(xc.astype(BF16), wg_ref[d], preferred_element_type=F32) + bg_ref[d]
        r = _sigmoid(g[:, :RNN_HALF])
        i = _sigmoid(g[:, RNN_HALF:])
        neg_lam = -lam_ref[d]
        softplus = jnp.maximum(neg_lam, 0.0) + jnp.log(1.0 + jnp.exp(-jnp.abs(neg_lam)))
        log_a = -RG_C * r * softplus
        a = jnp.exp(log_a)
        b = jnp.sqrt(jnp.tanh(-log_a) * (a * a + 1.0)) * (i * xc)
        return t0, a, b

    def per_seq(s, carry):
        h0 = h0_ref[s]

        def fwd(c, h):
            t0, a, b = gates(s, c, 0)
            hc, h = _block_scan(a, b, h, reverse=False)
            hf_ref[pl.ds(t0, SCAN_CHUNK), :] = hc
            return h

        hf_last = lax.fori_loop(0, n_chunks, fwd, h0[0:1, :])

        def bwd(k, h):
            c = n_chunks - 1 - k
            t0, a, b = gates(s, c, 1)
            hc, h = _block_scan(a, b, h, reverse=True)
            y = (hf_ref[pl.ds(t0, SCAN_CHUNK), :] + hc) * gg_ref[s, pl.ds(t0, SCAN_CHUNK), :]
            y_ref[s, pl.ds(t0, SCAN_CHUNK), :] = y.astype(BF16)
            return h

        hb_first = lax.fori_loop(0, n_chunks, bwd, h0[1:2, :])
        fs_ref[s] = jnp.concatenate([hf_last, hb_first], axis=0)
        return carry

    lax.fori_loop(0, nb, per_seq, 0)


def _rglru(xr, gg, h0, conv_w, conv_b, w_gates, b_gates, lam, nb):
    n_seq, seq_len, _ = xr.shape
    blk = lambda i, j: (i, 0, j)
    chan = lambda i, j: (0, j)
    return pl.pallas_call(
        functools.partial(_rglru_kernel, nb=nb, seq_len=seq_len),
        out_shape=(jax.ShapeDtypeStruct((n_seq, seq_len, D_RNN), BF16),
                   jax.ShapeDtypeStruct((n_seq, 2, D_RNN), F32)),
        grid=(n_seq // nb, D_RNN // RNN_HALF),
        in_specs=[pl.BlockSpec((nb, seq_len, RNN_HALF), blk),
                  pl.BlockSpec((nb, seq_len, RNN_HALF), blk),
                  pl.BlockSpec((nb, 2, RNN_HALF), blk),
                  pl.BlockSpec((4, RNN_HALF), chan),
                  pl.BlockSpec((1, RNN_HALF), chan),
                  pl.BlockSpec((None, 2, RNN_HALF, 2 * RNN_HALF), lambda i, j: (j, 0, 0, 0)),
                  pl.BlockSpec((None, 2, 1, 2 * RNN_HALF), lambda i, j: (j, 0, 0, 0)),
                  pl.BlockSpec((2, 1, RNN_HALF), lambda i, j: (0, 0, j))],
        out_specs=(pl.BlockSpec((nb, seq_len, RNN_HALF), blk),
                   pl.BlockSpec((nb, 2, RNN_HALF), blk)),
        scratch_shapes=[pltpu.VMEM((seq_len, RNN_HALF), F32)],
        compiler_params=_params(("parallel", "parallel")),
        name="rglru",
    )(xr, gg, h0, conv_w, conv_b, w_gates, b_gates, lam)


def _route(logits):
    lane = lax.broadcasted_iota(jnp.int32, logits.shape, 1)
    neg = jnp.float32(-jnp.inf)
    gmask = lane < N_GROUPS
    gl = jnp.where(gmask, logits, neg)
    gmax = jnp.max(gl, axis=-1, keepdims=True)
    gsum = jnp.sum(jnp.where(gmask, jnp.exp(gl - gmax), 0.0), axis=-1, keepdims=True)
    g_w = 1.0 / gsum
    g_idx = jnp.min(jnp.where(gl == gmax, lane, ROUTER_LANES), axis=-1, keepdims=True)
    e0 = E_LANE0 + EXPERTS_PER_GROUP * g_idx
    emask = (lane >= e0) & (lane < e0 + EXPERTS_PER_GROUP)
    el = jnp.where(emask, logits, neg)
    emax = jnp.max(el, axis=-1, keepdims=True)
    eexp = jnp.where(emask, jnp.exp(el - emax), 0.0)
    prob = eexp / jnp.sum(eexp, axis=-1, keepdims=True)
    p1 = jnp.max(jnp.where(emask, prob, -1.0), axis=-1, keepdims=True)
    i1 = jnp.min(jnp.where(emask & (prob == p1), lane, ROUTER_LANES), axis=-1, keepdims=True)
    mask2 = emask & (lane != i1)
    p2 = jnp.max(jnp.where(mask2, prob, -1.0), axis=-1, keepdims=True)
    i2 = jnp.min(jnp.where(mask2 & (prob == p2), lane, ROUTER_LANES), axis=-1, keepdims=True)
    tot = p1 + p2
    return (jnp.where(lane == i1, p1 / tot * g_w, 0.0)
            + jnp.where(lane == i2, p2 / tot * g_w, 0.0))


def _postmix_kernel(*refs, add_pos):
    if add_pos:
        (x_ref, rows_ref, cols_ref, yr_ref, ys_ref, mod_ref, gpost_ref, gpre_ref, wout_ref,
         rw_ref, rb_ref, x1_ref, hn_ref, cw_ref) = refs
        pos_refs = (rows_ref, cols_ref)
    else:
        (x_ref, yr_ref, ys_ref, mod_ref, gpost_ref, gpre_ref, wout_ref,
         rw_ref, rb_ref, x1_ref, hn_ref, cw_ref) = refs
        pos_refs = None
    x = _load_x(x_ref, pos_refs)
    gate1 = mod_ref[0, :, 2 * D_MODEL:3 * D_MODEL]
    shift2 = mod_ref[0, :, 3 * D_MODEL:4 * D_MODEL]
    scale2 = mod_ref[0, :, 4 * D_MODEL:5 * D_MODEL]
    y = (jnp.dot(yr_ref[...], wout_ref[0:D_RNN, :], preferred_element_type=F32)
         + jnp.dot(ys_ref[...], wout_ref[D_RNN:, :], preferred_element_type=F32))
    x1 = x + gate1 * (_rms(y) * gpost_ref[...])
    x1_ref[...] = x1
    hn = (_rms(x1) * gpre_ref[...] * (1.0 + scale2) + shift2).astype(BF16)
    hn_ref[...] = hn
    logits = jnp.dot(hn, rw_ref[...], preferred_element_type=F32) + rb_ref[...]
    cw_ref[...] = _route(logits)


def _postmix(x, y_rnn, y_sgu, mod3, cond_of_tile, g_post, g_pre, w_out_b, router_w, router_b, pos_tab):
    n_tok = x.shape[0]
    add_pos = pos_tab is not None
    tok = lambda i: (i, 0)
    const2 = lambda i: (0, 0)
    in_specs = [pl.BlockSpec((TM, D_MODEL), tok)]
    args = [x]
    if add_pos:
        reps = TM // GRID_W
        tiles_per_seq = GRID_W // reps
        in_specs += [pl.BlockSpec((None, reps, D_MODEL // 2), lambda i: (i % tiles_per_seq, 0, 0)),
                     pl.BlockSpec((GRID_W, D_MODEL // 2), const2)]
        args += [pos_tab.reshape(tiles_per_seq, reps, D_MODEL // 2), pos_tab]
    in_specs += [pl.BlockSpec((TM, D_RNN), tok),
                 pl.BlockSpec((TM, D_SGU), tok),
                 pl.BlockSpec((1, 1, 6 * D_MODEL), lambda i: (cond_of_tile(i), 0, 0)),
                 pl.BlockSpec((1, D_MODEL), const2),
                 pl.BlockSpec((1, D_MODEL), const2),
                 pl.BlockSpec((D_MODEL, D_MODEL), const2),
                 pl.BlockSpec((D_MODEL, ROUTER_LANES), const2),
                 pl.BlockSpec((1, ROUTER_LANES), const2)]
    args += [y_rnn, y_sgu, mod3, g_post, g_pre, w_out_b, router_w, router_b]
    return pl.pallas_call(
        functools.partial(_postmix_kernel, add_pos=add_pos),
        out_shape=(jax.ShapeDtypeStruct((n_tok, D_MODEL), F32),
                   jax.ShapeDtypeStruct((n_tok, D_MODEL), BF16),
                   jax.ShapeDtypeStruct((n_tok, ROUTER_LANES), F32)),
        grid=(n_tok // TM,),
        in_specs=in_specs,
        out_specs=(pl.BlockSpec((TM, D_MODEL), tok),
                   pl.BlockSpec((TM, D_MODEL), tok),
                   pl.BlockSpec((TM, ROUTER_LANES), tok)),
        compiler_params=_params(("parallel",)),
        name="postmix",
    )(*args)


def _moe_kernel(hn_ref, cw_ref, wg_ref, wu_ref, wd_ref, x1_ref, mod_ref, gpost_ref, o_ref, acc_ref):
    e = pl.program_id(1)

    @pl.when(e == 0)
    def _():
        acc_ref[...] = jnp.zeros_like(acc_ref)

    xb = hn_ref[...]
    g = jnp.dot(xb, wg_ref[0], preferred_element_type=F32)
    u = jnp.dot(xb, wu_ref[0], preferred_element_type=F32)
    lane = lax.broadcasted_iota(jnp.int32, cw_ref.shape, 1)
    cwe = jnp.sum(jnp.where(lane == e + E_LANE0, cw_ref[...], 0.0), axis=-1, keepdims=True)
    act = (g * _sigmoid(g)) * u * cwe
    acc_ref[...] += jnp.dot(act.astype(BF16), wd_ref[0], preferred_element_type=F32)

    @pl.when(e == N_EXPERTS - 1)
    def _():
        gate2 = mod_ref[0, :, 5 * D_MODEL:6 * D_MODEL]
        o_ref[...] = x1_ref[...] + gate2 * (_rms(acc_ref[...]) * gpost_ref[...])


def _moe(hn, cw, wg_b, wu_b, wd_b, x1, mod3, cond_of_tile, g_post):
    n_tok = hn.shape[0]
    tok = lambda i, e: (i, 0)
    exp = lambda i, e: (e, 0, 0)
    return pl.pallas_call(
        _moe_kernel,
        out_shape=jax.ShapeDtypeStruct((n_tok, D_MODEL), F32),
        grid=(n_tok // TM_MOE, N_EXPERTS),
        in_specs=[pl.BlockSpec((TM_MOE, D_MODEL), tok),
                  pl.BlockSpec((TM_MOE, ROUTER_LANES), tok),
                  pl.BlockSpec((1, D_MODEL, D_EXPERT), exp),
                  pl.BlockSpec((1, D_MODEL, D_EXPERT), exp),
                  pl.BlockSpec((1, D_EXPERT, D_MODEL), exp),
                  pl.BlockSpec((TM_MOE, D_MODEL), tok),
                  pl.BlockSpec((1, 1, 6 * D_MODEL), lambda i, e: (cond_of_tile(i), 0, 0)),
                  pl.BlockSpec((1, D_MODEL), lambda i, e: (0, 0))],
        out_specs=pl.BlockSpec((TM_MOE, D_MODEL), tok),
        scratch_shapes=[pltpu.VMEM((TM_MOE, D_MODEL), F32)],
        compiler_params=_params(("parallel", "arbitrary")),
        name="experts",
    )(hn, cw, wg_b, wu_b, wd_b, x1, mod3, g_post)


def _block_diag_gates(rg_wa, rg_wx):
    heads_per_half = N_HEADS_RNN // 2

    def bd(w):
        w = w.reshape(2, 2, heads_per_half, HEAD_RNN, HEAD_RNN)
        eye = jnp.eye(heads_per_half, dtype=w.dtype)
        full = jnp.einsum('dghij,hk->dghikj', w, eye)
        return full.reshape(2, 2, RNN_HALF, RNN_HALF)

    w = jnp.concatenate([bd(rg_wa), bd(rg_wx)], axis=-1)
    return jnp.transpose(w, (1, 0, 2, 3)).astype(BF16)


def kernel(x_prompt, x_sample, state_rglru, c, c_ctx, w_mod, b_mod, g_pre_mix, g_post_mix, g_pre_ffn,
           g_post_ffn, w_in, conv_w, conv_b, rg_wa, rg_ba, rg_wx, rg_bx, rg_lambda, sgu_g, sgu_w, sgu_b,
           w_out, router_g_w, router_g_b, router_e_w, router_e_b, exp_w_gate, exp_w_up, exp_w_down):
    assert w_mod.shape[0] == 1, "single-layer trunk"
    n_ctx, ctx_len, _ = x_prompt.shape
    n_dec, dec_len, _ = x_sample.shape
    l = 0

    n_cond = SUBLANES
    cond = jnp.zeros((n_cond, D_MODEL), F32).at[0].set(c_ctx).at[1:1 + n_dec].set(c)
    mod3 = _modulation(cond, w_mod[l], b_mod[l]).reshape(n_cond, 1, 6 * D_MODEL)
    pos_tab = _pos_table()

    w_in_b = w_in[l].astype(BF16)
    w_out_b = w_out[l].astype(BF16)
    sgu_w_b = sgu_w[l].reshape(2, 4 * CHUNK, CHUNK).astype(BF16)
    sgu_bias_tile = jnp.repeat(sgu_b[l].T, HEAD_SGU, axis=1)
    w_gates = _block_diag_gates(rg_wa[l], rg_wx[l])
    b_gates = jnp.concatenate([rg_ba[l].reshape(2, 2, 1, RNN_HALF), rg_bx[l].reshape(2, 2, 1, RNN_HALF)],
                              axis=-1)
    b_gates = jnp.transpose(b_gates, (1, 0, 2, 3))
    lam = rg_lambda[l].reshape(2, 1, D_RNN)
    router_w = jnp.zeros((D_MODEL, ROUTER_LANES), F32)
    router_w = router_w.at[:, :N_GROUPS].set(router_g_w[l]).at[:, E_LANE0:E_LANE0 + N_EXPERTS].set(router_e_w[l])
    router_w = router_w.astype(BF16)
    router_b = jnp.zeros((1, ROUTER_LANES), F32)
    router_b = router_b.at[0, :N_GROUPS].set(router_g_b[l]).at[0, E_LANE0:E_LANE0 + N_EXPERTS].set(router_e_b[l])
    wg_b = exp_w_gate[l].astype(BF16)
    wu_b = exp_w_up[l].astype(BF16)
    wd_b = exp_w_down[l].astype(BF16)
    row = lambda v: v.reshape(1, -1)

    def trunk(x, h0, cond_of_tile, cond_of_moe_tile, seqs_per_step, use_pos):
        n_seq, seq_len, _ = x.shape
        xf = x.reshape(n_seq * seq_len, D_MODEL)
        tab = pos_tab if use_pos else None
        xr, gg, y_sgu = _premix(xf, mod3, cond_of_tile, row(g_pre_mix[l]), w_in_b, row(sgu_g[l]),
                                sgu_w_b, sgu_bias_tile, tab)
        y_rnn, fstate = _rglru(xr.reshape(n_seq, seq_len, D_RNN), gg.reshape(n_seq, seq_len, D_RNN), h0,
                               conv_w[l], row(conv_b[l]), w_gates, b_gates, lam, seqs_per_step)
        x1, hn, cw = _postmix(xf, y_rnn.reshape(n_seq * seq_len, D_RNN), y_sgu, mod3, cond_of_tile,
                              row(g_post_mix[l]), row(g_pre_ffn[l]), w_out_b, router_w, router_b, tab)
        out = _moe(hn, cw, wg_b, wu_b, wd_b, x1, mod3, cond_of_moe_tile, row(g_post_ffn[l]))
        return out.reshape(n_seq, seq_len, D_MODEL), fstate

    h0_ctx = jnp.zeros((n_ctx, 2, D_RNN), F32)
    y_prompt, st = trunk(x_prompt, h0_ctx, lambda i: 0, lambda i: 0, 8, False)
    new_state = st.astype(state_rglru.dtype)[:, None]

    tiles_per_seq = dec_len // TM
    moe_tiles_per_seq = dec_len // TM_MOE
    y_sample, _ = trunk(x_sample, state_rglru[:, l].astype(F32),
                        lambda i: 1 + i // tiles_per_seq, lambda i: 1 + i // moe_tiles_per_seq, 1, True)
    return (y_prompt, y_sample, new_state)
```

```python
import functools
import math

import jax
import jax.numpy as jnp
from jax import lax
from jax.experimental import pallas as pl
from jax.experimental.pallas import tpu as pltpu

D_MODEL = 1024
D_RNN = 512
D_SGU = 512
N_HEADS_RNN = 8
HEAD_RNN = D_RNN // N_HEADS_RNN
N_HEADS_SGU = 8
HEAD_SGU = D_SGU // N_HEADS_SGU
CHUNK = 128
GRID_W = 64
RG_C = 8.0
N_GROUPS = 4
EXPERTS_PER_GROUP = 4
N_EXPERTS = N_GROUPS * EXPERTS_PER_GROUP
D_EXPERT = 512
EPS = 1e-6
POS_BASE = 10000.0

LANES = 128
SUBLANES = 8
RNN_HALF = D_RNN // 2
ROUTER_LANES = LANES
E_LANE0 = N_GROUPS

PAIRS_PER_GROUP = EXPERTS_PER_GROUP * (EXPERTS_PER_GROUP - 1) // 2
N_BUCKETS = N_GROUPS * PAIRS_PER_GROUP

TM = 256
TMX = 512
SCAN_CHUNK = 128
VMEM_LIMIT = 56 * 1024 * 1024

F32 = jnp.float32
BF16 = jnp.bfloat16


def _params(sem):
    return pltpu.CompilerParams(dimension_semantics=sem, vmem_limit_bytes=VMEM_LIMIT)


def _rms(x):
    return x * lax.rsqrt(jnp.mean(x * x, axis=-1, keepdims=True) + EPS)


def _sigmoid(x):
    return 1.0 / (1.0 + jnp.exp(-x))


def _mod_kernel(cond_ref, w_ref, b_ref, o_ref):
    c = cond_ref[...]
    s = c * _sigmoid(c)
    o_ref[...] = jnp.dot(s.astype(BF16), w_ref[...].astype(BF16),
                         preferred_element_type=F32) + b_ref[...]


def _modulation(cond, w_mod, b_mod):
    n = w_mod.shape[1]
    bn = 1024
    return pl.pallas_call(
        _mod_kernel,
        out_shape=jax.ShapeDtypeStruct((cond.shape[0], n), F32),
        grid=(n // bn,),
        in_specs=[pl.BlockSpec(cond.shape, lambda j: (0, 0)),
                  pl.BlockSpec((D_MODEL, bn), lambda j: (0, j)),
                  pl.BlockSpec((1, bn), lambda j: (0, j))],
        out_specs=pl.BlockSpec((cond.shape[0], bn), lambda j: (0, j)),
        compiler_params=_params(("arbitrary",)),
        name="modulation",
    )(cond, w_mod, b_mod.reshape(1, n))


def _pos_kernel(o_ref):
    n_freq = D_MODEL // 4
    k = lax.broadcasted_iota(jnp.int32, (GRID_W, n_freq), 1).astype(F32)
    p = lax.broadcasted_iota(jnp.int32, (GRID_W, n_freq), 0).astype(F32)
    freq = jnp.exp(-math.log(POS_BASE) * k / n_freq)
    ang = p * freq
    o_ref[:, 0:n_freq] = jnp.sin(ang)
    o_ref[:, n_freq:2 * n_freq] = jnp.cos(ang)


def _pos_table():
    return pl.pallas_call(
        _pos_kernel,
        out_shape=jax.ShapeDtypeStruct((GRID_W, D_MODEL // 2), F32),
        name="pos_table",
    )()


def _tile_pos(rows_ref, cols_ref):
    reps = TM // GRID_W
    rpart = jnp.concatenate(
        [jnp.broadcast_to(rows_ref[q:q + 1, :], (GRID_W, D_MODEL // 2)) for q in range(reps)], axis=0)
    cpart = jnp.concatenate([cols_ref[...]] * reps, axis=0)
    return rpart, cpart


def _load_x(x_ref, pos_refs):
    x = x_ref[...]
    if pos_refs is None:
        return x
    rpart, cpart = _tile_pos(*pos_refs)
    return jnp.concatenate([x[:, :D_MODEL // 2] + rpart, x[:, D_MODEL // 2:] + cpart], axis=1)


def _premix_kernel(*refs, add_pos):
    if add_pos:
        (x_ref, rows_ref, cols_ref, mod_ref, g_ref, win_ref, sgug_ref, sguw_ref, sgub_ref,
         xr_ref, gg_ref, ys_ref) = refs
        pos_refs = (rows_ref, cols_ref)
    else:
        (x_ref, mod_ref, g_ref, win_ref, sgug_ref, sguw_ref, sgub_ref,
         xr_ref, gg_ref, ys_ref) = refs
        pos_refs = None
    x = _load_x(x_ref, pos_refs)
    shift = mod_ref[0, :, 0:D_MODEL]
    scale = mod_ref[0, :, D_MODEL:2 * D_MODEL]
    hn = _rms(x) * g_ref[...] * (1.0 + scale) + shift
    z = jnp.dot(hn.astype(BF16), win_ref[...], preferred_element_type=F32)
    xr_ref[...] = z[:, 0:D_RNN]
    gg_ref[...] = jax.nn.gelu(z[:, D_RNN:2 * D_RNN])
    u = z[:, 2 * D_RNN:2 * D_RNN + D_SGU]
    v = z[:, 2 * D_RNN + D_SGU:]
    vn = (_rms(v) * sgug_ref[...]).astype(BF16)
    half = D_SGU // 2
    heads_per_half = N_HEADS_SGU // 2
    lane_head = lax.broadcasted_iota(jnp.int32, (CHUNK, half), 1) // HEAD_SGU
    for c in range(TM // CHUNK):
        rows = slice(c * CHUNK, (c + 1) * CHUNK)
        parts = []
        for hf in range(2):
            r = jnp.dot(sguw_ref[hf], vn[rows, hf * half:(hf + 1) * half],
                        preferred_element_type=F32)
            s = jnp.zeros((CHUNK, half), F32)
            for h in range(heads_per_half):
                s = jnp.where(lane_head == h, r[h * CHUNK:(h + 1) * CHUNK], s)
            parts.append(s)
        s = jnp.concatenate(parts, axis=1) + sgub_ref[...]
        ys_ref[rows, :] = (u[rows] * s).astype(BF16)


def _premix(x, mod3, cond_of_tile, g_pre, w_in_b, sgu_g, sgu_w_b, sgu_bias_tile, pos_tab):
    n_tok = x.shape[0]
    n_tiles = n_tok // TM
    add_pos = pos_tab is not None
    tok = lambda i: (i, 0)
    const2 = lambda i: (0, 0)
    in_specs = [pl.BlockSpec((TM, D_MODEL), tok)]
    args = [x]
    if add_pos:
        reps = TM // GRID_W
        tiles_per_seq = GRID_W // reps
        in_specs += [pl.BlockSpec((None, reps, D_MODEL // 2), lambda i: (i % tiles_per_seq, 0, 0)),
                     pl.BlockSpec((GRID_W, D_MODEL // 2), const2)]
        args += [pos_tab.reshape(tiles_per_seq, reps, D_MODEL // 2), pos_tab]
    in_specs += [pl.BlockSpec((1, 1, 6 * D_MODEL), lambda i: (cond_of_tile(i), 0, 0)),
                 pl.BlockSpec((1, D_MODEL), const2),
                 pl.BlockSpec((D_MODEL, 2 * D_RNN + 2 * D_SGU), const2),
                 pl.BlockSpec((1, D_SGU), const2),
                 pl.BlockSpec((2, 4 * CHUNK, CHUNK), lambda i: (0, 0, 0)),
                 pl.BlockSpec((CHUNK, D_SGU), const2)]
    args += [mod3, g_pre, w_in_b, sgu_g, sgu_w_b, sgu_bias_tile]
    return pl.pallas_call(
        functools.partial(_premix_kernel, add_pos=add_pos),
        out_shape=(jax.ShapeDtypeStruct((n_tok, D_RNN), F32),
                   jax.ShapeDtypeStruct((n_tok, D_RNN), F32),
                   jax.ShapeDtypeStruct((n_tok, D_SGU), BF16)),
        grid=(n_tiles,),
        in_specs=in_specs,
        out_specs=(pl.BlockSpec((TM, D_RNN), tok),
                   pl.BlockSpec((TM, D_RNN), tok),
                   pl.BlockSpec((TM, D_SGU), tok)),
        compiler_params=_params(("parallel",)),
        name="premix",
    )(*args)


def _block_scan(a, b, h, reverse):
    n, w = a.shape
    nblk = n // SUBLANES
    a = a.reshape(nblk, SUBLANES, w)
    b = b.reshape(nblk, SUBLANES, w)
    sub = lax.broadcasted_iota(jnp.int32, (1, SUBLANES, w), 1)
    for s in (1, 2, 4):
        if reverse:
            keep = sub < SUBLANES - s
            shift = SUBLANES - s
        else:
            keep = sub >= s
            shift = s
        a_sh = jnp.where(keep, pltpu.roll(a, shift, 1), 1.0)
        b_sh = jnp.where(keep, pltpu.roll(b, shift, 1), 0.0)
        b = a * b_sh + b
        a = a * a_sh
    out = [None] * nblk
    order = range(nblk - 1, -1, -1) if reverse else range(nblk)
    edge = 0 if reverse else SUBLANES - 1
    for j in order:
        hj = b[j] + a[j] * h
        out[j] = hj
        h = hj[edge:edge + 1, :]
    return jnp.concatenate(out, axis=0), h


def _rglru_kernel(xr_ref, gg_ref, h0_ref, convw_ref, convb_ref, wg_ref, bg_ref, lam_ref,
                  y_ref, fs_ref, hf_ref, *, nb, seq_len):
    n_chunks = seq_len // SCAN_CHUNK
    pad = SUBLANES
    win_rows = SCAN_CHUNK + 2 * pad

    def gates(s, c, d):
        t0 = pl.multiple_of(c * SCAN_CHUNK, SCAN_CHUNK)
        before = pl.multiple_of(jnp.maximum(t0 - pad, 0), pad)
        after = pl.multiple_of(jnp.minimum(t0 + SCAN_CHUNK, seq_len - pad), pad)
        head = jnp.where(c > 0, xr_ref[s, pl.ds(before, pad), :], 0.0)
        tail = jnp.where(c < n_chunks - 1, xr_ref[s, pl.ds(after, pad), :], 0.0)
        win = jnp.concatenate([head, xr_ref[s, pl.ds(t0, SCAN_CHUNK), :], tail], axis=0)
        taps = [pltpu.roll(win, 2, 0), pltpu.roll(win, 1, 0), win, pltpu.roll(win, win_rows - 1, 0)]
        xc = convb_ref[...]
        for k in range(4):
            xc = xc + taps[k][pad:pad + SCAN_CHUNK] * convw_ref[k:k + 1, :]
        g = jnp.dot(xc.astype(BF16), wg_ref[d], preferred_element_type=F32) + bg_ref[d]
        r = _sigmoid(g[:, :RNN_HALF])
        i = _sigmoid(g[:, RNN_HALF:])
        neg_lam = -lam_ref[d]
        softplus = jnp.maximum(neg_lam, 0.0) + jnp.log(1.0 + jnp.exp(-jnp.abs(neg_lam)))
        log_a = -RG_C * r * softplus
        a = jnp.exp(log_a)
        b = jnp.sqrt(jnp.tanh(-log_a) * (a * a + 1.0)) * (i * xc)
        return t0, a, b

    def per_seq(s, carry):
        h0 = h0_ref[s]

        def fwd(c, h):
            t0, a, b = gates(s, c, 0)
            hc, h = _block_scan(a, b, h, reverse=False)
            hf_ref[pl.ds(t0, SCAN_CHUNK), :] = hc
            return h

        hf_last = lax.fori_loop(0, n_chunks, fwd, h0[0:1, :])

        def bwd(k, h):
            c = n_chunks - 1 - k
            t0, a, b = gates(s, c, 1)
            hc, h = _block_scan(a, b, h, reverse=True)
            y = (hf_ref[pl.ds(t0, SCAN_CHUNK), :] + hc) * gg_ref[s, pl.ds(t0, SCAN_CHUNK), :]
            y_ref[s, pl.ds(t0, SCAN_CHUNK), :] = y.astype(BF16)
            return h

        hb_first = lax.fori_loop(0, n_chunks, bwd, h0[1:2, :])
        fs_ref[s] = jnp.concatenate([hf_last, hb_first], axis=0)
        return carry

    lax.fori_loop(0, nb, per_seq, 0)


def _rglru(xr, gg, h0, conv_w, conv_b, w_gates, b_gates, lam, nb):
    n_seq, seq_len, _ = xr.shape
    blk = lambda i, j: (i, 0, j)
    chan = lambda i, j: (0, j)
    return pl.pallas_call(
        functools.partial(_rglru_kernel, nb=nb, seq_len=seq_len),
        out_shape=(jax.ShapeDtypeStruct((n_seq, seq_len, D_RNN), BF16),
                   jax.ShapeDtypeStruct((n_seq, 2, D_RNN), F32)),
        grid=(n_seq // nb, D_RNN // RNN_HALF),
        in_specs=[pl.BlockSpec((nb, seq_len, RNN_HALF), blk),
                  pl.BlockSpec((nb, seq_len, RNN_HALF), blk),
                  pl.BlockSpec((nb, 2, RNN_HALF), blk),
                  pl.BlockSpec((4, RNN_HALF), chan),
                  pl.BlockSpec((1, RNN_HALF), chan),
                  pl.BlockSpec((None, 2, RNN_HALF, 2 * RNN_HALF), lambda i, j: (j, 0, 0, 0)),
                  pl.BlockSpec((None, 2, 1, 2 * RNN_HALF), lambda i, j: (j, 0, 0, 0)),
                  pl.BlockSpec((2, 1, RNN_HALF), lambda i, j: (0, 0, j))],
        out_specs=(pl.BlockSpec((nb, seq_len, RNN_HALF), blk),
                   pl.BlockSpec((nb, 2, RNN_HALF), blk)),
        scratch_shapes=[pltpu.VMEM((seq_len, RNN_HALF), F32)],
        compiler_params=_params(("parallel", "parallel")),
        name="rglru",
    )(xr, gg, h0, conv_w, conv_b, w_gates, b_gates, lam)


def _route(logits):
    lane = lax.broadcasted_iota(jnp.int32, logits.shape, 1)
    neg = jnp.float32(-jnp.inf)
    gmask = lane < N_GROUPS
    gl = jnp.where(gmask, logits, neg)
    gmax = jnp.max(gl, axis=-1, keepdims=True)
    g_idx = jnp.min(jnp.where(gl == gmax, lane, ROUTER_LANES), axis=-1, keepdims=True)
    e0 = E_LANE0 + EXPERTS_PER_GROUP * g_idx
    emask = (lane >= e0) & (lane < e0 + EXPERTS_PER_GROUP)
    el = jnp.where(emask, logits, neg)
    l1 = jnp.max(el, axis=-1, keepdims=True)
    i1 = jnp.min(jnp.where(el == l1, lane, ROUTER_LANES), axis=-1, keepdims=True)
    el2 = jnp.where(lane == i1, neg, el)
    l2 = jnp.max(el2, axis=-1, keepdims=True)
    i2 = jnp.min(jnp.where(emask & (lane != i1) & (el2 == l2), lane, ROUTER_LANES), axis=-1, keepdims=True)
    ja = jnp.minimum(i1, i2) - e0
    jb = jnp.maximum(i1, i2) - e0
    pair = (ja * (2 * EXPERTS_PER_GROUP - 1 - ja)) // 2 + (jb - ja - 1)
    return g_idx * PAIRS_PER_GROUP + pair


def _postmix_kernel(*refs, add_pos, n_aliased):
    refs = list(refs)
    x_ref = refs.pop(0)
    pos_refs = (refs.pop(0), refs.pop(0)) if add_pos else None
    (yr_ref, ys_ref, mod_ref, gpost_ref, gpre_ref, wout_ref, rw_ref, rb_ref, cnt0_ref) = refs[:9]
    x1_ref, hn_ref, rt_ref, cnt_ref, run_ref = refs[9 + n_aliased:]

    @pl.when(pl.program_id(0) == 0)
    def _():
        run_ref[...] = cnt0_ref[...]

    x = _load_x(x_ref, pos_refs)
    gate1 = mod_ref[0, :, 2 * D_MODEL:3 * D_MODEL]
    shift2 = mod_ref[0, :, 3 * D_MODEL:4 * D_MODEL]
    scale2 = mod_ref[0, :, 4 * D_MODEL:5 * D_MODEL]
    y = (jnp.dot(yr_ref[...], wout_ref[0:D_RNN, :], preferred_element_type=F32)
         + jnp.dot(ys_ref[...], wout_ref[D_RNN:, :], preferred_element_type=F32))
    x1 = x + gate1 * (_rms(y) * gpost_ref[...])
    x1_ref[...] = x1
    hn = _rms(x1) * gpre_ref[...] * (1.0 + scale2) + shift2
    hn_ref[...] = hn
    logits = jnp.dot(hn.astype(BF16), rw_ref[...], preferred_element_type=F32) + rb_ref[...]
    bucket = _route(logits)
    lane = lax.broadcasted_iota(jnp.int32, (TM, ROUTER_LANES), 1)
    onehot = lane == bucket
    r_i = lax.broadcasted_iota(jnp.int32, (TM, TM), 0)
    c_i = lax.broadcasted_iota(jnp.int32, (TM, TM), 1)
    earlier = (c_i < r_i).astype(BF16)
    before = jnp.dot(earlier, onehot.astype(BF16), preferred_element_type=F32) + run_ref[...]
    rank = jnp.sum(jnp.where(onehot, before, 0.0), axis=-1, keepdims=True).astype(jnp.int32)
    rt_ref[...] = jnp.where(lane == 0, bucket, jnp.where(lane == 1, rank, 0))
    run_ref[...] += jnp.sum(onehot.astype(F32), axis=0, keepdims=True)
    cnt_ref[...] = run_ref[...]


def _postmix(x, y_rnn, y_sgu, mod3, cond_of_tile, g_post, g_pre, w_out_b, router_w, router_b, pos_tab,
             counts0, tile0, n_total, chained):
    n_tok = x.shape[0]
    add_pos = pos_tab is not None
    tok = lambda i: (i, 0)
    tok_all = lambda i: (i + tile0, 0)
    const2 = lambda i: (0, 0)
    in_specs = [pl.BlockSpec((TM, D_MODEL), tok)]
    args = [x]
    if add_pos:
        reps = TM // GRID_W
        tiles_per_seq = GRID_W // reps
        in_specs += [pl.BlockSpec((None, reps, D_MODEL // 2), lambda i: (i % tiles_per_seq, 0, 0)),
                     pl.BlockSpec((GRID_W, D_MODEL // 2), const2)]
        args += [pos_tab.reshape(tiles_per_seq, reps, D_MODEL // 2), pos_tab]
    in_specs += [pl.BlockSpec((TM, D_RNN), tok),
                 pl.BlockSpec((TM, D_SGU), tok),
                 pl.BlockSpec((1, 1, 6 * D_MODEL), lambda i: (cond_of_tile(i), 0, 0)),
                 pl.BlockSpec((1, D_MODEL), const2),
                 pl.BlockSpec((1, D_MODEL), const2),
                 pl.BlockSpec((D_MODEL, D_MODEL), const2),
                 pl.BlockSpec((D_MODEL, ROUTER_LANES), const2),
                 pl.BlockSpec((1, ROUTER_LANES), const2),
                 pl.BlockSpec((1, ROUTER_LANES), const2)]
    args += [y_rnn, y_sgu, mod3, g_post, g_pre, w_out_b, router_w, router_b, counts0]
    aliases = {}
    if chained is not None:
        first = len(args)
        in_specs += [pl.BlockSpec(memory_space=pl.ANY)] * len(chained)
        args += list(chained)
        aliases = {first + k: k for k in range(len(chained))}
    return pl.pallas_call(
        functools.partial(_postmix_kernel, add_pos=add_pos, n_aliased=len(aliases)),
        out_shape=(jax.ShapeDtypeStruct((n_total, D_MODEL), F32),
                   jax.ShapeDtypeStruct((n_total, D_MODEL), F32),
                   jax.ShapeDtypeStruct((n_total, ROUTER_LANES), jnp.int32),
                   jax.ShapeDtypeStruct((1, ROUTER_LANES), F32)),
        grid=(n_tok // TM,),
        in_specs=in_specs,
        out_specs=(pl.BlockSpec((TM, D_MODEL), tok_all),
                   pl.BlockSpec((TM, D_MODEL), tok_all),
                   pl.BlockSpec((TM, ROUTER_LANES), tok_all),
                   pl.BlockSpec((1, ROUTER_LANES), const2)),
        scratch_shapes=[pltpu.VMEM((1, ROUTER_LANES), F32)],
        input_output_aliases=aliases,
        compiler_params=_params(("arbitrary",)),
        name="postmix",
    )(*args)


def _row_copy(src_ref, src_row, dst_ref, dst_row, sem):
    return pltpu.make_async_copy(src_ref.at[pl.ds(src_row, 1), :], dst_ref.at[pl.ds(dst_row, 1), :], sem)


def _dispatch_kernel(dest_ref, hn_ref, xs_ref, sems):
    base = pl.program_id(0) * TM

    def start(r, carry):
        _row_copy(hn_ref, base + r, xs_ref, dest_ref[base + r], sems.at[r]).start()
        return carry

    def wait(r, carry):
        _row_copy(hn_ref, base + r, xs_ref, dest_ref[base + r], sems.at[r]).wait()
        return carry

    lax.fori_loop(0, TM, start, 0)
    lax.fori_loop(0, TM, wait, 0)


def _dispatch(dest, hn_all, n_slots):
    n_tok = hn_all.shape[0]
    return pl.pallas_call(
        _dispatch_kernel,
        out_shape=jax.ShapeDtypeStruct((n_slots, D_MODEL), F32),
        grid_spec=pltpu.PrefetchScalarGridSpec(
            num_scalar_prefetch=1,
            grid=(n_tok // TM,),
            in_specs=[pl.BlockSpec(memory_space=pl.ANY)],
            out_specs=pl.BlockSpec(memory_space=pl.ANY),
            scratch_shapes=[pltpu.SemaphoreType.DMA((TM,))]),
        compiler_params=_params(("arbitrary",)),
        name="dispatch",
    )(dest, hn_all)


def _experts_kernel(ea_ref, eb_ref, blk_ref, nv_ref, xs_ref, rw_ref, rb_ref,
                    wga_ref, wua_ref, wda_ref, wgb_ref, wub_ref, wdb_ref, gpost_ref, ys_ref):
    i = pl.program_id(0)
    nv = nv_ref[i]

    @pl.when(nv > 0)
    def _():
        row = lax.broadcasted_iota(jnp.int32, (TMX, 1), 0)
        xb = jnp.where(row < nv, xs_ref[...], 0.0).astype(BF16)
        logits = jnp.dot(xb, rw_ref[...], preferred_element_type=F32) + rb_ref[...]
        lane = lax.broadcasted_iota(jnp.int32, logits.shape, 1)
        ea = ea_ref[i]
        eb = eb_ref[i]
        gmask = lane < N_GROUPS
        gl = jnp.where(gmask, logits, -jnp.inf)
        gmax = jnp.max(gl, axis=-1, keepdims=True)
        gexp = jnp.where(gmask, jnp.exp(gl - gmax), 0.0)
        g_own = jnp.sum(jnp.where(lane == ea // EXPERTS_PER_GROUP, gexp, 0.0), axis=-1, keepdims=True)
        g_w = g_own / jnp.sum(gexp, axis=-1, keepdims=True)
        la = jnp.sum(jnp.where(lane == ea + E_LANE0, logits, 0.0), axis=-1, keepdims=True)
        lb = jnp.sum(jnp.where(lane == eb + E_LANE0, logits, 0.0), axis=-1, keepdims=True)
        m = jnp.maximum(la, lb)
        pa = jnp.exp(la - m)
        pb = jnp.exp(lb - m)
        inv = g_w / (pa + pb)

        def expert(wg_ref, wu_ref, wd_ref, w):
            g = jnp.dot(xb, wg_ref[0], preferred_element_type=F32)
            u = jnp.dot(xb, wu_ref[0], preferred_element_type=F32)
            act = (g * _sigmoid(g)) * u * w
            return jnp.dot(act.astype(BF16), wd_ref[0], preferred_element_type=F32)

        y = expert(wga_ref, wua_ref, wda_ref, pa * inv) + expert(wgb_ref, wub_ref, wdb_ref, pb * inv)
        ys_ref[...] = _rms(y) * gpost_ref[...]


def _experts(sched, xs, router_w, router_b, wg_b, wu_b, wd_b, g_post):
    ea, eb, blk, nv = sched
    n_tiles = ea.shape[0]
    rows = lambda i, ea, eb, blk, nv: (blk[i], 0)
    const2 = lambda i, ea, eb, blk, nv: (0, 0)
    exp_a = lambda i, ea, eb, blk, nv: (ea[i], 0, 0)
    exp_b = lambda i, ea, eb, blk, nv: (eb[i], 0, 0)
    w_in_spec = lambda m: pl.BlockSpec((1, D_MODEL, D_EXPERT), m)
    w_out_spec = lambda m: pl.BlockSpec((1, D_EXPERT, D_MODEL), m)
    return pl.pallas_call(
        _experts_kernel,
        out_shape=jax.ShapeDtypeStruct(xs.shape, F32),
        grid_spec=pltpu.PrefetchScalarGridSpec(
            num_scalar_prefetch=4,
            grid=(n_tiles,),
            in_specs=[pl.BlockSpec((TMX, D_MODEL), rows),
                      pl.BlockSpec((D_MODEL, ROUTER_LANES), const2),
                      pl.BlockSpec((1, ROUTER_LANES), const2),
                      w_in_spec(exp_a), w_in_spec(exp_a), w_out_spec(exp_a),
                      w_in_spec(exp_b), w_in_spec(exp_b), w_out_spec(exp_b),
                      pl.BlockSpec((1, D_MODEL), const2)],
            out_specs=pl.BlockSpec((TMX, D_MODEL), rows)),
        compiler_params=_params(("arbitrary",)),
        name="experts",
    )(ea, eb, blk, nv, xs, router_w, router_b, wg_b, wu_b, wd_b, wg_b, wu_b, wd_b, g_post)


def _combine_kernel(dest_ref, ys_ref, x1_ref, mod_ref, o_ref, ybuf, sems, *, tile0):
    base = (pl.program_id(0) + tile0) * TM

    def start(r, carry):
        _row_copy(ys_ref, dest_ref[base + r], ybuf, r, sems.at[r]).start()
        return carry

    def wait(r, carry):
        _row_copy(ys_ref, dest_ref[base + r], ybuf, r, sems.at[r]).wait()
        return carry

    lax.fori_loop(0, TM, start, 0)
    lax.fori_loop(0, TM, wait, 0)
    gate2 = mod_ref[0, :, 5 * D_MODEL:6 * D_MODEL]
    o_ref[...] = x1_ref[...] + gate2 * ybuf[...]


def _combine(dest, ys, x1_all, mod3, cond_of_tile, tile0, n_tok):
    return pl.pallas_call(
        functools.partial(_combine_kernel, tile0=tile0),
        out_shape=jax.ShapeDtypeStruct((n_tok, D_MODEL), F32),
        grid_spec=pltpu.PrefetchScalarGridSpec(
            num_scalar_prefetch=1,
            grid=(n_tok // TM,),
            in_specs=[pl.BlockSpec(memory_space=pl.ANY),
                      pl.BlockSpec((TM, D_MODEL), lambda i, d: (i + tile0, 0)),
                      pl.BlockSpec((1, 1, 6 * D_MODEL), lambda i, d: (cond_of_tile(i), 0, 0))],
            out_specs=pl.BlockSpec((TM, D_MODEL), lambda i, d: (i, 0)),
            scratch_shapes=[pltpu.VMEM((TM, D_MODEL), F32), pltpu.SemaphoreType.DMA((TM,))]),
        compiler_params=_params(("arbitrary",)),
        name="combine",
    )(dest, ys, x1_all, mod3)


def _schedule(route, counts):
    n_tok = route.shape[0]
    n_max = n_tok // TMX + N_BUCKETS
    bucket = route[:, 0]
    rank = route[:, 1]
    cnt = counts[0, :N_BUCKETS].astype(jnp.int32)
    tiles = (cnt + TMX - 1) // TMX
    tile_end = jnp.cumsum(tiles)
    tile_start = tile_end - tiles
    ids = jnp.arange(N_BUCKETS, dtype=jnp.int32)
    slot0 = jnp.sum(jnp.where(bucket[:, None] == ids[None, :], (tile_start * TMX)[None, :], 0), axis=1)
    dest = slot0 + rank
    i = jnp.arange(n_max, dtype=jnp.int32)
    total = tile_end[-1]
    valid = i < total
    tb = jnp.sum((jnp.minimum(i, total - 1)[:, None] >= tile_end[None, :]).astype(jnp.int32), axis=1)
    pairs = [(a, b) for a in range(EXPERTS_PER_GROUP) for b in range(a + 1, EXPERTS_PER_GROUP)]
    ea_tab = jnp.array([g * EXPERTS_PER_GROUP + a for g in range(N_GROUPS) for a, _ in pairs], jnp.int32)
    eb_tab = jnp.array([g * EXPERTS_PER_GROUP + b for g in range(N_GROUPS) for _, b in pairs], jnp.int32)
    ea = ea_tab[tb]
    eb = eb_tab[tb]
    blk = jnp.where(valid, i, n_max)
    nv = jnp.where(valid, jnp.clip(cnt[tb] - (i - tile_start[tb]) * TMX, 0, TMX), 0)
    return dest, (ea, eb, blk, nv), (n_max + 1) * TMX


def _block_diag_gates(rg_wa, rg_wx):
    heads_per_half = N_HEADS_RNN // 2

    def bd(w):
        w = w.reshape(2, 2, heads_per_half, HEAD_RNN, HEAD_RNN)
        eye = jnp.eye(heads_per_half, dtype=w.dtype)
        full = jnp.einsum('dghij,hk->dghikj', w, eye)
        return full.reshape(2, 2, RNN_HALF, RNN_HALF)

    w = jnp.concatenate([bd(rg_wa), bd(rg_wx)], axis=-1)
    return jnp.transpose(w, (1, 0, 2, 3)).astype(BF16)


def kernel(x_prompt, x_sample, state_rglru, c, c_ctx, w_mod, b_mod, g_pre_mix, g_post_mix, g_pre_ffn,
           g_post_ffn, w_in, conv_w, conv_b, rg_wa, rg_ba, rg_wx, rg_bx, rg_lambda, sgu_g, sgu_w, sgu_b,
           w_out, router_g_w, router_g_b, router_e_w, router_e_b, exp_w_gate, exp_w_up, exp_w_down):
    assert w_mod.shape[0] == 1, "single-layer trunk"
    n_ctx, ctx_len, _ = x_prompt.shape
    n_dec, dec_len, _ = x_sample.shape
    l = 0

    n_cond = SUBLANES
    cond = jnp.zeros((n_cond, D_MODEL), F32).at[0].set(c_ctx).at[1:1 + n_dec].set(c)
    mod3 = _modulation(cond, w_mod[l], b_mod[l]).reshape(n_cond, 1, 6 * D_MODEL)
    pos_tab = _pos_table()

    w_in_b = w_in[l].astype(BF16)
    w_out_b = w_out[l].astype(BF16)
    sgu_w_b = sgu_w[l].reshape(2, 4 * CHUNK, CHUNK).astype(BF16)
    sgu_bias_tile = jnp.repeat(sgu_b[l].T, HEAD_SGU, axis=1)
    w_gates = _block_diag_gates(rg_wa[l], rg_wx[l])
    b_gates = jnp.concatenate([rg_ba[l].reshape(2, 2, 1, RNN_HALF), rg_bx[l].reshape(2, 2, 1, RNN_HALF)],
                              axis=-1)
    b_gates = jnp.transpose(b_gates, (1, 0, 2, 3))
    lam = rg_lambda[l].reshape(2, 1, D_RNN)
    router_w = jnp.zeros((D_MODEL, ROUTER_LANES), F32)
    router_w = router_w.at[:, :N_GROUPS].set(router_g_w[l]).at[:, E_LANE0:E_LANE0 + N_EXPERTS].set(router_e_w[l])
    router_w = router_w.astype(BF16)
    router_b = jnp.zeros((1, ROUTER_LANES), F32)
    router_b = router_b.at[0, :N_GROUPS].set(router_g_b[l]).at[0, E_LANE0:E_LANE0 + N_EXPERTS].set(router_e_b[l])
    wg_b = exp_w_gate[l].astype(BF16)
    wu_b = exp_w_up[l].astype(BF16)
    wd_b = exp_w_down[l].astype(BF16)
    row = lambda v: v.reshape(1, -1)

    n_ctx_tok = n_ctx * ctx_len
    n_dec_tok = n_dec * dec_len
    n_total = n_ctx_tok + n_dec_tok

    def mixer(x, h0, cond_of_tile, seqs_per_step, use_pos, counts0, tile0, chained):
        n_seq, seq_len, _ = x.shape
        xf = x.reshape(n_seq * seq_len, D_MODEL)
        tab = pos_tab if use_pos else None
        xr, gg, y_sgu = _premix(xf, mod3, cond_of_tile, row(g_pre_mix[l]), w_in_b, row(sgu_g[l]),
                                sgu_w_b, sgu_bias_tile, tab)
        y_rnn, fstate = _rglru(xr.reshape(n_seq, seq_len, D_RNN), gg.reshape(n_seq, seq_len, D_RNN), h0,
                               conv_w[l], row(conv_b[l]), w_gates, b_gates, lam, seqs_per_step)
        x1, hn, route, counts = _postmix(
            xf, y_rnn.reshape(n_seq * seq_len, D_RNN), y_sgu, mod3, cond_of_tile, row(g_post_mix[l]),
            row(g_pre_ffn[l]), w_out_b, router_w, router_b, tab, counts0, tile0, n_total, chained)
        return (x1, hn, route), counts, fstate

    tiles_per_seq = dec_len // TM
    ctx_cond = lambda i: 0
    dec_cond = lambda i: 1 + i // tiles_per_seq
    h0_ctx = jnp.zeros((n_ctx, 2, D_RNN), F32)
    counts0 = jnp.zeros((1, ROUTER_LANES), F32)
    arrays, counts, st = mixer(x_prompt, h0_ctx, ctx_cond, 8, False, counts0, 0, None)
    new_state = st.astype(state_rglru.dtype)[:, None]
    (x1_all, hn_all, route), counts, _ = mixer(x_sample, state_rglru[:, l].astype(F32), dec_cond, 1, True,
                                               counts, n_ctx_tok // TM, arrays)

    dest, sched, n_slots = _schedule(route, counts)
    xs = _dispatch(dest, hn_all, n_slots)
    ys = _experts(sched, xs, router_w, router_b, wg_b, wu_b, wd_b, row(g_post_ffn[l]))
    y_prompt = _combine(dest, ys, x1_all, mod3, ctx_cond, 0, n_ctx_tok)
    y_sample = _combine(dest, ys, x1_all, mod3, dec_cond, n_ctx_tok // TM, n_dec_tok)
    return (y_prompt.reshape(x_prompt.shape), y_sample.reshape(x_sample.shape), new_state)
```

```python
import functools
import math

import jax
import jax.numpy as jnp
from jax import lax
from jax.experimental import pallas as pl
from jax.experimental.pallas import tpu as pltpu

D_MODEL = 1024
D_RNN = 512
D_SGU = 512
N_HEADS_RNN = 8
HEAD_RNN = D_RNN // N_HEADS_RNN
N_HEADS_SGU = 8
HEAD_SGU = D_SGU // N_HEADS_SGU
CHUNK = 128
GRID_W = 64
RG_C = 8.0
N_GROUPS = 4
EXPERTS_PER_GROUP = 4
N_EXPERTS = N_GROUPS * EXPERTS_PER_GROUP
D_EXPERT = 512
EPS = 1e-6
POS_BASE = 10000.0

LANES = 128
SUBLANES = 8
RNN_HALF = D_RNN // 2
ROUTER_LANES = LANES
E_LANE0 = N_GROUPS

PAIRS_PER_GROUP = EXPERTS_PER_GROUP * (EXPERTS_PER_GROUP - 1) // 2
N_BUCKETS = N_GROUPS * PAIRS_PER_GROUP

ROW_TILES = D_MODEL // LANES

TM = 256
TD = 512
TMX = 512
SCAN_CHUNK = 128
VMEM_LIMIT = 56 * 1024 * 1024

F32 = jnp.float32
BF16 = jnp.bfloat16


def _params(sem):
    return pltpu.CompilerParams(dimension_semantics=sem, vmem_limit_bytes=VMEM_LIMIT)


def _rms(x):
    return x * lax.rsqrt(jnp.mean(x * x, axis=-1, keepdims=True) + EPS)


def _sigmoid(x):
    return 1.0 / (1.0 + jnp.exp(-x))


def _mod_kernel(cond_ref, w_ref, b_ref, o_ref):
    c = cond_ref[...]
    s = c * _sigmoid(c)
    o_ref[...] = jnp.dot(s.astype(BF16), w_ref[...].astype(BF16),
                         preferred_element_type=F32) + b_ref[...]


def _modulation(cond, w_mod, b_mod):
    n = w_mod.shape[1]
    bn = 1024
    return pl.pallas_call(
        _mod_kernel,
        out_shape=jax.ShapeDtypeStruct((cond.shape[0], n), F32),
        grid=(n // bn,),
        in_specs=[pl.BlockSpec(cond.shape, lambda j: (0, 0)),
                  pl.BlockSpec((D_MODEL, bn), lambda j: (0, j)),
                  pl.BlockSpec((1, bn), lambda j: (0, j))],
        out_specs=pl.BlockSpec((cond.shape[0], bn), lambda j: (0, j)),
        compiler_params=_params(("arbitrary",)),
        name="modulation",
    )(cond, w_mod, b_mod.reshape(1, n))


def _pos_kernel(o_ref):
    n_freq = D_MODEL // 4
    k = lax.broadcasted_iota(jnp.int32, (GRID_W, n_freq), 1).astype(F32)
    p = lax.broadcasted_iota(jnp.int32, (GRID_W, n_freq), 0).astype(F32)
    freq = jnp.exp(-math.log(POS_BASE) * k / n_freq)
    ang = p * freq
    o_ref[:, 0:n_freq] = jnp.sin(ang)
    o_ref[:, n_freq:2 * n_freq] = jnp.cos(ang)


def _pos_table():
    return pl.pallas_call(
        _pos_kernel,
        out_shape=jax.ShapeDtypeStruct((GRID_W, D_MODEL // 2), F32),
        name="pos_table",
    )()


def _tile_pos(rows_ref, cols_ref):
    reps = TM // GRID_W
    rpart = jnp.concatenate(
        [jnp.broadcast_to(rows_ref[q:q + 1, :], (GRID_W, D_MODEL // 2)) for q in range(reps)], axis=0)
    cpart = jnp.concatenate([cols_ref[...]] * reps, axis=0)
    return rpart, cpart


def _load_x(x_ref, pos_refs):
    x = x_ref[...]
    if pos_refs is None:
        return x
    rpart, cpart = _tile_pos(*pos_refs)
    return jnp.concatenate([x[:, :D_MODEL // 2] + rpart, x[:, D_MODEL // 2:] + cpart], axis=1)


def _premix_kernel(*refs, add_pos):
    if add_pos:
        (x_ref, rows_ref, cols_ref, mod_ref, g_ref, win_ref, sgug_ref, sguw_ref, sgub_ref,
         xr_ref, gg_ref, ys_ref) = refs
        pos_refs = (rows_ref, cols_ref)
    else:
        (x_ref, mod_ref, g_ref, win_ref, sgug_ref, sguw_ref, sgub_ref,
         xr_ref, gg_ref, ys_ref) = refs
        pos_refs = None
    x = _load_x(x_ref, pos_refs)
    shift = mod_ref[0, :, 0:D_MODEL]
    scale = mod_ref[0, :, D_MODEL:2 * D_MODEL]
    hn = _rms(x) * g_ref[...] * (1.0 + scale) + shift
    z = jnp.dot(hn.astype(BF16), win_ref[...], preferred_element_type=F32)
    xr_ref[...] = z[:, 0:D_RNN]
    gg_ref[...] = jax.nn.gelu(z[:, D_RNN:2 * D_RNN])
    u = z[:, 2 * D_RNN:2 * D_RNN + D_SGU]
    v = z[:, 2 * D_RNN + D_SGU:]
    vn = (_rms(v) * sgug_ref[...]).astype(BF16)
    half = D_SGU // 2
    heads_per_half = N_HEADS_SGU // 2
    lane_head = lax.broadcasted_iota(jnp.int32, (CHUNK, half), 1) // HEAD_SGU
    for c in range(TM // CHUNK):
        rows = slice(c * CHUNK, (c + 1) * CHUNK)
        parts = []
        for hf in range(2):
            r = jnp.dot(sguw_ref[hf], vn[rows, hf * half:(hf + 1) * half],
                        preferred_element_type=F32)
            s = jnp.zeros((CHUNK, half), F32)
            for h in range(heads_per_half):
                s = jnp.where(lane_head == h, r[h * CHUNK:(h + 1) * CHUNK], s)
            parts.append(s)
        s = jnp.concatenate(parts, axis=1) + sgub_ref[...]
        ys_ref[rows, :] = (u[rows] * s).astype(BF16)


def _premix(x, mod3, cond_of_tile, g_pre, w_in_b, sgu_g, sgu_w_b, sgu_bias_tile, pos_tab):
    n_tok = x.shape[0]
    n_tiles = n_tok // TM
    add_pos = pos_tab is not None
    tok = lambda i: (i, 0)
    const2 = lambda i: (0, 0)
    in_specs = [pl.BlockSpec((TM, D_MODEL), tok)]
    args = [x]
    if add_pos:
        reps = TM // GRID_W
        tiles_per_seq = GRID_W // reps
        in_specs += [pl.BlockSpec((None, reps, D_MODEL // 2), lambda i: (i % tiles_per_seq, 0, 0)),
                     pl.BlockSpec((GRID_W, D_MODEL // 2), const2)]
        args += [pos_tab.reshape(tiles_per_seq, reps, D_MODEL // 2), pos_tab]
    in_specs += [pl.BlockSpec((1, 1, 6 * D_MODEL), lambda i: (cond_of_tile(i), 0, 0)),
                 pl.BlockSpec((1, D_MODEL), const2),
                 pl.BlockSpec((D_MODEL, 2 * D_RNN + 2 * D_SGU), const2),
                 pl.BlockSpec((1, D_SGU), const2),
                 pl.BlockSpec((2, 4 * CHUNK, CHUNK), lambda i: (0, 0, 0)),
                 pl.BlockSpec((CHUNK, D_SGU), const2)]
    args += [mod3, g_pre, w_in_b, sgu_g, sgu_w_b, sgu_bias_tile]
    return pl.pallas_call(
        functools.partial(_premix_kernel, add_pos=add_pos),
        out_shape=(jax.ShapeDtypeStruct((n_tok, D_RNN), F32),
                   jax.ShapeDtypeStruct((n_tok, D_RNN), F32),
                   jax.ShapeDtypeStruct((n_tok, D_SGU), BF16)),
        grid=(n_tiles,),
        in_specs=in_specs,
        out_specs=(pl.BlockSpec((TM, D_RNN), tok),
                   pl.BlockSpec((TM, D_RNN), tok),
                   pl.BlockSpec((TM, D_SGU), tok)),
        compiler_params=_params(("parallel",)),
        name="premix",
    )(*args)


def _block_scan(a, b, h, reverse):
    n, w = a.shape
    nblk = n // SUBLANES
    a = a.reshape(nblk, SUBLANES, w)
    b = b.reshape(nblk, SUBLANES, w)
    sub = lax.broadcasted_iota(jnp.int32, (1, SUBLANES, w), 1)
    for s in (1, 2, 4):
        if reverse:
            keep = sub < SUBLANES - s
            shift = SUBLANES - s
        else:
            keep = sub >= s
            shift = s
        a_sh = jnp.where(keep, pltpu.roll(a, shift, 1), 1.0)
        b_sh = jnp.where(keep, pltpu.roll(b, shift, 1), 0.0)
        b = a * b_sh + b
        a = a * a_sh
    out = [None] * nblk
    order = range(nblk - 1, -1, -1) if reverse else range(nblk)
    edge = 0 if reverse else SUBLANES - 1
    for j in order:
        hj = b[j] + a[j] * h
        out[j] = hj
        h = hj[edge:edge + 1, :]
    return jnp.concatenate(out, axis=0), h


def _rglru_kernel(xr_ref, gg_ref, h0_ref, convw_ref, convb_ref, wg_ref, bg_ref, lam_ref,
                  y_ref, fs_ref, hf_ref, *, nb, seq_len):
    n_chunks = seq_len // SCAN_CHUNK
    pad = SUBLANES
    win_rows = SCAN_CHUNK + 2 * pad

    def gates(s, c, d):
        t0 = pl.multiple_of(c * SCAN_CHUNK, SCAN_CHUNK)
        before = pl.multiple_of(jnp.maximum(t0 - pad, 0), pad)
        after = pl.multiple_of(jnp.minimum(t0 + SCAN_CHUNK, seq_len - pad), pad)
        head = jnp.where(c > 0, xr_ref[s, pl.ds(before, pad), :], 0.0)
        tail = jnp.where(c < n_chunks - 1, xr_ref[s, pl.ds(after, pad), :], 0.0)
        win = jnp.concatenate([head, xr_ref[s, pl.ds(t0, SCAN_CHUNK), :], tail], axis=0)
        taps = [pltpu.roll(win, 2, 0), pltpu.roll(win, 1, 0), win, pltpu.roll(win, win_rows - 1, 0)]
        xc = convb_ref[...]
        for k in range(4):
            xc = xc + taps[k][pad:pad + SCAN_CHUNK] * convw_ref[k:k + 1, :]
        g = jnp.dot(xc.astype(BF16), wg_ref[d], preferred_element_type=F32) + bg_ref[d]
        r = _sigmoid(g[:, :RNN_HALF])
        i = _sigmoid(g[:, RNN_HALF:])
        neg_lam = -lam_ref[d]
        softplus = jnp.maximum(neg_lam, 0.0) + jnp.log(1.0 + jnp.exp(-jnp.abs(neg_lam)))
        log_a = -RG_C * r * softplus
        a = jnp.exp(log_a)
        b = jnp.sqrt(jnp.tanh(-log_a) * (a * a + 1.0)) * (i * xc)
        return t0, a, b

    def per_seq(s, carry):
        h0 = h0_ref[s]

        def fwd(c, h):
            t0, a, b = gates(s, c, 0)
            hc, h = _block_scan(a, b, h, reverse=False)
            hf_ref[pl.ds(t0, SCAN_CHUNK), :] = hc
            return h

        hf_last = lax.fori_loop(0, n_chunks, fwd, h0[0:1, :])

        def bwd(k, h):
            c = n_chunks - 1 - k
            t0, a, b = gates(s, c, 1)
            hc, h = _block_scan(a, b, h, reverse=True)
            y = (hf_ref[pl.ds(t0, SCAN_CHUNK), :] + hc) * gg_ref[s, pl.ds(t0, SCAN_CHUNK), :]
            y_ref[s, pl.ds(t0, SCAN_CHUNK), :] = y.astype(BF16)
            return h

        hb_first = lax.fori_loop(0, n_chunks, bwd, h0[1:2, :])
        fs_ref[s] = jnp.concatenate([hf_last, hb_first], axis=0)
        return carry

    lax.fori_loop(0, nb, per_seq, 0)


def _rglru(xr, gg, h0, conv_w, conv_b, w_gates, b_gates, lam, nb):
    n_seq, seq_len, _ = xr.shape
    blk = lambda i, j: (i, 0, j)
    chan = lambda i, j: (0, j)
    return pl.pallas_call(
        functools.partial(_rglru_kernel, nb=nb, seq_len=seq_len),
        out_shape=(jax.ShapeDtypeStruct((n_seq, seq_len, D_RNN), BF16),
                   jax.ShapeDtypeStruct((n_seq, 2, D_RNN), F32)),
        grid=(n_seq // nb, D_RNN // RNN_HALF),
        in_specs=[pl.BlockSpec((nb, seq_len, RNN_HALF), blk),
                  pl.BlockSpec((nb, seq_len, RNN_HALF), blk),
                  pl.BlockSpec((nb, 2, RNN_HALF), blk),
                  pl.BlockSpec((4, RNN_HALF), chan),
                  pl.BlockSpec((1, RNN_HALF), chan),
                  pl.BlockSpec((None, 2, RNN_HALF, 2 * RNN_HALF), lambda i, j: (j, 0, 0, 0)),
                  pl.BlockSpec((None, 2, 1, 2 * RNN_HALF), lambda i, j: (j, 0, 0, 0)),
                  pl.BlockSpec((2, 1, RNN_HALF), lambda i, j: (0, 0, j))],
        out_specs=(pl.BlockSpec((nb, seq_len, RNN_HALF), blk),
                   pl.BlockSpec((nb, 2, RNN_HALF), blk)),
        scratch_shapes=[pltpu.VMEM((seq_len, RNN_HALF), F32)],
        compiler_params=_params(("parallel", "parallel")),
        name="rglru",
    )(xr, gg, h0, conv_w, conv_b, w_gates, b_gates, lam)


def _route(logits):
    lane = lax.broadcasted_iota(jnp.int32, logits.shape, 1)
    neg = jnp.float32(-jnp.inf)
    gmask = lane < N_GROUPS
    gl = jnp.where(gmask, logits, neg)
    gmax = jnp.max(gl, axis=-1, keepdims=True)
    g_idx = jnp.min(jnp.where(gl == gmax, lane, ROUTER_LANES), axis=-1, keepdims=True)
    e0 = E_LANE0 + EXPERTS_PER_GROUP * g_idx
    emask = (lane >= e0) & (lane < e0 + EXPERTS_PER_GROUP)
    el = jnp.where(emask, logits, neg)
    l1 = jnp.max(el, axis=-1, keepdims=True)
    i1 = jnp.min(jnp.where(el == l1, lane, ROUTER_LANES), axis=-1, keepdims=True)
    el2 = jnp.where(lane == i1, neg, el)
    l2 = jnp.max(el2, axis=-1, keepdims=True)
    i2 = jnp.min(jnp.where(emask & (lane != i1) & (el2 == l2), lane, ROUTER_LANES), axis=-1, keepdims=True)
    ja = jnp.minimum(i1, i2) - e0
    jb = jnp.maximum(i1, i2) - e0
    pair = (ja * (2 * EXPERTS_PER_GROUP - 1 - ja)) // 2 + (jb - ja - 1)
    return g_idx * PAIRS_PER_GROUP + pair


def _store_token_major(ref, x):
    n = x.shape[0]
    for k in range(ROW_TILES):
        ref[pl.ds(k, n, stride=ROW_TILES), :] = x[:, k * LANES:(k + 1) * LANES]


def _load_token_major(ref, n):
    return jnp.concatenate([ref[pl.ds(k, n, stride=ROW_TILES), :] for k in range(ROW_TILES)], axis=1)


def _postmix_kernel(*refs, add_pos):
    refs = list(refs)
    x_ref = refs.pop(0)
    pos_refs = (refs.pop(0), refs.pop(0)) if add_pos else None
    (yr_ref, ys_ref, mod_ref, gpost_ref, gpre_ref, wout_ref, rw_ref, rb_ref, cnt0_ref,
     x1_ref, hn_ref, rt_ref, cnt_ref, run_ref) = refs

    @pl.when(pl.program_id(0) == 0)
    def _():
        run_ref[...] = cnt0_ref[...]

    x = _load_x(x_ref, pos_refs)
    gate1 = mod_ref[0, :, 2 * D_MODEL:3 * D_MODEL]
    shift2 = mod_ref[0, :, 3 * D_MODEL:4 * D_MODEL]
    scale2 = mod_ref[0, :, 4 * D_MODEL:5 * D_MODEL]
    y = (jnp.dot(yr_ref[...], wout_ref[0:D_RNN, :], preferred_element_type=F32)
         + jnp.dot(ys_ref[...], wout_ref[D_RNN:, :], preferred_element_type=F32))
    x1 = x + gate1 * (_rms(y) * gpost_ref[...])
    x1_ref[...] = x1
    hn = _rms(x1) * gpre_ref[...] * (1.0 + scale2) + shift2
    _store_token_major(hn_ref, hn)
    logits = jnp.dot(hn.astype(BF16), rw_ref[...], preferred_element_type=F32) + rb_ref[...]
    bucket = _route(logits)
    lane = lax.broadcasted_iota(jnp.int32, (TM, ROUTER_LANES), 1)
    onehot = lane == bucket
    r_i = lax.broadcasted_iota(jnp.int32, (TM, TM), 0)
    c_i = lax.broadcasted_iota(jnp.int32, (TM, TM), 1)
    earlier = (c_i < r_i).astype(BF16)
    before = jnp.dot(earlier, onehot.astype(BF16), preferred_element_type=F32) + run_ref[...]
    rank = jnp.sum(jnp.where(onehot, before, 0.0), axis=-1, keepdims=True).astype(jnp.int32)
    rt_ref[...] = jnp.where(lane == 0, bucket, jnp.where(lane == 1, rank, 0))
    run_ref[...] += jnp.sum(onehot.astype(F32), axis=0, keepdims=True)
    cnt_ref[...] = run_ref[...]


def _postmix(x, y_rnn, y_sgu, mod3, cond_of_tile, g_post, g_pre, w_out_b, router_w, router_b, pos_tab,
             counts0):
    n_tok = x.shape[0]
    add_pos = pos_tab is not None
    tok = lambda i: (i, 0)
    const2 = lambda i: (0, 0)
    in_specs = [pl.BlockSpec((TM, D_MODEL), tok)]
    args = [x]
    if add_pos:
        reps = TM // GRID_W
        tiles_per_seq = GRID_W // reps
        in_specs += [pl.BlockSpec((None, reps, D_MODEL // 2), lambda i: (i % tiles_per_seq, 0, 0)),
                     pl.BlockSpec((GRID_W, D_MODEL // 2), const2)]
        args += [pos_tab.reshape(tiles_per_seq, reps, D_MODEL // 2), pos_tab]
    in_specs += [pl.BlockSpec((TM, D_RNN), tok),
                 pl.BlockSpec((TM, D_SGU), tok),
                 pl.BlockSpec((1, 1, 6 * D_MODEL), lambda i: (cond_of_tile(i), 0, 0)),
                 pl.BlockSpec((1, D_MODEL), const2),
                 pl.BlockSpec((1, D_MODEL), const2),
                 pl.BlockSpec((D_MODEL, D_MODEL), const2),
                 pl.BlockSpec((D_MODEL, ROUTER_LANES), const2),
                 pl.BlockSpec((1, ROUTER_LANES), const2),
                 pl.BlockSpec((1, ROUTER_LANES), const2)]
    args += [y_rnn, y_sgu, mod3, g_post, g_pre, w_out_b, router_w, router_b, counts0]
    return pl.pallas_call(
        functools.partial(_postmix_kernel, add_pos=add_pos),
        out_shape=(jax.ShapeDtypeStruct((n_tok, D_MODEL), F32),
                   jax.ShapeDtypeStruct((n_tok * ROW_TILES, LANES), F32),
                   jax.ShapeDtypeStruct((n_tok, ROUTER_LANES), jnp.int32),
                   jax.ShapeDtypeStruct((1, ROUTER_LANES), F32)),
        grid=(n_tok // TM,),
        in_specs=in_specs,
        out_specs=(pl.BlockSpec((TM, D_MODEL), tok),
                   pl.BlockSpec((TM * ROW_TILES, LANES), tok),
                   pl.BlockSpec((TM, ROUTER_LANES), tok),
                   pl.BlockSpec((1, ROUTER_LANES), const2)),
        scratch_shapes=[pltpu.VMEM((1, ROUTER_LANES), F32)],
        compiler_params=_params(("arbitrary",)),
        name="postmix",
    )(*args)


def _token_rows(ref, t):
    return ref.at[pl.ds(pl.multiple_of(t * ROW_TILES, ROW_TILES), ROW_TILES), :]


def _dispatch_kernel(dest_ref, hc_ref, hs_ref, xs_ref, sem, *, n_ctx_steps):
    i = pl.program_id(0)
    base = i * TD

    def scatter(src_ref):
        def start(r, carry):
            pltpu.make_async_copy(_token_rows(src_ref, r), _token_rows(xs_ref, dest_ref[base + r]), sem).start()
            return carry

        lax.fori_loop(0, TD, start, 0, unroll=8)
        pltpu.make_async_copy(src_ref, xs_ref.at[pl.ds(0, TD * ROW_TILES), :], sem).wait()

    @pl.when(i < n_ctx_steps)
    def _():
        scatter(hc_ref)

    @pl.when(i >= n_ctx_steps)
    def _():
        scatter(hs_ref)


def _dispatch(dest, hn_ctx, hn_dec, n_slots):
    n_ctx_steps = hn_ctx.shape[0] // (TD * ROW_TILES)
    n_dec_steps = hn_dec.shape[0] // (TD * ROW_TILES)
    return pl.pallas_call(
        functools.partial(_dispatch_kernel, n_ctx_steps=n_ctx_steps),
        out_shape=jax.ShapeDtypeStruct((n_slots * ROW_TILES, LANES), F32),
        grid_spec=pltpu.PrefetchScalarGridSpec(
            num_scalar_prefetch=1,
            grid=(n_ctx_steps + n_dec_steps,),
            in_specs=[pl.BlockSpec((TD * ROW_TILES, LANES), lambda i, d: (jnp.minimum(i, n_ctx_steps - 1), 0)),
                      pl.BlockSpec((TD * ROW_TILES, LANES), lambda i, d: (jnp.maximum(i - n_ctx_steps, 0), 0))],
            out_specs=pl.BlockSpec(memory_space=pl.ANY),
            scratch_shapes=[pltpu.SemaphoreType.DMA(())]),
        compiler_params=_params(("arbitrary",)),
        name="dispatch",
    )(dest, hn_ctx, hn_dec)


def _experts_kernel(ea_ref, eb_ref, nv_ref, xs_ref, rw_ref, rb_ref,
                    wga_ref, wua_ref, wda_ref, wgb_ref, wub_ref, wdb_ref, gpost_ref, ys_ref):
    i = pl.program_id(0)
    nv = nv_ref[i]

    @pl.when(nv == 0)
    def _():
        ys_ref[...] = jnp.zeros_like(ys_ref)

    @pl.when(nv > 0)
    def _():
        row = lax.broadcasted_iota(jnp.int32, (TMX, 1), 0)
        xb = jnp.where(row < nv, _load_token_major(xs_ref, TMX), 0.0).astype(BF16)
        logits = jnp.dot(xb, rw_ref[...], preferred_element_type=F32) + rb_ref[...]
        lane = lax.broadcasted_iota(jnp.int32, logits.shape, 1)
        ea = ea_ref[i]
        eb = eb_ref[i]
        gmask = lane < N_GROUPS
        gl = jnp.where(gmask, logits, -jnp.inf)
        gmax = jnp.max(gl, axis=-1, keepdims=True)
        gexp = jnp.where(gmask, jnp.exp(gl - gmax), 0.0)
        g_own = jnp.sum(jnp.where(lane == ea // EXPERTS_PER_GROUP, gexp, 0.0), axis=-1, keepdims=True)
        g_w = g_own / jnp.sum(gexp, axis=-1, keepdims=True)
        la = jnp.sum(jnp.where(lane == ea + E_LANE0, logits, 0.0), axis=-1, keepdims=True)
        lb = jnp.sum(jnp.where(lane == eb + E_LANE0, logits, 0.0), axis=-1, keepdims=True)
        m = jnp.maximum(la, lb)
        pa = jnp.exp(la - m)
        pb = jnp.exp(lb - m)
        inv = g_w / (pa + pb)

        def expert(wg_ref, wu_ref, wd_ref, w):
            g = jnp.dot(xb, wg_ref[0], preferred_element_type=F32)
            u = jnp.dot(xb, wu_ref[0], preferred_element_type=F32)
            act = (g * _sigmoid(g)) * u * w
            return jnp.dot(act.astype(BF16), wd_ref[0], preferred_element_type=F32)

        y = expert(wga_ref, wua_ref, wda_ref, pa * inv) + expert(wgb_ref, wub_ref, wdb_ref, pb * inv)
        _store_token_major(ys_ref, _rms(y) * gpost_ref[...])


def _experts(sched, xs, router_w, router_b, wg_b, wu_b, wd_b, g_post):
    ea, eb, nv = sched
    n_tiles = ea.shape[0]
    rows = lambda i, ea, eb, nv: (i, 0)
    const2 = lambda i, ea, eb, nv: (0, 0)
    exp_a = lambda i, ea, eb, nv: (ea[i], 0, 0)
    exp_b = lambda i, ea, eb, nv: (eb[i], 0, 0)
    w_in_spec = lambda m: pl.BlockSpec((1, D_MODEL, D_EXPERT), m)
    w_out_spec = lambda m: pl.BlockSpec((1, D_EXPERT, D_MODEL), m)
    return pl.pallas_call(
        _experts_kernel,
        out_shape=jax.ShapeDtypeStruct(xs.shape, F32),
        grid_spec=pltpu.PrefetchScalarGridSpec(
            num_scalar_prefetch=3,
            grid=(n_tiles,),
            in_specs=[pl.BlockSpec((TMX * ROW_TILES, LANES), rows),
                      pl.BlockSpec((D_MODEL, ROUTER_LANES), const2),
                      pl.BlockSpec((1, ROUTER_LANES), const2),
                      w_in_spec(exp_a), w_in_spec(exp_a), w_out_spec(exp_a),
                      w_in_spec(exp_b), w_in_spec(exp_b), w_out_spec(exp_b),
                      pl.BlockSpec((1, D_MODEL), const2)],
            out_specs=pl.BlockSpec((TMX * ROW_TILES, LANES), rows)),
        compiler_params=_params(("arbitrary",)),
        name="experts",
    )(ea, eb, nv, xs, router_w, router_b, wg_b, wu_b, wd_b, wg_b, wu_b, wd_b, g_post)


def _combine_kernel(dest_ref, ys_ref, x1_ref, mod_ref, o_ref, ybuf, sems, *, tile0):
    i = pl.program_id(0)
    n = pl.num_programs(0)

    def fetch(tile, slot):
        base = (tile + tile0) * TM

        def start(r, carry):
            pltpu.make_async_copy(_token_rows(ys_ref, dest_ref[base + r]), _token_rows(ybuf.at[slot], r),
                                  sems.at[slot]).start()
            return carry

        lax.fori_loop(0, TM, start, 0, unroll=8)

    @pl.when(i == 0)
    def _():
        fetch(0, 0)

    @pl.when(i + 1 < n)
    def _():
        fetch(i + 1, (i + 1) % 2)

    slot = i % 2
    pltpu.make_async_copy(ys_ref.at[pl.ds(0, TM * ROW_TILES), :], ybuf.at[slot], sems.at[slot]).wait()
    gate2 = mod_ref[0, :, 5 * D_MODEL:6 * D_MODEL]
    o_ref[...] = x1_ref[...] + gate2 * _load_token_major(ybuf.at[slot], TM)


def _combine(dest, ys, x1, mod3, cond_of_tile, tile0):
    n_tok = x1.shape[0]
    return pl.pallas_call(
        functools.partial(_combine_kernel, tile0=tile0),
        out_shape=jax.ShapeDtypeStruct((n_tok, D_MODEL), F32),
        grid_spec=pltpu.PrefetchScalarGridSpec(
            num_scalar_prefetch=1,
            grid=(n_tok // TM,),
            in_specs=[pl.BlockSpec(memory_space=pl.ANY),
                      pl.BlockSpec((TM, D_MODEL), lambda i, d: (i, 0)),
                      pl.BlockSpec((1, 1, 6 * D_MODEL), lambda i, d: (cond_of_tile(i), 0, 0))],
            out_specs=pl.BlockSpec((TM, D_MODEL), lambda i, d: (i, 0)),
            scratch_shapes=[pltpu.VMEM((2, TM * ROW_TILES, LANES), F32), pltpu.SemaphoreType.DMA((2,))]),
        compiler_params=_params(("arbitrary",)),
        name="combine",
    )(dest, ys, x1, mod3)


def _schedule(route, counts):
    n_tok = route.shape[0]
    n_max = n_tok // TMX + N_BUCKETS
    bucket = route[:, 0]
    rank = route[:, 1]
    cnt = counts[0, :N_BUCKETS].astype(jnp.int32)
    tiles = (cnt + TMX - 1) // TMX
    tile_end = jnp.cumsum(tiles)
    tile_start = tile_end - tiles
    ids = jnp.arange(N_BUCKETS, dtype=jnp.int32)
    slot0 = jnp.sum(jnp.where(bucket[:, None] == ids[None, :], (tile_start * TMX)[None, :], 0), axis=1)
    dest = slot0 + rank
    i = jnp.arange(n_max, dtype=jnp.int32)
    total = tile_end[-1]
    valid = i < total
    tb = jnp.sum((jnp.minimum(i, total - 1)[:, None] >= tile_end[None, :]).astype(jnp.int32), axis=1)
    pairs = [(a, b) for a in range(EXPERTS_PER_GROUP) for b in range(a + 1, EXPERTS_PER_GROUP)]
    ea_tab = jnp.array([g * EXPERTS_PER_GROUP + a for g in range(N_GROUPS) for a, _ in pairs], jnp.int32)
    eb_tab = jnp.array([g * EXPERTS_PER_GROUP + b for g in range(N_GROUPS) for _, b in pairs], jnp.int32)
    ea = ea_tab[tb]
    eb = eb_tab[tb]
    nv = jnp.where(valid, jnp.clip(cnt[tb] - (i - tile_start[tb]) * TMX, 0, TMX), 0)
    return dest, (ea, eb, nv), n_max * TMX


def _block_diag_gates(rg_wa, rg_wx):
    heads_per_half = N_HEADS_RNN // 2

    def bd(w):
        w = w.reshape(2, 2, heads_per_half, HEAD_RNN, HEAD_RNN)
        eye = jnp.eye(heads_per_half, dtype=w.dtype)
        full = jnp.einsum('dghij,hk->dghikj', w, eye)
        return full.reshape(2, 2, RNN_HALF, RNN_HALF)

    w = jnp.concatenate([bd(rg_wa), bd(rg_wx)], axis=-1)
    return jnp.transpose(w, (1, 0, 2, 3)).astype(BF16)


def kernel(x_prompt, x_sample, state_rglru, c, c_ctx, w_mod, b_mod, g_pre_mix, g_post_mix, g_pre_ffn,
           g_post_ffn, w_in, conv_w, conv_b, rg_wa, rg_ba, rg_wx, rg_bx, rg_lambda, sgu_g, sgu_w, sgu_b,
           w_out, router_g_w, router_g_b, router_e_w, router_e_b, exp_w_gate, exp_w_up, exp_w_down):
    assert w_mod.shape[0] == 1, "single-layer trunk"
    n_ctx, ctx_len, _ = x_prompt.shape
    n_dec, dec_len, _ = x_sample.shape
    l = 0

    n_cond = SUBLANES
    cond = jnp.zeros((n_cond, D_MODEL), F32).at[0].set(c_ctx).at[1:1 + n_dec].set(c)
    mod3 = _modulation(cond, w_mod[l], b_mod[l]).reshape(n_cond, 1, 6 * D_MODEL)
    pos_tab = _pos_table()

    w_in_b = w_in[l].astype(BF16)
    w_out_b = w_out[l].astype(BF16)
    sgu_w_b = sgu_w[l].reshape(2, 4 * CHUNK, CHUNK).astype(BF16)
    sgu_bias_tile = jnp.repeat(sgu_b[l].T, HEAD_SGU, axis=1)
    w_gates = _block_diag_gates(rg_wa[l], rg_wx[l])
    b_gates = jnp.concatenate([rg_ba[l].reshape(2, 2, 1, RNN_HALF), rg_bx[l].reshape(2, 2, 1, RNN_HALF)],
                              axis=-1)
    b_gates = jnp.transpose(b_gates, (1, 0, 2, 3))
    lam = rg_lambda[l].reshape(2, 1, D_RNN)
    router_w = jnp.zeros((D_MODEL, ROUTER_LANES), F32)
    router_w = router_w.at[:, :N_GROUPS].set(router_g_w[l]).at[:, E_LANE0:E_LANE0 + N_EXPERTS].set(router_e_w[l])
    router_w = router_w.astype(BF16)
    router_b = jnp.zeros((1, ROUTER_LANES), F32)
    router_b = router_b.at[0, :N_GROUPS].set(router_g_b[l]).at[0, E_LANE0:E_LANE0 + N_EXPERTS].set(router_e_b[l])
    wg_b = exp_w_gate[l].astype(BF16)
    wu_b = exp_w_up[l].astype(BF16)
    wd_b = exp_w_down[l].astype(BF16)
    row = lambda v: v.reshape(1, -1)

    n_ctx_tok = n_ctx * ctx_len

    def mixer(x, h0, cond_of_tile, seqs_per_step, use_pos, counts0):
        n_seq, seq_len, _ = x.shape
        xf = x.reshape(n_seq * seq_len, D_MODEL)
        tab = pos_tab if use_pos else None
        xr, gg, y_sgu = _premix(xf, mod3, cond_of_tile, row(g_pre_mix[l]), w_in_b, row(sgu_g[l]),
                                sgu_w_b, sgu_bias_tile, tab)
        y_rnn, fstate = _rglru(xr.reshape(n_seq, seq_len, D_RNN), gg.reshape(n_seq, seq_len, D_RNN), h0,
                               conv_w[l], row(conv_b[l]), w_gates, b_gates, lam, seqs_per_step)
        x1, hn, route, counts = _postmix(
            xf, y_rnn.reshape(n_seq * seq_len, D_RNN), y_sgu, mod3, cond_of_tile, row(g_post_mix[l]),
            row(g_pre_ffn[l]), w_out_b, router_w, router_b, tab, counts0)
        return x1, hn, route, counts, fstate

    tiles_per_seq = dec_len // TM
    ctx_cond = lambda i: 0
    dec_cond = lambda i: 1 + i // tiles_per_seq
    h0_ctx = jnp.zeros((n_ctx, 2, D_RNN), F32)
    counts0 = jnp.zeros((1, ROUTER_LANES), F32)
    x1_ctx, hn_ctx, route_ctx, counts, st = mixer(x_prompt, h0_ctx, ctx_cond, 8, False, counts0)
    new_state = st.astype(state_rglru.dtype)[:, None]
    x1_dec, hn_dec, route_dec, counts, _ = mixer(x_sample, state_rglru[:, l].astype(F32), dec_cond, 1, True,
                                                 counts)

    route = jnp.concatenate([route_ctx[:, :2], route_dec[:, :2]], axis=0)
    dest, sched, n_slots = _schedule(route, counts)
    xs = _dispatch(dest, hn_ctx, hn_dec, n_slots)
    ys = _experts(sched, xs, router_w, router_b, wg_b, wu_b, wd_b, row(g_post_ffn[l]))
    y_prompt = _combine(dest, ys, x1_ctx, mod3, ctx_cond, 0)
    y_sample = _combine(dest, ys, x1_dec, mod3, dec_cond, n_ctx_tok // TM)
    return (y_prompt.reshape(x_prompt.shape), y_sample.reshape(x_sample.shape), new_state)
```

```python
import functools
import math

import jax
import jax.numpy as jnp
from jax import lax
from jax.experimental import pallas as pl
from jax.experimental.pallas import tpu as pltpu

D_MODEL = 1024
D_RNN = 512
D_SGU = 512
N_HEADS_RNN = 8
HEAD_RNN = D_RNN // N_HEADS_RNN
N_HEADS_SGU = 8
HEAD_SGU = D_SGU // N_HEADS_SGU
CHUNK = 128
GRID_W = 64
RG_C = 8.0
N_GROUPS = 4
EXPERTS_PER_GROUP = 4
N_EXPERTS = N_GROUPS * EXPERTS_PER_GROUP
D_EXPERT = 512
EPS = 1e-6
POS_BASE = 10000.0

LANES = 128
SUBLANES = 8
RNN_HALF = D_RNN // 2
ROUTER_LANES = LANES
E_LANE0 = N_GROUPS

PAIRS_PER_GROUP = EXPERTS_PER_GROUP * (EXPERTS_PER_GROUP - 1) // 2
N_BUCKETS = N_GROUPS * PAIRS_PER_GROUP

ROW_TILES = D_MODEL // LANES

TM = 512
TD = 512
TMX = 512
SCAN_CHUNK = 128
VMEM_LIMIT = 56 * 1024 * 1024

F32 = jnp.float32
BF16 = jnp.bfloat16


def _params(sem):
    return pltpu.CompilerParams(dimension_semantics=sem, vmem_limit_bytes=VMEM_LIMIT)


def _rms(x):
    return x * lax.rsqrt(jnp.mean(x * x, axis=-1, keepdims=True) + EPS)


def _sigmoid(x):
    return 0.5 * jnp.tanh(0.5 * x) + 0.5


def _mod_kernel(cond_ref, w_ref, b_ref, o_ref):
    c = cond_ref[...]
    s = c * _sigmoid(c)
    o_ref[...] = jnp.dot(s.astype(BF16), w_ref[...].astype(BF16),
                         preferred_element_type=F32) + b_ref[...]


def _modulation(cond, w_mod, b_mod):
    n = w_mod.shape[1]
    bn = 1024
    return pl.pallas_call(
        _mod_kernel,
        out_shape=jax.ShapeDtypeStruct((cond.shape[0], n), F32),
        grid=(n // bn,),
        in_specs=[pl.BlockSpec(cond.shape, lambda j: (0, 0)),
                  pl.BlockSpec((D_MODEL, bn), lambda j: (0, j)),
                  pl.BlockSpec((1, bn), lambda j: (0, j))],
        out_specs=pl.BlockSpec((cond.shape[0], bn), lambda j: (0, j)),
        compiler_params=_params(("arbitrary",)),
        name="modulation",
    )(cond, w_mod, b_mod.reshape(1, n))


def _pos_kernel(o_ref):
    n_freq = D_MODEL // 4
    k = lax.broadcasted_iota(jnp.int32, (GRID_W, n_freq), 1).astype(F32)
    p = lax.broadcasted_iota(jnp.int32, (GRID_W, n_freq), 0).astype(F32)
    freq = jnp.exp(-math.log(POS_BASE) * k / n_freq)
    ang = p * freq
    o_ref[:, 0:n_freq] = jnp.sin(ang)
    o_ref[:, n_freq:2 * n_freq] = jnp.cos(ang)


def _pos_table():
    return pl.pallas_call(
        _pos_kernel,
        out_shape=jax.ShapeDtypeStruct((GRID_W, D_MODEL // 2), F32),
        name="pos_table",
    )()


def _tile_pos(rows_ref, cols_ref):
    reps = TM // GRID_W
    rpart = jnp.concatenate(
        [jnp.broadcast_to(rows_ref[q:q + 1, :], (GRID_W, D_MODEL // 2)) for q in range(reps)], axis=0)
    cpart = jnp.concatenate([cols_ref[...]] * reps, axis=0)
    return rpart, cpart


def _load_x(x_ref, pos_refs):
    x = x_ref[...]
    if pos_refs is None:
        return x
    rpart, cpart = _tile_pos(*pos_refs)
    return jnp.concatenate([x[:, :D_MODEL // 2] + rpart, x[:, D_MODEL // 2:] + cpart], axis=1)


def _premix_kernel(*refs, add_pos):
    if add_pos:
        (x_ref, rows_ref, cols_ref, mod_ref, g_ref, win_ref, sgug_ref, sguw_ref, sgub_ref,
         xr_ref, gg_ref, ys_ref) = refs
        pos_refs = (rows_ref, cols_ref)
    else:
        (x_ref, mod_ref, g_ref, win_ref, sgug_ref, sguw_ref, sgub_ref,
         xr_ref, gg_ref, ys_ref) = refs
        pos_refs = None
    x = _load_x(x_ref, pos_refs)
    shift = mod_ref[0, :, 0:D_MODEL]
    scale = mod_ref[0, :, D_MODEL:2 * D_MODEL]
    hn = _rms(x) * (g_ref[...] * (1.0 + scale)) + shift
    z = jnp.dot(hn.astype(BF16), win_ref[...], preferred_element_type=F32)
    xr_ref[...] = z[:, 0:D_RNN]
    gg_ref[...] = jax.nn.gelu(z[:, D_RNN:2 * D_RNN])
    u = z[:, 2 * D_RNN:2 * D_RNN + D_SGU]
    v = z[:, 2 * D_RNN + D_SGU:]
    vn = (_rms(v) * sgug_ref[...]).astype(BF16)
    half = D_SGU // 2
    heads_per_half = N_HEADS_SGU // 2
    lane_head = lax.broadcasted_iota(jnp.int32, (CHUNK, half), 1) // HEAD_SGU
    for c in range(TM // CHUNK):
        rows = slice(c * CHUNK, (c + 1) * CHUNK)
        parts = []
        for hf in range(2):
            r = jnp.dot(sguw_ref[hf], vn[rows, hf * half:(hf + 1) * half],
                        preferred_element_type=F32)
            s = jnp.zeros((CHUNK, half), F32)
            for h in range(heads_per_half):
                s = jnp.where(lane_head == h, r[h * CHUNK:(h + 1) * CHUNK], s)
            parts.append(s)
        s = jnp.concatenate(parts, axis=1) + sgub_ref[...]
        ys_ref[rows, :] = (u[rows] * s).astype(BF16)


def _premix(x, mod3, cond_of_tile, g_pre, w_in_b, sgu_g, sgu_w_b, sgu_bias_tile, pos_tab):
    n_tok = x.shape[0]
    n_tiles = n_tok // TM
    add_pos = pos_tab is not None
    tok = lambda i: (i, 0)
    const2 = lambda i: (0, 0)
    in_specs = [pl.BlockSpec((TM, D_MODEL), tok)]
    args = [x]
    if add_pos:
        reps = TM // GRID_W
        tiles_per_seq = GRID_W // reps
        in_specs += [pl.BlockSpec((None, reps, D_MODEL // 2), lambda i: (i % tiles_per_seq, 0, 0)),
                     pl.BlockSpec((GRID_W, D_MODEL // 2), const2)]
        args += [pos_tab.reshape(tiles_per_seq, reps, D_MODEL // 2), pos_tab]
    in_specs += [pl.BlockSpec((1, 1, 6 * D_MODEL), lambda i: (cond_of_tile(i), 0, 0)),
                 pl.BlockSpec((1, D_MODEL), const2),
                 pl.BlockSpec((D_MODEL, 2 * D_RNN + 2 * D_SGU), const2),
                 pl.BlockSpec((1, D_SGU), const2),
                 pl.BlockSpec((2, 4 * CHUNK, CHUNK), lambda i: (0, 0, 0)),
                 pl.BlockSpec((CHUNK, D_SGU), const2)]
    args += [mod3, g_pre, w_in_b, sgu_g, sgu_w_b, sgu_bias_tile]
    return pl.pallas_call(
        functools.partial(_premix_kernel, add_pos=add_pos),
        out_shape=(jax.ShapeDtypeStruct((n_tok, D_RNN), F32),
                   jax.ShapeDtypeStruct((n_tok, D_RNN), F32),
                   jax.ShapeDtypeStruct((n_tok, D_SGU), BF16)),
        grid=(n_tiles,),
        in_specs=in_specs,
        out_specs=(pl.BlockSpec((TM, D_RNN), tok),
                   pl.BlockSpec((TM, D_RNN), tok),
                   pl.BlockSpec((TM, D_SGU), tok)),
        compiler_params=_params(("parallel",)),
        name="premix",
    )(*args)


def _block_scan(a, b, h, reverse):
    n, w = a.shape
    nblk = n // SUBLANES
    a = a.reshape(nblk, SUBLANES, w)
    b = b.reshape(nblk, SUBLANES, w)
    sub = lax.broadcasted_iota(jnp.int32, (1, SUBLANES, w), 1)
    for s in (1, 2, 4):
        if reverse:
            keep = sub < SUBLANES - s
            shift = SUBLANES - s
        else:
            keep = sub >= s
            shift = s
        a_sh = jnp.where(keep, pltpu.roll(a, shift, 1), 1.0)
        b_sh = jnp.where(keep, pltpu.roll(b, shift, 1), 0.0)
        b = a * b_sh + b
        a = a * a_sh
    out = [None] * nblk
    order = range(nblk - 1, -1, -1) if reverse else range(nblk)
    edge = 0 if reverse else SUBLANES - 1
    for j in order:
        hj = b[j] + a[j] * h
        out[j] = hj
        h = hj[edge:edge + 1, :]
    return jnp.concatenate(out, axis=0), h


def _rglru_kernel(xr_ref, gg_ref, h0_ref, convw_ref, convb_ref, wg_ref, bg_ref, lam_ref,
                  y_ref, fs_ref, hf_ref, *, nb, seq_len):
    n_chunks = seq_len // SCAN_CHUNK
    pad = SUBLANES
    win_rows = SCAN_CHUNK + 2 * pad

    def gates(s, c, d):
        t0 = pl.multiple_of(c * SCAN_CHUNK, SCAN_CHUNK)
        before = pl.multiple_of(jnp.maximum(t0 - pad, 0), pad)
        after = pl.multiple_of(jnp.minimum(t0 + SCAN_CHUNK, seq_len - pad), pad)
        head = jnp.where(c > 0, xr_ref[s, pl.ds(before, pad), :], 0.0)
        tail = jnp.where(c < n_chunks - 1, xr_ref[s, pl.ds(after, pad), :], 0.0)
        win = jnp.concatenate([head, xr_ref[s, pl.ds(t0, SCAN_CHUNK), :], tail], axis=0)
        taps = [pltpu.roll(win, 2, 0), pltpu.roll(win, 1, 0), win, pltpu.roll(win, win_rows - 1, 0)]
        xc = convb_ref[...]
        for k in range(4):
            xc = xc + taps[k][pad:pad + SCAN_CHUNK] * convw_ref[k:k + 1, :]
        g = jnp.dot(xc.astype(BF16), wg_ref[d], preferred_element_type=F32) + bg_ref[d]
        r = _sigmoid(g[:, :RNN_HALF])
        i = _sigmoid(g[:, RNN_HALF:])
        neg_lam = -lam_ref[d]
        softplus = jnp.maximum(neg_lam, 0.0) + jnp.log(1.0 + jnp.exp(-jnp.abs(neg_lam)))
        log_a = r * (-RG_C * softplus)
        a = jnp.exp(log_a)
        q = jnp.tanh(-log_a) * (a * a + 1.0)
        b = jnp.where(q > 0.0, q * lax.rsqrt(q), 0.0) * (i * xc)
        return t0, a, b

    def per_seq(s, carry):
        h0 = h0_ref[s]

        def fwd(c, h):
            t0, a, b = gates(s, c, 0)
            hc, h = _block_scan(a, b, h, reverse=False)
            hf_ref[pl.ds(t0, SCAN_CHUNK), :] = hc
            return h

        hf_last = lax.fori_loop(0, n_chunks, fwd, h0[0:1, :])

        def bwd(k, h):
            c = n_chunks - 1 - k
            t0, a, b = gates(s, c, 1)
            hc, h = _block_scan(a, b, h, reverse=True)
            y = (hf_ref[pl.ds(t0, SCAN_CHUNK), :] + hc) * gg_ref[s, pl.ds(t0, SCAN_CHUNK), :]
            y_ref[s, pl.ds(t0, SCAN_CHUNK), :] = y.astype(BF16)
            return h

        hb_first = lax.fori_loop(0, n_chunks, bwd, h0[1:2, :])
        fs_ref[s] = jnp.concatenate([hf_last, hb_first], axis=0)
        return carry

    lax.fori_loop(0, nb, per_seq, 0)


def _rglru(xr, gg, h0, conv_w, conv_b, w_gates, b_gates, lam, nb):
    n_seq, seq_len, _ = xr.shape
    blk = lambda i, j: (i, 0, j)
    chan = lambda i, j: (0, j)
    return pl.pallas_call(
        functools.partial(_rglru_kernel, nb=nb, seq_len=seq_len),
        out_shape=(jax.ShapeDtypeStruct((n_seq, seq_len, D_RNN), BF16),
                   jax.ShapeDtypeStruct((n_seq, 2, D_RNN), F32)),
        grid=(n_seq // nb, D_RNN // RNN_HALF),
        in_specs=[pl.BlockSpec((nb, seq_len, RNN_HALF), blk),
                  pl.BlockSpec((nb, seq_len, RNN_HALF), blk),
                  pl.BlockSpec((nb, 2, RNN_HALF), blk),
                  pl.BlockSpec((4, RNN_HALF), chan),
                  pl.BlockSpec((1, RNN_HALF), chan),
                  pl.BlockSpec((None, 2, RNN_HALF, 2 * RNN_HALF), lambda i, j: (j, 0, 0, 0)),
                  pl.BlockSpec((None, 2, 1, 2 * RNN_HALF), lambda i, j: (j, 0, 0, 0)),
                  pl.BlockSpec((2, 1, RNN_HALF), lambda i, j: (0, 0, j))],
        out_specs=(pl.BlockSpec((nb, seq_len, RNN_HALF), blk),
                   pl.BlockSpec((nb, 2, RNN_HALF), blk)),
        scratch_shapes=[pltpu.VMEM((seq_len, RNN_HALF), F32)],
        compiler_params=_params(("parallel", "parallel")),
        name="rglru",
    )(xr, gg, h0, conv_w, conv_b, w_gates, b_gates, lam)


def _route(logits):
    lane = lax.broadcasted_iota(jnp.int32, logits.shape, 1)
    neg = jnp.float32(-jnp.inf)
    gmask = lane < N_GROUPS
    gl = jnp.where(gmask, logits, neg)
    gmax = jnp.max(gl, axis=-1, keepdims=True)
    g_idx = jnp.min(jnp.where(gl == gmax, lane, ROUTER_LANES), axis=-1, keepdims=True)
    e0 = E_LANE0 + EXPERTS_PER_GROUP * g_idx
    emask = (lane >= e0) & (lane < e0 + EXPERTS_PER_GROUP)
    el = jnp.where(emask, logits, neg)
    l1 = jnp.max(el, axis=-1, keepdims=True)
    i1 = jnp.min(jnp.where(el == l1, lane, ROUTER_LANES), axis=-1, keepdims=True)
    el2 = jnp.where(lane == i1, neg, el)
    l2 = jnp.max(el2, axis=-1, keepdims=True)
    i2 = jnp.min(jnp.where(emask & (lane != i1) & (el2 == l2), lane, ROUTER_LANES), axis=-1, keepdims=True)
    ja = jnp.minimum(i1, i2) - e0
    jb = jnp.maximum(i1, i2) - e0
    pair = (ja * (2 * EXPERTS_PER_GROUP - 1 - ja)) // 2 + (jb - ja - 1)
    return g_idx * PAIRS_PER_GROUP + pair


def _store_token_major(ref, x):
    n = x.shape[0]
    for k in range(ROW_TILES):
        ref[pl.ds(k, n, stride=ROW_TILES), :] = x[:, k * LANES:(k + 1) * LANES]


def _load_token_major(ref, n):
    return jnp.concatenate([ref[pl.ds(k, n, stride=ROW_TILES), :] for k in range(ROW_TILES)], axis=1)


def _postmix_kernel(*refs, add_pos):
    refs = list(refs)
    x_ref = refs.pop(0)
    pos_refs = (refs.pop(0), refs.pop(0)) if add_pos else None
    (yr_ref, ys_ref, mod_ref, gpost_ref, gpre_ref, wout_ref, rw_ref, rb_ref, cnt0_ref,
     x1_ref, hn_ref, rt_ref, cnt_ref, run_ref) = refs

    @pl.when(pl.program_id(0) == 0)
    def _():
        run_ref[...] = cnt0_ref[...]

    x = _load_x(x_ref, pos_refs)
    gate1 = mod_ref[0, :, 2 * D_MODEL:3 * D_MODEL]
    shift2 = mod_ref[0, :, 3 * D_MODEL:4 * D_MODEL]
    scale2 = mod_ref[0, :, 4 * D_MODEL:5 * D_MODEL]
    y = (jnp.dot(yr_ref[...], wout_ref[0:D_RNN, :], preferred_element_type=F32)
         + jnp.dot(ys_ref[...], wout_ref[D_RNN:, :], preferred_element_type=F32))
    x1 = x + _rms(y) * (gate1 * gpost_ref[...])
    x1_ref[...] = x1
    hn = _rms(x1) * (gpre_ref[...] * (1.0 + scale2)) + shift2
    _store_token_major(hn_ref, hn)
    logits = jnp.dot(hn.astype(BF16), rw_ref[...], preferred_element_type=F32) + rb_ref[...]
    bucket = _route(logits)
    lane = lax.broadcasted_iota(jnp.int32, (TM, ROUTER_LANES), 1)
    onehot = lane == bucket
    r_i = lax.broadcasted_iota(jnp.int32, (TM, TM), 0)
    c_i = lax.broadcasted_iota(jnp.int32, (TM, TM), 1)
    earlier = (c_i < r_i).astype(BF16)
    before = jnp.dot(earlier, onehot.astype(BF16), preferred_element_type=F32) + run_ref[...]
    rank = jnp.sum(jnp.where(onehot, before, 0.0), axis=-1, keepdims=True).astype(jnp.int32)
    rt_ref[...] = jnp.where(lane == 0, bucket, jnp.where(lane == 1, rank, 0))
    run_ref[...] += jnp.sum(onehot.astype(F32), axis=0, keepdims=True)
    cnt_ref[...] = run_ref[...]


def _postmix(x, y_rnn, y_sgu, mod3, cond_of_tile, g_post, g_pre, w_out_b, router_w, router_b, pos_tab,
             counts0):
    n_tok = x.shape[0]
    add_pos = pos_tab is not None
    tok = lambda i: (i, 0)
    const2 = lambda i: (0, 0)
    in_specs = [pl.BlockSpec((TM, D_MODEL), tok)]
    args = [x]
    if add_pos:
        reps = TM // GRID_W
        tiles_per_seq = GRID_W // reps
        in_specs += [pl.BlockSpec((None, reps, D_MODEL // 2), lambda i: (i % tiles_per_seq, 0, 0)),
                     pl.BlockSpec((GRID_W, D_MODEL // 2), const2)]
        args += [pos_tab.reshape(tiles_per_seq, reps, D_MODEL // 2), pos_tab]
    in_specs += [pl.BlockSpec((TM, D_RNN), tok),
                 pl.BlockSpec((TM, D_SGU), tok),
                 pl.BlockSpec((1, 1, 6 * D_MODEL), lambda i: (cond_of_tile(i), 0, 0)),
                 pl.BlockSpec((1, D_MODEL), const2),
                 pl.BlockSpec((1, D_MODEL), const2),
                 pl.BlockSpec((D_MODEL, D_MODEL), const2),
                 pl.BlockSpec((D_MODEL, ROUTER_LANES), const2),
                 pl.BlockSpec((1, ROUTER_LANES), const2),
                 pl.BlockSpec((1, ROUTER_LANES), const2)]
    args += [y_rnn, y_sgu, mod3, g_post, g_pre, w_out_b, router_w, router_b, counts0]
    return pl.pallas_call(
        functools.partial(_postmix_kernel, add_pos=add_pos),
        out_shape=(jax.ShapeDtypeStruct((n_tok, D_MODEL), F32),
                   jax.ShapeDtypeStruct((n_tok * ROW_TILES, LANES), F32),
                   jax.ShapeDtypeStruct((n_tok, ROUTER_LANES), jnp.int32),
                   jax.ShapeDtypeStruct((1, ROUTER_LANES), F32)),
        grid=(n_tok // TM,),
        in_specs=in_specs,
        out_specs=(pl.BlockSpec((TM, D_MODEL), tok),
                   pl.BlockSpec((TM * ROW_TILES, LANES), tok),
                   pl.BlockSpec((TM, ROUTER_LANES), tok),
                   pl.BlockSpec((1, ROUTER_LANES), const2)),
        scratch_shapes=[pltpu.VMEM((1, ROUTER_LANES), F32)],
        compiler_params=_params(("arbitrary",)),
        name="postmix",
    )(*args)


def _token_rows(ref, t):
    return ref.at[pl.ds(pl.multiple_of(t * ROW_TILES, ROW_TILES), ROW_TILES), :]


def _dispatch_kernel(dest_ref, hc_ref, hs_ref, xs_ref, sem, *, n_ctx_steps):
    i = pl.program_id(0)
    base = i * TD

    def scatter(src_ref):
        def start(r, carry):
            pltpu.make_async_copy(_token_rows(src_ref, r), _token_rows(xs_ref, dest_ref[base + r]), sem).start()
            return carry

        lax.fori_loop(0, TD, start, 0, unroll=8)
        pltpu.make_async_copy(src_ref, xs_ref.at[pl.ds(0, TD * ROW_TILES), :], sem).wait()

    @pl.when(i < n_ctx_steps)
    def _():
        scatter(hc_ref)

    @pl.when(i >= n_ctx_steps)
    def _():
        scatter(hs_ref)


def _dispatch(dest, hn_ctx, hn_dec, n_slots):
    n_ctx_steps = hn_ctx.shape[0] // (TD * ROW_TILES)
    n_dec_steps = hn_dec.shape[0] // (TD * ROW_TILES)
    return pl.pallas_call(
        functools.partial(_dispatch_kernel, n_ctx_steps=n_ctx_steps),
        out_shape=jax.ShapeDtypeStruct((n_slots * ROW_TILES, LANES), F32),
        grid_spec=pltpu.PrefetchScalarGridSpec(
            num_scalar_prefetch=1,
            grid=(n_ctx_steps + n_dec_steps,),
            in_specs=[pl.BlockSpec((TD * ROW_TILES, LANES), lambda i, d: (jnp.minimum(i, n_ctx_steps - 1), 0)),
                      pl.BlockSpec((TD * ROW_TILES, LANES), lambda i, d: (jnp.maximum(i - n_ctx_steps, 0), 0))],
            out_specs=pl.BlockSpec(memory_space=pl.ANY),
            scratch_shapes=[pltpu.SemaphoreType.DMA(())]),
        compiler_params=_params(("arbitrary",)),
        name="dispatch",
    )(dest, hn_ctx, hn_dec)


def _experts_kernel(ea_ref, eb_ref, nv_ref, xs_ref, rw_ref, rb_ref,
                    wga_ref, wua_ref, wda_ref, wgb_ref, wub_ref, wdb_ref, gpost_ref, ys_ref):
    i = pl.program_id(0)
    nv = nv_ref[i]

    @pl.when(nv == 0)
    def _():
        ys_ref[...] = jnp.zeros_like(ys_ref)

    @pl.when(nv > 0)
    def _():
        row = lax.broadcasted_iota(jnp.int32, (TMX, 1), 0)
        xb = jnp.where(row < nv, _load_token_major(xs_ref, TMX), 0.0).astype(BF16)
        logits = jnp.dot(xb, rw_ref[...], preferred_element_type=F32) + rb_ref[...]
        lane = lax.broadcasted_iota(jnp.int32, logits.shape, 1)
        ea = ea_ref[i]
        eb = eb_ref[i]
        gmask = lane < N_GROUPS
        gl = jnp.where(gmask, logits, -jnp.inf)
        gmax = jnp.max(gl, axis=-1, keepdims=True)
        gexp = jnp.where(gmask, jnp.exp(gl - gmax), 0.0)
        g_own = jnp.sum(jnp.where(lane == ea // EXPERTS_PER_GROUP, gexp, 0.0), axis=-1, keepdims=True)
        g_w = g_own / jnp.sum(gexp, axis=-1, keepdims=True)
        la = jnp.sum(jnp.where(lane == ea + E_LANE0, logits, 0.0), axis=-1, keepdims=True)
        lb = jnp.sum(jnp.where(lane == eb + E_LANE0, logits, 0.0), axis=-1, keepdims=True)
        m = jnp.maximum(la, lb)
        pa = jnp.exp(la - m)
        pb = jnp.exp(lb - m)
        inv = g_w / (pa + pb)

        def expert(wg_ref, wu_ref, wd_ref, w):
            g = jnp.dot(xb, wg_ref[0], preferred_element_type=F32)
            u = jnp.dot(xb, wu_ref[0], preferred_element_type=F32)
            act = (g * _sigmoid(g)) * u * w
            return jnp.dot(act.astype(BF16), wd_ref[0], preferred_element_type=F32)

        y = expert(wga_ref, wua_ref, wda_ref, pa * inv) + expert(wgb_ref, wub_ref, wdb_ref, pb * inv)
        _store_token_major(ys_ref, _rms(y) * gpost_ref[...])


def _experts(sched, xs, router_w, router_b, wg_b, wu_b, wd_b, g_post):
    ea, eb, nv = sched
    n_tiles = ea.shape[0]
    rows = lambda i, ea, eb, nv: (i, 0)
    const2 = lambda i, ea, eb, nv: (0, 0)
    exp_a = lambda i, ea, eb, nv: (ea[i], 0, 0)
    exp_b = lambda i, ea, eb, nv: (eb[i], 0, 0)
    w_in_spec = lambda m: pl.BlockSpec((1, D_MODEL, D_EXPERT), m)
    w_out_spec = lambda m: pl.BlockSpec((1, D_EXPERT, D_MODEL), m)
    return pl.pallas_call(
        _experts_kernel,
        out_shape=jax.ShapeDtypeStruct(xs.shape, F32),
        grid_spec=pltpu.PrefetchScalarGridSpec(
            num_scalar_prefetch=3,
            grid=(n_tiles,),
            in_specs=[pl.BlockSpec((TMX * ROW_TILES, LANES), rows),
                      pl.BlockSpec((D_MODEL, ROUTER_LANES), const2),
                      pl.BlockSpec((1, ROUTER_LANES), const2),
                      w_in_spec(exp_a), w_in_spec(exp_a), w_out_spec(exp_a),
                      w_in_spec(exp_b), w_in_spec(exp_b), w_out_spec(exp_b),
                      pl.BlockSpec((1, D_MODEL), const2)],
            out_specs=pl.BlockSpec((TMX * ROW_TILES, LANES), rows)),
        compiler_params=_params(("arbitrary",)),
        name="experts",
    )(ea, eb, nv, xs, router_w, router_b, wg_b, wu_b, wd_b, wg_b, wu_b, wd_b, g_post)


def _combine_kernel(dest_ref, ys_ref, x1_ref, mod_ref, o_ref, ybuf, sems, *, tile0):
    i = pl.program_id(0)
    n = pl.num_programs(0)

    def fetch(tile, slot):
        base = (tile + tile0) * TM

        def start(r, carry):
            pltpu.make_async_copy(_token_rows(ys_ref, dest_ref[base + r]), _token_rows(ybuf.at[slot], r),
                                  sems.at[slot]).start()
            return carry

        lax.fori_loop(0, TM, start, 0, unroll=8)

    @pl.when(i == 0)
    def _():
        fetch(0, 0)

    @pl.when(i + 1 < n)
    def _():
        fetch(i + 1, (i + 1) % 2)

    slot = i % 2
    pltpu.make_async_copy(ys_ref.at[pl.ds(0, TM * ROW_TILES), :], ybuf.at[slot], sems.at[slot]).wait()
    gate2 = mod_ref[0, :, 5 * D_MODEL:6 * D_MODEL]
    o_ref[...] = x1_ref[...] + gate2 * _load_token_major(ybuf.at[slot], TM)


def _combine(dest, ys, x1, mod3, cond_of_tile, tile0):
    n_tok = x1.shape[0]
    return pl.pallas_call(
        functools.partial(_combine_kernel, tile0=tile0),
        out_shape=jax.ShapeDtypeStruct((n_tok, D_MODEL), F32),
        grid_spec=pltpu.PrefetchScalarGridSpec(
            num_scalar_prefetch=1,
            grid=(n_tok // TM,),
            in_specs=[pl.BlockSpec(memory_space=pl.ANY),
                      pl.BlockSpec((TM, D_MODEL), lambda i, d: (i, 0)),
                      pl.BlockSpec((1, 1, 6 * D_MODEL), lambda i, d: (cond_of_tile(i), 0, 0))],
            out_specs=pl.BlockSpec((TM, D_MODEL), lambda i, d: (i, 0)),
            scratch_shapes=[pltpu.VMEM((2, TM * ROW_TILES, LANES), F32), pltpu.SemaphoreType.DMA((2,))]),
        compiler_params=_params(("arbitrary",)),
        name="combine",
    )(dest, ys, x1, mod3)


def _schedule(route, counts):
    n_tok = route.shape[0]
    n_max = n_tok // TMX + N_BUCKETS
    bucket = route[:, 0]
    rank = route[:, 1]
    cnt = counts[0, :N_BUCKETS].astype(jnp.int32)
    tiles = (cnt + TMX - 1) // TMX
    tile_end = jnp.cumsum(tiles)
    tile_start = tile_end - tiles
    ids = jnp.arange(N_BUCKETS, dtype=jnp.int32)
    slot0 = jnp.sum(jnp.where(bucket[:, None] == ids[None, :], (tile_start * TMX)[None, :], 0), axis=1)
    dest = slot0 + rank
    i = jnp.arange(n_max, dtype=jnp.int32)
    total = tile_end[-1]
    valid = i < total
    tb = jnp.sum((jnp.minimum(i, total - 1)[:, None] >= tile_end[None, :]).astype(jnp.int32), axis=1)
    pairs = [(a, b) for a in range(EXPERTS_PER_GROUP) for b in range(a + 1, EXPERTS_PER_GROUP)]
    ea_tab = jnp.array([g * EXPERTS_PER_GROUP + a for g in range(N_GROUPS) for a, _ in pairs], jnp.int32)
    eb_tab = jnp.array([g * EXPERTS_PER_GROUP + b for g in range(N_GROUPS) for _, b in pairs], jnp.int32)
    ea = ea_tab[tb]
    eb = eb_tab[tb]
    nv = jnp.where(valid, jnp.clip(cnt[tb] - (i - tile_start[tb]) * TMX, 0, TMX), 0)
    return dest, (ea, eb, nv), n_max * TMX


def _block_diag_gates(rg_wa, rg_wx):
    heads_per_half = N_HEADS_RNN // 2

    def bd(w):
        w = w.reshape(2, 2, heads_per_half, HEAD_RNN, HEAD_RNN)
        eye = jnp.eye(heads_per_half, dtype=w.dtype)
        full = jnp.einsum('dghij,hk->dghikj', w, eye)
        return full.reshape(2, 2, RNN_HALF, RNN_HALF)

    w = jnp.concatenate([bd(rg_wa), bd(rg_wx)], axis=-1)
    return jnp.transpose(w, (1, 0, 2, 3)).astype(BF16)


def kernel(x_prompt, x_sample, state_rglru, c, c_ctx, w_mod, b_mod, g_pre_mix, g_post_mix, g_pre_ffn,
           g_post_ffn, w_in, conv_w, conv_b, rg_wa, rg_ba, rg_wx, rg_bx, rg_lambda, sgu_g, sgu_w, sgu_b,
           w_out, router_g_w, router_g_b, router_e_w, router_e_b, exp_w_gate, exp_w_up, exp_w_down):
    assert w_mod.shape[0] == 1, "single-layer trunk"
    n_ctx, ctx_len, _ = x_prompt.shape
    n_dec, dec_len, _ = x_sample.shape
    l = 0

    n_cond = SUBLANES
    cond = jnp.zeros((n_cond, D_MODEL), F32).at[0].set(c_ctx).at[1:1 + n_dec].set(c)
    mod3 = _modulation(cond, w_mod[l], b_mod[l]).reshape(n_cond, 1, 6 * D_MODEL)
    pos_tab = _pos_table()

    w_in_b = w_in[l].astype(BF16)
    w_out_b = w_out[l].astype(BF16)
    sgu_w_b = sgu_w[l].reshape(2, 4 * CHUNK, CHUNK).astype(BF16)
    sgu_bias_tile = jnp.repeat(sgu_b[l].T, HEAD_SGU, axis=1)
    w_gates = _block_diag_gates(rg_wa[l], rg_wx[l])
    b_gates = jnp.concatenate([rg_ba[l].reshape(2, 2, 1, RNN_HALF), rg_bx[l].reshape(2, 2, 1, RNN_HALF)],
                              axis=-1)
    b_gates = jnp.transpose(b_gates, (1, 0, 2, 3))
    lam = rg_lambda[l].reshape(2, 1, D_RNN)
    router_w = jnp.zeros((D_MODEL, ROUTER_LANES), F32)
    router_w = router_w.at[:, :N_GROUPS].set(router_g_w[l]).at[:, E_LANE0:E_LANE0 + N_EXPERTS].set(router_e_w[l])
    router_w = router_w.astype(BF16)
    router_b = jnp.zeros((1, ROUTER_LANES), F32)
    router_b = router_b.at[0, :N_GROUPS].set(router_g_b[l]).at[0, E_LANE0:E_LANE0 + N_EXPERTS].set(router_e_b[l])
    wg_b = exp_w_gate[l].astype(BF16)
    wu_b = exp_w_up[l].astype(BF16)
    wd_b = exp_w_down[l].astype(BF16)
    row = lambda v: v.reshape(1, -1)

    n_ctx_tok = n_ctx * ctx_len

    def mixer(x, h0, cond_of_tile, seqs_per_step, use_pos, counts0):
        n_seq, seq_len, _ = x.shape
        xf = x.reshape(n_seq * seq_len, D_MODEL)
        tab = pos_tab if use_pos else None
        xr, gg, y_sgu = _premix(xf, mod3, cond_of_tile, row(g_pre_mix[l]), w_in_b, row(sgu_g[l]),
                                sgu_w_b, sgu_bias_tile, tab)
        y_rnn, fstate = _rglru(xr.reshape(n_seq, seq_len, D_RNN), gg.reshape(n_seq, seq_len, D_RNN), h0,
                               conv_w[l], row(conv_b[l]), w_gates, b_gates, lam, seqs_per_step)
        x1, hn, route, counts = _postmix(
            xf, y_rnn.reshape(n_seq * seq_len, D_RNN), y_sgu, mod3, cond_of_tile, row(g_post_mix[l]),
            row(g_pre_ffn[l]), w_out_b, router_w, router_b, tab, counts0)
        return x1, hn, route, counts, fstate

    tiles_per_seq = dec_len // TM
    ctx_cond = lambda i: 0
    dec_cond = lambda i: 1 + i // tiles_per_seq
    h0_ctx = jnp.zeros((n_ctx, 2, D_RNN), F32)
    counts0 = jnp.zeros((1, ROUTER_LANES), F32)
    x1_ctx, hn_ctx, route_ctx, counts, st = mixer(x_prompt, h0_ctx, ctx_cond, 8, False, counts0)
    new_state = st.astype(state_rglru.dtype)[:, None]
    x1_dec, hn_dec, route_dec, counts, _ = mixer(x_sample, state_rglru[:, l].astype(F32), dec_cond, 1, True,
                                                 counts)

    route = jnp.concatenate([route_ctx[:, :2], route_dec[:, :2]], axis=0)
    dest, sched, n_slots = _schedule(route, counts)
    xs = _dispatch(dest, hn_ctx, hn_dec, n_slots)
    ys = _experts(sched, xs, router_w, router_b, wg_b, wu_b, wd_b, row(g_post_ffn[l]))
    y_prompt = _combine(dest, ys, x1_ctx, mod3, ctx_cond, 0)
    y_sample = _combine(dest, ys, x1_dec, mod3, dec_cond, n_ctx_tok // TM)
    return (y_prompt.reshape(x_prompt.shape), y_sample.reshape(x_sample.shape), new_state)
```

```python
import functools
import math

import jax
import jax.numpy as jnp
from jax import lax
from jax.experimental import pallas as pl
from jax.experimental.pallas import tpu as pltpu

D_MODEL = 1024
D_RNN = 512
D_SGU = 512
N_HEADS_RNN = 8
HEAD_RNN = D_RNN // N_HEADS_RNN
N_HEADS_SGU = 8
HEAD_SGU = D_SGU // N_HEADS_SGU
CHUNK = 128
GRID_W = 64
RG_C = 8.0
N_GROUPS = 4
EXPERTS_PER_GROUP = 4
N_EXPERTS = N_GROUPS * EXPERTS_PER_GROUP
D_EXPERT = 512
EPS = 1e-6
POS_BASE = 10000.0

LANES = 128
SUBLANES = 8
RNN_HALF = D_RNN // 2
ROUTER_LANES = LANES
E_LANE0 = N_GROUPS

PAIRS_PER_GROUP = EXPERTS_PER_GROUP * (EXPERTS_PER_GROUP - 1) // 2
N_BUCKETS = N_GROUPS * PAIRS_PER_GROUP

ROW_TILES = D_MODEL // LANES

TM = 512
TD = 512
EXPERT_ROW_PARTS = 2
TILE_ROW_PARTS = 1
TMX = 512
SCAN_CHUNK = 128
VMEM_LIMIT = 56 * 1024 * 1024

F32 = jnp.float32
BF16 = jnp.bfloat16


def _params(sem):
    return pltpu.CompilerParams(dimension_semantics=sem, vmem_limit_bytes=VMEM_LIMIT)


def _rms(x):
    return x * lax.rsqrt(jnp.mean(x * x, axis=-1, keepdims=True) + EPS)


def _sigmoid(x):
    return 0.5 * jnp.tanh(0.5 * x) + 0.5


def _mod_kernel(cond_ref, w_ref, b_ref, o_ref):
    c = cond_ref[...]
    s = c * _sigmoid(c)
    o_ref[...] = jnp.dot(s.astype(BF16), w_ref[...].astype(BF16),
                         preferred_element_type=F32) + b_ref[...]


def _modulation(cond, w_mod, b_mod):
    n = w_mod.shape[1]
    bn = 1024
    return pl.pallas_call(
        _mod_kernel,
        out_shape=jax.ShapeDtypeStruct((cond.shape[0], n), F32),
        grid=(n // bn,),
        in_specs=[pl.BlockSpec(cond.shape, lambda j: (0, 0)),
                  pl.BlockSpec((D_MODEL, bn), lambda j: (0, j)),
                  pl.BlockSpec((1, bn), lambda j: (0, j))],
        out_specs=pl.BlockSpec((cond.shape[0], bn), lambda j: (0, j)),
        compiler_params=_params(("arbitrary",)),
        name="modulation",
    )(cond, w_mod, b_mod.reshape(1, n))


def _pos_kernel(o_ref):
    n_freq = D_MODEL // 4
    k = lax.broadcasted_iota(jnp.int32, (GRID_W, n_freq), 1).astype(F32)
    p = lax.broadcasted_iota(jnp.int32, (GRID_W, n_freq), 0).astype(F32)
    freq = jnp.exp(-math.log(POS_BASE) * k / n_freq)
    ang = p * freq
    o_ref[:, 0:n_freq] = jnp.sin(ang)
    o_ref[:, n_freq:2 * n_freq] = jnp.cos(ang)


def _pos_table():
    return pl.pallas_call(
        _pos_kernel,
        out_shape=jax.ShapeDtypeStruct((GRID_W, D_MODEL // 2), F32),
        name="pos_table",
    )()


def _load_x(x_ref, pos_refs, r0, n):
    x = x_ref[r0:r0 + n, :]
    if pos_refs is None:
        return x
    rows_ref, cols_ref = pos_refs
    q0, reps = r0 // GRID_W, n // GRID_W
    rpart = jnp.concatenate(
        [jnp.broadcast_to(rows_ref[q:q + 1, :], (GRID_W, D_MODEL // 2)) for q in range(q0, q0 + reps)], axis=0)
    cpart = jnp.concatenate([cols_ref[...]] * reps, axis=0)
    return jnp.concatenate([x[:, :D_MODEL // 2] + rpart, x[:, D_MODEL // 2:] + cpart], axis=1)


def _premix_kernel(*refs, add_pos):
    if add_pos:
        (x_ref, rows_ref, cols_ref, mod_ref, g_ref, win_ref, sgug_ref, sguw_ref, sgub_ref,
         xr_ref, gg_ref, ys_ref) = refs
        pos_refs = (rows_ref, cols_ref)
    else:
        (x_ref, mod_ref, g_ref, win_ref, sgug_ref, sguw_ref, sgub_ref,
         xr_ref, gg_ref, ys_ref) = refs
        pos_refs = None
    shift = mod_ref[0, :, 0:D_MODEL]
    scale = mod_ref[0, :, D_MODEL:2 * D_MODEL]
    gain = g_ref[...] * (1.0 + scale)
    half = D_SGU // 2
    heads_per_half = N_HEADS_SGU // 2
    lane_head = lax.broadcasted_iota(jnp.int32, (CHUNK, half), 1) // HEAD_SGU
    part = TM // TILE_ROW_PARTS
    for p in range(TILE_ROW_PARTS):
        r0 = p * part
        hn = _rms(_load_x(x_ref, pos_refs, r0, part)) * gain + shift
        z = jnp.dot(hn.astype(BF16), win_ref[...], preferred_element_type=F32)
        xr_ref[r0:r0 + part, :] = z[:, 0:D_RNN]
        gg_ref[r0:r0 + part, :] = jax.nn.gelu(z[:, D_RNN:2 * D_RNN])
        u = z[:, 2 * D_RNN:2 * D_RNN + D_SGU]
        v = z[:, 2 * D_RNN + D_SGU:]
        vn = (_rms(v) * sgug_ref[...]).astype(BF16)
        for c in range(part // CHUNK):
            rows = slice(c * CHUNK, (c + 1) * CHUNK)
            halves = []
            for hf in range(2):
                r = jnp.dot(sguw_ref[hf], vn[rows, hf * half:(hf + 1) * half],
                            preferred_element_type=F32)
                s = jnp.zeros((CHUNK, half), F32)
                for h in range(heads_per_half):
                    s = jnp.where(lane_head == h, r[h * CHUNK:(h + 1) * CHUNK], s)
                halves.append(s)
            s = jnp.concatenate(halves, axis=1) + sgub_ref[...]
            ys_ref[r0 + c * CHUNK:r0 + (c + 1) * CHUNK, :] = (u[rows] * s).astype(BF16)


def _premix(x, mod3, cond_of_tile, g_pre, w_in_b, sgu_g, sgu_w_b, sgu_bias_tile, pos_tab):
    n_tok = x.shape[0]
    n_tiles = n_tok // TM
    add_pos = pos_tab is not None
    tok = lambda i: (i, 0)
    const2 = lambda i: (0, 0)
    in_specs = [pl.BlockSpec((TM, D_MODEL), tok)]
    args = [x]
    if add_pos:
        reps = TM // GRID_W
        tiles_per_seq = GRID_W // reps
        in_specs += [pl.BlockSpec((None, reps, D_MODEL // 2), lambda i: (i % tiles_per_seq, 0, 0)),
                     pl.BlockSpec((GRID_W, D_MODEL // 2), const2)]
        args += [pos_tab.reshape(tiles_per_seq, reps, D_MODEL // 2), pos_tab]
    in_specs += [pl.BlockSpec((1, 1, 6 * D_MODEL), lambda i: (cond_of_tile(i), 0, 0)),
                 pl.BlockSpec((1, D_MODEL), const2),
                 pl.BlockSpec((D_MODEL, 2 * D_RNN + 2 * D_SGU), const2),
                 pl.BlockSpec((1, D_SGU), const2),
                 pl.BlockSpec((2, 4 * CHUNK, CHUNK), lambda i: (0, 0, 0)),
                 pl.BlockSpec((CHUNK, D_SGU), const2)]
    args += [mod3, g_pre, w_in_b, sgu_g, sgu_w_b, sgu_bias_tile]
    return pl.pallas_call(
        functools.partial(_premix_kernel, add_pos=add_pos),
        out_shape=(jax.ShapeDtypeStruct((n_tok, D_RNN), F32),
                   jax.ShapeDtypeStruct((n_tok, D_RNN), F32),
                   jax.ShapeDtypeStruct((n_tok, D_SGU), BF16)),
        grid=(n_tiles,),
        in_specs=in_specs,
        out_specs=(pl.BlockSpec((TM, D_RNN), tok),
                   pl.BlockSpec((TM, D_RNN), tok),
                   pl.BlockSpec((TM, D_SGU), tok)),
        compiler_params=_params(("parallel",)),
        name="premix",
    )(*args)


def _block_scan(a, b, h, reverse):
    n, w = a.shape
    nblk = n // SUBLANES
    a = a.reshape(nblk, SUBLANES, w)
    b = b.reshape(nblk, SUBLANES, w)
    sub = lax.broadcasted_iota(jnp.int32, (1, SUBLANES, w), 1)
    for s in (1, 2, 4):
        if reverse:
            keep = sub < SUBLANES - s
            shift = SUBLANES - s
        else:
            keep = sub >= s
            shift = s
        a_sh = jnp.where(keep, pltpu.roll(a, shift, 1), 1.0)
        b_sh = jnp.where(keep, pltpu.roll(b, shift, 1), 0.0)
        b = a * b_sh + b
        a = a * a_sh
    out = [None] * nblk
    order = range(nblk - 1, -1, -1) if reverse else range(nblk)
    edge = 0 if reverse else SUBLANES - 1
    for j in order:
        hj = b[j] + a[j] * h
        out[j] = hj
        h = hj[edge:edge + 1, :]
    return jnp.concatenate(out, axis=0), h


def _rglru_kernel(xr_ref, gg_ref, h0_ref, convw_ref, convb_ref, wg_ref, bg_ref, lam_ref,
                  y_ref, fs_ref, hf_ref, *, nb, seq_len):
    n_chunks = seq_len // SCAN_CHUNK
    pad = SUBLANES
    win_rows = SCAN_CHUNK + 2 * pad

    def gates(s, c, d):
        t0 = pl.multiple_of(c * SCAN_CHUNK, SCAN_CHUNK)
        before = pl.multiple_of(jnp.maximum(t0 - pad, 0), pad)
        after = pl.multiple_of(jnp.minimum(t0 + SCAN_CHUNK, seq_len - pad), pad)
        head = jnp.where(c > 0, xr_ref[s, pl.ds(before, pad), :], 0.0)
        tail = jnp.where(c < n_chunks - 1, xr_ref[s, pl.ds(after, pad), :], 0.0)
        win = jnp.concatenate([head, xr_ref[s, pl.ds(t0, SCAN_CHUNK), :], tail], axis=0)
        taps = [pltpu.roll(win, 2, 0), pltpu.roll(win, 1, 0), win, pltpu.roll(win, win_rows - 1, 0)]
        xc = convb_ref[...]
        for k in range(4):
            xc = xc + taps[k][pad:pad + SCAN_CHUNK] * convw_ref[k:k + 1, :]
        g = jnp.dot(xc.astype(BF16), wg_ref[d], preferred_element_type=F32) + bg_ref[d]
        r = _sigmoid(g[:, :RNN_HALF])
        i = _sigmoid(g[:, RNN_HALF:])
        neg_lam = -lam_ref[d]
        softplus = jnp.maximum(neg_lam, 0.0) + jnp.log(1.0 + jnp.exp(-jnp.abs(neg_lam)))
        log_a = r * (-RG_C * softplus)
        a = jnp.exp(log_a)
        q = jnp.tanh(-log_a) * (a * a + 1.0)
        b = jnp.where(q > 0.0, q * lax.rsqrt(q), 0.0) * (i * xc)
        return t0, a, b

    def per_seq(s, carry):
        h0 = h0_ref[s]

        def fwd(c, h):
            t0, a, b = gates(s, c, 0)
            hc, h = _block_scan(a, b, h, reverse=False)
            hf_ref[pl.ds(t0, SCAN_CHUNK), :] = hc
            return h

        hf_last = lax.fori_loop(0, n_chunks, fwd, h0[0:1, :])

        def bwd(k, h):
            c = n_chunks - 1 - k
            t0, a, b = gates(s, c, 1)
            hc, h = _block_scan(a, b, h, reverse=True)
            y = (hf_ref[pl.ds(t0, SCAN_CHUNK), :] + hc) * gg_ref[s, pl.ds(t0, SCAN_CHUNK), :]
            y_ref[s, pl.ds(t0, SCAN_CHUNK), :] = y.astype(BF16)
            return h

        hb_first = lax.fori_loop(0, n_chunks, bwd, h0[1:2, :])
        fs_ref[s] = jnp.concatenate([hf_last, hb_first], axis=0)
        return carry

    lax.fori_loop(0, nb, per_seq, 0)


def _rglru(xr, gg, h0, conv_w, conv_b, w_gates, b_gates, lam, nb):
    n_seq, seq_len, _ = xr.shape
    blk = lambda i, j: (i, 0, j)
    chan = lambda i, j: (0, j)
    return pl.pallas_call(
        functools.partial(_rglru_kernel, nb=nb, seq_len=seq_len),
        out_shape=(jax.ShapeDtypeStruct((n_seq, seq_len, D_RNN), BF16),
                   jax.ShapeDtypeStruct((n_seq, 2, D_RNN), F32)),
        grid=(n_seq // nb, D_RNN // RNN_HALF),
        in_specs=[pl.BlockSpec((nb, seq_len, RNN_HALF), blk),
                  pl.BlockSpec((nb, seq_len, RNN_HALF), blk),
                  pl.BlockSpec((nb, 2, RNN_HALF), blk),
                  pl.BlockSpec((4, RNN_HALF), chan),
                  pl.BlockSpec((1, RNN_HALF), chan),
                  pl.BlockSpec((None, 2, RNN_HALF, 2 * RNN_HALF), lambda i, j: (j, 0, 0, 0)),
                  pl.BlockSpec((None, 2, 1, 2 * RNN_HALF), lambda i, j: (j, 0, 0, 0)),
                  pl.BlockSpec((2, 1, RNN_HALF), lambda i, j: (0, 0, j))],
        out_specs=(pl.BlockSpec((nb, seq_len, RNN_HALF), blk),
                   pl.BlockSpec((nb, 2, RNN_HALF), blk)),
        scratch_shapes=[pltpu.VMEM((seq_len, RNN_HALF), F32)],
        compiler_params=_params(("parallel", "parallel")),
        name="rglru",
    )(xr, gg, h0, conv_w, conv_b, w_gates, b_gates, lam)


def _route(logits):
    lane = lax.broadcasted_iota(jnp.int32, logits.shape, 1)
    neg = jnp.float32(-jnp.inf)
    gmask = lane < N_GROUPS
    gl = jnp.where(gmask, logits, neg)
    gmax = jnp.max(gl, axis=-1, keepdims=True)
    g_idx = jnp.min(jnp.where(gl == gmax, lane, ROUTER_LANES), axis=-1, keepdims=True)
    e0 = E_LANE0 + EXPERTS_PER_GROUP * g_idx
    emask = (lane >= e0) & (lane < e0 + EXPERTS_PER_GROUP)
    el = jnp.where(emask, logits, neg)
    l1 = jnp.max(el, axis=-1, keepdims=True)
    i1 = jnp.min(jnp.where(el == l1, lane, ROUTER_LANES), axis=-1, keepdims=True)
    el2 = jnp.where(lane == i1, neg, el)
    l2 = jnp.max(el2, axis=-1, keepdims=True)
    i2 = jnp.min(jnp.where(emask & (lane != i1) & (el2 == l2), lane, ROUTER_LANES), axis=-1, keepdims=True)
    ja = jnp.minimum(i1, i2) - e0
    jb = jnp.maximum(i1, i2) - e0
    pair = (ja * (2 * EXPERTS_PER_GROUP - 1 - ja)) // 2 + (jb - ja - 1)
    return g_idx * PAIRS_PER_GROUP + pair


def _store_token_major(ref, x, t0=0):
    n = x.shape[0]
    for k in range(ROW_TILES):
        ref[pl.ds(t0 * ROW_TILES + k, n, stride=ROW_TILES), :] = x[:, k * LANES:(k + 1) * LANES]


def _load_token_major(ref, n):
    return jnp.concatenate([ref[pl.ds(k, n, stride=ROW_TILES), :] for k in range(ROW_TILES)], axis=1)


def _postmix_kernel(*refs, add_pos):
    refs = list(refs)
    x_ref = refs.pop(0)
    pos_refs = (refs.pop(0), refs.pop(0)) if add_pos else None
    (yr_ref, ys_ref, mod_ref, gpost_ref, gpre_ref, wout_ref, rw_ref, rb_ref, cnt0_ref,
     x1_ref, hn_ref, rt_ref, cnt_ref, run_ref) = refs

    @pl.when(pl.program_id(0) == 0)
    def _():
        run_ref[...] = cnt0_ref[...]

    gate1 = mod_ref[0, :, 2 * D_MODEL:3 * D_MODEL]
    shift2 = mod_ref[0, :, 3 * D_MODEL:4 * D_MODEL]
    scale2 = mod_ref[0, :, 4 * D_MODEL:5 * D_MODEL]
    gain1 = gate1 * gpost_ref[...]
    gain2 = gpre_ref[...] * (1.0 + scale2)
    part = TM // TILE_ROW_PARTS
    lane = lax.broadcasted_iota(jnp.int32, (part, ROUTER_LANES), 1)
    r_i = lax.broadcasted_iota(jnp.int32, (part, part), 0)
    c_i = lax.broadcasted_iota(jnp.int32, (part, part), 1)
    earlier = (c_i < r_i).astype(BF16)
    running = run_ref[...]
    for p in range(TILE_ROW_PARTS):
        r0 = p * part
        rows = slice(r0, r0 + part)
        y = (jnp.dot(yr_ref[rows, :], wout_ref[0:D_RNN, :], preferred_element_type=F32)
             + jnp.dot(ys_ref[rows, :], wout_ref[D_RNN:, :], preferred_element_type=F32))
        x1 = _load_x(x_ref, pos_refs, r0, part) + _rms(y) * gain1
        x1_ref[rows, :] = x1
        hn = _rms(x1) * gain2 + shift2
        _store_token_major(hn_ref, hn, r0)
        logits = jnp.dot(hn.astype(BF16), rw_ref[...], preferred_element_type=F32) + rb_ref[...]
        bucket = _route(logits)
        onehot = lane == bucket
        before = jnp.dot(earlier, onehot.astype(BF16), preferred_element_type=F32) + running
        rank = jnp.sum(jnp.where(onehot, before, 0.0), axis=-1, keepdims=True).astype(jnp.int32)
        rt_ref[rows, :] = jnp.where(lane == 0, bucket, jnp.where(lane == 1, rank, 0))
        running = running + jnp.sum(onehot.astype(F32), axis=0, keepdims=True)
    run_ref[...] = running
    cnt_ref[...] = running


def _postmix(x, y_rnn, y_sgu, mod3, cond_of_tile, g_post, g_pre, w_out_b, router_w, router_b, pos_tab,
             counts0):
    n_tok = x.shape[0]
    add_pos = pos_tab is not None
    tok = lambda i: (i, 0)
    const2 = lambda i: (0, 0)
    in_specs = [pl.BlockSpec((TM, D_MODEL), tok)]
    args = [x]
    if add_pos:
        reps = TM // GRID_W
        tiles_per_seq = GRID_W // reps
        in_specs += [pl.BlockSpec((None, reps, D_MODEL // 2), lambda i: (i % tiles_per_seq, 0, 0)),
                     pl.BlockSpec((GRID_W, D_MODEL // 2), const2)]
        args += [pos_tab.reshape(tiles_per_seq, reps, D_MODEL // 2), pos_tab]
    in_specs += [pl.BlockSpec((TM, D_RNN), tok),
                 pl.BlockSpec((TM, D_SGU), tok),
                 pl.BlockSpec((1, 1, 6 * D_MODEL), lambda i: (cond_of_tile(i), 0, 0)),
                 pl.BlockSpec((1, D_MODEL), const2),
                 pl.BlockSpec((1, D_MODEL), const2),
                 pl.BlockSpec((D_MODEL, D_MODEL), const2),
                 pl.BlockSpec((D_MODEL, ROUTER_LANES), const2),
                 pl.BlockSpec((1, ROUTER_LANES), const2),
                 pl.BlockSpec((1, ROUTER_LANES), const2)]
    args += [y_rnn, y_sgu, mod3, g_post, g_pre, w_out_b, router_w, router_b, counts0]
    return pl.pallas_call(
        functools.partial(_postmix_kernel, add_pos=add_pos),
        out_shape=(jax.ShapeDtypeStruct((n_tok, D_MODEL), F32),
                   jax.ShapeDtypeStruct((n_tok * ROW_TILES, LANES), F32),
                   jax.ShapeDtypeStruct((n_tok, ROUTER_LANES), jnp.int32),
                   jax.ShapeDtypeStruct((1, ROUTER_LANES), F32)),
        grid=(n_tok // TM,),
        in_specs=in_specs,
        out_specs=(pl.BlockSpec((TM, D_MODEL), tok),
                   pl.BlockSpec((TM * ROW_TILES, LANES), tok),
                   pl.BlockSpec((TM, ROUTER_LANES), tok),
                   pl.BlockSpec((1, ROUTER_LANES), const2)),
        scratch_shapes=[pltpu.VMEM((1, ROUTER_LANES), F32)],
        compiler_params=_params(("arbitrary",)),
        name="postmix",
    )(*args)


def _token_rows(ref, t):
    return ref.at[pl.ds(pl.multiple_of(t * ROW_TILES, ROW_TILES), ROW_TILES), :]


def _dispatch_kernel(dest_ref, hc_ref, hs_ref, xs_ref, sem, *, n_ctx_steps):
    i = pl.program_id(0)
    base = i * TD

    def scatter(src_ref):
        def start(r, carry):
            pltpu.make_async_copy(_token_rows(src_ref, r), _token_rows(xs_ref, dest_ref[base + r]), sem).start()
            return carry

        lax.fori_loop(0, TD, start, 0, unroll=8)
        pltpu.make_async_copy(src_ref, xs_ref.at[pl.ds(0, TD * ROW_TILES), :], sem).wait()

    @pl.when(i < n_ctx_steps)
    def _():
        scatter(hc_ref)

    @pl.when(i >= n_ctx_steps)
    def _():
        scatter(hs_ref)


def _dispatch(dest, hn_ctx, hn_dec, n_slots):
    n_ctx_steps = hn_ctx.shape[0] // (TD * ROW_TILES)
    n_dec_steps = hn_dec.shape[0] // (TD * ROW_TILES)
    return pl.pallas_call(
        functools.partial(_dispatch_kernel, n_ctx_steps=n_ctx_steps),
        out_shape=jax.ShapeDtypeStruct((n_slots * ROW_TILES, LANES), F32),
        grid_spec=pltpu.PrefetchScalarGridSpec(
            num_scalar_prefetch=1,
            grid=(n_ctx_steps + n_dec_steps,),
            in_specs=[pl.BlockSpec((TD * ROW_TILES, LANES), lambda i, d: (jnp.minimum(i, n_ctx_steps - 1), 0)),
                      pl.BlockSpec((TD * ROW_TILES, LANES), lambda i, d: (jnp.maximum(i - n_ctx_steps, 0), 0))],
            out_specs=pl.BlockSpec(memory_space=pl.ANY),
            scratch_shapes=[pltpu.SemaphoreType.DMA(())]),
        compiler_params=_params(("arbitrary",)),
        name="dispatch",
    )(dest, hn_ctx, hn_dec)


def _experts_kernel(ea_ref, eb_ref, nv_ref, xs_ref, rw_ref, rb_ref,
                    wgua_ref, wda_ref, wgub_ref, wdb_ref, gpost_ref, ys_ref):
    i = pl.program_id(0)
    nv = nv_ref[i]

    @pl.when(nv == 0)
    def _():
        ys_ref[...] = jnp.zeros_like(ys_ref)

    @pl.when(nv > 0)
    def _():
        row = lax.broadcasted_iota(jnp.int32, (TMX, 1), 0)
        xb = jnp.where(row < nv, _load_token_major(xs_ref, TMX), 0.0).astype(BF16)
        logits = jnp.dot(xb, rw_ref[...], preferred_element_type=F32) + rb_ref[...]
        lane = lax.broadcasted_iota(jnp.int32, logits.shape, 1)
        ea = ea_ref[i]
        eb = eb_ref[i]
        gmask = lane < N_GROUPS
        gl = jnp.where(gmask, logits, -jnp.inf)
        gmax = jnp.max(gl, axis=-1, keepdims=True)
        gexp = jnp.where(gmask, jnp.exp(gl - gmax), 0.0)
        g_own = jnp.sum(jnp.where(lane == ea // EXPERTS_PER_GROUP, gexp, 0.0), axis=-1, keepdims=True)
        g_w = g_own / jnp.sum(gexp, axis=-1, keepdims=True)
        la = jnp.sum(jnp.where(lane == ea + E_LANE0, logits, 0.0), axis=-1, keepdims=True)
        lb = jnp.sum(jnp.where(lane == eb + E_LANE0, logits, 0.0), axis=-1, keepdims=True)
        m = jnp.maximum(la, lb)
        pa = jnp.exp(la - m)
        pb = jnp.exp(lb - m)
        inv = g_w / (pa + pb)

        def hidden(x, wgu_ref, w):
            gu = jnp.dot(x, wgu_ref[0], preferred_element_type=F32)
            g = gu[:, :D_EXPERT]
            return ((g * _sigmoid(g)) * gu[:, D_EXPERT:] * w).astype(BF16)

        part = TMX // EXPERT_ROW_PARTS
        ys = []
        for h in range(EXPERT_ROW_PARTS):
            rows = slice(h * part, (h + 1) * part)
            act_a = hidden(xb[rows], wgua_ref, (pa * inv)[rows])
            act_b = hidden(xb[rows], wgub_ref, (pb * inv)[rows])
            y = (jnp.dot(act_a, wda_ref[0], preferred_element_type=F32)
                 + jnp.dot(act_b, wdb_ref[0], preferred_element_type=F32))
            ys.append(_rms(y) * gpost_ref[...])
        _store_token_major(ys_ref, jnp.concatenate(ys, axis=0))


def _experts(sched, xs, router_w, router_b, wgu_b, wd_b, g_post):
    ea, eb, nv = sched
    n_tiles = ea.shape[0]
    rows = lambda i, ea, eb, nv: (i, 0)
    const2 = lambda i, ea, eb, nv: (0, 0)
    exp_a = lambda i, ea, eb, nv: (ea[i], 0, 0)
    exp_b = lambda i, ea, eb, nv: (eb[i], 0, 0)
    w_in_spec = lambda m: pl.BlockSpec((1, D_MODEL, 2 * D_EXPERT), m)
    w_out_spec = lambda m: pl.BlockSpec((1, D_EXPERT, D_MODEL), m)
    return pl.pallas_call(
        _experts_kernel,
        out_shape=jax.ShapeDtypeStruct(xs.shape, F32),
        grid_spec=pltpu.PrefetchScalarGridSpec(
            num_scalar_prefetch=3,
            grid=(n_tiles,),
            in_specs=[pl.BlockSpec((TMX * ROW_TILES, LANES), rows),
                      pl.BlockSpec((D_MODEL, ROUTER_LANES), const2),
                      pl.BlockSpec((1, ROUTER_LANES), const2),
                      w_in_spec(exp_a), w_out_spec(exp_a),
                      w_in_spec(exp_b), w_out_spec(exp_b),
                      pl.BlockSpec((1, D_MODEL), const2)],
            out_specs=pl.BlockSpec((TMX * ROW_TILES, LANES), rows)),
        compiler_params=_params(("arbitrary",)),
        name="experts",
    )(ea, eb, nv, xs, router_w, router_b, wgu_b, wd_b, wgu_b, wd_b, g_post)


def _combine_kernel(dest_ref, ys_ref, x1_ref, mod_ref, o_ref, ybuf, sems, *, tile0):
    i = pl.program_id(0)
    n = pl.num_programs(0)

    def fetch(tile, slot):
        base = (tile + tile0) * TM

        def start(r, carry):
            pltpu.make_async_copy(_token_rows(ys_ref, dest_ref[base + r]), _token_rows(ybuf.at[slot], r),
                                  sems.at[slot]).start()
            return carry

        lax.fori_loop(0, TM, start, 0, unroll=8)

    @pl.when(i == 0)
    def _():
        fetch(0, 0)

    @pl.when(i + 1 < n)
    def _():
        fetch(i + 1, (i + 1) % 2)

    slot = i % 2
    pltpu.make_async_copy(ys_ref.at[pl.ds(0, TM * ROW_TILES), :], ybuf.at[slot], sems.at[slot]).wait()
    gate2 = mod_ref[0, :, 5 * D_MODEL:6 * D_MODEL]
    o_ref[...] = x1_ref[...] + gate2 * _load_token_major(ybuf.at[slot], TM)


def _combine(dest, ys, x1, mod3, cond_of_tile, tile0):
    n_tok = x1.shape[0]
    return pl.pallas_call(
        functools.partial(_combine_kernel, tile0=tile0),
        out_shape=jax.ShapeDtypeStruct((n_tok, D_MODEL), F32),
        grid_spec=pltpu.PrefetchScalarGridSpec(
            num_scalar_prefetch=1,
            grid=(n_tok // TM,),
            in_specs=[pl.BlockSpec(memory_space=pl.ANY),
                      pl.BlockSpec((TM, D_MODEL), lambda i, d: (i, 0)),
                      pl.BlockSpec((1, 1, 6 * D_MODEL), lambda i, d: (cond_of_tile(i), 0, 0))],
            out_specs=pl.BlockSpec((TM, D_MODEL), lambda i, d: (i, 0)),
            scratch_shapes=[pltpu.VMEM((2, TM * ROW_TILES, LANES), F32), pltpu.SemaphoreType.DMA((2,))]),
        compiler_params=_params(("arbitrary",)),
        name="combine",
    )(dest, ys, x1, mod3)


def _schedule(route, counts):
    n_tok = route.shape[0]
    n_max = n_tok // TMX + N_BUCKETS
    bucket = route[:, 0]
    rank = route[:, 1]
    cnt = counts[0, :N_BUCKETS].astype(jnp.int32)
    tiles = (cnt + TMX - 1) // TMX
    tile_end = jnp.cumsum(tiles)
    tile_start = tile_end - tiles
    ids = jnp.arange(N_BUCKETS, dtype=jnp.int32)
    slot0 = jnp.sum(jnp.where(bucket[:, None] == ids[None, :], (tile_start * TMX)[None, :], 0), axis=1)
    dest = slot0 + rank
    i = jnp.arange(n_max, dtype=jnp.int32)
    total = tile_end[-1]
    valid = i < total
    tb = jnp.sum((jnp.minimum(i, total - 1)[:, None] >= tile_end[None, :]).astype(jnp.int32), axis=1)
    pairs = [(a, b) for a in range(EXPERTS_PER_GROUP) for b in range(a + 1, EXPERTS_PER_GROUP)]
    ea_tab = jnp.array([g * EXPERTS_PER_GROUP + a for g in range(N_GROUPS) for a, _ in pairs], jnp.int32)
    eb_tab = jnp.array([g * EXPERTS_PER_GROUP + b for g in range(N_GROUPS) for _, b in pairs], jnp.int32)
    ea = ea_tab[tb]
    eb = eb_tab[tb]
    nv = jnp.where(valid, jnp.clip(cnt[tb] - (i - tile_start[tb]) * TMX, 0, TMX), 0)
    return dest, (ea, eb, nv), n_max * TMX


def _block_diag_gates(rg_wa, rg_wx):
    heads_per_half = N_HEADS_RNN // 2

    def bd(w):
        w = w.reshape(2, 2, heads_per_half, HEAD_RNN, HEAD_RNN)
        eye = jnp.eye(heads_per_half, dtype=w.dtype)
        full = jnp.einsum('dghij,hk->dghikj', w, eye)
        return full.reshape(2, 2, RNN_HALF, RNN_HALF)

    w = jnp.concatenate([bd(rg_wa), bd(rg_wx)], axis=-1)
    return jnp.transpose(w, (1, 0, 2, 3)).astype(BF16)


def kernel(x_prompt, x_sample, state_rglru, c, c_ctx, w_mod, b_mod, g_pre_mix, g_post_mix, g_pre_ffn,
           g_post_ffn, w_in, conv_w, conv_b, rg_wa, rg_ba, rg_wx, rg_bx, rg_lambda, sgu_g, sgu_w, sgu_b,
           w_out, router_g_w, router_g_b, router_e_w, router_e_b, exp_w_gate, exp_w_up, exp_w_down):
    assert w_mod.shape[0] == 1, "single-layer trunk"
    n_ctx, ctx_len, _ = x_prompt.shape
    n_dec, dec_len, _ = x_sample.shape
    l = 0

    n_cond = SUBLANES
    cond = jnp.zeros((n_cond, D_MODEL), F32).at[0].set(c_ctx).at[1:1 + n_dec].set(c)
    mod3 = _modulation(cond, w_mod[l], b_mod[l]).reshape(n_cond, 1, 6 * D_MODEL)
    pos_tab = _pos_table()

    w_in_b = w_in[l].astype(BF16)
    w_out_b = w_out[l].astype(BF16)
    sgu_w_b = sgu_w[l].reshape(2, 4 * CHUNK, CHUNK).astype(BF16)
    sgu_bias_tile = jnp.repeat(sgu_b[l].T, HEAD_SGU, axis=1)
    w_gates = _block_diag_gates(rg_wa[l], rg_wx[l])
    b_gates = jnp.concatenate([rg_ba[l].reshape(2, 2, 1, RNN_HALF), rg_bx[l].reshape(2, 2, 1, RNN_HALF)],
                              axis=-1)
    b_gates = jnp.transpose(b_gates, (1, 0, 2, 3))
    lam = rg_lambda[l].reshape(2, 1, D_RNN)
    router_w = jnp.zeros((D_MODEL, ROUTER_LANES), F32)
    router_w = router_w.at[:, :N_GROUPS].set(router_g_w[l]).at[:, E_LANE0:E_LANE0 + N_EXPERTS].set(router_e_w[l])
    router_w = router_w.astype(BF16)
    router_b = jnp.zeros((1, ROUTER_LANES), F32)
    router_b = router_b.at[0, :N_GROUPS].set(router_g_b[l]).at[0, E_LANE0:E_LANE0 + N_EXPERTS].set(router_e_b[l])
    wgu_b = jnp.concatenate([exp_w_gate[l].astype(BF16), exp_w_up[l].astype(BF16)], axis=-1)
    wd_b = exp_w_down[l].astype(BF16)
    row = lambda v: v.reshape(1, -1)

    n_ctx_tok = n_ctx * ctx_len

    def mixer(x, h0, cond_of_tile, seqs_per_step, use_pos, counts0):
        n_seq, seq_len, _ = x.shape
        xf = x.reshape(n_seq * seq_len, D_MODEL)
        tab = pos_tab if use_pos else None
        xr, gg, y_sgu = _premix(xf, mod3, cond_of_tile, row(g_pre_mix[l]), w_in_b, row(sgu_g[l]),
                                sgu_w_b, sgu_bias_tile, tab)
        y_rnn, fstate = _rglru(xr.reshape(n_seq, seq_len, D_RNN), gg.reshape(n_seq, seq_len, D_RNN), h0,
                               conv_w[l], row(conv_b[l]), w_gates, b_gates, lam, seqs_per_step)
        x1, hn, route, counts = _postmix(
            xf, y_rnn.reshape(n_seq * seq_len, D_RNN), y_sgu, mod3, cond_of_tile, row(g_post_mix[l]),
            row(g_pre_ffn[l]), w_out_b, router_w, router_b, tab, counts0)
        return x1, hn, route, counts, fstate

    tiles_per_seq = dec_len // TM
    ctx_cond = lambda i: 0
    dec_cond = lambda i: 1 + i // tiles_per_seq
    h0_ctx = jnp.zeros((n_ctx, 2, D_RNN), F32)
    counts0 = jnp.zeros((1, ROUTER_LANES), F32)
    x1_ctx, hn_ctx, route_ctx, counts, st = mixer(x_prompt, h0_ctx, ctx_cond, 8, False, counts0)
    new_state = st.astype(state_rglru.dtype)[:, None]
    x1_dec, hn_dec, route_dec, counts, _ = mixer(x_sample, state_rglru[:, l].astype(F32), dec_cond, 1, True,
                                                 counts)

    route = jnp.concatenate([route_ctx[:, :2], route_dec[:, :2]], axis=0)
    dest, sched, n_slots = _schedule(route, counts)
    xs = _dispatch(dest, hn_ctx, hn_dec, n_slots)
    ys = _experts(sched, xs, router_w, router_b, wgu_b, wd_b, row(g_post_ffn[l]))
    y_prompt = _combine(dest, ys, x1_ctx, mod3, ctx_cond, 0)
    y_sample = _combine(dest, ys, x1_dec, mod3, dec_cond, n_ctx_tok // TM)
    return (y_prompt.reshape(x_prompt.shape), y_sample.reshape(x_sample.shape), new_state)
```

```python
import functools
import math

import jax
import jax.numpy as jnp
from jax import lax
from jax.experimental import pallas as pl
from jax.experimental.pallas import tpu as pltpu

D_MODEL = 1024
D_RNN = 512
D_SGU = 512
N_HEADS_RNN = 8
HEAD_RNN = D_RNN // N_HEADS_RNN
N_HEADS_SGU = 8
HEAD_SGU = D_SGU // N_HEADS_SGU
CHUNK = 128
GRID_W = 64
RG_C = 8.0
N_GROUPS = 4
EXPERTS_PER_GROUP = 4
N_EXPERTS = N_GROUPS * EXPERTS_PER_GROUP
D_EXPERT = 512
EPS = 1e-6
POS_BASE = 10000.0

LANES = 128
SUBLANES = 8
CONV_W = 4
CONV_LEFT = 2
PAIR = 2
RNN_BLOCKS = D_RNN // LANES
TMJ_ROWS = PAIR * RNN_BLOCKS
ROUTER_LANES = LANES
E_LANE0 = N_GROUPS

PAIRS_PER_GROUP = EXPERTS_PER_GROUP * (EXPERTS_PER_GROUP - 1) // 2
N_BUCKETS = N_GROUPS * PAIRS_PER_GROUP

ROW_TILES = D_MODEL // LANES

TM = 512
TD = 512
EXPERT_ROW_PARTS = 2
TILE_ROW_PARTS = 1
TMX = 512
TT = TM // PAIR
LC = 256
TS = 16
PB = 2
VMEM_LIMIT = 56 * 1024 * 1024

F32 = jnp.float32
BF16 = jnp.bfloat16


def _params(sem):
    return pltpu.CompilerParams(dimension_semantics=sem, vmem_limit_bytes=VMEM_LIMIT)


def _rms(x):
    return x * lax.rsqrt(jnp.mean(x * x, axis=-1, keepdims=True) + EPS)


def _sigmoid(x):
    return 0.5 * jnp.tanh(0.5 * x) + 0.5


def _mod_kernel(cond_ref, w_ref, b_ref, o_ref):
    c = cond_ref[...]
    s = c * _sigmoid(c)
    o_ref[...] = jnp.dot(s.astype(BF16), w_ref[...].astype(BF16),
                         preferred_element_type=F32) + b_ref[...]


def _modulation(cond, w_mod, b_mod):
    n = w_mod.shape[1]
    bn = 1024
    return pl.pallas_call(
        _mod_kernel,
        out_shape=jax.ShapeDtypeStruct((cond.shape[0], n), F32),
        grid=(n // bn,),
        in_specs=[pl.BlockSpec(cond.shape, lambda j: (0, 0)),
                  pl.BlockSpec((D_MODEL, bn), lambda j: (0, j)),
                  pl.BlockSpec((1, bn), lambda j: (0, j))],
        out_specs=pl.BlockSpec((cond.shape[0], bn), lambda j: (0, j)),
        compiler_params=_params(("arbitrary",)),
        name="modulation",
    )(cond, w_mod, b_mod.reshape(1, n))


def _pos_kernel(o_ref):
    n_freq = D_MODEL // 4
    k = lax.broadcasted_iota(jnp.int32, (GRID_W, n_freq), 1).astype(F32)
    p = lax.broadcasted_iota(jnp.int32, (GRID_W, n_freq), 0).astype(F32)
    freq = jnp.exp(-math.log(POS_BASE) * k / n_freq)
    ang = p * freq
    o_ref[:, 0:n_freq] = jnp.sin(ang)
    o_ref[:, n_freq:2 * n_freq] = jnp.cos(ang)


def _pos_table():
    return pl.pallas_call(
        _pos_kernel,
        out_shape=jax.ShapeDtypeStruct((GRID_W, D_MODEL // 2), F32),
        name="pos_table",
    )()


def _add_pos(x, pos_refs, q0):
    if pos_refs is None:
        return x
    rows_ref, cols_ref = pos_refs
    reps = x.shape[0] // GRID_W
    rpart = jnp.concatenate(
        [jnp.broadcast_to(rows_ref[q:q + 1, :], (GRID_W, D_MODEL // 2)) for q in range(q0, q0 + reps)], axis=0)
    cpart = jnp.concatenate([cols_ref[...]] * reps, axis=0)
    return jnp.concatenate([x[:, :D_MODEL // 2] + rpart, x[:, D_MODEL // 2:] + cpart], axis=1)


def _load_x(x_ref, pos_refs, r0, n):
    return _add_pos(x_ref[r0:r0 + n, :], pos_refs, r0 // GRID_W)


def _premix_kernel(*refs, add_pos):
    refs = list(refs)
    x_ref = refs.pop(0)
    pos_refs = (refs.pop(0), refs.pop(0)) if add_pos else None
    mod_ref, g_ref, win_ref, sgug_ref, sguw_ref, sgub_ref, xr_ref, gg_ref, ys_ref = refs
    hn = []
    for s in range(PAIR):
        shift = mod_ref[s, :, 0:D_MODEL]
        scale = mod_ref[s, :, D_MODEL:2 * D_MODEL]
        hn.append(_rms(_add_pos(x_ref[s], pos_refs, 0)) * (g_ref[...] * (1.0 + scale)) + shift)
    z = jnp.dot(jnp.concatenate(hn, axis=0).astype(BF16), win_ref[...],
                preferred_element_type=F32)
    half = D_SGU // 2
    heads_per_half = N_HEADS_SGU // 2
    lane_head = lax.broadcasted_iota(jnp.int32, (CHUNK, half), 1) // HEAD_SGU
    for s in range(PAIR):
        zs = z[s * TT:(s + 1) * TT]
        gg = jax.nn.gelu(zs[:, D_RNN:2 * D_RNN])
        for k in range(RNN_BLOCKS):
            rows = pl.ds(s * RNN_BLOCKS + k, TT, stride=TMJ_ROWS)
            xr_ref[rows, :] = zs[:, k * LANES:(k + 1) * LANES]
            gg_ref[rows, :] = gg[:, k * LANES:(k + 1) * LANES]
        u = zs[:, 2 * D_RNN:2 * D_RNN + D_SGU]
        vn = (_rms(zs[:, 2 * D_RNN + D_SGU:]) * sgug_ref[...]).astype(BF16)
        for c in range(TT // CHUNK):
            rows = slice(c * CHUNK, (c + 1) * CHUNK)
            halves = []
            for hf in range(2):
                r = jnp.dot(sguw_ref[hf], vn[rows, hf * half:(hf + 1) * half],
                            preferred_element_type=F32)
                sel = jnp.zeros((CHUNK, half), F32)
                for h in range(heads_per_half):
                    sel = jnp.where(lane_head == h, r[h * CHUNK:(h + 1) * CHUNK], sel)
                halves.append(sel)
            gatev = jnp.concatenate(halves, axis=1) + sgub_ref[...]
            ys_ref[s, rows, :] = (u[rows] * gatev).astype(BF16)


def _premix(x, mod3, cond_block, g_pre, w_in_b, sgu_g, sgu_w_b, sgu_bias_tile, pos_tab):
    n_seq, seq_len, _ = x.shape
    n_pairs, n_tiles = n_seq // PAIR, seq_len // TT
    add_pos = pos_tab is not None
    const2 = lambda p, j: (0, 0)
    in_specs = [pl.BlockSpec((PAIR, TT, D_MODEL), lambda p, j: (p, j, 0))]
    args = [x]
    if add_pos:
        reps = TT // GRID_W
        in_specs += [pl.BlockSpec((None, reps, D_MODEL // 2), lambda p, j: (j, 0, 0)),
                     pl.BlockSpec((GRID_W, D_MODEL // 2), const2)]
        args += [pos_tab.reshape(GRID_W // reps, reps, D_MODEL // 2), pos_tab]
    in_specs += [pl.BlockSpec((PAIR, 1, 6 * D_MODEL), lambda p, j: (cond_block(p), 0, 0)),
                 pl.BlockSpec((1, D_MODEL), const2),
                 pl.BlockSpec((D_MODEL, 2 * D_RNN + 2 * D_SGU), const2),
                 pl.BlockSpec((1, D_SGU), const2),
                 pl.BlockSpec((2, 4 * CHUNK, CHUNK), lambda p, j: (0, 0, 0)),
                 pl.BlockSpec((CHUNK, D_SGU), const2)]
    args += [mod3, g_pre, w_in_b, sgu_g, sgu_w_b, sgu_bias_tile]
    tmj = jax.ShapeDtypeStruct((n_pairs * seq_len * TMJ_ROWS, LANES), F32)
    tmj_spec = pl.BlockSpec((TT * TMJ_ROWS, LANES), lambda p, j: (p * n_tiles + j, 0))
    xr, gg, y_sgu = pl.pallas_call(
        functools.partial(_premix_kernel, add_pos=add_pos),
        out_shape=(tmj, tmj, jax.ShapeDtypeStruct((n_seq, seq_len, D_SGU), BF16)),
        grid=(n_pairs, n_tiles),
        in_specs=in_specs,
        out_specs=(tmj_spec, tmj_spec, pl.BlockSpec((PAIR, TT, D_SGU), lambda p, j: (p, j, 0))),
        compiler_params=_params(("parallel", "parallel")),
        name="premix",
    )(*args)
    shape4 = (n_pairs, seq_len, TMJ_ROWS, LANES)
    return xr.reshape(shape4), gg.reshape(shape4), y_sgu


def _scan_kernel(*refs, reverse, n_chunks):
    if reverse:
        (xprev_ref, x_ref, xnext_ref, gg_ref, hf_ref, h0_ref, cw_ref, cb_ref, wg_ref, bg_ref, lam_ref,
         y_ref, fs_ref, xwin, xc_s, r_s, i_s, a_s, b_s, y_s, hcar) = refs
    else:
        (xprev_ref, x_ref, xnext_ref, h0_ref, cw_ref, cb_ref, wg_ref, bg_ref, lam_ref,
         hf_ref, fs_ref, xwin, xc_s, r_s, i_s, a_s, b_s, hcar) = refs
    c = pl.program_id(1)
    chunk = n_chunks - 1 - c if reverse else c
    sub_rows = TS * TMJ_ROWS

    def rows_of(pb, t0, n_steps):
        return pl.ds(pl.multiple_of((pb * LC + t0) * TMJ_ROWS, TMJ_ROWS), n_steps * TMJ_ROWS)

    @pl.when(c == 0)
    def _():
        hcar[...] = h0_ref[...]

    xwin[:, 0:CONV_LEFT] = jnp.where(chunk > 0, xprev_ref[...], 0.0)
    xwin[:, LC + CONV_LEFT:LC + CONV_W - 1] = jnp.where(chunk < n_chunks - 1, xnext_ref[...], 0.0)

    def copy(i, carry):
        t0 = pl.multiple_of(i * TS, TS)
        xwin[:, pl.ds(t0 + CONV_LEFT, TS)] = x_ref[:, pl.ds(t0, TS)]
        return carry

    lax.fori_loop(0, LC // TS, copy, 0)

    def conv(i, carry):
        t0 = pl.multiple_of(i * TS, TS)
        xc = cb_ref[...] + cw_ref[0] * xwin[:, pl.ds(t0, TS)]
        for k in range(1, CONV_W):
            xc = xc + cw_ref[k] * xwin[:, pl.ds(t0 + k, TS)]
        for pb in range(PB):
            xc_s[rows_of(pb, t0, TS), :] = xc[pb].reshape(sub_rows, LANES)
        return carry

    lax.fori_loop(0, LC // TS, conv, 0)

    n_rows = PB * LC * PAIR
    for k in range(RNN_BLOCKS):
        rows = pl.ds(k, n_rows, stride=RNN_BLOCKS)
        g = jnp.dot(xc_s[rows, :].astype(BF16), wg_ref[k], preferred_element_type=F32)
        r_s[rows, :] = g[:, :LANES]
        i_s[rows, :] = g[:, LANES:]

    neg_lam = -lam_ref[...]
    softplus = jnp.maximum(neg_lam, 0.0) + jnp.log(1.0 + jnp.exp(-jnp.abs(neg_lam)))
    decay = -RG_C * softplus

    def gates(i, carry):
        t0 = pl.multiple_of(i * TS, TS)
        for pb in range(PB):
            rows = rows_of(pb, t0, TS)
            tile = lambda ref: ref[rows, :].reshape(TS, TMJ_ROWS, LANES)
            xc = tile(xc_s)
            r = _sigmoid(tile(r_s) + bg_ref[0])
            ig = _sigmoid(tile(i_s) + bg_ref[1])
            log_a = r * decay
            a = jnp.exp(log_a)
            q = jnp.tanh(-log_a) * (a * a + 1.0)
            b = jnp.where(q > 0.0, q * lax.rsqrt(q), 0.0) * (ig * xc)
            a_s[rows, :] = a.reshape(sub_rows, LANES)
            b_s[rows, :] = b.reshape(sub_rows, LANES)
        return carry

    lax.fori_loop(0, LC // TS, gates, 0)

    def step(j, hs):
        t = LC - 1 - j if reverse else j
        out = []
        for pb in range(PB):
            rows = rows_of(pb, t, 1)
            h = a_s[rows, :] * hs[pb] + b_s[rows, :]
            if reverse:
                y_s[rows, :] = (hf_ref[pb, t] + h) * gg_ref[pb, t]
            else:
                hf_ref[pb, t] = h
            out.append(h)
        return tuple(out)

    hs = lax.fori_loop(0, LC, step, tuple(hcar[pb] for pb in range(PB)), unroll=8)
    for pb in range(PB):
        hcar[pb] = hs[pb]
        fs_ref[pb] = hs[pb]

    if reverse:
        for pb in range(PB):
            for s in range(PAIR):
                cols = [y_s[pl.ds(pb * LC * TMJ_ROWS + s * RNN_BLOCKS + k, LC, stride=TMJ_ROWS), :]
                        for k in range(RNN_BLOCKS)]
                y_ref[pb * PAIR + s] = jnp.concatenate(cols, axis=1).astype(BF16)


def _scan(xr, gg, hf, h0, conv_w, conv_b, w_gates, b_gates, lam, direction):
    n_pairs, seq_len = xr.shape[:2]
    n_chunks = seq_len // LC
    reverse = direction == 1
    pos = (lambda c: n_chunks - 1 - c) if reverse else (lambda c: c)
    tmj_blk = pl.BlockSpec((PB, LC, TMJ_ROWS, LANES), lambda i, c: (i, pos(c), 0, 0))
    state_blk = pl.BlockSpec((PB, TMJ_ROWS, LANES), lambda i, c: (i, 0, 0))
    per_dir = lambda *shape: pl.BlockSpec((None,) + shape, lambda i, c: (direction,) + (0,) * len(shape))
    in_specs = [
        pl.BlockSpec((PB, CONV_LEFT, TMJ_ROWS, LANES),
                     lambda i, c: (i, jnp.maximum(pos(c) * (LC // CONV_LEFT) - 1, 0), 0, 0)),
        tmj_blk,
        pl.BlockSpec((PB, 1, TMJ_ROWS, LANES), lambda i, c: (i, jnp.minimum((pos(c) + 1) * LC, seq_len - 1), 0, 0)),
    ]
    args = [xr, xr, xr]
    if reverse:
        in_specs += [tmj_blk, tmj_blk]
        args += [gg, hf]
    in_specs += [state_blk,
                 pl.BlockSpec((CONV_W, TMJ_ROWS, LANES), lambda i, c: (0, 0, 0)),
                 pl.BlockSpec((TMJ_ROWS, LANES), lambda i, c: (0, 0)),
                 per_dir(RNN_BLOCKS, LANES, 2 * LANES),
                 per_dir(2, TMJ_ROWS, LANES),
                 per_dir(TMJ_ROWS, LANES)]
    args += [h0, conv_w, conv_b, w_gates, b_gates, lam]
    flat = pltpu.VMEM((PB * LC * TMJ_ROWS, LANES), F32)
    scratch = [pltpu.VMEM((PB, LC + CONV_W - 1, TMJ_ROWS, LANES), F32)] + [flat] * (6 if reverse else 5)
    scratch += [pltpu.VMEM((PB, TMJ_ROWS, LANES), F32)]
    state = jax.ShapeDtypeStruct((n_pairs, TMJ_ROWS, LANES), F32)
    if reverse:
        out_shape = (jax.ShapeDtypeStruct((n_pairs * PAIR, seq_len, D_RNN), BF16), state)
        out_specs = (pl.BlockSpec((PB * PAIR, LC, D_RNN), lambda i, c: (i, pos(c), 0)), state_blk)
    else:
        out_shape = (jax.ShapeDtypeStruct(xr.shape, F32), state)
        out_specs = (tmj_blk, state_blk)
    return pl.pallas_call(
        functools.partial(_scan_kernel, reverse=reverse, n_chunks=n_chunks),
        out_shape=out_shape,
        grid=(n_pairs // PB, n_chunks),
        in_specs=in_specs,
        out_specs=out_specs,
        scratch_shapes=scratch,
        compiler_params=_params(("parallel", "arbitrary")),
        name="scan_bwd" if reverse else "scan_fwd",
    )(*args)


def _route(logits):
    lane = lax.broadcasted_iota(jnp.int32, logits.shape, 1)
    neg = jnp.float32(-jnp.inf)
    gmask = lane < N_GROUPS
    gl = jnp.where(gmask, logits, neg)
    gmax = jnp.max(gl, axis=-1, keepdims=True)
    g_idx = jnp.min(jnp.where(gl == gmax, lane, ROUTER_LANES), axis=-1, keepdims=True)
    e0 = E_LANE0 + EXPERTS_PER_GROUP * g_idx
    emask = (lane >= e0) & (lane < e0 + EXPERTS_PER_GROUP)
    el = jnp.where(emask, logits, neg)
    l1 = jnp.max(el, axis=-1, keepdims=True)
    i1 = jnp.min(jnp.where(el == l1, lane, ROUTER_LANES), axis=-1, keepdims=True)
    el2 = jnp.where(lane == i1, neg, el)
    l2 = jnp.max(el2, axis=-1, keepdims=True)
    i2 = jnp.min(jnp.where(emask & (lane != i1) & (el2 == l2), lane, ROUTER_LANES), axis=-1, keepdims=True)
    ja = jnp.minimum(i1, i2) - e0
    jb = jnp.maximum(i1, i2) - e0
    pair = (ja * (2 * EXPERTS_PER_GROUP - 1 - ja)) // 2 + (jb - ja - 1)
    return g_idx * PAIRS_PER_GROUP + pair


def _store_token_major(ref, x, t0=0):
    n = x.shape[0]
    for k in range(ROW_TILES):
        ref[pl.ds(t0 * ROW_TILES + k, n, stride=ROW_TILES), :] = x[:, k * LANES:(k + 1) * LANES]


def _load_token_major(ref, n):
    return jnp.concatenate([ref[pl.ds(k, n, stride=ROW_TILES), :] for k in range(ROW_TILES)], axis=1)


def _postmix_kernel(*refs, add_pos):
    refs = list(refs)
    x_ref = refs.pop(0)
    pos_refs = (refs.pop(0), refs.pop(0)) if add_pos else None
    (yr_ref, ys_ref, mod_ref, gpost_ref, gpre_ref, wout_ref, rw_ref, rb_ref, cnt0_ref,
     x1_ref, hn_ref, rt_ref, cnt_ref, run_ref) = refs

    @pl.when(pl.program_id(0) == 0)
    def _():
        run_ref[...] = cnt0_ref[...]

    gate1 = mod_ref[0, :, 2 * D_MODEL:3 * D_MODEL]
    shift2 = mod_ref[0, :, 3 * D_MODEL:4 * D_MODEL]
    scale2 = mod_ref[0, :, 4 * D_MODEL:5 * D_MODEL]
    gain1 = gate1 * gpost_ref[...]
    gain2 = gpre_ref[...] * (1.0 + scale2)
    part = TM // TILE_ROW_PARTS
    lane = lax.broadcasted_iota(jnp.int32, (part, ROUTER_LANES), 1)
    r_i = lax.broadcasted_iota(jnp.int32, (part, part), 0)
    c_i = lax.broadcasted_iota(jnp.int32, (part, part), 1)
    earlier = (c_i < r_i).astype(BF16)
    running = run_ref[...]
    for p in range(TILE_ROW_PARTS):
        r0 = p * part
        rows = slice(r0, r0 + part)
        y = (jnp.dot(yr_ref[rows, :], wout_ref[0:D_RNN, :], preferred_element_type=F32)
             + jnp.dot(ys_ref[rows, :], wout_ref[D_RNN:, :], preferred_element_type=F32))
        x1 = _load_x(x_ref, pos_refs, r0, part) + _rms(y) * gain1
        x1_ref[rows, :] = x1
        hn = _rms(x1) * gain2 + shift2
        _store_token_major(hn_ref, hn, r0)
        logits = jnp.dot(hn.astype(BF16), rw_ref[...], preferred_element_type=F32) + rb_ref[...]
        bucket = _route(logits)
        onehot = lane == bucket
        before = jnp.dot(earlier, onehot.astype(BF16), preferred_element_type=F32) + running
        rank = jnp.sum(jnp.where(onehot, before, 0.0), axis=-1, keepdims=True).astype(jnp.int32)
        rt_ref[rows, :] = jnp.where(lane == 0, bucket, jnp.where(lane == 1, rank, 0))
        running = running + jnp.sum(onehot.astype(F32), axis=0, keepdims=True)
    run_ref[...] = running
    cnt_ref[...] = running


def _postmix(x, y_rnn, y_sgu, mod3, cond_of_tile, g_post, g_pre, w_out_b, router_w, router_b, pos_tab,
             counts0):
    n_tok = x.shape[0]
    add_pos = pos_tab is not None
    tok = lambda i: (i, 0)
    const2 = lambda i: (0, 0)
    in_specs = [pl.BlockSpec((TM, D_MODEL), tok)]
    args = [x]
    if add_pos:
        reps = TM // GRID_W
        tiles_per_seq = GRID_W // reps
        in_specs += [pl.BlockSpec((None, reps, D_MODEL // 2), lambda i: (i % tiles_per_seq, 0, 0)),
                     pl.BlockSpec((GRID_W, D_MODEL // 2), const2)]
        args += [pos_tab.reshape(tiles_per_seq, reps, D_MODEL // 2), pos_tab]
    in_specs += [pl.BlockSpec((TM, D_RNN), tok),
                 pl.BlockSpec((TM, D_SGU), tok),
                 pl.BlockSpec((1, 1, 6 * D_MODEL), lambda i: (cond_of_tile(i), 0, 0)),
                 pl.BlockSpec((1, D_MODEL), const2),
                 pl.BlockSpec((1, D_MODEL), const2),
                 pl.BlockSpec((D_MODEL, D_MODEL), const2),
                 pl.BlockSpec((D_MODEL, ROUTER_LANES), const2),
                 pl.BlockSpec((1, ROUTER_LANES), const2),
                 pl.BlockSpec((1, ROUTER_LANES), const2)]
    args += [y_rnn, y_sgu, mod3, g_post, g_pre, w_out_b, router_w, router_b, counts0]
    return pl.pallas_call(
        functools.partial(_postmix_kernel, add_pos=add_pos),
        out_shape=(jax.ShapeDtypeStruct((n_tok, D_MODEL), F32),
                   jax.ShapeDtypeStruct((n_tok * ROW_TILES, LANES), F32),
                   jax.ShapeDtypeStruct((n_tok, ROUTER_LANES), jnp.int32),
                   jax.ShapeDtypeStruct((1, ROUTER_LANES), F32)),
        grid=(n_tok // TM,),
        in_specs=in_specs,
        out_specs=(pl.BlockSpec((TM, D_MODEL), tok),
                   pl.BlockSpec((TM * ROW_TILES, LANES), tok),
                   pl.BlockSpec((TM, ROUTER_LANES), tok),
                   pl.BlockSpec((1, ROUTER_LANES), const2)),
        scratch_shapes=[pltpu.VMEM((1, ROUTER_LANES), F32)],
        compiler_params=_params(("arbitrary",)),
        name="postmix",
    )(*args)


def _token_rows(ref, t):
    return ref.at[pl.ds(pl.multiple_of(t * ROW_TILES, ROW_TILES), ROW_TILES), :]


def _dispatch_kernel(dest_ref, hc_ref, hs_ref, xs_ref, sem, *, n_ctx_steps):
    i = pl.program_id(0)
    base = i * TD

    def scatter(src_ref):
        def start(r, carry):
            pltpu.make_async_copy(_token_rows(src_ref, r), _token_rows(xs_ref, dest_ref[base + r]), sem).start()
            return carry

        lax.fori_loop(0, TD, start, 0, unroll=8)
        pltpu.make_async_copy(src_ref, xs_ref.at[pl.ds(0, TD * ROW_TILES), :], sem).wait()

    @pl.when(i < n_ctx_steps)
    def _():
        scatter(hc_ref)

    @pl.when(i >= n_ctx_steps)
    def _():
        scatter(hs_ref)


def _dispatch(dest, hn_ctx, hn_dec, n_slots):
    n_ctx_steps = hn_ctx.shape[0] // (TD * ROW_TILES)
    n_dec_steps = hn_dec.shape[0] // (TD * ROW_TILES)
    return pl.pallas_call(
        functools.partial(_dispatch_kernel, n_ctx_steps=n_ctx_steps),
        out_shape=jax.ShapeDtypeStruct((n_slots * ROW_TILES, LANES), F32),
        grid_spec=pltpu.PrefetchScalarGridSpec(
            num_scalar_prefetch=1,
            grid=(n_ctx_steps + n_dec_steps,),
            in_specs=[pl.BlockSpec((TD * ROW_TILES, LANES), lambda i, d: (jnp.minimum(i, n_ctx_steps - 1), 0)),
                      pl.BlockSpec((TD * ROW_TILES, LANES), lambda i, d: (jnp.maximum(i - n_ctx_steps, 0), 0))],
            out_specs=pl.BlockSpec(memory_space=pl.ANY),
            scratch_shapes=[pltpu.SemaphoreType.DMA(())]),
        compiler_params=_params(("arbitrary",)),
        name="dispatch",
    )(dest, hn_ctx, hn_dec)


def _experts_kernel(ea_ref, eb_ref, nv_ref, xs_ref, rw_ref, rb_ref,
                    wgua_ref, wda_ref, wgub_ref, wdb_ref, gpost_ref, ys_ref):
    i = pl.program_id(0)
    nv = nv_ref[i]

    @pl.when(nv == 0)
    def _():
        ys_ref[...] = jnp.zeros_like(ys_ref)

    @pl.when(nv > 0)
    def _():
        row = lax.broadcasted_iota(jnp.int32, (TMX, 1), 0)
        xb = jnp.where(row < nv, _load_token_major(xs_ref, TMX), 0.0).astype(BF16)
        logits = jnp.dot(xb, rw_ref[...], preferred_element_type=F32) + rb_ref[...]
        lane = lax.broadcasted_iota(jnp.int32, logits.shape, 1)
        ea = ea_ref[i]
        eb = eb_ref[i]
        gmask = lane < N_GROUPS
        gl = jnp.where(gmask, logits, -jnp.inf)
        gmax = jnp.max(gl, axis=-1, keepdims=True)
        gexp = jnp.where(gmask, jnp.exp(gl - gmax), 0.0)
        g_own = jnp.sum(jnp.where(lane == ea // EXPERTS_PER_GROUP, gexp, 0.0), axis=-1, keepdims=True)
        g_w = g_own / jnp.sum(gexp, axis=-1, keepdims=True)
        la = jnp.sum(jnp.where(lane == ea + E_LANE0, logits, 0.0), axis=-1, keepdims=True)
        lb = jnp.sum(jnp.where(lane == eb + E_LANE0, logits, 0.0), axis=-1, keepdims=True)
        m = jnp.maximum(la, lb)
        pa = jnp.exp(la - m)
        pb = jnp.exp(lb - m)
        inv = g_w / (pa + pb)

        def hidden(x, wgu_ref, w):
            gu = jnp.dot(x, wgu_ref[0], preferred_element_type=F32)
            g = gu[:, :D_EXPERT]
            return ((g * _sigmoid(g)) * gu[:, D_EXPERT:] * w).astype(BF16)

        part = TMX // EXPERT_ROW_PARTS
        ys = []
        for h in range(EXPERT_ROW_PARTS):
            rows = slice(h * part, (h + 1) * part)
            act_a = hidden(xb[rows], wgua_ref, (pa * inv)[rows])
            act_b = hidden(xb[rows], wgub_ref, (pb * inv)[rows])
            y = (jnp.dot(act_a, wda_ref[0], preferred_element_type=F32)
                 + jnp.dot(act_b, wdb_ref[0], preferred_element_type=F32))
            ys.append(_rms(y) * gpost_ref[...])
        _store_token_major(ys_ref, jnp.concatenate(ys, axis=0))


def _experts(sched, xs, router_w, router_b, wgu_b, wd_b, g_post):
    ea, eb, nv = sched
    n_tiles = ea.shape[0]
    rows = lambda i, ea, eb, nv: (i, 0)
    const2 = lambda i, ea, eb, nv: (0, 0)
    exp_a = lambda i, ea, eb, nv: (ea[i], 0, 0)
    exp_b = lambda i, ea, eb, nv: (eb[i], 0, 0)
    w_in_spec = lambda m: pl.BlockSpec((1, D_MODEL, 2 * D_EXPERT), m)
    w_out_spec = lambda m: pl.BlockSpec((1, D_EXPERT, D_MODEL), m)
    return pl.pallas_call(
        _experts_kernel,
        out_shape=jax.ShapeDtypeStruct(xs.shape, F32),
        grid_spec=pltpu.PrefetchScalarGridSpec(
            num_scalar_prefetch=3,
            grid=(n_tiles,),
            in_specs=[pl.BlockSpec((TMX * ROW_TILES, LANES), rows),
                      pl.BlockSpec((D_MODEL, ROUTER_LANES), const2),
                      pl.BlockSpec((1, ROUTER_LANES), const2),
                      w_in_spec(exp_a), w_out_spec(exp_a),
                      w_in_spec(exp_b), w_out_spec(exp_b),
                      pl.BlockSpec((1, D_MODEL), const2)],
            out_specs=pl.BlockSpec((TMX * ROW_TILES, LANES), rows)),
        compiler_params=_params(("arbitrary",)),
        name="experts",
    )(ea, eb, nv, xs, router_w, router_b, wgu_b, wd_b, wgu_b, wd_b, g_post)


def _combine_kernel(dest_ref, ys_ref, x1_ref, mod_ref, o_ref, ybuf, sems, *, tile0):
    i = pl.program_id(0)
    n = pl.num_programs(0)

    def fetch(tile, slot):
        base = (tile + tile0) * TM

        def start(r, carry):
            pltpu.make_async_copy(_token_rows(ys_ref, dest_ref[base + r]), _token_rows(ybuf.at[slot], r),
                                  sems.at[slot]).start()
            return carry

        lax.fori_loop(0, TM, start, 0, unroll=8)

    @pl.when(i == 0)
    def _():
        fetch(0, 0)

    @pl.when(i + 1 < n)
    def _():
        fetch(i + 1, (i + 1) % 2)

    slot = i % 2
    pltpu.make_async_copy(ys_ref.at[pl.ds(0, TM * ROW_TILES), :], ybuf.at[slot], sems.at[slot]).wait()
    gate2 = mod_ref[0, :, 5 * D_MODEL:6 * D_MODEL]
    o_ref[...] = x1_ref[...] + gate2 * _load_token_major(ybuf.at[slot], TM)


def _combine(dest, ys, x1, mod3, cond_of_tile, tile0):
    n_tok = x1.shape[0]
    return pl.pallas_call(
        functools.partial(_combine_kernel, tile0=tile0),
        out_shape=jax.ShapeDtypeStruct((n_tok, D_MODEL), F32),
        grid_spec=pltpu.PrefetchScalarGridSpec(
            num_scalar_prefetch=1,
            grid=(n_tok // TM,),
            in_specs=[pl.BlockSpec(memory_space=pl.ANY),
                      pl.BlockSpec((TM, D_MODEL), lambda i, d: (i, 0)),
                      pl.BlockSpec((1, 1, 6 * D_MODEL), lambda i, d: (cond_of_tile(i), 0, 0))],
            out_specs=pl.BlockSpec((TM, D_MODEL), lambda i, d: (i, 0)),
            scratch_shapes=[pltpu.VMEM((2, TM * ROW_TILES, LANES), F32), pltpu.SemaphoreType.DMA((2,))]),
        compiler_params=_params(("arbitrary",)),
        name="combine",
    )(dest, ys, x1, mod3)


def _schedule(route, counts):
    n_tok = route.shape[0]
    n_max = n_tok // TMX + N_BUCKETS
    bucket = route[:, 0]
    rank = route[:, 1]
    cnt = counts[0, :N_BUCKETS].astype(jnp.int32)
    tiles = (cnt + TMX - 1) // TMX
    tile_end = jnp.cumsum(tiles)
    tile_start = tile_end - tiles
    ids = jnp.arange(N_BUCKETS, dtype=jnp.int32)
    slot0 = jnp.sum(jnp.where(bucket[:, None] == ids[None, :], (tile_start * TMX)[None, :], 0), axis=1)
    dest = slot0 + rank
    i = jnp.arange(n_max, dtype=jnp.int32)
    total = tile_end[-1]
    valid = i < total
    tb = jnp.sum((jnp.minimum(i, total - 1)[:, None] >= tile_end[None, :]).astype(jnp.int32), axis=1)
    pairs = [(a, b) for a in range(EXPERTS_PER_GROUP) for b in range(a + 1, EXPERTS_PER_GROUP)]
    ea_tab = jnp.array([g * EXPERTS_PER_GROUP + a for g in range(N_GROUPS) for a, _ in pairs], jnp.int32)
    eb_tab = jnp.array([g * EXPERTS_PER_GROUP + b for g in range(N_GROUPS) for _, b in pairs], jnp.int32)
    ea = ea_tab[tb]
    eb = eb_tab[tb]
    nv = jnp.where(valid, jnp.clip(cnt[tb] - (i - tile_start[tb]) * TMX, 0, TMX), 0)
    return dest, (ea, eb, nv), n_max * TMX


def _block_diag_gates(rg_wa, rg_wx):
    heads = LANES // HEAD_RNN

    def bd(w):
        w = w.reshape(2, RNN_BLOCKS, heads, HEAD_RNN, HEAD_RNN)
        eye = jnp.eye(heads, dtype=w.dtype)
        full = jnp.einsum('dghij,hk->dghikj', w, eye)
        return full.reshape(2, RNN_BLOCKS, LANES, LANES)

    return jnp.concatenate([bd(rg_wa), bd(rg_wx)], axis=-1).astype(BF16)


def _row_tile(v):
    blocks = v.reshape(v.shape[:-1] + (RNN_BLOCKS, LANES))
    return jnp.concatenate([blocks] * PAIR, axis=-2)


def _to_time_major_state(h):
    return h.reshape(h.shape[0] // PAIR, TMJ_ROWS, LANES)


def kernel(x_prompt, x_sample, state_rglru, c, c_ctx, w_mod, b_mod, g_pre_mix, g_post_mix, g_pre_ffn,
           g_post_ffn, w_in, conv_w, conv_b, rg_wa, rg_ba, rg_wx, rg_bx, rg_lambda, sgu_g, sgu_w, sgu_b,
           w_out, router_g_w, router_g_b, router_e_w, router_e_b, exp_w_gate, exp_w_up, exp_w_down):
    assert w_mod.shape[0] == 1, "single-layer trunk"
    n_ctx, ctx_len, _ = x_prompt.shape
    n_dec, dec_len, _ = x_sample.shape
    l = 0

    n_cond = SUBLANES
    assert n_dec % PAIR == 0 and n_dec + PAIR <= n_cond
    cond = jnp.zeros((n_cond, D_MODEL), F32).at[:n_dec].set(c).at[n_dec:n_dec + PAIR].set(c_ctx)
    mod3 = _modulation(cond, w_mod[l], b_mod[l]).reshape(n_cond, 1, 6 * D_MODEL)
    pos_tab = _pos_table()

    w_in_b = w_in[l].astype(BF16)
    w_out_b = w_out[l].astype(BF16)
    sgu_w_b = sgu_w[l].reshape(2, 4 * CHUNK, CHUNK).astype(BF16)
    sgu_bias_tile = jnp.repeat(sgu_b[l].T, HEAD_SGU, axis=1)
    w_gates = _block_diag_gates(rg_wa[l], rg_wx[l])
    b_gates = jnp.stack([_row_tile(rg_ba[l]), _row_tile(rg_bx[l])], axis=1)
    lam = _row_tile(rg_lambda[l])
    conv_w_t = _row_tile(conv_w[l])
    conv_b_t = _row_tile(conv_b[l])
    router_w = jnp.zeros((D_MODEL, ROUTER_LANES), F32)
    router_w = router_w.at[:, :N_GROUPS].set(router_g_w[l]).at[:, E_LANE0:E_LANE0 + N_EXPERTS].set(router_e_w[l])
    router_w = router_w.astype(BF16)
    router_b = jnp.zeros((1, ROUTER_LANES), F32)
    router_b = router_b.at[0, :N_GROUPS].set(router_g_b[l]).at[0, E_LANE0:E_LANE0 + N_EXPERTS].set(router_e_b[l])
    wgu_b = jnp.concatenate([exp_w_gate[l].astype(BF16), exp_w_up[l].astype(BF16)], axis=-1)
    wd_b = exp_w_down[l].astype(BF16)
    row = lambda v: v.reshape(1, -1)

    n_ctx_tok = n_ctx * ctx_len

    def mixer(x, h0, cond_of_tile, cond_block, use_pos, counts0):
        n_seq, seq_len, _ = x.shape
        xf = x.reshape(n_seq * seq_len, D_MODEL)
        tab = pos_tab if use_pos else None
        xr, gg, y_sgu = _premix(x, mod3, cond_block, row(g_pre_mix[l]), w_in_b, row(sgu_g[l]),
                                sgu_w_b, sgu_bias_tile, tab)
        scan_params = (conv_w_t, conv_b_t, w_gates, b_gates, lam)
        hf, hf_last = _scan(xr, None, None, _to_time_major_state(h0[:, 0]), *scan_params, direction=0)
        y_rnn, hb_first = _scan(xr, gg, hf, _to_time_major_state(h0[:, 1]), *scan_params, direction=1)
        fstate = jnp.stack([hf_last.reshape(n_seq, D_RNN), hb_first.reshape(n_seq, D_RNN)], axis=1)
        x1, hn, route, counts = _postmix(
            xf, y_rnn.reshape(n_seq * seq_len, D_RNN), y_sgu.reshape(n_seq * seq_len, D_SGU), mod3,
            cond_of_tile, row(g_post_mix[l]), row(g_pre_ffn[l]), w_out_b, router_w, router_b, tab, counts0)
        return x1, hn, route, counts, fstate

    tiles_per_seq = dec_len // TM
    ctx_cond = lambda i: n_dec
    dec_cond = lambda i: i // tiles_per_seq
    h0_ctx = jnp.zeros((n_ctx, 2, D_RNN), F32)
    counts0 = jnp.zeros((1, ROUTER_LANES), F32)
    x1_ctx, hn_ctx, route_ctx, counts, st = mixer(x_prompt, h0_ctx, ctx_cond, lambda p: n_dec // PAIR, False,
                                                  counts0)
    new_state = st.astype(state_rglru.dtype)[:, None]
    x1_dec, hn_dec, route_dec, counts, _ = mixer(x_sample, state_rglru[:, l].astype(F32), dec_cond,
                                                 lambda p: p, True, counts)

    route = jnp.concatenate([route_ctx[:, :2], route_dec[:, :2]], axis=0)
    dest, sched, n_slots = _schedule(route, counts)
    xs = _dispatch(dest, hn_ctx, hn_dec, n_slots)
    ys = _experts(sched, xs, router_w, router_b, wgu_b, wd_b, row(g_post_ffn[l]))
    y_prompt = _combine(dest, ys, x1_ctx, mod3, ctx_cond, 0)
    y_sample = _combine(dest, ys, x1_dec, mod3, dec_cond, n_ctx_tok // TM)
    return (y_prompt.reshape(x_prompt.shape), y_sample.reshape(x_sample.shape), new_state)
```

```python
import functools
import math

import jax
import jax.numpy as jnp
from jax import lax
from jax.experimental import pallas as pl
from jax.experimental.pallas import tpu as pltpu

D_MODEL = 1024
D_RNN = 512
D_SGU = 512
N_HEADS_RNN = 8
HEAD_RNN = D_RNN // N_HEADS_RNN
N_HEADS_SGU = 8
HEAD_SGU = D_SGU // N_HEADS_SGU
CHUNK = 128
GRID_W = 64
RG_C = 8.0
N_GROUPS = 4
EXPERTS_PER_GROUP = 4
N_EXPERTS = N_GROUPS * EXPERTS_PER_GROUP
D_EXPERT = 512
EPS = 1e-6
POS_BASE = 10000.0

LANES = 128
SUBLANES = 8
CONV_W = 4
CONV_LEFT = 2
PAIR = 2
RNN_BLOCKS = D_RNN // LANES
TMJ_ROWS = PAIR * RNN_BLOCKS
ROUTER_LANES = LANES
E_LANE0 = N_GROUPS

PAIRS_PER_GROUP = EXPERTS_PER_GROUP * (EXPERTS_PER_GROUP - 1) // 2
N_BUCKETS = N_GROUPS * PAIRS_PER_GROUP

ROW_TILES = D_MODEL // LANES

TM = 512
TD = 512
DMA_UNROLL = 8
EXPERT_ROW_PARTS = 2
TILE_ROW_PARTS = 1
TMX = 512
TT = TM // PAIR
LC = 256
TS = 16
PB = 2
VMEM_LIMIT = 56 * 1024 * 1024

F32 = jnp.float32
BF16 = jnp.bfloat16


def _params(sem):
    return pltpu.CompilerParams(dimension_semantics=sem, vmem_limit_bytes=VMEM_LIMIT)


def _rms(x):
    return x * lax.rsqrt(jnp.mean(x * x, axis=-1, keepdims=True) + EPS)


def _sigmoid(x):
    return 0.5 * jnp.tanh(0.5 * x) + 0.5


def _mod_kernel(cond_ref, w_ref, b_ref, o_ref):
    c = cond_ref[...]
    s = c * _sigmoid(c)
    o_ref[...] = jnp.dot(s.astype(BF16), w_ref[...].astype(BF16),
                         preferred_element_type=F32) + b_ref[...]


def _modulation(cond, w_mod, b_mod):
    n = w_mod.shape[1]
    bn = 1024
    return pl.pallas_call(
        _mod_kernel,
        out_shape=jax.ShapeDtypeStruct((cond.shape[0], n), F32),
        grid=(n // bn,),
        in_specs=[pl.BlockSpec(cond.shape, lambda j: (0, 0)),
                  pl.BlockSpec((D_MODEL, bn), lambda j: (0, j)),
                  pl.BlockSpec((1, bn), lambda j: (0, j))],
        out_specs=pl.BlockSpec((cond.shape[0], bn), lambda j: (0, j)),
        compiler_params=_params(("arbitrary",)),
        name="modulation",
    )(cond, w_mod, b_mod.reshape(1, n))


def _pos_kernel(o_ref):
    n_freq = D_MODEL // 4
    k = lax.broadcasted_iota(jnp.int32, (GRID_W, n_freq), 1).astype(F32)
    p = lax.broadcasted_iota(jnp.int32, (GRID_W, n_freq), 0).astype(F32)
    freq = jnp.exp(-math.log(POS_BASE) * k / n_freq)
    ang = p * freq
    o_ref[:, 0:n_freq] = jnp.sin(ang)
    o_ref[:, n_freq:2 * n_freq] = jnp.cos(ang)


def _pos_table():
    return pl.pallas_call(
        _pos_kernel,
        out_shape=jax.ShapeDtypeStruct((GRID_W, D_MODEL // 2), F32),
        name="pos_table",
    )()


def _add_pos(x, pos_refs, q0):
    if pos_refs is None:
        return x
    rows_ref, cols_ref = pos_refs
    reps = x.shape[0] // GRID_W
    rpart = jnp.concatenate(
        [jnp.broadcast_to(rows_ref[q:q + 1, :], (GRID_W, D_MODEL // 2)) for q in range(q0, q0 + reps)], axis=0)
    cpart = jnp.concatenate([cols_ref[...]] * reps, axis=0)
    return jnp.concatenate([x[:, :D_MODEL // 2] + rpart, x[:, D_MODEL // 2:] + cpart], axis=1)


def _load_x(x_ref, pos_refs, r0, n):
    return _add_pos(x_ref[r0:r0 + n, :], pos_refs, r0 // GRID_W)


def _premix_kernel(*refs, add_pos):
    refs = list(refs)
    x_ref = refs.pop(0)
    pos_refs = (refs.pop(0), refs.pop(0)) if add_pos else None
    mod_ref, g_ref, win_ref, sgug_ref, sguw_ref, sgub_ref, xr_ref, gg_ref, ys_ref = refs
    hn = []
    for s in range(PAIR):
        shift = mod_ref[s, :, 0:D_MODEL]
        scale = mod_ref[s, :, D_MODEL:2 * D_MODEL]
        hn.append(_rms(_add_pos(x_ref[s], pos_refs, 0)) * (g_ref[...] * (1.0 + scale)) + shift)
    z = jnp.dot(jnp.concatenate(hn, axis=0).astype(BF16), win_ref[...],
                preferred_element_type=F32)
    half = D_SGU // 2
    heads_per_half = N_HEADS_SGU // 2
    lane_head = lax.broadcasted_iota(jnp.int32, (CHUNK, half), 1) // HEAD_SGU
    for s in range(PAIR):
        zs = z[s * TT:(s + 1) * TT]
        gg = jax.nn.gelu(zs[:, D_RNN:2 * D_RNN])
        for k in range(RNN_BLOCKS):
            rows = pl.ds(s * RNN_BLOCKS + k, TT, stride=TMJ_ROWS)
            xr_ref[rows, :] = zs[:, k * LANES:(k + 1) * LANES]
            gg_ref[rows, :] = gg[:, k * LANES:(k + 1) * LANES]
        u = zs[:, 2 * D_RNN:2 * D_RNN + D_SGU]
        vn = (_rms(zs[:, 2 * D_RNN + D_SGU:]) * sgug_ref[...]).astype(BF16)
        for c in range(TT // CHUNK):
            rows = slice(c * CHUNK, (c + 1) * CHUNK)
            halves = []
            for hf in range(2):
                r = jnp.dot(sguw_ref[hf], vn[rows, hf * half:(hf + 1) * half],
                            preferred_element_type=F32)
                sel = jnp.zeros((CHUNK, half), F32)
                for h in range(heads_per_half):
                    sel = jnp.where(lane_head == h, r[h * CHUNK:(h + 1) * CHUNK], sel)
                halves.append(sel)
            gatev = jnp.concatenate(halves, axis=1) + sgub_ref[...]
            ys_ref[s, rows, :] = (u[rows] * gatev).astype(BF16)


def _premix(x, mod3, cond_block, g_pre, w_in_b, sgu_g, sgu_w_b, sgu_bias_tile, pos_tab):
    n_seq, seq_len, _ = x.shape
    n_pairs, n_tiles = n_seq // PAIR, seq_len // TT
    add_pos = pos_tab is not None
    const2 = lambda p, j: (0, 0)
    in_specs = [pl.BlockSpec((PAIR, TT, D_MODEL), lambda p, j: (p, j, 0))]
    args = [x]
    if add_pos:
        reps = TT // GRID_W
        in_specs += [pl.BlockSpec((None, reps, D_MODEL // 2), lambda p, j: (j, 0, 0)),
                     pl.BlockSpec((GRID_W, D_MODEL // 2), const2)]
        args += [pos_tab.reshape(GRID_W // reps, reps, D_MODEL // 2), pos_tab]
    in_specs += [pl.BlockSpec((PAIR, 1, 6 * D_MODEL), lambda p, j: (cond_block(p), 0, 0)),
                 pl.BlockSpec((1, D_MODEL), const2),
                 pl.BlockSpec((D_MODEL, 2 * D_RNN + 2 * D_SGU), const2),
                 pl.BlockSpec((1, D_SGU), const2),
                 pl.BlockSpec((2, 4 * CHUNK, CHUNK), lambda p, j: (0, 0, 0)),
                 pl.BlockSpec((CHUNK, D_SGU), const2)]
    args += [mod3, g_pre, w_in_b, sgu_g, sgu_w_b, sgu_bias_tile]
    tmj = jax.ShapeDtypeStruct((n_pairs * seq_len * TMJ_ROWS, LANES), F32)
    tmj_spec = pl.BlockSpec((TT * TMJ_ROWS, LANES), lambda p, j: (p * n_tiles + j, 0))
    xr, gg, y_sgu = pl.pallas_call(
        functools.partial(_premix_kernel, add_pos=add_pos),
        out_shape=(tmj, tmj, jax.ShapeDtypeStruct((n_seq, seq_len, D_SGU), BF16)),
        grid=(n_pairs, n_tiles),
        in_specs=in_specs,
        out_specs=(tmj_spec, tmj_spec, pl.BlockSpec((PAIR, TT, D_SGU), lambda p, j: (p, j, 0))),
        compiler_params=_params(("parallel", "parallel")),
        name="premix",
    )(*args)
    shape4 = (n_pairs, seq_len, TMJ_ROWS, LANES)
    return xr.reshape(shape4), gg.reshape(shape4), y_sgu


def _scan_kernel(*refs, reverse, n_chunks):
    if reverse:
        (xprev_ref, x_ref, xnext_ref, gg_ref, hf_ref, h0_ref, cw_ref, cb_ref, wg_ref, bg_ref, lam_ref,
         y_ref, fs_ref, xwin, xc_s, r_s, i_s, a_s, b_s, y_s, hcar) = refs
    else:
        (xprev_ref, x_ref, xnext_ref, h0_ref, cw_ref, cb_ref, wg_ref, bg_ref, lam_ref,
         hf_ref, fs_ref, xwin, xc_s, r_s, i_s, a_s, b_s, hcar) = refs
    c = pl.program_id(1)
    chunk = n_chunks - 1 - c if reverse else c
    sub_rows = TS * TMJ_ROWS

    def rows_of(pb, t0, n_steps):
        return pl.ds(pl.multiple_of((pb * LC + t0) * TMJ_ROWS, TMJ_ROWS), n_steps * TMJ_ROWS)

    @pl.when(c == 0)
    def _():
        hcar[...] = h0_ref[...]

    xwin[:, 0:CONV_LEFT] = jnp.where(chunk > 0, xprev_ref[...], 0.0)
    xwin[:, LC + CONV_LEFT:LC + CONV_W - 1] = jnp.where(chunk < n_chunks - 1, xnext_ref[...], 0.0)

    def copy(i, carry):
        t0 = pl.multiple_of(i * TS, TS)
        xwin[:, pl.ds(t0 + CONV_LEFT, TS)] = x_ref[:, pl.ds(t0, TS)]
        return carry

    lax.fori_loop(0, LC // TS, copy, 0)

    def conv(i, carry):
        t0 = pl.multiple_of(i * TS, TS)
        xc = cb_ref[...] + cw_ref[0] * xwin[:, pl.ds(t0, TS)]
        for k in range(1, CONV_W):
            xc = xc + cw_ref[k] * xwin[:, pl.ds(t0 + k, TS)]
        for pb in range(PB):
            xc_s[rows_of(pb, t0, TS), :] = xc[pb].reshape(sub_rows, LANES)
        return carry

    lax.fori_loop(0, LC // TS, conv, 0)

    n_rows = PB * LC * PAIR
    for k in range(RNN_BLOCKS):
        rows = pl.ds(k, n_rows, stride=RNN_BLOCKS)
        g = jnp.dot(xc_s[rows, :].astype(BF16), wg_ref[k], preferred_element_type=F32)
        r_s[rows, :] = g[:, :LANES]
        i_s[rows, :] = g[:, LANES:]

    neg_lam = -lam_ref[...]
    softplus = jnp.maximum(neg_lam, 0.0) + jnp.log(1.0 + jnp.exp(-jnp.abs(neg_lam)))
    decay = -RG_C * softplus

    def gates(i, carry):
        t0 = pl.multiple_of(i * TS, TS)
        for pb in range(PB):
            rows = rows_of(pb, t0, TS)
            tile = lambda ref: ref[rows, :].reshape(TS, TMJ_ROWS, LANES)
            xc = tile(xc_s)
            r = _sigmoid(tile(r_s) + bg_ref[0])
            ig = _sigmoid(tile(i_s) + bg_ref[1])
            log_a = r * decay
            a = jnp.exp(log_a)
            q = jnp.tanh(-log_a) * (a * a + 1.0)
            b = jnp.where(q > 0.0, q * lax.rsqrt(q), 0.0) * (ig * xc)
            a_s[rows, :] = a.reshape(sub_rows, LANES)
            b_s[rows, :] = b.reshape(sub_rows, LANES)
        return carry

    lax.fori_loop(0, LC // TS, gates, 0)

    def step(j, hs):
        t = LC - 1 - j if reverse else j
        out = []
        for pb in range(PB):
            rows = rows_of(pb, t, 1)
            h = a_s[rows, :] * hs[pb] + b_s[rows, :]
            if reverse:
                y_s[rows, :] = (hf_ref[pb, t] + h) * gg_ref[pb, t]
            else:
                hf_ref[pb, t] = h
            out.append(h)
        return tuple(out)

    hs = lax.fori_loop(0, LC, step, tuple(hcar[pb] for pb in range(PB)), unroll=8)
    for pb in range(PB):
        hcar[pb] = hs[pb]
        fs_ref[pb] = hs[pb]

    if reverse:
        for pb in range(PB):
            for s in range(PAIR):
                cols = [y_s[pl.ds(pb * LC * TMJ_ROWS + s * RNN_BLOCKS + k, LC, stride=TMJ_ROWS), :]
                        for k in range(RNN_BLOCKS)]
                y_ref[pb * PAIR + s] = jnp.concatenate(cols, axis=1).astype(BF16)


def _scan(xr, gg, hf, h0, conv_w, conv_b, w_gates, b_gates, lam, direction):
    n_pairs, seq_len = xr.shape[:2]
    n_chunks = seq_len // LC
    reverse = direction == 1
    pos = (lambda c: n_chunks - 1 - c) if reverse else (lambda c: c)
    tmj_blk = pl.BlockSpec((PB, LC, TMJ_ROWS, LANES), lambda i, c: (i, pos(c), 0, 0))
    state_blk = pl.BlockSpec((PB, TMJ_ROWS, LANES), lambda i, c: (i, 0, 0))
    per_dir = lambda *shape: pl.BlockSpec((None,) + shape, lambda i, c: (direction,) + (0,) * len(shape))
    in_specs = [
        pl.BlockSpec((PB, CONV_LEFT, TMJ_ROWS, LANES),
                     lambda i, c: (i, jnp.maximum(pos(c) * (LC // CONV_LEFT) - 1, 0), 0, 0)),
        tmj_blk,
        pl.BlockSpec((PB, 1, TMJ_ROWS, LANES), lambda i, c: (i, jnp.minimum((pos(c) + 1) * LC, seq_len - 1), 0, 0)),
    ]
    args = [xr, xr, xr]
    if reverse:
        in_specs += [tmj_blk, tmj_blk]
        args += [gg, hf]
    in_specs += [state_blk,
                 pl.BlockSpec((CONV_W, TMJ_ROWS, LANES), lambda i, c: (0, 0, 0)),
                 pl.BlockSpec((TMJ_ROWS, LANES), lambda i, c: (0, 0)),
                 per_dir(RNN_BLOCKS, LANES, 2 * LANES),
                 per_dir(2, TMJ_ROWS, LANES),
                 per_dir(TMJ_ROWS, LANES)]
    args += [h0, conv_w, conv_b, w_gates, b_gates, lam]
    flat = pltpu.VMEM((PB * LC * TMJ_ROWS, LANES), F32)
    scratch = [pltpu.VMEM((PB, LC + CONV_W - 1, TMJ_ROWS, LANES), F32)] + [flat] * (6 if reverse else 5)
    scratch += [pltpu.VMEM((PB, TMJ_ROWS, LANES), F32)]
    state = jax.ShapeDtypeStruct((n_pairs, TMJ_ROWS, LANES), F32)
    if reverse:
        out_shape = (jax.ShapeDtypeStruct((n_pairs * PAIR, seq_len, D_RNN), BF16), state)
        out_specs = (pl.BlockSpec((PB * PAIR, LC, D_RNN), lambda i, c: (i, pos(c), 0)), state_blk)
    else:
        out_shape = (jax.ShapeDtypeStruct(xr.shape, F32), state)
        out_specs = (tmj_blk, state_blk)
    return pl.pallas_call(
        functools.partial(_scan_kernel, reverse=reverse, n_chunks=n_chunks),
        out_shape=out_shape,
        grid=(n_pairs // PB, n_chunks),
        in_specs=in_specs,
        out_specs=out_specs,
        scratch_shapes=scratch,
        compiler_params=_params(("parallel", "arbitrary")),
        name="scan_bwd" if reverse else "scan_fwd",
    )(*args)


def _route(logits):
    lane = lax.broadcasted_iota(jnp.int32, logits.shape, 1)
    neg = jnp.float32(-jnp.inf)
    gmask = lane < N_GROUPS
    gl = jnp.where(gmask, logits, neg)
    gmax = jnp.max(gl, axis=-1, keepdims=True)
    g_idx = jnp.min(jnp.where(gl == gmax, lane, ROUTER_LANES), axis=-1, keepdims=True)
    e0 = E_LANE0 + EXPERTS_PER_GROUP * g_idx
    emask = (lane >= e0) & (lane < e0 + EXPERTS_PER_GROUP)
    el = jnp.where(emask, logits, neg)
    l1 = jnp.max(el, axis=-1, keepdims=True)
    i1 = jnp.min(jnp.where(el == l1, lane, ROUTER_LANES), axis=-1, keepdims=True)
    el2 = jnp.where(lane == i1, neg, el)
    l2 = jnp.max(el2, axis=-1, keepdims=True)
    i2 = jnp.min(jnp.where(emask & (lane != i1) & (el2 == l2), lane, ROUTER_LANES), axis=-1, keepdims=True)
    ja = jnp.minimum(i1, i2) - e0
    jb = jnp.maximum(i1, i2) - e0
    pair = (ja * (2 * EXPERTS_PER_GROUP - 1 - ja)) // 2 + (jb - ja - 1)
    return g_idx * PAIRS_PER_GROUP + pair


def _store_token_major(ref, x, t0=0):
    n = x.shape[0]
    for k in range(ROW_TILES):
        ref[pl.ds(t0 * ROW_TILES + k, n, stride=ROW_TILES), :] = x[:, k * LANES:(k + 1) * LANES]


def _load_token_major(ref, n):
    return jnp.concatenate([ref[pl.ds(k, n, stride=ROW_TILES), :] for k in range(ROW_TILES)], axis=1)


def _postmix_kernel(*refs, add_pos):
    refs = list(refs)
    x_ref = refs.pop(0)
    pos_refs = (refs.pop(0), refs.pop(0)) if add_pos else None
    (yr_ref, ys_ref, mod_ref, gpost_ref, gpre_ref, wout_ref, rw_ref, rb_ref, cnt0_ref,
     x1_ref, hn_ref, rt_ref, cnt_ref, run_ref) = refs

    @pl.when(pl.program_id(0) == 0)
    def _():
        run_ref[...] = cnt0_ref[...]

    gate1 = mod_ref[0, :, 2 * D_MODEL:3 * D_MODEL]
    shift2 = mod_ref[0, :, 3 * D_MODEL:4 * D_MODEL]
    scale2 = mod_ref[0, :, 4 * D_MODEL:5 * D_MODEL]
    gain1 = gate1 * gpost_ref[...]
    gain2 = gpre_ref[...] * (1.0 + scale2)
    part = TM // TILE_ROW_PARTS
    lane = lax.broadcasted_iota(jnp.int32, (part, ROUTER_LANES), 1)
    r_i = lax.broadcasted_iota(jnp.int32, (part, part), 0)
    c_i = lax.broadcasted_iota(jnp.int32, (part, part), 1)
    earlier = (c_i < r_i).astype(BF16)
    running = run_ref[...]
    for p in range(TILE_ROW_PARTS):
        r0 = p * part
        rows = slice(r0, r0 + part)
        y = (jnp.dot(yr_ref[rows, :], wout_ref[0:D_RNN, :], preferred_element_type=F32)
             + jnp.dot(ys_ref[rows, :], wout_ref[D_RNN:, :], preferred_element_type=F32))
        x1 = _load_x(x_ref, pos_refs, r0, part) + _rms(y) * gain1
        x1_ref[rows, :] = x1
        hn = _rms(x1) * gain2 + shift2
        _store_token_major(hn_ref, hn, r0)
        logits = jnp.dot(hn.astype(BF16), rw_ref[...], preferred_element_type=F32) + rb_ref[...]
        bucket = _route(logits)
        onehot = lane == bucket
        before = jnp.dot(earlier, onehot.astype(BF16), preferred_element_type=F32) + running
        rank = jnp.sum(jnp.where(onehot, before, 0.0), axis=-1, keepdims=True).astype(jnp.int32)
        rt_ref[rows, :] = jnp.where(lane == 0, bucket, jnp.where(lane == 1, rank, 0))
        running = running + jnp.sum(onehot.astype(F32), axis=0, keepdims=True)
    run_ref[...] = running
    cnt_ref[...] = running


def _postmix(x, y_rnn, y_sgu, mod3, cond_of_tile, g_post, g_pre, w_out_b, router_w, router_b, pos_tab,
             counts0):
    n_tok = x.shape[0]
    add_pos = pos_tab is not None
    tok = lambda i: (i, 0)
    const2 = lambda i: (0, 0)
    in_specs = [pl.BlockSpec((TM, D_MODEL), tok)]
    args = [x]
    if add_pos:
        reps = TM // GRID_W
        tiles_per_seq = GRID_W // reps
        in_specs += [pl.BlockSpec((None, reps, D_MODEL // 2), lambda i: (i % tiles_per_seq, 0, 0)),
                     pl.BlockSpec((GRID_W, D_MODEL // 2), const2)]
        args += [pos_tab.reshape(tiles_per_seq, reps, D_MODEL // 2), pos_tab]
    in_specs += [pl.BlockSpec((TM, D_RNN), tok),
                 pl.BlockSpec((TM, D_SGU), tok),
                 pl.BlockSpec((1, 1, 6 * D_MODEL), lambda i: (cond_of_tile(i), 0, 0)),
                 pl.BlockSpec((1, D_MODEL), const2),
                 pl.BlockSpec((1, D_MODEL), const2),
                 pl.BlockSpec((D_MODEL, D_MODEL), const2),
                 pl.BlockSpec((D_MODEL, ROUTER_LANES), const2),
                 pl.BlockSpec((1, ROUTER_LANES), const2),
                 pl.BlockSpec((1, ROUTER_LANES), const2)]
    args += [y_rnn, y_sgu, mod3, g_post, g_pre, w_out_b, router_w, router_b, counts0]
    return pl.pallas_call(
        functools.partial(_postmix_kernel, add_pos=add_pos),
        out_shape=(jax.ShapeDtypeStruct((n_tok, D_MODEL), F32),
                   jax.ShapeDtypeStruct((n_tok * ROW_TILES, LANES), F32),
                   jax.ShapeDtypeStruct((n_tok, ROUTER_LANES), jnp.int32),
                   jax.ShapeDtypeStruct((1, ROUTER_LANES), F32)),
        grid=(n_tok // TM,),
        in_specs=in_specs,
        out_specs=(pl.BlockSpec((TM, D_MODEL), tok),
                   pl.BlockSpec((TM * ROW_TILES, LANES), tok),
                   pl.BlockSpec((TM, ROUTER_LANES), tok),
                   pl.BlockSpec((1, ROUTER_LANES), const2)),
        scratch_shapes=[pltpu.VMEM((1, ROUTER_LANES), F32)],
        compiler_params=_params(("arbitrary",)),
        name="postmix",
    )(*args)


def _token_rows(ref, t):
    return ref.at[pl.ds(pl.multiple_of(t * ROW_TILES, ROW_TILES), ROW_TILES), :]


def _dispatch_kernel(dest_ref, hc_ref, hs_ref, xs_ref, sem, *, n_ctx_steps):
    i = pl.program_id(0)
    base = i * TD

    def scatter(src_ref):
        def start(g, carry):
            for u in range(DMA_UNROLL):
                r = g * DMA_UNROLL + u
                pltpu.make_async_copy(_token_rows(src_ref, r), _token_rows(xs_ref, dest_ref[base + r]),
                                      sem).start(priority=u % 2)
            return carry

        lax.fori_loop(0, TD // DMA_UNROLL, start, 0)
        pltpu.make_async_copy(src_ref, xs_ref.at[pl.ds(0, TD * ROW_TILES), :], sem).wait()

    @pl.when(i < n_ctx_steps)
    def _():
        scatter(hc_ref)

    @pl.when(i >= n_ctx_steps)
    def _():
        scatter(hs_ref)


def _dispatch(dest, hn_ctx, hn_dec, n_slots):
    n_ctx_steps = hn_ctx.shape[0] // (TD * ROW_TILES)
    n_dec_steps = hn_dec.shape[0] // (TD * ROW_TILES)
    return pl.pallas_call(
        functools.partial(_dispatch_kernel, n_ctx_steps=n_ctx_steps),
        out_shape=jax.ShapeDtypeStruct((n_slots * ROW_TILES, LANES), F32),
        grid_spec=pltpu.PrefetchScalarGridSpec(
            num_scalar_prefetch=1,
            grid=(n_ctx_steps + n_dec_steps,),
            in_specs=[pl.BlockSpec((TD * ROW_TILES, LANES), lambda i, d: (jnp.minimum(i, n_ctx_steps - 1), 0)),
                      pl.BlockSpec((TD * ROW_TILES, LANES), lambda i, d: (jnp.maximum(i - n_ctx_steps, 0), 0))],
            out_specs=pl.BlockSpec(memory_space=pl.ANY),
            scratch_shapes=[pltpu.SemaphoreType.DMA(())]),
        compiler_params=_params(("arbitrary",)),
        name="dispatch",
    )(dest, hn_ctx, hn_dec)


def _experts_kernel(ea_ref, eb_ref, nv_ref, xs_ref, rw_ref, rb_ref,
                    wga_ref, wua_ref, wda_ref, wgb_ref, wub_ref, wdb_ref, gpost_ref, ys_ref):
    i = pl.program_id(0)
    nv = nv_ref[i]

    @pl.when(nv == 0)
    def _():
        ys_ref[...] = jnp.zeros_like(ys_ref)

    @pl.when(nv > 0)
    def _():
        row = lax.broadcasted_iota(jnp.int32, (TMX, 1), 0)
        xb = jnp.where(row < nv, _load_token_major(xs_ref, TMX), 0.0).astype(BF16)
        logits = jnp.dot(xb, rw_ref[...], preferred_element_type=F32) + rb_ref[...]
        lane = lax.broadcasted_iota(jnp.int32, logits.shape, 1)
        ea = ea_ref[i]
        eb = eb_ref[i]
        gmask = lane < N_GROUPS
        gl = jnp.where(gmask, logits, -jnp.inf)
        gmax = jnp.max(gl, axis=-1, keepdims=True)
        gexp = jnp.where(gmask, jnp.exp(gl - gmax), 0.0)
        g_own = jnp.sum(jnp.where(lane == ea // EXPERTS_PER_GROUP, gexp, 0.0), axis=-1, keepdims=True)
        g_w = g_own / jnp.sum(gexp, axis=-1, keepdims=True)
        la = jnp.sum(jnp.where(lane == ea + E_LANE0, logits, 0.0), axis=-1, keepdims=True)
        lb = jnp.sum(jnp.where(lane == eb + E_LANE0, logits, 0.0), axis=-1, keepdims=True)
        m = jnp.maximum(la, lb)
        pa = jnp.exp(la - m)
        pb = jnp.exp(lb - m)
        inv = g_w / (pa + pb)

        def hidden(x, wg_ref, wu_ref, w):
            g = jnp.dot(x, wg_ref[0], preferred_element_type=F32)
            u = jnp.dot(x, wu_ref[0], preferred_element_type=F32)
            return ((g * _sigmoid(g)) * u * w).astype(BF16)

        part = TMX // EXPERT_ROW_PARTS
        ys = []
        for h in range(EXPERT_ROW_PARTS):
            rows = slice(h * part, (h + 1) * part)
            act_a = hidden(xb[rows], wga_ref, wua_ref, (pa * inv)[rows])
            act_b = hidden(xb[rows], wgb_ref, wub_ref, (pb * inv)[rows])
            y = (jnp.dot(act_a, wda_ref[0], preferred_element_type=F32)
                 + jnp.dot(act_b, wdb_ref[0], preferred_element_type=F32))
            ys.append(_rms(y) * gpost_ref[...])
        _store_token_major(ys_ref, jnp.concatenate(ys, axis=0))


def _experts(sched, xs, router_w, router_b, wg_b, wu_b, wd_b, g_post):
    ea, eb, nv = sched
    n_tiles = ea.shape[0]
    rows = lambda i, ea, eb, nv: (i, 0)
    const2 = lambda i, ea, eb, nv: (0, 0)
    exp_a = lambda i, ea, eb, nv: (ea[i], 0, 0)
    exp_b = lambda i, ea, eb, nv: (eb[i], 0, 0)
    w_in_spec = lambda m: pl.BlockSpec((1, D_MODEL, D_EXPERT), m)
    w_out_spec = lambda m: pl.BlockSpec((1, D_EXPERT, D_MODEL), m)
    return pl.pallas_call(
        _experts_kernel,
        out_shape=jax.ShapeDtypeStruct(xs.shape, F32),
        grid_spec=pltpu.PrefetchScalarGridSpec(
            num_scalar_prefetch=3,
            grid=(n_tiles,),
            in_specs=[pl.BlockSpec((TMX * ROW_TILES, LANES), rows),
                      pl.BlockSpec((D_MODEL, ROUTER_LANES), const2),
                      pl.BlockSpec((1, ROUTER_LANES), const2),
                      w_in_spec(exp_a), w_in_spec(exp_a), w_out_spec(exp_a),
                      w_in_spec(exp_b), w_in_spec(exp_b), w_out_spec(exp_b),
                      pl.BlockSpec((1, D_MODEL), const2)],
            out_specs=pl.BlockSpec((TMX * ROW_TILES, LANES), rows)),
        compiler_params=_params(("arbitrary",)),
        name="experts",
    )(ea, eb, nv, xs, router_w, router_b, wg_b, wu_b, wd_b, wg_b, wu_b, wd_b, g_post)


def _combine_kernel(dest_ref, ys_ref, x1_ref, mod_ref, o_ref, ybuf, sems, *, tile0):
    i = pl.program_id(0)
    n = pl.num_programs(0)

    def fetch(tile, slot):
        base = (tile + tile0) * TM

        def start(g, carry):
            for u in range(DMA_UNROLL):
                r = g * DMA_UNROLL + u
                pltpu.make_async_copy(_token_rows(ys_ref, dest_ref[base + r]), _token_rows(ybuf.at[slot], r),
                                      sems.at[slot]).start(priority=u % 2)
            return carry

        lax.fori_loop(0, TM // DMA_UNROLL, start, 0)

    @pl.when(i == 0)
    def _():
        fetch(0, 0)

    @pl.when(i + 1 < n)
    def _():
        fetch(i + 1, (i + 1) % 2)

    slot = i % 2
    pltpu.make_async_copy(ys_ref.at[pl.ds(0, TM * ROW_TILES), :], ybuf.at[slot], sems.at[slot]).wait()
    gate2 = mod_ref[0, :, 5 * D_MODEL:6 * D_MODEL]
    o_ref[...] = x1_ref[...] + gate2 * _load_token_major(ybuf.at[slot], TM)


def _combine(dest, ys, x1, mod3, cond_of_tile, tile0):
    n_tok = x1.shape[0]
    return pl.pallas_call(
        functools.partial(_combine_kernel, tile0=tile0),
        out_shape=jax.ShapeDtypeStruct((n_tok, D_MODEL), F32),
        grid_spec=pltpu.PrefetchScalarGridSpec(
            num_scalar_prefetch=1,
            grid=(n_tok // TM,),
            in_specs=[pl.BlockSpec(memory_space=pl.ANY),
                      pl.BlockSpec((TM, D_MODEL), lambda i, d: (i, 0)),
                      pl.BlockSpec((1, 1, 6 * D_MODEL), lambda i, d: (cond_of_tile(i), 0, 0))],
            out_specs=pl.BlockSpec((TM, D_MODEL), lambda i, d: (i, 0)),
            scratch_shapes=[pltpu.VMEM((2, TM * ROW_TILES, LANES), F32), pltpu.SemaphoreType.DMA((2,))]),
        compiler_params=_params(("arbitrary",)),
        name="combine",
    )(dest, ys, x1, mod3)


def _schedule(route, counts):
    n_tok = route.shape[0]
    n_max = n_tok // TMX + N_BUCKETS
    bucket = route[:, 0]
    rank = route[:, 1]
    cnt = counts[0, :N_BUCKETS].astype(jnp.int32)
    tiles = (cnt + TMX - 1) // TMX
    tile_end = jnp.cumsum(tiles)
    tile_start = tile_end - tiles
    ids = jnp.arange(N_BUCKETS, dtype=jnp.int32)
    slot0 = jnp.sum(jnp.where(bucket[:, None] == ids[None, :], (tile_start * TMX)[None, :], 0), axis=1)
    dest = slot0 + rank
    i = jnp.arange(n_max, dtype=jnp.int32)
    total = tile_end[-1]
    valid = i < total
    tb = jnp.sum((jnp.minimum(i, total - 1)[:, None] >= tile_end[None, :]).astype(jnp.int32), axis=1)
    pairs = [(a, b) for a in range(EXPERTS_PER_GROUP) for b in range(a + 1, EXPERTS_PER_GROUP)]
    ea_tab = jnp.array([g * EXPERTS_PER_GROUP + a for g in range(N_GROUPS) for a, _ in pairs], jnp.int32)
    eb_tab = jnp.array([g * EXPERTS_PER_GROUP + b for g in range(N_GROUPS) for _, b in pairs], jnp.int32)
    ea = ea_tab[tb]
    eb = eb_tab[tb]
    nv = jnp.where(valid, jnp.clip(cnt[tb] - (i - tile_start[tb]) * TMX, 0, TMX), 0)
    return dest, (ea, eb, nv), n_max * TMX


def _block_diag_gates(rg_wa, rg_wx):
    heads = LANES // HEAD_RNN

    def bd(w):
        w = w.reshape(2, RNN_BLOCKS, heads, HEAD_RNN, HEAD_RNN)
        eye = jnp.eye(heads, dtype=w.dtype)
        full = jnp.einsum('dghij,hk->dghikj', w, eye)
        return full.reshape(2, RNN_BLOCKS, LANES, LANES)

    return jnp.concatenate([bd(rg_wa), bd(rg_wx)], axis=-1).astype(BF16)


def _row_tile(v):
    blocks = v.reshape(v.shape[:-1] + (RNN_BLOCKS, LANES))
    return jnp.concatenate([blocks] * PAIR, axis=-2)


def _to_time_major_state(h):
    return h.reshape(h.shape[0] // PAIR, TMJ_ROWS, LANES)


def kernel(x_prompt, x_sample, state_rglru, c, c_ctx, w_mod, b_mod, g_pre_mix, g_post_mix, g_pre_ffn,
           g_post_ffn, w_in, conv_w, conv_b, rg_wa, rg_ba, rg_wx, rg_bx, rg_lambda, sgu_g, sgu_w, sgu_b,
           w_out, router_g_w, router_g_b, router_e_w, router_e_b, exp_w_gate, exp_w_up, exp_w_down):
    assert w_mod.shape[0] == 1, "single-layer trunk"
    n_ctx, ctx_len, _ = x_prompt.shape
    n_dec, dec_len, _ = x_sample.shape
    l = 0

    n_cond = SUBLANES
    assert n_dec % PAIR == 0 and n_dec + PAIR <= n_cond
    cond = jnp.zeros((n_cond, D_MODEL), F32).at[:n_dec].set(c).at[n_dec:n_dec + PAIR].set(c_ctx)
    mod3 = _modulation(cond, w_mod[l], b_mod[l]).reshape(n_cond, 1, 6 * D_MODEL)
    pos_tab = _pos_table()

    w_in_b = w_in[l].astype(BF16)
    w_out_b = w_out[l].astype(BF16)
    sgu_w_b = sgu_w[l].reshape(2, 4 * CHUNK, CHUNK).astype(BF16)
    sgu_bias_tile = jnp.repeat(sgu_b[l].T, HEAD_SGU, axis=1)
    w_gates = _block_diag_gates(rg_wa[l], rg_wx[l])
    b_gates = jnp.stack([_row_tile(rg_ba[l]), _row_tile(rg_bx[l])], axis=1)
    lam = _row_tile(rg_lambda[l])
    conv_w_t = _row_tile(conv_w[l])
    conv_b_t = _row_tile(conv_b[l])
    router_w = jnp.zeros((D_MODEL, ROUTER_LANES), F32)
    router_w = router_w.at[:, :N_GROUPS].set(router_g_w[l]).at[:, E_LANE0:E_LANE0 + N_EXPERTS].set(router_e_w[l])
    router_w = router_w.astype(BF16)
    router_b = jnp.zeros((1, ROUTER_LANES), F32)
    router_b = router_b.at[0, :N_GROUPS].set(router_g_b[l]).at[0, E_LANE0:E_LANE0 + N_EXPERTS].set(router_e_b[l])
    wg_b = exp_w_gate[l].astype(BF16)
    wu_b = exp_w_up[l].astype(BF16)
    wd_b = exp_w_down[l].astype(BF16)
    row = lambda v: v.reshape(1, -1)

    n_ctx_tok = n_ctx * ctx_len

    def mixer(x, h0, cond_of_tile, cond_block, use_pos, counts0):
        n_seq, seq_len, _ = x.shape
        xf = x.reshape(n_seq * seq_len, D_MODEL)
        tab = pos_tab if use_pos else None
        xr, gg, y_sgu = _premix(x, mod3, cond_block, row(g_pre_mix[l]), w_in_b, row(sgu_g[l]),
                                sgu_w_b, sgu_bias_tile, tab)
        scan_params = (conv_w_t, conv_b_t, w_gates, b_gates, lam)
        hf, hf_last = _scan(xr, None, None, _to_time_major_state(h0[:, 0]), *scan_params, direction=0)
        y_rnn, hb_first = _scan(xr, gg, hf, _to_time_major_state(h0[:, 1]), *scan_params, direction=1)
        fstate = jnp.stack([hf_last.reshape(n_seq, D_RNN), hb_first.reshape(n_seq, D_RNN)], axis=1)
        x1, hn, route, counts = _postmix(
            xf, y_rnn.reshape(n_seq * seq_len, D_RNN), y_sgu.reshape(n_seq * seq_len, D_SGU), mod3,
            cond_of_tile, row(g_post_mix[l]), row(g_pre_ffn[l]), w_out_b, router_w, router_b, tab, counts0)
        return x1, hn, route, counts, fstate

    tiles_per_seq = dec_len // TM
    ctx_cond = lambda i: n_dec
    dec_cond = lambda i: i // tiles_per_seq
    h0_ctx = jnp.zeros((n_ctx, 2, D_RNN), F32)
    counts0 = jnp.zeros((1, ROUTER_LANES), F32)
    x1_ctx, hn_ctx, route_ctx, counts, st = mixer(x_prompt, h0_ctx, ctx_cond, lambda p: n_dec // PAIR, False,
                                                  counts0)
    new_state = st.astype(state_rglru.dtype)[:, None]
    x1_dec, hn_dec, route_dec, counts, _ = mixer(x_sample, state_rglru[:, l].astype(F32), dec_cond,
                                                 lambda p: p, True, counts)

    route = jnp.concatenate([route_ctx[:, :2], route_dec[:, :2]], axis=0)
    dest, sched, n_slots = _schedule(route, counts)
    xs = _dispatch(dest, hn_ctx, hn_dec, n_slots)
    ys = _experts(sched, xs, router_w, router_b, wg_b, wu_b, wd_b, row(g_post_ffn[l]))
    y_prompt = _combine(dest, ys, x1_ctx, mod3, ctx_cond, 0)
    y_sample = _combine(dest, ys, x1_dec, mod3, dec_cond, n_ctx_tok // TM)
    return (y_prompt.reshape(x_prompt.shape), y_sample.reshape(x_sample.shape), new_state)
```

```python
import functools
import math

import jax
import jax.numpy as jnp
from jax import lax
from jax.experimental import pallas as pl
from jax.experimental.pallas import tpu as pltpu

D_MODEL = 1024
D_RNN = 512
D_SGU = 512
N_HEADS_RNN = 8
HEAD_RNN = D_RNN // N_HEADS_RNN
N_HEADS_SGU = 8
HEAD_SGU = D_SGU // N_HEADS_SGU
CHUNK = 128
GRID_W = 64
RG_C = 8.0
N_GROUPS = 4
EXPERTS_PER_GROUP = 4
N_EXPERTS = N_GROUPS * EXPERTS_PER_GROUP
D_EXPERT = 512
EPS = 1e-6
POS_BASE = 10000.0

LANES = 128
SUBLANES = 8
CONV_W = 4
CONV_LEFT = 2
PAIR = 2
RNN_BLOCKS = D_RNN // LANES
TMJ_ROWS = PAIR * RNN_BLOCKS
ROUTER_LANES = LANES
E_LANE0 = N_GROUPS
ROUTER_ROWS = 32
E_ROW0 = SUBLANES

PAIRS_PER_GROUP = EXPERTS_PER_GROUP * (EXPERTS_PER_GROUP - 1) // 2
N_BUCKETS = N_GROUPS * PAIRS_PER_GROUP

ROW_TILES = D_MODEL // LANES

TM = 512
TD = 1024
DMA_UNROLL = 8
EXPERT_ROW_PARTS = 2
TILE_ROW_PARTS = 1
TMX = 512
TT = TM // PAIR
LC = 256
TS = 16
PB = 2
VMEM_LIMIT = 56 * 1024 * 1024

F32 = jnp.float32
BF16 = jnp.bfloat16


def _params(sem):
    return pltpu.CompilerParams(dimension_semantics=sem, vmem_limit_bytes=VMEM_LIMIT)


def _rms(x):
    return x * lax.rsqrt(jnp.mean(x * x, axis=-1, keepdims=True) + EPS)


def _sigmoid(x):
    return 0.5 * jnp.tanh(0.5 * x) + 0.5


def _mod_kernel(cond_ref, w_ref, b_ref, o_ref):
    c = cond_ref[...]
    s = c * _sigmoid(c)
    o_ref[...] = jnp.dot(s.astype(BF16), w_ref[...].astype(BF16),
                         preferred_element_type=F32) + b_ref[...]


def _modulation(cond, w_mod, b_mod):
    n = w_mod.shape[1]
    bn = 1024
    return pl.pallas_call(
        _mod_kernel,
        out_shape=jax.ShapeDtypeStruct((cond.shape[0], n), F32),
        grid=(n // bn,),
        in_specs=[pl.BlockSpec(cond.shape, lambda j: (0, 0)),
                  pl.BlockSpec((D_MODEL, bn), lambda j: (0, j)),
                  pl.BlockSpec((1, bn), lambda j: (0, j))],
        out_specs=pl.BlockSpec((cond.shape[0], bn), lambda j: (0, j)),
        compiler_params=_params(("arbitrary",)),
        name="modulation",
    )(cond, w_mod, b_mod.reshape(1, n))


def _pos_kernel(o_ref):
    n_freq = D_MODEL // 4
    k = lax.broadcasted_iota(jnp.int32, (GRID_W, n_freq), 1).astype(F32)
    p = lax.broadcasted_iota(jnp.int32, (GRID_W, n_freq), 0).astype(F32)
    freq = jnp.exp(-math.log(POS_BASE) * k / n_freq)
    ang = p * freq
    o_ref[:, 0:n_freq] = jnp.sin(ang)
    o_ref[:, n_freq:2 * n_freq] = jnp.cos(ang)


def _pos_table():
    return pl.pallas_call(
        _pos_kernel,
        out_shape=jax.ShapeDtypeStruct((GRID_W, D_MODEL // 2), F32),
        name="pos_table",
    )()


def _add_pos(x, pos_refs, q0):
    if pos_refs is None:
        return x
    rows_ref, cols_ref = pos_refs
    reps = x.shape[0] // GRID_W
    rpart = jnp.concatenate(
        [jnp.broadcast_to(rows_ref[q:q + 1, :], (GRID_W, D_MODEL // 2)) for q in range(q0, q0 + reps)], axis=0)
    cpart = jnp.concatenate([cols_ref[...]] * reps, axis=0)
    return jnp.concatenate([x[:, :D_MODEL // 2] + rpart, x[:, D_MODEL // 2:] + cpart], axis=1)


def _load_x(x_ref, pos_refs, r0, n):
    return _add_pos(x_ref[r0:r0 + n, :], pos_refs, r0 // GRID_W)


def _premix_kernel(*refs, add_pos):
    refs = list(refs)
    x_ref = refs.pop(0)
    pos_refs = (refs.pop(0), refs.pop(0)) if add_pos else None
    mod_ref, g_ref, win_ref, sgug_ref, sguw_ref, sgub_ref, xr_ref, gg_ref, ys_ref = refs
    hn = []
    for s in range(PAIR):
        shift = mod_ref[s, :, 0:D_MODEL]
        scale = mod_ref[s, :, D_MODEL:2 * D_MODEL]
        hn.append(_rms(_add_pos(x_ref[s], pos_refs, 0)) * (g_ref[...] * (1.0 + scale)) + shift)
    z = jnp.dot(jnp.concatenate(hn, axis=0).astype(BF16), win_ref[...],
                preferred_element_type=F32)
    half = D_SGU // 2
    heads_per_half = N_HEADS_SGU // 2
    lane_head = lax.broadcasted_iota(jnp.int32, (CHUNK, half), 1) // HEAD_SGU
    for s in range(PAIR):
        zs = z[s * TT:(s + 1) * TT]
        gg = jax.nn.gelu(zs[:, D_RNN:2 * D_RNN])
        for k in range(RNN_BLOCKS):
            rows = pl.ds(s * RNN_BLOCKS + k, TT, stride=TMJ_ROWS)
            xr_ref[rows, :] = zs[:, k * LANES:(k + 1) * LANES]
            gg_ref[rows, :] = gg[:, k * LANES:(k + 1) * LANES]
        u = zs[:, 2 * D_RNN:2 * D_RNN + D_SGU]
        vn = (_rms(zs[:, 2 * D_RNN + D_SGU:]) * sgug_ref[...]).astype(BF16)
        for c in range(TT // CHUNK):
            rows = slice(c * CHUNK, (c + 1) * CHUNK)
            halves = []
            for hf in range(2):
                r = jnp.dot(sguw_ref[hf], vn[rows, hf * half:(hf + 1) * half],
                            preferred_element_type=F32)
                sel = jnp.zeros((CHUNK, half), F32)
                for h in range(heads_per_half):
                    sel = jnp.where(lane_head == h, r[h * CHUNK:(h + 1) * CHUNK], sel)
                halves.append(sel)
            gatev = jnp.concatenate(halves, axis=1) + sgub_ref[...]
            ys_ref[s, rows, :] = (u[rows] * gatev).astype(BF16)


def _premix(x, mod3, cond_block, g_pre, w_in_b, sgu_g, sgu_w_b, sgu_bias_tile, pos_tab):
    n_seq, seq_len, _ = x.shape
    n_pairs, n_tiles = n_seq // PAIR, seq_len // TT
    add_pos = pos_tab is not None
    const2 = lambda p, j: (0, 0)
    in_specs = [pl.BlockSpec((PAIR, TT, D_MODEL), lambda p, j: (p, j, 0))]
    args = [x]
    if add_pos:
        reps = TT // GRID_W
        in_specs += [pl.BlockSpec((None, reps, D_MODEL // 2), lambda p, j: (j, 0, 0)),
                     pl.BlockSpec((GRID_W, D_MODEL // 2), const2)]
        args += [pos_tab.reshape(GRID_W // reps, reps, D_MODEL // 2), pos_tab]
    in_specs += [pl.BlockSpec((PAIR, 1, 6 * D_MODEL), lambda p, j: (cond_block(p), 0, 0)),
                 pl.BlockSpec((1, D_MODEL), const2),
                 pl.BlockSpec((D_MODEL, 2 * D_RNN + 2 * D_SGU), const2),
                 pl.BlockSpec((1, D_SGU), const2),
                 pl.BlockSpec((2, 4 * CHUNK, CHUNK), lambda p, j: (0, 0, 0)),
                 pl.BlockSpec((CHUNK, D_SGU), const2)]
    args += [mod3, g_pre, w_in_b, sgu_g, sgu_w_b, sgu_bias_tile]
    tmj = jax.ShapeDtypeStruct((n_pairs * seq_len * TMJ_ROWS, LANES), F32)
    tmj_spec = pl.BlockSpec((TT * TMJ_ROWS, LANES), lambda p, j: (p * n_tiles + j, 0))
    xr, gg, y_sgu = pl.pallas_call(
        functools.partial(_premix_kernel, add_pos=add_pos),
        out_shape=(tmj, tmj, jax.ShapeDtypeStruct((n_seq, seq_len, D_SGU), BF16)),
        grid=(n_pairs, n_tiles),
        in_specs=in_specs,
        out_specs=(tmj_spec, tmj_spec, pl.BlockSpec((PAIR, TT, D_SGU), lambda p, j: (p, j, 0))),
        compiler_params=_params(("parallel", "parallel")),
        name="premix",
    )(*args)
    shape4 = (n_pairs, seq_len, TMJ_ROWS, LANES)
    return xr.reshape(shape4), gg.reshape(shape4), y_sgu


def _scan_kernel(*refs, reverse, n_chunks):
    if reverse:
        (xprev_ref, x_ref, xnext_ref, gg_ref, hf_ref, h0_ref, cw_ref, cb_ref, wg_ref, bg_ref, lam_ref,
         y_ref, fs_ref, xwin, xc_s, r_s, i_s, a_s, b_s, y_s, hcar) = refs
    else:
        (xprev_ref, x_ref, xnext_ref, h0_ref, cw_ref, cb_ref, wg_ref, bg_ref, lam_ref,
         hf_ref, fs_ref, xwin, xc_s, r_s, i_s, a_s, b_s, hcar) = refs
    c = pl.program_id(1)
    chunk = n_chunks - 1 - c if reverse else c
    sub_rows = TS * TMJ_ROWS

    def rows_of(pb, t0, n_steps):
        return pl.ds(pl.multiple_of((pb * LC + t0) * TMJ_ROWS, TMJ_ROWS), n_steps * TMJ_ROWS)

    @pl.when(c == 0)
    def _():
        hcar[...] = h0_ref[...]

    xwin[:, 0:CONV_LEFT] = jnp.where(chunk > 0, xprev_ref[...], 0.0)
    xwin[:, LC + CONV_LEFT:LC + CONV_W - 1] = jnp.where(chunk < n_chunks - 1, xnext_ref[...], 0.0)

    def copy(i, carry):
        t0 = pl.multiple_of(i * TS, TS)
        xwin[:, pl.ds(t0 + CONV_LEFT, TS)] = x_ref[:, pl.ds(t0, TS)]
        return carry

    lax.fori_loop(0, LC // TS, copy, 0)

    def conv(i, carry):
        t0 = pl.multiple_of(i * TS, TS)
        xc = cb_ref[...] + cw_ref[0] * xwin[:, pl.ds(t0, TS)]
        for k in range(1, CONV_W):
            xc = xc + cw_ref[k] * xwin[:, pl.ds(t0 + k, TS)]
        for pb in range(PB):
            xc_s[rows_of(pb, t0, TS), :] = xc[pb].reshape(sub_rows, LANES)
        return carry

    lax.fori_loop(0, LC // TS, conv, 0)

    n_rows = PB * LC * PAIR
    for k in range(RNN_BLOCKS):
        rows = pl.ds(k, n_rows, stride=RNN_BLOCKS)
        g = jnp.dot(xc_s[rows, :].astype(BF16), wg_ref[k], preferred_element_type=F32)
        r_s[rows, :] = g[:, :LANES]
        i_s[rows, :] = g[:, LANES:]

    neg_lam = -lam_ref[...]
    softplus = jnp.maximum(neg_lam, 0.0) + jnp.log(1.0 + jnp.exp(-jnp.abs(neg_lam)))
    decay = -RG_C * softplus

    def gates(i, carry):
        t0 = pl.multiple_of(i * TS, TS)
        for pb in range(PB):
            rows = rows_of(pb, t0, TS)
            tile = lambda ref: ref[rows, :].reshape(TS, TMJ_ROWS, LANES)
            xc = tile(xc_s)
            r = _sigmoid(tile(r_s) + bg_ref[0])
            ig = _sigmoid(tile(i_s) + bg_ref[1])
            log_a = r * decay
            a = jnp.exp(log_a)
            q = jnp.tanh(-log_a) * (a * a + 1.0)
            b = jnp.where(q > 0.0, q * lax.rsqrt(q), 0.0) * (ig * xc)
            a_s[rows, :] = a.reshape(sub_rows, LANES)
            b_s[rows, :] = b.reshape(sub_rows, LANES)
        return carry

    lax.fori_loop(0, LC // TS, gates, 0)

    def step(j, hs):
        t = LC - 1 - j if reverse else j
        out = []
        for pb in range(PB):
            rows = rows_of(pb, t, 1)
            h = a_s[rows, :] * hs[pb] + b_s[rows, :]
            if reverse:
                y_s[rows, :] = (hf_ref[pb, t] + h) * gg_ref[pb, t]
            else:
                hf_ref[pb, t] = h
            out.append(h)
        return tuple(out)

    hs = lax.fori_loop(0, LC, step, tuple(hcar[pb] for pb in range(PB)), unroll=8)
    for pb in range(PB):
        hcar[pb] = hs[pb]
        fs_ref[pb] = hs[pb]

    if reverse:
        for pb in range(PB):
            for s in range(PAIR):
                cols = [y_s[pl.ds(pb * LC * TMJ_ROWS + s * RNN_BLOCKS + k, LC, stride=TMJ_ROWS), :]
                        for k in range(RNN_BLOCKS)]
                y_ref[pb * PAIR + s] = jnp.concatenate(cols, axis=1).astype(BF16)


def _scan(xr, gg, hf, h0, conv_w, conv_b, w_gates, b_gates, lam, direction):
    n_pairs, seq_len = xr.shape[:2]
    n_chunks = seq_len // LC
    reverse = direction == 1
    pos = (lambda c: n_chunks - 1 - c) if reverse else (lambda c: c)
    tmj_blk = pl.BlockSpec((PB, LC, TMJ_ROWS, LANES), lambda i, c: (i, pos(c), 0, 0))
    state_blk = pl.BlockSpec((PB, TMJ_ROWS, LANES), lambda i, c: (i, 0, 0))
    per_dir = lambda *shape: pl.BlockSpec((None,) + shape, lambda i, c: (direction,) + (0,) * len(shape))
    in_specs = [
        pl.BlockSpec((PB, CONV_LEFT, TMJ_ROWS, LANES),
                     lambda i, c: (i, jnp.maximum(pos(c) * (LC // CONV_LEFT) - 1, 0), 0, 0)),
        tmj_blk,
        pl.BlockSpec((PB, 1, TMJ_ROWS, LANES), lambda i, c: (i, jnp.minimum((pos(c) + 1) * LC, seq_len - 1), 0, 0)),
    ]
    args = [xr, xr, xr]
    if reverse:
        in_specs += [tmj_blk, tmj_blk]
        args += [gg, hf]
    in_specs += [state_blk,
                 pl.BlockSpec((CONV_W, TMJ_ROWS, LANES), lambda i, c: (0, 0, 0)),
                 pl.BlockSpec((TMJ_ROWS, LANES), lambda i, c: (0, 0)),
                 per_dir(RNN_BLOCKS, LANES, 2 * LANES),
                 per_dir(2, TMJ_ROWS, LANES),
                 per_dir(TMJ_ROWS, LANES)]
    args += [h0, conv_w, conv_b, w_gates, b_gates, lam]
    flat = pltpu.VMEM((PB * LC * TMJ_ROWS, LANES), F32)
    scratch = [pltpu.VMEM((PB, LC + CONV_W - 1, TMJ_ROWS, LANES), F32)] + [flat] * (6 if reverse else 5)
    scratch += [pltpu.VMEM((PB, TMJ_ROWS, LANES), F32)]
    state = jax.ShapeDtypeStruct((n_pairs, TMJ_ROWS, LANES), F32)
    if reverse:
        out_shape = (jax.ShapeDtypeStruct((n_pairs * PAIR, seq_len, D_RNN), BF16), state)
        out_specs = (pl.BlockSpec((PB * PAIR, LC, D_RNN), lambda i, c: (i, pos(c), 0)), state_blk)
    else:
        out_shape = (jax.ShapeDtypeStruct(xr.shape, F32), state)
        out_specs = (tmj_blk, state_blk)
    return pl.pallas_call(
        functools.partial(_scan_kernel, reverse=reverse, n_chunks=n_chunks),
        out_shape=out_shape,
        grid=(n_pairs // PB, n_chunks),
        in_specs=in_specs,
        out_specs=out_specs,
        scratch_shapes=scratch,
        compiler_params=_params(("parallel", "arbitrary")),
        name="scan_bwd" if reverse else "scan_fwd",
    )(*args)


def _route(lt):
    n = lt.shape[1]
    row = lax.broadcasted_iota(jnp.int32, (EXPERTS_PER_GROUP, n), 0)
    neg = jnp.float32(-jnp.inf)

    def arg_max(v):
        m = jnp.max(v, axis=0, keepdims=True)
        return jnp.min(jnp.where(v == m, row, EXPERTS_PER_GROUP), axis=0, keepdims=True)

    g_idx = arg_max(lt[0:N_GROUPS])
    el = lt[E_ROW0:E_ROW0 + EXPERTS_PER_GROUP]
    for g in range(1, N_GROUPS):
        first = E_ROW0 + g * EXPERTS_PER_GROUP
        el = jnp.where(g_idx == g, lt[first:first + EXPERTS_PER_GROUP], el)
    i1 = arg_max(el)
    i2 = arg_max(jnp.where(row == i1, neg, el))
    ja = jnp.minimum(i1, i2)
    jb = jnp.maximum(i1, i2)
    pair = (ja * (2 * EXPERTS_PER_GROUP - 1 - ja)) // 2 + (jb - ja - 1)
    return g_idx * PAIRS_PER_GROUP + pair


def _store_token_major(ref, x, t0=0):
    n = x.shape[0]
    for k in range(ROW_TILES):
        ref[pl.ds(t0 * ROW_TILES + k, n, stride=ROW_TILES), :] = x[:, k * LANES:(k + 1) * LANES]


def _load_token_major(ref, n):
    return jnp.concatenate([ref[pl.ds(k, n, stride=ROW_TILES), :] for k in range(ROW_TILES)], axis=1)


def _postmix_kernel(*refs, add_pos):
    refs = list(refs)
    x_ref = refs.pop(0)
    pos_refs = (refs.pop(0), refs.pop(0)) if add_pos else None
    (yr_ref, ys_ref, mod_ref, gpost_ref, gpre_ref, wout_ref, rw_ref, rb_ref, earlier_ref, cnt0_ref,
     x1_ref, hn_ref, rt_ref, cnt_ref, run_ref) = refs

    @pl.when(pl.program_id(0) == 0)
    def _():
        run_ref[...] = cnt0_ref[...]

    gate1 = mod_ref[0, :, 2 * D_MODEL:3 * D_MODEL]
    shift2 = mod_ref[0, :, 3 * D_MODEL:4 * D_MODEL]
    scale2 = mod_ref[0, :, 4 * D_MODEL:5 * D_MODEL]
    y = (jnp.dot(yr_ref[...], wout_ref[0:D_RNN, :], preferred_element_type=F32)
         + jnp.dot(ys_ref[...], wout_ref[D_RNN:, :], preferred_element_type=F32))
    x1 = _load_x(x_ref, pos_refs, 0, TM) + _rms(y) * (gate1 * gpost_ref[...])
    x1_ref[...] = x1
    hn = _rms(x1) * (gpre_ref[...] * (1.0 + scale2)) + shift2
    _store_token_major(hn_ref, hn)
    lt = lax.dot_general(rw_ref[...], hn.astype(BF16), (((1,), (1,)), ((), ())),
                         preferred_element_type=F32) + rb_ref[:, 0:1]
    bucket = _route(lt)
    onehot = lax.broadcasted_iota(jnp.int32, (ROUTER_ROWS, TM), 0) == bucket
    before = jnp.dot(onehot.astype(BF16), earlier_ref[...], preferred_element_type=F32) + run_ref[:, 0:1]
    rank = jnp.sum(jnp.where(onehot, before, 0.0), axis=0, keepdims=True).astype(jnp.int32)
    row = lax.broadcasted_iota(jnp.int32, (SUBLANES, TM), 0)
    rt_ref[...] = jnp.where(row == 0, bucket, jnp.where(row == 1, rank, 0))
    run_ref[...] += jnp.sum(onehot.astype(F32), axis=1, keepdims=True)
    cnt_ref[...] = run_ref[...]


def _postmix(x, y_rnn, y_sgu, mod3, cond_of_tile, g_post, g_pre, w_out_b, router_wt, router_bt, earlier, pos_tab,
             counts0):
    n_tok = x.shape[0]
    add_pos = pos_tab is not None
    tok = lambda i: (i, 0)
    const2 = lambda i: (0, 0)
    in_specs = [pl.BlockSpec((TM, D_MODEL), tok)]
    args = [x]
    if add_pos:
        reps = TM // GRID_W
        tiles_per_seq = GRID_W // reps
        in_specs += [pl.BlockSpec((None, reps, D_MODEL // 2), lambda i: (i % tiles_per_seq, 0, 0)),
                     pl.BlockSpec((GRID_W, D_MODEL // 2), const2)]
        args += [pos_tab.reshape(tiles_per_seq, reps, D_MODEL // 2), pos_tab]
    in_specs += [pl.BlockSpec((TM, D_RNN), tok),
                 pl.BlockSpec((TM, D_SGU), tok),
                 pl.BlockSpec((1, 1, 6 * D_MODEL), lambda i: (cond_of_tile(i), 0, 0)),
                 pl.BlockSpec((1, D_MODEL), const2),
                 pl.BlockSpec((1, D_MODEL), const2),
                 pl.BlockSpec((D_MODEL, D_MODEL), const2),
                 pl.BlockSpec((ROUTER_ROWS, D_MODEL), const2),
                 pl.BlockSpec((ROUTER_ROWS, LANES), const2),
                 pl.BlockSpec((TM, TM), const2),
                 pl.BlockSpec((ROUTER_ROWS, LANES), const2)]
    args += [y_rnn, y_sgu, mod3, g_post, g_pre, w_out_b, router_wt, router_bt, earlier, counts0]
    n_tiles = n_tok // TM
    counts_spec = pl.BlockSpec((ROUTER_ROWS, LANES), const2)
    x1, hn, route, counts = pl.pallas_call(
        functools.partial(_postmix_kernel, add_pos=add_pos),
        out_shape=(jax.ShapeDtypeStruct((n_tok, D_MODEL), F32),
                   jax.ShapeDtypeStruct((n_tok * ROW_TILES, LANES), F32),
                   jax.ShapeDtypeStruct((n_tiles * SUBLANES, TM), jnp.int32),
                   jax.ShapeDtypeStruct((ROUTER_ROWS, LANES), F32)),
        grid=(n_tiles,),
        in_specs=in_specs,
        out_specs=(pl.BlockSpec((TM, D_MODEL), tok),
                   pl.BlockSpec((TM * ROW_TILES, LANES), tok),
                   pl.BlockSpec((SUBLANES, TM), tok),
                   counts_spec),
        scratch_shapes=[pltpu.VMEM((ROUTER_ROWS, LANES), F32)],
        compiler_params=_params(("arbitrary",)),
        name="postmix",
    )(*args)
    route = route.reshape(n_tiles, SUBLANES, TM)
    return x1, hn, route[:, 0].reshape(n_tok), route[:, 1].reshape(n_tok), counts


def _token_rows(ref, t):
    return ref.at[pl.ds(pl.multiple_of(t * ROW_TILES, ROW_TILES), ROW_TILES), :]


def _dispatch_kernel(dest_ref, hc_ref, hs_ref, xs_ref, sem, *, n_ctx_steps):
    i = pl.program_id(0)
    base = i * TD

    def scatter(src_ref):
        def start(g, carry):
            for u in range(DMA_UNROLL):
                r = g * DMA_UNROLL + u
                pltpu.make_async_copy(_token_rows(src_ref, r), _token_rows(xs_ref, dest_ref[base + r]),
                                      sem).start(priority=u % 2)
            return carry

        lax.fori_loop(0, TD // DMA_UNROLL, start, 0)
        pltpu.make_async_copy(src_ref, xs_ref.at[pl.ds(0, TD * ROW_TILES), :], sem).wait()

    @pl.when(i < n_ctx_steps)
    def _():
        scatter(hc_ref)

    @pl.when(i >= n_ctx_steps)
    def _():
        scatter(hs_ref)


def _dispatch(dest, hn_ctx, hn_dec, n_slots):
    n_ctx_steps = hn_ctx.shape[0] // (TD * ROW_TILES)
    n_dec_steps = hn_dec.shape[0] // (TD * ROW_TILES)
    return pl.pallas_call(
        functools.partial(_dispatch_kernel, n_ctx_steps=n_ctx_steps),
        out_shape=jax.ShapeDtypeStruct((n_slots * ROW_TILES, LANES), F32),
        grid_spec=pltpu.PrefetchScalarGridSpec(
            num_scalar_prefetch=1,
            grid=(n_ctx_steps + n_dec_steps,),
            in_specs=[pl.BlockSpec((TD * ROW_TILES, LANES), lambda i, d: (jnp.minimum(i, n_ctx_steps - 1), 0)),
                      pl.BlockSpec((TD * ROW_TILES, LANES), lambda i, d: (jnp.maximum(i - n_ctx_steps, 0), 0))],
            out_specs=pl.BlockSpec(memory_space=pl.ANY),
            scratch_shapes=[pltpu.SemaphoreType.DMA(())]),
        compiler_params=_params(("arbitrary",)),
        name="dispatch",
    )(dest, hn_ctx, hn_dec)


def _experts_kernel(ea_ref, eb_ref, nv_ref, xs_ref, rw_ref, rb_ref,
                    wga_ref, wua_ref, wda_ref, wgb_ref, wub_ref, wdb_ref, gpost_ref, ys_ref):
    i = pl.program_id(0)
    nv = nv_ref[i]

    @pl.when(nv == 0)
    def _():
        ys_ref[...] = jnp.zeros_like(ys_ref)

    @pl.when(nv > 0)
    def _():
        row = lax.broadcasted_iota(jnp.int32, (TMX, 1), 0)
        xb = jnp.where(row < nv, _load_token_major(xs_ref, TMX), 0.0).astype(BF16)
        logits = jnp.dot(xb, rw_ref[...], preferred_element_type=F32) + rb_ref[...]
        lane = lax.broadcasted_iota(jnp.int32, logits.shape, 1)
        ea = ea_ref[i]
        eb = eb_ref[i]
        gmask = lane < N_GROUPS
        gl = jnp.where(gmask, logits, -jnp.inf)
        gmax = jnp.max(gl, axis=-1, keepdims=True)
        gexp = jnp.where(gmask, jnp.exp(gl - gmax), 0.0)
        g_own = jnp.sum(jnp.where(lane == ea // EXPERTS_PER_GROUP, gexp, 0.0), axis=-1, keepdims=True)
        g_w = g_own / jnp.sum(gexp, axis=-1, keepdims=True)
        la = jnp.sum(jnp.where(lane == ea + E_LANE0, logits, 0.0), axis=-1, keepdims=True)
        lb = jnp.sum(jnp.where(lane == eb + E_LANE0, logits, 0.0), axis=-1, keepdims=True)
        m = jnp.maximum(la, lb)
        pa = jnp.exp(la - m)
        pb = jnp.exp(lb - m)
        inv = g_w / (pa + pb)

        def hidden(x, wg_ref, wu_ref, w):
            g = jnp.dot(x, wg_ref[0], preferred_element_type=F32)
            u = jnp.dot(x, wu_ref[0], preferred_element_type=F32)
            return ((g * _sigmoid(g)) * u * w).astype(BF16)

        part = TMX // EXPERT_ROW_PARTS
        ys = []
        for h in range(EXPERT_ROW_PARTS):
            rows = slice(h * part, (h + 1) * part)
            act_a = hidden(xb[rows], wga_ref, wua_ref, (pa * inv)[rows])
            act_b = hidden(xb[rows], wgb_ref, wub_ref, (pb * inv)[rows])
            y = (jnp.dot(act_a, wda_ref[0], preferred_element_type=F32)
                 + jnp.dot(act_b, wdb_ref[0], preferred_element_type=F32))
            ys.append(_rms(y) * gpost_ref[...])
        _store_token_major(ys_ref, jnp.concatenate(ys, axis=0))


def _experts(sched, xs, router_w, router_b, wg_b, wu_b, wd_b, g_post):
    ea, eb, nv = sched
    n_tiles = ea.shape[0]
    rows = lambda i, ea, eb, nv: (i, 0)
    const2 = lambda i, ea, eb, nv: (0, 0)
    exp_a = lambda i, ea, eb, nv: (ea[i], 0, 0)
    exp_b = lambda i, ea, eb, nv: (eb[i], 0, 0)
    w_in_spec = lambda m: pl.BlockSpec((1, D_MODEL, D_EXPERT), m)
    w_out_spec = lambda m: pl.BlockSpec((1, D_EXPERT, D_MODEL), m)
    return pl.pallas_call(
        _experts_kernel,
        out_shape=jax.ShapeDtypeStruct(xs.shape, F32),
        grid_spec=pltpu.PrefetchScalarGridSpec(
            num_scalar_prefetch=3,
            grid=(n_tiles,),
            in_specs=[pl.BlockSpec((TMX * ROW_TILES, LANES), rows),
                      pl.BlockSpec((D_MODEL, ROUTER_LANES), const2),
                      pl.BlockSpec((1, ROUTER_LANES), const2),
                      w_in_spec(exp_a), w_in_spec(exp_a), w_out_spec(exp_a),
                      w_in_spec(exp_b), w_in_spec(exp_b), w_out_spec(exp_b),
                      pl.BlockSpec((1, D_MODEL), const2)],
            out_specs=pl.BlockSpec((TMX * ROW_TILES, LANES), rows)),
        compiler_params=_params(("arbitrary",)),
        name="experts",
    )(ea, eb, nv, xs, router_w, router_b, wg_b, wu_b, wd_b, wg_b, wu_b, wd_b, g_post)


def _combine_kernel(dest_ref, ys_ref, x1_ref, mod_ref, o_ref, ybuf, sems, *, tile0):
    i = pl.program_id(0)
    n = pl.num_programs(0)

    def fetch(tile, slot):
        base = (tile + tile0) * TM

        def start(g, carry):
            for u in range(DMA_UNROLL):
                r = g * DMA_UNROLL + u
                pltpu.make_async_copy(_token_rows(ys_ref, dest_ref[base + r]), _token_rows(ybuf.at[slot], r),
                                      sems.at[slot]).start(priority=u % 2)
            return carry

        lax.fori_loop(0, TM // DMA_UNROLL, start, 0)

    @pl.when(i == 0)
    def _():
        fetch(0, 0)

    @pl.when(i + 1 < n)
    def _():
        fetch(i + 1, (i + 1) % 2)

    slot = i % 2
    pltpu.make_async_copy(ys_ref.at[pl.ds(0, TM * ROW_TILES), :], ybuf.at[slot], sems.at[slot]).wait()
    gate2 = mod_ref[0, :, 5 * D_MODEL:6 * D_MODEL]
    o_ref[...] = x1_ref[...] + gate2 * _load_token_major(ybuf.at[slot], TM)


def _combine(dest, ys, x1, mod3, cond_of_tile, tile0):
    n_tok = x1.shape[0]
    return pl.pallas_call(
        functools.partial(_combine_kernel, tile0=tile0),
        out_shape=jax.ShapeDtypeStruct((n_tok, D_MODEL), F32),
        grid_spec=pltpu.PrefetchScalarGridSpec(
            num_scalar_prefetch=1,
            grid=(n_tok // TM,),
            in_specs=[pl.BlockSpec(memory_space=pl.ANY),
                      pl.BlockSpec((TM, D_MODEL), lambda i, d: (i, 0)),
                      pl.BlockSpec((1, 1, 6 * D_MODEL), lambda i, d: (cond_of_tile(i), 0, 0))],
            out_specs=pl.BlockSpec((TM, D_MODEL), lambda i, d: (i, 0)),
            scratch_shapes=[pltpu.VMEM((2, TM * ROW_TILES, LANES), F32), pltpu.SemaphoreType.DMA((2,))]),
        compiler_params=_params(("arbitrary",)),
        name="combine",
    )(dest, ys, x1, mod3)


def _schedule(bucket, rank, counts):
    n_tok = bucket.shape[0]
    n_max = n_tok // TMX + N_BUCKETS
    cnt = counts[:N_BUCKETS, 0].astype(jnp.int32)
    tiles = (cnt + TMX - 1) // TMX
    tile_end = jnp.cumsum(tiles)
    tile_start = tile_end - tiles
    ids = jnp.arange(N_BUCKETS, dtype=jnp.int32)
    slot0 = jnp.sum(jnp.where(bucket[:, None] == ids[None, :], (tile_start * TMX)[None, :], 0), axis=1)
    dest = slot0 + rank
    i = jnp.arange(n_max, dtype=jnp.int32)
    total = tile_end[-1]
    valid = i < total
    tb = jnp.sum((jnp.minimum(i, total - 1)[:, None] >= tile_end[None, :]).astype(jnp.int32), axis=1)
    pairs = [(a, b) for a in range(EXPERTS_PER_GROUP) for b in range(a + 1, EXPERTS_PER_GROUP)]
    ea_tab = jnp.array([g * EXPERTS_PER_GROUP + a for g in range(N_GROUPS) for a, _ in pairs], jnp.int32)
    eb_tab = jnp.array([g * EXPERTS_PER_GROUP + b for g in range(N_GROUPS) for _, b in pairs], jnp.int32)
    ea = ea_tab[tb]
    eb = eb_tab[tb]
    nv = jnp.where(valid, jnp.clip(cnt[tb] - (i - tile_start[tb]) * TMX, 0, TMX), 0)
    return dest, (ea, eb, nv), n_max * TMX


def _block_diag_gates(rg_wa, rg_wx):
    heads = LANES // HEAD_RNN

    def bd(w):
        w = w.reshape(2, RNN_BLOCKS, heads, HEAD_RNN, HEAD_RNN)
        eye = jnp.eye(heads, dtype=w.dtype)
        full = jnp.einsum('dghij,hk->dghikj', w, eye)
        return full.reshape(2, RNN_BLOCKS, LANES, LANES)

    return jnp.concatenate([bd(rg_wa), bd(rg_wx)], axis=-1).astype(BF16)


def _row_tile(v):
    blocks = v.reshape(v.shape[:-1] + (RNN_BLOCKS, LANES))
    return jnp.concatenate([blocks] * PAIR, axis=-2)


def _to_time_major_state(h):
    return h.reshape(h.shape[0] // PAIR, TMJ_ROWS, LANES)


def kernel(x_prompt, x_sample, state_rglru, c, c_ctx, w_mod, b_mod, g_pre_mix, g_post_mix, g_pre_ffn,
           g_post_ffn, w_in, conv_w, conv_b, rg_wa, rg_ba, rg_wx, rg_bx, rg_lambda, sgu_g, sgu_w, sgu_b,
           w_out, router_g_w, router_g_b, router_e_w, router_e_b, exp_w_gate, exp_w_up, exp_w_down):
    assert w_mod.shape[0] == 1, "single-layer trunk"
    n_ctx, ctx_len, _ = x_prompt.shape
    n_dec, dec_len, _ = x_sample.shape
    l = 0

    n_cond = SUBLANES
    assert n_dec % PAIR == 0 and n_dec + PAIR <= n_cond
    cond = jnp.zeros((n_cond, D_MODEL), F32).at[:n_dec].set(c).at[n_dec:n_dec + PAIR].set(c_ctx)
    mod3 = _modulation(cond, w_mod[l], b_mod[l]).reshape(n_cond, 1, 6 * D_MODEL)
    pos_tab = _pos_table()

    w_in_b = w_in[l].astype(BF16)
    w_out_b = w_out[l].astype(BF16)
    sgu_w_b = sgu_w[l].reshape(2, 4 * CHUNK, CHUNK).astype(BF16)
    sgu_bias_tile = jnp.repeat(sgu_b[l].T, HEAD_SGU, axis=1)
    w_gates = _block_diag_gates(rg_wa[l], rg_wx[l])
    b_gates = jnp.stack([_row_tile(rg_ba[l]), _row_tile(rg_bx[l])], axis=1)
    lam = _row_tile(rg_lambda[l])
    conv_w_t = _row_tile(conv_w[l])
    conv_b_t = _row_tile(conv_b[l])
    router_w = jnp.zeros((D_MODEL, ROUTER_LANES), F32)
    router_w = router_w.at[:, :N_GROUPS].set(router_g_w[l]).at[:, E_LANE0:E_LANE0 + N_EXPERTS].set(router_e_w[l])
    router_w = router_w.astype(BF16)
    router_b = jnp.zeros((1, ROUTER_LANES), F32)
    router_b = router_b.at[0, :N_GROUPS].set(router_g_b[l]).at[0, E_LANE0:E_LANE0 + N_EXPERTS].set(router_e_b[l])
    router_wt = jnp.zeros((ROUTER_ROWS, D_MODEL), F32)
    router_wt = router_wt.at[:N_GROUPS].set(router_g_w[l].T).at[E_ROW0:E_ROW0 + N_EXPERTS].set(router_e_w[l].T)
    router_wt = router_wt.astype(BF16)
    router_bt = jnp.zeros((ROUTER_ROWS,), F32)
    router_bt = router_bt.at[:N_GROUPS].set(router_g_b[l]).at[E_ROW0:E_ROW0 + N_EXPERTS].set(router_e_b[l])
    router_bt = jnp.broadcast_to(router_bt[:, None], (ROUTER_ROWS, LANES))
    earlier = jnp.triu(jnp.ones((TM, TM), BF16), k=1)
    wg_b = exp_w_gate[l].astype(BF16)
    wu_b = exp_w_up[l].astype(BF16)
    wd_b = exp_w_down[l].astype(BF16)
    row = lambda v: v.reshape(1, -1)

    n_ctx_tok = n_ctx * ctx_len

    def mixer(x, h0, cond_of_tile, cond_block, use_pos, counts0):
        n_seq, seq_len, _ = x.shape
        xf = x.reshape(n_seq * seq_len, D_MODEL)
        tab = pos_tab if use_pos else None
        xr, gg, y_sgu = _premix(x, mod3, cond_block, row(g_pre_mix[l]), w_in_b, row(sgu_g[l]),
                                sgu_w_b, sgu_bias_tile, tab)
        scan_params = (conv_w_t, conv_b_t, w_gates, b_gates, lam)
        hf, hf_last = _scan(xr, None, None, _to_time_major_state(h0[:, 0]), *scan_params, direction=0)
        y_rnn, hb_first = _scan(xr, gg, hf, _to_time_major_state(h0[:, 1]), *scan_params, direction=1)
        fstate = jnp.stack([hf_last.reshape(n_seq, D_RNN), hb_first.reshape(n_seq, D_RNN)], axis=1)
        x1, hn, bucket, rank, counts = _postmix(
            xf, y_rnn.reshape(n_seq * seq_len, D_RNN), y_sgu.reshape(n_seq * seq_len, D_SGU), mod3,
            cond_of_tile, row(g_post_mix[l]), row(g_pre_ffn[l]), w_out_b, router_wt, router_bt, earlier, tab,
            counts0)
        return x1, hn, bucket, rank, counts, fstate

    tiles_per_seq = dec_len // TM
    ctx_cond = lambda i: n_dec
    dec_cond = lambda i: i // tiles_per_seq
    h0_ctx = jnp.zeros((n_ctx, 2, D_RNN), F32)
    counts0 = jnp.zeros((ROUTER_ROWS, LANES), F32)
    x1_ctx, hn_ctx, bucket_ctx, rank_ctx, counts, st = mixer(x_prompt, h0_ctx, ctx_cond, lambda p: n_dec // PAIR,
                                                             False, counts0)
    new_state = st.astype(state_rglru.dtype)[:, None]
    x1_dec, hn_dec, bucket_dec, rank_dec, counts, _ = mixer(x_sample, state_rglru[:, l].astype(F32), dec_cond,
                                                            lambda p: p, True, counts)

    dest, sched, n_slots = _schedule(jnp.concatenate([bucket_ctx, bucket_dec]),
                                     jnp.concatenate([rank_ctx, rank_dec]), counts)
    xs = _dispatch(dest, hn_ctx, hn_dec, n_slots)
    ys = _experts(sched, xs, router_w, router_b, wg_b, wu_b, wd_b, row(g_post_ffn[l]))
    y_prompt = _combine(dest, ys, x1_ctx, mod3, ctx_cond, 0)
    y_sample = _combine(dest, ys, x1_dec, mod3, dec_cond, n_ctx_tok // TM)
    return (y_prompt.reshape(x_prompt.shape), y_sample.reshape(x_sample.shape), new_state)
```

```python
import functools
import math

import jax
import jax.numpy as jnp
from jax import lax
from jax.experimental import pallas as pl
from jax.experimental.pallas import tpu as pltpu

D_MODEL = 1024
D_RNN = 512
D_SGU = 512
N_HEADS_RNN = 8
HEAD_RNN = D_RNN // N_HEADS_RNN
N_HEADS_SGU = 8
HEAD_SGU = D_SGU // N_HEADS_SGU
CHUNK = 128
GRID_W = 64
RG_C = 8.0
N_GROUPS = 4
EXPERTS_PER_GROUP = 4
N_EXPERTS = N_GROUPS * EXPERTS_PER_GROUP
D_EXPERT = 512
EPS = 1e-6
POS_BASE = 10000.0

LANES = 128
SUBLANES = 8
CONV_W = 4
CONV_LEFT = 2
PAIR = 2
RNN_BLOCKS = D_RNN // LANES
TMJ_ROWS = PAIR * RNN_BLOCKS
ROUTER_LANES = LANES
E_LANE0 = N_GROUPS
ROUTER_ROWS = 32
E_ROW0 = SUBLANES

PAIRS_PER_GROUP = EXPERTS_PER_GROUP * (EXPERTS_PER_GROUP - 1) // 2
N_BUCKETS = N_GROUPS * PAIRS_PER_GROUP

ROW_TILES = D_MODEL // LANES

TM = 512
TD = 1024
DMA_UNROLL = 8
EXPERT_ROW_PARTS = 2
TILE_ROW_PARTS = 1
TMX = 512
TT = TM // PAIR
LC = 256
TS = 16
PB = 2
VMEM_LIMIT = 56 * 1024 * 1024

F32 = jnp.float32
BF16 = jnp.bfloat16


def _params(sem):
    return pltpu.CompilerParams(dimension_semantics=sem, vmem_limit_bytes=VMEM_LIMIT)


def _rms(x):
    return x * lax.rsqrt(jnp.mean(x * x, axis=-1, keepdims=True) + EPS)


def _sigmoid(x):
    return 0.5 * jnp.tanh(0.5 * x) + 0.5


def _mod_kernel(cond_ref, w_ref, b_ref, o_ref):
    c = cond_ref[...]
    s = c * _sigmoid(c)
    o_ref[...] = jnp.dot(s.astype(BF16), w_ref[...].astype(BF16),
                         preferred_element_type=F32) + b_ref[...]


def _modulation(cond, w_mod, b_mod):
    n = w_mod.shape[1]
    bn = 1024
    return pl.pallas_call(
        _mod_kernel,
        out_shape=jax.ShapeDtypeStruct((cond.shape[0], n), F32),
        grid=(n // bn,),
        in_specs=[pl.BlockSpec(cond.shape, lambda j: (0, 0)),
                  pl.BlockSpec((D_MODEL, bn), lambda j: (0, j)),
                  pl.BlockSpec((1, bn), lambda j: (0, j))],
        out_specs=pl.BlockSpec((cond.shape[0], bn), lambda j: (0, j)),
        compiler_params=_params(("arbitrary",)),
        name="modulation",
    )(cond, w_mod, b_mod.reshape(1, n))


def _pos_kernel(o_ref):
    n_freq = D_MODEL // 4
    k = lax.broadcasted_iota(jnp.int32, (GRID_W, n_freq), 1).astype(F32)
    p = lax.broadcasted_iota(jnp.int32, (GRID_W, n_freq), 0).astype(F32)
    freq = jnp.exp(-math.log(POS_BASE) * k / n_freq)
    ang = p * freq
    o_ref[:, 0:n_freq] = jnp.sin(ang)
    o_ref[:, n_freq:2 * n_freq] = jnp.cos(ang)


def _pos_table():
    return pl.pallas_call(
        _pos_kernel,
        out_shape=jax.ShapeDtypeStruct((GRID_W, D_MODEL // 2), F32),
        name="pos_table",
    )()


def _add_pos(x, pos_refs, q0):
    if pos_refs is None:
        return x
    rows_ref, cols_ref = pos_refs
    reps = x.shape[0] // GRID_W
    rpart = jnp.concatenate(
        [jnp.broadcast_to(rows_ref[q:q + 1, :], (GRID_W, D_MODEL // 2)) for q in range(q0, q0 + reps)], axis=0)
    cpart = jnp.concatenate([cols_ref[...]] * reps, axis=0)
    return jnp.concatenate([x[:, :D_MODEL // 2] + rpart, x[:, D_MODEL // 2:] + cpart], axis=1)


def _load_x(x_ref, pos_refs, r0, n):
    return _add_pos(x_ref[r0:r0 + n, :], pos_refs, r0 // GRID_W)


def _premix_kernel(*refs, add_pos):
    refs = list(refs)
    x_ref = refs.pop(0)
    pos_refs = (refs.pop(0), refs.pop(0)) if add_pos else None
    mod_ref, g_ref, win_ref, sgug_ref, sguw_ref, sgub_ref, xr_ref, gg_ref, ys_ref = refs
    hn = []
    for s in range(PAIR):
        shift = mod_ref[s, :, 0:D_MODEL]
        scale = mod_ref[s, :, D_MODEL:2 * D_MODEL]
        hn.append(_rms(_add_pos(x_ref[s], pos_refs, 0)) * (g_ref[...] * (1.0 + scale)) + shift)
    z = jnp.dot(jnp.concatenate(hn, axis=0).astype(BF16), win_ref[...],
                preferred_element_type=F32)
    half = D_SGU // 2
    heads_per_half = N_HEADS_SGU // 2
    lane_head = lax.broadcasted_iota(jnp.int32, (CHUNK, half), 1) // HEAD_SGU
    for s in range(PAIR):
        zs = z[s * TT:(s + 1) * TT]
        gg = jax.nn.gelu(zs[:, D_RNN:2 * D_RNN])
        for k in range(RNN_BLOCKS):
            rows = pl.ds(s * RNN_BLOCKS + k, TT, stride=TMJ_ROWS)
            xr_ref[rows, :] = zs[:, k * LANES:(k + 1) * LANES]
            gg_ref[rows, :] = gg[:, k * LANES:(k + 1) * LANES]
        u = zs[:, 2 * D_RNN:2 * D_RNN + D_SGU]
        vn = (_rms(zs[:, 2 * D_RNN + D_SGU:]) * sgug_ref[...]).astype(BF16)
        for c in range(TT // CHUNK):
            rows = slice(c * CHUNK, (c + 1) * CHUNK)
            halves = []
            for hf in range(2):
                r = jnp.dot(sguw_ref[hf], vn[rows, hf * half:(hf + 1) * half],
                            preferred_element_type=F32)
                sel = jnp.zeros((CHUNK, half), F32)
                for h in range(heads_per_half):
                    sel = jnp.where(lane_head == h, r[h * CHUNK:(h + 1) * CHUNK], sel)
                halves.append(sel)
            gatev = jnp.concatenate(halves, axis=1) + sgub_ref[...]
            ys_ref[s, rows, :] = (u[rows] * gatev).astype(BF16)


def _premix(x, mod3, cond_block, g_pre, w_in_b, sgu_g, sgu_w_b, sgu_bias_tile, pos_tab):
    n_seq, seq_len, _ = x.shape
    n_pairs, n_tiles = n_seq // PAIR, seq_len // TT
    add_pos = pos_tab is not None
    const2 = lambda p, j: (0, 0)
    in_specs = [pl.BlockSpec((PAIR, TT, D_MODEL), lambda p, j: (p, j, 0))]
    args = [x]
    if add_pos:
        reps = TT // GRID_W
        in_specs += [pl.BlockSpec((None, reps, D_MODEL // 2), lambda p, j: (j, 0, 0)),
                     pl.BlockSpec((GRID_W, D_MODEL // 2), const2)]
        args += [pos_tab.reshape(GRID_W // reps, reps, D_MODEL // 2), pos_tab]
    in_specs += [pl.BlockSpec((PAIR, 1, 6 * D_MODEL), lambda p, j: (cond_block(p), 0, 0)),
                 pl.BlockSpec((1, D_MODEL), const2),
                 pl.BlockSpec((D_MODEL, 2 * D_RNN + 2 * D_SGU), const2),
                 pl.BlockSpec((1, D_SGU), const2),
                 pl.BlockSpec((2, 4 * CHUNK, CHUNK), lambda p, j: (0, 0, 0)),
                 pl.BlockSpec((CHUNK, D_SGU), const2)]
    args += [mod3, g_pre, w_in_b, sgu_g, sgu_w_b, sgu_bias_tile]
    tmj = jax.ShapeDtypeStruct((n_pairs * seq_len * TMJ_ROWS, LANES), F32)
    tmj_spec = pl.BlockSpec((TT * TMJ_ROWS, LANES), lambda p, j: (p * n_tiles + j, 0))
    xr, gg, y_sgu = pl.pallas_call(
        functools.partial(_premix_kernel, add_pos=add_pos),
        out_shape=(tmj, tmj, jax.ShapeDtypeStruct((n_seq, seq_len, D_SGU), BF16)),
        grid=(n_pairs, n_tiles),
        in_specs=in_specs,
        out_specs=(tmj_spec, tmj_spec, pl.BlockSpec((PAIR, TT, D_SGU), lambda p, j: (p, j, 0))),
        compiler_params=_params(("parallel", "parallel")),
        name="premix",
    )(*args)
    shape4 = (n_pairs, seq_len, TMJ_ROWS, LANES)
    return xr.reshape(shape4), gg.reshape(shape4), y_sgu


def _scan_kernel(*refs, reverse, n_chunks):
    if reverse:
        (xprev_ref, x_ref, xnext_ref, gg_ref, hf_ref, h0_ref, cw_ref, cb_ref, wg_ref, bg_ref, lam_ref,
         y_ref, fs_ref, xwin, xc_s, r_s, i_s, a_s, b_s, y_s, hcar) = refs
    else:
        (xprev_ref, x_ref, xnext_ref, h0_ref, cw_ref, cb_ref, wg_ref, bg_ref, lam_ref,
         hf_ref, fs_ref, xwin, xc_s, r_s, i_s, a_s, b_s, hcar) = refs
    c = pl.program_id(1)
    chunk = n_chunks - 1 - c if reverse else c
    sub_rows = TS * TMJ_ROWS

    def rows_of(pb, t0, n_steps):
        return pl.ds(pl.multiple_of((pb * LC + t0) * TMJ_ROWS, TMJ_ROWS), n_steps * TMJ_ROWS)

    @pl.when(c == 0)
    def _():
        hcar[...] = h0_ref[...]

    xwin[:, 0:CONV_LEFT] = jnp.where(chunk > 0, xprev_ref[...], 0.0)
    xwin[:, LC + CONV_LEFT:LC + CONV_W - 1] = jnp.where(chunk < n_chunks - 1, xnext_ref[...], 0.0)

    def copy(i, carry):
        t0 = pl.multiple_of(i * TS, TS)
        xwin[:, pl.ds(t0 + CONV_LEFT, TS)] = x_ref[:, pl.ds(t0, TS)]
        return carry

    lax.fori_loop(0, LC // TS, copy, 0)

    def conv(i, carry):
        t0 = pl.multiple_of(i * TS, TS)
        xc = cb_ref[...] + cw_ref[0] * xwin[:, pl.ds(t0, TS)]
        for k in range(1, CONV_W):
            xc = xc + cw_ref[k] * xwin[:, pl.ds(t0 + k, TS)]
        for pb in range(PB):
            xc_s[rows_of(pb, t0, TS), :] = xc[pb].reshape(sub_rows, LANES)
        return carry

    lax.fori_loop(0, LC // TS, conv, 0)

    n_rows = PB * LC * PAIR
    for k in range(RNN_BLOCKS):
        rows = pl.ds(k, n_rows, stride=RNN_BLOCKS)
        g = jnp.dot(xc_s[rows, :].astype(BF16), wg_ref[k], preferred_element_type=F32)
        r_s[rows, :] = g[:, :LANES]
        i_s[rows, :] = g[:, LANES:]

    neg_lam = -lam_ref[...]
    softplus = jnp.maximum(neg_lam, 0.0) + jnp.log(1.0 + jnp.exp(-jnp.abs(neg_lam)))
    half_decay = (-0.5 * RG_C * math.log2(math.e)) * softplus

    def gates(i, carry):
        t0 = pl.multiple_of(i * TS, TS)
        for pb in range(PB):
            rows = rows_of(pb, t0, TS)
            tile = lambda ref: ref[rows, :].reshape(TS, TMJ_ROWS, LANES)
            tr = jnp.tanh(tile(r_s) + bg_ref[0])
            ti = jnp.tanh(tile(i_s) + bg_ref[1])
            log2_a = tr * half_decay + half_decay
            a = jnp.exp2(log2_a)
            q = jnp.tanh(log2_a * (-math.log(2.0))) * (a * a + 1.0)
            b = jnp.where(q > 0.0, q * lax.rsqrt(q), 0.0) * ((ti + 1.0) * tile(xc_s))
            a_s[rows, :] = a.reshape(sub_rows, LANES)
            b_s[rows, :] = b.reshape(sub_rows, LANES)
        return carry

    lax.fori_loop(0, LC // TS, gates, 0)

    def step(j, hs):
        t = LC - 1 - j if reverse else j
        out = []
        for pb in range(PB):
            rows = rows_of(pb, t, 1)
            h = a_s[rows, :] * hs[pb] + b_s[rows, :]
            if reverse:
                y_s[rows, :] = (hf_ref[pb, t] + h) * gg_ref[pb, t]
            else:
                hf_ref[pb, t] = h
            out.append(h)
        return tuple(out)

    hs = lax.fori_loop(0, LC, step, tuple(hcar[pb] for pb in range(PB)), unroll=8)
    for pb in range(PB):
        hcar[pb] = hs[pb]
        fs_ref[pb] = hs[pb]

    if reverse:
        for pb in range(PB):
            for s in range(PAIR):
                cols = [y_s[pl.ds(pb * LC * TMJ_ROWS + s * RNN_BLOCKS + k, LC, stride=TMJ_ROWS), :]
                        for k in range(RNN_BLOCKS)]
                y_ref[pb * PAIR + s] = jnp.concatenate(cols, axis=1).astype(BF16)


def _scan(xr, gg, hf, h0, conv_w, conv_b, w_gates, b_gates, lam, direction):
    n_pairs, seq_len = xr.shape[:2]
    n_chunks = seq_len // LC
    reverse = direction == 1
    pos = (lambda c: n_chunks - 1 - c) if reverse else (lambda c: c)
    tmj_blk = pl.BlockSpec((PB, LC, TMJ_ROWS, LANES), lambda i, c: (i, pos(c), 0, 0))
    state_blk = pl.BlockSpec((PB, TMJ_ROWS, LANES), lambda i, c: (i, 0, 0))
    per_dir = lambda *shape: pl.BlockSpec((None,) + shape, lambda i, c: (direction,) + (0,) * len(shape))
    in_specs = [
        pl.BlockSpec((PB, CONV_LEFT, TMJ_ROWS, LANES),
                     lambda i, c: (i, jnp.maximum(pos(c) * (LC // CONV_LEFT) - 1, 0), 0, 0)),
        tmj_blk,
        pl.BlockSpec((PB, 1, TMJ_ROWS, LANES), lambda i, c: (i, jnp.minimum((pos(c) + 1) * LC, seq_len - 1), 0, 0)),
    ]
    args = [xr, xr, xr]
    if reverse:
        in_specs += [tmj_blk, tmj_blk]
        args += [gg, hf]
    in_specs += [state_blk,
                 pl.BlockSpec((CONV_W, TMJ_ROWS, LANES), lambda i, c: (0, 0, 0)),
                 pl.BlockSpec((TMJ_ROWS, LANES), lambda i, c: (0, 0)),
                 per_dir(RNN_BLOCKS, LANES, 2 * LANES),
                 per_dir(2, TMJ_ROWS, LANES),
                 per_dir(TMJ_ROWS, LANES)]
    args += [h0, conv_w, conv_b, w_gates, b_gates, lam]
    flat = pltpu.VMEM((PB * LC * TMJ_ROWS, LANES), F32)
    scratch = [pltpu.VMEM((PB, LC + CONV_W - 1, TMJ_ROWS, LANES), F32)] + [flat] * (6 if reverse else 5)
    scratch += [pltpu.VMEM((PB, TMJ_ROWS, LANES), F32)]
    state = jax.ShapeDtypeStruct((n_pairs, TMJ_ROWS, LANES), F32)
    if reverse:
        out_shape = (jax.ShapeDtypeStruct((n_pairs * PAIR, seq_len, D_RNN), BF16), state)
        out_specs = (pl.BlockSpec((PB * PAIR, LC, D_RNN), lambda i, c: (i, pos(c), 0)), state_blk)
    else:
        out_shape = (jax.ShapeDtypeStruct(xr.shape, F32), state)
        out_specs = (tmj_blk, state_blk)
    return pl.pallas_call(
        functools.partial(_scan_kernel, reverse=reverse, n_chunks=n_chunks),
        out_shape=out_shape,
        grid=(n_pairs // PB, n_chunks),
        in_specs=in_specs,
        out_specs=out_specs,
        scratch_shapes=scratch,
        compiler_params=_params(("parallel", "arbitrary")),
        name="scan_bwd" if reverse else "scan_fwd",
    )(*args)


def _route(lt):
    n = lt.shape[1]
    row = lax.broadcasted_iota(jnp.int32, (EXPERTS_PER_GROUP, n), 0)
    neg = jnp.float32(-jnp.inf)

    def arg_max(v):
        m = jnp.max(v, axis=0, keepdims=True)
        return jnp.min(jnp.where(v == m, row, EXPERTS_PER_GROUP), axis=0, keepdims=True)

    g_idx = arg_max(lt[0:N_GROUPS])
    el = lt[E_ROW0:E_ROW0 + EXPERTS_PER_GROUP]
    for g in range(1, N_GROUPS):
        first = E_ROW0 + g * EXPERTS_PER_GROUP
        el = jnp.where(g_idx == g, lt[first:first + EXPERTS_PER_GROUP], el)
    i1 = arg_max(el)
    i2 = arg_max(jnp.where(row == i1, neg, el))
    ja = jnp.minimum(i1, i2)
    jb = jnp.maximum(i1, i2)
    pair = (ja * (2 * EXPERTS_PER_GROUP - 1 - ja)) // 2 + (jb - ja - 1)
    return g_idx * PAIRS_PER_GROUP + pair


def _store_token_major(ref, x, t0=0):
    n = x.shape[0]
    for k in range(ROW_TILES):
        ref[pl.ds(t0 * ROW_TILES + k, n, stride=ROW_TILES), :] = x[:, k * LANES:(k + 1) * LANES]


def _load_token_major(ref, n):
    return jnp.concatenate([ref[pl.ds(k, n, stride=ROW_TILES), :] for k in range(ROW_TILES)], axis=1)


def _postmix_kernel(*refs, add_pos):
    refs = list(refs)
    x_ref = refs.pop(0)
    pos_refs = (refs.pop(0), refs.pop(0)) if add_pos else None
    (yr_ref, ys_ref, mod_ref, gpost_ref, gpre_ref, wout_ref, rw_ref, rb_ref, earlier_ref, cnt0_ref,
     x1_ref, hn_ref, rt_ref, cnt_ref, run_ref) = refs

    @pl.when(pl.program_id(0) == 0)
    def _():
        run_ref[...] = cnt0_ref[...]

    gate1 = mod_ref[0, :, 2 * D_MODEL:3 * D_MODEL]
    shift2 = mod_ref[0, :, 3 * D_MODEL:4 * D_MODEL]
    scale2 = mod_ref[0, :, 4 * D_MODEL:5 * D_MODEL]
    y = (jnp.dot(yr_ref[...], wout_ref[0:D_RNN, :], preferred_element_type=F32)
         + jnp.dot(ys_ref[...], wout_ref[D_RNN:, :], preferred_element_type=F32))
    x1 = _load_x(x_ref, pos_refs, 0, TM) + _rms(y) * (gate1 * gpost_ref[...])
    x1_ref[...] = x1
    hn = _rms(x1) * (gpre_ref[...] * (1.0 + scale2)) + shift2
    _store_token_major(hn_ref, hn)
    lt = lax.dot_general(rw_ref[...], hn.astype(BF16), (((1,), (1,)), ((), ())),
                         preferred_element_type=F32) + rb_ref[:, 0:1]
    bucket = _route(lt)
    onehot = lax.broadcasted_iota(jnp.int32, (ROUTER_ROWS, TM), 0) == bucket
    before = jnp.dot(onehot.astype(BF16), earlier_ref[...], preferred_element_type=F32) + run_ref[:, 0:1]
    rank = jnp.sum(jnp.where(onehot, before, 0.0), axis=0, keepdims=True).astype(jnp.int32)
    row = lax.broadcasted_iota(jnp.int32, (SUBLANES, TM), 0)
    rt_ref[...] = jnp.where(row == 0, bucket, jnp.where(row == 1, rank, 0))
    run_ref[...] += jnp.sum(onehot.astype(F32), axis=1, keepdims=True)
    cnt_ref[...] = run_ref[...]


def _postmix(x, y_rnn, y_sgu, mod3, cond_of_tile, g_post, g_pre, w_out_b, router_wt, router_bt, earlier, pos_tab,
             counts0):
    n_tok = x.shape[0]
    add_pos = pos_tab is not None
    tok = lambda i: (i, 0)
    const2 = lambda i: (0, 0)
    in_specs = [pl.BlockSpec((TM, D_MODEL), tok)]
    args = [x]
    if add_pos:
        reps = TM // GRID_W
        tiles_per_seq = GRID_W // reps
        in_specs += [pl.BlockSpec((None, reps, D_MODEL // 2), lambda i: (i % tiles_per_seq, 0, 0)),
                     pl.BlockSpec((GRID_W, D_MODEL // 2), const2)]
        args += [pos_tab.reshape(tiles_per_seq, reps, D_MODEL // 2), pos_tab]
    in_specs += [pl.BlockSpec((TM, D_RNN), tok),
                 pl.BlockSpec((TM, D_SGU), tok),
                 pl.BlockSpec((1, 1, 6 * D_MODEL), lambda i: (cond_of_tile(i), 0, 0)),
                 pl.BlockSpec((1, D_MODEL), const2),
                 pl.BlockSpec((1, D_MODEL), const2),
                 pl.BlockSpec((D_MODEL, D_MODEL), const2),
                 pl.BlockSpec((ROUTER_ROWS, D_MODEL), const2),
                 pl.BlockSpec((ROUTER_ROWS, LANES), const2),
                 pl.BlockSpec((TM, TM), const2),
                 pl.BlockSpec((ROUTER_ROWS, LANES), const2)]
    args += [y_rnn, y_sgu, mod3, g_post, g_pre, w_out_b, router_wt, router_bt, earlier, counts0]
    n_tiles = n_tok // TM
    counts_spec = pl.BlockSpec((ROUTER_ROWS, LANES), const2)
    x1, hn, route, counts = pl.pallas_call(
        functools.partial(_postmix_kernel, add_pos=add_pos),
        out_shape=(jax.ShapeDtypeStruct((n_tok, D_MODEL), F32),
                   jax.ShapeDtypeStruct((n_tok * ROW_TILES, LANES), F32),
                   jax.ShapeDtypeStruct((n_tiles * SUBLANES, TM), jnp.int32),
                   jax.ShapeDtypeStruct((ROUTER_ROWS, LANES), F32)),
        grid=(n_tiles,),
        in_specs=in_specs,
        out_specs=(pl.BlockSpec((TM, D_MODEL), tok),
                   pl.BlockSpec((TM * ROW_TILES, LANES), tok),
                   pl.BlockSpec((SUBLANES, TM), tok),
                   counts_spec),
        scratch_shapes=[pltpu.VMEM((ROUTER_ROWS, LANES), F32)],
        compiler_params=_params(("arbitrary",)),
        name="postmix",
    )(*args)
    route = route.reshape(n_tiles, SUBLANES, TM)
    return x1, hn, route[:, 0].reshape(n_tok), route[:, 1].reshape(n_tok), counts


def _token_rows(ref, t):
    return ref.at[pl.ds(pl.multiple_of(t * ROW_TILES, ROW_TILES), ROW_TILES), :]


def _dispatch_kernel(dest_ref, hc_ref, hs_ref, xs_ref, sem, *, n_ctx_steps):
    i = pl.program_id(0)
    base = i * TD

    def scatter(src_ref):
        def start(g, carry):
            for u in range(DMA_UNROLL):
                r = g * DMA_UNROLL + u
                pltpu.make_async_copy(_token_rows(src_ref, r), _token_rows(xs_ref, dest_ref[base + r]),
                                      sem).start(priority=u % 2)
            return carry

        lax.fori_loop(0, TD // DMA_UNROLL, start, 0)
        pltpu.make_async_copy(src_ref, xs_ref.at[pl.ds(0, TD * ROW_TILES), :], sem).wait()

    @pl.when(i < n_ctx_steps)
    def _():
        scatter(hc_ref)

    @pl.when(i >= n_ctx_steps)
    def _():
        scatter(hs_ref)


def _dispatch(dest, hn_ctx, hn_dec, n_slots):
    n_ctx_steps = hn_ctx.shape[0] // (TD * ROW_TILES)
    n_dec_steps = hn_dec.shape[0] // (TD * ROW_TILES)
    return pl.pallas_call(
        functools.partial(_dispatch_kernel, n_ctx_steps=n_ctx_steps),
        out_shape=jax.ShapeDtypeStruct((n_slots * ROW_TILES, LANES), F32),
        grid_spec=pltpu.PrefetchScalarGridSpec(
            num_scalar_prefetch=1,
            grid=(n_ctx_steps + n_dec_steps,),
            in_specs=[pl.BlockSpec((TD * ROW_TILES, LANES), lambda i, d: (jnp.minimum(i, n_ctx_steps - 1), 0)),
                      pl.BlockSpec((TD * ROW_TILES, LANES), lambda i, d: (jnp.maximum(i - n_ctx_steps, 0), 0))],
            out_specs=pl.BlockSpec(memory_space=pl.ANY),
            scratch_shapes=[pltpu.SemaphoreType.DMA(())]),
        compiler_params=_params(("arbitrary",)),
        name="dispatch",
    )(dest, hn_ctx, hn_dec)


def _experts_kernel(ea_ref, eb_ref, nv_ref, xs_ref, rw_ref, rb_ref,
                    wga_ref, wua_ref, wda_ref, wgb_ref, wub_ref, wdb_ref, gpost_ref, ys_ref):
    i = pl.program_id(0)
    nv = nv_ref[i]

    part = TMX // EXPERT_ROW_PARTS

    @pl.when(nv == 0)
    def _():
        ys_ref[...] = jnp.zeros_like(ys_ref)

    def process(n_parts):
        n = n_parts * part
        row = lax.broadcasted_iota(jnp.int32, (n, 1), 0)
        xb = jnp.where(row < nv, _load_token_major(xs_ref, n), 0.0).astype(BF16)
        logits = jnp.dot(xb, rw_ref[...], preferred_element_type=F32) + rb_ref[...]
        lane = lax.broadcasted_iota(jnp.int32, logits.shape, 1)
        ea = ea_ref[i]
        eb = eb_ref[i]
        gmask = lane < N_GROUPS
        gl = jnp.where(gmask, logits, -jnp.inf)
        gmax = jnp.max(gl, axis=-1, keepdims=True)
        gexp = jnp.where(gmask, jnp.exp(gl - gmax), 0.0)
        g_own = jnp.sum(jnp.where(lane == ea // EXPERTS_PER_GROUP, gexp, 0.0), axis=-1, keepdims=True)
        g_w = g_own / jnp.sum(gexp, axis=-1, keepdims=True)
        la = jnp.sum(jnp.where(lane == ea + E_LANE0, logits, 0.0), axis=-1, keepdims=True)
        lb = jnp.sum(jnp.where(lane == eb + E_LANE0, logits, 0.0), axis=-1, keepdims=True)
        m = jnp.maximum(la, lb)
        pa = jnp.exp(la - m)
        pb = jnp.exp(lb - m)
        inv = g_w / (pa + pb)

        def hidden(x, wg_ref, wu_ref, w):
            g = jnp.dot(x, wg_ref[0], preferred_element_type=F32)
            u = jnp.dot(x, wu_ref[0], preferred_element_type=F32)
            return ((g * _sigmoid(g)) * u * w).astype(BF16)

        ys = []
        for h in range(n_parts):
            rows = slice(h * part, (h + 1) * part)
            act_a = hidden(xb[rows], wga_ref, wua_ref, (pa * inv)[rows])
            act_b = hidden(xb[rows], wgb_ref, wub_ref, (pb * inv)[rows])
            y = (jnp.dot(act_a, wda_ref[0], preferred_element_type=F32)
                 + jnp.dot(act_b, wdb_ref[0], preferred_element_type=F32))
            ys.append(_rms(y) * gpost_ref[...])
        _store_token_major(ys_ref, jnp.concatenate(ys, axis=0))
        if n < TMX:
            ys_ref[n * ROW_TILES:, :] = jnp.zeros(((TMX - n) * ROW_TILES, LANES), F32)

    for n_parts in range(1, EXPERT_ROW_PARTS + 1):
        pl.when((nv > (n_parts - 1) * part) & (nv <= n_parts * part))(functools.partial(process, n_parts))


def _experts(sched, xs, router_w, router_b, wg_b, wu_b, wd_b, g_post):
    ea, eb, nv = sched
    n_tiles = ea.shape[0]
    rows = lambda i, ea, eb, nv: (i, 0)
    const2 = lambda i, ea, eb, nv: (0, 0)
    exp_a = lambda i, ea, eb, nv: (ea[i], 0, 0)
    exp_b = lambda i, ea, eb, nv: (eb[i], 0, 0)
    w_in_spec = lambda m: pl.BlockSpec((1, D_MODEL, D_EXPERT), m)
    w_out_spec = lambda m: pl.BlockSpec((1, D_EXPERT, D_MODEL), m)
    return pl.pallas_call(
        _experts_kernel,
        out_shape=jax.ShapeDtypeStruct(xs.shape, F32),
        grid_spec=pltpu.PrefetchScalarGridSpec(
            num_scalar_prefetch=3,
            grid=(n_tiles,),
            in_specs=[pl.BlockSpec((TMX * ROW_TILES, LANES), rows),
                      pl.BlockSpec((D_MODEL, ROUTER_LANES), const2),
                      pl.BlockSpec((1, ROUTER_LANES), const2),
                      w_in_spec(exp_a), w_in_spec(exp_a), w_out_spec(exp_a),
                      w_in_spec(exp_b), w_in_spec(exp_b), w_out_spec(exp_b),
                      pl.BlockSpec((1, D_MODEL), const2)],
            out_specs=pl.BlockSpec((TMX * ROW_TILES, LANES), rows)),
        compiler_params=_params(("arbitrary",)),
        name="experts",
    )(ea, eb, nv, xs, router_w, router_b, wg_b, wu_b, wd_b, wg_b, wu_b, wd_b, g_post)


def _combine_kernel(dest_ref, ys_ref, x1_ref, mod_ref, o_ref, ybuf, sems, *, tile0):
    i = pl.program_id(0)
    n = pl.num_programs(0)

    def fetch(tile, slot):
        base = (tile + tile0) * TM

        def start(g, carry):
            for u in range(DMA_UNROLL):
                r = g * DMA_UNROLL + u
                pltpu.make_async_copy(_token_rows(ys_ref, dest_ref[base + r]), _token_rows(ybuf.at[slot], r),
                                      sems.at[slot]).start(priority=u % 2)
            return carry

        lax.fori_loop(0, TM // DMA_UNROLL, start, 0)

    @pl.when(i == 0)
    def _():
        fetch(0, 0)

    @pl.when(i + 1 < n)
    def _():
        fetch(i + 1, (i + 1) % 2)

    slot = i % 2
    pltpu.make_async_copy(ys_ref.at[pl.ds(0, TM * ROW_TILES), :], ybuf.at[slot], sems.at[slot]).wait()
    gate2 = mod_ref[0, :, 5 * D_MODEL:6 * D_MODEL]
    o_ref[...] = x1_ref[...] + gate2 * _load_token_major(ybuf.at[slot], TM)


def _combine(dest, ys, x1, mod3, cond_of_tile, tile0):
    n_tok = x1.shape[0]
    return pl.pallas_call(
        functools.partial(_combine_kernel, tile0=tile0),
        out_shape=jax.ShapeDtypeStruct((n_tok, D_MODEL), F32),
        grid_spec=pltpu.PrefetchScalarGridSpec(
            num_scalar_prefetch=1,
            grid=(n_tok // TM,),
            in_specs=[pl.BlockSpec(memory_space=pl.ANY),
                      pl.BlockSpec((TM, D_MODEL), lambda i, d: (i, 0)),
                      pl.BlockSpec((1, 1, 6 * D_MODEL), lambda i, d: (cond_of_tile(i), 0, 0))],
            out_specs=pl.BlockSpec((TM, D_MODEL), lambda i, d: (i, 0)),
            scratch_shapes=[pltpu.VMEM((2, TM * ROW_TILES, LANES), F32), pltpu.SemaphoreType.DMA((2,))]),
        compiler_params=_params(("arbitrary",)),
        name="combine",
    )(dest, ys, x1, mod3)


def _schedule(bucket, rank, counts):
    n_tok = bucket.shape[0]
    n_max = n_tok // TMX + N_BUCKETS
    cnt = counts[:N_BUCKETS, 0].astype(jnp.int32)
    tiles = (cnt + TMX - 1) // TMX
    tile_end = jnp.cumsum(tiles)
    tile_start = tile_end - tiles
    ids = jnp.arange(N_BUCKETS, dtype=jnp.int32)
    slot0 = jnp.sum(jnp.where(bucket[:, None] == ids[None, :], (tile_start * TMX)[None, :], 0), axis=1)
    dest = slot0 + rank
    i = jnp.arange(n_max, dtype=jnp.int32)
    total = tile_end[-1]
    valid = i < total
    tb = jnp.sum((jnp.minimum(i, total - 1)[:, None] >= tile_end[None, :]).astype(jnp.int32), axis=1)
    pairs = [(a, b) for a in range(EXPERTS_PER_GROUP) for b in range(a + 1, EXPERTS_PER_GROUP)]
    ea_tab = jnp.array([g * EXPERTS_PER_GROUP + a for g in range(N_GROUPS) for a, _ in pairs], jnp.int32)
    eb_tab = jnp.array([g * EXPERTS_PER_GROUP + b for g in range(N_GROUPS) for _, b in pairs], jnp.int32)
    ea = ea_tab[tb]
    eb = eb_tab[tb]
    nv = jnp.where(valid, jnp.clip(cnt[tb] - (i - tile_start[tb]) * TMX, 0, TMX), 0)
    return dest, (ea, eb, nv), n_max * TMX


def _block_diag_gates(rg_wa, rg_wx):
    heads = LANES // HEAD_RNN

    def bd(w):
        w = w.reshape(2, RNN_BLOCKS, heads, HEAD_RNN, HEAD_RNN)
        eye = jnp.eye(heads, dtype=w.dtype)
        full = jnp.einsum('dghij,hk->dghikj', w, eye)
        return full.reshape(2, RNN_BLOCKS, LANES, LANES)

    return jnp.concatenate([bd(rg_wa), bd(rg_wx)], axis=-1).astype(BF16)


def _row_tile(v):
    blocks = v.reshape(v.shape[:-1] + (RNN_BLOCKS, LANES))
    return jnp.concatenate([blocks] * PAIR, axis=-2)


def _to_time_major_state(h):
    return h.reshape(h.shape[0] // PAIR, TMJ_ROWS, LANES)


def kernel(x_prompt, x_sample, state_rglru, c, c_ctx, w_mod, b_mod, g_pre_mix, g_post_mix, g_pre_ffn,
           g_post_ffn, w_in, conv_w, conv_b, rg_wa, rg_ba, rg_wx, rg_bx, rg_lambda, sgu_g, sgu_w, sgu_b,
           w_out, router_g_w, router_g_b, router_e_w, router_e_b, exp_w_gate, exp_w_up, exp_w_down):
    assert w_mod.shape[0] == 1, "single-layer trunk"
    n_ctx, ctx_len, _ = x_prompt.shape
    n_dec, dec_len, _ = x_sample.shape
    l = 0

    n_cond = SUBLANES
    assert n_dec % PAIR == 0 and n_dec + PAIR <= n_cond
    cond = jnp.zeros((n_cond, D_MODEL), F32).at[:n_dec].set(c).at[n_dec:n_dec + PAIR].set(c_ctx)
    mod3 = _modulation(cond, w_mod[l], b_mod[l]).reshape(n_cond, 1, 6 * D_MODEL)
    pos_tab = _pos_table()

    w_in_b = w_in[l].astype(BF16)
    w_out_b = w_out[l].astype(BF16)
    sgu_w_b = sgu_w[l].reshape(2, 4 * CHUNK, CHUNK).astype(BF16)
    sgu_bias_tile = jnp.repeat(sgu_b[l].T, HEAD_SGU, axis=1)
    w_gates = _block_diag_gates(rg_wa[l], rg_wx[l])
    b_gates = 0.5 * jnp.stack([_row_tile(rg_ba[l]), _row_tile(rg_bx[l])], axis=1)
    lam = _row_tile(rg_lambda[l])
    conv_w_t = 0.5 * _row_tile(conv_w[l])
    conv_b_t = 0.5 * _row_tile(conv_b[l])
    router_w = jnp.zeros((D_MODEL, ROUTER_LANES), F32)
    router_w = router_w.at[:, :N_GROUPS].set(router_g_w[l]).at[:, E_LANE0:E_LANE0 + N_EXPERTS].set(router_e_w[l])
    router_w = router_w.astype(BF16)
    router_b = jnp.zeros((1, ROUTER_LANES), F32)
    router_b = router_b.at[0, :N_GROUPS].set(router_g_b[l]).at[0, E_LANE0:E_LANE0 + N_EXPERTS].set(router_e_b[l])
    router_wt = jnp.zeros((ROUTER_ROWS, D_MODEL), F32)
    router_wt = router_wt.at[:N_GROUPS].set(router_g_w[l].T).at[E_ROW0:E_ROW0 + N_EXPERTS].set(router_e_w[l].T)
    router_wt = router_wt.astype(BF16)
    router_bt = jnp.zeros((ROUTER_ROWS,), F32)
    router_bt = router_bt.at[:N_GROUPS].set(router_g_b[l]).at[E_ROW0:E_ROW0 + N_EXPERTS].set(router_e_b[l])
    router_bt = jnp.broadcast_to(router_bt[:, None], (ROUTER_ROWS, LANES))
    earlier = jnp.triu(jnp.ones((TM, TM), BF16), k=1)
    wg_b = exp_w_gate[l].astype(BF16)
    wu_b = exp_w_up[l].astype(BF16)
    wd_b = exp_w_down[l].astype(BF16)
    row = lambda v: v.reshape(1, -1)

    n_ctx_tok = n_ctx * ctx_len

    def mixer(x, h0, cond_of_tile, cond_block, use_pos, counts0):
        n_seq, seq_len, _ = x.shape
        xf = x.reshape(n_seq * seq_len, D_MODEL)
        tab = pos_tab if use_pos else None
        xr, gg, y_sgu = _premix(x, mod3, cond_block, row(g_pre_mix[l]), w_in_b, row(sgu_g[l]),
                                sgu_w_b, sgu_bias_tile, tab)
        scan_params = (conv_w_t, conv_b_t, w_gates, b_gates, lam)
        hf, hf_last = _scan(xr, None, None, _to_time_major_state(h0[:, 0]), *scan_params, direction=0)
        y_rnn, hb_first = _scan(xr, gg, hf, _to_time_major_state(h0[:, 1]), *scan_params, direction=1)
        fstate = jnp.stack([hf_last.reshape(n_seq, D_RNN), hb_first.reshape(n_seq, D_RNN)], axis=1)
        x1, hn, bucket, rank, counts = _postmix(
            xf, y_rnn.reshape(n_seq * seq_len, D_RNN), y_sgu.reshape(n_seq * seq_len, D_SGU), mod3,
            cond_of_tile, row(g_post_mix[l]), row(g_pre_ffn[l]), w_out_b, router_wt, router_bt, earlier, tab,
            counts0)
        return x1, hn, bucket, rank, counts, fstate

    tiles_per_seq = dec_len // TM
    ctx_cond = lambda i: n_dec
    dec_cond = lambda i: i // tiles_per_seq
    h0_ctx = jnp.zeros((n_ctx, 2, D_RNN), F32)
    counts0 = jnp.zeros((ROUTER_ROWS, LANES), F32)
    x1_ctx, hn_ctx, bucket_ctx, rank_ctx, counts, st = mixer(x_prompt, h0_ctx, ctx_cond, lambda p: n_dec // PAIR,
                                                             False, counts0)
    new_state = st.astype(state_rglru.dtype)[:, None]
    x1_dec, hn_dec, bucket_dec, rank_dec, counts, _ = mixer(x_sample, state_rglru[:, l].astype(F32), dec_cond,
                                                            lambda p: p, True, counts)

    dest, sched, n_slots = _schedule(jnp.concatenate([bucket_ctx, bucket_dec]),
                                     jnp.concatenate([rank_ctx, rank_dec]), counts)
    xs = _dispatch(dest, hn_ctx, hn_dec, n_slots)
    ys = _experts(sched, xs, router_w, router_b, wg_b, wu_b, wd_b, row(g_post_ffn[l]))
    y_prompt = _combine(dest, ys, x1_ctx, mod3, ctx_cond, 0)
    y_sample = _combine(dest, ys, x1_dec, mod3, dec_cond, n_ctx_tok // TM)
    return (y_prompt.reshape(x_prompt.shape), y_sample.reshape(x_sample.shape), new_state)
```

```python
import functools
import math

import jax
import jax.numpy as jnp
from jax import lax
from jax.experimental import pallas as pl
from jax.experimental.pallas import tpu as pltpu

D_MODEL = 1024
D_RNN = 512
D_SGU = 512
N_HEADS_RNN = 8
HEAD_RNN = D_RNN // N_HEADS_RNN
N_HEADS_SGU = 8
HEAD_SGU = D_SGU // N_HEADS_SGU
CHUNK = 128
GRID_W = 64
RG_C = 8.0
N_GROUPS = 4
EXPERTS_PER_GROUP = 4
N_EXPERTS = N_GROUPS * EXPERTS_PER_GROUP
D_EXPERT = 512
EPS = 1e-6
POS_BASE = 10000.0

LANES = 128
SUBLANES = 8
CONV_W = 4
CONV_LEFT = 2
PAIR = 2
RNN_BLOCKS = D_RNN // LANES
TMJ_ROWS = PAIR * RNN_BLOCKS
ROUTER_LANES = LANES
E_LANE0 = N_GROUPS
ROUTER_ROWS = 32
E_ROW0 = SUBLANES

PAIRS_PER_GROUP = EXPERTS_PER_GROUP * (EXPERTS_PER_GROUP - 1) // 2
N_BUCKETS = N_GROUPS * PAIRS_PER_GROUP

ROW_TILES = D_MODEL // LANES

TM = 512
TD = 1024
DMA_UNROLL = 8
COMBINE_AHEAD = 2
COMBINE_SLOTS = COMBINE_AHEAD + 1
EXPERT_ROW_PARTS = 2
TILE_ROW_PARTS = 1
TMX = 512
TT = TM // PAIR
LC = 256
TS = 16
PB = 2
VMEM_LIMIT = 56 * 1024 * 1024

F32 = jnp.float32
BF16 = jnp.bfloat16


def _params(sem):
    return pltpu.CompilerParams(dimension_semantics=sem, vmem_limit_bytes=VMEM_LIMIT)


def _rms(x):
    return x * lax.rsqrt(jnp.mean(x * x, axis=-1, keepdims=True) + EPS)


def _sigmoid(x):
    return 0.5 * jnp.tanh(0.5 * x) + 0.5


def _mod_kernel(cond_ref, w_ref, b_ref, o_ref):
    c = cond_ref[...]
    s = c * _sigmoid(c)
    o_ref[...] = jnp.dot(s.astype(BF16), w_ref[...].astype(BF16),
                         preferred_element_type=F32) + b_ref[...]


def _modulation(cond, w_mod, b_mod):
    n = w_mod.shape[1]
    bn = 1024
    return pl.pallas_call(
        _mod_kernel,
        out_shape=jax.ShapeDtypeStruct((cond.shape[0], n), F32),
        grid=(n // bn,),
        in_specs=[pl.BlockSpec(cond.shape, lambda j: (0, 0)),
                  pl.BlockSpec((D_MODEL, bn), lambda j: (0, j)),
                  pl.BlockSpec((1, bn), lambda j: (0, j))],
        out_specs=pl.BlockSpec((cond.shape[0], bn), lambda j: (0, j)),
        compiler_params=_params(("arbitrary",)),
        name="modulation",
    )(cond, w_mod, b_mod.reshape(1, n))


def _pos_kernel(o_ref):
    n_freq = D_MODEL // 4
    k = lax.broadcasted_iota(jnp.int32, (GRID_W, n_freq), 1).astype(F32)
    p = lax.broadcasted_iota(jnp.int32, (GRID_W, n_freq), 0).astype(F32)
    freq = jnp.exp(-math.log(POS_BASE) * k / n_freq)
    ang = p * freq
    o_ref[:, 0:n_freq] = jnp.sin(ang)
    o_ref[:, n_freq:2 * n_freq] = jnp.cos(ang)


def _pos_table():
    return pl.pallas_call(
        _pos_kernel,
        out_shape=jax.ShapeDtypeStruct((GRID_W, D_MODEL // 2), F32),
        name="pos_table",
    )()


def _add_pos(x, pos_refs, q0):
    if pos_refs is None:
        return x
    rows_ref, cols_ref = pos_refs
    reps = x.shape[0] // GRID_W
    rpart = jnp.concatenate(
        [jnp.broadcast_to(rows_ref[q:q + 1, :], (GRID_W, D_MODEL // 2)) for q in range(q0, q0 + reps)], axis=0)
    cpart = jnp.concatenate([cols_ref[...]] * reps, axis=0)
    return jnp.concatenate([x[:, :D_MODEL // 2] + rpart, x[:, D_MODEL // 2:] + cpart], axis=1)


def _load_x(x_ref, pos_refs, r0, n):
    return _add_pos(x_ref[r0:r0 + n, :], pos_refs, r0 // GRID_W)


def _premix_kernel(*refs, add_pos):
    refs = list(refs)
    x_ref = refs.pop(0)
    pos_refs = (refs.pop(0), refs.pop(0)) if add_pos else None
    mod_ref, g_ref, win_ref, sgug_ref, sguw_ref, sgub_ref, xr_ref, gg_ref, ys_ref = refs
    hn = []
    for s in range(PAIR):
        shift = mod_ref[s, :, 0:D_MODEL]
        scale = mod_ref[s, :, D_MODEL:2 * D_MODEL]
        hn.append(_rms(_add_pos(x_ref[s], pos_refs, 0)) * (g_ref[...] * (1.0 + scale)) + shift)
    z = jnp.dot(jnp.concatenate(hn, axis=0).astype(BF16), win_ref[...],
                preferred_element_type=F32)
    half = D_SGU // 2
    heads_per_half = N_HEADS_SGU // 2
    lane_head = lax.broadcasted_iota(jnp.int32, (CHUNK, half), 1) // HEAD_SGU
    for s in range(PAIR):
        zs = z[s * TT:(s + 1) * TT]
        gg = jax.nn.gelu(zs[:, D_RNN:2 * D_RNN])
        for k in range(RNN_BLOCKS):
            rows = pl.ds(s * RNN_BLOCKS + k, TT, stride=TMJ_ROWS)
            xr_ref[rows, :] = zs[:, k * LANES:(k + 1) * LANES]
            gg_ref[rows, :] = gg[:, k * LANES:(k + 1) * LANES]
        u = zs[:, 2 * D_RNN:2 * D_RNN + D_SGU]
        vn = (_rms(zs[:, 2 * D_RNN + D_SGU:]) * sgug_ref[...]).astype(BF16)
        for c in range(TT // CHUNK):
            rows = slice(c * CHUNK, (c + 1) * CHUNK)
            halves = []
            for hf in range(2):
                r = jnp.dot(sguw_ref[hf], vn[rows, hf * half:(hf + 1) * half],
                            preferred_element_type=F32)
                sel = jnp.zeros((CHUNK, half), F32)
                for h in range(heads_per_half):
                    sel = jnp.where(lane_head == h, r[h * CHUNK:(h + 1) * CHUNK], sel)
                halves.append(sel)
            gatev = jnp.concatenate(halves, axis=1) + sgub_ref[...]
            ys_ref[s, rows, :] = (u[rows] * gatev).astype(BF16)


def _premix(x, mod3, cond_block, g_pre, w_in_b, sgu_g, sgu_w_b, sgu_bias_tile, pos_tab):
    n_seq, seq_len, _ = x.shape
    n_pairs, n_tiles = n_seq // PAIR, seq_len // TT
    add_pos = pos_tab is not None
    const2 = lambda p, j: (0, 0)
    in_specs = [pl.BlockSpec((PAIR, TT, D_MODEL), lambda p, j: (p, j, 0))]
    args = [x]
    if add_pos:
        reps = TT // GRID_W
        in_specs += [pl.BlockSpec((None, reps, D_MODEL // 2), lambda p, j: (j, 0, 0)),
                     pl.BlockSpec((GRID_W, D_MODEL // 2), const2)]
        args += [pos_tab.reshape(GRID_W // reps, reps, D_MODEL // 2), pos_tab]
    in_specs += [pl.BlockSpec((PAIR, 1, 6 * D_MODEL), lambda p, j: (cond_block(p), 0, 0)),
                 pl.BlockSpec((1, D_MODEL), const2),
                 pl.BlockSpec((D_MODEL, 2 * D_RNN + 2 * D_SGU), const2),
                 pl.BlockSpec((1, D_SGU), const2),
                 pl.BlockSpec((2, 4 * CHUNK, CHUNK), lambda p, j: (0, 0, 0)),
                 pl.BlockSpec((CHUNK, D_SGU), const2)]
    args += [mod3, g_pre, w_in_b, sgu_g, sgu_w_b, sgu_bias_tile]
    tmj = jax.ShapeDtypeStruct((n_pairs * seq_len * TMJ_ROWS, LANES), F32)
    tmj_spec = pl.BlockSpec((TT * TMJ_ROWS, LANES), lambda p, j: (p * n_tiles + j, 0))
    xr, gg, y_sgu = pl.pallas_call(
        functools.partial(_premix_kernel, add_pos=add_pos),
        out_shape=(tmj, tmj, jax.ShapeDtypeStruct((n_seq, seq_len, D_SGU), BF16)),
        grid=(n_pairs, n_tiles),
        in_specs=in_specs,
        out_specs=(tmj_spec, tmj_spec, pl.BlockSpec((PAIR, TT, D_SGU), lambda p, j: (p, j, 0))),
        compiler_params=_params(("parallel", "parallel")),
        name="premix",
    )(*args)
    shape4 = (n_pairs, seq_len, TMJ_ROWS, LANES)
    return xr.reshape(shape4), gg.reshape(shape4), y_sgu


def _scan_kernel(*refs, reverse, n_chunks):
    if reverse:
        (xprev_ref, x_ref, xnext_ref, gg_ref, hf_ref, h0_ref, cw_ref, cb_ref, wg_ref, bg_ref, lam_ref,
         y_ref, fs_ref, xwin, xc_s, r_s, i_s, a_s, b_s, y_s, hcar) = refs
    else:
        (xprev_ref, x_ref, xnext_ref, h0_ref, cw_ref, cb_ref, wg_ref, bg_ref, lam_ref,
         hf_ref, fs_ref, xwin, xc_s, r_s, i_s, a_s, b_s, hcar) = refs
    c = pl.program_id(1)
    chunk = n_chunks - 1 - c if reverse else c
    sub_rows = TS * TMJ_ROWS

    def rows_of(pb, t0, n_steps):
        return pl.ds(pl.multiple_of((pb * LC + t0) * TMJ_ROWS, TMJ_ROWS), n_steps * TMJ_ROWS)

    @pl.when(c == 0)
    def _():
        hcar[...] = h0_ref[...]

    xwin[:, 0:CONV_LEFT] = jnp.where(chunk > 0, xprev_ref[...], 0.0)
    xwin[:, LC + CONV_LEFT:LC + CONV_W - 1] = jnp.where(chunk < n_chunks - 1, xnext_ref[...], 0.0)

    def copy(i, carry):
        t0 = pl.multiple_of(i * TS, TS)
        xwin[:, pl.ds(t0 + CONV_LEFT, TS)] = x_ref[:, pl.ds(t0, TS)]
        return carry

    lax.fori_loop(0, LC // TS, copy, 0)

    def conv(i, carry):
        t0 = pl.multiple_of(i * TS, TS)
        xc = cb_ref[...] + cw_ref[0] * xwin[:, pl.ds(t0, TS)]
        for k in range(1, CONV_W):
            xc = xc + cw_ref[k] * xwin[:, pl.ds(t0 + k, TS)]
        for pb in range(PB):
            xc_s[rows_of(pb, t0, TS), :] = xc[pb].reshape(sub_rows, LANES)
        return carry

    lax.fori_loop(0, LC // TS, conv, 0)

    n_rows = PB * LC * PAIR
    for k in range(RNN_BLOCKS):
        rows = pl.ds(k, n_rows, stride=RNN_BLOCKS)
        g = jnp.dot(xc_s[rows, :].astype(BF16), wg_ref[k], preferred_element_type=F32)
        r_s[rows, :] = g[:, :LANES]
        i_s[rows, :] = g[:, LANES:]

    neg_lam = -lam_ref[...]
    softplus = jnp.maximum(neg_lam, 0.0) + jnp.log(1.0 + jnp.exp(-jnp.abs(neg_lam)))
    half_decay = (-0.5 * RG_C * math.log2(math.e)) * softplus

    def gates(i, carry):
        t0 = pl.multiple_of(i * TS, TS)
        for pb in range(PB):
            rows = rows_of(pb, t0, TS)
            tile = lambda ref: ref[rows, :].reshape(TS, TMJ_ROWS, LANES)
            tr = jnp.tanh(tile(r_s) + bg_ref[0])
            ti = jnp.tanh(tile(i_s) + bg_ref[1])
            log2_a = tr * half_decay + half_decay
            a = jnp.exp2(log2_a)
            q = jnp.tanh(log2_a * (-math.log(2.0))) * (a * a + 1.0)
            b = jnp.where(q > 0.0, q * lax.rsqrt(q), 0.0) * ((ti + 1.0) * tile(xc_s))
            a_s[rows, :] = a.reshape(sub_rows, LANES)
            b_s[rows, :] = b.reshape(sub_rows, LANES)
        return carry

    lax.fori_loop(0, LC // TS, gates, 0)

    def step(j, hs):
        t = LC - 1 - j if reverse else j
        out = []
        for pb in range(PB):
            rows = rows_of(pb, t, 1)
            h = a_s[rows, :] * hs[pb] + b_s[rows, :]
            if reverse:
                y_s[rows, :] = (hf_ref[pb, t] + h) * gg_ref[pb, t]
            else:
                hf_ref[pb, t] = h
            out.append(h)
        return tuple(out)

    hs = lax.fori_loop(0, LC, step, tuple(hcar[pb] for pb in range(PB)), unroll=8)
    for pb in range(PB):
        hcar[pb] = hs[pb]
        fs_ref[pb] = hs[pb]

    if reverse:
        for pb in range(PB):
            for s in range(PAIR):
                cols = [y_s[pl.ds(pb * LC * TMJ_ROWS + s * RNN_BLOCKS + k, LC, stride=TMJ_ROWS), :]
                        for k in range(RNN_BLOCKS)]
                y_ref[pb * PAIR + s] = jnp.concatenate(cols, axis=1).astype(BF16)


def _scan(xr, gg, hf, h0, conv_w, conv_b, w_gates, b_gates, lam, direction):
    n_pairs, seq_len = xr.shape[:2]
    n_chunks = seq_len // LC
    reverse = direction == 1
    pos = (lambda c: n_chunks - 1 - c) if reverse else (lambda c: c)
    tmj_blk = pl.BlockSpec((PB, LC, TMJ_ROWS, LANES), lambda i, c: (i, pos(c), 0, 0))
    state_blk = pl.BlockSpec((PB, TMJ_ROWS, LANES), lambda i, c: (i, 0, 0))
    per_dir = lambda *shape: pl.BlockSpec((None,) + shape, lambda i, c: (direction,) + (0,) * len(shape))
    in_specs = [
        pl.BlockSpec((PB, CONV_LEFT, TMJ_ROWS, LANES),
                     lambda i, c: (i, jnp.maximum(pos(c) * (LC // CONV_LEFT) - 1, 0), 0, 0)),
        tmj_blk,
        pl.BlockSpec((PB, 1, TMJ_ROWS, LANES), lambda i, c: (i, jnp.minimum((pos(c) + 1) * LC, seq_len - 1), 0, 0)),
    ]
    args = [xr, xr, xr]
    if reverse:
        in_specs += [tmj_blk, tmj_blk]
        args += [gg, hf]
    in_specs += [state_blk,
                 pl.BlockSpec((CONV_W, TMJ_ROWS, LANES), lambda i, c: (0, 0, 0)),
                 pl.BlockSpec((TMJ_ROWS, LANES), lambda i, c: (0, 0)),
                 per_dir(RNN_BLOCKS, LANES, 2 * LANES),
                 per_dir(2, TMJ_ROWS, LANES),
                 per_dir(TMJ_ROWS, LANES)]
    args += [h0, conv_w, conv_b, w_gates, b_gates, lam]
    flat = pltpu.VMEM((PB * LC * TMJ_ROWS, LANES), F32)
    scratch = [pltpu.VMEM((PB, LC + CONV_W - 1, TMJ_ROWS, LANES), F32)] + [flat] * (6 if reverse else 5)
    scratch += [pltpu.VMEM((PB, TMJ_ROWS, LANES), F32)]
    state = jax.ShapeDtypeStruct((n_pairs, TMJ_ROWS, LANES), F32)
    if reverse:
        out_shape = (jax.ShapeDtypeStruct((n_pairs * PAIR, seq_len, D_RNN), BF16), state)
        out_specs = (pl.BlockSpec((PB * PAIR, LC, D_RNN), lambda i, c: (i, pos(c), 0)), state_blk)
    else:
        out_shape = (jax.ShapeDtypeStruct(xr.shape, F32), state)
        out_specs = (tmj_blk, state_blk)
    return pl.pallas_call(
        functools.partial(_scan_kernel, reverse=reverse, n_chunks=n_chunks),
        out_shape=out_shape,
        grid=(n_pairs // PB, n_chunks),
        in_specs=in_specs,
        out_specs=out_specs,
        scratch_shapes=scratch,
        compiler_params=_params(("parallel", "arbitrary")),
        name="scan_bwd" if reverse else "scan_fwd",
    )(*args)


def _route(lt):
    n = lt.shape[1]
    row = lax.broadcasted_iota(jnp.int32, (EXPERTS_PER_GROUP, n), 0)
    neg = jnp.float32(-jnp.inf)

    def arg_max(v):
        m = jnp.max(v, axis=0, keepdims=True)
        return jnp.min(jnp.where(v == m, row, EXPERTS_PER_GROUP), axis=0, keepdims=True)

    g_idx = arg_max(lt[0:N_GROUPS])
    el = lt[E_ROW0:E_ROW0 + EXPERTS_PER_GROUP]
    for g in range(1, N_GROUPS):
        first = E_ROW0 + g * EXPERTS_PER_GROUP
        el = jnp.where(g_idx == g, lt[first:first + EXPERTS_PER_GROUP], el)
    i1 = arg_max(el)
    i2 = arg_max(jnp.where(row == i1, neg, el))
    ja = jnp.minimum(i1, i2)
    jb = jnp.maximum(i1, i2)
    pair = (ja * (2 * EXPERTS_PER_GROUP - 1 - ja)) // 2 + (jb - ja - 1)
    return g_idx * PAIRS_PER_GROUP + pair


def _store_token_major(ref, x, t0=0):
    n = x.shape[0]
    for k in range(ROW_TILES):
        ref[pl.ds(t0 * ROW_TILES + k, n, stride=ROW_TILES), :] = x[:, k * LANES:(k + 1) * LANES]


def _load_token_major(ref, n):
    return jnp.concatenate([ref[pl.ds(k, n, stride=ROW_TILES), :] for k in range(ROW_TILES)], axis=1)


def _postmix_kernel(*refs, add_pos):
    refs = list(refs)
    x_ref = refs.pop(0)
    pos_refs = (refs.pop(0), refs.pop(0)) if add_pos else None
    (yr_ref, ys_ref, mod_ref, gpost_ref, gpre_ref, wout_ref, rw_ref, rb_ref, earlier_ref, cnt0_ref,
     x1_ref, hn_ref, rt_ref, cnt_ref, run_ref) = refs

    @pl.when(pl.program_id(0) == 0)
    def _():
        run_ref[...] = cnt0_ref[...]

    gate1 = mod_ref[0, :, 2 * D_MODEL:3 * D_MODEL]
    shift2 = mod_ref[0, :, 3 * D_MODEL:4 * D_MODEL]
    scale2 = mod_ref[0, :, 4 * D_MODEL:5 * D_MODEL]
    y = (jnp.dot(yr_ref[...], wout_ref[0:D_RNN, :], preferred_element_type=F32)
         + jnp.dot(ys_ref[...], wout_ref[D_RNN:, :], preferred_element_type=F32))
    x1 = _load_x(x_ref, pos_refs, 0, TM) + _rms(y) * (gate1 * gpost_ref[...])
    x1_ref[...] = x1
    hn = _rms(x1) * (gpre_ref[...] * (1.0 + scale2)) + shift2
    _store_token_major(hn_ref, hn)
    lt = lax.dot_general(rw_ref[...], hn.astype(BF16), (((1,), (1,)), ((), ())),
                         preferred_element_type=F32) + rb_ref[:, 0:1]
    bucket = _route(lt)
    onehot = lax.broadcasted_iota(jnp.int32, (ROUTER_ROWS, TM), 0) == bucket
    before = jnp.dot(onehot.astype(BF16), earlier_ref[...], preferred_element_type=F32) + run_ref[:, 0:1]
    rank = jnp.sum(jnp.where(onehot, before, 0.0), axis=0, keepdims=True).astype(jnp.int32)
    row = lax.broadcasted_iota(jnp.int32, (SUBLANES, TM), 0)
    rt_ref[...] = jnp.where(row == 0, bucket, jnp.where(row == 1, rank, 0))
    run_ref[...] += jnp.sum(onehot.astype(F32), axis=1, keepdims=True)
    cnt_ref[...] = run_ref[...]


def _postmix(x, y_rnn, y_sgu, mod3, cond_of_tile, g_post, g_pre, w_out_b, router_wt, router_bt, earlier, pos_tab,
             counts0):
    n_tok = x.shape[0]
    add_pos = pos_tab is not None
    tok = lambda i: (i, 0)
    const2 = lambda i: (0, 0)
    in_specs = [pl.BlockSpec((TM, D_MODEL), tok)]
    args = [x]
    if add_pos:
        reps = TM // GRID_W
        tiles_per_seq = GRID_W // reps
        in_specs += [pl.BlockSpec((None, reps, D_MODEL // 2), lambda i: (i % tiles_per_seq, 0, 0)),
                     pl.BlockSpec((GRID_W, D_MODEL // 2), const2)]
        args += [pos_tab.reshape(tiles_per_seq, reps, D_MODEL // 2), pos_tab]
    in_specs += [pl.BlockSpec((TM, D_RNN), tok),
                 pl.BlockSpec((TM, D_SGU), tok),
                 pl.BlockSpec((1, 1, 6 * D_MODEL), lambda i: (cond_of_tile(i), 0, 0)),
                 pl.BlockSpec((1, D_MODEL), const2),
                 pl.BlockSpec((1, D_MODEL), const2),
                 pl.BlockSpec((D_MODEL, D_MODEL), const2),
                 pl.BlockSpec((ROUTER_ROWS, D_MODEL), const2),
                 pl.BlockSpec((ROUTER_ROWS, LANES), const2),
                 pl.BlockSpec((TM, TM), const2),
                 pl.BlockSpec((ROUTER_ROWS, LANES), const2)]
    args += [y_rnn, y_sgu, mod3, g_post, g_pre, w_out_b, router_wt, router_bt, earlier, counts0]
    n_tiles = n_tok // TM
    counts_spec = pl.BlockSpec((ROUTER_ROWS, LANES), const2)
    x1, hn, route, counts = pl.pallas_call(
        functools.partial(_postmix_kernel, add_pos=add_pos),
        out_shape=(jax.ShapeDtypeStruct((n_tok, D_MODEL), F32),
                   jax.ShapeDtypeStruct((n_tok * ROW_TILES, LANES), F32),
                   jax.ShapeDtypeStruct((n_tiles * SUBLANES, TM), jnp.int32),
                   jax.ShapeDtypeStruct((ROUTER_ROWS, LANES), F32)),
        grid=(n_tiles,),
        in_specs=in_specs,
        out_specs=(pl.BlockSpec((TM, D_MODEL), tok),
                   pl.BlockSpec((TM * ROW_TILES, LANES), tok),
                   pl.BlockSpec((SUBLANES, TM), tok),
                   counts_spec),
        scratch_shapes=[pltpu.VMEM((ROUTER_ROWS, LANES), F32)],
        compiler_params=_params(("arbitrary",)),
        name="postmix",
    )(*args)
    route = route.reshape(n_tiles, SUBLANES, TM)
    return x1, hn, route[:, 0].reshape(n_tok), route[:, 1].reshape(n_tok), counts


def _token_rows(ref, t):
    return ref.at[pl.ds(pl.multiple_of(t * ROW_TILES, ROW_TILES), ROW_TILES), :]


def _dispatch_kernel(dest_ref, hc_ref, hs_ref, xs_ref, sem, *, n_ctx_steps):
    i = pl.program_id(0)
    base = i * TD

    def scatter(src_ref):
        def start(g, carry):
            for u in range(DMA_UNROLL):
                r = g * DMA_UNROLL + u
                pltpu.make_async_copy(_token_rows(src_ref, r), _token_rows(xs_ref, dest_ref[base + r]),
                                      sem).start(priority=u % 2)
            return carry

        lax.fori_loop(0, TD // DMA_UNROLL, start, 0)
        pltpu.make_async_copy(src_ref, xs_ref.at[pl.ds(0, TD * ROW_TILES), :], sem).wait()

    @pl.when(i < n_ctx_steps)
    def _():
        scatter(hc_ref)

    @pl.when(i >= n_ctx_steps)
    def _():
        scatter(hs_ref)


def _dispatch(dest, hn_ctx, hn_dec, n_slots):
    n_ctx_steps = hn_ctx.shape[0] // (TD * ROW_TILES)
    n_dec_steps = hn_dec.shape[0] // (TD * ROW_TILES)
    return pl.pallas_call(
        functools.partial(_dispatch_kernel, n_ctx_steps=n_ctx_steps),
        out_shape=jax.ShapeDtypeStruct((n_slots * ROW_TILES, LANES), F32),
        grid_spec=pltpu.PrefetchScalarGridSpec(
            num_scalar_prefetch=1,
            grid=(n_ctx_steps + n_dec_steps,),
            in_specs=[pl.BlockSpec((TD * ROW_TILES, LANES), lambda i, d: (jnp.minimum(i, n_ctx_steps - 1), 0)),
                      pl.BlockSpec((TD * ROW_TILES, LANES), lambda i, d: (jnp.maximum(i - n_ctx_steps, 0), 0))],
            out_specs=pl.BlockSpec(memory_space=pl.ANY),
            scratch_shapes=[pltpu.SemaphoreType.DMA(())]),
        compiler_params=_params(("arbitrary",)),
        name="dispatch",
    )(dest, hn_ctx, hn_dec)


def _experts_kernel(ea_ref, eb_ref, nv_ref, xs_ref, rw_ref, rb_ref,
                    wga_ref, wua_ref, wda_ref, wgb_ref, wub_ref, wdb_ref, gpost_ref, ys_ref):
    i = pl.program_id(0)
    nv = nv_ref[i]

    part = TMX // EXPERT_ROW_PARTS

    @pl.when(nv == 0)
    def _():
        ys_ref[...] = jnp.zeros_like(ys_ref)

    def process(n_parts):
        n = n_parts * part
        row = lax.broadcasted_iota(jnp.int32, (n, 1), 0)
        xb = jnp.where(row < nv, _load_token_major(xs_ref, n), 0.0).astype(BF16)
        logits = jnp.dot(xb, rw_ref[...], preferred_element_type=F32) + rb_ref[...]
        lane = lax.broadcasted_iota(jnp.int32, logits.shape, 1)
        ea = ea_ref[i]
        eb = eb_ref[i]
        gmask = lane < N_GROUPS
        gl = jnp.where(gmask, logits, -jnp.inf)
        gmax = jnp.max(gl, axis=-1, keepdims=True)
        gexp = jnp.where(gmask, jnp.exp(gl - gmax), 0.0)
        g_own = jnp.sum(jnp.where(lane == ea // EXPERTS_PER_GROUP, gexp, 0.0), axis=-1, keepdims=True)
        g_w = g_own / jnp.sum(gexp, axis=-1, keepdims=True)
        la = jnp.sum(jnp.where(lane == ea + E_LANE0, logits, 0.0), axis=-1, keepdims=True)
        lb = jnp.sum(jnp.where(lane == eb + E_LANE0, logits, 0.0), axis=-1, keepdims=True)
        m = jnp.maximum(la, lb)
        pa = jnp.exp(la - m)
        pb = jnp.exp(lb - m)
        inv = g_w / (pa + pb)

        def hidden(x, wg_ref, wu_ref, w):
            g = jnp.dot(x, wg_ref[0], preferred_element_type=F32)
            u = jnp.dot(x, wu_ref[0], preferred_element_type=F32)
            return ((g * _sigmoid(g)) * u * w).astype(BF16)

        ys = []
        for h in range(n_parts):
            rows = slice(h * part, (h + 1) * part)
            act_a = hidden(xb[rows], wga_ref, wua_ref, (pa * inv)[rows])
            act_b = hidden(xb[rows], wgb_ref, wub_ref, (pb * inv)[rows])
            y = (jnp.dot(act_a, wda_ref[0], preferred_element_type=F32)
                 + jnp.dot(act_b, wdb_ref[0], preferred_element_type=F32))
            ys.append(_rms(y) * gpost_ref[...])
        _store_token_major(ys_ref, jnp.concatenate(ys, axis=0))
        if n < TMX:
            ys_ref[n * ROW_TILES:, :] = jnp.zeros(((TMX - n) * ROW_TILES, LANES), F32)

    for n_parts in range(1, EXPERT_ROW_PARTS + 1):
        pl.when((nv > (n_parts - 1) * part) & (nv <= n_parts * part))(functools.partial(process, n_parts))


def _experts(sched, xs, router_w, router_b, wg_b, wu_b, wd_b, g_post):
    ea, eb, nv = sched
    n_tiles = ea.shape[0]
    rows = lambda i, ea, eb, nv: (i, 0)
    const2 = lambda i, ea, eb, nv: (0, 0)
    exp_a = lambda i, ea, eb, nv: (ea[i], 0, 0)
    exp_b = lambda i, ea, eb, nv: (eb[i], 0, 0)
    w_in_spec = lambda m: pl.BlockSpec((1, D_MODEL, D_EXPERT), m)
    w_out_spec = lambda m: pl.BlockSpec((1, D_EXPERT, D_MODEL), m)
    return pl.pallas_call(
        _experts_kernel,
        out_shape=jax.ShapeDtypeStruct(xs.shape, F32),
        grid_spec=pltpu.PrefetchScalarGridSpec(
            num_scalar_prefetch=3,
            grid=(n_tiles,),
            in_specs=[pl.BlockSpec((TMX * ROW_TILES, LANES), rows),
                      pl.BlockSpec((D_MODEL, ROUTER_LANES), const2),
                      pl.BlockSpec((1, ROUTER_LANES), const2),
                      w_in_spec(exp_a), w_in_spec(exp_a), w_out_spec(exp_a),
                      w_in_spec(exp_b), w_in_spec(exp_b), w_out_spec(exp_b),
                      pl.BlockSpec((1, D_MODEL), const2)],
            out_specs=pl.BlockSpec((TMX * ROW_TILES, LANES), rows)),
        compiler_params=_params(("arbitrary",)),
        name="experts",
    )(ea, eb, nv, xs, router_w, router_b, wg_b, wu_b, wd_b, wg_b, wu_b, wd_b, g_post)


def _combine_kernel(dest_ref, ys_ref, x1_ref, mod_ref, o_ref, ybuf, sems, *, tile0):
    i = pl.program_id(0)
    n = pl.num_programs(0)

    def start_row(tile, slot, r, queue):
        src = _token_rows(ys_ref, dest_ref[(tile + tile0) * TM + r])
        pltpu.make_async_copy(src, _token_rows(ybuf.at[slot], r), sems.at[slot]).start(priority=queue)

    @pl.when(i == 0)
    def _():
        for tile in range(COMBINE_AHEAD):
            def start(g, carry, tile=tile):
                for u in range(DMA_UNROLL):
                    start_row(tile, tile, g * DMA_UNROLL + u, u % 2)
                return carry

            lax.fori_loop(0, TM // DMA_UNROLL, start, 0)

    def step(slot, prefetch):
        pltpu.make_async_copy(ys_ref.at[pl.ds(0, TM * ROW_TILES), :], ybuf.at[slot], sems.at[slot]).wait()
        if prefetch:
            for r in range(TM):
                start_row(i + COMBINE_AHEAD, (slot + COMBINE_AHEAD) % COMBINE_SLOTS, r, r % 2)
        gate2 = mod_ref[0, :, 5 * D_MODEL:6 * D_MODEL]
        o_ref[...] = x1_ref[...] + gate2 * _load_token_major(ybuf.at[slot], TM)

    for slot in range(COMBINE_SLOTS):
        for prefetch in (True, False):
            more = i + COMBINE_AHEAD < n
            pl.when((i % COMBINE_SLOTS == slot) & (more if prefetch else jnp.logical_not(more)))(
                functools.partial(step, slot, prefetch))


def _combine(dest, ys, x1, mod3, cond_of_tile, tile0):
    n_tok = x1.shape[0]
    return pl.pallas_call(
        functools.partial(_combine_kernel, tile0=tile0),
        out_shape=jax.ShapeDtypeStruct((n_tok, D_MODEL), F32),
        grid_spec=pltpu.PrefetchScalarGridSpec(
            num_scalar_prefetch=1,
            grid=(n_tok // TM,),
            in_specs=[pl.BlockSpec(memory_space=pl.ANY),
                      pl.BlockSpec((TM, D_MODEL), lambda i, d: (i, 0)),
                      pl.BlockSpec((1, 1, 6 * D_MODEL), lambda i, d: (cond_of_tile(i), 0, 0))],
            out_specs=pl.BlockSpec((TM, D_MODEL), lambda i, d: (i, 0)),
            scratch_shapes=[pltpu.VMEM((COMBINE_SLOTS, TM * ROW_TILES, LANES), F32),
                            pltpu.SemaphoreType.DMA((COMBINE_SLOTS,))]),
        compiler_params=_params(("arbitrary",)),
        name="combine",
    )(dest, ys, x1, mod3)


def _schedule(bucket, rank, counts):
    n_tok = bucket.shape[0]
    n_max = n_tok // TMX + N_BUCKETS
    cnt = counts[:N_BUCKETS, 0].astype(jnp.int32)
    tiles = (cnt + TMX - 1) // TMX
    tile_end = jnp.cumsum(tiles)
    tile_start = tile_end - tiles
    ids = jnp.arange(N_BUCKETS, dtype=jnp.int32)
    slot0 = jnp.sum(jnp.where(bucket[:, None] == ids[None, :], (tile_start * TMX)[None, :], 0), axis=1)
    dest = slot0 + rank
    i = jnp.arange(n_max, dtype=jnp.int32)
    total = tile_end[-1]
    valid = i < total
    tb = jnp.sum((jnp.minimum(i, total - 1)[:, None] >= tile_end[None, :]).astype(jnp.int32), axis=1)
    pairs = [(a, b) for a in range(EXPERTS_PER_GROUP) for b in range(a + 1, EXPERTS_PER_GROUP)]
    ea_tab = jnp.array([g * EXPERTS_PER_GROUP + a for g in range(N_GROUPS) for a, _ in pairs], jnp.int32)
    eb_tab = jnp.array([g * EXPERTS_PER_GROUP + b for g in range(N_GROUPS) for _, b in pairs], jnp.int32)
    ea = ea_tab[tb]
    eb = eb_tab[tb]
    nv = jnp.where(valid, jnp.clip(cnt[tb] - (i - tile_start[tb]) * TMX, 0, TMX), 0)
    return dest, (ea, eb, nv), n_max * TMX


def _block_diag_gates(rg_wa, rg_wx):
    heads = LANES // HEAD_RNN

    def bd(w):
        w = w.reshape(2, RNN_BLOCKS, heads, HEAD_RNN, HEAD_RNN)
        eye = jnp.eye(heads, dtype=w.dtype)
        full = jnp.einsum('dghij,hk->dghikj', w, eye)
        return full.reshape(2, RNN_BLOCKS, LANES, LANES)

    return jnp.concatenate([bd(rg_wa), bd(rg_wx)], axis=-1).astype(BF16)


def _row_tile(v):
    blocks = v.reshape(v.shape[:-1] + (RNN_BLOCKS, LANES))
    return jnp.concatenate([blocks] * PAIR, axis=-2)


def _to_time_major_state(h):
    return h.reshape(h.shape[0] // PAIR, TMJ_ROWS, LANES)


def kernel(x_prompt, x_sample, state_rglru, c, c_ctx, w_mod, b_mod, g_pre_mix, g_post_mix, g_pre_ffn,
           g_post_ffn, w_in, conv_w, conv_b, rg_wa, rg_ba, rg_wx, rg_bx, rg_lambda, sgu_g, sgu_w, sgu_b,
           w_out, router_g_w, router_g_b, router_e_w, router_e_b, exp_w_gate, exp_w_up, exp_w_down):
    assert w_mod.shape[0] == 1, "single-layer trunk"
    n_ctx, ctx_len, _ = x_prompt.shape
    n_dec, dec_len, _ = x_sample.shape
    l = 0

    n_cond = SUBLANES
    assert n_dec % PAIR == 0 and n_dec + PAIR <= n_cond
    cond = jnp.zeros((n_cond, D_MODEL), F32).at[:n_dec].set(c).at[n_dec:n_dec + PAIR].set(c_ctx)
    mod3 = _modulation(cond, w_mod[l], b_mod[l]).reshape(n_cond, 1, 6 * D_MODEL)
    pos_tab = _pos_table()

    w_in_b = w_in[l].astype(BF16)
    w_out_b = w_out[l].astype(BF16)
    sgu_w_b = sgu_w[l].reshape(2, 4 * CHUNK, CHUNK).astype(BF16)
    sgu_bias_tile = jnp.repeat(sgu_b[l].T, HEAD_SGU, axis=1)
    w_gates = _block_diag_gates(rg_wa[l], rg_wx[l])
    b_gates = 0.5 * jnp.stack([_row_tile(rg_ba[l]), _row_tile(rg_bx[l])], axis=1)
    lam = _row_tile(rg_lambda[l])
    conv_w_t = 0.5 * _row_tile(conv_w[l])
    conv_b_t = 0.5 * _row_tile(conv_b[l])
    router_w = jnp.zeros((D_MODEL, ROUTER_LANES), F32)
    router_w = router_w.at[:, :N_GROUPS].set(router_g_w[l]).at[:, E_LANE0:E_LANE0 + N_EXPERTS].set(router_e_w[l])
    router_w = router_w.astype(BF16)
    router_b = jnp.zeros((1, ROUTER_LANES), F32)
    router_b = router_b.at[0, :N_GROUPS].set(router_g_b[l]).at[0, E_LANE0:E_LANE0 + N_EXPERTS].set(router_e_b[l])
    router_wt = jnp.zeros((ROUTER_ROWS, D_MODEL), F32)
    router_wt = router_wt.at[:N_GROUPS].set(router_g_w[l].T).at[E_ROW0:E_ROW0 + N_EXPERTS].set(router_e_w[l].T)
    router_wt = router_wt.astype(BF16)
    router_bt = jnp.zeros((ROUTER_ROWS,), F32)
    router_bt = router_bt.at[:N_GROUPS].set(router_g_b[l]).at[E_ROW0:E_ROW0 + N_EXPERTS].set(router_e_b[l])
    router_bt = jnp.broadcast_to(router_bt[:, None], (ROUTER_ROWS, LANES))
    earlier = jnp.triu(jnp.ones((TM, TM), BF16), k=1)
    wg_b = exp_w_gate[l].astype(BF16)
    wu_b = exp_w_up[l].astype(BF16)
    wd_b = exp_w_down[l].astype(BF16)
    row = lambda v: v.reshape(1, -1)

    n_ctx_tok = n_ctx * ctx_len

    def mixer(x, h0, cond_of_tile, cond_block, use_pos, counts0):
        n_seq, seq_len, _ = x.shape
        xf = x.reshape(n_seq * seq_len, D_MODEL)
        tab = pos_tab if use_pos else None
        xr, gg, y_sgu = _premix(x, mod3, cond_block, row(g_pre_mix[l]), w_in_b, row(sgu_g[l]),
                                sgu_w_b, sgu_bias_tile, tab)
        scan_params = (conv_w_t, conv_b_t, w_gates, b_gates, lam)
        hf, hf_last = _scan(xr, None, None, _to_time_major_state(h0[:, 0]), *scan_params, direction=0)
        y_rnn, hb_first = _scan(xr, gg, hf, _to_time_major_state(h0[:, 1]), *scan_params, direction=1)
        fstate = jnp.stack([hf_last.reshape(n_seq, D_RNN), hb_first.reshape(n_seq, D_RNN)], axis=1)
        x1, hn, bucket, rank, counts = _postmix(
            xf, y_rnn.reshape(n_seq * seq_len, D_RNN), y_sgu.reshape(n_seq * seq_len, D_SGU), mod3,
            cond_of_tile, row(g_post_mix[l]), row(g_pre_ffn[l]), w_out_b, router_wt, router_bt, earlier, tab,
            counts0)
        return x1, hn, bucket, rank, counts, fstate

    tiles_per_seq = dec_len // TM
    ctx_cond = lambda i: n_dec
    dec_cond = lambda i: i // tiles_per_seq
    h0_ctx = jnp.zeros((n_ctx, 2, D_RNN), F32)
    counts0 = jnp.zeros((ROUTER_ROWS, LANES), F32)
    x1_ctx, hn_ctx, bucket_ctx, rank_ctx, counts, st = mixer(x_prompt, h0_ctx, ctx_cond, lambda p: n_dec // PAIR,
                                                             False, counts0)
    new_state = st.astype(state_rglru.dtype)[:, None]
    x1_dec, hn_dec, bucket_dec, rank_dec, counts, _ = mixer(x_sample, state_rglru[:, l].astype(F32), dec_cond,
                                                            lambda p: p, True, counts)

    dest, sched, n_slots = _schedule(jnp.concatenate([bucket_ctx, bucket_dec]),
                                     jnp.concatenate([rank_ctx, rank_dec]), counts)
    xs = _dispatch(dest, hn_ctx, hn_dec, n_slots)
    ys = _experts(sched, xs, router_w, router_b, wg_b, wu_b, wd_b, row(g_post_ffn[l]))
    y_prompt = _combine(dest, ys, x1_ctx, mod3, ctx_cond, 0)
    y_sample = _combine(dest, ys, x1_dec, mod3, dec_cond, n_ctx_tok // TM)
    return (y_prompt.reshape(x_prompt.shape), y_sample.reshape(x_sample.shape), new_state)
```

```python
import functools
import math

import jax
import jax.numpy as jnp
from jax import lax
from jax.experimental import pallas as pl
from jax.experimental.pallas import tpu as pltpu

D_MODEL = 1024
D_RNN = 512
D_SGU = 512
N_HEADS_RNN = 8
HEAD_RNN = D_RNN // N_HEADS_RNN
N_HEADS_SGU = 8
HEAD_SGU = D_SGU // N_HEADS_SGU
CHUNK = 128
GRID_W = 64
RG_C = 8.0
N_GROUPS = 4
EXPERTS_PER_GROUP = 4
N_EXPERTS = N_GROUPS * EXPERTS_PER_GROUP
D_EXPERT = 512
EPS = 1e-6
POS_BASE = 10000.0

LANES = 128
SUBLANES = 8
CONV_W = 4
CONV_LEFT = 2
PAIR = 2
RNN_BLOCKS = D_RNN // LANES
TMJ_ROWS = PAIR * RNN_BLOCKS
ROUTER_LANES = LANES
E_LANE0 = N_GROUPS
ROUTER_ROWS = 32
E_ROW0 = SUBLANES

PAIRS_PER_GROUP = EXPERTS_PER_GROUP * (EXPERTS_PER_GROUP - 1) // 2
N_BUCKETS = N_GROUPS * PAIRS_PER_GROUP

ROW_TILES = D_MODEL // LANES

TM = 512
DMA_UNROLL = 8
EXPERT_AHEAD = 2
EXPERT_SLOTS = EXPERT_AHEAD + 1
EXPERT_ROW_PARTS = 2
TILE_ROW_PARTS = 1
TMX = 512
TT = TM // PAIR
LC = 256
TS = 16
PB = 2
VMEM_LIMIT = 56 * 1024 * 1024

F32 = jnp.float32
BF16 = jnp.bfloat16


def _params(sem):
    return pltpu.CompilerParams(dimension_semantics=sem, vmem_limit_bytes=VMEM_LIMIT)


def _rms(x):
    return x * lax.rsqrt(jnp.mean(x * x, axis=-1, keepdims=True) + EPS)


def _sigmoid(x):
    return 0.5 * jnp.tanh(0.5 * x) + 0.5


def _mod_kernel(cond_ref, w_ref, b_ref, o_ref):
    c = cond_ref[...]
    s = c * _sigmoid(c)
    o_ref[...] = jnp.dot(s.astype(BF16), w_ref[...].astype(BF16),
                         preferred_element_type=F32) + b_ref[...]


def _modulation(cond, w_mod, b_mod):
    n = w_mod.shape[1]
    bn = 1024
    return pl.pallas_call(
        _mod_kernel,
        out_shape=jax.ShapeDtypeStruct((cond.shape[0], n), F32),
        grid=(n // bn,),
        in_specs=[pl.BlockSpec(cond.shape, lambda j: (0, 0)),
                  pl.BlockSpec((D_MODEL, bn), lambda j: (0, j)),
                  pl.BlockSpec((1, bn), lambda j: (0, j))],
        out_specs=pl.BlockSpec((cond.shape[0], bn), lambda j: (0, j)),
        compiler_params=_params(("arbitrary",)),
        name="modulation",
    )(cond, w_mod, b_mod.reshape(1, n))


def _pos_kernel(o_ref):
    n_freq = D_MODEL // 4
    k = lax.broadcasted_iota(jnp.int32, (GRID_W, n_freq), 1).astype(F32)
    p = lax.broadcasted_iota(jnp.int32, (GRID_W, n_freq), 0).astype(F32)
    freq = jnp.exp(-math.log(POS_BASE) * k / n_freq)
    ang = p * freq
    o_ref[:, 0:n_freq] = jnp.sin(ang)
    o_ref[:, n_freq:2 * n_freq] = jnp.cos(ang)


def _pos_table():
    return pl.pallas_call(
        _pos_kernel,
        out_shape=jax.ShapeDtypeStruct((GRID_W, D_MODEL // 2), F32),
        name="pos_table",
    )()


def _add_pos(x, pos_refs, q0):
    if pos_refs is None:
        return x
    rows_ref, cols_ref = pos_refs
    reps = x.shape[0] // GRID_W
    rpart = jnp.concatenate(
        [jnp.broadcast_to(rows_ref[q:q + 1, :], (GRID_W, D_MODEL // 2)) for q in range(q0, q0 + reps)], axis=0)
    cpart = jnp.concatenate([cols_ref[...]] * reps, axis=0)
    return jnp.concatenate([x[:, :D_MODEL // 2] + rpart, x[:, D_MODEL // 2:] + cpart], axis=1)


def _load_x(x_ref, pos_refs, r0, n):
    return _add_pos(x_ref[r0:r0 + n, :], pos_refs, r0 // GRID_W)


def _premix_kernel(*refs, add_pos):
    refs = list(refs)
    x_ref = refs.pop(0)
    pos_refs = (refs.pop(0), refs.pop(0)) if add_pos else None
    mod_ref, g_ref, win_ref, sgug_ref, sguw_ref, sgub_ref, xr_ref, gg_ref, ys_ref = refs
    hn = []
    for s in range(PAIR):
        shift = mod_ref[s, :, 0:D_MODEL]
        scale = mod_ref[s, :, D_MODEL:2 * D_MODEL]
        hn.append(_rms(_add_pos(x_ref[s], pos_refs, 0)) * (g_ref[...] * (1.0 + scale)) + shift)
    z = jnp.dot(jnp.concatenate(hn, axis=0).astype(BF16), win_ref[...],
                preferred_element_type=F32)
    half = D_SGU // 2
    heads_per_half = N_HEADS_SGU // 2
    lane_head = lax.broadcasted_iota(jnp.int32, (CHUNK, half), 1) // HEAD_SGU
    for s in range(PAIR):
        zs = z[s * TT:(s + 1) * TT]
        gg = jax.nn.gelu(zs[:, D_RNN:2 * D_RNN])
        for k in range(RNN_BLOCKS):
            rows = pl.ds(s * RNN_BLOCKS + k, TT, stride=TMJ_ROWS)
            xr_ref[rows, :] = zs[:, k * LANES:(k + 1) * LANES]
            gg_ref[rows, :] = gg[:, k * LANES:(k + 1) * LANES]
        u = zs[:, 2 * D_RNN:2 * D_RNN + D_SGU]
        vn = (_rms(zs[:, 2 * D_RNN + D_SGU:]) * sgug_ref[...]).astype(BF16)
        for c in range(TT // CHUNK):
            rows = slice(c * CHUNK, (c + 1) * CHUNK)
            halves = []
            for hf in range(2):
                r = jnp.dot(sguw_ref[hf], vn[rows, hf * half:(hf + 1) * half],
                            preferred_element_type=F32)
                sel = jnp.zeros((CHUNK, half), F32)
                for h in range(heads_per_half):
                    sel = jnp.where(lane_head == h, r[h * CHUNK:(h + 1) * CHUNK], sel)
                halves.append(sel)
            gatev = jnp.concatenate(halves, axis=1) + sgub_ref[...]
            ys_ref[s, rows, :] = (u[rows] * gatev).astype(BF16)


def _premix(x, mod3, cond_block, g_pre, w_in_b, sgu_g, sgu_w_b, sgu_bias_tile, pos_tab):
    n_seq, seq_len, _ = x.shape
    n_pairs, n_tiles = n_seq // PAIR, seq_len // TT
    add_pos = pos_tab is not None
    const2 = lambda p, j: (0, 0)
    in_specs = [pl.BlockSpec((PAIR, TT, D_MODEL), lambda p, j: (p, j, 0))]
    args = [x]
    if add_pos:
        reps = TT // GRID_W
        in_specs += [pl.BlockSpec((None, reps, D_MODEL // 2), lambda p, j: (j, 0, 0)),
                     pl.BlockSpec((GRID_W, D_MODEL // 2), const2)]
        args += [pos_tab.reshape(GRID_W // reps, reps, D_MODEL // 2), pos_tab]
    in_specs += [pl.BlockSpec((PAIR, 1, 6 * D_MODEL), lambda p, j: (cond_block(p), 0, 0)),
                 pl.BlockSpec((1, D_MODEL), const2),
                 pl.BlockSpec((D_MODEL, 2 * D_RNN + 2 * D_SGU), const2),
                 pl.BlockSpec((1, D_SGU), const2),
                 pl.BlockSpec((2, 4 * CHUNK, CHUNK), lambda p, j: (0, 0, 0)),
                 pl.BlockSpec((CHUNK, D_SGU), const2)]
    args += [mod3, g_pre, w_in_b, sgu_g, sgu_w_b, sgu_bias_tile]
    tmj = jax.ShapeDtypeStruct((n_pairs * seq_len * TMJ_ROWS, LANES), F32)
    tmj_spec = pl.BlockSpec((TT * TMJ_ROWS, LANES), lambda p, j: (p * n_tiles + j, 0))
    xr, gg, y_sgu = pl.pallas_call(
        functools.partial(_premix_kernel, add_pos=add_pos),
        out_shape=(tmj, tmj, jax.ShapeDtypeStruct((n_seq, seq_len, D_SGU), BF16)),
        grid=(n_pairs, n_tiles),
        in_specs=in_specs,
        out_specs=(tmj_spec, tmj_spec, pl.BlockSpec((PAIR, TT, D_SGU), lambda p, j: (p, j, 0))),
        compiler_params=_params(("parallel", "parallel")),
        name="premix",
    )(*args)
    shape4 = (n_pairs, seq_len, TMJ_ROWS, LANES)
    return xr.reshape(shape4), gg.reshape(shape4), y_sgu


def _scan_kernel(*refs, reverse, n_chunks):
    if reverse:
        (xprev_ref, x_ref, xnext_ref, gg_ref, hf_ref, h0_ref, cw_ref, cb_ref, wg_ref, bg_ref, lam_ref,
         y_ref, fs_ref, xwin, xc_s, r_s, i_s, a_s, b_s, y_s, hcar) = refs
    else:
        (xprev_ref, x_ref, xnext_ref, h0_ref, cw_ref, cb_ref, wg_ref, bg_ref, lam_ref,
         hf_ref, fs_ref, xwin, xc_s, r_s, i_s, a_s, b_s, hcar) = refs
    c = pl.program_id(1)
    chunk = n_chunks - 1 - c if reverse else c
    sub_rows = TS * TMJ_ROWS

    def rows_of(pb, t0, n_steps):
        return pl.ds(pl.multiple_of((pb * LC + t0) * TMJ_ROWS, TMJ_ROWS), n_steps * TMJ_ROWS)

    @pl.when(c == 0)
    def _():
        hcar[...] = h0_ref[...]

    xwin[:, 0:CONV_LEFT] = jnp.where(chunk > 0, xprev_ref[...], 0.0)
    xwin[:, LC + CONV_LEFT:LC + CONV_W - 1] = jnp.where(chunk < n_chunks - 1, xnext_ref[...], 0.0)

    def copy(i, carry):
        t0 = pl.multiple_of(i * TS, TS)
        xwin[:, pl.ds(t0 + CONV_LEFT, TS)] = x_ref[:, pl.ds(t0, TS)]
        return carry

    lax.fori_loop(0, LC // TS, copy, 0)

    def conv(i, carry):
        t0 = pl.multiple_of(i * TS, TS)
        xc = cb_ref[...] + cw_ref[0] * xwin[:, pl.ds(t0, TS)]
        for k in range(1, CONV_W):
            xc = xc + cw_ref[k] * xwin[:, pl.ds(t0 + k, TS)]
        for pb in range(PB):
            xc_s[rows_of(pb, t0, TS), :] = xc[pb].reshape(sub_rows, LANES)
        return carry

    lax.fori_loop(0, LC // TS, conv, 0)

    n_rows = PB * LC * PAIR
    for k in range(RNN_BLOCKS):
        rows = pl.ds(k, n_rows, stride=RNN_BLOCKS)
        g = jnp.dot(xc_s[rows, :].astype(BF16), wg_ref[k], preferred_element_type=F32)
        r_s[rows, :] = g[:, :LANES]
        i_s[rows, :] = g[:, LANES:]

    neg_lam = -lam_ref[...]
    softplus = jnp.maximum(neg_lam, 0.0) + jnp.log(1.0 + jnp.exp(-jnp.abs(neg_lam)))
    half_decay = (-0.5 * RG_C * math.log2(math.e)) * softplus

    def gates(i, carry):
        t0 = pl.multiple_of(i * TS, TS)
        for pb in range(PB):
            rows = rows_of(pb, t0, TS)
            tile = lambda ref: ref[rows, :].reshape(TS, TMJ_ROWS, LANES)
            tr = jnp.tanh(tile(r_s) + bg_ref[0])
            ti = jnp.tanh(tile(i_s) + bg_ref[1])
            log2_a = tr * half_decay + half_decay
            a = jnp.exp2(log2_a)
            q = jnp.tanh(log2_a * (-math.log(2.0))) * (a * a + 1.0)
            b = jnp.where(q > 0.0, q * lax.rsqrt(q), 0.0) * ((ti + 1.0) * tile(xc_s))
            a_s[rows, :] = a.reshape(sub_rows, LANES)
            b_s[rows, :] = b.reshape(sub_rows, LANES)
        return carry

    lax.fori_loop(0, LC // TS, gates, 0)

    def step(j, hs):
        t = LC - 1 - j if reverse else j
        out = []
        for pb in range(PB):
            rows = rows_of(pb, t, 1)
            h = a_s[rows, :] * hs[pb] + b_s[rows, :]
            if reverse:
                y_s[rows, :] = (hf_ref[pb, t] + h) * gg_ref[pb, t]
            else:
                hf_ref[pb, t] = h
            out.append(h)
        return tuple(out)

    hs = lax.fori_loop(0, LC, step, tuple(hcar[pb] for pb in range(PB)), unroll=8)
    for pb in range(PB):
        hcar[pb] = hs[pb]
        fs_ref[pb] = hs[pb]

    if reverse:
        for pb in range(PB):
            for s in range(PAIR):
                cols = [y_s[pl.ds(pb * LC * TMJ_ROWS + s * RNN_BLOCKS + k, LC, stride=TMJ_ROWS), :]
                        for k in range(RNN_BLOCKS)]
                y_ref[pb * PAIR + s] = jnp.concatenate(cols, axis=1).astype(BF16)


def _scan(xr, gg, hf, h0, conv_w, conv_b, w_gates, b_gates, lam, direction):
    n_pairs, seq_len = xr.shape[:2]
    n_chunks = seq_len // LC
    reverse = direction == 1
    pos = (lambda c: n_chunks - 1 - c) if reverse else (lambda c: c)
    tmj_blk = pl.BlockSpec((PB, LC, TMJ_ROWS, LANES), lambda i, c: (i, pos(c), 0, 0))
    state_blk = pl.BlockSpec((PB, TMJ_ROWS, LANES), lambda i, c: (i, 0, 0))
    per_dir = lambda *shape: pl.BlockSpec((None,) + shape, lambda i, c: (direction,) + (0,) * len(shape))
    in_specs = [
        pl.BlockSpec((PB, CONV_LEFT, TMJ_ROWS, LANES),
                     lambda i, c: (i, jnp.maximum(pos(c) * (LC // CONV_LEFT) - 1, 0), 0, 0)),
        tmj_blk,
        pl.BlockSpec((PB, 1, TMJ_ROWS, LANES), lambda i, c: (i, jnp.minimum((pos(c) + 1) * LC, seq_len - 1), 0, 0)),
    ]
    args = [xr, xr, xr]
    if reverse:
        in_specs += [tmj_blk, tmj_blk]
        args += [gg, hf]
    in_specs += [state_blk,
                 pl.BlockSpec((CONV_W, TMJ_ROWS, LANES), lambda i, c: (0, 0, 0)),
                 pl.BlockSpec((TMJ_ROWS, LANES), lambda i, c: (0, 0)),
                 per_dir(RNN_BLOCKS, LANES, 2 * LANES),
                 per_dir(2, TMJ_ROWS, LANES),
                 per_dir(TMJ_ROWS, LANES)]
    args += [h0, conv_w, conv_b, w_gates, b_gates, lam]
    flat = pltpu.VMEM((PB * LC * TMJ_ROWS, LANES), F32)
    scratch = [pltpu.VMEM((PB, LC + CONV_W - 1, TMJ_ROWS, LANES), F32)] + [flat] * (6 if reverse else 5)
    scratch += [pltpu.VMEM((PB, TMJ_ROWS, LANES), F32)]
    state = jax.ShapeDtypeStruct((n_pairs, TMJ_ROWS, LANES), F32)
    if reverse:
        out_shape = (jax.ShapeDtypeStruct((n_pairs * PAIR, seq_len, D_RNN), BF16), state)
        out_specs = (pl.BlockSpec((PB * PAIR, LC, D_RNN), lambda i, c: (i, pos(c), 0)), state_blk)
    else:
        out_shape = (jax.ShapeDtypeStruct(xr.shape, F32), state)
        out_specs = (tmj_blk, state_blk)
    return pl.pallas_call(
        functools.partial(_scan_kernel, reverse=reverse, n_chunks=n_chunks),
        out_shape=out_shape,
        grid=(n_pairs // PB, n_chunks),
        in_specs=in_specs,
        out_specs=out_specs,
        scratch_shapes=scratch,
        compiler_params=_params(("parallel", "arbitrary")),
        name="scan_bwd" if reverse else "scan_fwd",
    )(*args)


def _route(lt):
    n = lt.shape[1]
    row = lax.broadcasted_iota(jnp.int32, (EXPERTS_PER_GROUP, n), 0)
    neg = jnp.float32(-jnp.inf)

    def arg_max(v):
        m = jnp.max(v, axis=0, keepdims=True)
        return jnp.min(jnp.where(v == m, row, EXPERTS_PER_GROUP), axis=0, keepdims=True)

    g_idx = arg_max(lt[0:N_GROUPS])
    el = lt[E_ROW0:E_ROW0 + EXPERTS_PER_GROUP]
    for g in range(1, N_GROUPS):
        first = E_ROW0 + g * EXPERTS_PER_GROUP
        el = jnp.where(g_idx == g, lt[first:first + EXPERTS_PER_GROUP], el)
    i1 = arg_max(el)
    i2 = arg_max(jnp.where(row == i1, neg, el))
    ja = jnp.minimum(i1, i2)
    jb = jnp.maximum(i1, i2)
    pair = (ja * (2 * EXPERTS_PER_GROUP - 1 - ja)) // 2 + (jb - ja - 1)
    return g_idx * PAIRS_PER_GROUP + pair


def _store_token_major(ref, x, t0=0):
    n = x.shape[0]
    for k in range(ROW_TILES):
        ref[pl.ds(t0 * ROW_TILES + k, n, stride=ROW_TILES), :] = x[:, k * LANES:(k + 1) * LANES]


def _load_token_major(ref, n):
    return jnp.concatenate([ref[pl.ds(k, n, stride=ROW_TILES), :] for k in range(ROW_TILES)], axis=1)


def _postmix_kernel(*refs, add_pos, n_aliased):
    refs = list(refs)
    x_ref = refs.pop(0)
    pos_refs = (refs.pop(0), refs.pop(0)) if add_pos else None
    (yr_ref, ys_ref, mod_ref, gpost_ref, gpre_ref, wout_ref, rw_ref, rb_ref, earlier_ref, cnt0_ref) = refs[:10]
    x1_ref, hn_ref, rt_ref, cnt_ref, run_ref = refs[10 + n_aliased:]

    @pl.when(pl.program_id(0) == 0)
    def _():
        run_ref[...] = cnt0_ref[...]

    gate1 = mod_ref[0, :, 2 * D_MODEL:3 * D_MODEL]
    shift2 = mod_ref[0, :, 3 * D_MODEL:4 * D_MODEL]
    scale2 = mod_ref[0, :, 4 * D_MODEL:5 * D_MODEL]
    y = (jnp.dot(yr_ref[...], wout_ref[0:D_RNN, :], preferred_element_type=F32)
         + jnp.dot(ys_ref[...], wout_ref[D_RNN:, :], preferred_element_type=F32))
    x1 = _load_x(x_ref, pos_refs, 0, TM) + _rms(y) * (gate1 * gpost_ref[...])
    x1_ref[...] = x1
    hn = _rms(x1) * (gpre_ref[...] * (1.0 + scale2)) + shift2
    _store_token_major(hn_ref, hn)
    lt = lax.dot_general(rw_ref[...], hn.astype(BF16), (((1,), (1,)), ((), ())),
                         preferred_element_type=F32) + rb_ref[:, 0:1]
    bucket = _route(lt)
    onehot = lax.broadcasted_iota(jnp.int32, (ROUTER_ROWS, TM), 0) == bucket
    before = jnp.dot(onehot.astype(BF16), earlier_ref[...], preferred_element_type=F32) + run_ref[:, 0:1]
    rank = jnp.sum(jnp.where(onehot, before, 0.0), axis=0, keepdims=True).astype(jnp.int32)
    row = lax.broadcasted_iota(jnp.int32, (SUBLANES, TM), 0)
    rt_ref[...] = jnp.where(row == 0, bucket, jnp.where(row == 1, rank, 0))
    run_ref[...] += jnp.sum(onehot.astype(F32), axis=1, keepdims=True)
    cnt_ref[...] = run_ref[...]


def _postmix(x, y_rnn, y_sgu, mod3, cond_of_tile, g_post, g_pre, w_out_b, router_wt, router_bt, earlier, pos_tab,
             counts0, hn_all, tile0, n_all):
    n_tok = x.shape[0]
    add_pos = pos_tab is not None
    tok = lambda i: (i, 0)
    const2 = lambda i: (0, 0)
    in_specs = [pl.BlockSpec((TM, D_MODEL), tok)]
    args = [x]
    if add_pos:
        reps = TM // GRID_W
        tiles_per_seq = GRID_W // reps
        in_specs += [pl.BlockSpec((None, reps, D_MODEL // 2), lambda i: (i % tiles_per_seq, 0, 0)),
                     pl.BlockSpec((GRID_W, D_MODEL // 2), const2)]
        args += [pos_tab.reshape(tiles_per_seq, reps, D_MODEL // 2), pos_tab]
    in_specs += [pl.BlockSpec((TM, D_RNN), tok),
                 pl.BlockSpec((TM, D_SGU), tok),
                 pl.BlockSpec((1, 1, 6 * D_MODEL), lambda i: (cond_of_tile(i), 0, 0)),
                 pl.BlockSpec((1, D_MODEL), const2),
                 pl.BlockSpec((1, D_MODEL), const2),
                 pl.BlockSpec((D_MODEL, D_MODEL), const2),
                 pl.BlockSpec((ROUTER_ROWS, D_MODEL), const2),
                 pl.BlockSpec((ROUTER_ROWS, LANES), const2),
                 pl.BlockSpec((TM, TM), const2),
                 pl.BlockSpec((ROUTER_ROWS, LANES), const2)]
    args += [y_rnn, y_sgu, mod3, g_post, g_pre, w_out_b, router_wt, router_bt, earlier, counts0]
    aliases = {}
    if hn_all is not None:
        aliases = {len(args): 1}
        in_specs += [pl.BlockSpec(memory_space=pl.ANY)]
        args += [hn_all]
    n_tiles = n_tok // TM
    counts_spec = pl.BlockSpec((ROUTER_ROWS, LANES), const2)
    x1, hn, route, counts = pl.pallas_call(
        functools.partial(_postmix_kernel, add_pos=add_pos, n_aliased=len(aliases)),
        out_shape=(jax.ShapeDtypeStruct((n_tok, D_MODEL), F32),
                   jax.ShapeDtypeStruct((n_all * ROW_TILES, LANES), F32),
                   jax.ShapeDtypeStruct((n_tiles * SUBLANES, TM), jnp.int32),
                   jax.ShapeDtypeStruct((ROUTER_ROWS, LANES), F32)),
        grid=(n_tiles,),
        in_specs=in_specs,
        out_specs=(pl.BlockSpec((TM, D_MODEL), tok),
                   pl.BlockSpec((TM * ROW_TILES, LANES), lambda i: (i + tile0, 0)),
                   pl.BlockSpec((SUBLANES, TM), tok),
                   counts_spec),
        scratch_shapes=[pltpu.VMEM((ROUTER_ROWS, LANES), F32)],
        input_output_aliases=aliases,
        compiler_params=_params(("arbitrary",)),
        name="postmix",
    )(*args)
    route = route.reshape(n_tiles, SUBLANES, TM)
    return x1, hn, route[:, 0].reshape(n_tok), route[:, 1].reshape(n_tok), counts


def _token_rows(ref, t):
    return ref.at[pl.ds(pl.multiple_of(t * ROW_TILES, ROW_TILES), ROW_TILES), :]


def _invert_kernel(dest_ref, spare_ref, src_ref, sem, *, n_tok):
    copy = pltpu.make_async_copy(spare_ref, src_ref, sem)
    copy.start()
    copy.wait()

    def place(t, carry):
        src_ref[dest_ref[t]] = t
        return carry

    lax.fori_loop(0, n_tok, place, 0, unroll=16)


def _invert(dest, n_slots):
    n_tok = dest.shape[0]
    spare = n_tok + jnp.arange(n_slots, dtype=jnp.int32) % TMX
    return pl.pallas_call(
        functools.partial(_invert_kernel, n_tok=n_tok),
        out_shape=jax.ShapeDtypeStruct((n_slots,), jnp.int32),
        grid_spec=pltpu.PrefetchScalarGridSpec(
            num_scalar_prefetch=1,
            grid=(1,),
            in_specs=[pl.BlockSpec(memory_space=pl.ANY)],
            out_specs=pl.BlockSpec(memory_space=pltpu.SMEM),
            scratch_shapes=[pltpu.SemaphoreType.DMA(())]),
        compiler_params=_params(("arbitrary",)),
        name="invert",
    )(dest, spare)


def _experts_kernel(ea_ref, eb_ref, nv_ref, src_ref, total_ref, hn_ref, rw_ref, rb_ref,
                    wga_ref, wua_ref, wda_ref, wgb_ref, wub_ref, wdb_ref, gpost_ref, yt_ref,
                    xbuf, ybuf, gsem, ssem, *, n_tok):
    i = pl.program_id(0)
    total = total_ref[0]
    dummy_tile = pl.num_programs(0) - 1
    tile_rows = TMX * ROW_TILES

    def gather_row(tile, slot, r, queue):
        tok = jnp.minimum(src_ref[tile * TMX + r], n_tok - 1)
        pltpu.make_async_copy(_token_rows(hn_ref, tok), _token_rows(xbuf.at[slot], r),
                              gsem.at[slot]).start(priority=queue)

    def scatter_row(tile, slot, r, queue):
        pltpu.make_async_copy(_token_rows(ybuf.at[slot], r), _token_rows(yt_ref, src_ref[tile * TMX + r]),
                              ssem.at[slot]).start(priority=queue)

    def looped(row_fn, tile, slot):
        def start(g, carry):
            for u in range(DMA_UNROLL):
                row_fn(tile, slot, g * DMA_UNROLL + u, u % 2)
            return carry

        lax.fori_loop(0, TMX // DMA_UNROLL, start, 0)

    def wait_gather(slot):
        pltpu.make_async_copy(hn_ref.at[pl.ds(0, tile_rows), :], xbuf.at[slot], gsem.at[slot]).wait()

    def wait_scatter(slot):
        pltpu.make_async_copy(ybuf.at[slot], yt_ref.at[pl.ds(0, tile_rows), :], ssem.at[slot]).wait()

    @pl.when(i == 0)
    def _():
        ybuf[...] = jnp.zeros_like(ybuf)
        for t in range(EXPERT_AHEAD):
            looped(gather_row, jnp.minimum(t, total - 1), t)
        looped(scatter_row, dummy_tile, 0)

    @pl.when(i == total)
    def _():
        last = total - 1
        wait_scatter(total % 2)
        looped(scatter_row, last, last % 2)
        wait_scatter(last % 2)
        for t in range(EXPERT_AHEAD):
            wait_gather((total + t) % EXPERT_SLOTS)

    @pl.when(i < total)
    def _():
        nv = nv_ref[i]
        slot = i % EXPERT_SLOTS
        wait_gather(slot)
        wait_scatter(i % 2)
        row = lax.broadcasted_iota(jnp.int32, (TMX, 1), 0)
        xb = jnp.where(row < nv, _load_token_major(xbuf.at[slot], TMX), 0.0).astype(BF16)
        ahead = jnp.minimum(i + EXPERT_AHEAD, total - 1)
        behind = jnp.where(i == 0, dummy_tile, i - 1)

        for r in range(TMX):
            gather_row(ahead, (i + EXPERT_AHEAD) % EXPERT_SLOTS, r, r % 2)
            scatter_row(behind, (i + 1) % 2, r, (r + 1) % 2)
        logits = jnp.dot(xb, rw_ref[...], preferred_element_type=F32) + rb_ref[...]
        lane = lax.broadcasted_iota(jnp.int32, logits.shape, 1)
        ea = ea_ref[i]
        eb = eb_ref[i]
        gmask = lane < N_GROUPS
        gl = jnp.where(gmask, logits, -jnp.inf)
        gmax = jnp.max(gl, axis=-1, keepdims=True)
        gexp = jnp.where(gmask, jnp.exp(gl - gmax), 0.0)
        g_own = jnp.sum(jnp.where(lane == ea // EXPERTS_PER_GROUP, gexp, 0.0), axis=-1, keepdims=True)
        g_w = g_own / jnp.sum(gexp, axis=-1, keepdims=True)
        la = jnp.sum(jnp.where(lane == ea + E_LANE0, logits, 0.0), axis=-1, keepdims=True)
        lb = jnp.sum(jnp.where(lane == eb + E_LANE0, logits, 0.0), axis=-1, keepdims=True)
        m = jnp.maximum(la, lb)
        pa = jnp.exp(la - m)
        pb = jnp.exp(lb - m)
        inv = g_w / (pa + pb)

        def hidden(x, wg_ref, wu_ref, w):
            g = jnp.dot(x, wg_ref[0], preferred_element_type=F32)
            u = jnp.dot(x, wu_ref[0], preferred_element_type=F32)
            return ((g * _sigmoid(g)) * u * w).astype(BF16)

        part = TMX // EXPERT_ROW_PARTS
        ys = []
        for h in range(EXPERT_ROW_PARTS):
            rows = slice(h * part, (h + 1) * part)
            act_a = hidden(xb[rows], wga_ref, wua_ref, (pa * inv)[rows])
            act_b = hidden(xb[rows], wgb_ref, wub_ref, (pb * inv)[rows])
            y =(jnp.dot(act_a, wda_ref[0], preferred_element_type=F32)
                 + jnp.dot(act_b, wdb_ref[0], preferred_element_type=F32))
            ys.append(_rms(y) * gpost_ref[...])
        _store_token_major(ybuf.at[i % 2], jnp.concatenate(ys, axis=0))


def _experts(sched, src, total, hn, router_w, router_b, wg_b, wu_b, wd_b, g_post):
    ea, eb, nv = sched
    n_steps = ea.shape[0]
    n_tok = hn.shape[0] // ROW_TILES
    const2 = lambda i, *_: (0, 0)
    exp_a = lambda i, ea, eb, nv, src, total: (ea[i], 0, 0)
    exp_b = lambda i, ea, eb, nv, src, total: (eb[i], 0, 0)
    w_in_spec = lambda m: pl.BlockSpec((1, D_MODEL, D_EXPERT), m)
    w_out_spec = lambda m: pl.BlockSpec((1, D_EXPERT, D_MODEL), m)
    tile = (TMX * ROW_TILES, LANES)
    return pl.pallas_call(
        functools.partial(_experts_kernel, n_tok=n_tok),
        out_shape=jax.ShapeDtypeStruct(((n_tok + TMX) * ROW_TILES, LANES), F32),
        grid_spec=pltpu.PrefetchScalarGridSpec(
            num_scalar_prefetch=5,
            grid=(n_steps,),
            in_specs=[pl.BlockSpec(memory_space=pl.ANY),
                      pl.BlockSpec((D_MODEL, ROUTER_LANES), const2),
                      pl.BlockSpec((1, ROUTER_LANES), const2),
                      w_in_spec(exp_a), w_in_spec(exp_a), w_out_spec(exp_a),
                      w_in_spec(exp_b), w_in_spec(exp_b), w_out_spec(exp_b),
                      pl.BlockSpec((1, D_MODEL), const2)],
            out_specs=pl.BlockSpec(memory_space=pl.ANY),
            scratch_shapes=[pltpu.VMEM((EXPERT_SLOTS,) + tile, F32), pltpu.VMEM((2,) + tile, F32),
                            pltpu.SemaphoreType.DMA((EXPERT_SLOTS,)), pltpu.SemaphoreType.DMA((2,))]),
        compiler_params=_params(("arbitrary",)),
        name="experts",
    )(ea, eb, nv, src, total, hn, router_w, router_b, wg_b, wu_b, wd_b, wg_b, wu_b, wd_b, g_post)


def _combine_kernel(yt_ref, x1_ref, mod_ref, o_ref):
    gate2 = mod_ref[0, :, 5 * D_MODEL:6 * D_MODEL]
    o_ref[...] = x1_ref[...] + gate2 * _load_token_major(yt_ref, TM)


def _combine(yt, x1, mod3, cond_of_tile, tile0):
    n_tok = x1.shape[0]
    return pl.pallas_call(
        _combine_kernel,
        out_shape=jax.ShapeDtypeStruct((n_tok, D_MODEL), F32),
        grid=(n_tok // TM,),
        in_specs=[pl.BlockSpec((TM * ROW_TILES, LANES), lambda i: (i + tile0, 0)),
                  pl.BlockSpec((TM, D_MODEL), lambda i: (i, 0)),
                  pl.BlockSpec((1, 1, 6 * D_MODEL), lambda i: (cond_of_tile(i), 0, 0))],
        out_specs=pl.BlockSpec((TM, D_MODEL), lambda i: (i, 0)),
        compiler_params=_params(("parallel",)),
        name="combine",
    )(yt, x1, mod3)


def _schedule(bucket, rank, counts):
    n_tok = bucket.shape[0]
    n_max = n_tok // TMX + N_BUCKETS
    cnt = counts[:N_BUCKETS, 0].astype(jnp.int32)
    tiles = (cnt + TMX - 1) // TMX
    tile_end = jnp.cumsum(tiles)
    tile_start = tile_end - tiles
    ids = jnp.arange(N_BUCKETS, dtype=jnp.int32)
    slot0 = jnp.sum(jnp.where(bucket[:, None] == ids[None, :], (tile_start * TMX)[None, :], 0), axis=1)
    dest = slot0 + rank
    n_steps = n_max + 1
    i = jnp.arange(n_steps, dtype=jnp.int32)
    total = tile_end[-1]
    valid = i < total
    tb = jnp.sum((jnp.minimum(i, total - 1)[:, None] >= tile_end[None, :]).astype(jnp.int32), axis=1)
    pairs = [(a, b) for a in range(EXPERTS_PER_GROUP) for b in range(a + 1, EXPERTS_PER_GROUP)]
    ea_tab = jnp.array([g * EXPERTS_PER_GROUP + a for g in range(N_GROUPS) for a, _ in pairs], jnp.int32)
    eb_tab = jnp.array([g * EXPERTS_PER_GROUP + b for g in range(N_GROUPS) for _, b in pairs], jnp.int32)
    ea = ea_tab[tb]
    eb = eb_tab[tb]
    nv = jnp.where(valid, jnp.clip(cnt[tb] - (i - tile_start[tb]) * TMX, 0, TMX), 0)
    return dest, (ea, eb, nv), total[None], n_steps * TMX


def _block_diag_gates(rg_wa, rg_wx):
    heads = LANES // HEAD_RNN

    def bd(w):
        w = w.reshape(2, RNN_BLOCKS, heads, HEAD_RNN, HEAD_RNN)
        eye = jnp.eye(heads, dtype=w.dtype)
        full = jnp.einsum('dghij,hk->dghikj', w, eye)
        return full.reshape(2, RNN_BLOCKS, LANES, LANES)

    return jnp.concatenate([bd(rg_wa), bd(rg_wx)], axis=-1).astype(BF16)


def _row_tile(v):
    blocks = v.reshape(v.shape[:-1] + (RNN_BLOCKS, LANES))
    return jnp.concatenate([blocks] * PAIR, axis=-2)


def _to_time_major_state(h):
    return h.reshape(h.shape[0] // PAIR, TMJ_ROWS, LANES)


def kernel(x_prompt, x_sample, state_rglru, c, c_ctx, w_mod, b_mod, g_pre_mix, g_post_mix, g_pre_ffn,
           g_post_ffn, w_in, conv_w, conv_b, rg_wa, rg_ba, rg_wx, rg_bx, rg_lambda, sgu_g, sgu_w, sgu_b,
           w_out, router_g_w, router_g_b, router_e_w, router_e_b, exp_w_gate, exp_w_up, exp_w_down):
    assert w_mod.shape[0] == 1, "single-layer trunk"
    n_ctx, ctx_len, _ = x_prompt.shape
    n_dec, dec_len, _ = x_sample.shape
    l = 0

    n_cond = SUBLANES
    assert n_dec % PAIR == 0 and n_dec + PAIR <= n_cond
    cond = jnp.zeros((n_cond, D_MODEL), F32).at[:n_dec].set(c).at[n_dec:n_dec + PAIR].set(c_ctx)
    mod3 = _modulation(cond, w_mod[l], b_mod[l]).reshape(n_cond, 1, 6 * D_MODEL)
    pos_tab = _pos_table()

    w_in_b = w_in[l].astype(BF16)
    w_out_b = w_out[l].astype(BF16)
    sgu_w_b = sgu_w[l].reshape(2, 4 * CHUNK, CHUNK).astype(BF16)
    sgu_bias_tile = jnp.repeat(sgu_b[l].T, HEAD_SGU, axis=1)
    w_gates = _block_diag_gates(rg_wa[l], rg_wx[l])
    b_gates = 0.5 * jnp.stack([_row_tile(rg_ba[l]), _row_tile(rg_bx[l])], axis=1)
    lam = _row_tile(rg_lambda[l])
    conv_w_t = 0.5 * _row_tile(conv_w[l])
    conv_b_t = 0.5 * _row_tile(conv_b[l])
    router_w = jnp.zeros((D_MODEL, ROUTER_LANES), F32)
    router_w = router_w.at[:, :N_GROUPS].set(router_g_w[l]).at[:, E_LANE0:E_LANE0 + N_EXPERTS].set(router_e_w[l])
    router_w = router_w.astype(BF16)
    router_b = jnp.zeros((1, ROUTER_LANES), F32)
    router_b = router_b.at[0, :N_GROUPS].set(router_g_b[l]).at[0, E_LANE0:E_LANE0 + N_EXPERTS].set(router_e_b[l])
    router_wt = jnp.zeros((ROUTER_ROWS, D_MODEL), F32)
    router_wt = router_wt.at[:N_GROUPS].set(router_g_w[l].T).at[E_ROW0:E_ROW0 + N_EXPERTS].set(router_e_w[l].T)
    router_wt = router_wt.astype(BF16)
    router_bt = jnp.zeros((ROUTER_ROWS,), F32)
    router_bt = router_bt.at[:N_GROUPS].set(router_g_b[l]).at[E_ROW0:E_ROW0 + N_EXPERTS].set(router_e_b[l])
    router_bt = jnp.broadcast_to(router_bt[:, None], (ROUTER_ROWS, LANES))
    earlier = jnp.triu(jnp.ones((TM, TM), BF16), k=1)
    wg_b = exp_w_gate[l].astype(BF16)
    wu_b = exp_w_up[l].astype(BF16)
    wd_b = exp_w_down[l].astype(BF16)
    row = lambda v: v.reshape(1, -1)

    n_ctx_tok = n_ctx * ctx_len

    n_all = n_ctx_tok + n_dec * dec_len

    def mixer(x, h0, cond_of_tile, cond_block, use_pos, counts0, hn_all, tile0):
        n_seq, seq_len, _ = x.shape
        xf = x.reshape(n_seq * seq_len, D_MODEL)
        tab = pos_tab if use_pos else None
        xr, gg, y_sgu = _premix(x, mod3, cond_block, row(g_pre_mix[l]), w_in_b, row(sgu_g[l]),
                                sgu_w_b, sgu_bias_tile, tab)
        scan_params = (conv_w_t, conv_b_t, w_gates, b_gates, lam)
        hf, hf_last = _scan(xr, None, None, _to_time_major_state(h0[:, 0]), *scan_params, direction=0)
        y_rnn, hb_first = _scan(xr, gg, hf, _to_time_major_state(h0[:, 1]), *scan_params, direction=1)
        fstate = jnp.stack([hf_last.reshape(n_seq, D_RNN), hb_first.reshape(n_seq, D_RNN)], axis=1)
        x1, hn, bucket, rank, counts = _postmix(
            xf, y_rnn.reshape(n_seq * seq_len, D_RNN), y_sgu.reshape(n_seq * seq_len, D_SGU), mod3,
            cond_of_tile, row(g_post_mix[l]), row(g_pre_ffn[l]), w_out_b, router_wt, router_bt, earlier, tab,
            counts0, hn_all, tile0, n_all)
        return x1, hn, bucket, rank, counts, fstate

    tiles_per_seq = dec_len // TM
    ctx_cond = lambda i: n_dec
    dec_cond = lambda i: i // tiles_per_seq
    h0_ctx = jnp.zeros((n_ctx, 2, D_RNN), F32)
    counts0 = jnp.zeros((ROUTER_ROWS, LANES), F32)
    x1_ctx, hn_all, bucket_ctx, rank_ctx, counts, st = mixer(x_prompt, h0_ctx, ctx_cond, lambda p: n_dec // PAIR,
                                                             False, counts0, None, 0)
    new_state = st.astype(state_rglru.dtype)[:, None]
    x1_dec, hn_all, bucket_dec, rank_dec, counts, _ = mixer(x_sample, state_rglru[:, l].astype(F32), dec_cond,
                                                            lambda p: p, True, counts, hn_all, n_ctx_tok // TM)

    dest, sched, total, n_slots = _schedule(jnp.concatenate([bucket_ctx, bucket_dec]),
                                            jnp.concatenate([rank_ctx, rank_dec]), counts)
    src = _invert(dest, n_slots)
    yt = _experts(sched, src, total, hn_all, router_w, router_b, wg_b, wu_b, wd_b, row(g_post_ffn[l]))
    y_prompt = _combine(yt, x1_ctx, mod3, ctx_cond, 0)
    y_sample = _combine(yt, x1_dec, mod3, dec_cond, n_ctx_tok // TM)
    return (y_prompt.reshape(x_prompt.shape), y_sample.reshape(x_sample.shape), new_state)
```

```python
import functools
import math

import jax
import jax.numpy as jnp
from jax import lax
from jax.experimental import pallas as pl
from jax.experimental.pallas import tpu as pltpu

D_MODEL = 1024
D_RNN = 512
D_SGU = 512
N_HEADS_RNN = 8
HEAD_RNN = D_RNN // N_HEADS_RNN
N_HEADS_SGU = 8
HEAD_SGU = D_SGU // N_HEADS_SGU
CHUNK = 128
GRID_W = 64
RG_C = 8.0
N_GROUPS = 4
EXPERTS_PER_GROUP = 4
N_EXPERTS = N_GROUPS * EXPERTS_PER_GROUP
D_EXPERT = 512
EPS = 1e-6
POS_BASE = 10000.0

LANES = 128
SUBLANES = 8
CONV_W = 4
CONV_LEFT = 2
PAIR = 2
RNN_BLOCKS = D_RNN // LANES
TMJ_ROWS = PAIR * RNN_BLOCKS
ROUTER_LANES = LANES
E_LANE0 = N_GROUPS
ROUTER_ROWS = 32
E_ROW0 = SUBLANES

PAIRS_PER_GROUP = EXPERTS_PER_GROUP * (EXPERTS_PER_GROUP - 1) // 2
N_BUCKETS = N_GROUPS * PAIRS_PER_GROUP

ROW_TILES = D_MODEL // LANES

TM = 512
DMA_UNROLL = 8
EXPERT_AHEAD = 2
EXPERT_SLOTS = EXPERT_AHEAD + 1
EXPERT_ROW_PARTS = 2
TILE_ROW_PARTS = 1
TMX = 512
TT = TM // PAIR
LC = 256
TS = 16
PB = 2
VMEM_LIMIT = 56 * 1024 * 1024

F32 = jnp.float32
BF16 = jnp.bfloat16


def _params(sem):
    return pltpu.CompilerParams(dimension_semantics=sem, vmem_limit_bytes=VMEM_LIMIT)


def _rms(x):
    return x * lax.rsqrt(jnp.mean(x * x, axis=-1, keepdims=True) + EPS)


def _sigmoid(x):
    return 0.5 * jnp.tanh(0.5 * x) + 0.5


def _mod_kernel(cond_ref, w_ref, b_ref, o_ref):
    c = cond_ref[...]
    s = c * _sigmoid(c)
    o_ref[...] = jnp.dot(s.astype(BF16), w_ref[...].astype(BF16),
                         preferred_element_type=F32) + b_ref[...]


def _modulation(cond, w_mod, b_mod):
    n = w_mod.shape[1]
    bn = 1024
    return pl.pallas_call(
        _mod_kernel,
        out_shape=jax.ShapeDtypeStruct((cond.shape[0], n), F32),
        grid=(n // bn,),
        in_specs=[pl.BlockSpec(cond.shape, lambda j: (0, 0)),
                  pl.BlockSpec((D_MODEL, bn), lambda j: (0, j)),
                  pl.BlockSpec((1, bn), lambda j: (0, j))],
        out_specs=pl.BlockSpec((cond.shape[0], bn), lambda j: (0, j)),
        compiler_params=_params(("arbitrary",)),
        name="modulation",
    )(cond, w_mod, b_mod.reshape(1, n))


def _pos_kernel(o_ref):
    n_freq = D_MODEL // 4
    k = lax.broadcasted_iota(jnp.int32, (GRID_W, n_freq), 1).astype(F32)
    p = lax.broadcasted_iota(jnp.int32, (GRID_W, n_freq), 0).astype(F32)
    freq = jnp.exp(-math.log(POS_BASE) * k / n_freq)
    ang = p * freq
    o_ref[:, 0:n_freq] = jnp.sin(ang)
    o_ref[:, n_freq:2 * n_freq] = jnp.cos(ang)


def _pos_table():
    return pl.pallas_call(
        _pos_kernel,
        out_shape=jax.ShapeDtypeStruct((GRID_W, D_MODEL // 2), F32),
        name="pos_table",
    )()


def _add_pos(x, pos_refs, q0):
    if pos_refs is None:
        return x
    rows_ref, cols_ref = pos_refs
    reps = x.shape[0] // GRID_W
    rpart = jnp.concatenate(
        [jnp.broadcast_to(rows_ref[q:q + 1, :], (GRID_W, D_MODEL // 2)) for q in range(q0, q0 + reps)], axis=0)
    cpart = jnp.concatenate([cols_ref[...]] * reps, axis=0)
    return jnp.concatenate([x[:, :D_MODEL // 2] + rpart, x[:, D_MODEL // 2:] + cpart], axis=1)


def _load_x(x_ref, pos_refs, r0, n):
    return _add_pos(x_ref[r0:r0 + n, :], pos_refs, r0 // GRID_W)


def _premix_kernel(*refs, add_pos):
    refs = list(refs)
    x_ref = refs.pop(0)
    pos_refs = (refs.pop(0), refs.pop(0)) if add_pos else None
    mod_ref, g_ref, win_ref, sgug_ref, sguw_ref, sgub_ref, xr_ref, gg_ref, ys_ref = refs
    hn = []
    for s in range(PAIR):
        shift = mod_ref[s, :, 0:D_MODEL]
        scale = mod_ref[s, :, D_MODEL:2 * D_MODEL]
        hn.append(_rms(_add_pos(x_ref[s], pos_refs, 0)) * (g_ref[...] * (1.0 + scale)) + shift)
    z = jnp.dot(jnp.concatenate(hn, axis=0).astype(BF16), win_ref[...],
                preferred_element_type=F32)
    half = D_SGU // 2
    heads_per_half = N_HEADS_SGU // 2
    lane_head = lax.broadcasted_iota(jnp.int32, (CHUNK, half), 1) // HEAD_SGU
    for s in range(PAIR):
        zs = z[s * TT:(s + 1) * TT]
        gg = jax.nn.gelu(zs[:, D_RNN:2 * D_RNN])
        for k in range(RNN_BLOCKS):
            rows = pl.ds(s * RNN_BLOCKS + k, TT, stride=TMJ_ROWS)
            xr_ref[rows, :] = zs[:, k * LANES:(k + 1) * LANES]
            gg_ref[rows, :] = gg[:, k * LANES:(k + 1) * LANES]
        u = zs[:, 2 * D_RNN:2 * D_RNN + D_SGU]
        vn = (_rms(zs[:, 2 * D_RNN + D_SGU:]) * sgug_ref[...]).astype(BF16)
        for c in range(TT // CHUNK):
            rows = slice(c * CHUNK, (c + 1) * CHUNK)
            halves = []
            for hf in range(2):
                r = jnp.dot(sguw_ref[hf], vn[rows, hf * half:(hf + 1) * half],
                            preferred_element_type=F32)
                sel = jnp.zeros((CHUNK, half), F32)
                for h in range(heads_per_half):
                    sel = jnp.where(lane_head == h, r[h * CHUNK:(h + 1) * CHUNK], sel)
                halves.append(sel)
            gatev = jnp.concatenate(halves, axis=1) + sgub_ref[...]
            ys_ref[s, rows, :] = (u[rows] * gatev).astype(BF16)


def _premix(x, mod3, cond_block, g_pre, w_in_b, sgu_g, sgu_w_b, sgu_bias_tile, pos_tab):
    n_seq, seq_len, _ = x.shape
    n_pairs, n_tiles = n_seq // PAIR, seq_len // TT
    add_pos = pos_tab is not None
    const2 = lambda p, j: (0, 0)
    in_specs = [pl.BlockSpec((PAIR, TT, D_MODEL), lambda p, j: (p, j, 0))]
    args = [x]
    if add_pos:
        reps = TT // GRID_W
        in_specs += [pl.BlockSpec((None, reps, D_MODEL // 2), lambda p, j: (j, 0, 0)),
                     pl.BlockSpec((GRID_W, D_MODEL // 2), const2)]
        args += [pos_tab.reshape(GRID_W // reps, reps, D_MODEL // 2), pos_tab]
    in_specs += [pl.BlockSpec((PAIR, 1, 6 * D_MODEL), lambda p, j: (cond_block(p), 0, 0)),
                 pl.BlockSpec((1, D_MODEL), const2),
                 pl.BlockSpec((D_MODEL, 2 * D_RNN + 2 * D_SGU), const2),
                 pl.BlockSpec((1, D_SGU), const2),
                 pl.BlockSpec((2, 4 * CHUNK, CHUNK), lambda p, j: (0, 0, 0)),
                 pl.BlockSpec((CHUNK, D_SGU), const2)]
    args += [mod3, g_pre, w_in_b, sgu_g, sgu_w_b, sgu_bias_tile]
    tmj = jax.ShapeDtypeStruct((n_pairs * seq_len * TMJ_ROWS, LANES), F32)
    tmj_spec = pl.BlockSpec((TT * TMJ_ROWS, LANES), lambda p, j: (p * n_tiles + j, 0))
    xr, gg, y_sgu = pl.pallas_call(
        functools.partial(_premix_kernel, add_pos=add_pos),
        out_shape=(tmj, tmj, jax.ShapeDtypeStruct((n_seq, seq_len, D_SGU), BF16)),
        grid=(n_pairs, n_tiles),
        in_specs=in_specs,
        out_specs=(tmj_spec, tmj_spec, pl.BlockSpec((PAIR, TT, D_SGU), lambda p, j: (p, j, 0))),
        compiler_params=_params(("parallel", "parallel")),
        name="premix",
    )(*args)
    shape4 = (n_pairs, seq_len, TMJ_ROWS, LANES)
    return xr.reshape(shape4), gg.reshape(shape4), y_sgu


def _scan_kernel(*refs, reverse, n_chunks):
    if reverse:
        (xprev_ref, x_ref, xnext_ref, gg_ref, hf_ref, h0_ref, cw_ref, cb_ref, wg_ref, bg_ref, lam_ref,
         y_ref, fs_ref, xwin, xc_s, r_s, i_s, a_s, b_s, y_s, hcar) = refs
    else:
        (xprev_ref, x_ref, xnext_ref, h0_ref, cw_ref, cb_ref, wg_ref, bg_ref, lam_ref,
         hf_ref, fs_ref, xwin, xc_s, r_s, i_s, a_s, b_s, hcar) = refs
    c = pl.program_id(1)
    chunk = n_chunks - 1 - c if reverse else c
    sub_rows = TS * TMJ_ROWS

    def rows_of(pb, t0, n_steps):
        return pl.ds(pl.multiple_of((pb * LC + t0) * TMJ_ROWS, TMJ_ROWS), n_steps * TMJ_ROWS)

    @pl.when(c == 0)
    def _():
        hcar[...] = h0_ref[...]

    xwin[:, 0:CONV_LEFT] = jnp.where(chunk > 0, xprev_ref[...], 0.0)
    xwin[:, LC + CONV_LEFT:LC + CONV_W - 1] = jnp.where(chunk < n_chunks - 1, xnext_ref[...], 0.0)

    def copy(i, carry):
        t0 = pl.multiple_of(i * TS, TS)
        xwin[:, pl.ds(t0 + CONV_LEFT, TS)] = x_ref[:, pl.ds(t0, TS)]
        return carry

    lax.fori_loop(0, LC // TS, copy, 0)

    def conv(i, carry):
        t0 = pl.multiple_of(i * TS, TS)
        xc = cb_ref[...] + cw_ref[0] * xwin[:, pl.ds(t0, TS)]
        for k in range(1, CONV_W):
            xc = xc + cw_ref[k] * xwin[:, pl.ds(t0 + k, TS)]
        for pb in range(PB):
            xc_s[rows_of(pb, t0, TS), :] = xc[pb].reshape(sub_rows, LANES)
        return carry

    lax.fori_loop(0, LC // TS, conv, 0)

    n_rows = PB * LC * PAIR
    for k in range(RNN_BLOCKS):
        rows = pl.ds(k, n_rows, stride=RNN_BLOCKS)
        g = jnp.dot(xc_s[rows, :].astype(BF16), wg_ref[k], preferred_element_type=F32)
        r_s[rows, :] = g[:, :LANES]
        i_s[rows, :] = g[:, LANES:]

    neg_lam = -lam_ref[...]
    softplus = jnp.maximum(neg_lam, 0.0) + jnp.log(1.0 + jnp.exp(-jnp.abs(neg_lam)))
    half_decay = (-0.5 * RG_C * math.log2(math.e)) * softplus

    def gates(i, carry):
        t0 = pl.multiple_of(i * TS, TS)
        for pb in range(PB):
            rows = rows_of(pb, t0, TS)
            tile = lambda ref: ref[rows, :].reshape(TS, TMJ_ROWS, LANES)
            tr = jnp.tanh(tile(r_s) + bg_ref[0])
            ti = jnp.tanh(tile(i_s) + bg_ref[1])
            log2_a = tr * half_decay + half_decay
            a = jnp.exp2(log2_a)
            q = jnp.tanh(log2_a * (-math.log(2.0))) * (a * a + 1.0)
            b = jnp.where(q > 0.0, q * lax.rsqrt(q), 0.0) * ((ti + 1.0) * tile(xc_s))
            a_s[rows, :] = a.reshape(sub_rows, LANES)
            b_s[rows, :] = b.reshape(sub_rows, LANES)
        return carry

    lax.fori_loop(0, LC // TS, gates, 0)

    def step(j, hs):
        t = LC - 1 - j if reverse else j
        out = []
        for pb in range(PB):
            rows = rows_of(pb, t, 1)
            h = a_s[rows, :] * hs[pb] + b_s[rows, :]
            if reverse:
                y_s[rows, :] = (hf_ref[pb, t] + h) * gg_ref[pb, t]
            else:
                hf_ref[pb, t] = h
            out.append(h)
        return tuple(out)

    hs = lax.fori_loop(0, LC, step, tuple(hcar[pb] for pb in range(PB)), unroll=8)
    for pb in range(PB):
        hcar[pb] = hs[pb]
        fs_ref[pb] = hs[pb]

    if reverse:
        for pb in range(PB):
            for s in range(PAIR):
                cols = [y_s[pl.ds(pb * LC * TMJ_ROWS + s * RNN_BLOCKS + k, LC, stride=TMJ_ROWS), :]
                        for k in range(RNN_BLOCKS)]
                y_ref[pb * PAIR + s] = jnp.concatenate(cols, axis=1).astype(BF16)


def _scan(xr, gg, hf, h0, conv_w, conv_b, w_gates, b_gates, lam, direction):
    n_pairs, seq_len = xr.shape[:2]
    n_chunks = seq_len // LC
    reverse = direction == 1
    pos = (lambda c: n_chunks - 1 - c) if reverse else (lambda c: c)
    tmj_blk = pl.BlockSpec((PB, LC, TMJ_ROWS, LANES), lambda i, c: (i, pos(c), 0, 0))
    state_blk = pl.BlockSpec((PB, TMJ_ROWS, LANES), lambda i, c: (i, 0, 0))
    per_dir = lambda *shape: pl.BlockSpec((None,) + shape, lambda i, c: (direction,) + (0,) * len(shape))
    in_specs = [
        pl.BlockSpec((PB, CONV_LEFT, TMJ_ROWS, LANES),
                     lambda i, c: (i, jnp.maximum(pos(c) * (LC // CONV_LEFT) - 1, 0), 0, 0)),
        tmj_blk,
        pl.BlockSpec((PB, 1, TMJ_ROWS, LANES), lambda i, c: (i, jnp.minimum((pos(c) + 1) * LC, seq_len - 1), 0, 0)),
    ]
    args = [xr, xr, xr]
    if reverse:
        in_specs += [tmj_blk, tmj_blk]
        args += [gg, hf]
    in_specs += [state_blk,
                 pl.BlockSpec((CONV_W, TMJ_ROWS, LANES), lambda i, c: (0, 0, 0)),
                 pl.BlockSpec((TMJ_ROWS, LANES), lambda i, c: (0, 0)),
                 per_dir(RNN_BLOCKS, LANES, 2 * LANES),
                 per_dir(2, TMJ_ROWS, LANES),
                 per_dir(TMJ_ROWS, LANES)]
    args += [h0, conv_w, conv_b, w_gates, b_gates, lam]
    flat = pltpu.VMEM((PB * LC * TMJ_ROWS, LANES), F32)
    scratch = [pltpu.VMEM((PB, LC + CONV_W - 1, TMJ_ROWS, LANES), F32)] + [flat] * (6 if reverse else 5)
    scratch += [pltpu.VMEM((PB, TMJ_ROWS, LANES), F32)]
    state = jax.ShapeDtypeStruct((n_pairs, TMJ_ROWS, LANES), F32)
    if reverse:
        out_shape = (jax.ShapeDtypeStruct((n_pairs * PAIR, seq_len, D_RNN), BF16), state)
        out_specs = (pl.BlockSpec((PB * PAIR, LC, D_RNN), lambda i, c: (i, pos(c), 0)), state_blk)
    else:
        out_shape = (jax.ShapeDtypeStruct(xr.shape, F32), state)
        out_specs = (tmj_blk, state_blk)
    return pl.pallas_call(
        functools.partial(_scan_kernel, reverse=reverse, n_chunks=n_chunks),
        out_shape=out_shape,
        grid=(n_pairs // PB, n_chunks),
        in_specs=in_specs,
        out_specs=out_specs,
        scratch_shapes=scratch,
        compiler_params=_params(("parallel", "arbitrary")),
        name="scan_bwd" if reverse else "scan_fwd",
    )(*args)


def _route(lt):
    n = lt.shape[1]
    row = lax.broadcasted_iota(jnp.int32, (EXPERTS_PER_GROUP, n), 0)
    neg = jnp.float32(-jnp.inf)

    def arg_max(v):
        m = jnp.max(v, axis=0, keepdims=True)
        return jnp.min(jnp.where(v == m, row, EXPERTS_PER_GROUP), axis=0, keepdims=True)

    g_idx = arg_max(lt[0:N_GROUPS])
    el = lt[E_ROW0:E_ROW0 + EXPERTS_PER_GROUP]
    for g in range(1, N_GROUPS):
        first = E_ROW0 + g * EXPERTS_PER_GROUP
        el = jnp.where(g_idx == g, lt[first:first + EXPERTS_PER_GROUP], el)
    i1 = arg_max(el)
    i2 = arg_max(jnp.where(row == i1, neg, el))
    ja = jnp.minimum(i1, i2)
    jb = jnp.maximum(i1, i2)
    pair = (ja * (2 * EXPERTS_PER_GROUP - 1 - ja)) // 2 + (jb - ja - 1)
    return g_idx * PAIRS_PER_GROUP + pair


def _store_token_major(ref, x, t0=0):
    n = x.shape[0]
    for k in range(ROW_TILES):
        ref[pl.ds(t0 * ROW_TILES + k, n, stride=ROW_TILES), :] = x[:, k * LANES:(k + 1) * LANES]


def _load_token_major(ref, n):
    return jnp.concatenate([ref[pl.ds(k, n, stride=ROW_TILES), :] for k in range(ROW_TILES)], axis=1)


def _postmix_kernel(*refs, add_pos, n_aliased):
    refs = list(refs)
    x_ref = refs.pop(0)
    pos_refs = (refs.pop(0), refs.pop(0)) if add_pos else None
    (yr_ref, ys_ref, mod_ref, gpost_ref, gpre_ref, wout_ref, rw_ref, rb_ref, earlier_ref, cnt0_ref) = refs[:10]
    x1_ref, hn_ref, rt_ref, cnt_ref, run_ref = refs[10 + n_aliased:]

    @pl.when(pl.program_id(0) == 0)
    def _():
        run_ref[...] = cnt0_ref[...]

    gate1 = mod_ref[0, :, 2 * D_MODEL:3 * D_MODEL]
    shift2 = mod_ref[0, :, 3 * D_MODEL:4 * D_MODEL]
    scale2 = mod_ref[0, :, 4 * D_MODEL:5 * D_MODEL]
    y = (jnp.dot(yr_ref[...], wout_ref[0:D_RNN, :], preferred_element_type=F32)
         + jnp.dot(ys_ref[...], wout_ref[D_RNN:, :], preferred_element_type=F32))
    x1 = _load_x(x_ref, pos_refs, 0, TM) + _rms(y) * (gate1 * gpost_ref[...])
    x1_ref[...] = x1
    hn = _rms(x1) * (gpre_ref[...] * (1.0 + scale2)) + shift2
    _store_token_major(hn_ref, hn)
    lt = lax.dot_general(rw_ref[...], hn.astype(BF16), (((1,), (1,)), ((), ())),
                         preferred_element_type=F32) + rb_ref[:, 0:1]
    bucket = _route(lt)
    onehot = lax.broadcasted_iota(jnp.int32, (ROUTER_ROWS, TM), 0) == bucket
    before = jnp.dot(onehot.astype(BF16), earlier_ref[...], preferred_element_type=F32) + run_ref[:, 0:1]
    rank = jnp.sum(jnp.where(onehot, before, 0.0), axis=0, keepdims=True).astype(jnp.int32)
    row = lax.broadcasted_iota(jnp.int32, (SUBLANES, TM), 0)
    rt_ref[...] = jnp.where(row == 0, bucket, jnp.where(row == 1, rank, 0))
    run_ref[...] += jnp.sum(onehot.astype(F32), axis=1, keepdims=True)
    cnt_ref[...] = run_ref[...]


def _postmix(x, y_rnn, y_sgu, mod3, cond_of_tile, g_post, g_pre, w_out_b, router_wt, router_bt, earlier, pos_tab,
             counts0, hn_all, tile0, n_all):
    n_tok = x.shape[0]
    add_pos = pos_tab is not None
    tok = lambda i: (i, 0)
    const2 = lambda i: (0, 0)
    in_specs = [pl.BlockSpec((TM, D_MODEL), tok)]
    args = [x]
    if add_pos:
        reps = TM // GRID_W
        tiles_per_seq = GRID_W // reps
        in_specs += [pl.BlockSpec((None, reps, D_MODEL // 2), lambda i: (i % tiles_per_seq, 0, 0)),
                     pl.BlockSpec((GRID_W, D_MODEL // 2), const2)]
        args += [pos_tab.reshape(tiles_per_seq, reps, D_MODEL // 2), pos_tab]
    in_specs += [pl.BlockSpec((TM, D_RNN), tok),
                 pl.BlockSpec((TM, D_SGU), tok),
                 pl.BlockSpec((1, 1, 6 * D_MODEL), lambda i: (cond_of_tile(i), 0, 0)),
                 pl.BlockSpec((1, D_MODEL), const2),
                 pl.BlockSpec((1, D_MODEL), const2),
                 pl.BlockSpec((D_MODEL, D_MODEL), const2),
                 pl.BlockSpec((ROUTER_ROWS, D_MODEL), const2),
                 pl.BlockSpec((ROUTER_ROWS, LANES), const2),
                 pl.BlockSpec((TM, TM), const2),
                 pl.BlockSpec((ROUTER_ROWS, LANES), const2)]
    args += [y_rnn, y_sgu, mod3, g_post, g_pre, w_out_b, router_wt, router_bt, earlier, counts0]
    aliases = {}
    if hn_all is not None:
        aliases = {len(args): 1}
        in_specs += [pl.BlockSpec(memory_space=pl.ANY)]
        args += [hn_all]
    n_tiles = n_tok // TM
    counts_spec = pl.BlockSpec((ROUTER_ROWS, LANES), const2)
    x1, hn, route, counts = pl.pallas_call(
        functools.partial(_postmix_kernel, add_pos=add_pos, n_aliased=len(aliases)),
        out_shape=(jax.ShapeDtypeStruct((n_tok, D_MODEL), F32),
                   jax.ShapeDtypeStruct((n_all * ROW_TILES, LANES), F32),
                   jax.ShapeDtypeStruct((n_tiles * SUBLANES, TM), jnp.int32),
                   jax.ShapeDtypeStruct((ROUTER_ROWS, LANES), F32)),
        grid=(n_tiles,),
        in_specs=in_specs,
        out_specs=(pl.BlockSpec((TM, D_MODEL), tok),
                   pl.BlockSpec((TM * ROW_TILES, LANES), lambda i: (i + tile0, 0)),
                   pl.BlockSpec((SUBLANES, TM), tok),
                   counts_spec),
        scratch_shapes=[pltpu.VMEM((ROUTER_ROWS, LANES), F32)],
        input_output_aliases=aliases,
        compiler_params=_params(("arbitrary",)),
        name="postmix",
    )(*args)
    route = route.reshape(n_tiles, SUBLANES, TM)
    return x1, hn, route[:, 0].reshape(n_tok), route[:, 1].reshape(n_tok), counts


def _token_rows(ref, t):
    return ref.at[pl.ds(pl.multiple_of(t * ROW_TILES, ROW_TILES), ROW_TILES), :]


def _invert_kernel(dest_ref, spare_ref, src_ref, sem, *, n_tok):
    copy = pltpu.make_async_copy(spare_ref, src_ref, sem)
    copy.start()
    copy.wait()

    def place(t, carry):
        src_ref[dest_ref[t]] = t
        return carry

    lax.fori_loop(0, n_tok, place, 0, unroll=16)


def _invert(dest, n_slots):
    n_tok = dest.shape[0]
    spare = n_tok + jnp.arange(n_slots, dtype=jnp.int32) % TMX
    return pl.pallas_call(
        functools.partial(_invert_kernel, n_tok=n_tok),
        out_shape=jax.ShapeDtypeStruct((n_slots,), jnp.int32),
        grid_spec=pltpu.PrefetchScalarGridSpec(
            num_scalar_prefetch=1,
            grid=(1,),
            in_specs=[pl.BlockSpec(memory_space=pl.ANY)],
            out_specs=pl.BlockSpec(memory_space=pltpu.SMEM),
            scratch_shapes=[pltpu.SemaphoreType.DMA(())]),
        compiler_params=_params(("arbitrary",)),
        name="invert",
    )(dest, spare)


def _experts_kernel(ea_ref, eb_ref, nv_ref, src_ref, total_ref, hn_ref, rw_ref, rb_ref,
                    wga_ref, wua_ref, wda_ref, wgb_ref, wub_ref, wdb_ref, gpost_ref, ys_ref,
                    xbuf, gsem, *, n_tok):
    i = pl.program_id(0)
    total = total_ref[0]
    tile_rows = TMX * ROW_TILES

    def gather_row(tile, slot, r, queue):
        tok = jnp.minimum(src_ref[tile * TMX + r], n_tok - 1)
        pltpu.make_async_copy(_token_rows(hn_ref, tok), _token_rows(xbuf.at[slot], r),
                              gsem.at[slot]).start(priority=queue)

    def wait_gather(slot):
        pltpu.make_async_copy(hn_ref.at[pl.ds(0, tile_rows), :], xbuf.at[slot], gsem.at[slot]).wait()

    @pl.when(i == 0)
    def _():
        for t in range(EXPERT_AHEAD):
            def start(g, carry, t=t):
                for u in range(DMA_UNROLL):
                    gather_row(jnp.minimum(t, total - 1), t, g * DMA_UNROLL + u, u % 2)
                return carry

            lax.fori_loop(0, TMX // DMA_UNROLL, start, 0)

    @pl.when(i == total)
    def _():
        for t in range(EXPERT_AHEAD):
            wait_gather((total + t) % EXPERT_SLOTS)

    @pl.when(i < total)
    def _():
        nv = nv_ref[i]
        slot = i % EXPERT_SLOTS
        wait_gather(slot)
        row = lax.broadcasted_iota(jnp.int32, (TMX, 1), 0)
        xb = jnp.where(row < nv, _load_token_major(xbuf.at[slot], TMX), 0.0).astype(BF16)
        ahead = jnp.minimum(i + EXPERT_AHEAD, total - 1)
        for r in range(TMX):
            gather_row(ahead, (i + EXPERT_AHEAD) % EXPERT_SLOTS, r, r % 2)
        logits = jnp.dot(xb, rw_ref[...], preferred_element_type=F32) + rb_ref[...]
        lane = lax.broadcasted_iota(jnp.int32, logits.shape, 1)
        ea = ea_ref[i]
        eb = eb_ref[i]
        gmask = lane < N_GROUPS
        gl = jnp.where(gmask, logits, -jnp.inf)
        gmax = jnp.max(gl, axis=-1, keepdims=True)
        gexp = jnp.where(gmask, jnp.exp(gl - gmax), 0.0)
        g_own = jnp.sum(jnp.where(lane == ea // EXPERTS_PER_GROUP, gexp, 0.0), axis=-1, keepdims=True)
        g_w = g_own / jnp.sum(gexp, axis=-1, keepdims=True)
        la = jnp.sum(jnp.where(lane == ea + E_LANE0, logits, 0.0), axis=-1, keepdims=True)
        lb = jnp.sum(jnp.where(lane == eb + E_LANE0, logits, 0.0), axis=-1, keepdims=True)
        m = jnp.maximum(la, lb)
        pa = jnp.exp(la - m)
        pb = jnp.exp(lb - m)
        inv = g_w / (pa + pb)

        def hidden(x, wg_ref, wu_ref, w):
            g = jnp.dot(x, wg_ref[0], preferred_element_type=F32)
            u = jnp.dot(x, wu_ref[0], preferred_element_type=F32)
            return ((g * _sigmoid(g)) * u * w).astype(BF16)

        part = TMX // EXPERT_ROW_PARTS
        ys = []
        for h in range(EXPERT_ROW_PARTS):
            rows = slice(h * part, (h + 1) * part)
            act_a = hidden(xb[rows], wga_ref, wua_ref, (pa * inv)[rows])
            act_b = hidden(xb[rows], wgb_ref, wub_ref, (pb * inv)[rows])
            y =(jnp.dot(act_a, wda_ref[0], preferred_element_type=F32)
                 + jnp.dot(act_b, wdb_ref[0], preferred_element_type=F32))
            ys.append(_rms(y) * gpost_ref[...])
        _store_token_major(ys_ref, jnp.concatenate(ys, axis=0))

    @pl.when(i >= total)
    def _():
        ys_ref[...] = jnp.zeros_like(ys_ref)


def _experts(sched, src, total, hn, router_w, router_b, wg_b, wu_b, wd_b, g_post):
    ea, eb, nv = sched
    n_steps = ea.shape[0]
    n_tok = hn.shape[0] // ROW_TILES
    const2 = lambda i, *_: (0, 0)
    exp_a = lambda i, ea, eb, nv, src, total: (ea[i], 0, 0)
    exp_b = lambda i, ea, eb, nv, src, total: (eb[i], 0, 0)
    w_in_spec = lambda m: pl.BlockSpec((1, D_MODEL, D_EXPERT), m)
    w_out_spec = lambda m: pl.BlockSpec((1, D_EXPERT, D_MODEL), m)
    tile = (TMX * ROW_TILES, LANES)
    return pl.pallas_call(
        functools.partial(_experts_kernel, n_tok=n_tok),
        out_shape=jax.ShapeDtypeStruct((n_steps * tile[0], LANES), F32),
        grid_spec=pltpu.PrefetchScalarGridSpec(
            num_scalar_prefetch=5,
            grid=(n_steps,),
            in_specs=[pl.BlockSpec(memory_space=pl.ANY),
                      pl.BlockSpec((D_MODEL, ROUTER_LANES), const2),
                      pl.BlockSpec((1, ROUTER_LANES), const2),
                      w_in_spec(exp_a), w_in_spec(exp_a), w_out_spec(exp_a),
                      w_in_spec(exp_b), w_in_spec(exp_b), w_out_spec(exp_b),
                      pl.BlockSpec((1, D_MODEL), const2)],
            out_specs=pl.BlockSpec(tile, lambda i, *_: (i, 0)),
            scratch_shapes=[pltpu.VMEM((EXPERT_SLOTS,) + tile, F32), pltpu.SemaphoreType.DMA((EXPERT_SLOTS,))]),
        compiler_params=_params(("arbitrary",)),
        name="experts",
    )(ea, eb, nv, src, total, hn, router_w, router_b, wg_b, wu_b, wd_b, wg_b, wu_b, wd_b, g_post)


def _combine_kernel(dest_ref, ys_ref, x1_ref, mod_ref, o_ref, ybuf, sems, *, tile0):
    i = pl.program_id(0)
    n = pl.num_programs(0)

    def fetch(tile, slot):
        base = (tile + tile0) * TM

        def start(g, carry):
            for u in range(DMA_UNROLL):
                r = g * DMA_UNROLL + u
                pltpu.make_async_copy(_token_rows(ys_ref, dest_ref[base + r]), _token_rows(ybuf.at[slot], r),
                                      sems.at[slot]).start(priority=u % 2)
            return carry

        lax.fori_loop(0, TM // DMA_UNROLL, start, 0)

    @pl.when(i == 0)
    def _():
        fetch(0, 0)

    @pl.when(i + 1 < n)
    def _():
        fetch(i + 1, (i + 1) % 2)

    slot = i % 2
    pltpu.make_async_copy(ys_ref.at[pl.ds(0, TM * ROW_TILES), :], ybuf.at[slot], sems.at[slot]).wait()
    gate2 = mod_ref[0, :, 5 * D_MODEL:6 * D_MODEL]
    o_ref[...] = x1_ref[...] + gate2 * _load_token_major(ybuf.at[slot], TM)


def _combine(dest, ys, x1, mod3, cond_of_tile, tile0):
    n_tok = x1.shape[0]
    return pl.pallas_call(
        functools.partial(_combine_kernel, tile0=tile0),
        out_shape=jax.ShapeDtypeStruct((n_tok, D_MODEL), F32),
        grid_spec=pltpu.PrefetchScalarGridSpec(
            num_scalar_prefetch=1,
            grid=(n_tok // TM,),
            in_specs=[pl.BlockSpec(memory_space=pl.ANY),
                      pl.BlockSpec((TM, D_MODEL), lambda i, d: (i, 0)),
                      pl.BlockSpec((1, 1, 6 * D_MODEL), lambda i, d: (cond_of_tile(i), 0, 0))],
            out_specs=pl.BlockSpec((TM, D_MODEL), lambda i, d: (i, 0)),
            scratch_shapes=[pltpu.VMEM((2, TM * ROW_TILES, LANES), F32), pltpu.SemaphoreType.DMA((2,))]),
        compiler_params=_params(("arbitrary",)),
        name="combine",
    )(dest, ys, x1, mod3)


def _schedule(bucket, rank, counts):
    n_tok = bucket.shape[0]
    n_max = n_tok // TMX + N_BUCKETS
    cnt = counts[:N_BUCKETS, 0].astype(jnp.int32)
    tiles = (cnt + TMX - 1) // TMX
    tile_end = jnp.cumsum(tiles)
    tile_start = tile_end - tiles
    ids = jnp.arange(N_BUCKETS, dtype=jnp.int32)
    slot0 = jnp.sum(jnp.where(bucket[:, None] == ids[None, :], (tile_start * TMX)[None, :], 0), axis=1)
    dest = slot0 + rank
    n_steps = n_max + 1
    i = jnp.arange(n_steps, dtype=jnp.int32)
    total = tile_end[-1]
    valid = i < total
    tb = jnp.sum((jnp.minimum(i, total - 1)[:, None] >= tile_end[None, :]).astype(jnp.int32), axis=1)
    pairs = [(a, b) for a in range(EXPERTS_PER_GROUP) for b in range(a + 1, EXPERTS_PER_GROUP)]
    ea_tab = jnp.array([g * EXPERTS_PER_GROUP + a for g in range(N_GROUPS) for a, _ in pairs], jnp.int32)
    eb_tab = jnp.array([g * EXPERTS_PER_GROUP + b for g in range(N_GROUPS) for _, b in pairs], jnp.int32)
    ea = ea_tab[tb]
    eb = eb_tab[tb]
    nv = jnp.where(valid, jnp.clip(cnt[tb] - (i - tile_start[tb]) * TMX, 0, TMX), 0)
    return dest, (ea, eb, nv), total[None], n_steps * TMX


def _block_diag_gates(rg_wa, rg_wx):
    heads = LANES // HEAD_RNN

    def bd(w):
        w = w.reshape(2, RNN_BLOCKS, heads, HEAD_RNN, HEAD_RNN)
        eye = jnp.eye(heads, dtype=w.dtype)
        full = jnp.einsum('dghij,hk->dghikj', w, eye)
        return full.reshape(2, RNN_BLOCKS, LANES, LANES)

    return jnp.concatenate([bd(rg_wa), bd(rg_wx)], axis=-1).astype(BF16)


def _row_tile(v):
    blocks = v.reshape(v.shape[:-1] + (RNN_BLOCKS, LANES))
    return jnp.concatenate([blocks] * PAIR, axis=-2)


def _to_time_major_state(h):
    return h.reshape(h.shape[0] // PAIR, TMJ_ROWS, LANES)


def kernel(x_prompt, x_sample, state_rglru, c, c_ctx, w_mod, b_mod, g_pre_mix, g_post_mix, g_pre_ffn,
           g_post_ffn, w_in, conv_w, conv_b, rg_wa, rg_ba, rg_wx, rg_bx, rg_lambda, sgu_g, sgu_w, sgu_b,
           w_out, router_g_w, router_g_b, router_e_w, router_e_b, exp_w_gate, exp_w_up, exp_w_down):
    assert w_mod.shape[0] == 1, "single-layer trunk"
    n_ctx, ctx_len, _ = x_prompt.shape
    n_dec, dec_len, _ = x_sample.shape
    l = 0

    n_cond = SUBLANES
    assert n_dec % PAIR == 0 and n_dec + PAIR <= n_cond
    cond = jnp.zeros((n_cond, D_MODEL), F32).at[:n_dec].set(c).at[n_dec:n_dec + PAIR].set(c_ctx)
    mod3 = _modulation(cond, w_mod[l], b_mod[l]).reshape(n_cond, 1, 6 * D_MODEL)
    pos_tab = _pos_table()

    w_in_b = w_in[l].astype(BF16)
    w_out_b = w_out[l].astype(BF16)
    sgu_w_b = sgu_w[l].reshape(2, 4 * CHUNK, CHUNK).astype(BF16)
    sgu_bias_tile = jnp.repeat(sgu_b[l].T, HEAD_SGU, axis=1)
    w_gates = _block_diag_gates(rg_wa[l], rg_wx[l])
    b_gates = 0.5 * jnp.stack([_row_tile(rg_ba[l]), _row_tile(rg_bx[l])], axis=1)
    lam = _row_tile(rg_lambda[l])
    conv_w_t = 0.5 * _row_tile(conv_w[l])
    conv_b_t = 0.5 * _row_tile(conv_b[l])
    router_w = jnp.zeros((D_MODEL, ROUTER_LANES), F32)
    router_w = router_w.at[:, :N_GROUPS].set(router_g_w[l]).at[:, E_LANE0:E_LANE0 + N_EXPERTS].set(router_e_w[l])
    router_w = router_w.astype(BF16)
    router_b = jnp.zeros((1, ROUTER_LANES), F32)
    router_b = router_b.at[0, :N_GROUPS].set(router_g_b[l]).at[0, E_LANE0:E_LANE0 + N_EXPERTS].set(router_e_b[l])
    router_wt = jnp.zeros((ROUTER_ROWS, D_MODEL), F32)
    router_wt = router_wt.at[:N_GROUPS].set(router_g_w[l].T).at[E_ROW0:E_ROW0 + N_EXPERTS].set(router_e_w[l].T)
    router_wt = router_wt.astype(BF16)
    router_bt = jnp.zeros((ROUTER_ROWS,), F32)
    router_bt = router_bt.at[:N_GROUPS].set(router_g_b[l]).at[E_ROW0:E_ROW0 + N_EXPERTS].set(router_e_b[l])
    router_bt = jnp.broadcast_to(router_bt[:, None], (ROUTER_ROWS, LANES))
    earlier = jnp.triu(jnp.ones((TM, TM), BF16), k=1)
    wg_b = exp_w_gate[l].astype(BF16)
    wu_b = exp_w_up[l].astype(BF16)
    wd_b = exp_w_down[l].astype(BF16)
    row = lambda v: v.reshape(1, -1)

    n_ctx_tok = n_ctx * ctx_len

    n_all = n_ctx_tok + n_dec * dec_len

    def mixer(x, h0, cond_of_tile, cond_block, use_pos, counts0, hn_all, tile0):
        n_seq, seq_len, _ = x.shape
        xf = x.reshape(n_seq * seq_len, D_MODEL)
        tab = pos_tab if use_pos else None
        xr, gg, y_sgu = _premix(x, mod3, cond_block, row(g_pre_mix[l]), w_in_b, row(sgu_g[l]),
                                sgu_w_b, sgu_bias_tile, tab)
        scan_params = (conv_w_t, conv_b_t, w_gates, b_gates, lam)
        hf, hf_last = _scan(xr, None, None, _to_time_major_state(h0[:, 0]), *scan_params, direction=0)
        y_rnn, hb_first = _scan(xr, gg, hf, _to_time_major_state(h0[:, 1]), *scan_params, direction=1)
        fstate = jnp.stack([hf_last.reshape(n_seq, D_RNN), hb_first.reshape(n_seq, D_RNN)], axis=1)
        x1, hn, bucket, rank, counts = _postmix(
            xf, y_rnn.reshape(n_seq * seq_len, D_RNN), y_sgu.reshape(n_seq * seq_len, D_SGU), mod3,
            cond_of_tile, row(g_post_mix[l]), row(g_pre_ffn[l]), w_out_b, router_wt, router_bt, earlier, tab,
            counts0, hn_all, tile0, n_all)
        return x1, hn, bucket, rank, counts, fstate

    tiles_per_seq = dec_len // TM
    ctx_cond = lambda i: n_dec
    dec_cond = lambda i: i // tiles_per_seq
    h0_ctx = jnp.zeros((n_ctx, 2, D_RNN), F32)
    counts0 = jnp.zeros((ROUTER_ROWS, LANES), F32)
    x1_ctx, hn_all, bucket_ctx, rank_ctx, counts, st = mixer(x_prompt, h0_ctx, ctx_cond, lambda p: n_dec // PAIR,
                                                             False, counts0, None, 0)
    new_state = st.astype(state_rglru.dtype)[:, None]
    x1_dec, hn_all, bucket_dec, rank_dec, counts, _ = mixer(x_sample, state_rglru[:, l].astype(F32), dec_cond,
                                                            lambda p: p, True, counts, hn_all, n_ctx_tok // TM)

    dest, sched, total, n_slots = _schedule(jnp.concatenate([bucket_ctx, bucket_dec]),
                                            jnp.concatenate([rank_ctx, rank_dec]), counts)
    src = _invert(dest, n_slots)
    ys = _experts(sched, src, total, hn_all, router_w, router_b, wg_b, wu_b, wd_b, row(g_post_ffn[l]))
    y_prompt = _combine(dest, ys, x1_ctx, mod3, ctx_cond, 0)
    y_sample = _combine(dest, ys, x1_dec, mod3, dec_cond, n_ctx_tok // TM)
    return (y_prompt.reshape(x_prompt.shape), y_sample.reshape(x_sample.shape), new_state)
```

```python
import functools
import math

import jax
import jax.numpy as jnp
from jax import lax
from jax.experimental import pallas as pl
from jax.experimental.pallas import tpu as pltpu

D_MODEL = 1024
D_RNN = 512
D_SGU = 512
N_HEADS_RNN = 8
HEAD_RNN = D_RNN // N_HEADS_RNN
N_HEADS_SGU = 8
HEAD_SGU = D_SGU // N_HEADS_SGU
CHUNK = 128
GRID_W = 64
RG_C = 8.0
N_GROUPS = 4
EXPERTS_PER_GROUP = 4
N_EXPERTS = N_GROUPS * EXPERTS_PER_GROUP
D_EXPERT = 512
EPS = 1e-6
POS_BASE = 10000.0

LANES = 128
SUBLANES = 8
CONV_W = 4
CONV_LEFT = 2
PAIR = 2
RNN_BLOCKS = D_RNN // LANES
TMJ_ROWS = PAIR * RNN_BLOCKS
ROUTER_LANES = LANES
E_LANE0 = N_GROUPS
ROUTER_ROWS = 32
E_ROW0 = SUBLANES

PAIRS_PER_GROUP = EXPERTS_PER_GROUP * (EXPERTS_PER_GROUP - 1) // 2
N_BUCKETS = N_GROUPS * PAIRS_PER_GROUP

ROW_TILES = D_MODEL // LANES

TM = 512
TD = 1024
DMA_UNROLL = 8
EXPERT_ROW_PARTS = 2
TMX = 512
TT = TM // PAIR
LC = 256
TS = 16
PB = 2
VMEM_LIMIT = 56 * 1024 * 1024

F32 = jnp.float32
BF16 = jnp.bfloat16


def _params(sem):
    return pltpu.CompilerParams(dimension_semantics=sem, vmem_limit_bytes=VMEM_LIMIT)


def _rms(x):
    return x * lax.rsqrt(jnp.mean(x * x, axis=-1, keepdims=True) + EPS)


def _sigmoid(x):
    return 0.5 * jnp.tanh(0.5 * x) + 0.5


def _mod_kernel(cond_ref, w_ref, b_ref, o_ref):
    c = cond_ref[...]
    s = c * _sigmoid(c)
    o_ref[...] = jnp.dot(s.astype(BF16), w_ref[...].astype(BF16),
                         preferred_element_type=F32) + b_ref[...]


def _modulation(cond, w_mod, b_mod):
    n = w_mod.shape[1]
    bn = 1024
    return pl.pallas_call(
        _mod_kernel,
        out_shape=jax.ShapeDtypeStruct((cond.shape[0], n), F32),
        grid=(n // bn,),
        in_specs=[pl.BlockSpec(cond.shape, lambda j: (0, 0)),
                  pl.BlockSpec((D_MODEL, bn), lambda j: (0, j)),
                  pl.BlockSpec((1, bn), lambda j: (0, j))],
        out_specs=pl.BlockSpec((cond.shape[0], bn), lambda j: (0, j)),
        compiler_params=_params(("arbitrary",)),
        name="modulation",
    )(cond, w_mod, b_mod.reshape(1, n))


def _pos_kernel(o_ref):
    n_freq = D_MODEL // 4
    k = lax.broadcasted_iota(jnp.int32, (GRID_W, n_freq), 1).astype(F32)
    p = lax.broadcasted_iota(jnp.int32, (GRID_W, n_freq), 0).astype(F32)
    freq = jnp.exp(-math.log(POS_BASE) * k / n_freq)
    ang = p * freq
    o_ref[:, 0:n_freq] = jnp.sin(ang)
    o_ref[:, n_freq:2 * n_freq] = jnp.cos(ang)


def _pos_table():
    return pl.pallas_call(
        _pos_kernel,
        out_shape=jax.ShapeDtypeStruct((GRID_W, D_MODEL // 2), F32),
        name="pos_table",
    )()


def _add_pos(x, pos_refs, q0):
    if pos_refs is None:
        return x
    rows_ref, cols_ref = pos_refs
    reps = x.shape[0] // GRID_W
    rpart = jnp.concatenate(
        [jnp.broadcast_to(rows_ref[q:q + 1, :], (GRID_W, D_MODEL // 2)) for q in range(q0, q0 + reps)], axis=0)
    cpart = jnp.concatenate([cols_ref[...]] * reps, axis=0)
    return jnp.concatenate([x[:, :D_MODEL // 2] + rpart, x[:, D_MODEL // 2:] + cpart], axis=1)


def _load_x(x_ref, pos_refs, r0, n):
    return _add_pos(x_ref[r0:r0 + n, :], pos_refs, r0 // GRID_W)


def _premix_kernel(*refs, add_pos):
    refs = list(refs)
    x_ref = refs.pop(0)
    pos_refs = (refs.pop(0), refs.pop(0)) if add_pos else None
    mod_ref, g_ref, win_ref, sgug_ref, sguw_ref, sgub_ref, xr_ref, gg_ref, ys_ref = refs
    hn = []
    for s in range(PAIR):
        shift = mod_ref[s, :, 0:D_MODEL]
        scale = mod_ref[s, :, D_MODEL:2 * D_MODEL]
        hn.append(_rms(_add_pos(x_ref[s], pos_refs, 0)) * (g_ref[...] * (1.0 + scale)) + shift)
    z = jnp.dot(jnp.concatenate(hn, axis=0).astype(BF16), win_ref[...],
                preferred_element_type=F32)
    half = D_SGU // 2
    heads_per_half = N_HEADS_SGU // 2
    lane_head = lax.broadcasted_iota(jnp.int32, (CHUNK, half), 1) // HEAD_SGU
    for s in range(PAIR):
        zs = z[s * TT:(s + 1) * TT]
        gg = jax.nn.gelu(zs[:, D_RNN:2 * D_RNN])
        for k in range(RNN_BLOCKS):
            rows = pl.ds(s * RNN_BLOCKS + k, TT, stride=TMJ_ROWS)
            xr_ref[rows, :] = zs[:, k * LANES:(k + 1) * LANES]
            gg_ref[rows, :] = gg[:, k * LANES:(k + 1) * LANES]
        u = zs[:, 2 * D_RNN:2 * D_RNN + D_SGU]
        vn = (_rms(zs[:, 2 * D_RNN + D_SGU:]) * sgug_ref[...]).astype(BF16)
        for c in range(TT // CHUNK):
            rows = slice(c * CHUNK, (c + 1) * CHUNK)
            halves = []
            for hf in range(2):
                r = jnp.dot(sguw_ref[hf], vn[rows, hf * half:(hf + 1) * half],
                            preferred_element_type=F32)
                sel = jnp.zeros((CHUNK, half), F32)
                for h in range(heads_per_half):
                    sel = jnp.where(lane_head == h, r[h * CHUNK:(h + 1) * CHUNK], sel)
                halves.append(sel)
            gatev = jnp.concatenate(halves, axis=1) + sgub_ref[...]
            ys_ref[s, rows, :] = (u[rows] * gatev).astype(BF16)


def _premix(x, mod3, cond_block, g_pre, w_in_b, sgu_g, sgu_w_b, sgu_bias_tile, pos_tab):
    n_seq, seq_len, _ = x.shape
    n_pairs, n_tiles = n_seq // PAIR, seq_len // TT
    add_pos = pos_tab is not None
    const2 = lambda p, j: (0, 0)
    in_specs = [pl.BlockSpec((PAIR, TT, D_MODEL), lambda p, j: (p, j, 0))]
    args = [x]
    if add_pos:
        reps = TT // GRID_W
        in_specs += [pl.BlockSpec((None, reps, D_MODEL // 2), lambda p, j: (j, 0, 0)),
                     pl.BlockSpec((GRID_W, D_MODEL // 2), const2)]
        args += [pos_tab.reshape(GRID_W // reps, reps, D_MODEL // 2), pos_tab]
    in_specs += [pl.BlockSpec((PAIR, 1, 6 * D_MODEL), lambda p, j: (cond_block(p), 0, 0)),
                 pl.BlockSpec((1, D_MODEL), const2),
                 pl.BlockSpec((D_MODEL, 2 * D_RNN + 2 * D_SGU), const2),
                 pl.BlockSpec((1, D_SGU), const2),
                 pl.BlockSpec((2, 4 * CHUNK, CHUNK), lambda p, j: (0, 0, 0)),
                 pl.BlockSpec((CHUNK, D_SGU), const2)]
    args += [mod3, g_pre, w_in_b, sgu_g, sgu_w_b, sgu_bias_tile]
    tmj = jax.ShapeDtypeStruct((n_pairs * seq_len * TMJ_ROWS, LANES), F32)
    tmj_spec = pl.BlockSpec((TT * TMJ_ROWS, LANES), lambda p, j: (p * n_tiles + j, 0))
    xr, gg, y_sgu = pl.pallas_call(
        functools.partial(_premix_kernel, add_pos=add_pos),
        out_shape=(tmj, tmj, jax.ShapeDtypeStruct((n_seq, seq_len, D_SGU), BF16)),
        grid=(n_pairs, n_tiles),
        in_specs=in_specs,
        out_specs=(tmj_spec, tmj_spec, pl.BlockSpec((PAIR, TT, D_SGU), lambda p, j: (p, j, 0))),
        compiler_params=_params(("parallel", "parallel")),
        name="premix",
    )(*args)
    shape4 = (n_pairs, seq_len, TMJ_ROWS, LANES)
    return xr.reshape(shape4), gg.reshape(shape4), y_sgu


def _scan_kernel(*refs, reverse, n_chunks):
    if reverse:
        (xprev_ref, x_ref, xnext_ref, gg_ref, hf_ref, h0_ref, cw_ref, cb_ref, wg_ref, bg_ref, lam_ref,
         y_ref, fs_ref, xwin, xc_s, r_s, i_s, a_s, b_s, y_s, hcar) = refs
    else:
        (xprev_ref, x_ref, xnext_ref, h0_ref, cw_ref, cb_ref, wg_ref, bg_ref, lam_ref,
         hf_ref, fs_ref, xwin, xc_s, r_s, i_s, a_s, b_s, hcar) = refs
    c = pl.program_id(1)
    chunk = n_chunks - 1 - c if reverse else c
    sub_rows = TS * TMJ_ROWS

    def rows_of(pb, t0, n_steps):
        first = (pb * LC + t0) * TMJ_ROWS
        if not isinstance(first, int):
            first = pl.multiple_of(first, TMJ_ROWS)
        return pl.ds(first, n_steps * TMJ_ROWS)

    @pl.when(c == 0)
    def _():
        hcar[...] = h0_ref[...]

    xwin[:, 0:CONV_LEFT] = jnp.where(chunk > 0, xprev_ref[...], 0.0)
    xwin[:, LC + CONV_LEFT:LC + CONV_W - 1] = jnp.where(chunk < n_chunks - 1, xnext_ref[...], 0.0)

    xwin[:, CONV_LEFT:CONV_LEFT + LC] = x_ref[...]

    neg_lam = -lam_ref[...]
    softplus = jnp.maximum(neg_lam, 0.0) + jnp.log(1.0 + jnp.exp(-jnp.abs(neg_lam)))
    half_decay = (-0.5 * RG_C * math.log2(math.e)) * softplus

    def conv(pb):
        for t0 in range(0, LC, TS):
            xc = cb_ref[...] + cw_ref[0] * xwin[pb, t0:t0 + TS]
            for k in range(1, CONV_W):
                xc = xc + cw_ref[k] * xwin[pb, t0 + k:t0 + k + TS]
            xc_s[rows_of(pb, t0, TS), :] = xc.reshape(sub_rows, LANES)

    def gate_matmuls(pb):
        for k in range(RNN_BLOCKS):
            rows = pl.ds(pb * LC * TMJ_ROWS + k, LC * PAIR, stride=RNN_BLOCKS)
            g = jnp.dot(xc_s[rows, :].astype(BF16), wg_ref[k], preferred_element_type=F32)
            r_s[rows, :] = g[:, :LANES]
            i_s[rows, :] = g[:, LANES:]

    def gates(pb):
        for t0 in range(0, LC, TS):
            rows = rows_of(pb, t0, TS)
            tile = lambda ref: ref[rows, :].reshape(TS, TMJ_ROWS, LANES)
            tr = jnp.tanh(tile(r_s) + bg_ref[0])
            ti = jnp.tanh(tile(i_s) + bg_ref[1])
            log2_a = tr * half_decay + half_decay
            a = jnp.exp2(log2_a)
            q = jnp.tanh(log2_a * (-math.log(2.0))) * (a * a + 1.0)
            b = jnp.where(q > 0.0, q * lax.rsqrt(q), 0.0) * ((ti + 1.0) * tile(xc_s))
            a_s[rows, :] = a.reshape(sub_rows, LANES)
            b_s[rows, :] = b.reshape(sub_rows, LANES)

    for stage in (conv, gate_matmuls, gates):
        for pb in range(PB):
            stage(pb)

    def step(j, hs):
        t = LC - 1 - j if reverse else j
        out = []
        for pb in range(PB):
            rows = rows_of(pb, t, 1)
            h = a_s[rows, :] * hs[pb] + b_s[rows, :]
            if reverse:
                y_s[rows, :] = (hf_ref[pb, t] + h) * gg_ref[pb, t]
            else:
                hf_ref[pb, t] = h
            out.append(h)
        return tuple(out)

    hs = lax.fori_loop(0, LC, step, tuple(hcar[pb] for pb in range(PB)), unroll=8)
    for pb in range(PB):
        hcar[pb] = hs[pb]
        fs_ref[pb] = hs[pb]

    if reverse:
        for pb in range(PB):
            for s in range(PAIR):
                cols = [y_s[pl.ds(pb * LC * TMJ_ROWS + s * RNN_BLOCKS + k, LC, stride=TMJ_ROWS), :]
                        for k in range(RNN_BLOCKS)]
                y_ref[pb * PAIR + s] = jnp.concatenate(cols, axis=1).astype(BF16)


def _scan(xr, gg, hf, h0, conv_w, conv_b, w_gates, b_gates, lam, direction):
    n_pairs, seq_len = xr.shape[:2]
    n_chunks = seq_len // LC
    reverse = direction == 1
    pos = (lambda c: n_chunks - 1 - c) if reverse else (lambda c: c)
    tmj_blk = pl.BlockSpec((PB, LC, TMJ_ROWS, LANES), lambda i, c: (i, pos(c), 0, 0))
    state_blk = pl.BlockSpec((PB, TMJ_ROWS, LANES), lambda i, c: (i, 0, 0))
    per_dir = lambda *shape: pl.BlockSpec((None,) + shape, lambda i, c: (direction,) + (0,) * len(shape))
    in_specs = [
        pl.BlockSpec((PB, CONV_LEFT, TMJ_ROWS, LANES),
                     lambda i, c: (i, jnp.maximum(pos(c) * (LC // CONV_LEFT) - 1, 0), 0, 0)),
        tmj_blk,
        pl.BlockSpec((PB, 1, TMJ_ROWS, LANES), lambda i, c: (i, jnp.minimum((pos(c) + 1) * LC, seq_len - 1), 0, 0)),
    ]
    args = [xr, xr, xr]
    if reverse:
        in_specs += [tmj_blk, tmj_blk]
        args += [gg, hf]
    in_specs += [state_blk,
                 pl.BlockSpec((CONV_W, TMJ_ROWS, LANES), lambda i, c: (0, 0, 0)),
                 pl.BlockSpec((TMJ_ROWS, LANES), lambda i, c: (0, 0)),
                 per_dir(RNN_BLOCKS, LANES, 2 * LANES),
                 per_dir(2, TMJ_ROWS, LANES),
                 per_dir(TMJ_ROWS, LANES)]
    args += [h0, conv_w, conv_b, w_gates, b_gates, lam]
    flat = pltpu.VMEM((PB * LC * TMJ_ROWS, LANES), F32)
    scratch = [pltpu.VMEM((PB, LC + CONV_W - 1, TMJ_ROWS, LANES), F32)] + [flat] * (6 if reverse else 5)
    scratch += [pltpu.VMEM((PB, TMJ_ROWS, LANES), F32)]
    state = jax.ShapeDtypeStruct((n_pairs, TMJ_ROWS, LANES), F32)
    if reverse:
        out_shape = (jax.ShapeDtypeStruct((n_pairs * PAIR, seq_len, D_RNN), BF16), state)
        out_specs = (pl.BlockSpec((PB * PAIR, LC, D_RNN), lambda i, c: (i, pos(c), 0)), state_blk)
    else:
        out_shape = (jax.ShapeDtypeStruct(xr.shape, F32), state)
        out_specs = (tmj_blk, state_blk)
    return pl.pallas_call(
        functools.partial(_scan_kernel, reverse=reverse, n_chunks=n_chunks),
        out_shape=out_shape,
        grid=(n_pairs // PB, n_chunks),
        in_specs=in_specs,
        out_specs=out_specs,
        scratch_shapes=scratch,
        compiler_params=_params(("parallel", "arbitrary")),
        name="scan_bwd" if reverse else "scan_fwd",
    )(*args)


def _route(lt):
    n = lt.shape[1]
    row = lax.broadcasted_iota(jnp.int32, (EXPERTS_PER_GROUP, n), 0)
    neg = jnp.float32(-jnp.inf)

    def arg_max(v):
        m = jnp.max(v, axis=0, keepdims=True)
        return jnp.min(jnp.where(v == m, row, EXPERTS_PER_GROUP), axis=0, keepdims=True)

    g_idx = arg_max(lt[0:N_GROUPS])
    el = lt[E_ROW0:E_ROW0 + EXPERTS_PER_GROUP]
    for g in range(1, N_GROUPS):
        first = E_ROW0 + g * EXPERTS_PER_GROUP
        el = jnp.where(g_idx == g, lt[first:first + EXPERTS_PER_GROUP], el)
    i1 = arg_max(el)
    i2 = arg_max(jnp.where(row == i1, neg, el))
    ja = jnp.minimum(i1, i2)
    jb = jnp.maximum(i1, i2)
    pair = (ja * (2 * EXPERTS_PER_GROUP - 1 - ja)) // 2 + (jb - ja - 1)
    return g_idx * PAIRS_PER_GROUP + pair


def _store_token_major(ref, x, t0=0):
    n = x.shape[0]
    for k in range(ROW_TILES):
        ref[pl.ds(t0 * ROW_TILES + k, n, stride=ROW_TILES), :] = x[:, k * LANES:(k + 1) * LANES]


def _load_token_major(ref, n):
    return jnp.concatenate([ref[pl.ds(k, n, stride=ROW_TILES), :] for k in range(ROW_TILES)], axis=1)


def _postmix_kernel(*refs, add_pos):
    refs = list(refs)
    x_ref = refs.pop(0)
    pos_refs = (refs.pop(0), refs.pop(0)) if add_pos else None
    (yr_ref, ys_ref, mod_ref, gpost_ref, gpre_ref, wout_ref, rw_ref, rb_ref, earlier_ref, cnt0_ref,
     x1_ref, hn_ref, rt_ref, cnt_ref, run_ref) = refs

    @pl.when(pl.program_id(0) == 0)
    def _():
        run_ref[...] = cnt0_ref[...]

    gate1 = mod_ref[0, :, 2 * D_MODEL:3 * D_MODEL]
    shift2 = mod_ref[0, :, 3 * D_MODEL:4 * D_MODEL]
    scale2 = mod_ref[0, :, 4 * D_MODEL:5 * D_MODEL]
    y = (jnp.dot(yr_ref[...], wout_ref[0:D_RNN, :], preferred_element_type=F32)
         + jnp.dot(ys_ref[...], wout_ref[D_RNN:, :], preferred_element_type=F32))
    x1 = _load_x(x_ref, pos_refs, 0, TM) + _rms(y) * (gate1 * gpost_ref[...])
    x1_ref[...] = x1
    hn = _rms(x1) * (gpre_ref[...] * (1.0 + scale2)) + shift2
    _store_token_major(hn_ref, hn)
    lt = lax.dot_general(rw_ref[...], hn.astype(BF16), (((1,), (1,)), ((), ())),
                         preferred_element_type=F32) + rb_ref[:, 0:1]
    bucket = _route(lt)
    onehot = lax.broadcasted_iota(jnp.int32, (ROUTER_ROWS, TM), 0) == bucket
    before = jnp.dot(onehot.astype(BF16), earlier_ref[...], preferred_element_type=F32) + run_ref[:, 0:1]
    rank = jnp.sum(jnp.where(onehot, before, 0.0), axis=0, keepdims=True).astype(jnp.int32)
    row = lax.broadcasted_iota(jnp.int32, (SUBLANES, TM), 0)
    rt_ref[...] = jnp.where(row == 0, bucket, jnp.where(row == 1, rank, 0))
    run_ref[...] += jnp.sum(onehot.astype(F32), axis=1, keepdims=True)
    cnt_ref[...] = run_ref[...]


def _postmix(x, y_rnn, y_sgu, mod3, cond_of_tile, g_post, g_pre, w_out_b, router_wt, router_bt, earlier, pos_tab,
             counts0):
    n_tok = x.shape[0]
    add_pos = pos_tab is not None
    tok = lambda i: (i, 0)
    const2 = lambda i: (0, 0)
    in_specs = [pl.BlockSpec((TM, D_MODEL), tok)]
    args = [x]
    if add_pos:
        reps = TM // GRID_W
        tiles_per_seq = GRID_W // reps
        in_specs += [pl.BlockSpec((None, reps, D_MODEL // 2), lambda i: (i % tiles_per_seq, 0, 0)),
                     pl.BlockSpec((GRID_W, D_MODEL // 2), const2)]
        args += [pos_tab.reshape(tiles_per_seq, reps, D_MODEL // 2), pos_tab]
    in_specs += [pl.BlockSpec((TM, D_RNN), tok),
                 pl.BlockSpec((TM, D_SGU), tok),
                 pl.BlockSpec((1, 1, 6 * D_MODEL), lambda i: (cond_of_tile(i), 0, 0)),
                 pl.BlockSpec((1, D_MODEL), const2),
                 pl.BlockSpec((1, D_MODEL), const2),
                 pl.BlockSpec((D_MODEL, D_MODEL), const2),
                 pl.BlockSpec((ROUTER_ROWS, D_MODEL), const2),
                 pl.BlockSpec((ROUTER_ROWS, LANES), const2),
                 pl.BlockSpec((TM, TM), const2),
                 pl.BlockSpec((ROUTER_ROWS, LANES), const2)]
    args += [y_rnn, y_sgu, mod3, g_post, g_pre, w_out_b, router_wt, router_bt, earlier, counts0]
    n_tiles = n_tok // TM
    counts_spec = pl.BlockSpec((ROUTER_ROWS, LANES), const2)
    x1, hn, route, counts = pl.pallas_call(
        functools.partial(_postmix_kernel, add_pos=add_pos),
        out_shape=(jax.ShapeDtypeStruct((n_tok, D_MODEL), F32),
                   jax.ShapeDtypeStruct((n_tok * ROW_TILES, LANES), F32),
                   jax.ShapeDtypeStruct((n_tiles * SUBLANES, TM), jnp.int32),
                   jax.ShapeDtypeStruct((ROUTER_ROWS, LANES), F32)),
        grid=(n_tiles,),
        in_specs=in_specs,
        out_specs=(pl.BlockSpec((TM, D_MODEL), tok),
                   pl.BlockSpec((TM * ROW_TILES, LANES), tok),
                   pl.BlockSpec((SUBLANES, TM), tok),
                   counts_spec),
        scratch_shapes=[pltpu.VMEM((ROUTER_ROWS, LANES), F32)],
        compiler_params=_params(("arbitrary",)),
        name="postmix",
    )(*args)
    route = route.reshape(n_tiles, SUBLANES, TM)
    return x1, hn, route[:, 0].reshape(n_tok), route[:, 1].reshape(n_tok), counts


def _token_rows(ref, t):
    return ref.at[pl.ds(pl.multiple_of(t * ROW_TILES, ROW_TILES), ROW_TILES), :]


def _dispatch_kernel(dest_ref, hc_ref, hs_ref, xs_ref, sem, *, n_ctx_steps):
    i = pl.program_id(0)
    base = i * TD

    def scatter(src_ref):
        def start(g, carry):
            for u in range(DMA_UNROLL):
                r = g * DMA_UNROLL + u
                pltpu.make_async_copy(_token_rows(src_ref, r), _token_rows(xs_ref, dest_ref[base + r]),
                                      sem).start(priority=u % 2)
            return carry

        lax.fori_loop(0, TD // DMA_UNROLL, start, 0)
        pltpu.make_async_copy(src_ref, xs_ref.at[pl.ds(0, TD * ROW_TILES), :], sem).wait()

    @pl.when(i < n_ctx_steps)
    def _():
        scatter(hc_ref)

    @pl.when(i >= n_ctx_steps)
    def _():
        scatter(hs_ref)


def _dispatch(dest, hn_ctx, hn_dec, n_slots):
    n_ctx_steps = hn_ctx.shape[0] // (TD * ROW_TILES)
    n_dec_steps = hn_dec.shape[0] // (TD * ROW_TILES)
    return pl.pallas_call(
        functools.partial(_dispatch_kernel, n_ctx_steps=n_ctx_steps),
        out_shape=jax.ShapeDtypeStruct((n_slots * ROW_TILES, LANES), F32),
        grid_spec=pltpu.PrefetchScalarGridSpec(
            num_scalar_prefetch=1,
            grid=(n_ctx_steps + n_dec_steps,),
            in_specs=[pl.BlockSpec((TD * ROW_TILES, LANES), lambda i, d: (jnp.minimum(i, n_ctx_steps - 1), 0)),
                      pl.BlockSpec((TD * ROW_TILES, LANES), lambda i, d: (jnp.maximum(i - n_ctx_steps, 0), 0))],
            out_specs=pl.BlockSpec(memory_space=pl.ANY),
            scratch_shapes=[pltpu.SemaphoreType.DMA(())]),
        compiler_params=_params(("arbitrary",)),
        name="dispatch",
    )(dest, hn_ctx, hn_dec)


def _experts_kernel(ea_ref, eb_ref, nv_ref, xs_ref, rw_ref, rb_ref,
                    wga_ref, wua_ref, wda_ref, wgb_ref, wub_ref, wdb_ref, gpost_ref, ys_ref):
    i = pl.program_id(0)
    nv = nv_ref[i]

    @pl.when(nv == 0)
    def _():
        ys_ref[...] = jnp.zeros_like(ys_ref)

    @pl.when(nv > 0)
    def _():
        row = lax.broadcasted_iota(jnp.int32, (TMX, 1), 0)
        xb = jnp.where(row < nv, _load_token_major(xs_ref, TMX), 0.0).astype(BF16)
        logits = jnp.dot(xb, rw_ref[...], preferred_element_type=F32) + rb_ref[...]
        lane = lax.broadcasted_iota(jnp.int32, logits.shape, 1)
        ea = ea_ref[i]
        eb = eb_ref[i]
        gmask = lane < N_GROUPS
        gl = jnp.where(gmask, logits, -jnp.inf)
        gmax = jnp.max(gl, axis=-1, keepdims=True)
        gexp = jnp.where(gmask, jnp.exp(gl - gmax), 0.0)
        g_own = jnp.sum(jnp.where(lane == ea // EXPERTS_PER_GROUP, gexp, 0.0), axis=-1, keepdims=True)
        g_w = g_own / jnp.sum(gexp, axis=-1, keepdims=True)
        la = jnp.sum(jnp.where(lane == ea + E_LANE0, logits, 0.0), axis=-1, keepdims=True)
        lb = jnp.sum(jnp.where(lane == eb + E_LANE0, logits, 0.0), axis=-1, keepdims=True)
        m = jnp.maximum(la, lb)
        pa = jnp.exp(la - m)
        pb = jnp.exp(lb - m)
        inv = g_w / (pa + pb)

        def hidden(x, wg_ref, wu_ref, w):
            g = jnp.dot(x, wg_ref[0], preferred_element_type=F32)
            u = jnp.dot(x, wu_ref[0], preferred_element_type=F32)
            return ((g * _sigmoid(g)) * u * w).astype(BF16)

        part = TMX // EXPERT_ROW_PARTS
        ys = []
        for h in range(EXPERT_ROW_PARTS):
            rows = slice(h * part, (h + 1) * part)
            act_a = hidden(xb[rows], wga_ref, wua_ref, (pa * inv)[rows])
            act_b = hidden(xb[rows], wgb_ref, wub_ref, (pb * inv)[rows])
            y = (jnp.dot(act_a, wda_ref[0], preferred_element_type=F32)
                 + jnp.dot(act_b, wdb_ref[0], preferred_element_type=F32))
            ys.append(_rms(y) * gpost_ref[...])
        _store_token_major(ys_ref, jnp.concatenate(ys, axis=0))


def _experts(sched, xs, router_w, router_b, wg_b, wu_b, wd_b, g_post):
    ea, eb, nv = sched
    n_tiles = ea.shape[0]
    rows = lambda i, ea, eb, nv: (i, 0)
    const2 = lambda i, ea, eb, nv: (0, 0)
    exp_a = lambda i, ea, eb, nv: (ea[i], 0, 0)
    exp_b = lambda i, ea, eb, nv: (eb[i], 0, 0)
    w_in_spec = lambda m: pl.BlockSpec((1, D_MODEL, D_EXPERT), m)
    w_out_spec = lambda m: pl.BlockSpec((1, D_EXPERT, D_MODEL), m)
    return pl.pallas_call(
        _experts_kernel,
        out_shape=jax.ShapeDtypeStruct(xs.shape, F32),
        grid_spec=pltpu.PrefetchScalarGridSpec(
            num_scalar_prefetch=3,
            grid=(n_tiles,),
            in_specs=[pl.BlockSpec((TMX * ROW_TILES, LANES), rows),
                      pl.BlockSpec((D_MODEL, ROUTER_LANES), const2),
                      pl.BlockSpec((1, ROUTER_LANES), const2),
                      w_in_spec(exp_a), w_in_spec(exp_a), w_out_spec(exp_a),
                      w_in_spec(exp_b), w_in_spec(exp_b), w_out_spec(exp_b),
                      pl.BlockSpec((1, D_MODEL), const2)],
            out_specs=pl.BlockSpec((TMX * ROW_TILES, LANES), rows)),
        compiler_params=_params(("arbitrary",)),
        name="experts",
    )(ea, eb, nv, xs, router_w, router_b, wg_b, wu_b, wd_b, wg_b, wu_b, wd_b, g_post)


def _combine_kernel(dest_ref, ys_ref, x1_ref, mod_ref, o_ref, ybuf, sems, *, tile0):
    i = pl.program_id(0)
    n = pl.num_programs(0)

    def fetch(tile, slot):
        base = (tile + tile0) * TM

        def start(g, carry):
            for u in range(DMA_UNROLL):
                r = g * DMA_UNROLL + u
                pltpu.make_async_copy(_token_rows(ys_ref, dest_ref[base + r]), _token_rows(ybuf.at[slot], r),
                                      sems.at[slot]).start(priority=u % 2)
            return carry

        lax.fori_loop(0, TM // DMA_UNROLL, start, 0)

    @pl.when(i == 0)
    def _():
        fetch(0, 0)

    @pl.when(i + 1 < n)
    def _():
        fetch(i + 1, (i + 1) % 2)

    slot = i % 2
    pltpu.make_async_copy(ys_ref.at[pl.ds(0, TM * ROW_TILES), :], ybuf.at[slot], sems.at[slot]).wait()
    gate2 = mod_ref[0, :, 5 * D_MODEL:6 * D_MODEL]
    o_ref[...] = x1_ref[...] + gate2 * _load_token_major(ybuf.at[slot], TM)


def _combine(dest, ys, x1, mod3, cond_of_tile, tile0):
    n_tok = x1.shape[0]
    return pl.pallas_call(
        functools.partial(_combine_kernel, tile0=tile0),
        out_shape=jax.ShapeDtypeStruct((n_tok, D_MODEL), F32),
        grid_spec=pltpu.PrefetchScalarGridSpec(
            num_scalar_prefetch=1,
            grid=(n_tok // TM,),
            in_specs=[pl.BlockSpec(memory_space=pl.ANY),
                      pl.BlockSpec((TM, D_MODEL), lambda i, d: (i, 0)),
                      pl.BlockSpec((1, 1, 6 * D_MODEL), lambda i, d: (cond_of_tile(i), 0, 0))],
            out_specs=pl.BlockSpec((TM, D_MODEL), lambda i, d: (i, 0)),
            scratch_shapes=[pltpu.VMEM((2, TM * ROW_TILES, LANES), F32), pltpu.SemaphoreType.DMA((2,))]),
        compiler_params=_params(("arbitrary",)),
        name="combine",
    )(dest, ys, x1, mod3)


def _schedule(bucket, rank, counts):
    n_tok = bucket.shape[0]
    n_max = n_tok // TMX + N_BUCKETS
    cnt = counts[:N_BUCKETS, 0].astype(jnp.int32)
    tiles = (cnt + TMX - 1) // TMX
    tile_end = jnp.cumsum(tiles)
    tile_start = tile_end - tiles
    ids = jnp.arange(N_BUCKETS, dtype=jnp.int32)
    slot0 = jnp.sum(jnp.where(bucket[:, None] == ids[None, :], (tile_start * TMX)[None, :], 0), axis=1)
    dest = slot0 + rank
    i = jnp.arange(n_max, dtype=jnp.int32)
    total = tile_end[-1]
    valid = i < total
    tb = jnp.sum((jnp.minimum(i, total - 1)[:, None] >= tile_end[None, :]).astype(jnp.int32), axis=1)
    pairs = [(a, b) for a in range(EXPERTS_PER_GROUP) for b in range(a + 1, EXPERTS_PER_GROUP)]
    ea_tab = jnp.array([g * EXPERTS_PER_GROUP + a for g in range(N_GROUPS) for a, _ in pairs], jnp.int32)
    eb_tab = jnp.array([g * EXPERTS_PER_GROUP + b for g in range(N_GROUPS) for _, b in pairs], jnp.int32)
    ea = ea_tab[tb]
    eb = eb_tab[tb]
    nv = jnp.where(valid, jnp.clip(cnt[tb] - (i - tile_start[tb]) * TMX, 0, TMX), 0)
    return dest, (ea, eb, nv), n_max * TMX


def _block_diag_gates(rg_wa, rg_wx):
    heads = LANES // HEAD_RNN

    def bd(w):
        w = w.reshape(2, RNN_BLOCKS, heads, HEAD_RNN, HEAD_RNN)
        eye = jnp.eye(heads, dtype=w.dtype)
        full = jnp.einsum('dghij,hk->dghikj', w, eye)
        return full.reshape(2, RNN_BLOCKS, LANES, LANES)

    return jnp.concatenate([bd(rg_wa), bd(rg_wx)], axis=-1).astype(BF16)


def _row_tile(v):
    blocks = v.reshape(v.shape[:-1] + (RNN_BLOCKS, LANES))
    return jnp.concatenate([blocks] * PAIR, axis=-2)


def _to_time_major_state(h):
    return h.reshape(h.shape[0] // PAIR, TMJ_ROWS, LANES)


def kernel(x_prompt, x_sample, state_rglru, c, c_ctx, w_mod, b_mod, g_pre_mix, g_post_mix, g_pre_ffn,
           g_post_ffn, w_in, conv_w, conv_b, rg_wa, rg_ba, rg_wx, rg_bx, rg_lambda, sgu_g, sgu_w, sgu_b,
           w_out, router_g_w, router_g_b, router_e_w, router_e_b, exp_w_gate, exp_w_up, exp_w_down):
    assert w_mod.shape[0] == 1, "single-layer trunk"
    n_ctx, ctx_len, _ = x_prompt.shape
    n_dec, dec_len, _ = x_sample.shape
    l = 0

    n_cond = SUBLANES
    assert n_dec % PAIR == 0 and n_dec + PAIR <= n_cond
    cond = jnp.zeros((n_cond, D_MODEL), F32).at[:n_dec].set(c).at[n_dec:n_dec + PAIR].set(c_ctx)
    mod3 = _modulation(cond, w_mod[l], b_mod[l]).reshape(n_cond, 1, 6 * D_MODEL)
    pos_tab = _pos_table()

    w_in_b = w_in[l].astype(BF16)
    w_out_b = w_out[l].astype(BF16)
    sgu_w_b = sgu_w[l].reshape(2, 4 * CHUNK, CHUNK).astype(BF16)
    sgu_bias_tile = jnp.repeat(sgu_b[l].T, HEAD_SGU, axis=1)
    w_gates = _block_diag_gates(rg_wa[l], rg_wx[l])
    b_gates = 0.5 * jnp.stack([_row_tile(rg_ba[l]), _row_tile(rg_bx[l])], axis=1)
    lam = _row_tile(rg_lambda[l])
    conv_w_t = 0.5 * _row_tile(conv_w[l])
    conv_b_t = 0.5 * _row_tile(conv_b[l])
    router_w = jnp.zeros((D_MODEL, ROUTER_LANES), F32)
    router_w = router_w.at[:, :N_GROUPS].set(router_g_w[l]).at[:, E_LANE0:E_LANE0 + N_EXPERTS].set(router_e_w[l])
    router_w = router_w.astype(BF16)
    router_b = jnp.zeros((1, ROUTER_LANES), F32)
    router_b = router_b.at[0, :N_GROUPS].set(router_g_b[l]).at[0, E_LANE0:E_LANE0 + N_EXPERTS].set(router_e_b[l])
    router_wt = jnp.zeros((ROUTER_ROWS, D_MODEL), F32)
    router_wt = router_wt.at[:N_GROUPS].set(router_g_w[l].T).at[E_ROW0:E_ROW0 + N_EXPERTS].set(router_e_w[l].T)
    router_wt = router_wt.astype(BF16)
    router_bt = jnp.zeros((ROUTER_ROWS,), F32)
    router_bt = router_bt.at[:N_GROUPS].set(router_g_b[l]).at[E_ROW0:E_ROW0 + N_EXPERTS].set(router_e_b[l])
    router_bt = jnp.broadcast_to(router_bt[:, None], (ROUTER_ROWS, LANES))
    earlier = jnp.triu(jnp.ones((TM, TM), BF16), k=1)
    wg_b = exp_w_gate[l].astype(BF16)
    wu_b = exp_w_up[l].astype(BF16)
    wd_b = exp_w_down[l].astype(BF16)
    row = lambda v: v.reshape(1, -1)

    n_ctx_tok = n_ctx * ctx_len

    def mixer(x, h0, cond_of_tile, cond_block, use_pos, counts0):
        n_seq, seq_len, _ = x.shape
        xf = x.reshape(n_seq * seq_len, D_MODEL)
        tab = pos_tab if use_pos else None
        xr, gg, y_sgu = _premix(x, mod3, cond_block, row(g_pre_mix[l]), w_in_b, row(sgu_g[l]),
                                sgu_w_b, sgu_bias_tile, tab)
        scan_params = (conv_w_t, conv_b_t, w_gates, b_gates, lam)
        hf, hf_last = _scan(xr, None, None, _to_time_major_state(h0[:, 0]), *scan_params, direction=0)
        y_rnn, hb_first = _scan(xr, gg, hf, _to_time_major_state(h0[:, 1]), *scan_params, direction=1)
        fstate = jnp.stack([hf_last.reshape(n_seq, D_RNN), hb_first.reshape(n_seq, D_RNN)], axis=1)
        x1, hn, bucket, rank, counts = _postmix(
            xf, y_rnn.reshape(n_seq * seq_len, D_RNN), y_sgu.reshape(n_seq * seq_len, D_SGU), mod3,
            cond_of_tile, row(g_post_mix[l]), row(g_pre_ffn[l]), w_out_b, router_wt, router_bt, earlier, tab,
            counts0)
        return x1, hn, bucket, rank, counts, fstate

    tiles_per_seq = dec_len // TM
    ctx_cond = lambda i: n_dec
    dec_cond = lambda i: i // tiles_per_seq
    h0_ctx = jnp.zeros((n_ctx, 2, D_RNN), F32)
    counts0 = jnp.zeros((ROUTER_ROWS, LANES), F32)
    x1_ctx, hn_ctx, bucket_ctx, rank_ctx, counts, st = mixer(x_prompt, h0_ctx, ctx_cond, lambda p: n_dec // PAIR,
                                                             False, counts0)
    new_state = st.astype(state_rglru.dtype)[:, None]
    x1_dec, hn_dec, bucket_dec, rank_dec, counts, _ = mixer(x_sample, state_rglru[:, l].astype(F32), dec_cond,
                                                            lambda p: p, True, counts)

    dest, sched, n_slots = _schedule(jnp.concatenate([bucket_ctx, bucket_dec]),
                                     jnp.concatenate([rank_ctx, rank_dec]), counts)
    xs = _dispatch(dest, hn_ctx, hn_dec, n_slots)
    ys = _experts(sched, xs, router_w, router_b, wg_b, wu_b, wd_b, row(g_post_ffn[l]))
    y_prompt = _combine(dest, ys, x1_ctx, mod3, ctx_cond, 0)
    y_sample = _combine(dest, ys, x1_dec, mod3, dec_cond, n_ctx_tok // TM)
    return (y_prompt.reshape(x_prompt.shape), y_sample.reshape(x_sample.shape), new_state)
```

```python
import functools
import math

import jax
import jax.numpy as jnp
from jax import lax
from jax.experimental import pallas as pl
from jax.experimental.pallas import tpu as pltpu

D_MODEL = 1024
D_RNN = 512
D_SGU = 512
N_HEADS_RNN = 8
HEAD_RNN = D_RNN // N_HEADS_RNN
N_HEADS_SGU = 8
HEAD_SGU = D_SGU // N_HEADS_SGU
CHUNK = 128
GRID_W = 64
RG_C = 8.0
N_GROUPS = 4
EXPERTS_PER_GROUP = 4
N_EXPERTS = N_GROUPS * EXPERTS_PER_GROUP
D_EXPERT = 512
EPS = 1e-6
POS_BASE = 10000.0

LANES = 128
SUBLANES = 8
CONV_W = 4
CONV_LEFT = 2
PAIR = 2
RNN_BLOCKS = D_RNN // LANES
TMJ_ROWS = PAIR * RNN_BLOCKS
ROUTER_LANES = LANES
E_LANE0 = N_GROUPS
ROUTER_ROWS = 32
E_ROW0 = SUBLANES

PAIRS_PER_GROUP = EXPERTS_PER_GROUP * (EXPERTS_PER_GROUP - 1) // 2
N_BUCKETS = N_GROUPS * PAIRS_PER_GROUP

ROW_TILES = D_MODEL // LANES

TM = 512
TD = 1024
DMA_UNROLL = 8
EXPERT_ROW_PARTS = 2
TMX = 512
TT = TM // PAIR
LC = 256
TS = 16
PB = 2
VMEM_LIMIT = 56 * 1024 * 1024

F32 = jnp.float32
BF16 = jnp.bfloat16


def _params(sem):
    return pltpu.CompilerParams(dimension_semantics=sem, vmem_limit_bytes=VMEM_LIMIT)


def _rms(x):
    return x * lax.rsqrt(jnp.mean(x * x, axis=-1, keepdims=True) + EPS)


def _sigmoid(x):
    return 0.5 * jnp.tanh(0.5 * x) + 0.5


def _mod_kernel(cond_ref, w_ref, b_ref, o_ref):
    c = cond_ref[...]
    s = c * _sigmoid(c)
    o_ref[...] = jnp.dot(s.astype(BF16), w_ref[...].astype(BF16),
                         preferred_element_type=F32) + b_ref[...]


def _modulation(cond, w_mod, b_mod):
    n = w_mod.shape[1]
    bn = 1024
    return pl.pallas_call(
        _mod_kernel,
        out_shape=jax.ShapeDtypeStruct((cond.shape[0], n), F32),
        grid=(n // bn,),
        in_specs=[pl.BlockSpec(cond.shape, lambda j: (0, 0)),
                  pl.BlockSpec((D_MODEL, bn), lambda j: (0, j)),
                  pl.BlockSpec((1, bn), lambda j: (0, j))],
        out_specs=pl.BlockSpec((cond.shape[0], bn), lambda j: (0, j)),
        compiler_params=_params(("arbitrary",)),
        name="modulation",
    )(cond, w_mod, b_mod.reshape(1, n))


def _pos_kernel(o_ref):
    n_freq = D_MODEL // 4
    k = lax.broadcasted_iota(jnp.int32, (GRID_W, n_freq), 1).astype(F32)
    p = lax.broadcasted_iota(jnp.int32, (GRID_W, n_freq), 0).astype(F32)
    freq = jnp.exp(-math.log(POS_BASE) * k / n_freq)
    ang = p * freq
    o_ref[:, 0:n_freq] = jnp.sin(ang)
    o_ref[:, n_freq:2 * n_freq] = jnp.cos(ang)


def _pos_table():
    return pl.pallas_call(
        _pos_kernel,
        out_shape=jax.ShapeDtypeStruct((GRID_W, D_MODEL // 2), F32),
        name="pos_table",
    )()


def _add_pos(x, pos_refs, q0):
    if pos_refs is None:
        return x
    rows_ref, cols_ref = pos_refs
    reps = x.shape[0] // GRID_W
    rpart = jnp.concatenate(
        [jnp.broadcast_to(rows_ref[q:q + 1, :], (GRID_W, D_MODEL // 2)) for q in range(q0, q0 + reps)], axis=0)
    cpart = jnp.concatenate([cols_ref[...]] * reps, axis=0)
    return jnp.concatenate([x[:, :D_MODEL // 2] + rpart, x[:, D_MODEL // 2:] + cpart], axis=1)


def _load_x(x_ref, pos_refs, r0, n):
    return _add_pos(x_ref[r0:r0 + n, :], pos_refs, r0 // GRID_W)


def _premix_kernel(*refs, add_pos):
    refs = list(refs)
    x_ref = refs.pop(0)
    pos_refs = (refs.pop(0), refs.pop(0)) if add_pos else None
    mod_ref, g_ref, win_ref, sgug_ref, sguw_ref, sgub_ref, xr_ref, gg_ref, ys_ref = refs
    hn = []
    for s in range(PAIR):
        shift = mod_ref[s, :, 0:D_MODEL]
        scale = mod_ref[s, :, D_MODEL:2 * D_MODEL]
        hn.append(_rms(_add_pos(x_ref[s], pos_refs, 0)) * (g_ref[...] * (1.0 + scale)) + shift)
    z = jnp.dot(jnp.concatenate(hn, axis=0).astype(BF16), win_ref[...],
                preferred_element_type=F32)
    half = D_SGU // 2
    heads_per_half = N_HEADS_SGU // 2
    lane_head = lax.broadcasted_iota(jnp.int32, (CHUNK, half), 1) // HEAD_SGU
    for s in range(PAIR):
        zs = z[s * TT:(s + 1) * TT]
        gg = jax.nn.gelu(zs[:, D_RNN:2 * D_RNN])
        for k in range(RNN_BLOCKS):
            rows = pl.ds(s * RNN_BLOCKS + k, TT, stride=TMJ_ROWS)
            xr_ref[rows, :] = zs[:, k * LANES:(k + 1) * LANES]
            gg_ref[rows, :] = gg[:, k * LANES:(k + 1) * LANES]
        u = zs[:, 2 * D_RNN:2 * D_RNN + D_SGU]
        vn = (_rms(zs[:, 2 * D_RNN + D_SGU:]) * sgug_ref[...]).astype(BF16)
        for c in range(TT // CHUNK):
            rows = slice(c * CHUNK, (c + 1) * CHUNK)
            halves = []
            for hf in range(2):
                r = jnp.dot(sguw_ref[hf], vn[rows, hf * half:(hf + 1) * half],
                            preferred_element_type=F32)
                sel = jnp.zeros((CHUNK, half), F32)
                for h in range(heads_per_half):
                    sel = jnp.where(lane_head == h, r[h * CHUNK:(h + 1) * CHUNK], sel)
                halves.append(sel)
            gatev = jnp.concatenate(halves, axis=1) + sgub_ref[...]
            ys_ref[s, rows, :] = (u[rows] * gatev).astype(BF16)


def _premix(x, mod3, cond_block, g_pre, w_in_b, sgu_g, sgu_w_b, sgu_bias_tile, pos_tab):
    n_seq, seq_len, _ = x.shape
    n_pairs, n_tiles = n_seq // PAIR, seq_len // TT
    add_pos = pos_tab is not None
    const2 = lambda p, j: (0, 0)
    in_specs = [pl.BlockSpec((PAIR, TT, D_MODEL), lambda p, j: (p, j, 0))]
    args = [x]
    if add_pos:
        reps = TT // GRID_W
        in_specs += [pl.BlockSpec((None, reps, D_MODEL // 2), lambda p, j: (j, 0, 0)),
                     pl.BlockSpec((GRID_W, D_MODEL // 2), const2)]
        args += [pos_tab.reshape(GRID_W // reps, reps, D_MODEL // 2), pos_tab]
    in_specs += [pl.BlockSpec((PAIR, 1, 6 * D_MODEL), lambda p, j: (cond_block(p), 0, 0)),
                 pl.BlockSpec((1, D_MODEL), const2),
                 pl.BlockSpec((D_MODEL, 2 * D_RNN + 2 * D_SGU), const2),
                 pl.BlockSpec((1, D_SGU), const2),
                 pl.BlockSpec((2, 4 * CHUNK, CHUNK), lambda p, j: (0, 0, 0)),
                 pl.BlockSpec((CHUNK, D_SGU), const2)]
    args += [mod3, g_pre, w_in_b, sgu_g, sgu_w_b, sgu_bias_tile]
    tmj = jax.ShapeDtypeStruct((n_pairs * seq_len * TMJ_ROWS, LANES), F32)
    tmj_spec = pl.BlockSpec((TT * TMJ_ROWS, LANES), lambda p, j: (p * n_tiles + j, 0))
    xr, gg, y_sgu = pl.pallas_call(
        functools.partial(_premix_kernel, add_pos=add_pos),
        out_shape=(tmj, tmj, jax.ShapeDtypeStruct((n_seq, seq_len, D_SGU), BF16)),
        grid=(n_pairs, n_tiles),
        in_specs=in_specs,
        out_specs=(tmj_spec, tmj_spec, pl.BlockSpec((PAIR, TT, D_SGU), lambda p, j: (p, j, 0))),
        compiler_params=_params(("parallel", "parallel")),
        name="premix",
    )(*args)
    shape4 = (n_pairs, seq_len, TMJ_ROWS, LANES)
    return xr.reshape(shape4), gg.reshape(shape4), y_sgu


def _scan_kernel(*refs, reverse, n_chunks):
    if reverse:
        (xprev_ref, x_ref, xnext_ref, gg_ref, hf_ref, h0_ref, cw_ref, cb_ref, wg_ref, bg_ref, lam_ref,
         y_ref, fs_ref, xwin, xc_s, r_s, i_s, a_s, b_s, y_s, hcar) = refs
    else:
        (xprev_ref, x_ref, xnext_ref, h0_ref, cw_ref, cb_ref, wg_ref, bg_ref, lam_ref,
         hf_ref, fs_ref, xwin, xc_s, r_s, i_s, a_s, b_s, hcar) = refs
    c = pl.program_id(1)
    chunk = n_chunks - 1 - c if reverse else c
    sub_rows = TS * TMJ_ROWS

    def rows_of(pb, t0, n_steps):
        first = (pb * LC + t0) * TMJ_ROWS
        if not isinstance(first, int):
            first = pl.multiple_of(first, TMJ_ROWS)
        return pl.ds(first, n_steps * TMJ_ROWS)

    @pl.when(c == 0)
    def _():
        hcar[...] = h0_ref[...]

    xwin[:, 0:CONV_LEFT] = jnp.where(chunk > 0, xprev_ref[...], 0.0)
    xwin[:, LC + CONV_LEFT:LC + CONV_W - 1] = jnp.where(chunk < n_chunks - 1, xnext_ref[...], 0.0)

    xwin[:, CONV_LEFT:CONV_LEFT + LC] = x_ref[...]

    neg_lam = -lam_ref[...]
    softplus = jnp.maximum(neg_lam, 0.0) + jnp.log(1.0 + jnp.exp(-jnp.abs(neg_lam)))
    half_decay = (-0.5 * RG_C * math.log2(math.e)) * softplus

    def conv(pb):
        for t0 in range(0, LC, TS):
            xc = cb_ref[...] + cw_ref[0] * xwin[pb, t0:t0 + TS]
            for k in range(1, CONV_W):
                xc = xc + cw_ref[k] * xwin[pb, t0 + k:t0 + k + TS]
            xc_s[rows_of(pb, t0, TS), :] = xc.reshape(sub_rows, LANES)

    def gate_matmuls(pb):
        for k in range(RNN_BLOCKS):
            rows = pl.ds(pb * LC * TMJ_ROWS + k, LC * PAIR, stride=RNN_BLOCKS)
            g = jnp.dot(xc_s[rows, :].astype(BF16), wg_ref[k], preferred_element_type=F32)
            r_s[rows, :] = g[:, :LANES]
            i_s[rows, :] = g[:, LANES:]

    def gates(pb):
        for t0 in range(0, LC, TS):
            rows = rows_of(pb, t0, TS)
            tile = lambda ref: ref[rows, :].reshape(TS, TMJ_ROWS, LANES)
            tr = jnp.tanh(tile(r_s) + bg_ref[0])
            ti = jnp.tanh(tile(i_s) + bg_ref[1])
            log2_a = tr * half_decay + half_decay
            a = jnp.exp2(log2_a)
            q = jnp.tanh(log2_a * (-math.log(2.0))) * (a * a + 1.0)
            b = jnp.where(q > 0.0, q * lax.rsqrt(q), 0.0) * ((ti + 1.0) * tile(xc_s))
            a_s[rows, :] = a.reshape(sub_rows, LANES)
            b_s[rows, :] = b.reshape(sub_rows, LANES)

    for stage in (conv, gate_matmuls, gates):
        for pb in range(PB):
            stage(pb)

    def step(j, hs):
        t = LC - 1 - j if reverse else j
        out = []
        for pb in range(PB):
            rows = rows_of(pb, t, 1)
            h = a_s[rows, :] * hs[pb] + b_s[rows, :]
            if reverse:
                y_s[rows, :] = (hf_ref[pb, t] + h) * gg_ref[pb, t]
            else:
                hf_ref[pb, t] = h
            out.append(h)
        return tuple(out)

    hs = lax.fori_loop(0, LC, step, tuple(hcar[pb] for pb in range(PB)), unroll=8)
    for pb in range(PB):
        hcar[pb] = hs[pb]
        fs_ref[pb] = hs[pb]

    if reverse:
        for pb in range(PB):
            for s in range(PAIR):
                cols = [y_s[pl.ds(pb * LC * TMJ_ROWS + s * RNN_BLOCKS + k, LC, stride=TMJ_ROWS), :]
                        for k in range(RNN_BLOCKS)]
                y_ref[pb * PAIR + s] = jnp.concatenate(cols, axis=1).astype(BF16)


def _scan(xr, gg, hf, h0, conv_w, conv_b, w_gates, b_gates, lam, direction):
    n_pairs, seq_len = xr.shape[:2]
    n_chunks = seq_len // LC
    reverse = direction == 1
    pos = (lambda c: n_chunks - 1 - c) if reverse else (lambda c: c)
    tmj_blk = pl.BlockSpec((PB, LC, TMJ_ROWS, LANES), lambda i, c: (i, pos(c), 0, 0))
    state_blk = pl.BlockSpec((PB, TMJ_ROWS, LANES), lambda i, c: (i, 0, 0))
    per_dir = lambda *shape: pl.BlockSpec((None,) + shape, lambda i, c: (direction,) + (0,) * len(shape))
    in_specs = [
        pl.BlockSpec((PB, CONV_LEFT, TMJ_ROWS, LANES),
                     lambda i, c: (i, jnp.maximum(pos(c) * (LC // CONV_LEFT) - 1, 0), 0, 0)),
        tmj_blk,
        pl.BlockSpec((PB, 1, TMJ_ROWS, LANES), lambda i, c: (i, jnp.minimum((pos(c) + 1) * LC, seq_len - 1), 0, 0)),
    ]
    args = [xr, xr, xr]
    if reverse:
        in_specs += [tmj_blk, tmj_blk]
        args += [gg, hf]
    in_specs += [state_blk,
                 pl.BlockSpec((CONV_W, TMJ_ROWS, LANES), lambda i, c: (0, 0, 0)),
                 pl.BlockSpec((TMJ_ROWS, LANES), lambda i, c: (0, 0)),
                 per_dir(RNN_BLOCKS, LANES, 2 * LANES),
                 per_dir(2, TMJ_ROWS, LANES),
                 per_dir(TMJ_ROWS, LANES)]
    args += [h0, conv_w, conv_b, w_gates, b_gates, lam]
    flat = pltpu.VMEM((PB * LC * TMJ_ROWS, LANES), F32)
    scratch = [pltpu.VMEM((PB, LC + CONV_W - 1, TMJ_ROWS, LANES), F32)] + [flat] * (6 if reverse else 5)
    scratch += [pltpu.VMEM((PB, TMJ_ROWS, LANES), F32)]
    state = jax.ShapeDtypeStruct((n_pairs, TMJ_ROWS, LANES), F32)
    if reverse:
        out_shape = (jax.ShapeDtypeStruct((n_pairs * PAIR, seq_len, D_RNN), BF16), state)
        out_specs = (pl.BlockSpec((PB * PAIR, LC, D_RNN), lambda i, c: (i, pos(c), 0)), state_blk)
    else:
        out_shape = (jax.ShapeDtypeStruct(xr.shape, F32), state)
        out_specs = (tmj_blk, state_blk)
    return pl.pallas_call(
        functools.partial(_scan_kernel, reverse=reverse, n_chunks=n_chunks),
        out_shape=out_shape,
        grid=(n_pairs // PB, n_chunks),
        in_specs=in_specs,
        out_specs=out_specs,
        scratch_shapes=scratch,
        compiler_params=_params(("parallel", "arbitrary")),
        name="scan_bwd" if reverse else "scan_fwd",
    )(*args)


def _route(lt):
    n = lt.shape[1]
    row = lax.broadcasted_iota(jnp.int32, (EXPERTS_PER_GROUP, n), 0)
    neg = jnp.float32(-jnp.inf)

    def arg_max(v):
        m = jnp.max(v, axis=0, keepdims=True)
        return jnp.min(jnp.where(v == m, row, EXPERTS_PER_GROUP), axis=0, keepdims=True)

    g_idx = arg_max(lt[0:N_GROUPS])
    el = lt[E_ROW0:E_ROW0 + EXPERTS_PER_GROUP]
    for g in range(1, N_GROUPS):
        first = E_ROW0 + g * EXPERTS_PER_GROUP
        el = jnp.where(g_idx == g, lt[first:first + EXPERTS_PER_GROUP], el)
    i1 = arg_max(el)
    i2 = arg_max(jnp.where(row == i1, neg, el))
    ja = jnp.minimum(i1, i2)
    jb = jnp.maximum(i1, i2)
    pair = (ja * (2 * EXPERTS_PER_GROUP - 1 - ja)) // 2 + (jb - ja - 1)
    return g_idx * PAIRS_PER_GROUP + pair


def _store_token_major(ref, x, t0=0):
    n = x.shape[0]
    for k in range(ROW_TILES):
        ref[pl.ds(t0 * ROW_TILES + k, n, stride=ROW_TILES), :] = x[:, k * LANES:(k + 1) * LANES]


def _load_token_major(ref, n):
    return jnp.concatenate([ref[pl.ds(k, n, stride=ROW_TILES), :] for k in range(ROW_TILES)], axis=1)


def _postmix_kernel(*refs, add_pos):
    refs = list(refs)
    x_ref = refs.pop(0)
    pos_refs = (refs.pop(0), refs.pop(0)) if add_pos else None
    (yr_ref, ys_ref, mod_ref, gpost_ref, gpre_ref, wout_ref, rw_ref, rb_ref, earlier_ref, cnt0_ref,
     x1_ref, hn_ref, rt_ref, cnt_ref, run_ref) = refs

    @pl.when(pl.program_id(0) == 0)
    def _():
        run_ref[...] = cnt0_ref[...]

    gate1 = mod_ref[0, :, 2 * D_MODEL:3 * D_MODEL]
    shift2 = mod_ref[0, :, 3 * D_MODEL:4 * D_MODEL]
    scale2 = mod_ref[0, :, 4 * D_MODEL:5 * D_MODEL]
    y = (jnp.dot(yr_ref[...], wout_ref[0:D_RNN, :], preferred_element_type=F32)
         + jnp.dot(ys_ref[...], wout_ref[D_RNN:, :], preferred_element_type=F32))
    x1 = _load_x(x_ref, pos_refs, 0, TM) + _rms(y) * (gate1 * gpost_ref[...])
    x1_ref[...] = x1
    hn = _rms(x1) * (gpre_ref[...] * (1.0 + scale2)) + shift2
    _store_token_major(hn_ref, hn)
    lt = lax.dot_general(rw_ref[...], hn.astype(BF16), (((1,), (1,)), ((), ())),
                         preferred_element_type=F32) + rb_ref[:, 0:1]
    bucket = _route(lt)
    onehot = lax.broadcasted_iota(jnp.int32, (ROUTER_ROWS, TM), 0) == bucket
    before = jnp.dot(onehot.astype(BF16), earlier_ref[...], preferred_element_type=F32) + run_ref[:, 0:1]
    rank = jnp.sum(jnp.where(onehot, before, 0.0), axis=0, keepdims=True).astype(jnp.int32)
    row = lax.broadcasted_iota(jnp.int32, (SUBLANES, TM), 0)
    rt_ref[...] = jnp.where(row == 0, bucket, jnp.where(row == 1, rank, 0))
    run_ref[...] += jnp.sum(onehot.astype(F32), axis=1, keepdims=True)
    cnt_ref[...] = run_ref[...]


def _postmix(x, y_rnn, y_sgu, mod3, cond_of_tile, g_post, g_pre, w_out_b, router_wt, router_bt, earlier, pos_tab,
             counts0):
    n_tok = x.shape[0]
    add_pos = pos_tab is not None
    tok = lambda i: (i, 0)
    const2 = lambda i: (0, 0)
    in_specs = [pl.BlockSpec((TM, D_MODEL), tok)]
    args = [x]
    if add_pos:
        reps = TM // GRID_W
        tiles_per_seq = GRID_W // reps
        in_specs += [pl.BlockSpec((None, reps, D_MODEL // 2), lambda i: (i % tiles_per_seq, 0, 0)),
                     pl.BlockSpec((GRID_W, D_MODEL // 2), const2)]
        args += [pos_tab.reshape(tiles_per_seq, reps, D_MODEL // 2), pos_tab]
    in_specs += [pl.BlockSpec((TM, D_RNN), tok),
                 pl.BlockSpec((TM, D_SGU), tok),
                 pl.BlockSpec((1, 1, 6 * D_MODEL), lambda i: (cond_of_tile(i), 0, 0)),
                 pl.BlockSpec((1, D_MODEL), const2),
                 pl.BlockSpec((1, D_MODEL), const2),
                 pl.BlockSpec((D_MODEL, D_MODEL), const2),
                 pl.BlockSpec((ROUTER_ROWS, D_MODEL), const2),
                 pl.BlockSpec((ROUTER_ROWS, LANES), const2),
                 pl.BlockSpec((TM, TM), const2),
                 pl.BlockSpec((ROUTER_ROWS, LANES), const2)]
    args += [y_rnn, y_sgu, mod3, g_post, g_pre, w_out_b, router_wt, router_bt, earlier, counts0]
    n_tiles = n_tok // TM
    counts_spec = pl.BlockSpec((ROUTER_ROWS, LANES), const2)
    x1, hn, route, counts = pl.pallas_call(
        functools.partial(_postmix_kernel, add_pos=add_pos),
        out_shape=(jax.ShapeDtypeStruct((n_tok, D_MODEL), F32),
                   jax.ShapeDtypeStruct((n_tok * ROW_TILES, LANES), F32),
                   jax.ShapeDtypeStruct((n_tiles * SUBLANES, TM), jnp.int32),
                   jax.ShapeDtypeStruct((ROUTER_ROWS, LANES), F32)),
        grid=(n_tiles,),
        in_specs=in_specs,
        out_specs=(pl.BlockSpec((TM, D_MODEL), tok),
                   pl.BlockSpec((TM * ROW_TILES, LANES), tok),
                   pl.BlockSpec((SUBLANES, TM), tok),
                   counts_spec),
        scratch_shapes=[pltpu.VMEM((ROUTER_ROWS, LANES), F32)],
        compiler_params=_params(("arbitrary",)),
        name="postmix",
    )(*args)
    route = route.reshape(n_tiles, SUBLANES, TM)
    return x1, hn, route[:, 0].reshape(n_tok), route[:, 1].reshape(n_tok), counts


def _token_rows(ref, t):
    return ref.at[pl.ds(pl.multiple_of(t * ROW_TILES, ROW_TILES), ROW_TILES), :]


def _dispatch_kernel(dest_ref, hc_ref, hs_ref, xs_ref, sem, *, n_ctx_steps):
    i = pl.program_id(0)
    base = i * TD

    def scatter(src_ref):
        def start(g, carry):
            for u in range(DMA_UNROLL):
                r = g * DMA_UNROLL + u
                pltpu.make_async_copy(_token_rows(src_ref, r), _token_rows(xs_ref, dest_ref[base + r]),
                                      sem).start(priority=u % 2)
            return carry

        lax.fori_loop(0, TD // DMA_UNROLL, start, 0)
        pltpu.make_async_copy(src_ref, xs_ref.at[pl.ds(0, TD * ROW_TILES), :], sem).wait()

    @pl.when(i < n_ctx_steps)
    def _():
        scatter(hc_ref)

    @pl.when(i >= n_ctx_steps)
    def _():
        scatter(hs_ref)


def _dispatch(dest, hn_ctx, hn_dec, n_slots):
    n_ctx_steps = hn_ctx.shape[0] // (TD * ROW_TILES)
    n_dec_steps = hn_dec.shape[0] // (TD * ROW_TILES)
    return pl.pallas_call(
        functools.partial(_dispatch_kernel, n_ctx_steps=n_ctx_steps),
        out_shape=jax.ShapeDtypeStruct((n_slots * ROW_TILES, LANES), F32),
        grid_spec=pltpu.PrefetchScalarGridSpec(
            num_scalar_prefetch=1,
            grid=(n_ctx_steps + n_dec_steps,),
            in_specs=[pl.BlockSpec((TD * ROW_TILES, LANES), lambda i, d: (jnp.minimum(i, n_ctx_steps - 1), 0)),
                      pl.BlockSpec((TD * ROW_TILES, LANES), lambda i, d: (jnp.maximum(i - n_ctx_steps, 0), 0))],
            out_specs=pl.BlockSpec(memory_space=pl.ANY),
            scratch_shapes=[pltpu.SemaphoreType.DMA(())]),
        compiler_params=_params(("arbitrary",)),
        name="dispatch",
    )(dest, hn_ctx, hn_dec)


def _experts_kernel(ea_ref, eb_ref, nv_ref, xs_ref, rw_ref, rb_ref, *refs):
    w32_refs, (gpost_ref, ys_ref), w_refs = refs[:6], refs[6:8], refs[8:]
    wga_ref, wua_ref, wda_ref, wgb_ref, wub_ref, wdb_ref = w_refs
    i = pl.program_id(0)
    nv = nv_ref[i]
    prev = jnp.maximum(i - 1, 0)

    @pl.when((i == 0) | (ea_ref[i] != ea_ref[prev]) | (eb_ref[i] != eb_ref[prev]))
    def _():
        for w32_ref, w_ref in zip(w32_refs, w_refs):
            w_ref[...] = w32_ref[0].astype(BF16)

    @pl.when(nv == 0)
    def _():
        ys_ref[...] = jnp.zeros_like(ys_ref)

    @pl.when(nv > 0)
    def _():
        row = lax.broadcasted_iota(jnp.int32, (TMX, 1), 0)
        xb = jnp.where(row < nv, _load_token_major(xs_ref, TMX), 0.0).astype(BF16)
        logits = jnp.dot(xb, rw_ref[...], preferred_element_type=F32) + rb_ref[...]
        lane = lax.broadcasted_iota(jnp.int32, logits.shape, 1)
        ea = ea_ref[i]
        eb = eb_ref[i]
        gmask = lane < N_GROUPS
        gl = jnp.where(gmask, logits, -jnp.inf)
        gmax = jnp.max(gl, axis=-1, keepdims=True)
        gexp = jnp.where(gmask, jnp.exp(gl - gmax), 0.0)
        g_own = jnp.sum(jnp.where(lane == ea // EXPERTS_PER_GROUP, gexp, 0.0), axis=-1, keepdims=True)
        g_w = g_own / jnp.sum(gexp, axis=-1, keepdims=True)
        la = jnp.sum(jnp.where(lane == ea + E_LANE0, logits, 0.0), axis=-1, keepdims=True)
        lb = jnp.sum(jnp.where(lane == eb + E_LANE0, logits, 0.0), axis=-1, keepdims=True)
        m = jnp.maximum(la, lb)
        pa = jnp.exp(la - m)
        pb = jnp.exp(lb - m)
        inv = g_w / (pa + pb)

        def hidden(x, wg_ref, wu_ref, w):
            g = jnp.dot(x, wg_ref[...], preferred_element_type=F32)
            u = jnp.dot(x, wu_ref[...], preferred_element_type=F32)
            return ((g * _sigmoid(g)) * u * w).astype(BF16)

        part = TMX // EXPERT_ROW_PARTS
        ys = []
        for h in range(EXPERT_ROW_PARTS):
            rows = slice(h * part, (h + 1) * part)
            act_a = hidden(xb[rows], wga_ref, wua_ref, (pa * inv)[rows])
            act_b = hidden(xb[rows], wgb_ref, wub_ref, (pb * inv)[rows])
            y = (jnp.dot(act_a, wda_ref[...], preferred_element_type=F32)
                 + jnp.dot(act_b, wdb_ref[...], preferred_element_type=F32))
            ys.append(_rms(y) * gpost_ref[...])
        _store_token_major(ys_ref, jnp.concatenate(ys, axis=0))


def _experts(sched, xs, router_w, router_b, wg, wu, wd, g_post):
    ea, eb, nv = sched
    n_tiles = ea.shape[0]
    rows = lambda i, ea, eb, nv: (i, 0)
    const2 = lambda i, ea, eb, nv: (0, 0)
    exp_a = lambda i, ea, eb, nv: (ea[i], 0, 0)
    exp_b = lambda i, ea, eb, nv: (eb[i], 0, 0)
    w_in_spec = lambda m: pl.BlockSpec((1, D_MODEL, D_EXPERT), m)
    w_out_spec = lambda m: pl.BlockSpec((1, D_EXPERT, D_MODEL), m)
    return pl.pallas_call(
        _experts_kernel,
        out_shape=jax.ShapeDtypeStruct(xs.shape, F32),
        grid_spec=pltpu.PrefetchScalarGridSpec(
            num_scalar_prefetch=3,
            grid=(n_tiles,),
            in_specs=[pl.BlockSpec((TMX * ROW_TILES, LANES), rows),
                      pl.BlockSpec((D_MODEL, ROUTER_LANES), const2),
                      pl.BlockSpec((1, ROUTER_LANES), const2),
                      w_in_spec(exp_a), w_in_spec(exp_a), w_out_spec(exp_a),
                      w_in_spec(exp_b), w_in_spec(exp_b), w_out_spec(exp_b),
                      pl.BlockSpec((1, D_MODEL), const2)],
            out_specs=pl.BlockSpec((TMX * ROW_TILES, LANES), rows),
            scratch_shapes=[pltpu.VMEM((D_MODEL, D_EXPERT), BF16), pltpu.VMEM((D_MODEL, D_EXPERT), BF16),
                            pltpu.VMEM((D_EXPERT, D_MODEL), BF16)] * 2),
        compiler_params=_params(("arbitrary",)),
        name="experts",
    )(ea, eb, nv, xs, router_w, router_b, wg, wu, wd, wg, wu, wd, g_post)


def _combine_kernel(dest_ref, ys_ref, x1_ref, mod_ref, o_ref, ybuf, sems, *, tile0):
    i = pl.program_id(0)
    n = pl.num_programs(0)

    def fetch(tile, slot):
        base = (tile + tile0) * TM

        def start(g, carry):
            for u in range(DMA_UNROLL):
                r = g * DMA_UNROLL + u
                pltpu.make_async_copy(_token_rows(ys_ref, dest_ref[base + r]), _token_rows(ybuf.at[slot], r),
                                      sems.at[slot]).start(priority=u % 2)
            return carry

        lax.fori_loop(0, TM // DMA_UNROLL, start, 0)

    @pl.when(i == 0)
    def _():
        fetch(0, 0)

    @pl.when(i + 1 < n)
    def _():
        fetch(i + 1, (i + 1) % 2)

    slot = i % 2
    pltpu.make_async_copy(ys_ref.at[pl.ds(0, TM * ROW_TILES), :], ybuf.at[slot], sems.at[slot]).wait()
    gate2 = mod_ref[0, :, 5 * D_MODEL:6 * D_MODEL]
    o_ref[...] = x1_ref[...] + gate2 * _load_token_major(ybuf.at[slot], TM)


def _combine(dest, ys, x1, mod3, cond_of_tile, tile0):
    n_tok = x1.shape[0]
    return pl.pallas_call(
        functools.partial(_combine_kernel, tile0=tile0),
        out_shape=jax.ShapeDtypeStruct((n_tok, D_MODEL), F32),
        grid_spec=pltpu.PrefetchScalarGridSpec(
            num_scalar_prefetch=1,
            grid=(n_tok // TM,),
            in_specs=[pl.BlockSpec(memory_space=pl.ANY),
                      pl.BlockSpec((TM, D_MODEL), lambda i, d: (i, 0)),
                      pl.BlockSpec((1, 1, 6 * D_MODEL), lambda i, d: (cond_of_tile(i), 0, 0))],
            out_specs=pl.BlockSpec((TM, D_MODEL), lambda i, d: (i, 0)),
            scratch_shapes=[pltpu.VMEM((2, TM * ROW_TILES, LANES), F32), pltpu.SemaphoreType.DMA((2,))]),
        compiler_params=_params(("arbitrary",)),
        name="combine",
    )(dest, ys, x1, mod3)


def _schedule(bucket, rank, counts):
    n_tok = bucket.shape[0]
    n_max = n_tok // TMX + N_BUCKETS
    cnt = counts[:N_BUCKETS, 0].astype(jnp.int32)
    tiles = (cnt + TMX - 1) // TMX
    tile_end = jnp.cumsum(tiles)
    tile_start = tile_end - tiles
    ids = jnp.arange(N_BUCKETS, dtype=jnp.int32)
    slot0 = jnp.sum(jnp.where(bucket[:, None] == ids[None, :], (tile_start * TMX)[None, :], 0), axis=1)
    dest = slot0 + rank
    i = jnp.arange(n_max, dtype=jnp.int32)
    total = tile_end[-1]
    valid = i < total
    tb = jnp.sum((jnp.minimum(i, total - 1)[:, None] >= tile_end[None, :]).astype(jnp.int32), axis=1)
    pairs = [(a, b) for a in range(EXPERTS_PER_GROUP) for b in range(a + 1, EXPERTS_PER_GROUP)]
    ea_tab = jnp.array([g * EXPERTS_PER_GROUP + a for g in range(N_GROUPS) for a, _ in pairs], jnp.int32)
    eb_tab = jnp.array([g * EXPERTS_PER_GROUP + b for g in range(N_GROUPS) for _, b in pairs], jnp.int32)
    ea = ea_tab[tb]
    eb = eb_tab[tb]
    nv = jnp.where(valid, jnp.clip(cnt[tb] - (i - tile_start[tb]) * TMX, 0, TMX), 0)
    return dest, (ea, eb, nv), n_max * TMX


def _block_diag_gates(rg_wa, rg_wx):
    heads = LANES // HEAD_RNN

    def bd(w):
        w = w.reshape(2, RNN_BLOCKS, heads, HEAD_RNN, HEAD_RNN)
        eye = jnp.eye(heads, dtype=w.dtype)
        full = jnp.einsum('dghij,hk->dghikj', w, eye)
        return full.reshape(2, RNN_BLOCKS, LANES, LANES)

    return jnp.concatenate([bd(rg_wa), bd(rg_wx)], axis=-1).astype(BF16)


def _row_tile(v):
    blocks = v.reshape(v.shape[:-1] + (RNN_BLOCKS, LANES))
    return jnp.concatenate([blocks] * PAIR, axis=-2)


def _to_time_major_state(h):
    return h.reshape(h.shape[0] // PAIR, TMJ_ROWS, LANES)


def kernel(x_prompt, x_sample, state_rglru, c, c_ctx, w_mod, b_mod, g_pre_mix, g_post_mix, g_pre_ffn,
           g_post_ffn, w_in, conv_w, conv_b, rg_wa, rg_ba, rg_wx, rg_bx, rg_lambda, sgu_g, sgu_w, sgu_b,
           w_out, router_g_w, router_g_b, router_e_w, router_e_b, exp_w_gate, exp_w_up, exp_w_down):
    assert w_mod.shape[0] == 1, "single-layer trunk"
    n_ctx, ctx_len, _ = x_prompt.shape
    n_dec, dec_len, _ = x_sample.shape
    l = 0

    n_cond = SUBLANES
    assert n_dec % PAIR == 0 and n_dec + PAIR <= n_cond
    cond = jnp.zeros((n_cond, D_MODEL), F32).at[:n_dec].set(c).at[n_dec:n_dec + PAIR].set(c_ctx)
    mod3 = _modulation(cond, w_mod[l], b_mod[l]).reshape(n_cond, 1, 6 * D_MODEL)
    pos_tab = _pos_table()

    w_in_b = w_in[l].astype(BF16)
    w_out_b = w_out[l].astype(BF16)
    sgu_w_b = sgu_w[l].reshape(2, 4 * CHUNK, CHUNK).astype(BF16)
    sgu_bias_tile = jnp.repeat(sgu_b[l].T, HEAD_SGU, axis=1)
    w_gates = _block_diag_gates(rg_wa[l], rg_wx[l])
    b_gates = 0.5 * jnp.stack([_row_tile(rg_ba[l]), _row_tile(rg_bx[l])], axis=1)
    lam = _row_tile(rg_lambda[l])
    conv_w_t = 0.5 * _row_tile(conv_w[l])
    conv_b_t = 0.5 * _row_tile(conv_b[l])
    router_w = jnp.zeros((D_MODEL, ROUTER_LANES), F32)
    router_w = router_w.at[:, :N_GROUPS].set(router_g_w[l]).at[:, E_LANE0:E_LANE0 + N_EXPERTS].set(router_e_w[l])
    router_w = router_w.astype(BF16)
    router_b = jnp.zeros((1, ROUTER_LANES), F32)
    router_b = router_b.at[0, :N_GROUPS].set(router_g_b[l]).at[0, E_LANE0:E_LANE0 + N_EXPERTS].set(router_e_b[l])
    router_wt = jnp.zeros((ROUTER_ROWS, D_MODEL), F32)
    router_wt = router_wt.at[:N_GROUPS].set(router_g_w[l].T).at[E_ROW0:E_ROW0 + N_EXPERTS].set(router_e_w[l].T)
    router_wt = router_wt.astype(BF16)
    router_bt = jnp.zeros((ROUTER_ROWS,), F32)
    router_bt = router_bt.at[:N_GROUPS].set(router_g_b[l]).at[E_ROW0:E_ROW0 + N_EXPERTS].set(router_e_b[l])
    router_bt = jnp.broadcast_to(router_bt[:, None], (ROUTER_ROWS, LANES))
    earlier = jnp.triu(jnp.ones((TM, TM), BF16), k=1)
    row = lambda v: v.reshape(1, -1)

    n_ctx_tok = n_ctx * ctx_len

    def mixer(x, h0, cond_of_tile, cond_block, use_pos, counts0):
        n_seq, seq_len, _ = x.shape
        xf = x.reshape(n_seq * seq_len, D_MODEL)
        tab = pos_tab if use_pos else None
        xr, gg, y_sgu = _premix(x, mod3, cond_block, row(g_pre_mix[l]), w_in_b, row(sgu_g[l]),
                                sgu_w_b, sgu_bias_tile, tab)
        scan_params = (conv_w_t, conv_b_t, w_gates, b_gates, lam)
        hf, hf_last = _scan(xr, None, None, _to_time_major_state(h0[:, 0]), *scan_params, direction=0)
        y_rnn, hb_first = _scan(xr, gg, hf, _to_time_major_state(h0[:, 1]), *scan_params, direction=1)
        fstate = jnp.stack([hf_last.reshape(n_seq, D_RNN), hb_first.reshape(n_seq, D_RNN)], axis=1)
        x1, hn, bucket, rank, counts = _postmix(
            xf, y_rnn.reshape(n_seq * seq_len, D_RNN), y_sgu.reshape(n_seq * seq_len, D_SGU), mod3,
            cond_of_tile, row(g_post_mix[l]), row(g_pre_ffn[l]), w_out_b, router_wt, router_bt, earlier, tab,
            counts0)
        return x1, hn, bucket, rank, counts, fstate

    tiles_per_seq = dec_len // TM
    ctx_cond = lambda i: n_dec
    dec_cond = lambda i: i // tiles_per_seq
    h0_ctx = jnp.zeros((n_ctx, 2, D_RNN), F32)
    counts0 = jnp.zeros((ROUTER_ROWS, LANES), F32)
    x1_ctx, hn_ctx, bucket_ctx, rank_ctx, counts, st = mixer(x_prompt, h0_ctx, ctx_cond, lambda p: n_dec // PAIR,
                                                             False, counts0)
    new_state = st.astype(state_rglru.dtype)[:, None]
    x1_dec, hn_dec, bucket_dec, rank_dec, counts, _ = mixer(x_sample, state_rglru[:, l].astype(F32), dec_cond,
                                                            lambda p: p, True, counts)

    dest, sched, n_slots = _schedule(jnp.concatenate([bucket_ctx, bucket_dec]),
                                     jnp.concatenate([rank_ctx, rank_dec]), counts)
    xs = _dispatch(dest, hn_ctx, hn_dec, n_slots)
    ys = _experts(sched, xs, router_w, router_b, exp_w_gate[l], exp_w_up[l], exp_w_down[l], row(g_post_ffn[l]))
    y_prompt = _combine(dest, ys, x1_ctx, mod3, ctx_cond, 0)
    y_sample = _combine(dest, ys, x1_dec, mod3, dec_cond, n_ctx_tok // TM)
    return (y_prompt.reshape(x_prompt.shape), y_sample.reshape(x_sample.shape), new_state)
```

```python
import functools
import math

import jax
import jax.numpy as jnp
from jax import lax
from jax.experimental import pallas as pl
from jax.experimental.pallas import tpu as pltpu

D_MODEL = 1024
D_RNN = 512
D_SGU = 512
N_HEADS_RNN = 8
HEAD_RNN = D_RNN // N_HEADS_RNN
N_HEADS_SGU = 8
HEAD_SGU = D_SGU // N_HEADS_SGU
CHUNK = 128
GRID_W = 64
RG_C = 8.0
N_GROUPS = 4
EXPERTS_PER_GROUP = 4
N_EXPERTS = N_GROUPS * EXPERTS_PER_GROUP
D_EXPERT = 512
EPS = 1e-6
POS_BASE = 10000.0

LANES = 128
SUBLANES = 8
CONV_W = 4
CONV_LEFT = 2
PAIR = 2
RNN_BLOCKS = D_RNN // LANES
TMJ_ROWS = PAIR * RNN_BLOCKS
ROUTER_LANES = LANES
E_LANE0 = N_GROUPS
ROUTER_ROWS = 32
E_ROW0 = SUBLANES

PAIRS_PER_GROUP = EXPERTS_PER_GROUP * (EXPERTS_PER_GROUP - 1) // 2
N_BUCKETS = N_GROUPS * PAIRS_PER_GROUP

ROW_TILES = D_MODEL // LANES

TM = 512
TD = 2048
DMA_UNROLL = 8
EXPERT_ROW_PARTS = 2
TMX = 512
TT = TM // PAIR
LC = 256
TS = 16
PB = 2
VMEM_LIMIT = 56 * 1024 * 1024

F32 = jnp.float32
BF16 = jnp.bfloat16


def _params(sem):
    return pltpu.CompilerParams(dimension_semantics=sem, vmem_limit_bytes=VMEM_LIMIT)


def _rms(x):
    return x * lax.rsqrt(jnp.mean(x * x, axis=-1, keepdims=True) + EPS)


def _sigmoid(x):
    return 0.5 * jnp.tanh(0.5 * x) + 0.5


def _mod_kernel(cond_ref, w_ref, b_ref, o_ref):
    c = cond_ref[...]
    s = c * _sigmoid(c)
    o_ref[...] = jnp.dot(s.astype(BF16), w_ref[...].astype(BF16),
                         preferred_element_type=F32) + b_ref[...]


def _modulation(cond, w_mod, b_mod):
    n = w_mod.shape[1]
    bn = 1024
    return pl.pallas_call(
        _mod_kernel,
        out_shape=jax.ShapeDtypeStruct((cond.shape[0], n), F32),
        grid=(n // bn,),
        in_specs=[pl.BlockSpec(cond.shape, lambda j: (0, 0)),
                  pl.BlockSpec((D_MODEL, bn), lambda j: (0, j)),
                  pl.BlockSpec((1, bn), lambda j: (0, j))],
        out_specs=pl.BlockSpec((cond.shape[0], bn), lambda j: (0, j)),
        compiler_params=_params(("arbitrary",)),
        name="modulation",
    )(cond, w_mod, b_mod.reshape(1, n))


def _pos_kernel(o_ref):
    n_freq = D_MODEL // 4
    k = lax.broadcasted_iota(jnp.int32, (GRID_W, n_freq), 1).astype(F32)
    p = lax.broadcasted_iota(jnp.int32, (GRID_W, n_freq), 0).astype(F32)
    freq = jnp.exp(-math.log(POS_BASE) * k / n_freq)
    ang = p * freq
    o_ref[:, 0:n_freq] = jnp.sin(ang)
    o_ref[:, n_freq:2 * n_freq] = jnp.cos(ang)


def _pos_table():
    return pl.pallas_call(
        _pos_kernel,
        out_shape=jax.ShapeDtypeStruct((GRID_W, D_MODEL // 2), F32),
        name="pos_table",
    )()


def _add_pos(x, pos_refs, q0):
    if pos_refs is None:
        return x
    rows_ref, cols_ref = pos_refs
    reps = x.shape[0] // GRID_W
    rpart = jnp.concatenate(
        [jnp.broadcast_to(rows_ref[q:q + 1, :], (GRID_W, D_MODEL // 2)) for q in range(q0, q0 + reps)], axis=0)
    cpart = jnp.concatenate([cols_ref[...]] * reps, axis=0)
    return jnp.concatenate([x[:, :D_MODEL // 2] + rpart, x[:, D_MODEL // 2:] + cpart], axis=1)


def _load_x(x_ref, pos_refs, r0, n):
    return _add_pos(x_ref[r0:r0 + n, :], pos_refs, r0 // GRID_W)


def _premix_kernel(*refs, add_pos):
    refs = list(refs)
    x_ref = refs.pop(0)
    pos_refs = (refs.pop(0), refs.pop(0)) if add_pos else None
    mod_ref, g_ref, win_ref, sgug_ref, sguw_ref, sgub_ref, xr_ref, gg_ref, ys_ref = refs
    hn = []
    for s in range(PAIR):
        shift = mod_ref[s, :, 0:D_MODEL]
        scale = mod_ref[s, :, D_MODEL:2 * D_MODEL]
        hn.append(_rms(_add_pos(x_ref[s], pos_refs, 0)) * (g_ref[...] * (1.0 + scale)) + shift)
    z = jnp.dot(jnp.concatenate(hn, axis=0).astype(BF16), win_ref[...],
                preferred_element_type=F32)
    half = D_SGU // 2
    heads_per_half = N_HEADS_SGU // 2
    lane_head = lax.broadcasted_iota(jnp.int32, (CHUNK, half), 1) // HEAD_SGU
    for s in range(PAIR):
        zs = z[s * TT:(s + 1) * TT]
        gg = jax.nn.gelu(zs[:, D_RNN:2 * D_RNN])
        for k in range(RNN_BLOCKS):
            rows = pl.ds(s * RNN_BLOCKS + k, TT, stride=TMJ_ROWS)
            xr_ref[rows, :] = zs[:, k * LANES:(k + 1) * LANES]
            gg_ref[rows, :] = gg[:, k * LANES:(k + 1) * LANES]
        u = zs[:, 2 * D_RNN:2 * D_RNN + D_SGU]
        vn = (_rms(zs[:, 2 * D_RNN + D_SGU:]) * sgug_ref[...]).astype(BF16)
        for c in range(TT // CHUNK):
            rows = slice(c * CHUNK, (c + 1) * CHUNK)
            halves = []
            for hf in range(2):
                r = jnp.dot(sguw_ref[hf], vn[rows, hf * half:(hf + 1) * half],
                            preferred_element_type=F32)
                sel = jnp.zeros((CHUNK, half), F32)
                for h in range(heads_per_half):
                    sel = jnp.where(lane_head == h, r[h * CHUNK:(h + 1) * CHUNK], sel)
                halves.append(sel)
            gatev = jnp.concatenate(halves, axis=1) + sgub_ref[...]
            ys_ref[s, rows, :] = (u[rows] * gatev).astype(BF16)


def _premix(x, mod3, cond_block, g_pre, w_in_b, sgu_g, sgu_w_b, sgu_bias_tile, pos_tab):
    n_seq, seq_len, _ = x.shape
    n_pairs, n_tiles = n_seq // PAIR, seq_len // TT
    add_pos = pos_tab is not None
    const2 = lambda p, j: (0, 0)
    in_specs = [pl.BlockSpec((PAIR, TT, D_MODEL), lambda p, j: (p, j, 0))]
    args = [x]
    if add_pos:
        reps = TT // GRID_W
        in_specs += [pl.BlockSpec((None, reps, D_MODEL // 2), lambda p, j: (j, 0, 0)),
                     pl.BlockSpec((GRID_W, D_MODEL // 2), const2)]
        args += [pos_tab.reshape(GRID_W // reps, reps, D_MODEL // 2), pos_tab]
    in_specs += [pl.BlockSpec((PAIR, 1, 6 * D_MODEL), lambda p, j: (cond_block(p), 0, 0)),
                 pl.BlockSpec((1, D_MODEL), const2),
                 pl.BlockSpec((D_MODEL, 2 * D_RNN + 2 * D_SGU), const2),
                 pl.BlockSpec((1, D_SGU), const2),
                 pl.BlockSpec((2, 4 * CHUNK, CHUNK), lambda p, j: (0, 0, 0)),
                 pl.BlockSpec((CHUNK, D_SGU), const2)]
    args += [mod3, g_pre, w_in_b, sgu_g, sgu_w_b, sgu_bias_tile]
    tmj = jax.ShapeDtypeStruct((n_pairs * seq_len * TMJ_ROWS, LANES), F32)
    tmj_spec = pl.BlockSpec((TT * TMJ_ROWS, LANES), lambda p, j: (p * n_tiles + j, 0))
    xr, gg, y_sgu = pl.pallas_call(
        functools.partial(_premix_kernel, add_pos=add_pos),
        out_shape=(tmj, tmj, jax.ShapeDtypeStruct((n_seq, seq_len, D_SGU), BF16)),
        grid=(n_pairs, n_tiles),
        in_specs=in_specs,
        out_specs=(tmj_spec, tmj_spec, pl.BlockSpec((PAIR, TT, D_SGU), lambda p, j: (p, j, 0))),
        compiler_params=_params(("parallel", "parallel")),
        name="premix",
    )(*args)
    shape4 = (n_pairs, seq_len, TMJ_ROWS, LANES)
    return xr.reshape(shape4), gg.reshape(shape4), y_sgu


def _scan_kernel(*refs, reverse, n_chunks):
    if reverse:
        (xprev_ref, x_ref, xnext_ref, gg_ref, hf_ref, h0_ref, cw_ref, cb_ref, wg_ref, bg_ref, lam_ref,
         y_ref, fs_ref, xwin, xc_s, r_s, i_s, a_s, b_s, y_s, hcar) = refs
    else:
        (xprev_ref, x_ref, xnext_ref, h0_ref, cw_ref, cb_ref, wg_ref, bg_ref, lam_ref,
         hf_ref, fs_ref, xwin, xc_s, r_s, i_s, a_s, b_s, hcar) = refs
    c = pl.program_id(1)
    chunk = n_chunks - 1 - c if reverse else c
    sub_rows = TS * TMJ_ROWS

    def rows_of(pb, t0, n_steps):
        first = (pb * LC + t0) * TMJ_ROWS
        if not isinstance(first, int):
            first = pl.multiple_of(first, TMJ_ROWS)
        return pl.ds(first, n_steps * TMJ_ROWS)

    @pl.when(c == 0)
    def _():
        hcar[...] = h0_ref[...]

    xwin[:, 0:CONV_LEFT] = jnp.where(chunk > 0, xprev_ref[...], 0.0)
    xwin[:, LC + CONV_LEFT:LC + CONV_W - 1] = jnp.where(chunk < n_chunks - 1, xnext_ref[...], 0.0)

    xwin[:, CONV_LEFT:CONV_LEFT + LC] = x_ref[...]

    neg_lam = -lam_ref[...]
    softplus = jnp.maximum(neg_lam, 0.0) + jnp.log(1.0 + jnp.exp(-jnp.abs(neg_lam)))
    half_decay = (-0.5 * RG_C * math.log2(math.e)) * softplus

    def conv(pb):
        for t0 in range(0, LC, TS):
            xc = cb_ref[...] + cw_ref[0] * xwin[pb, t0:t0 + TS]
            for k in range(1, CONV_W):
                xc = xc + cw_ref[k] * xwin[pb, t0 + k:t0 + k + TS]
            xc_s[rows_of(pb, t0, TS), :] = xc.reshape(sub_rows, LANES)

    def gate_matmuls(pb):
        for k in range(RNN_BLOCKS):
            rows = pl.ds(pb * LC * TMJ_ROWS + k, LC * PAIR, stride=RNN_BLOCKS)
            g = jnp.dot(xc_s[rows, :].astype(BF16), wg_ref[k], preferred_element_type=F32)
            r_s[rows, :] = g[:, :LANES]
            i_s[rows, :] = g[:, LANES:]

    def gates(pb):
        for t0 in range(0, LC, TS):
            rows = rows_of(pb, t0, TS)
            tile = lambda ref: ref[rows, :].reshape(TS, TMJ_ROWS, LANES)
            tr = jnp.tanh(tile(r_s) + bg_ref[0])
            ti = jnp.tanh(tile(i_s) + bg_ref[1])
            log2_a = tr * half_decay + half_decay
            a = jnp.exp2(log2_a)
            q = jnp.tanh(log2_a * (-math.log(2.0))) * (a * a + 1.0)
            b = jnp.where(q > 0.0, q * lax.rsqrt(q), 0.0) * ((ti + 1.0) * tile(xc_s))
            a_s[rows, :] = a.reshape(sub_rows, LANES)
            b_s[rows, :] = b.reshape(sub_rows, LANES)

    for stage in (conv, gate_matmuls, gates):
        for pb in range(PB):
            stage(pb)

    def step(j, hs):
        t = LC - 1 - j if reverse else j
        out = []
        for pb in range(PB):
            rows = rows_of(pb, t, 1)
            h = a_s[rows, :] * hs[pb] + b_s[rows, :]
            if reverse:
                y_s[rows, :] = (hf_ref[pb, t] + h) * gg_ref[pb, t]
            else:
                hf_ref[pb, t] = h
            out.append(h)
        return tuple(out)

    hs = lax.fori_loop(0, LC, step, tuple(hcar[pb] for pb in range(PB)), unroll=8)
    for pb in range(PB):
        hcar[pb] = hs[pb]
        fs_ref[pb] = hs[pb]

    if reverse:
        for pb in range(PB):
            for s in range(PAIR):
                cols = [y_s[pl.ds(pb * LC * TMJ_ROWS + s * RNN_BLOCKS + k, LC, stride=TMJ_ROWS), :]
                        for k in range(RNN_BLOCKS)]
                y_ref[pb * PAIR + s] = jnp.concatenate(cols, axis=1).astype(BF16)


def _scan(xr, gg, hf, h0, conv_w, conv_b, w_gates, b_gates, lam, direction):
    n_pairs, seq_len = xr.shape[:2]
    n_chunks = seq_len // LC
    reverse = direction == 1
    pos = (lambda c: n_chunks - 1 - c) if reverse else (lambda c: c)
    tmj_blk = pl.BlockSpec((PB, LC, TMJ_ROWS, LANES), lambda i, c: (i, pos(c), 0, 0))
    state_blk = pl.BlockSpec((PB, TMJ_ROWS, LANES), lambda i, c: (i, 0, 0))
    per_dir = lambda *shape: pl.BlockSpec((None,) + shape, lambda i, c: (direction,) + (0,) * len(shape))
    in_specs = [
        pl.BlockSpec((PB, CONV_LEFT, TMJ_ROWS, LANES),
                     lambda i, c: (i, jnp.maximum(pos(c) * (LC // CONV_LEFT) - 1, 0), 0, 0)),
        tmj_blk,
        pl.BlockSpec((PB, 1, TMJ_ROWS, LANES), lambda i, c: (i, jnp.minimum((pos(c) + 1) * LC, seq_len - 1), 0, 0)),
    ]
    args = [xr, xr, xr]
    if reverse:
        in_specs += [tmj_blk, tmj_blk]
        args += [gg, hf]
    in_specs += [state_blk,
                 pl.BlockSpec((CONV_W, TMJ_ROWS, LANES), lambda i, c: (0, 0, 0)),
                 pl.BlockSpec((TMJ_ROWS, LANES), lambda i, c: (0, 0)),
                 per_dir(RNN_BLOCKS, LANES, 2 * LANES),
                 per_dir(2, TMJ_ROWS, LANES),
                 per_dir(TMJ_ROWS, LANES)]
    args += [h0, conv_w, conv_b, w_gates, b_gates, lam]
    flat = pltpu.VMEM((PB * LC * TMJ_ROWS, LANES), F32)
    scratch = [pltpu.VMEM((PB, LC + CONV_W - 1, TMJ_ROWS, LANES), F32)] + [flat] * (6 if reverse else 5)
    scratch += [pltpu.VMEM((PB, TMJ_ROWS, LANES), F32)]
    state = jax.ShapeDtypeStruct((n_pairs, TMJ_ROWS, LANES), F32)
    if reverse:
        out_shape = (jax.ShapeDtypeStruct((n_pairs * PAIR, seq_len, D_RNN), BF16), state)
        out_specs = (pl.BlockSpec((PB * PAIR, LC, D_RNN), lambda i, c: (i, pos(c), 0)), state_blk)
    else:
        out_shape = (jax.ShapeDtypeStruct(xr.shape, F32), state)
        out_specs = (tmj_blk, state_blk)
    return pl.pallas_call(
        functools.partial(_scan_kernel, reverse=reverse, n_chunks=n_chunks),
        out_shape=out_shape,
        grid=(n_pairs // PB, n_chunks),
        in_specs=in_specs,
        out_specs=out_specs,
        scratch_shapes=scratch,
        compiler_params=_params(("parallel", "arbitrary")),
        name="scan_bwd" if reverse else "scan_fwd",
    )(*args)


def _route(lt):
    n = lt.shape[1]
    row = lax.broadcasted_iota(jnp.int32, (EXPERTS_PER_GROUP, n), 0)
    neg = jnp.float32(-jnp.inf)

    def arg_max(v):
        m = jnp.max(v, axis=0, keepdims=True)
        return jnp.min(jnp.where(v == m, row, EXPERTS_PER_GROUP), axis=0, keepdims=True)

    g_idx = arg_max(lt[0:N_GROUPS])
    el = lt[E_ROW0:E_ROW0 + EXPERTS_PER_GROUP]
    for g in range(1, N_GROUPS):
        first = E_ROW0 + g * EXPERTS_PER_GROUP
        el = jnp.where(g_idx == g, lt[first:first + EXPERTS_PER_GROUP], el)
    i1 = arg_max(el)
    i2 = arg_max(jnp.where(row == i1, neg, el))
    ja = jnp.minimum(i1, i2)
    jb = jnp.maximum(i1, i2)
    pair = (ja * (2 * EXPERTS_PER_GROUP - 1 - ja)) // 2 + (jb - ja - 1)
    return g_idx * PAIRS_PER_GROUP + pair


def _store_token_major(ref, x, t0=0):
    n = x.shape[0]
    for k in range(ROW_TILES):
        ref[pl.ds(t0 * ROW_TILES + k, n, stride=ROW_TILES), :] = x[:, k * LANES:(k + 1) * LANES]


def _load_token_major(ref, n):
    return jnp.concatenate([ref[pl.ds(k, n, stride=ROW_TILES), :] for k in range(ROW_TILES)], axis=1)


def _postmix_kernel(*refs, add_pos):
    refs = list(refs)
    x_ref = refs.pop(0)
    pos_refs = (refs.pop(0), refs.pop(0)) if add_pos else None
    (yr_ref, ys_ref, mod_ref, gpost_ref, gpre_ref, wout_ref, rw_ref, rb_ref, earlier_ref, cnt0_ref,
     x1_ref, hn_ref, rt_ref, cnt_ref, run_ref) = refs

    @pl.when(pl.program_id(0) == 0)
    def _():
        run_ref[...] = cnt0_ref[...]

    gate1 = mod_ref[0, :, 2 * D_MODEL:3 * D_MODEL]
    shift2 = mod_ref[0, :, 3 * D_MODEL:4 * D_MODEL]
    scale2 = mod_ref[0, :, 4 * D_MODEL:5 * D_MODEL]
    y = (jnp.dot(yr_ref[...], wout_ref[0:D_RNN, :], preferred_element_type=F32)
         + jnp.dot(ys_ref[...], wout_ref[D_RNN:, :], preferred_element_type=F32))
    x1 = _load_x(x_ref, pos_refs, 0, TM) + _rms(y) * (gate1 * gpost_ref[...])
    x1_ref[...] = x1
    hn = _rms(x1) * (gpre_ref[...] * (1.0 + scale2)) + shift2
    _store_token_major(hn_ref, hn)
    lt = lax.dot_general(rw_ref[...], hn.astype(BF16), (((1,), (1,)), ((), ())),
                         preferred_element_type=F32) + rb_ref[:, 0:1]
    bucket = _route(lt)
    onehot = lax.broadcasted_iota(jnp.int32, (ROUTER_ROWS, TM), 0) == bucket
    before = jnp.dot(onehot.astype(BF16), earlier_ref[...], preferred_element_type=F32) + run_ref[:, 0:1]
    rank = jnp.sum(jnp.where(onehot, before, 0.0), axis=0, keepdims=True).astype(jnp.int32)
    row = lax.broadcasted_iota(jnp.int32, (SUBLANES, TM), 0)
    rt_ref[...] = jnp.where(row == 0, bucket, jnp.where(row == 1, rank, 0))
    run_ref[...] += jnp.sum(onehot.astype(F32), axis=1, keepdims=True)
    cnt_ref[...] = run_ref[...]


def _postmix(x, y_rnn, y_sgu, mod3, cond_of_tile, g_post, g_pre, w_out_b, router_wt, router_bt, earlier, pos_tab,
             counts0):
    n_tok = x.shape[0]
    add_pos = pos_tab is not None
    tok = lambda i: (i, 0)
    const2 = lambda i: (0, 0)
    in_specs = [pl.BlockSpec((TM, D_MODEL), tok)]
    args = [x]
    if add_pos:
        reps = TM // GRID_W
        tiles_per_seq = GRID_W // reps
        in_specs += [pl.BlockSpec((None, reps, D_MODEL // 2), lambda i: (i % tiles_per_seq, 0, 0)),
                     pl.BlockSpec((GRID_W, D_MODEL // 2), const2)]
        args += [pos_tab.reshape(tiles_per_seq, reps, D_MODEL // 2), pos_tab]
    in_specs += [pl.BlockSpec((TM, D_RNN), tok),
                 pl.BlockSpec((TM, D_SGU), tok),
                 pl.BlockSpec((1, 1, 6 * D_MODEL), lambda i: (cond_of_tile(i), 0, 0)),
                 pl.BlockSpec((1, D_MODEL), const2),
                 pl.BlockSpec((1, D_MODEL), const2),
                 pl.BlockSpec((D_MODEL, D_MODEL), const2),
                 pl.BlockSpec((ROUTER_ROWS, D_MODEL), const2),
                 pl.BlockSpec((ROUTER_ROWS, LANES), const2),
                 pl.BlockSpec((TM, TM), const2),
                 pl.BlockSpec((ROUTER_ROWS, LANES), const2)]
    args += [y_rnn, y_sgu, mod3, g_post, g_pre, w_out_b, router_wt, router_bt, earlier, counts0]
    n_tiles = n_tok // TM
    counts_spec = pl.BlockSpec((ROUTER_ROWS, LANES), const2)
    x1, hn, route, counts = pl.pallas_call(
        functools.partial(_postmix_kernel, add_pos=add_pos),
        out_shape=(jax.ShapeDtypeStruct((n_tok, D_MODEL), F32),
                   jax.ShapeDtypeStruct((n_tok * ROW_TILES, LANES), F32),
                   jax.ShapeDtypeStruct((n_tiles * SUBLANES, TM), jnp.int32),
                   jax.ShapeDtypeStruct((ROUTER_ROWS, LANES), F32)),
        grid=(n_tiles,),
        in_specs=in_specs,
        out_specs=(pl.BlockSpec((TM, D_MODEL), tok),
                   pl.BlockSpec((TM * ROW_TILES, LANES), tok),
                   pl.BlockSpec((SUBLANES, TM), tok),
                   counts_spec),
        scratch_shapes=[pltpu.VMEM((ROUTER_ROWS, LANES), F32)],
        compiler_params=_params(("arbitrary",)),
        name="postmix",
    )(*args)
    route = route.reshape(n_tiles, SUBLANES, TM)
    return x1, hn, route[:, 0].reshape(n_tok), route[:, 1].reshape(n_tok), counts


def _token_rows(ref, t):
    return ref.at[pl.ds(pl.multiple_of(t * ROW_TILES, ROW_TILES), ROW_TILES), :]


def _dispatch_kernel(dest_ref, hc_ref, hs_ref, xs_ref, sem, *, n_ctx_steps):
    i = pl.program_id(0)
    base = i * TD

    def scatter(src_ref):
        def start(g, carry):
            for u in range(DMA_UNROLL):
                r = g * DMA_UNROLL + u
                pltpu.make_async_copy(_token_rows(src_ref, r), _token_rows(xs_ref, dest_ref[base + r]),
                                      sem).start(priority=u % 2)
            return carry

        lax.fori_loop(0, TD // DMA_UNROLL, start, 0)
        pltpu.make_async_copy(src_ref, xs_ref.at[pl.ds(0, TD * ROW_TILES), :], sem).wait()

    @pl.when(i < n_ctx_steps)
    def _():
        scatter(hc_ref)

    @pl.when(i >= n_ctx_steps)
    def _():
        scatter(hs_ref)


def _dispatch(dest, hn_ctx, hn_dec, n_slots):
    n_ctx_steps = hn_ctx.shape[0] // (TD * ROW_TILES)
    n_dec_steps = hn_dec.shape[0] // (TD * ROW_TILES)
    return pl.pallas_call(
        functools.partial(_dispatch_kernel, n_ctx_steps=n_ctx_steps),
        out_shape=jax.ShapeDtypeStruct((n_slots * ROW_TILES, LANES), F32),
        grid_spec=pltpu.PrefetchScalarGridSpec(
            num_scalar_prefetch=1,
            grid=(n_ctx_steps + n_dec_steps,),
            in_specs=[pl.BlockSpec((TD * ROW_TILES, LANES), lambda i, d: (jnp.minimum(i, n_ctx_steps - 1), 0)),
                      pl.BlockSpec((TD * ROW_TILES, LANES), lambda i, d: (jnp.maximum(i - n_ctx_steps, 0), 0))],
            out_specs=pl.BlockSpec(memory_space=pl.ANY),
            scratch_shapes=[pltpu.SemaphoreType.DMA(())]),
        compiler_params=_params(("arbitrary",)),
        name="dispatch",
    )(dest, hn_ctx, hn_dec)


def _experts_kernel(ea_ref, eb_ref, nv_ref, xs_ref, rw_ref, rb_ref, *refs):
    w32_refs, ys_ref, w_refs = refs[:6], refs[6], refs[7:]
    wga_ref, wua_ref, wda_ref, wgb_ref, wub_ref, wdb_ref = w_refs
    i = pl.program_id(0)
    nv = nv_ref[i]
    prev = jnp.maximum(i - 1, 0)

    @pl.when((i == 0) | (ea_ref[i] != ea_ref[prev]) | (eb_ref[i] != eb_ref[prev]))
    def _():
        for w32_ref, w_ref in zip(w32_refs, w_refs):
            w_ref[...] = w32_ref[0].astype(BF16)

    @pl.when(nv == 0)
    def _():
        ys_ref[...] = jnp.zeros_like(ys_ref)

    @pl.when(nv > 0)
    def _():
        row = lax.broadcasted_iota(jnp.int32, (TMX, 1), 0)
        xb = jnp.where(row < nv, _load_token_major(xs_ref, TMX), 0.0).astype(BF16)
        logits = jnp.dot(xb, rw_ref[...], preferred_element_type=F32) + rb_ref[...]
        lane = lax.broadcasted_iota(jnp.int32, logits.shape, 1)
        ea = ea_ref[i]
        eb = eb_ref[i]
        gmask = lane < N_GROUPS
        gl = jnp.where(gmask, logits, -jnp.inf)
        gmax = jnp.max(gl, axis=-1, keepdims=True)
        gexp = jnp.where(gmask, jnp.exp(gl - gmax), 0.0)
        g_own = jnp.sum(jnp.where(lane == ea // EXPERTS_PER_GROUP, gexp, 0.0), axis=-1, keepdims=True)
        g_w = g_own / jnp.sum(gexp, axis=-1, keepdims=True)
        la = jnp.sum(jnp.where(lane == ea + E_LANE0, logits, 0.0), axis=-1, keepdims=True)
        lb = jnp.sum(jnp.where(lane == eb + E_LANE0, logits, 0.0), axis=-1, keepdims=True)
        m = jnp.maximum(la, lb)
        pa = jnp.exp(la - m)
        pb = jnp.exp(lb - m)
        inv = g_w / (pa + pb)

        def hidden(x, wg_ref, wu_ref, w):
            g = jnp.dot(x, wg_ref[...], preferred_element_type=F32)
            u = jnp.dot(x, wu_ref[...], preferred_element_type=F32)
            return ((g * _sigmoid(g)) * u * w).astype(BF16)

        part = TMX // EXPERT_ROW_PARTS
        ys = []
        for h in range(EXPERT_ROW_PARTS):
            rows = slice(h * part, (h + 1) * part)
            act_a = hidden(xb[rows], wga_ref, wua_ref, (pa * inv)[rows])
            act_b = hidden(xb[rows], wgb_ref, wub_ref, (pb * inv)[rows])
            y = (jnp.dot(act_a, wda_ref[...], preferred_element_type=F32)
                 + jnp.dot(act_b, wdb_ref[...], preferred_element_type=F32))
            ys.append(y)
        _store_token_major(ys_ref, jnp.concatenate(ys, axis=0))


def _experts(sched, xs, router_w, router_b, wg, wu, wd):
    ea, eb, nv = sched
    n_tiles = ea.shape[0]
    rows = lambda i, ea, eb, nv: (i, 0)
    const2 = lambda i, ea, eb, nv: (0, 0)
    exp_a = lambda i, ea, eb, nv: (ea[i], 0, 0)
    exp_b = lambda i, ea, eb, nv: (eb[i], 0, 0)
    w_in_spec = lambda m: pl.BlockSpec((1, D_MODEL, D_EXPERT), m)
    w_out_spec = lambda m: pl.BlockSpec((1, D_EXPERT, D_MODEL), m)
    return pl.pallas_call(
        _experts_kernel,
        out_shape=jax.ShapeDtypeStruct(xs.shape, F32),
        grid_spec=pltpu.PrefetchScalarGridSpec(
            num_scalar_prefetch=3,
            grid=(n_tiles,),
            in_specs=[pl.BlockSpec((TMX * ROW_TILES, LANES), rows),
                      pl.BlockSpec((D_MODEL, ROUTER_LANES), const2),
                      pl.BlockSpec((1, ROUTER_LANES), const2),
                      w_in_spec(exp_a), w_in_spec(exp_a), w_out_spec(exp_a),
                      w_in_spec(exp_b), w_in_spec(exp_b), w_out_spec(exp_b)],
            out_specs=pl.BlockSpec((TMX * ROW_TILES, LANES), rows),
            scratch_shapes=[pltpu.VMEM((D_MODEL, D_EXPERT), BF16), pltpu.VMEM((D_MODEL, D_EXPERT), BF16),
                            pltpu.VMEM((D_EXPERT, D_MODEL), BF16)] * 2),
        compiler_params=_params(("arbitrary",)),
        name="experts",
    )(ea, eb, nv, xs, router_w, router_b, wg, wu, wd, wg, wu, wd)


def _combine_kernel(dest_ref, ys_ref, x1_ref, mod_ref, gpost_ref, o_ref, ybuf, sems, *, tile0):
    i = pl.program_id(0)
    n = pl.num_programs(0)

    def fetch(tile, slot):
        base = (tile + tile0) * TM

        def start(g, carry):
            for u in range(DMA_UNROLL):
                r = g * DMA_UNROLL + u
                pltpu.make_async_copy(_token_rows(ys_ref, dest_ref[base + r]), _token_rows(ybuf.at[slot], r),
                                      sems.at[slot]).start(priority=u % 2)
            return carry

        lax.fori_loop(0, TM // DMA_UNROLL, start, 0)

    @pl.when(i == 0)
    def _():
        fetch(0, 0)

    @pl.when(i + 1 < n)
    def _():
        fetch(i + 1, (i + 1) % 2)

    slot = i % 2
    pltpu.make_async_copy(ys_ref.at[pl.ds(0, TM * ROW_TILES), :], ybuf.at[slot], sems.at[slot]).wait()
    gate2 = mod_ref[0, :, 5 * D_MODEL:6 * D_MODEL]
    o_ref[...] = x1_ref[...] + _rms(_load_token_major(ybuf.at[slot], TM)) * (gate2 * gpost_ref[...])


def _combine(dest, ys, x1, mod3, g_post, cond_of_tile, tile0):
    n_tok = x1.shape[0]
    return pl.pallas_call(
        functools.partial(_combine_kernel, tile0=tile0),
        out_shape=jax.ShapeDtypeStruct((n_tok, D_MODEL), F32),
        grid_spec=pltpu.PrefetchScalarGridSpec(
            num_scalar_prefetch=1,
            grid=(n_tok // TM,),
            in_specs=[pl.BlockSpec(memory_space=pl.ANY),
                      pl.BlockSpec((TM, D_MODEL), lambda i, d: (i, 0)),
                      pl.BlockSpec((1, 1, 6 * D_MODEL), lambda i, d: (cond_of_tile(i), 0, 0)),
                      pl.BlockSpec((1, D_MODEL), lambda i, d: (0, 0))],
            out_specs=pl.BlockSpec((TM, D_MODEL), lambda i, d: (i, 0)),
            scratch_shapes=[pltpu.VMEM((2, TM * ROW_TILES, LANES), F32), pltpu.SemaphoreType.DMA((2,))]),
        compiler_params=_params(("arbitrary",)),
        name="combine",
    )(dest, ys, x1, mod3, g_post)


def _schedule(bucket, rank, counts):
    n_tok = bucket.shape[0]
    n_max = n_tok // TMX + N_BUCKETS
    cnt = counts[:N_BUCKETS, 0].astype(jnp.int32)
    tiles = (cnt + TMX - 1) // TMX
    tile_end = jnp.cumsum(tiles)
    tile_start = tile_end - tiles
    ids = jnp.arange(N_BUCKETS, dtype=jnp.int32)
    slot0 = jnp.sum(jnp.where(bucket[:, None] == ids[None, :], (tile_start * TMX)[None, :], 0), axis=1)
    dest = slot0 + rank
    i = jnp.arange(n_max, dtype=jnp.int32)
    total = tile_end[-1]
    valid = i < total
    tb = jnp.sum((jnp.minimum(i, total - 1)[:, None] >= tile_end[None, :]).astype(jnp.int32), axis=1)
    pairs = [(a, b) for a in range(EXPERTS_PER_GROUP) for b in range(a + 1, EXPERTS_PER_GROUP)]
    ea_tab = jnp.array([g * EXPERTS_PER_GROUP + a for g in range(N_GROUPS) for a, _ in pairs], jnp.int32)
    eb_tab = jnp.array([g * EXPERTS_PER_GROUP + b for g in range(N_GROUPS) for _, b in pairs], jnp.int32)
    ea = ea_tab[tb]
    eb = eb_tab[tb]
    nv = jnp.where(valid, jnp.clip(cnt[tb] - (i - tile_start[tb]) * TMX, 0, TMX), 0)
    return dest, (ea, eb, nv), n_max * TMX


def _block_diag_gates(rg_wa, rg_wx):
    heads = LANES // HEAD_RNN

    def bd(w):
        w = w.reshape(2, RNN_BLOCKS, heads, HEAD_RNN, HEAD_RNN)
        eye = jnp.eye(heads, dtype=w.dtype)
        full = jnp.einsum('dghij,hk->dghikj', w, eye)
        return full.reshape(2, RNN_BLOCKS, LANES, LANES)

    return jnp.concatenate([bd(rg_wa), bd(rg_wx)], axis=-1).astype(BF16)


def _row_tile(v):
    blocks = v.reshape(v.shape[:-1] + (RNN_BLOCKS, LANES))
    return jnp.concatenate([blocks] * PAIR, axis=-2)


def _to_time_major_state(h):
    return h.reshape(h.shape[0] // PAIR, TMJ_ROWS, LANES)


def kernel(x_prompt, x_sample, state_rglru, c, c_ctx, w_mod, b_mod, g_pre_mix, g_post_mix, g_pre_ffn,
           g_post_ffn, w_in, conv_w, conv_b, rg_wa, rg_ba, rg_wx, rg_bx, rg_lambda, sgu_g, sgu_w, sgu_b,
           w_out, router_g_w, router_g_b, router_e_w, router_e_b, exp_w_gate, exp_w_up, exp_w_down):
    assert w_mod.shape[0] == 1, "single-layer trunk"
    n_ctx, ctx_len, _ = x_prompt.shape
    n_dec, dec_len, _ = x_sample.shape
    l = 0

    n_cond = SUBLANES
    assert n_dec % PAIR == 0 and n_dec + PAIR <= n_cond
    cond = jnp.zeros((n_cond, D_MODEL), F32).at[:n_dec].set(c).at[n_dec:n_dec + PAIR].set(c_ctx)
    mod3 = _modulation(cond, w_mod[l], b_mod[l]).reshape(n_cond, 1, 6 * D_MODEL)
    pos_tab = _pos_table()

    w_in_b = w_in[l].astype(BF16)
    w_out_b = w_out[l].astype(BF16)
    sgu_w_b = sgu_w[l].reshape(2, 4 * CHUNK, CHUNK).astype(BF16)
    sgu_bias_tile = jnp.repeat(sgu_b[l].T, HEAD_SGU, axis=1)
    w_gates = _block_diag_gates(rg_wa[l], rg_wx[l])
    b_gates = 0.5 * jnp.stack([_row_tile(rg_ba[l]), _row_tile(rg_bx[l])], axis=1)
    lam = _row_tile(rg_lambda[l])
    conv_w_t = 0.5 * _row_tile(conv_w[l])
    conv_b_t = 0.5 * _row_tile(conv_b[l])
    router_w = jnp.zeros((D_MODEL, ROUTER_LANES), F32)
    router_w = router_w.at[:, :N_GROUPS].set(router_g_w[l]).at[:, E_LANE0:E_LANE0 + N_EXPERTS].set(router_e_w[l])
    router_w = router_w.astype(BF16)
    router_b = jnp.zeros((1, ROUTER_LANES), F32)
    router_b = router_b.at[0, :N_GROUPS].set(router_g_b[l]).at[0, E_LANE0:E_LANE0 + N_EXPERTS].set(router_e_b[l])
    router_wt = jnp.zeros((ROUTER_ROWS, D_MODEL), F32)
    router_wt = router_wt.at[:N_GROUPS].set(router_g_w[l].T).at[E_ROW0:E_ROW0 + N_EXPERTS].set(router_e_w[l].T)
    router_wt = router_wt.astype(BF16)
    router_bt = jnp.zeros((ROUTER_ROWS,), F32)
    router_bt = router_bt.at[:N_GROUPS].set(router_g_b[l]).at[E_ROW0:E_ROW0 + N_EXPERTS].set(router_e_b[l])
    router_bt = jnp.broadcast_to(router_bt[:, None], (ROUTER_ROWS, LANES))
    earlier = jnp.triu(jnp.ones((TM, TM), BF16), k=1)
    row = lambda v: v.reshape(1, -1)

    n_ctx_tok = n_ctx * ctx_len

    def mixer(x, h0, cond_of_tile, cond_block, use_pos, counts0):
        n_seq, seq_len, _ = x.shape
        xf = x.reshape(n_seq * seq_len, D_MODEL)
        tab = pos_tab if use_pos else None
        xr, gg, y_sgu = _premix(x, mod3, cond_block, row(g_pre_mix[l]), w_in_b, row(sgu_g[l]),
                                sgu_w_b, sgu_bias_tile, tab)
        scan_params = (conv_w_t, conv_b_t, w_gates, b_gates, lam)
        hf, hf_last = _scan(xr, None, None, _to_time_major_state(h0[:, 0]), *scan_params, direction=0)
        y_rnn, hb_first = _scan(xr, gg, hf, _to_time_major_state(h0[:, 1]), *scan_params, direction=1)
        fstate = jnp.stack([hf_last.reshape(n_seq, D_RNN), hb_first.reshape(n_seq, D_RNN)], axis=1)
        x1, hn, bucket, rank, counts = _postmix(
            xf, y_rnn.reshape(n_seq * seq_len, D_RNN), y_sgu.reshape(n_seq * seq_len, D_SGU), mod3,
            cond_of_tile, row(g_post_mix[l]), row(g_pre_ffn[l]), w_out_b, router_wt, router_bt, earlier, tab,
            counts0)
        return x1, hn, bucket, rank, counts, fstate

    tiles_per_seq = dec_len // TM
    ctx_cond = lambda i: n_dec
    dec_cond = lambda i: i // tiles_per_seq
    h0_ctx = jnp.zeros((n_ctx, 2, D_RNN), F32)
    counts0 = jnp.zeros((ROUTER_ROWS, LANES), F32)
    x1_ctx, hn_ctx, bucket_ctx, rank_ctx, counts, st = mixer(x_prompt, h0_ctx, ctx_cond, lambda p: n_dec // PAIR,
                                                             False, counts0)
    new_state = st.astype(state_rglru.dtype)[:, None]
    x1_dec, hn_dec, bucket_dec, rank_dec, counts, _ = mixer(x_sample, state_rglru[:, l].astype(F32), dec_cond,
                                                            lambda p: p, True, counts)

    dest, sched, n_slots = _schedule(jnp.concatenate([bucket_ctx, bucket_dec]),
                                     jnp.concatenate([rank_ctx, rank_dec]), counts)
    xs = _dispatch(dest, hn_ctx, hn_dec, n_slots)
    ys = _experts(sched, xs, router_w, router_b, exp_w_gate[l], exp_w_up[l], exp_w_down[l])
    y_prompt = _combine(dest, ys, x1_ctx, mod3, row(g_post_ffn[l]), ctx_cond, 0)
    y_sample = _combine(dest, ys, x1_dec, mod3, row(g_post_ffn[l]), dec_cond, n_ctx_tok // TM)
    return (y_prompt.reshape(x_prompt.shape), y_sample.reshape(x_sample.shape), new_state)
```

```python
import functools
import math

import jax
import jax.numpy as jnp
from jax import lax
from jax.experimental import pallas as pl
from jax.experimental.pallas import tpu as pltpu

D_MODEL = 1024
D_RNN = 512
D_SGU = 512
N_HEADS_RNN = 8
HEAD_RNN = D_RNN // N_HEADS_RNN
N_HEADS_SGU = 8
HEAD_SGU = D_SGU // N_HEADS_SGU
CHUNK = 128
GRID_W = 64
RG_C = 8.0
N_GROUPS = 4
EXPERTS_PER_GROUP = 4
N_EXPERTS = N_GROUPS * EXPERTS_PER_GROUP
D_EXPERT = 512
EPS = 1e-6
POS_BASE = 10000.0

LANES = 128
SUBLANES = 8
CONV_W = 4
CONV_LEFT = 2
PAIR = 2
RNN_BLOCKS = D_RNN // LANES
TMJ_ROWS = PAIR * RNN_BLOCKS
ROUTER_LANES = LANES
E_LANE0 = N_GROUPS
ROUTER_ROWS = 32
E_ROW0 = SUBLANES

PAIRS_PER_GROUP = EXPERTS_PER_GROUP * (EXPERTS_PER_GROUP - 1) // 2
N_BUCKETS = N_GROUPS * PAIRS_PER_GROUP

ROW_TILES = D_MODEL // LANES

TM = 512
TD = 2048
TC = 1024
DMA_UNROLL = 8
EXPERT_ROW_PARTS = 2
TMX = 512
TT = TM // PAIR
LC = 256
TS = 16
PB = 2
VMEM_LIMIT = 56 * 1024 * 1024

F32 = jnp.float32
BF16 = jnp.bfloat16


def _params(sem):
    return pltpu.CompilerParams(dimension_semantics=sem, vmem_limit_bytes=VMEM_LIMIT)


def _rms(x):
    return x * lax.rsqrt(jnp.mean(x * x, axis=-1, keepdims=True) + EPS)


def _sigmoid(x):
    return 0.5 * jnp.tanh(0.5 * x) + 0.5


def _mod_kernel(cond_ref, w_ref, b_ref, o_ref):
    c = cond_ref[...]
    s = c * _sigmoid(c)
    o_ref[...] = jnp.dot(s.astype(BF16), w_ref[...].astype(BF16),
                         preferred_element_type=F32) + b_ref[...]


def _modulation(cond, w_mod, b_mod):
    n = w_mod.shape[1]
    bn = 1024
    return pl.pallas_call(
        _mod_kernel,
        out_shape=jax.ShapeDtypeStruct((cond.shape[0], n), F32),
        grid=(n // bn,),
        in_specs=[pl.BlockSpec(cond.shape, lambda j: (0, 0)),
                  pl.BlockSpec((D_MODEL, bn), lambda j: (0, j)),
                  pl.BlockSpec((1, bn), lambda j: (0, j))],
        out_specs=pl.BlockSpec((cond.shape[0], bn), lambda j: (0, j)),
        compiler_params=_params(("arbitrary",)),
        name="modulation",
    )(cond, w_mod, b_mod.reshape(1, n))


def _pos_kernel(o_ref):
    n_freq = D_MODEL // 4
    k = lax.broadcasted_iota(jnp.int32, (GRID_W, n_freq), 1).astype(F32)
    p = lax.broadcasted_iota(jnp.int32, (GRID_W, n_freq), 0).astype(F32)
    freq = jnp.exp(-math.log(POS_BASE) * k / n_freq)
    ang = p * freq
    o_ref[:, 0:n_freq] = jnp.sin(ang)
    o_ref[:, n_freq:2 * n_freq] = jnp.cos(ang)


def _pos_table():
    return pl.pallas_call(
        _pos_kernel,
        out_shape=jax.ShapeDtypeStruct((GRID_W, D_MODEL // 2), F32),
        name="pos_table",
    )()


def _add_pos(x, pos_refs, q0):
    if pos_refs is None:
        return x
    rows_ref, cols_ref = pos_refs
    reps = x.shape[0] // GRID_W
    rpart = jnp.concatenate(
        [jnp.broadcast_to(rows_ref[q:q + 1, :], (GRID_W, D_MODEL // 2)) for q in range(q0, q0 + reps)], axis=0)
    cpart = jnp.concatenate([cols_ref[...]] * reps, axis=0)
    return jnp.concatenate([x[:, :D_MODEL // 2] + rpart, x[:, D_MODEL // 2:] + cpart], axis=1)


def _load_x(x_ref, pos_refs, r0, n):
    return _add_pos(x_ref[r0:r0 + n, :], pos_refs, r0 // GRID_W)


def _premix_kernel(*refs, add_pos):
    refs = list(refs)
    x_ref = refs.pop(0)
    pos_refs = (refs.pop(0), refs.pop(0)) if add_pos else None
    mod_ref, g_ref, win_ref, sgug_ref, sguw_ref, sgub_ref, xr_ref, gg_ref, ys_ref = refs
    hn = []
    for s in range(PAIR):
        shift = mod_ref[s, :, 0:D_MODEL]
        scale = mod_ref[s, :, D_MODEL:2 * D_MODEL]
        hn.append(_rms(_add_pos(x_ref[s], pos_refs, 0)) * (g_ref[...] * (1.0 + scale)) + shift)
    z = jnp.dot(jnp.concatenate(hn, axis=0).astype(BF16), win_ref[...],
                preferred_element_type=F32)
    half = D_SGU // 2
    heads_per_half = N_HEADS_SGU // 2
    lane_head = lax.broadcasted_iota(jnp.int32, (CHUNK, half), 1) // HEAD_SGU
    for s in range(PAIR):
        zs = z[s * TT:(s + 1) * TT]
        gg = jax.nn.gelu(zs[:, D_RNN:2 * D_RNN])
        for k in range(RNN_BLOCKS):
            rows = pl.ds(s * RNN_BLOCKS + k, TT, stride=TMJ_ROWS)
            xr_ref[rows, :] = zs[:, k * LANES:(k + 1) * LANES]
            gg_ref[rows, :] = gg[:, k * LANES:(k + 1) * LANES]
        u = zs[:, 2 * D_RNN:2 * D_RNN + D_SGU]
        vn = (_rms(zs[:, 2 * D_RNN + D_SGU:]) * sgug_ref[...]).astype(BF16)
        for c in range(TT // CHUNK):
            rows = slice(c * CHUNK, (c + 1) * CHUNK)
            halves = []
            for hf in range(2):
                r = jnp.dot(sguw_ref[hf], vn[rows, hf * half:(hf + 1) * half],
                            preferred_element_type=F32)
                sel = jnp.zeros((CHUNK, half), F32)
                for h in range(heads_per_half):
                    sel = jnp.where(lane_head == h, r[h * CHUNK:(h + 1) * CHUNK], sel)
                halves.append(sel)
            gatev = jnp.concatenate(halves, axis=1) + sgub_ref[...]
            ys_ref[s, rows, :] = (u[rows] * gatev).astype(BF16)


def _premix(x, mod3, cond_block, g_pre, w_in_b, sgu_g, sgu_w_b, sgu_bias_tile, pos_tab):
    n_seq, seq_len, _ = x.shape
    n_pairs, n_tiles = n_seq // PAIR, seq_len // TT
    add_pos = pos_tab is not None
    const2 = lambda p, j: (0, 0)
    in_specs = [pl.BlockSpec((PAIR, TT, D_MODEL), lambda p, j: (p, j, 0))]
    args = [x]
    if add_pos:
        reps = TT // GRID_W
        in_specs += [pl.BlockSpec((None, reps, D_MODEL // 2), lambda p, j: (j, 0, 0)),
                     pl.BlockSpec((GRID_W, D_MODEL // 2), const2)]
        args += [pos_tab.reshape(GRID_W // reps, reps, D_MODEL // 2), pos_tab]
    in_specs += [pl.BlockSpec((PAIR, 1, 6 * D_MODEL), lambda p, j: (cond_block(p), 0, 0)),
                 pl.BlockSpec((1, D_MODEL), const2),
                 pl.BlockSpec((D_MODEL, 2 * D_RNN + 2 * D_SGU), const2),
                 pl.BlockSpec((1, D_SGU), const2),
                 pl.BlockSpec((2, 4 * CHUNK, CHUNK), lambda p, j: (0, 0, 0)),
                 pl.BlockSpec((CHUNK, D_SGU), const2)]
    args += [mod3, g_pre, w_in_b, sgu_g, sgu_w_b, sgu_bias_tile]
    tmj = jax.ShapeDtypeStruct((n_pairs * seq_len * TMJ_ROWS, LANES), F32)
    tmj_spec = pl.BlockSpec((TT * TMJ_ROWS, LANES), lambda p, j: (p * n_tiles + j, 0))
    xr, gg, y_sgu = pl.pallas_call(
        functools.partial(_premix_kernel, add_pos=add_pos),
        out_shape=(tmj, tmj, jax.ShapeDtypeStruct((n_seq, seq_len, D_SGU), BF16)),
        grid=(n_pairs, n_tiles),
        in_specs=in_specs,
        out_specs=(tmj_spec, tmj_spec, pl.BlockSpec((PAIR, TT, D_SGU), lambda p, j: (p, j, 0))),
        compiler_params=_params(("parallel", "parallel")),
        name="premix",
    )(*args)
    shape4 = (n_pairs, seq_len, TMJ_ROWS, LANES)
    return xr.reshape(shape4), gg.reshape(shape4), y_sgu


def _scan_kernel(*refs, reverse, n_chunks):
    if reverse:
        (xprev_ref, x_ref, xnext_ref, gg_ref, hf_ref, h0_ref, cw_ref, cb_ref, wg_ref, bg_ref, lam_ref,
         y_ref, fs_ref, xwin, xc_s, r_s, i_s, a_s, b_s, y_s, hcar) = refs
    else:
        (xprev_ref, x_ref, xnext_ref, h0_ref, cw_ref, cb_ref, wg_ref, bg_ref, lam_ref,
         hf_ref, fs_ref, xwin, xc_s, r_s, i_s, a_s, b_s, hcar) = refs
    c = pl.program_id(1)
    chunk = n_chunks - 1 - c if reverse else c
    sub_rows = TS * TMJ_ROWS

    def rows_of(pb, t0, n_steps):
        first = (pb * LC + t0) * TMJ_ROWS
        if not isinstance(first, int):
            first = pl.multiple_of(first, TMJ_ROWS)
        return pl.ds(first, n_steps * TMJ_ROWS)

    @pl.when(c == 0)
    def _():
        hcar[...] = h0_ref[...]

    xwin[:, 0:CONV_LEFT] = jnp.where(chunk > 0, xprev_ref[...], 0.0)
    xwin[:, LC + CONV_LEFT:LC + CONV_W - 1] = jnp.where(chunk < n_chunks - 1, xnext_ref[...], 0.0)

    xwin[:, CONV_LEFT:CONV_LEFT + LC] = x_ref[...]

    neg_lam = -lam_ref[...]
    softplus = jnp.maximum(neg_lam, 0.0) + jnp.log(1.0 + jnp.exp(-jnp.abs(neg_lam)))
    half_decay = (-0.5 * RG_C * math.log2(math.e)) * softplus

    def conv(pb):
        for t0 in range(0, LC, TS):
            xc = cb_ref[...] + cw_ref[0] * xwin[pb, t0:t0 + TS]
            for k in range(1, CONV_W):
                xc = xc + cw_ref[k] * xwin[pb, t0 + k:t0 + k + TS]
            xc_s[rows_of(pb, t0, TS), :] = xc.reshape(sub_rows, LANES)

    def gate_matmuls(pb):
        for k in range(RNN_BLOCKS):
            rows = pl.ds(pb * LC * TMJ_ROWS + k, LC * PAIR, stride=RNN_BLOCKS)
            g = jnp.dot(xc_s[rows, :].astype(BF16), wg_ref[k], preferred_element_type=F32)
            r_s[rows, :] = g[:, :LANES]
            i_s[rows, :] = g[:, LANES:]

    def gates(pb):
        for t0 in range(0, LC, TS):
            rows = rows_of(pb, t0, TS)
            tile = lambda ref: ref[rows, :].reshape(TS, TMJ_ROWS, LANES)
            tr = jnp.tanh(tile(r_s) + bg_ref[0])
            ti = jnp.tanh(tile(i_s) + bg_ref[1])
            log2_a = tr * half_decay + half_decay
            a = jnp.exp2(log2_a)
            q = jnp.tanh(log2_a * (-math.log(2.0))) * (a * a + 1.0)
            b = jnp.where(q > 0.0, q * lax.rsqrt(q), 0.0) * ((ti + 1.0) * tile(xc_s))
            a_s[rows, :] = a.reshape(sub_rows, LANES)
            b_s[rows, :] = b.reshape(sub_rows, LANES)

    for stage in (conv, gate_matmuls, gates):
        for pb in range(PB):
            stage(pb)

    def step(j, hs):
        t = LC - 1 - j if reverse else j
        out = []
        for pb in range(PB):
            rows = rows_of(pb, t, 1)
            h = a_s[rows, :] * hs[pb] + b_s[rows, :]
            if reverse:
                y_s[rows, :] = (hf_ref[pb, t] + h) * gg_ref[pb, t]
            else:
                hf_ref[pb, t] = h
            out.append(h)
        return tuple(out)

    hs = lax.fori_loop(0, LC, step, tuple(hcar[pb] for pb in range(PB)), unroll=8)
    for pb in range(PB):
        hcar[pb] = hs[pb]
        fs_ref[pb] = hs[pb]

    if reverse:
        for pb in range(PB):
            for s in range(PAIR):
                cols = [y_s[pl.ds(pb * LC * TMJ_ROWS + s * RNN_BLOCKS + k, LC, stride=TMJ_ROWS), :]
                        for k in range(RNN_BLOCKS)]
                y_ref[pb * PAIR + s] = jnp.concatenate(cols, axis=1).astype(BF16)


def _scan(xr, gg, hf, h0, conv_w, conv_b, w_gates, b_gates, lam, direction):
    n_pairs, seq_len = xr.shape[:2]
    n_chunks = seq_len // LC
    reverse = direction == 1
    pos = (lambda c: n_chunks - 1 - c) if reverse else (lambda c: c)
    tmj_blk = pl.BlockSpec((PB, LC, TMJ_ROWS, LANES), lambda i, c: (i, pos(c), 0, 0))
    state_blk = pl.BlockSpec((PB, TMJ_ROWS, LANES), lambda i, c: (i, 0, 0))
    per_dir = lambda *shape: pl.BlockSpec((None,) + shape, lambda i, c: (direction,) + (0,) * len(shape))
    in_specs = [
        pl.BlockSpec((PB, CONV_LEFT, TMJ_ROWS, LANES),
                     lambda i, c: (i, jnp.maximum(pos(c) * (LC // CONV_LEFT) - 1, 0), 0, 0)),
        tmj_blk,
        pl.BlockSpec((PB, 1, TMJ_ROWS, LANES), lambda i, c: (i, jnp.minimum((pos(c) + 1) * LC, seq_len - 1), 0, 0)),
    ]
    args = [xr, xr, xr]
    if reverse:
        in_specs += [tmj_blk, tmj_blk]
        args += [gg, hf]
    in_specs += [state_blk,
                 pl.BlockSpec((CONV_W, TMJ_ROWS, LANES), lambda i, c: (0, 0, 0)),
                 pl.BlockSpec((TMJ_ROWS, LANES), lambda i, c: (0, 0)),
                 per_dir(RNN_BLOCKS, LANES, 2 * LANES),
                 per_dir(2, TMJ_ROWS, LANES),
                 per_dir(TMJ_ROWS, LANES)]
    args += [h0, conv_w, conv_b, w_gates, b_gates, lam]
    flat = pltpu.VMEM((PB * LC * TMJ_ROWS, LANES), F32)
    scratch = [pltpu.VMEM((PB, LC + CONV_W - 1, TMJ_ROWS, LANES), F32)] + [flat] * (6 if reverse else 5)
    scratch += [pltpu.VMEM((PB, TMJ_ROWS, LANES), F32)]
    state = jax.ShapeDtypeStruct((n_pairs, TMJ_ROWS, LANES), F32)
    if reverse:
        out_shape = (jax.ShapeDtypeStruct((n_pairs * PAIR, seq_len, D_RNN), BF16), state)
        out_specs = (pl.BlockSpec((PB * PAIR, LC, D_RNN), lambda i, c: (i, pos(c), 0)), state_blk)
    else:
        out_shape = (jax.ShapeDtypeStruct(xr.shape, F32), state)
        out_specs = (tmj_blk, state_blk)
    return pl.pallas_call(
        functools.partial(_scan_kernel, reverse=reverse, n_chunks=n_chunks),
        out_shape=out_shape,
        grid=(n_pairs // PB, n_chunks),
        in_specs=in_specs,
        out_specs=out_specs,
        scratch_shapes=scratch,
        compiler_params=_params(("parallel", "arbitrary")),
        name="scan_bwd" if reverse else "scan_fwd",
    )(*args)


def _route(lt):
    n = lt.shape[1]
    row = lax.broadcasted_iota(jnp.int32, (EXPERTS_PER_GROUP, n), 0)
    neg = jnp.float32(-jnp.inf)

    def arg_max(v):
        m = jnp.max(v, axis=0, keepdims=True)
        return jnp.min(jnp.where(v == m, row, EXPERTS_PER_GROUP), axis=0, keepdims=True)

    g_idx = arg_max(lt[0:N_GROUPS])
    el = lt[E_ROW0:E_ROW0 + EXPERTS_PER_GROUP]
    for g in range(1, N_GROUPS):
        first = E_ROW0 + g * EXPERTS_PER_GROUP
        el = jnp.where(g_idx == g, lt[first:first + EXPERTS_PER_GROUP], el)
    i1 = arg_max(el)
    i2 = arg_max(jnp.where(row == i1, neg, el))
    ja = jnp.minimum(i1, i2)
    jb = jnp.maximum(i1, i2)
    pair = (ja * (2 * EXPERTS_PER_GROUP - 1 - ja)) // 2 + (jb - ja - 1)
    return g_idx * PAIRS_PER_GROUP + pair


def _store_token_major(ref, x, t0=0):
    n = x.shape[0]
    for k in range(ROW_TILES):
        ref[pl.ds(t0 * ROW_TILES + k, n, stride=ROW_TILES), :] = x[:, k * LANES:(k + 1) * LANES]


def _load_token_major(ref, n):
    return jnp.concatenate([ref[pl.ds(k, n, stride=ROW_TILES), :] for k in range(ROW_TILES)], axis=1)


def _postmix_kernel(*refs, add_pos):
    refs = list(refs)
    x_ref = refs.pop(0)
    pos_refs = (refs.pop(0), refs.pop(0)) if add_pos else None
    (yr_ref, ys_ref, mod_ref, gpost_ref, gpre_ref, wout_ref, rw_ref, rb_ref, earlier_ref, cnt0_ref,
     x1_ref, hn_ref, rt_ref, cnt_ref, run_ref) = refs

    @pl.when(pl.program_id(0) == 0)
    def _():
        run_ref[...] = cnt0_ref[...]

    gate1 = mod_ref[0, :, 2 * D_MODEL:3 * D_MODEL]
    shift2 = mod_ref[0, :, 3 * D_MODEL:4 * D_MODEL]
    scale2 = mod_ref[0, :, 4 * D_MODEL:5 * D_MODEL]
    y = (jnp.dot(yr_ref[...], wout_ref[0:D_RNN, :], preferred_element_type=F32)
         + jnp.dot(ys_ref[...], wout_ref[D_RNN:, :], preferred_element_type=F32))
    x1 = _load_x(x_ref, pos_refs, 0, TM) + _rms(y) * (gate1 * gpost_ref[...])
    x1_ref[...] = x1
    hn = _rms(x1) * (gpre_ref[...] * (1.0 + scale2)) + shift2
    _store_token_major(hn_ref, hn)
    lt = lax.dot_general(rw_ref[...], hn.astype(BF16), (((1,), (1,)), ((), ())),
                         preferred_element_type=F32) + rb_ref[:, 0:1]
    bucket = _route(lt)
    onehot = lax.broadcasted_iota(jnp.int32, (ROUTER_ROWS, TM), 0) == bucket
    before = jnp.dot(onehot.astype(BF16), earlier_ref[...], preferred_element_type=F32) + run_ref[:, 0:1]
    rank = jnp.sum(jnp.where(onehot, before, 0.0), axis=0, keepdims=True).astype(jnp.int32)
    row = lax.broadcasted_iota(jnp.int32, (SUBLANES, TM), 0)
    rt_ref[...] = jnp.where(row == 0, bucket, jnp.where(row == 1, rank, 0))
    run_ref[...] += jnp.sum(onehot.astype(F32), axis=1, keepdims=True)
    cnt_ref[...] = run_ref[...]


def _postmix(x, y_rnn, y_sgu, mod3, cond_of_tile, g_post, g_pre, w_out_b, router_wt, router_bt, earlier, pos_tab,
             counts0):
    n_tok = x.shape[0]
    add_pos = pos_tab is not None
    tok = lambda i: (i, 0)
    const2 = lambda i: (0, 0)
    in_specs = [pl.BlockSpec((TM, D_MODEL), tok)]
    args = [x]
    if add_pos:
        reps = TM // GRID_W
        tiles_per_seq = GRID_W // reps
        in_specs += [pl.BlockSpec((None, reps, D_MODEL // 2), lambda i: (i % tiles_per_seq, 0, 0)),
                     pl.BlockSpec((GRID_W, D_MODEL // 2), const2)]
        args += [pos_tab.reshape(tiles_per_seq, reps, D_MODEL // 2), pos_tab]
    in_specs += [pl.BlockSpec((TM, D_RNN), tok),
                 pl.BlockSpec((TM, D_SGU), tok),
                 pl.BlockSpec((1, 1, 6 * D_MODEL), lambda i: (cond_of_tile(i), 0, 0)),
                 pl.BlockSpec((1, D_MODEL), const2),
                 pl.BlockSpec((1, D_MODEL), const2),
                 pl.BlockSpec((D_MODEL, D_MODEL), const2),
                 pl.BlockSpec((ROUTER_ROWS, D_MODEL), const2),
                 pl.BlockSpec((ROUTER_ROWS, LANES), const2),
                 pl.BlockSpec((TM, TM), const2),
                 pl.BlockSpec((ROUTER_ROWS, LANES), const2)]
    args += [y_rnn, y_sgu, mod3, g_post, g_pre, w_out_b, router_wt, router_bt, earlier, counts0]
    n_tiles = n_tok // TM
    counts_spec = pl.BlockSpec((ROUTER_ROWS, LANES), const2)
    x1, hn, route, counts = pl.pallas_call(
        functools.partial(_postmix_kernel, add_pos=add_pos),
        out_shape=(jax.ShapeDtypeStruct((n_tok, D_MODEL), F32),
                   jax.ShapeDtypeStruct((n_tok * ROW_TILES, LANES), F32),
                   jax.ShapeDtypeStruct((n_tiles * SUBLANES, TM), jnp.int32),
                   jax.ShapeDtypeStruct((ROUTER_ROWS, LANES), F32)),
        grid=(n_tiles,),
        in_specs=in_specs,
        out_specs=(pl.BlockSpec((TM, D_MODEL), tok),
                   pl.BlockSpec((TM * ROW_TILES, LANES), tok),
                   pl.BlockSpec((SUBLANES, TM), tok),
                   counts_spec),
        scratch_shapes=[pltpu.VMEM((ROUTER_ROWS, LANES), F32)],
        compiler_params=_params(("arbitrary",)),
        name="postmix",
    )(*args)
    route = route.reshape(n_tiles, SUBLANES, TM)
    return x1, hn, route[:, 0].reshape(n_tok), route[:, 1].reshape(n_tok), counts


def _token_rows(ref, t):
    return ref.at[pl.ds(pl.multiple_of(t * ROW_TILES, ROW_TILES), ROW_TILES), :]


def _dispatch_kernel(dest_ref, hc_ref, hs_ref, xs_ref, sem, *, n_ctx_steps):
    i = pl.program_id(0)
    base = i * TD

    def scatter(src_ref):
        def start(g, carry):
            for u in range(DMA_UNROLL):
                r = g * DMA_UNROLL + u
                pltpu.make_async_copy(_token_rows(src_ref, r), _token_rows(xs_ref, dest_ref[base + r]),
                                      sem).start(priority=u % 2)
            return carry

        lax.fori_loop(0, TD // DMA_UNROLL, start, 0)
        pltpu.make_async_copy(src_ref, xs_ref.at[pl.ds(0, TD * ROW_TILES), :], sem).wait()

    @pl.when(i < n_ctx_steps)
    def _():
        scatter(hc_ref)

    @pl.when(i >= n_ctx_steps)
    def _():
        scatter(hs_ref)


def _dispatch(dest, hn_ctx, hn_dec, n_slots):
    n_ctx_steps = hn_ctx.shape[0] // (TD * ROW_TILES)
    n_dec_steps = hn_dec.shape[0] // (TD * ROW_TILES)
    return pl.pallas_call(
        functools.partial(_dispatch_kernel, n_ctx_steps=n_ctx_steps),
        out_shape=jax.ShapeDtypeStruct((n_slots * ROW_TILES, LANES), F32),
        grid_spec=pltpu.PrefetchScalarGridSpec(
            num_scalar_prefetch=1,
            grid=(n_ctx_steps + n_dec_steps,),
            in_specs=[pl.BlockSpec((TD * ROW_TILES, LANES), lambda i, d: (jnp.minimum(i, n_ctx_steps - 1), 0)),
                      pl.BlockSpec((TD * ROW_TILES, LANES), lambda i, d: (jnp.maximum(i - n_ctx_steps, 0), 0))],
            out_specs=pl.BlockSpec(memory_space=pl.ANY),
            scratch_shapes=[pltpu.SemaphoreType.DMA(())]),
        compiler_params=_params(("arbitrary",)),
        name="dispatch",
    )(dest, hn_ctx, hn_dec)


def _experts_kernel(ea_ref, eb_ref, nv_ref, blk_ref, xs_ref, rw_ref, rb_ref, *refs):
    w32_refs, ys_ref, w_refs = refs[:6], refs[6], refs[7:]
    wga_ref, wua_ref, wda_ref, wgb_ref, wub_ref, wdb_ref = w_refs
    i = pl.program_id(0)
    nv = nv_ref[i]
    prev = jnp.maximum(i - 1, 0)

    @pl.when((i == 0) | (ea_ref[i] != ea_ref[prev]) | (eb_ref[i] != eb_ref[prev]))
    def _():
        for w32_ref, w_ref in zip(w32_refs, w_refs):
            w_ref[...] = w32_ref[0].astype(BF16)

    @pl.when(nv > 0)
    def _():
        row = lax.broadcasted_iota(jnp.int32, (TMX, 1), 0)
        xb = jnp.where(row < nv, _load_token_major(xs_ref, TMX), 0.0).astype(BF16)
        logits = jnp.dot(xb, rw_ref[...], preferred_element_type=F32) + rb_ref[...]
        lane = lax.broadcasted_iota(jnp.int32, logits.shape, 1)
        ea = ea_ref[i]
        eb = eb_ref[i]
        gmask = lane < N_GROUPS
        gl = jnp.where(gmask, logits, -jnp.inf)
        gmax = jnp.max(gl, axis=-1, keepdims=True)
        gexp = jnp.where(gmask, jnp.exp(gl - gmax), 0.0)
        g_own = jnp.sum(jnp.where(lane == ea // EXPERTS_PER_GROUP, gexp, 0.0), axis=-1, keepdims=True)
        g_w = g_own / jnp.sum(gexp, axis=-1, keepdims=True)
        la = jnp.sum(jnp.where(lane == ea + E_LANE0, logits, 0.0), axis=-1, keepdims=True)
        lb = jnp.sum(jnp.where(lane == eb + E_LANE0, logits, 0.0), axis=-1, keepdims=True)
        m = jnp.maximum(la, lb)
        pa = jnp.exp(la - m)
        pb = jnp.exp(lb - m)
        inv = g_w / (pa + pb)

        def hidden(x, wg_ref, wu_ref, w):
            g = jnp.dot(x, wg_ref[...], preferred_element_type=F32)
            u = jnp.dot(x, wu_ref[...], preferred_element_type=F32)
            return ((g * _sigmoid(g)) * u * w).astype(BF16)

        part = TMX // EXPERT_ROW_PARTS
        ys = []
        for h in range(EXPERT_ROW_PARTS):
            rows = slice(h * part, (h + 1) * part)
            act_a = hidden(xb[rows], wga_ref, wua_ref, (pa * inv)[rows])
            act_b = hidden(xb[rows], wgb_ref, wub_ref, (pb * inv)[rows])
            y = (jnp.dot(act_a, wda_ref[...], preferred_element_type=F32)
                 + jnp.dot(act_b, wdb_ref[...], preferred_element_type=F32))
            ys.append(y)
        _store_token_major(ys_ref, jnp.concatenate(ys, axis=0))


def _experts(sched, xs, router_w, router_b, wg, wu, wd):
    ea, eb, nv, blk = sched
    n_tiles = ea.shape[0]
    rows = lambda i, ea, eb, nv, blk: (blk[i], 0)
    const2 = lambda i, ea, eb, nv, blk: (0, 0)
    exp_a = lambda i, ea, eb, nv, blk: (ea[i], 0, 0)
    exp_b = lambda i, ea, eb, nv, blk: (eb[i], 0, 0)
    w_in_spec = lambda m: pl.BlockSpec((1, D_MODEL, D_EXPERT), m)
    w_out_spec = lambda m: pl.BlockSpec((1, D_EXPERT, D_MODEL), m)
    return pl.pallas_call(
        _experts_kernel,
        out_shape=jax.ShapeDtypeStruct(xs.shape, F32),
        grid_spec=pltpu.PrefetchScalarGridSpec(
            num_scalar_prefetch=4,
            grid=(n_tiles,),
            in_specs=[pl.BlockSpec((TMX * ROW_TILES, LANES), rows),
                      pl.BlockSpec((D_MODEL, ROUTER_LANES), const2),
                      pl.BlockSpec((1, ROUTER_LANES), const2),
                      w_in_spec(exp_a), w_in_spec(exp_a), w_out_spec(exp_a),
                      w_in_spec(exp_b), w_in_spec(exp_b), w_out_spec(exp_b)],
            out_specs=pl.BlockSpec((TMX * ROW_TILES, LANES), rows),
            scratch_shapes=[pltpu.VMEM((D_MODEL, D_EXPERT), BF16), pltpu.VMEM((D_MODEL, D_EXPERT), BF16),
                            pltpu.VMEM((D_EXPERT, D_MODEL), BF16)] * 2),
        compiler_params=_params(("arbitrary",)),
        name="experts",
    )(ea, eb, nv, blk, xs, router_w, router_b, wg, wu, wd, wg, wu, wd)


def _combine_kernel(dest_ref, ys_ref, x1_ref, mod_ref, gpost_ref, o_ref, ybuf, sems, *, tile0):
    i = pl.program_id(0)
    n = pl.num_programs(0)

    def fetch(tile, slot):
        base = (tile + tile0) * TC

        def start(g, carry):
            for u in range(DMA_UNROLL):
                r = g * DMA_UNROLL + u
                pltpu.make_async_copy(_token_rows(ys_ref, dest_ref[base + r]), _token_rows(ybuf.at[slot], r),
                                      sems.at[slot]).start(priority=u % 2)
            return carry

        lax.fori_loop(0, TC // DMA_UNROLL, start, 0)

    @pl.when(i == 0)
    def _():
        fetch(0, 0)

    @pl.when(i + 1 < n)
    def _():
        fetch(i + 1, (i + 1) % 2)

    slot = i % 2
    pltpu.make_async_copy(ys_ref.at[pl.ds(0, TC * ROW_TILES), :], ybuf.at[slot], sems.at[slot]).wait()
    gate2 = mod_ref[0, :, 5 * D_MODEL:6 * D_MODEL]
    o_ref[...] = x1_ref[...] + _rms(_load_token_major(ybuf.at[slot], TC)) * (gate2 * gpost_ref[...])


def _combine(dest, ys, x1, mod3, g_post, cond_of_tile, tile0):
    n_tok = x1.shape[0]
    return pl.pallas_call(
        functools.partial(_combine_kernel, tile0=tile0),
        out_shape=jax.ShapeDtypeStruct((n_tok, D_MODEL), F32),
        grid_spec=pltpu.PrefetchScalarGridSpec(
            num_scalar_prefetch=1,
            grid=(n_tok // TC,),
            in_specs=[pl.BlockSpec(memory_space=pl.ANY),
                      pl.BlockSpec((TC, D_MODEL), lambda i, d: (i, 0)),
                      pl.BlockSpec((1, 1, 6 * D_MODEL), lambda i, d: (cond_of_tile(i), 0, 0)),
                      pl.BlockSpec((1, D_MODEL), lambda i, d: (0, 0))],
            out_specs=pl.BlockSpec((TC, D_MODEL), lambda i, d: (i, 0)),
            scratch_shapes=[pltpu.VMEM((2, TC * ROW_TILES, LANES), F32), pltpu.SemaphoreType.DMA((2,))]),
        compiler_params=_params(("arbitrary",)),
        name="combine",
    )(dest, ys, x1, mod3, g_post)


def _schedule(bucket, rank, counts):
    n_tok = bucket.shape[0]
    n_max = n_tok // TMX + N_BUCKETS
    cnt = counts[:N_BUCKETS, 0].astype(jnp.int32)
    tiles = (cnt + TMX - 1) // TMX
    tile_end = jnp.cumsum(tiles)
    tile_start = tile_end - tiles
    ids = jnp.arange(N_BUCKETS, dtype=jnp.int32)
    slot0 = jnp.sum(jnp.where(bucket[:, None] == ids[None, :], (tile_start * TMX)[None, :], 0), axis=1)
    dest = slot0 + rank
    i = jnp.arange(n_max, dtype=jnp.int32)
    total = tile_end[-1]
    valid = i < total
    tb = jnp.sum((jnp.minimum(i, total - 1)[:, None] >= tile_end[None, :]).astype(jnp.int32), axis=1)
    pairs = [(a, b) for a in range(EXPERTS_PER_GROUP) for b in range(a + 1, EXPERTS_PER_GROUP)]
    ea_tab = jnp.array([g * EXPERTS_PER_GROUP + a for g in range(N_GROUPS) for a, _ in pairs], jnp.int32)
    eb_tab = jnp.array([g * EXPERTS_PER_GROUP + b for g in range(N_GROUPS) for _, b in pairs], jnp.int32)
    ea = ea_tab[tb]
    eb = eb_tab[tb]
    nv = jnp.where(valid, jnp.clip(cnt[tb] - (i - tile_start[tb]) * TMX, 0, TMX), 0)
    blk = jnp.minimum(i, total - 1)
    return dest, (ea, eb, nv, blk), n_max * TMX


def _block_diag_gates(rg_wa, rg_wx):
    heads = LANES // HEAD_RNN

    def bd(w):
        w = w.reshape(2, RNN_BLOCKS, heads, HEAD_RNN, HEAD_RNN)
        eye = jnp.eye(heads, dtype=w.dtype)
        full = jnp.einsum('dghij,hk->dghikj', w, eye)
        return full.reshape(2, RNN_BLOCKS, LANES, LANES)

    return jnp.concatenate([bd(rg_wa), bd(rg_wx)], axis=-1).astype(BF16)


def _row_tile(v):
    blocks = v.reshape(v.shape[:-1] + (RNN_BLOCKS, LANES))
    return jnp.concatenate([blocks] * PAIR, axis=-2)


def _to_time_major_state(h):
    return h.reshape(h.shape[0] // PAIR, TMJ_ROWS, LANES)


def kernel(x_prompt, x_sample, state_rglru, c, c_ctx, w_mod, b_mod, g_pre_mix, g_post_mix, g_pre_ffn,
           g_post_ffn, w_in, conv_w, conv_b, rg_wa, rg_ba, rg_wx, rg_bx, rg_lambda, sgu_g, sgu_w, sgu_b,
           w_out, router_g_w, router_g_b, router_e_w, router_e_b, exp_w_gate, exp_w_up, exp_w_down):
    assert w_mod.shape[0] == 1, "single-layer trunk"
    n_ctx, ctx_len, _ = x_prompt.shape
    n_dec, dec_len, _ = x_sample.shape
    l = 0

    n_cond = SUBLANES
    assert n_dec % PAIR == 0 and n_dec + PAIR <= n_cond
    cond = jnp.concatenate([c, jnp.broadcast_to(c_ctx, (PAIR, D_MODEL)),
                            jnp.zeros((n_cond - n_dec - PAIR, D_MODEL), F32)], axis=0)
    mod3 = _modulation(cond, w_mod[l], b_mod[l]).reshape(n_cond, 1, 6 * D_MODEL)
    pos_tab = _pos_table()

    w_in_b = w_in[l].astype(BF16)
    w_out_b = w_out[l].astype(BF16)
    sgu_w_b = sgu_w[l].reshape(2, 4 * CHUNK, CHUNK).astype(BF16)
    sgu_bias_tile = jnp.repeat(sgu_b[l].T, HEAD_SGU, axis=1)
    w_gates = _block_diag_gates(rg_wa[l], rg_wx[l])
    b_gates = 0.5 * jnp.stack([_row_tile(rg_ba[l]), _row_tile(rg_bx[l])], axis=1)
    lam = _row_tile(rg_lambda[l])
    conv_w_t = 0.5 * _row_tile(conv_w[l])
    conv_b_t = 0.5 * _row_tile(conv_b[l])
    lane_pad = ROUTER_LANES - E_LANE0 - N_EXPERTS
    router_w = jnp.pad(jnp.concatenate([router_g_w[l], router_e_w[l]], axis=1), ((0, 0), (0, lane_pad))).astype(BF16)
    router_b = jnp.pad(jnp.concatenate([router_g_b[l], router_e_b[l]]), (0, lane_pad)).reshape(1, ROUTER_LANES)
    gap, tail = E_ROW0 - N_GROUPS, ROUTER_ROWS - E_ROW0 - N_EXPERTS
    router_wt = jnp.concatenate([router_g_w[l].T, jnp.zeros((gap, D_MODEL), F32), router_e_w[l].T,
                                 jnp.zeros((tail, D_MODEL), F32)], axis=0).astype(BF16)
    router_bt = jnp.concatenate([router_g_b[l], jnp.zeros((gap,), F32), router_e_b[l], jnp.zeros((tail,), F32)])
    router_bt = jnp.broadcast_to(router_bt[:, None], (ROUTER_ROWS, LANES))
    earlier = jnp.triu(jnp.ones((TM, TM), BF16), k=1)
    row = lambda v: v.reshape(1, -1)

    n_ctx_tok = n_ctx * ctx_len

    def mixer(x, h0, cond_of_tile, cond_block, use_pos, counts0):
        n_seq, seq_len, _ = x.shape
        xf = x.reshape(n_seq * seq_len, D_MODEL)
        tab = pos_tab if use_pos else None
        xr, gg, y_sgu = _premix(x, mod3, cond_block, row(g_pre_mix[l]), w_in_b, row(sgu_g[l]),
                                sgu_w_b, sgu_bias_tile, tab)
        scan_params = (conv_w_t, conv_b_t, w_gates, b_gates, lam)
        hf, hf_last = _scan(xr, None, None, _to_time_major_state(h0[:, 0]), *scan_params, direction=0)
        y_rnn, hb_first = _scan(xr, gg, hf, _to_time_major_state(h0[:, 1]), *scan_params, direction=1)
        fstate = jnp.stack([hf_last.reshape(n_seq, D_RNN), hb_first.reshape(n_seq, D_RNN)], axis=1)
        x1, hn, bucket, rank, counts = _postmix(
            xf, y_rnn.reshape(n_seq * seq_len, D_RNN), y_sgu.reshape(n_seq * seq_len, D_SGU), mod3,
            cond_of_tile, row(g_post_mix[l]), row(g_pre_ffn[l]), w_out_b, router_wt, router_bt, earlier, tab,
            counts0)
        return x1, hn, bucket, rank, counts, fstate

    tiles_per_seq = dec_len // TM
    ctx_cond = lambda i: n_dec
    dec_cond = lambda i: i // tiles_per_seq
    h0_ctx = jnp.zeros((n_ctx, 2, D_RNN), F32)
    counts0 = jnp.zeros((ROUTER_ROWS, LANES), F32)
    x1_ctx, hn_ctx, bucket_ctx, rank_ctx, counts, st = mixer(x_prompt, h0_ctx, ctx_cond, lambda p: n_dec // PAIR,
                                                             False, counts0)
    new_state = st.astype(state_rglru.dtype)[:, None]
    x1_dec, hn_dec, bucket_dec, rank_dec, counts, _ = mixer(x_sample, state_rglru[:, l].astype(F32), dec_cond,
                                                            lambda p: p, True, counts)

    dest, sched, n_slots = _schedule(jnp.concatenate([bucket_ctx, bucket_dec]),
                                     jnp.concatenate([rank_ctx, rank_dec]), counts)
    xs = _dispatch(dest, hn_ctx, hn_dec, n_slots)
    ys = _experts(sched, xs, router_w, router_b, exp_w_gate[l], exp_w_up[l], exp_w_down[l])
    y_prompt = _combine(dest, ys, x1_ctx, mod3, row(g_post_ffn[l]), ctx_cond, 0)
    y_sample = _combine(dest, ys, x1_dec, mod3, row(g_post_ffn[l]), lambda i: i // (dec_len // TC), n_ctx_tok // TC)
    return (y_prompt.reshape(x_prompt.shape), y_sample.reshape(x_sample.shape), new_state)
```

```python
import functools
import math

import jax
import jax.numpy as jnp
from jax import lax
from jax.experimental import pallas as pl
from jax.experimental.pallas import tpu as pltpu

D_MODEL = 1024
D_RNN = 512
D_SGU = 512
N_HEADS_RNN = 8
HEAD_RNN = D_RNN // N_HEADS_RNN
N_HEADS_SGU = 8
HEAD_SGU = D_SGU // N_HEADS_SGU
CHUNK = 128
GRID_W = 64
RG_C = 8.0
N_GROUPS = 4
EXPERTS_PER_GROUP = 4
N_EXPERTS = N_GROUPS * EXPERTS_PER_GROUP
D_EXPERT = 512
EPS = 1e-6
POS_BASE = 10000.0

LANES = 128
SUBLANES = 8
CONV_W = 4
CONV_LEFT = 2
PAIR = 2
RNN_BLOCKS = D_RNN // LANES
TMJ_ROWS = PAIR * RNN_BLOCKS
ROUTER_LANES = LANES
E_LANE0 = N_GROUPS
ROUTER_ROWS = 32
E_ROW0 = SUBLANES

PAIRS_PER_GROUP = EXPERTS_PER_GROUP * (EXPERTS_PER_GROUP - 1) // 2
N_BUCKETS = N_GROUPS * PAIRS_PER_GROUP

ROW_TILES = D_MODEL // LANES

TM = 512
TD = 2048
TC = 512
DMA_UNROLL = 8
EXPERT_ROW_PARTS = 2
TMX = 512
TT = TM // PAIR
LC = 256
TS = 16
PB = 2
VMEM_LIMIT = 56 * 1024 * 1024

F32 = jnp.float32
BF16 = jnp.bfloat16


def _params(sem):
    return pltpu.CompilerParams(dimension_semantics=sem, vmem_limit_bytes=VMEM_LIMIT)


def _rms(x):
    return x * lax.rsqrt(jnp.mean(x * x, axis=-1, keepdims=True) + EPS)


def _sigmoid(x):
    return 0.5 * jnp.tanh(0.5 * x) + 0.5


def _mod_kernel(cond_ref, w_ref, b_ref, o_ref):
    c = cond_ref[...]
    s = c * _sigmoid(c)
    o_ref[...] = jnp.dot(s.astype(BF16), w_ref[...].astype(BF16),
                         preferred_element_type=F32) + b_ref[...]


def _modulation(cond, w_mod, b_mod):
    n = w_mod.shape[1]
    bn = 1024
    return pl.pallas_call(
        _mod_kernel,
        out_shape=jax.ShapeDtypeStruct((cond.shape[0], n), F32),
        grid=(n // bn,),
        in_specs=[pl.BlockSpec(cond.shape, lambda j: (0, 0)),
                  pl.BlockSpec((D_MODEL, bn), lambda j: (0, j)),
                  pl.BlockSpec((1, bn), lambda j: (0, j))],
        out_specs=pl.BlockSpec((cond.shape[0], bn), lambda j: (0, j)),
        compiler_params=_params(("arbitrary",)),
        name="modulation",
    )(cond, w_mod, b_mod.reshape(1, n))


def _pos_kernel(o_ref):
    n_freq = D_MODEL // 4
    k = lax.broadcasted_iota(jnp.int32, (GRID_W, n_freq), 1).astype(F32)
    p = lax.broadcasted_iota(jnp.int32, (GRID_W, n_freq), 0).astype(F32)
    freq = jnp.exp(-math.log(POS_BASE) * k / n_freq)
    ang = p * freq
    o_ref[:, 0:n_freq] = jnp.sin(ang)
    o_ref[:, n_freq:2 * n_freq] = jnp.cos(ang)


def _pos_table():
    return pl.pallas_call(
        _pos_kernel,
        out_shape=jax.ShapeDtypeStruct((GRID_W, D_MODEL // 2), F32),
        name="pos_table",
    )()


def _add_pos(x, pos_refs, q0):
    if pos_refs is None:
        return x
    rows_ref, cols_ref = pos_refs
    reps = x.shape[0] // GRID_W
    rpart = jnp.concatenate(
        [jnp.broadcast_to(rows_ref[q:q + 1, :], (GRID_W, D_MODEL // 2)) for q in range(q0, q0 + reps)], axis=0)
    cpart = jnp.concatenate([cols_ref[...]] * reps, axis=0)
    return jnp.concatenate([x[:, :D_MODEL // 2] + rpart, x[:, D_MODEL // 2:] + cpart], axis=1)


def _load_x(x_ref, pos_refs, r0, n):
    return _add_pos(x_ref[r0:r0 + n, :], pos_refs, r0 // GRID_W)


def _premix_kernel(*refs, add_pos):
    refs = list(refs)
    x_ref = refs.pop(0)
    pos_refs = (refs.pop(0), refs.pop(0)) if add_pos else None
    mod_ref, g_ref, win_ref, sgug_ref, sguw_ref, sgub_ref, xr_ref, gg_ref, ys_ref = refs
    hn = []
    for s in range(PAIR):
        shift = mod_ref[s, :, 0:D_MODEL]
        scale = mod_ref[s, :, D_MODEL:2 * D_MODEL]
        hn.append(_rms(_add_pos(x_ref[s], pos_refs, 0)) * (g_ref[...] * (1.0 + scale)) + shift)
    z = jnp.dot(jnp.concatenate(hn, axis=0).astype(BF16), win_ref[...],
                preferred_element_type=F32)
    half = D_SGU // 2
    heads_per_half = N_HEADS_SGU // 2
    lane_head = lax.broadcasted_iota(jnp.int32, (CHUNK, half), 1) // HEAD_SGU
    for s in range(PAIR):
        zs = z[s * TT:(s + 1) * TT]
        vn = (_rms(zs[:, 0:D_SGU]) * sgug_ref[...]).astype(BF16)
        u = zs[:, D_SGU:2 * D_SGU]
        for c in range(TT // CHUNK):
            rows = slice(c * CHUNK, (c + 1) * CHUNK)
            halves = []
            for hf in range(2):
                r = jnp.dot(sguw_ref[hf], vn[rows, hf * half:(hf + 1) * half],
                            preferred_element_type=F32)
                sel = jnp.zeros((CHUNK, half), F32)
                for h in range(heads_per_half):
                    sel = jnp.where(lane_head == h, r[h * CHUNK:(h + 1) * CHUNK], sel)
                halves.append(sel)
            gatev = jnp.concatenate(halves, axis=1) + sgub_ref[...]
            ys_ref[s, rows, :] = (u[rows] * gatev).astype(BF16)
    for s in range(PAIR):
        zs = z[s * TT:(s + 1) * TT]
        gg = jax.nn.gelu(zs[:, 2 * D_SGU:2 * D_SGU + D_RNN])
        xr = zs[:, 2 * D_SGU + D_RNN:]
        for k in range(RNN_BLOCKS):
            rows = pl.ds(s * RNN_BLOCKS + k, TT, stride=TMJ_ROWS)
            xr_ref[rows, :] = xr[:, k * LANES:(k + 1) * LANES]
            gg_ref[rows, :] = gg[:, k * LANES:(k + 1) * LANES]


def _premix(x, mod3, cond_block, g_pre, w_in_b, sgu_g, sgu_w_b, sgu_bias_tile, pos_tab):
    n_seq, seq_len, _ = x.shape
    n_pairs, n_tiles = n_seq // PAIR, seq_len // TT
    add_pos = pos_tab is not None
    const2 = lambda p, j: (0, 0)
    in_specs = [pl.BlockSpec((PAIR, TT, D_MODEL), lambda p, j: (p, j, 0))]
    args = [x]
    if add_pos:
        reps = TT // GRID_W
        in_specs += [pl.BlockSpec((None, reps, D_MODEL // 2), lambda p, j: (j, 0, 0)),
                     pl.BlockSpec((GRID_W, D_MODEL // 2), const2)]
        args += [pos_tab.reshape(GRID_W // reps, reps, D_MODEL // 2), pos_tab]
    in_specs += [pl.BlockSpec((PAIR, 1, 6 * D_MODEL), lambda p, j: (cond_block(p), 0, 0)),
                 pl.BlockSpec((1, D_MODEL), const2),
                 pl.BlockSpec((D_MODEL, 2 * D_RNN + 2 * D_SGU), const2),
                 pl.BlockSpec((1, D_SGU), const2),
                 pl.BlockSpec((2, 4 * CHUNK, CHUNK), lambda p, j: (0, 0, 0)),
                 pl.BlockSpec((CHUNK, D_SGU), const2)]
    args += [mod3, g_pre, w_in_b, sgu_g, sgu_w_b, sgu_bias_tile]
    tmj = jax.ShapeDtypeStruct((n_pairs * seq_len * TMJ_ROWS, LANES), F32)
    tmj_spec = pl.BlockSpec((TT * TMJ_ROWS, LANES), lambda p, j: (p * n_tiles + j, 0))
    xr, gg, y_sgu = pl.pallas_call(
        functools.partial(_premix_kernel, add_pos=add_pos),
        out_shape=(tmj, tmj, jax.ShapeDtypeStruct((n_seq, seq_len, D_SGU), BF16)),
        grid=(n_pairs, n_tiles),
        in_specs=in_specs,
        out_specs=(tmj_spec, tmj_spec, pl.BlockSpec((PAIR, TT, D_SGU), lambda p, j: (p, j, 0))),
        compiler_params=_params(("parallel", "parallel")),
        name="premix",
    )(*args)
    shape4 = (n_pairs, seq_len, TMJ_ROWS, LANES)
    return xr.reshape(shape4), gg.reshape(shape4), y_sgu


def _scan_kernel(*refs, reverse, n_chunks):
    if reverse:
        (xprev_ref, x_ref, xnext_ref, gg_ref, hf_ref, h0_ref, cw_ref, cb_ref, wg_ref, bg_ref, lam_ref,
         y_ref, fs_ref, xwin, xc_s, r_s, i_s, a_s, b_s, y_s, hcar) = refs
    else:
        (xprev_ref, x_ref, xnext_ref, h0_ref, cw_ref, cb_ref, wg_ref, bg_ref, lam_ref,
         hf_ref, fs_ref, xwin, xc_s, r_s, i_s, a_s, b_s, hcar) = refs
    c = pl.program_id(1)
    chunk = n_chunks - 1 - c if reverse else c
    sub_rows = TS * TMJ_ROWS

    def rows_of(pb, t0, n_steps):
        first = (pb * LC + t0) * TMJ_ROWS
        if not isinstance(first, int):
            first = pl.multiple_of(first, TMJ_ROWS)
        return pl.ds(first, n_steps * TMJ_ROWS)

    @pl.when(c == 0)
    def _():
        hcar[...] = h0_ref[...]

    xwin[:, 0:CONV_LEFT] = jnp.where(chunk > 0, xprev_ref[...], 0.0)
    xwin[:, LC + CONV_LEFT:LC + CONV_W - 1] = jnp.where(chunk < n_chunks - 1, xnext_ref[...], 0.0)

    xwin[:, CONV_LEFT:CONV_LEFT + LC] = x_ref[...]

    neg_lam = -lam_ref[...]
    softplus = jnp.maximum(neg_lam, 0.0) + jnp.log(1.0 + jnp.exp(-jnp.abs(neg_lam)))
    half_decay = (-0.5 * RG_C * math.log2(math.e)) * softplus

    def conv(pb):
        for t0 in range(0, LC, TS):
            xc = cb_ref[...] + cw_ref[0] * xwin[pb, t0:t0 + TS]
            for k in range(1, CONV_W):
                xc = xc + cw_ref[k] * xwin[pb, t0 + k:t0 + k + TS]
            xc_s[rows_of(pb, t0, TS), :] = xc.reshape(sub_rows, LANES)

    def gate_matmuls(pb):
        for k in range(RNN_BLOCKS):
            rows = pl.ds(pb * LC * TMJ_ROWS + k, LC * PAIR, stride=RNN_BLOCKS)
            g = jnp.dot(xc_s[rows, :].astype(BF16), wg_ref[k], preferred_element_type=F32)
            r_s[rows, :] = g[:, :LANES]
            i_s[rows, :] = g[:, LANES:]

    def gates(pb):
        for t0 in range(0, LC, TS):
            rows = rows_of(pb, t0, TS)
            tile = lambda ref: ref[rows, :].reshape(TS, TMJ_ROWS, LANES)
            tr = jnp.tanh(tile(r_s) + bg_ref[0])
            ti = jnp.tanh(tile(i_s) + bg_ref[1])
            log2_a = tr * half_decay + half_decay
            a = jnp.exp2(log2_a)
            q = jnp.tanh(log2_a * (-math.log(2.0))) * (a * a + 1.0)
            b = jnp.where(q > 0.0, q * lax.rsqrt(q), 0.0) * ((ti + 1.0) * tile(xc_s))
            a_s[rows, :] = a.reshape(sub_rows, LANES)
            b_s[rows, :] = b.reshape(sub_rows, LANES)

    for stage in (conv, gate_matmuls, gates):
        for pb in range(PB):
            stage(pb)

    def step(j, hs):
        t = LC - 1 - j if reverse else j
        out = []
        for pb in range(PB):
            rows = rows_of(pb, t, 1)
            h = a_s[rows, :] * hs[pb] + b_s[rows, :]
            if reverse:
                y_s[rows, :] = (hf_ref[pb, t] + h) * gg_ref[pb, t]
            else:
                hf_ref[pb, t] = h
            out.append(h)
        return tuple(out)

    hs = lax.fori_loop(0, LC, step, tuple(hcar[pb] for pb in range(PB)), unroll=8)
    for pb in range(PB):
        hcar[pb] = hs[pb]
        fs_ref[pb] = hs[pb]

    if reverse:
        for pb in range(PB):
            for s in range(PAIR):
                cols = [y_s[pl.ds(pb * LC * TMJ_ROWS + s * RNN_BLOCKS + k, LC, stride=TMJ_ROWS), :]
                        for k in range(RNN_BLOCKS)]
                y_ref[pb * PAIR + s] = jnp.concatenate(cols, axis=1).astype(BF16)


def _scan(xr, gg, hf, h0, conv_w, conv_b, w_gates, b_gates, lam, direction):
    n_pairs, seq_len = xr.shape[:2]
    n_chunks = seq_len // LC
    reverse = direction == 1
    pos = (lambda c: n_chunks - 1 - c) if reverse else (lambda c: c)
    tmj_blk = pl.BlockSpec((PB, LC, TMJ_ROWS, LANES), lambda i, c: (i, pos(c), 0, 0))
    state_blk = pl.BlockSpec((PB, TMJ_ROWS, LANES), lambda i, c: (i, 0, 0))
    per_dir = lambda *shape: pl.BlockSpec((None,) + shape, lambda i, c: (direction,) + (0,) * len(shape))
    in_specs = [
        pl.BlockSpec((PB, CONV_LEFT, TMJ_ROWS, LANES),
                     lambda i, c: (i, jnp.maximum(pos(c) * (LC // CONV_LEFT) - 1, 0), 0, 0)),
        tmj_blk,
        pl.BlockSpec((PB, 1, TMJ_ROWS, LANES), lambda i, c: (i, jnp.minimum((pos(c) + 1) * LC, seq_len - 1), 0, 0)),
    ]
    args = [xr, xr, xr]
    if reverse:
        in_specs += [tmj_blk, tmj_blk]
        args += [gg, hf]
    in_specs += [state_blk,
                 pl.BlockSpec((CONV_W, TMJ_ROWS, LANES), lambda i, c: (0, 0, 0)),
                 pl.BlockSpec((TMJ_ROWS, LANES), lambda i, c: (0, 0)),
                 per_dir(RNN_BLOCKS, LANES, 2 * LANES),
                 per_dir(2, TMJ_ROWS, LANES),
                 per_dir(TMJ_ROWS, LANES)]
    args += [h0, conv_w, conv_b, w_gates, b_gates, lam]
    flat = pltpu.VMEM((PB * LC * TMJ_ROWS, LANES), F32)
    scratch = [pltpu.VMEM((PB, LC + CONV_W - 1, TMJ_ROWS, LANES), F32)] + [flat] * (6 if reverse else 5)
    scratch += [pltpu.VMEM((PB, TMJ_ROWS, LANES), F32)]
    state = jax.ShapeDtypeStruct((n_pairs, TMJ_ROWS, LANES), F32)
    if reverse:
        out_shape = (jax.ShapeDtypeStruct((n_pairs * PAIR, seq_len, D_RNN), BF16), state)
        out_specs = (pl.BlockSpec((PB * PAIR, LC, D_RNN), lambda i, c: (i, pos(c), 0)), state_blk)
    else:
        out_shape = (jax.ShapeDtypeStruct(xr.shape, F32), state)
        out_specs = (tmj_blk, state_blk)
    return pl.pallas_call(
        functools.partial(_scan_kernel, reverse=reverse, n_chunks=n_chunks),
        out_shape=out_shape,
        grid=(n_pairs // PB, n_chunks),
        in_specs=in_specs,
        out_specs=out_specs,
        scratch_shapes=scratch,
        compiler_params=_params(("parallel", "arbitrary")),
        name="scan_bwd" if reverse else "scan_fwd",
    )(*args)


def _route(lt):
    n = lt.shape[1]
    row = lax.broadcasted_iota(jnp.int32, (EXPERTS_PER_GROUP, n), 0)
    neg = jnp.float32(-jnp.inf)

    def arg_max(v):
        m = jnp.max(v, axis=0, keepdims=True)
        return jnp.min(jnp.where(v == m, row, EXPERTS_PER_GROUP), axis=0, keepdims=True)

    g_idx = arg_max(lt[0:N_GROUPS])
    el = lt[E_ROW0:E_ROW0 + EXPERTS_PER_GROUP]
    for g in range(1, N_GROUPS):
        first = E_ROW0 + g * EXPERTS_PER_GROUP
        el = jnp.where(g_idx == g, lt[first:first + EXPERTS_PER_GROUP], el)
    i1 = arg_max(el)
    i2 = arg_max(jnp.where(row == i1, neg, el))
    ja = jnp.minimum(i1, i2)
    jb = jnp.maximum(i1, i2)
    pair = (ja * (2 * EXPERTS_PER_GROUP - 1 - ja)) // 2 + (jb - ja - 1)
    return g_idx * PAIRS_PER_GROUP + pair


def _store_token_major(ref, x, t0=0):
    n = x.shape[0]
    for k in range(ROW_TILES):
        ref[pl.ds(t0 * ROW_TILES + k, n, stride=ROW_TILES), :] = x[:, k * LANES:(k + 1) * LANES]


def _load_token_major(ref, n):
    return jnp.concatenate([ref[pl.ds(k, n, stride=ROW_TILES), :] for k in range(ROW_TILES)], axis=1)


def _postmix_kernel(*refs, add_pos):
    refs = list(refs)
    x_ref = refs.pop(0)
    pos_refs = (refs.pop(0), refs.pop(0)) if add_pos else None
    (yr_ref, ys_ref, mod_ref, gpost_ref, gpre_ref, wout_ref, rw_ref, rb_ref, earlier_ref, cnt0_ref,
     x1_ref, hn_ref, rt_ref, cnt_ref, run_ref) = refs

    @pl.when(pl.program_id(0) == 0)
    def _():
        run_ref[...] = cnt0_ref[...]

    gate1 = mod_ref[0, :, 2 * D_MODEL:3 * D_MODEL]
    shift2 = mod_ref[0, :, 3 * D_MODEL:4 * D_MODEL]
    scale2 = mod_ref[0, :, 4 * D_MODEL:5 * D_MODEL]
    y = (jnp.dot(yr_ref[...], wout_ref[0:D_RNN, :], preferred_element_type=F32)
         + jnp.dot(ys_ref[...], wout_ref[D_RNN:, :], preferred_element_type=F32))
    x1 = _load_x(x_ref, pos_refs, 0, TM) + _rms(y) * (gate1 * gpost_ref[...])
    x1_ref[...] = x1
    hn = _rms(x1) * (gpre_ref[...] * (1.0 + scale2)) + shift2
    _store_token_major(hn_ref, hn)
    lt = lax.dot_general(rw_ref[...], hn.astype(BF16), (((1,), (1,)), ((), ())),
                         preferred_element_type=F32) + rb_ref[:, 0:1]
    bucket = _route(lt)
    onehot = lax.broadcasted_iota(jnp.int32, (ROUTER_ROWS, TM), 0) == bucket
    before = jnp.dot(onehot.astype(BF16), earlier_ref[...], preferred_element_type=F32) + run_ref[:, 0:1]
    rank = jnp.sum(jnp.where(onehot, before, 0.0), axis=0, keepdims=True).astype(jnp.int32)
    row = lax.broadcasted_iota(jnp.int32, (SUBLANES, TM), 0)
    rt_ref[...] = jnp.where(row == 0, bucket, jnp.where(row == 1, rank, 0))
    run_ref[...] += jnp.sum(onehot.astype(F32), axis=1, keepdims=True)
    cnt_ref[...] = run_ref[...]


def _postmix(x, y_rnn, y_sgu, mod3, cond_of_tile, g_post, g_pre, w_out_b, router_wt, router_bt, earlier, pos_tab,
             counts0):
    n_tok = x.shape[0]
    add_pos = pos_tab is not None
    tok = lambda i: (i, 0)
    const2 = lambda i: (0, 0)
    in_specs = [pl.BlockSpec((TM, D_MODEL), tok)]
    args = [x]
    if add_pos:
        reps = TM // GRID_W
        tiles_per_seq = GRID_W // reps
        in_specs += [pl.BlockSpec((None, reps, D_MODEL // 2), lambda i: (i % tiles_per_seq, 0, 0)),
                     pl.BlockSpec((GRID_W, D_MODEL // 2), const2)]
        args += [pos_tab.reshape(tiles_per_seq, reps, D_MODEL // 2), pos_tab]
    in_specs += [pl.BlockSpec((TM, D_RNN), tok),
                 pl.BlockSpec((TM, D_SGU), tok),
                 pl.BlockSpec((1, 1, 6 * D_MODEL), lambda i: (cond_of_tile(i), 0, 0)),
                 pl.BlockSpec((1, D_MODEL), const2),
                 pl.BlockSpec((1, D_MODEL), const2),
                 pl.BlockSpec((D_MODEL, D_MODEL), const2),
                 pl.BlockSpec((ROUTER_ROWS, D_MODEL), const2),
                 pl.BlockSpec((ROUTER_ROWS, LANES), const2),
                 pl.BlockSpec((TM, TM), const2),
                 pl.BlockSpec((ROUTER_ROWS, LANES), const2)]
    args += [y_rnn, y_sgu, mod3, g_post, g_pre, w_out_b, router_wt, router_bt, earlier, counts0]
    n_tiles = n_tok // TM
    counts_spec = pl.BlockSpec((ROUTER_ROWS, LANES), const2)
    x1, hn, route, counts = pl.pallas_call(
        functools.partial(_postmix_kernel, add_pos=add_pos),
        out_shape=(jax.ShapeDtypeStruct((n_tok, D_MODEL), F32),
                   jax.ShapeDtypeStruct((n_tok * ROW_TILES, LANES), F32),
                   jax.ShapeDtypeStruct((n_tiles * SUBLANES, TM), jnp.int32),
                   jax.ShapeDtypeStruct((ROUTER_ROWS, LANES), F32)),
        grid=(n_tiles,),
        in_specs=in_specs,
        out_specs=(pl.BlockSpec((TM, D_MODEL), tok),
                   pl.BlockSpec((TM * ROW_TILES, LANES), tok),
                   pl.BlockSpec((SUBLANES, TM), tok),
                   counts_spec),
        scratch_shapes=[pltpu.VMEM((ROUTER_ROWS, LANES), F32)],
        compiler_params=_params(("arbitrary",)),
        name="postmix",
    )(*args)
    route = route.reshape(n_tiles, SUBLANES, TM)
    return x1, hn, route[:, 0].reshape(n_tok), route[:, 1].reshape(n_tok), counts


def _token_rows(ref, t):
    return ref.at[pl.ds(pl.multiple_of(t * ROW_TILES, ROW_TILES), ROW_TILES), :]


def _dispatch_kernel(dest_ref, hc_ref, hs_ref, xs_ref, sem, *, n_ctx_steps):
    i = pl.program_id(0)
    base = i * TD

    def scatter(src_ref):
        def start(g, carry):
            for u in range(DMA_UNROLL):
                r = g * DMA_UNROLL + u
                pltpu.make_async_copy(_token_rows(src_ref, r), _token_rows(xs_ref, dest_ref[base + r]),
                                      sem).start(priority=u % 2)
            return carry

        lax.fori_loop(0, TD // DMA_UNROLL, start, 0)
        pltpu.make_async_copy(src_ref, xs_ref.at[pl.ds(0, TD * ROW_TILES), :], sem).wait()

    @pl.when(i < n_ctx_steps)
    def _():
        scatter(hc_ref)

    @pl.when(i >= n_ctx_steps)
    def _():
        scatter(hs_ref)


def _dispatch(dest, hn_ctx, hn_dec, n_slots):
    n_ctx_steps = hn_ctx.shape[0] // (TD * ROW_TILES)
    n_dec_steps = hn_dec.shape[0] // (TD * ROW_TILES)
    return pl.pallas_call(
        functools.partial(_dispatch_kernel, n_ctx_steps=n_ctx_steps),
        out_shape=jax.ShapeDtypeStruct((n_slots * ROW_TILES, LANES), F32),
        grid_spec=pltpu.PrefetchScalarGridSpec(
            num_scalar_prefetch=1,
            grid=(n_ctx_steps + n_dec_steps,),
            in_specs=[pl.BlockSpec((TD * ROW_TILES, LANES), lambda i, d: (jnp.minimum(i, n_ctx_steps - 1), 0)),
                      pl.BlockSpec((TD * ROW_TILES, LANES), lambda i, d: (jnp.maximum(i - n_ctx_steps, 0), 0))],
            out_specs=pl.BlockSpec(memory_space=pl.ANY),
            scratch_shapes=[pltpu.SemaphoreType.DMA(())]),
        compiler_params=_params(("arbitrary",)),
        name="dispatch",
    )(dest, hn_ctx, hn_dec)


def _experts_kernel(ea_ref, eb_ref, nv_ref, blk_ref, xs_ref, rw_ref, rb_ref, *refs):
    w32_refs, ys_ref, w_refs = refs[:6], refs[6], refs[7:]
    wga_ref, wua_ref, wda_ref, wgb_ref, wub_ref, wdb_ref = w_refs
    i = pl.program_id(0)
    nv = nv_ref[i]
    prev = jnp.maximum(i - 1, 0)

    @pl.when((i == 0) | (ea_ref[i] != ea_ref[prev]) | (eb_ref[i] != eb_ref[prev]))
    def _():
        for w32_ref, w_ref in zip(w32_refs, w_refs):
            w_ref[...] = w32_ref[0].astype(BF16)

    @pl.when(nv > 0)
    def _():
        row = lax.broadcasted_iota(jnp.int32, (TMX, 1), 0)
        xb = jnp.where(row < nv, _load_token_major(xs_ref, TMX), 0.0).astype(BF16)
        logits = jnp.dot(xb, rw_ref[...], preferred_element_type=F32) + rb_ref[...]
        lane = lax.broadcasted_iota(jnp.int32, logits.shape, 1)
        ea = ea_ref[i]
        eb = eb_ref[i]
        gmask = lane < N_GROUPS
        gl = jnp.where(gmask, logits, -jnp.inf)
        gmax = jnp.max(gl, axis=-1, keepdims=True)
        gexp = jnp.where(gmask, jnp.exp(gl - gmax), 0.0)
        g_own = jnp.sum(jnp.where(lane == ea // EXPERTS_PER_GROUP, gexp, 0.0), axis=-1, keepdims=True)
        g_w = g_own / jnp.sum(gexp, axis=-1, keepdims=True)
        la = jnp.sum(jnp.where(lane == ea + E_LANE0, logits, 0.0), axis=-1, keepdims=True)
        lb = jnp.sum(jnp.where(lane == eb + E_LANE0, logits, 0.0), axis=-1, keepdims=True)
        m = jnp.maximum(la, lb)
        pa = jnp.exp(la - m)
        pb = jnp.exp(lb - m)
        inv = g_w / (pa + pb)

        def hidden(x, wg_ref, wu_ref, w):
            g = jnp.dot(x, wg_ref[...], preferred_element_type=F32)
            u = jnp.dot(x, wu_ref[...], preferred_element_type=F32)
            return ((g * _sigmoid(g)) * u * w).astype(BF16)

        part = TMX // EXPERT_ROW_PARTS
        ys = []
        for h in range(EXPERT_ROW_PARTS):
            rows = slice(h * part, (h + 1) * part)
            act_a = hidden(xb[rows], wga_ref, wua_ref, (pa * inv)[rows])
            act_b = hidden(xb[rows], wgb_ref, wub_ref, (pb * inv)[rows])
            y = (jnp.dot(act_a, wda_ref[...], preferred_element_type=F32)
                 + jnp.dot(act_b, wdb_ref[...], preferred_element_type=F32))
            ys.append(y)
        _store_token_major(ys_ref, jnp.concatenate(ys, axis=0))


def _experts(sched, xs, router_w, router_b, wg, wu, wd):
    ea, eb, nv, blk = sched
    n_tiles = ea.shape[0]
    rows = lambda i, ea, eb, nv, blk: (blk[i], 0)
    const2 = lambda i, ea, eb, nv, blk: (0, 0)
    exp_a = lambda i, ea, eb, nv, blk: (ea[i], 0, 0)
    exp_b = lambda i, ea, eb, nv, blk: (eb[i], 0, 0)
    w_in_spec = lambda m: pl.BlockSpec((1, D_MODEL, D_EXPERT), m)
    w_out_spec = lambda m: pl.BlockSpec((1, D_EXPERT, D_MODEL), m)
    return pl.pallas_call(
        _experts_kernel,
        out_shape=jax.ShapeDtypeStruct(xs.shape, F32),
        grid_spec=pltpu.PrefetchScalarGridSpec(
            num_scalar_prefetch=4,
            grid=(n_tiles,),
            in_specs=[pl.BlockSpec((TMX * ROW_TILES, LANES), rows),
                      pl.BlockSpec((D_MODEL, ROUTER_LANES), const2),
                      pl.BlockSpec((1, ROUTER_LANES), const2),
                      w_in_spec(exp_a), w_in_spec(exp_a), w_out_spec(exp_a),
                      w_in_spec(exp_b), w_in_spec(exp_b), w_out_spec(exp_b)],
            out_specs=pl.BlockSpec((TMX * ROW_TILES, LANES), rows),
            scratch_shapes=[pltpu.VMEM((D_MODEL, D_EXPERT), BF16), pltpu.VMEM((D_MODEL, D_EXPERT), BF16),
                            pltpu.VMEM((D_EXPERT, D_MODEL), BF16)] * 2),
        compiler_params=_params(("arbitrary",)),
        name="experts",
    )(ea, eb, nv, blk, xs, router_w, router_b, wg, wu, wd, wg, wu, wd)


def _combine_kernel(dest_ref, ys_ref, x1_ref, mod_ref, gpost_ref, o_ref, ybuf, sems, *, tile0):
    i = pl.program_id(0)
    n = pl.num_programs(0)

    def fetch(tile, slot):
        base = (tile + tile0) * TC

        def start(g, carry):
            for u in range(DMA_UNROLL):
                r = g * DMA_UNROLL + u
                pltpu.make_async_copy(_token_rows(ys_ref, dest_ref[base + r]), _token_rows(ybuf.at[slot], r),
                                      sems.at[slot]).start(priority=u % 2)
            return carry

        lax.fori_loop(0, TC // DMA_UNROLL, start, 0)

    @pl.when(i == 0)
    def _():
        fetch(0, 0)

    @pl.when(i + 1 < n)
    def _():
        fetch(i + 1, (i + 1) % 2)

    slot = i % 2
    pltpu.make_async_copy(ys_ref.at[pl.ds(0, TC * ROW_TILES), :], ybuf.at[slot], sems.at[slot]).wait()
    gate2 = mod_ref[0, :, 5 * D_MODEL:6 * D_MODEL]
    o_ref[...] = x1_ref[...] + _rms(_load_token_major(ybuf.at[slot], TC)) * (gate2 * gpost_ref[...])


def _combine(dest, ys, x1, mod3, g_post, cond_of_tile, tile0):
    n_tok = x1.shape[0]
    return pl.pallas_call(
        functools.partial(_combine_kernel, tile0=tile0),
        out_shape=jax.ShapeDtypeStruct((n_tok, D_MODEL), F32),
        grid_spec=pltpu.PrefetchScalarGridSpec(
            num_scalar_prefetch=1,
            grid=(n_tok // TC,),
            in_specs=[pl.BlockSpec(memory_space=pl.ANY),
                      pl.BlockSpec((TC, D_MODEL), lambda i, d: (i, 0)),
                      pl.BlockSpec((1, 1, 6 * D_MODEL), lambda i, d: (cond_of_tile(i), 0, 0)),
                      pl.BlockSpec((1, D_MODEL), lambda i, d: (0, 0))],
            out_specs=pl.BlockSpec((TC, D_MODEL), lambda i, d: (i, 0)),
            scratch_shapes=[pltpu.VMEM((2, TC * ROW_TILES, LANES), F32), pltpu.SemaphoreType.DMA((2,))]),
        compiler_params=_params(("arbitrary",)),
        name="combine",
    )(dest, ys, x1, mod3, g_post)


def _schedule(bucket, rank, counts):
    n_tok = bucket.shape[0]
    n_max = n_tok // TMX + N_BUCKETS
    cnt = counts[:N_BUCKETS, 0].astype(jnp.int32)
    tiles = (cnt + TMX - 1) // TMX
    tile_end = jnp.cumsum(tiles)
    tile_start = tile_end - tiles
    ids = jnp.arange(N_BUCKETS, dtype=jnp.int32)
    slot0 = jnp.sum(jnp.where(bucket[:, None] == ids[None, :], (tile_start * TMX)[None, :], 0), axis=1)
    dest = slot0 + rank
    i = jnp.arange(n_max, dtype=jnp.int32)
    total = tile_end[-1]
    valid = i < total
    tb = jnp.sum((jnp.minimum(i, total - 1)[:, None] >= tile_end[None, :]).astype(jnp.int32), axis=1)
    pairs = [(a, b) for a in range(EXPERTS_PER_GROUP) for b in range(a + 1, EXPERTS_PER_GROUP)]
    ea_tab = jnp.array([g * EXPERTS_PER_GROUP + a for g in range(N_GROUPS) for a, _ in pairs], jnp.int32)
    eb_tab = jnp.array([g * EXPERTS_PER_GROUP + b for g in range(N_GROUPS) for _, b in pairs], jnp.int32)
    ea = ea_tab[tb]
    eb = eb_tab[tb]
    nv = jnp.where(valid, jnp.clip(cnt[tb] - (i - tile_start[tb]) * TMX, 0, TMX), 0)
    blk = jnp.minimum(i, total - 1)
    return dest, (ea, eb, nv, blk), n_max * TMX


def _block_diag_gates(rg_wa, rg_wx):
    heads = LANES // HEAD_RNN

    def bd(w):
        w = w.reshape(2, RNN_BLOCKS, heads, HEAD_RNN, HEAD_RNN)
        eye = jnp.eye(heads, dtype=w.dtype)
        full = jnp.einsum('dghij,hk->dghikj', w, eye)
        return full.reshape(2, RNN_BLOCKS, LANES, LANES)

    return jnp.concatenate([bd(rg_wa), bd(rg_wx)], axis=-1).astype(BF16)


def _row_tile(v):
    blocks = v.reshape(v.shape[:-1] + (RNN_BLOCKS, LANES))
    return jnp.concatenate([blocks] * PAIR, axis=-2)


def _to_time_major_state(h):
    return h.reshape(h.shape[0] // PAIR, TMJ_ROWS, LANES)


def kernel(x_prompt, x_sample, state_rglru, c, c_ctx, w_mod, b_mod, g_pre_mix, g_post_mix, g_pre_ffn,
           g_post_ffn, w_in, conv_w, conv_b, rg_wa, rg_ba, rg_wx, rg_bx, rg_lambda, sgu_g, sgu_w, sgu_b,
           w_out, router_g_w, router_g_b, router_e_w, router_e_b, exp_w_gate, exp_w_up, exp_w_down):
    assert w_mod.shape[0] == 1, "single-layer trunk"
    n_ctx, ctx_len, _ = x_prompt.shape
    n_dec, dec_len, _ = x_sample.shape
    l = 0

    n_cond = SUBLANES
    assert n_dec % PAIR == 0 and n_dec + PAIR <= n_cond
    cond = jnp.concatenate([c, jnp.broadcast_to(c_ctx, (PAIR, D_MODEL)),
                            jnp.zeros((n_cond - n_dec - PAIR, D_MODEL), F32)], axis=0)
    mod3 = _modulation(cond, w_mod[l], b_mod[l]).reshape(n_cond, 1, 6 * D_MODEL)
    pos_tab = _pos_table()

    w_rx, w_gate, w_u, w_v = jnp.split(w_in[l], [D_RNN, 2 * D_RNN, 2 * D_RNN + D_SGU], axis=1)
    w_in_b = jnp.concatenate([w_v, w_u, w_gate, w_rx], axis=1).astype(BF16)
    w_out_b = w_out[l].astype(BF16)
    sgu_w_b = sgu_w[l].reshape(2, 4 * CHUNK, CHUNK).astype(BF16)
    sgu_bias_tile = jnp.repeat(sgu_b[l].T, HEAD_SGU, axis=1)
    w_gates = _block_diag_gates(rg_wa[l], rg_wx[l])
    b_gates = 0.5 * jnp.stack([_row_tile(rg_ba[l]), _row_tile(rg_bx[l])], axis=1)
    lam = _row_tile(rg_lambda[l])
    conv_w_t = 0.5 * _row_tile(conv_w[l])
    conv_b_t = 0.5 * _row_tile(conv_b[l])
    lane_pad = ROUTER_LANES - E_LANE0 - N_EXPERTS
    router_w = jnp.pad(jnp.concatenate([router_g_w[l], router_e_w[l]], axis=1), ((0, 0), (0, lane_pad))).astype(BF16)
    router_b = jnp.pad(jnp.concatenate([router_g_b[l], router_e_b[l]]), (0, lane_pad)).reshape(1, ROUTER_LANES)
    gap, tail = E_ROW0 - N_GROUPS, ROUTER_ROWS - E_ROW0 - N_EXPERTS
    router_wt = jnp.concatenate([router_g_w[l].T, jnp.zeros((gap, D_MODEL), F32), router_e_w[l].T,
                                 jnp.zeros((tail, D_MODEL), F32)], axis=0).astype(BF16)
    router_bt = jnp.concatenate([router_g_b[l], jnp.zeros((gap,), F32), router_e_b[l], jnp.zeros((tail,), F32)])
    router_bt = jnp.broadcast_to(router_bt[:, None], (ROUTER_ROWS, LANES))
    earlier = jnp.triu(jnp.ones((TM, TM), BF16), k=1)
    row = lambda v: v.reshape(1, -1)

    n_ctx_tok = n_ctx * ctx_len

    def mixer(x, h0, cond_of_tile, cond_block, use_pos, counts0):
        n_seq, seq_len, _ = x.shape
        xf = x.reshape(n_seq * seq_len, D_MODEL)
        tab = pos_tab if use_pos else None
        xr, gg, y_sgu = _premix(x, mod3, cond_block, row(g_pre_mix[l]), w_in_b, row(sgu_g[l]),
                                sgu_w_b, sgu_bias_tile, tab)
        scan_params = (conv_w_t, conv_b_t, w_gates, b_gates, lam)
        hf, hf_last = _scan(xr, None, None, _to_time_major_state(h0[:, 0]), *scan_params, direction=0)
        y_rnn, hb_first = _scan(xr, gg, hf, _to_time_major_state(h0[:, 1]), *scan_params, direction=1)
        fstate = jnp.stack([hf_last.reshape(n_seq, D_RNN), hb_first.reshape(n_seq, D_RNN)], axis=1)
        x1, hn, bucket, rank, counts = _postmix(
            xf, y_rnn.reshape(n_seq * seq_len, D_RNN), y_sgu.reshape(n_seq * seq_len, D_SGU), mod3,
            cond_of_tile, row(g_post_mix[l]), row(g_pre_ffn[l]), w_out_b, router_wt, router_bt, earlier, tab,
            counts0)
        return x1, hn, bucket, rank, counts, fstate

    tiles_per_seq = dec_len // TM
    ctx_cond = lambda i: n_dec
    dec_cond = lambda i: i // tiles_per_seq
    h0_ctx = jnp.zeros((n_ctx, 2, D_RNN), F32)
    counts0 = jnp.zeros((ROUTER_ROWS, LANES), F32)
    x1_ctx, hn_ctx, bucket_ctx, rank_ctx, counts, st = mixer(x_prompt, h0_ctx, ctx_cond, lambda p: n_dec // PAIR,
                                                             False, counts0)
    new_state = st.astype(state_rglru.dtype)[:, None]
    x1_dec, hn_dec, bucket_dec, rank_dec, counts, _ = mixer(x_sample, state_rglru[:, l].astype(F32), dec_cond,
                                                            lambda p: p, True, counts)

    dest, sched, n_slots = _schedule(jnp.concatenate([bucket_ctx, bucket_dec]),
                                     jnp.concatenate([rank_ctx, rank_dec]), counts)
    xs = _dispatch(dest, hn_ctx, hn_dec, n_slots)
    ys = _experts(sched, xs, router_w, router_b, exp_w_gate[l], exp_w_up[l], exp_w_down[l])
    y_prompt = _combine(dest, ys, x1_ctx, mod3, row(g_post_ffn[l]), ctx_cond, 0)
    y_sample = _combine(dest, ys, x1_dec, mod3, row(g_post_ffn[l]), lambda i: i // (dec_len // TC), n_ctx_tok // TC)
    return (y_prompt.reshape(x_prompt.shape), y_sample.reshape(x_sample.shape), new_state)
```

```python
import functools
import math

import jax
import jax.numpy as jnp
from jax import lax
from jax.experimental import pallas as pl
from jax.experimental.pallas import tpu as pltpu

D_MODEL = 1024
D_RNN = 512
D_SGU = 512
N_HEADS_RNN = 8
HEAD_RNN = D_RNN // N_HEADS_RNN
N_HEADS_SGU = 8
HEAD_SGU = D_SGU // N_HEADS_SGU
CHUNK = 128
GRID_W = 64
RG_C = 8.0
N_GROUPS = 4
EXPERTS_PER_GROUP = 4
N_EXPERTS = N_GROUPS * EXPERTS_PER_GROUP
D_EXPERT = 512
EPS = 1e-6
POS_BASE = 10000.0

LANES = 128
SUBLANES = 8
CONV_W = 4
CONV_LEFT = 2
PAIR = 2
RNN_BLOCKS = D_RNN // LANES
TMJ_ROWS = PAIR * RNN_BLOCKS
ROUTER_LANES = LANES
E_LANE0 = N_GROUPS
ROUTER_ROWS = 32
E_ROW0 = SUBLANES

PAIRS_PER_GROUP = EXPERTS_PER_GROUP * (EXPERTS_PER_GROUP - 1) // 2
N_BUCKETS = N_GROUPS * PAIRS_PER_GROUP

ROW_TILES = D_MODEL // LANES

TM = 512
TD = 2048
TC = 512
DMA_UNROLL = 8
EXPERT_ROW_PARTS = 2
TMX = 512
TT = TM // PAIR
LC = 256
TS = 16
PB = 2
VMEM_LIMIT = 56 * 1024 * 1024

F32 = jnp.float32
BF16 = jnp.bfloat16


def _params(sem):
    return pltpu.CompilerParams(dimension_semantics=sem, vmem_limit_bytes=VMEM_LIMIT)


def _rms(x):
    return x * lax.rsqrt(jnp.mean(x * x, axis=-1, keepdims=True) + EPS)


def _sigmoid(x):
    return 0.5 * jnp.tanh(0.5 * x) + 0.5


def _mod_kernel(cond_ref, w_ref, b_ref, o_ref):
    c = cond_ref[...]
    s = c * _sigmoid(c)
    o_ref[...] = jnp.dot(s.astype(BF16), w_ref[...].astype(BF16),
                         preferred_element_type=F32) + b_ref[...]


def _modulation(cond, w_mod, b_mod):
    n = w_mod.shape[1]
    bn = 1024
    return pl.pallas_call(
        _mod_kernel,
        out_shape=jax.ShapeDtypeStruct((cond.shape[0], n), F32),
        grid=(n // bn,),
        in_specs=[pl.BlockSpec(cond.shape, lambda j: (0, 0)),
                  pl.BlockSpec((D_MODEL, bn), lambda j: (0, j)),
                  pl.BlockSpec((1, bn), lambda j: (0, j))],
        out_specs=pl.BlockSpec((cond.shape[0], bn), lambda j: (0, j)),
        compiler_params=_params(("arbitrary",)),
        name="modulation",
    )(cond, w_mod, b_mod.reshape(1, n))


def _pos_kernel(o_ref):
    n_freq = D_MODEL // 4
    k = lax.broadcasted_iota(jnp.int32, (GRID_W, n_freq), 1).astype(F32)
    p = lax.broadcasted_iota(jnp.int32, (GRID_W, n_freq), 0).astype(F32)
    freq = jnp.exp(-math.log(POS_BASE) * k / n_freq)
    ang = p * freq
    o_ref[:, 0:n_freq] = jnp.sin(ang)
    o_ref[:, n_freq:2 * n_freq] = jnp.cos(ang)


def _pos_table():
    return pl.pallas_call(
        _pos_kernel,
        out_shape=jax.ShapeDtypeStruct((GRID_W, D_MODEL // 2), F32),
        name="pos_table",
    )()


def _add_pos(x, pos_refs, q0):
    if pos_refs is None:
        return x
    rows_ref, cols_ref = pos_refs
    reps = x.shape[0] // GRID_W
    rpart = jnp.concatenate(
        [jnp.broadcast_to(rows_ref[q:q + 1, :], (GRID_W, D_MODEL // 2)) for q in range(q0, q0 + reps)], axis=0)
    cpart = jnp.concatenate([cols_ref[...]] * reps, axis=0)
    return jnp.concatenate([x[:, :D_MODEL // 2] + rpart, x[:, D_MODEL // 2:] + cpart], axis=1)


def _load_x(x_ref, pos_refs, r0, n):
    return _add_pos(x_ref[r0:r0 + n, :], pos_refs, r0 // GRID_W)


def _premix_kernel(*refs, add_pos):
    refs = list(refs)
    x_ref = refs.pop(0)
    pos_refs = (refs.pop(0), refs.pop(0)) if add_pos else None
    mod_ref, g_ref, win_ref, sgug_ref, sguw_ref, sgub_ref, xr_ref, gg_ref, ys_ref = refs
    hn = []
    for s in range(PAIR):
        shift = mod_ref[s, :, 0:D_MODEL]
        scale = mod_ref[s, :, D_MODEL:2 * D_MODEL]
        hn.append(_rms(_add_pos(x_ref[s], pos_refs, 0)) * (g_ref[...] * (1.0 + scale)) + shift)
    z = jnp.dot(jnp.concatenate(hn, axis=0).astype(BF16), win_ref[...],
                preferred_element_type=F32)
    half = D_SGU // 2
    heads_per_half = N_HEADS_SGU // 2
    lane_head = lax.broadcasted_iota(jnp.int32, (CHUNK, half), 1) // HEAD_SGU
    for s in range(PAIR):
        zs = z[s * TT:(s + 1) * TT]
        vn = (_rms(zs[:, 0:D_SGU]) * sgug_ref[...]).astype(BF16)
        u = zs[:, D_SGU:2 * D_SGU]
        for c in range(TT // CHUNK):
            rows = slice(c * CHUNK, (c + 1) * CHUNK)
            halves = []
            for hf in range(2):
                r = jnp.dot(sguw_ref[hf], vn[rows, hf * half:(hf + 1) * half],
                            preferred_element_type=F32)
                sel = jnp.zeros((CHUNK, half), F32)
                for h in range(heads_per_half):
                    sel = jnp.where(lane_head == h, r[h * CHUNK:(h + 1) * CHUNK], sel)
                halves.append(sel)
            gatev = jnp.concatenate(halves, axis=1) + sgub_ref[...]
            ys_ref[s, rows, :] = (u[rows] * gatev).astype(BF16)
    for s in range(PAIR):
        zs = z[s * TT:(s + 1) * TT]
        gg = jax.nn.gelu(zs[:, 2 * D_SGU:2 * D_SGU + D_RNN])
        xr = zs[:, 2 * D_SGU + D_RNN:]
        for k in range(RNN_BLOCKS):
            rows = pl.ds(s * RNN_BLOCKS + k, TT, stride=TMJ_ROWS)
            xr_ref[rows, :] = xr[:, k * LANES:(k + 1) * LANES]
            gg_ref[rows, :] = gg[:, k * LANES:(k + 1) * LANES]


def _premix(x, mod3, cond_block, g_pre, w_in_b, sgu_g, sgu_w_b, sgu_bias_tile, pos_tab):
    n_seq, seq_len, _ = x.shape
    n_pairs, n_tiles = n_seq // PAIR, seq_len // TT
    add_pos = pos_tab is not None
    const2 = lambda p, j: (0, 0)
    in_specs = [pl.BlockSpec((PAIR, TT, D_MODEL), lambda p, j: (p, j, 0))]
    args = [x]
    if add_pos:
        reps = TT // GRID_W
        in_specs += [pl.BlockSpec((None, reps, D_MODEL // 2), lambda p, j: (j, 0, 0)),
                     pl.BlockSpec((GRID_W, D_MODEL // 2), const2)]
        args += [pos_tab.reshape(GRID_W // reps, reps, D_MODEL // 2), pos_tab]
    in_specs += [pl.BlockSpec((PAIR, 1, 6 * D_MODEL), lambda p, j: (cond_block(p), 0, 0)),
                 pl.BlockSpec((1, D_MODEL), const2),
                 pl.BlockSpec((D_MODEL, 2 * D_RNN + 2 * D_SGU), const2),
                 pl.BlockSpec((1, D_SGU), const2),
                 pl.BlockSpec((2, 4 * CHUNK, CHUNK), lambda p, j: (0, 0, 0)),
                 pl.BlockSpec((CHUNK, D_SGU), const2)]
    args += [mod3, g_pre, w_in_b, sgu_g, sgu_w_b, sgu_bias_tile]
    tmj = jax.ShapeDtypeStruct((n_pairs * seq_len * TMJ_ROWS, LANES), F32)
    tmj_spec = pl.BlockSpec((TT * TMJ_ROWS, LANES), lambda p, j: (p * n_tiles + j, 0))
    xr, gg, y_sgu = pl.pallas_call(
        functools.partial(_premix_kernel, add_pos=add_pos),
        out_shape=(tmj, tmj, jax.ShapeDtypeStruct((n_seq, seq_len, D_SGU), BF16)),
        grid=(n_pairs, n_tiles),
        in_specs=in_specs,
        out_specs=(tmj_spec, tmj_spec, pl.BlockSpec((PAIR, TT, D_SGU), lambda p, j: (p, j, 0))),
        compiler_params=_params(("parallel", "parallel")),
        name="premix",
    )(*args)
    shape4 = (n_pairs, seq_len, TMJ_ROWS, LANES)
    return xr.reshape(shape4), gg.reshape(shape4), y_sgu


def _scan_kernel(*refs, reverse, n_chunks):
    if reverse:
        (xprev_ref, x_ref, xnext_ref, gg_ref, hf_ref, h0_ref, cw_ref, cb_ref, wg_ref, bg_ref, lam_ref,
         y_ref, fs_ref, xwin, xc_s, r_s, i_s, a_s, b_s, y_s, hcar) = refs
    else:
        (xprev_ref, x_ref, xnext_ref, h0_ref, cw_ref, cb_ref, wg_ref, bg_ref, lam_ref,
         hf_ref, fs_ref, xwin, xc_s, r_s, i_s, a_s, b_s, hcar) = refs
    c = pl.program_id(1)
    chunk = n_chunks - 1 - c if reverse else c
    sub_rows = TS * TMJ_ROWS

    def rows_of(pb, t0, n_steps):
        first = (pb * LC + t0) * TMJ_ROWS
        if not isinstance(first, int):
            first = pl.multiple_of(first, TMJ_ROWS)
        return pl.ds(first, n_steps * TMJ_ROWS)

    @pl.when(c == 0)
    def _():
        hcar[...] = h0_ref[...]

    xwin[:, 0:CONV_LEFT] = jnp.where(chunk > 0, xprev_ref[...], 0.0)
    xwin[:, LC + CONV_LEFT:LC + CONV_W - 1] = jnp.where(chunk < n_chunks - 1, xnext_ref[...], 0.0)

    xwin[:, CONV_LEFT:CONV_LEFT + LC] = x_ref[...]

    neg_lam = -lam_ref[...]
    softplus = jnp.maximum(neg_lam, 0.0) + jnp.log(1.0 + jnp.exp(-jnp.abs(neg_lam)))
    half_decay = (-0.5 * RG_C * math.log2(math.e)) * softplus

    def conv(pb):
        for t0 in range(0, LC, TS):
            xc = cb_ref[...] + cw_ref[0] * xwin[pb, t0:t0 + TS]
            for k in range(1, CONV_W):
                xc = xc + cw_ref[k] * xwin[pb, t0 + k:t0 + k + TS]
            xc_s[rows_of(pb, t0, TS), :] = xc.reshape(sub_rows, LANES)

    def gate_matmuls(pb):
        for k in range(RNN_BLOCKS):
            rows = pl.ds(pb * LC * TMJ_ROWS + k, LC * PAIR, stride=RNN_BLOCKS)
            g = jnp.dot(xc_s[rows, :].astype(BF16), wg_ref[k], preferred_element_type=F32)
            r_s[rows, :] = g[:, :LANES]
            i_s[rows, :] = g[:, LANES:]

    def gates(pb):
        for t0 in range(0, LC, TS):
            rows = rows_of(pb, t0, TS)
            tile = lambda ref: ref[rows, :].reshape(TS, TMJ_ROWS, LANES)
            tr = jnp.tanh(tile(r_s) + bg_ref[0])
            ti = jnp.tanh(tile(i_s) + bg_ref[1])
            log2_a = tr * half_decay + half_decay
            a = jnp.exp2(log2_a)
            q = jnp.tanh(log2_a * (-math.log(2.0))) * (a * a + 1.0)
            b = jnp.where(q > 0.0, q * lax.rsqrt(q), 0.0) * ((ti + 1.0) * tile(xc_s))
            a_s[rows, :] = a.reshape(sub_rows, LANES)
            b_s[rows, :] = b.reshape(sub_rows, LANES)

    for stage in (conv, gate_matmuls, gates):
        for pb in range(PB):
            stage(pb)

    def step(j, hs):
        t = LC - 1 - j if reverse else j
        out = []
        for pb in range(PB):
            rows = rows_of(pb, t, 1)
            h = a_s[rows, :] * hs[pb] + b_s[rows, :]
            if reverse:
                y_s[rows, :] = (hf_ref[pb, t] + h) * gg_ref[pb, t]
            else:
                hf_ref[pb, t] = h
            out.append(h)
        return tuple(out)

    hs = lax.fori_loop(0, LC, step, tuple(hcar[pb] for pb in range(PB)), unroll=8)
    for pb in range(PB):
        hcar[pb] = hs[pb]
        fs_ref[pb] = hs[pb]

    if reverse:
        for pb in range(PB):
            for s in range(PAIR):
                cols = [y_s[pl.ds(pb * LC * TMJ_ROWS + s * RNN_BLOCKS + k, LC, stride=TMJ_ROWS), :]
                        for k in range(RNN_BLOCKS)]
                y_ref[pb * PAIR + s] = jnp.concatenate(cols, axis=1).astype(BF16)


def _scan(xr, gg, hf, h0, conv_w, conv_b, w_gates, b_gates, lam, direction):
    n_pairs, seq_len = xr.shape[:2]
    n_chunks = seq_len // LC
    reverse = direction == 1
    pos = (lambda c: n_chunks - 1 - c) if reverse else (lambda c: c)
    tmj_blk = pl.BlockSpec((PB, LC, TMJ_ROWS, LANES), lambda i, c: (i, pos(c), 0, 0))
    state_blk = pl.BlockSpec((PB, TMJ_ROWS, LANES), lambda i, c: (i, 0, 0))
    per_dir = lambda *shape: pl.BlockSpec((None,) + shape, lambda i, c: (direction,) + (0,) * len(shape))
    in_specs = [
        pl.BlockSpec((PB, CONV_LEFT, TMJ_ROWS, LANES),
                     lambda i, c: (i, jnp.maximum(pos(c) * (LC // CONV_LEFT) - 1, 0), 0, 0)),
        tmj_blk,
        pl.BlockSpec((PB, 1, TMJ_ROWS, LANES), lambda i, c: (i, jnp.minimum((pos(c) + 1) * LC, seq_len - 1), 0, 0)),
    ]
    args = [xr, xr, xr]
    if reverse:
        in_specs += [tmj_blk, tmj_blk]
        args += [gg, hf]
    in_specs += [state_blk,
                 pl.BlockSpec((CONV_W, TMJ_ROWS, LANES), lambda i, c: (0, 0, 0)),
                 pl.BlockSpec((TMJ_ROWS, LANES), lambda i, c: (0, 0)),
                 per_dir(RNN_BLOCKS, LANES, 2 * LANES),
                 per_dir(2, TMJ_ROWS, LANES),
                 per_dir(TMJ_ROWS, LANES)]
    args += [h0, conv_w, conv_b, w_gates, b_gates, lam]
    flat = pltpu.VMEM((PB * LC * TMJ_ROWS, LANES), F32)
    scratch = [pltpu.VMEM((PB, LC + CONV_W - 1, TMJ_ROWS, LANES), F32)] + [flat] * (6 if reverse else 5)
    scratch += [pltpu.VMEM((PB, TMJ_ROWS, LANES), F32)]
    state = jax.ShapeDtypeStruct((n_pairs, TMJ_ROWS, LANES), F32)
    if reverse:
        out_shape = (jax.ShapeDtypeStruct((n_pairs * PAIR, seq_len, D_RNN), BF16), state)
        out_specs = (pl.BlockSpec((PB * PAIR, LC, D_RNN), lambda i, c: (i, pos(c), 0)), state_blk)
    else:
        out_shape = (jax.ShapeDtypeStruct(xr.shape, F32), state)
        out_specs = (tmj_blk, state_blk)
    return pl.pallas_call(
        functools.partial(_scan_kernel, reverse=reverse, n_chunks=n_chunks),
        out_shape=out_shape,
        grid=(n_pairs // PB, n_chunks),
        in_specs=in_specs,
        out_specs=out_specs,
        scratch_shapes=scratch,
        compiler_params=_params(("parallel", "arbitrary")),
        name="scan_bwd" if reverse else "scan_fwd",
    )(*args)


def _route(lt):
    n = lt.shape[1]
    row = lax.broadcasted_iota(jnp.int32, (EXPERTS_PER_GROUP, n), 0)
    neg = jnp.float32(-jnp.inf)

    def arg_max(v):
        m = jnp.max(v, axis=0, keepdims=True)
        return jnp.min(jnp.where(v == m, row, EXPERTS_PER_GROUP), axis=0, keepdims=True)

    g_idx = arg_max(lt[0:N_GROUPS])
    el = lt[E_ROW0:E_ROW0 + EXPERTS_PER_GROUP]
    for g in range(1, N_GROUPS):
        first = E_ROW0 + g * EXPERTS_PER_GROUP
        el = jnp.where(g_idx == g, lt[first:first + EXPERTS_PER_GROUP], el)
    i1 = arg_max(el)
    i2 = arg_max(jnp.where(row == i1, neg, el))
    ja = jnp.minimum(i1, i2)
    jb = jnp.maximum(i1, i2)
    pair = (ja * (2 * EXPERTS_PER_GROUP - 1 - ja)) // 2 + (jb - ja - 1)
    return g_idx * PAIRS_PER_GROUP + pair


def _store_token_major(ref, x, t0=0):
    n = x.shape[0]
    for k in range(ROW_TILES):
        ref[pl.ds(t0 * ROW_TILES + k, n, stride=ROW_TILES), :] = x[:, k * LANES:(k + 1) * LANES]


def _load_token_major(ref, n):
    return jnp.concatenate([ref[pl.ds(k, n, stride=ROW_TILES), :] for k in range(ROW_TILES)], axis=1)


def _postmix_kernel(*refs, add_pos, n_tiles):
    refs = list(refs)
    x_ref = refs.pop(0)
    pos_refs = (refs.pop(0), refs.pop(0)) if add_pos else None
    (yr_ref, ys_ref, mod_ref, gpost_ref, gpre_ref, wout_ref, rw_ref, rb_ref, earlier_ref, cnt0_ref,
     x1_ref, hn_ref, rt_ref, cnt_ref, run_ref, y_even, y_odd) = refs
    i = pl.program_id(0)
    y_bufs = (y_even, y_odd)

    @pl.when(i == 0)
    def _():
        run_ref[...] = cnt0_ref[...]

    def project(y_ref):
        y_ref[...] = (jnp.dot(yr_ref[...], wout_ref[0:D_RNN, :], preferred_element_type=F32)
                      + jnp.dot(ys_ref[...], wout_ref[D_RNN:, :], preferred_element_type=F32))

    def finish(y_ref):
        gate1 = mod_ref[0, :, 2 * D_MODEL:3 * D_MODEL]
        shift2 = mod_ref[0, :, 3 * D_MODEL:4 * D_MODEL]
        scale2 = mod_ref[0, :, 4 * D_MODEL:5 * D_MODEL]
        x1 = _load_x(x_ref, pos_refs, 0, TM) + _rms(y_ref[...]) * (gate1 * gpost_ref[...])
        x1_ref[...] = x1
        hn = _rms(x1) * (gpre_ref[...] * (1.0 + scale2)) + shift2
        _store_token_major(hn_ref, hn)
        lt = lax.dot_general(rw_ref[...], hn.astype(BF16), (((1,), (1,)), ((), ())),
                             preferred_element_type=F32) + rb_ref[:, 0:1]
        bucket = _route(lt)
        onehot = lax.broadcasted_iota(jnp.int32, (ROUTER_ROWS, TM), 0) == bucket
        before = jnp.dot(onehot.astype(BF16), earlier_ref[...], preferred_element_type=F32) + run_ref[:, 0:1]
        rank = jnp.sum(jnp.where(onehot, before, 0.0), axis=0, keepdims=True).astype(jnp.int32)
        row = lax.broadcasted_iota(jnp.int32, (SUBLANES, TM), 0)
        rt_ref[...] = jnp.where(row == 0, bucket, jnp.where(row == 1, rank, 0))
        run_ref[...] += jnp.sum(onehot.astype(F32), axis=1, keepdims=True)
        cnt_ref[...] = run_ref[...]

    @pl.when(i == 0)
    def _():
        project(y_bufs[0])

    for parity in range(2):
        @pl.when((i > 0) & (i < n_tiles) & (i % 2 == parity))
        def _(parity=parity):
            project(y_bufs[parity])
            finish(y_bufs[1 - parity])

    @pl.when(i == n_tiles)
    def _():
        finish(y_bufs[(n_tiles - 1) % 2])


def _postmix(x, y_rnn, y_sgu, mod3, cond_of_tile, g_post, g_pre, w_out_b, router_wt, router_bt, earlier, pos_tab,
             counts0):
    n_tok = x.shape[0]
    n_tiles = n_tok // TM
    add_pos = pos_tab is not None
    proj = lambda i: jnp.minimum(i, n_tiles - 1)
    fin = lambda i: jnp.maximum(i - 1, 0)
    tok_proj = lambda i: (proj(i), 0)
    tok = lambda i: (fin(i), 0)
    const2 = lambda i: (0, 0)
    in_specs = [pl.BlockSpec((TM, D_MODEL), tok)]
    args = [x]
    if add_pos:
        reps = TM // GRID_W
        tiles_per_seq = GRID_W // reps
        in_specs += [pl.BlockSpec((None, reps, D_MODEL // 2), lambda i: (fin(i) % tiles_per_seq, 0, 0)),
                     pl.BlockSpec((GRID_W, D_MODEL // 2), const2)]
        args += [pos_tab.reshape(tiles_per_seq, reps, D_MODEL // 2), pos_tab]
    in_specs += [pl.BlockSpec((TM, D_RNN), tok_proj),
                 pl.BlockSpec((TM, D_SGU), tok_proj),
                 pl.BlockSpec((1, 1, 6 * D_MODEL), lambda i: (cond_of_tile(fin(i)), 0, 0)),
                 pl.BlockSpec((1, D_MODEL), const2),
                 pl.BlockSpec((1, D_MODEL), const2),
                 pl.BlockSpec((D_MODEL, D_MODEL), const2),
                 pl.BlockSpec((ROUTER_ROWS, D_MODEL), const2),
                 pl.BlockSpec((ROUTER_ROWS, LANES), const2),
                 pl.BlockSpec((TM, TM), const2),
                 pl.BlockSpec((ROUTER_ROWS, LANES), const2)]
    args += [y_rnn, y_sgu, mod3, g_post, g_pre, w_out_b, router_wt, router_bt, earlier, counts0]
    counts_spec = pl.BlockSpec((ROUTER_ROWS, LANES), const2)
    x1, hn, route, counts = pl.pallas_call(
        functools.partial(_postmix_kernel, add_pos=add_pos, n_tiles=n_tiles),
        out_shape=(jax.ShapeDtypeStruct((n_tok, D_MODEL), F32),
                   jax.ShapeDtypeStruct((n_tok * ROW_TILES, LANES), F32),
                   jax.ShapeDtypeStruct((n_tiles * SUBLANES, TM), jnp.int32),
                   jax.ShapeDtypeStruct((ROUTER_ROWS, LANES), F32)),
        grid=(n_tiles + 1,),
        in_specs=in_specs,
        out_specs=(pl.BlockSpec((TM, D_MODEL), tok),
                   pl.BlockSpec((TM * ROW_TILES, LANES), tok),
                   pl.BlockSpec((SUBLANES, TM), tok),
                   counts_spec),
        scratch_shapes=[pltpu.VMEM((ROUTER_ROWS, LANES), F32),
                        pltpu.VMEM((TM, D_MODEL), F32), pltpu.VMEM((TM, D_MODEL), F32)],
        compiler_params=_params(("arbitrary",)),
        name="postmix",
    )(*args)
    route = route.reshape(n_tiles, SUBLANES, TM)
    return x1, hn, route[:, 0].reshape(n_tok), route[:, 1].reshape(n_tok), counts


def _token_rows(ref, t):
    return ref.at[pl.ds(pl.multiple_of(t * ROW_TILES, ROW_TILES), ROW_TILES), :]


def _dispatch_kernel(dest_ref, hc_ref, hs_ref, xs_ref, sem, *, n_ctx_steps):
    i = pl.program_id(0)
    base = i * TD

    def scatter(src_ref):
        def start(g, carry):
            for u in range(DMA_UNROLL):
                r = g * DMA_UNROLL + u
                pltpu.make_async_copy(_token_rows(src_ref, r), _token_rows(xs_ref, dest_ref[base + r]),
                                      sem).start(priority=u % 2)
            return carry

        lax.fori_loop(0, TD // DMA_UNROLL, start, 0)
        pltpu.make_async_copy(src_ref, xs_ref.at[pl.ds(0, TD * ROW_TILES), :], sem).wait()

    @pl.when(i < n_ctx_steps)
    def _():
        scatter(hc_ref)

    @pl.when(i >= n_ctx_steps)
    def _():
        scatter(hs_ref)


def _dispatch(dest, hn_ctx, hn_dec, n_slots):
    n_ctx_steps = hn_ctx.shape[0] // (TD * ROW_TILES)
    n_dec_steps = hn_dec.shape[0] // (TD * ROW_TILES)
    return pl.pallas_call(
        functools.partial(_dispatch_kernel, n_ctx_steps=n_ctx_steps),
        out_shape=jax.ShapeDtypeStruct((n_slots * ROW_TILES, LANES), F32),
        grid_spec=pltpu.PrefetchScalarGridSpec(
            num_scalar_prefetch=1,
            grid=(n_ctx_steps + n_dec_steps,),
            in_specs=[pl.BlockSpec((TD * ROW_TILES, LANES), lambda i, d: (jnp.minimum(i, n_ctx_steps - 1), 0)),
                      pl.BlockSpec((TD * ROW_TILES, LANES), lambda i, d: (jnp.maximum(i - n_ctx_steps, 0), 0))],
            out_specs=pl.BlockSpec(memory_space=pl.ANY),
            scratch_shapes=[pltpu.SemaphoreType.DMA(())]),
        compiler_params=_params(("arbitrary",)),
        name="dispatch",
    )(dest, hn_ctx, hn_dec)


def _experts_kernel(ea_ref, eb_ref, nv_ref, blk_ref, xs_ref, rw_ref, rb_ref, *refs):
    w32_refs, ys_ref, w_refs = refs[:6], refs[6], refs[7:]
    wga_ref, wua_ref, wda_ref, wgb_ref, wub_ref, wdb_ref = w_refs
    i = pl.program_id(0)
    nv = nv_ref[i]
    prev = jnp.maximum(i - 1, 0)

    @pl.when((i == 0) | (ea_ref[i] != ea_ref[prev]) | (eb_ref[i] != eb_ref[prev]))
    def _():
        for w32_ref, w_ref in zip(w32_refs, w_refs):
            w_ref[...] = w32_ref[0].astype(BF16)

    @pl.when(nv > 0)
    def _():
        row = lax.broadcasted_iota(jnp.int32, (TMX, 1), 0)
        xb = jnp.where(row < nv, _load_token_major(xs_ref, TMX), 0.0).astype(BF16)
        logits = jnp.dot(xb, rw_ref[...], preferred_element_type=F32) + rb_ref[...]
        lane = lax.broadcasted_iota(jnp.int32, logits.shape, 1)
        ea = ea_ref[i]
        eb = eb_ref[i]
        gmask = lane < N_GROUPS
        gl = jnp.where(gmask, logits, -jnp.inf)
        gmax = jnp.max(gl, axis=-1, keepdims=True)
        gexp = jnp.where(gmask, jnp.exp(gl - gmax), 0.0)
        g_own = jnp.sum(jnp.where(lane == ea // EXPERTS_PER_GROUP, gexp, 0.0), axis=-1, keepdims=True)
        g_w = g_own / jnp.sum(gexp, axis=-1, keepdims=True)
        la = jnp.sum(jnp.where(lane == ea + E_LANE0, logits, 0.0), axis=-1, keepdims=True)
        lb = jnp.sum(jnp.where(lane == eb + E_LANE0, logits, 0.0), axis=-1, keepdims=True)
        m = jnp.maximum(la, lb)
        pa = jnp.exp(la - m)
        pb = jnp.exp(lb - m)
        inv = g_w / (pa + pb)

        def hidden(x, wg_ref, wu_ref, w):
            g = jnp.dot(x, wg_ref[...], preferred_element_type=F32)
            u = jnp.dot(x, wu_ref[...], preferred_element_type=F32)
            return ((g * _sigmoid(g)) * u * w).astype(BF16)

        part = TMX // EXPERT_ROW_PARTS
        ys = []
        for h in range(EXPERT_ROW_PARTS):
            rows = slice(h * part, (h + 1) * part)
            act_a = hidden(xb[rows], wga_ref, wua_ref, (pa * inv)[rows])
            act_b = hidden(xb[rows], wgb_ref, wub_ref, (pb * inv)[rows])
            y = (jnp.dot(act_a, wda_ref[...], preferred_element_type=F32)
                 + jnp.dot(act_b, wdb_ref[...], preferred_element_type=F32))
            ys.append(y)
        _store_token_major(ys_ref, jnp.concatenate(ys, axis=0))


def _experts(sched, xs, router_w, router_b, wg, wu, wd):
    ea, eb, nv, blk = sched
    n_tiles = ea.shape[0]
    rows = lambda i, ea, eb, nv, blk: (blk[i], 0)
    const2 = lambda i, ea, eb, nv, blk: (0, 0)
    exp_a = lambda i, ea, eb, nv, blk: (ea[i], 0, 0)
    exp_b = lambda i, ea, eb, nv, blk: (eb[i], 0, 0)
    w_in_spec = lambda m: pl.BlockSpec((1, D_MODEL, D_EXPERT), m)
    w_out_spec = lambda m: pl.BlockSpec((1, D_EXPERT, D_MODEL), m)
    return pl.pallas_call(
        _experts_kernel,
        out_shape=jax.ShapeDtypeStruct(xs.shape, F32),
        grid_spec=pltpu.PrefetchScalarGridSpec(
            num_scalar_prefetch=4,
            grid=(n_tiles,),
            in_specs=[pl.BlockSpec((TMX * ROW_TILES, LANES), rows),
                      pl.BlockSpec((D_MODEL, ROUTER_LANES), const2),
                      pl.BlockSpec((1, ROUTER_LANES), const2),
                      w_in_spec(exp_a), w_in_spec(exp_a), w_out_spec(exp_a),
                      w_in_spec(exp_b), w_in_spec(exp_b), w_out_spec(exp_b)],
            out_specs=pl.BlockSpec((TMX * ROW_TILES, LANES), rows),
            scratch_shapes=[pltpu.VMEM((D_MODEL, D_EXPERT), BF16), pltpu.VMEM((D_MODEL, D_EXPERT), BF16),
                            pltpu.VMEM((D_EXPERT, D_MODEL), BF16)] * 2),
        compiler_params=_params(("arbitrary",)),
        name="experts",
    )(ea, eb, nv, blk, xs, router_w, router_b, wg, wu, wd, wg, wu, wd)


def _combine_kernel(dest_ref, ys_ref, x1_ref, mod_ref, gpost_ref, o_ref, ybuf, sems, *, tile0):
    i = pl.program_id(0)
    n = pl.num_programs(0)

    def fetch(tile, slot):
        base = (tile + tile0) * TC

        def start(g, carry):
            for u in range(DMA_UNROLL):
                r = g * DMA_UNROLL + u
                pltpu.make_async_copy(_token_rows(ys_ref, dest_ref[base + r]), _token_rows(ybuf.at[slot], r),
                                      sems.at[slot]).start(priority=u % 2)
            return carry

        lax.fori_loop(0, TC // DMA_UNROLL, start, 0)

    @pl.when(i == 0)
    def _():
        fetch(0, 0)

    @pl.when(i + 1 < n)
    def _():
        fetch(i + 1, (i + 1) % 2)

    slot = i % 2
    pltpu.make_async_copy(ys_ref.at[pl.ds(0, TC * ROW_TILES), :], ybuf.at[slot], sems.at[slot]).wait()
    gate2 = mod_ref[0, :, 5 * D_MODEL:6 * D_MODEL]
    o_ref[...] = x1_ref[...] + _rms(_load_token_major(ybuf.at[slot], TC)) * (gate2 * gpost_ref[...])


def _combine(dest, ys, x1, mod3, g_post, cond_of_tile, tile0):
    n_tok = x1.shape[0]
    return pl.pallas_call(
        functools.partial(_combine_kernel, tile0=tile0),
        out_shape=jax.ShapeDtypeStruct((n_tok, D_MODEL), F32),
        grid_spec=pltpu.PrefetchScalarGridSpec(
            num_scalar_prefetch=1,
            grid=(n_tok // TC,),
            in_specs=[pl.BlockSpec(memory_space=pl.ANY),
                      pl.BlockSpec((TC, D_MODEL), lambda i, d: (i, 0)),
                      pl.BlockSpec((1, 1, 6 * D_MODEL), lambda i, d: (cond_of_tile(i), 0, 0)),
                      pl.BlockSpec((1, D_MODEL), lambda i, d: (0, 0))],
            out_specs=pl.BlockSpec((TC, D_MODEL), lambda i, d: (i, 0)),
            scratch_shapes=[pltpu.VMEM((2, TC * ROW_TILES, LANES), F32), pltpu.SemaphoreType.DMA((2,))]),
        compiler_params=_params(("arbitrary",)),
        name="combine",
    )(dest, ys, x1, mod3, g_post)


def _schedule(bucket, rank, counts):
    n_tok = bucket.shape[0]
    n_max = n_tok // TMX + N_BUCKETS
    cnt = counts[:N_BUCKETS, 0].astype(jnp.int32)
    tiles = (cnt + TMX - 1) // TMX
    tile_end = jnp.cumsum(tiles)
    tile_start = tile_end - tiles
    ids = jnp.arange(N_BUCKETS, dtype=jnp.int32)
    slot0 = jnp.sum(jnp.where(bucket[:, None] == ids[None, :], (tile_start * TMX)[None, :], 0), axis=1)
    dest = slot0 + rank
    i = jnp.arange(n_max, dtype=jnp.int32)
    total = tile_end[-1]
    valid = i < total
    tb = jnp.sum((jnp.minimum(i, total - 1)[:, None] >= tile_end[None, :]).astype(jnp.int32), axis=1)
    pairs = [(a, b) for a in range(EXPERTS_PER_GROUP) for b in range(a + 1, EXPERTS_PER_GROUP)]
    ea_tab = jnp.array([g * EXPERTS_PER_GROUP + a for g in range(N_GROUPS) for a, _ in pairs], jnp.int32)
    eb_tab = jnp.array([g * EXPERTS_PER_GROUP + b for g in range(N_GROUPS) for _, b in pairs], jnp.int32)
    ea = ea_tab[tb]
    eb = eb_tab[tb]
    nv = jnp.where(valid, jnp.clip(cnt[tb] - (i - tile_start[tb]) * TMX, 0, TMX), 0)
    blk = jnp.minimum(i, total - 1)
    return dest, (ea, eb, nv, blk), n_max * TMX


def _block_diag_gates(rg_wa, rg_wx):
    heads = LANES // HEAD_RNN

    def bd(w):
        w = w.reshape(2, RNN_BLOCKS, heads, HEAD_RNN, HEAD_RNN)
        eye = jnp.eye(heads, dtype=w.dtype)
        full = jnp.einsum('dghij,hk->dghikj', w, eye)
        return full.reshape(2, RNN_BLOCKS, LANES, LANES)

    return jnp.concatenate([bd(rg_wa), bd(rg_wx)], axis=-1).astype(BF16)


def _row_tile(v):
    blocks = v.reshape(v.shape[:-1] + (RNN_BLOCKS, LANES))
    return jnp.concatenate([blocks] * PAIR, axis=-2)


def _to_time_major_state(h):
    return h.reshape(h.shape[0] // PAIR, TMJ_ROWS, LANES)


def kernel(x_prompt, x_sample, state_rglru, c, c_ctx, w_mod, b_mod, g_pre_mix, g_post_mix, g_pre_ffn,
           g_post_ffn, w_in, conv_w, conv_b, rg_wa, rg_ba, rg_wx, rg_bx, rg_lambda, sgu_g, sgu_w, sgu_b,
           w_out, router_g_w, router_g_b, router_e_w, router_e_b, exp_w_gate, exp_w_up, exp_w_down):
    assert w_mod.shape[0] == 1, "single-layer trunk"
    n_ctx, ctx_len, _ = x_prompt.shape
    n_dec, dec_len, _ = x_sample.shape
    l = 0

    n_cond = SUBLANES
    assert n_dec % PAIR == 0 and n_dec + PAIR <= n_cond
    cond = jnp.concatenate([c, jnp.broadcast_to(c_ctx, (PAIR, D_MODEL)),
                            jnp.zeros((n_cond - n_dec - PAIR, D_MODEL), F32)], axis=0)
    mod3 = _modulation(cond, w_mod[l], b_mod[l]).reshape(n_cond, 1, 6 * D_MODEL)
    pos_tab = _pos_table()

    w_rx, w_gate, w_u, w_v = jnp.split(w_in[l], [D_RNN, 2 * D_RNN, 2 * D_RNN + D_SGU], axis=1)
    w_in_b = jnp.concatenate([w_v, w_u, w_gate, w_rx], axis=1).astype(BF16)
    w_out_b = w_out[l].astype(BF16)
    sgu_w_b = sgu_w[l].reshape(2, 4 * CHUNK, CHUNK).astype(BF16)
    sgu_bias_tile = jnp.repeat(sgu_b[l].T, HEAD_SGU, axis=1)
    w_gates = _block_diag_gates(rg_wa[l], rg_wx[l])
    b_gates = 0.5 * jnp.stack([_row_tile(rg_ba[l]), _row_tile(rg_bx[l])], axis=1)
    lam = _row_tile(rg_lambda[l])
    conv_w_t = 0.5 * _row_tile(conv_w[l])
    conv_b_t = 0.5 * _row_tile(conv_b[l])
    lane_pad = ROUTER_LANES - E_LANE0 - N_EXPERTS
    router_w = jnp.pad(jnp.concatenate([router_g_w[l], router_e_w[l]], axis=1), ((0, 0), (0, lane_pad))).astype(BF16)
    router_b = jnp.pad(jnp.concatenate([router_g_b[l], router_e_b[l]]), (0, lane_pad)).reshape(1, ROUTER_LANES)
    gap, tail = E_ROW0 - N_GROUPS, ROUTER_ROWS - E_ROW0 - N_EXPERTS
    router_wt = jnp.concatenate([router_g_w[l].T, jnp.zeros((gap, D_MODEL), F32), router_e_w[l].T,
                                 jnp.zeros((tail, D_MODEL), F32)], axis=0).astype(BF16)
    router_bt = jnp.concatenate([router_g_b[l], jnp.zeros((gap,), F32), router_e_b[l], jnp.zeros((tail,), F32)])
    router_bt = jnp.broadcast_to(router_bt[:, None], (ROUTER_ROWS, LANES))
    earlier = jnp.triu(jnp.ones((TM, TM), BF16), k=1)
    row = lambda v: v.reshape(1, -1)

    n_ctx_tok = n_ctx * ctx_len

    def mixer(x, h0, cond_of_tile, cond_block, use_pos, counts0):
        n_seq, seq_len, _ = x.shape
        xf = x.reshape(n_seq * seq_len, D_MODEL)
        tab = pos_tab if use_pos else None
        xr, gg, y_sgu = _premix(x, mod3, cond_block, row(g_pre_mix[l]), w_in_b, row(sgu_g[l]),
                                sgu_w_b, sgu_bias_tile, tab)
        scan_params = (conv_w_t, conv_b_t, w_gates, b_gates, lam)
        hf, hf_last = _scan(xr, None, None, _to_time_major_state(h0[:, 0]), *scan_params, direction=0)
        y_rnn, hb_first = _scan(xr, gg, hf, _to_time_major_state(h0[:, 1]), *scan_params, direction=1)
        fstate = jnp.stack([hf_last.reshape(n_seq, D_RNN), hb_first.reshape(n_seq, D_RNN)], axis=1)
        x1, hn, bucket, rank, counts = _postmix(
            xf, y_rnn.reshape(n_seq * seq_len, D_RNN), y_sgu.reshape(n_seq * seq_len, D_SGU), mod3,
            cond_of_tile, row(g_post_mix[l]), row(g_pre_ffn[l]), w_out_b, router_wt, router_bt, earlier, tab,
            counts0)
        return x1, hn, bucket, rank, counts, fstate

    tiles_per_seq = dec_len // TM
    ctx_cond = lambda i: n_dec
    dec_cond = lambda i: i // tiles_per_seq
    h0_ctx = jnp.zeros((n_ctx, 2, D_RNN), F32)
    counts0 = jnp.zeros((ROUTER_ROWS, LANES), F32)
    x1_ctx, hn_ctx, bucket_ctx, rank_ctx, counts, st = mixer(x_prompt, h0_ctx, ctx_cond, lambda p: n_dec // PAIR,
                                                             False, counts0)
    new_state = st.astype(state_rglru.dtype)[:, None]
    x1_dec, hn_dec, bucket_dec, rank_dec, counts, _ = mixer(x_sample, state_rglru[:, l].astype(F32), dec_cond,
                                                            lambda p: p, True, counts)

    dest, sched, n_slots = _schedule(jnp.concatenate([bucket_ctx, bucket_dec]),
                                     jnp.concatenate([rank_ctx, rank_dec]), counts)
    xs = _dispatch(dest, hn_ctx, hn_dec, n_slots)
    ys = _experts(sched, xs, router_w, router_b, exp_w_gate[l], exp_w_up[l], exp_w_down[l])
    y_prompt = _combine(dest, ys, x1_ctx, mod3, row(g_post_ffn[l]), ctx_cond, 0)
    y_sample = _combine(dest, ys, x1_dec, mod3, row(g_post_ffn[l]), lambda i: i // (dec_len // TC), n_ctx_tok // TC)
    return (y_prompt.reshape(x_prompt.shape), y_sample.reshape(x_sample.shape), new_state)
```

```python
import functools
import math

import jax
import jax.numpy as jnp
from jax import lax
from jax.experimental import pallas as pl
from jax.experimental.pallas import tpu as pltpu

D_MODEL = 1024
D_RNN = 512
D_SGU = 512
N_HEADS_RNN = 8
HEAD_RNN = D_RNN // N_HEADS_RNN
N_HEADS_SGU = 8
HEAD_SGU = D_SGU // N_HEADS_SGU
CHUNK = 128
GRID_W = 64
RG_C = 8.0
N_GROUPS = 4
EXPERTS_PER_GROUP = 4
N_EXPERTS = N_GROUPS * EXPERTS_PER_GROUP
D_EXPERT = 512
EPS = 1e-6
POS_BASE = 10000.0

LANES = 128
SUBLANES = 8
CONV_W = 4
CONV_LEFT = 2
PAIR = 2
RNN_BLOCKS = D_RNN // LANES
TMJ_ROWS = PAIR * RNN_BLOCKS
ROUTER_LANES = LANES
E_LANE0 = N_GROUPS
ROUTER_ROWS = 32
E_ROW0 = SUBLANES

PAIRS_PER_GROUP = EXPERTS_PER_GROUP * (EXPERTS_PER_GROUP - 1) // 2
N_BUCKETS = N_GROUPS * PAIRS_PER_GROUP

ROW_TILES = D_MODEL // LANES

MOD_COLS = 1024
TM = 512
TP = 1024
TD = 2048
TC = 512
DMA_UNROLL = 8
EXPERT_ROW_PARTS = 2
TMX = 512
TT = TM // PAIR
LC = 256
TS = 16
PB = 2
VMEM_LIMIT = 56 * 1024 * 1024

F32 = jnp.float32
BF16 = jnp.bfloat16


def _params(sem):
    return pltpu.CompilerParams(dimension_semantics=sem, vmem_limit_bytes=VMEM_LIMIT)


def _rms(x):
    return x * lax.rsqrt(jnp.mean(x * x, axis=-1, keepdims=True) + EPS)


def _sigmoid(x):
    return 0.5 * jnp.tanh(0.5 * x) + 0.5


def _mod_kernel(cond_ref, w_ref, b_ref, o_ref):
    c = cond_ref[...]
    s = c * _sigmoid(c)
    o_ref[...] = jnp.dot(s.astype(BF16), w_ref[...].astype(BF16),
                         preferred_element_type=F32) + b_ref[...]


def _modulation(cond, w_mod, b_mod):
    n = w_mod.shape[1]
    return pl.pallas_call(
        _mod_kernel,
        out_shape=jax.ShapeDtypeStruct((cond.shape[0], n), F32),
        grid=(n // MOD_COLS,),
        in_specs=[pl.BlockSpec(cond.shape, lambda j: (0, 0)),
                  pl.BlockSpec((D_MODEL, MOD_COLS), lambda j: (0, j)),
                  pl.BlockSpec((1, MOD_COLS), lambda j: (0, j))],
        out_specs=pl.BlockSpec((cond.shape[0], MOD_COLS), lambda j: (0, j)),
        compiler_params=_params(("arbitrary",)),
        name="modulation",
    )(cond, w_mod, b_mod.reshape(1, n))


def _pos_kernel(o_ref):
    n_freq = D_MODEL // 4
    k = lax.broadcasted_iota(jnp.int32, (GRID_W, n_freq), 1).astype(F32)
    p = lax.broadcasted_iota(jnp.int32, (GRID_W, n_freq), 0).astype(F32)
    freq = jnp.exp(-math.log(POS_BASE) * k / n_freq)
    ang = p * freq
    o_ref[:, 0:n_freq] = jnp.sin(ang)
    o_ref[:, n_freq:2 * n_freq] = jnp.cos(ang)


def _pos_table():
    return pl.pallas_call(
        _pos_kernel,
        out_shape=jax.ShapeDtypeStruct((GRID_W, D_MODEL // 2), F32),
        name="pos_table",
    )()


def _add_pos(x, pos_refs, q0):
    if pos_refs is None:
        return x
    rows_ref, cols_ref = pos_refs
    reps = x.shape[0] // GRID_W
    rpart = jnp.concatenate(
        [jnp.broadcast_to(rows_ref[q:q + 1, :], (GRID_W, D_MODEL // 2)) for q in range(q0, q0 + reps)], axis=0)
    cpart = jnp.concatenate([cols_ref[...]] * reps, axis=0)
    return jnp.concatenate([x[:, :D_MODEL // 2] + rpart, x[:, D_MODEL // 2:] + cpart], axis=1)


def _load_x(x_ref, pos_refs, r0, n):
    return _add_pos(x_ref[r0:r0 + n, :], pos_refs, r0 // GRID_W)


def _premix_kernel(*refs, add_pos):
    refs = list(refs)
    x_ref = refs.pop(0)
    pos_refs = (refs.pop(0), refs.pop(0)) if add_pos else None
    mod_ref, g_ref, win_ref, sgug_ref, sguw_ref, sgub_ref, xr_ref, gg_ref, ys_ref = refs
    hn = []
    for s in range(PAIR):
        shift = mod_ref[s, :, 0:D_MODEL]
        scale = mod_ref[s, :, D_MODEL:2 * D_MODEL]
        hn.append(_rms(_add_pos(x_ref[s], pos_refs, 0)) * (g_ref[...] * (1.0 + scale)) + shift)
    z = jnp.dot(jnp.concatenate(hn, axis=0).astype(BF16), win_ref[...],
                preferred_element_type=F32)
    half = D_SGU // 2
    heads_per_half = N_HEADS_SGU // 2
    lane_head = lax.broadcasted_iota(jnp.int32, (CHUNK, half), 1) // HEAD_SGU
    for s in range(PAIR):
        zs = z[s * TT:(s + 1) * TT]
        vn = (_rms(zs[:, 0:D_SGU]) * sgug_ref[...]).astype(BF16)
        u = zs[:, D_SGU:2 * D_SGU]
        for c in range(TT // CHUNK):
            rows = slice(c * CHUNK, (c + 1) * CHUNK)
            halves = []
            for hf in range(2):
                r = jnp.dot(sguw_ref[hf], vn[rows, hf * half:(hf + 1) * half],
                            preferred_element_type=F32)
                sel = jnp.zeros((CHUNK, half), F32)
                for h in range(heads_per_half):
                    sel = jnp.where(lane_head == h, r[h * CHUNK:(h + 1) * CHUNK], sel)
                halves.append(sel)
            gatev = jnp.concatenate(halves, axis=1) + sgub_ref[...]
            ys_ref[s, rows, :] = (u[rows] * gatev).astype(BF16)
    for s in range(PAIR):
        zs = z[s * TT:(s + 1) * TT]
        gg = jax.nn.gelu(zs[:, 2 * D_SGU:2 * D_SGU + D_RNN])
        xr = zs[:, 2 * D_SGU + D_RNN:]
        for k in range(RNN_BLOCKS):
            rows = pl.ds(s * RNN_BLOCKS + k, TT, stride=TMJ_ROWS)
            xr_ref[rows, :] = xr[:, k * LANES:(k + 1) * LANES]
            gg_ref[rows, :] = gg[:, k * LANES:(k + 1) * LANES]


def _premix(x, mod3, cond_block, g_pre, w_in_b, sgu_g, sgu_w_b, sgu_bias_tile, pos_tab):
    n_seq, seq_len, _ = x.shape
    n_pairs, n_tiles = n_seq // PAIR, seq_len // TT
    add_pos = pos_tab is not None
    const2 = lambda p, j: (0, 0)
    in_specs = [pl.BlockSpec((PAIR, TT, D_MODEL), lambda p, j: (p, j, 0))]
    args = [x]
    if add_pos:
        reps = TT // GRID_W
        in_specs += [pl.BlockSpec((None, reps, D_MODEL // 2), lambda p, j: (j, 0, 0)),
                     pl.BlockSpec((GRID_W, D_MODEL // 2), const2)]
        args += [pos_tab.reshape(GRID_W // reps, reps, D_MODEL // 2), pos_tab]
    in_specs += [pl.BlockSpec((PAIR, 1, 6 * D_MODEL), lambda p, j: (cond_block(p), 0, 0)),
                 pl.BlockSpec((1, D_MODEL), const2),
                 pl.BlockSpec((D_MODEL, 2 * D_RNN + 2 * D_SGU), const2),
                 pl.BlockSpec((1, D_SGU), const2),
                 pl.BlockSpec((2, 4 * CHUNK, CHUNK), lambda p, j: (0, 0, 0)),
                 pl.BlockSpec((CHUNK, D_SGU), const2)]
    args += [mod3, g_pre, w_in_b, sgu_g, sgu_w_b, sgu_bias_tile]
    tmj = jax.ShapeDtypeStruct((n_pairs * seq_len * TMJ_ROWS, LANES), F32)
    tmj_spec = pl.BlockSpec((TT * TMJ_ROWS, LANES), lambda p, j: (p * n_tiles + j, 0))
    xr, gg, y_sgu = pl.pallas_call(
        functools.partial(_premix_kernel, add_pos=add_pos),
        out_shape=(tmj, tmj, jax.ShapeDtypeStruct((n_seq, seq_len, D_SGU), BF16)),
        grid=(n_pairs, n_tiles),
        in_specs=in_specs,
        out_specs=(tmj_spec, tmj_spec, pl.BlockSpec((PAIR, TT, D_SGU), lambda p, j: (p, j, 0))),
        compiler_params=_params(("parallel", "parallel")),
        name="premix",
    )(*args)
    shape4 = (n_pairs, seq_len, TMJ_ROWS, LANES)
    return xr.reshape(shape4), gg.reshape(shape4), y_sgu


def _scan_kernel(*refs, reverse, n_chunks):
    if reverse:
        (xprev_ref, x_ref, xnext_ref, gg_ref, hf_ref, h0_ref, cw_ref, cb_ref, wg_ref, bg_ref, lam_ref,
         y_ref, fs_ref, xwin, xc_s, r_s, i_s, a_s, b_s, y_s, hcar) = refs
    else:
        (xprev_ref, x_ref, xnext_ref, h0_ref, cw_ref, cb_ref, wg_ref, bg_ref, lam_ref,
         hf_ref, fs_ref, xwin, xc_s, r_s, i_s, a_s, b_s, hcar) = refs
    c = pl.program_id(1)
    chunk = n_chunks - 1 - c if reverse else c
    sub_rows = TS * TMJ_ROWS

    def rows_of(pb, t0, n_steps):
        first = (pb * LC + t0) * TMJ_ROWS
        if not isinstance(first, int):
            first = pl.multiple_of(first, TMJ_ROWS)
        return pl.ds(first, n_steps * TMJ_ROWS)

    @pl.when(c == 0)
    def _():
        hcar[...] = h0_ref[...]

    xwin[:, 0:CONV_LEFT] = jnp.where(chunk > 0, xprev_ref[...], 0.0)
    xwin[:, LC + CONV_LEFT:LC + CONV_W - 1] = jnp.where(chunk < n_chunks - 1, xnext_ref[...], 0.0)

    xwin[:, CONV_LEFT:CONV_LEFT + LC] = x_ref[...]

    neg_lam = -lam_ref[...]
    softplus = jnp.maximum(neg_lam, 0.0) + jnp.log(1.0 + jnp.exp(-jnp.abs(neg_lam)))
    half_decay = (-0.5 * RG_C * math.log2(math.e)) * softplus

    def conv(pb):
        for t0 in range(0, LC, TS):
            xc = cb_ref[...] + cw_ref[0] * xwin[pb, t0:t0 + TS]
            for k in range(1, CONV_W):
                xc = xc + cw_ref[k] * xwin[pb, t0 + k:t0 + k + TS]
            xc_s[rows_of(pb, t0, TS), :] = xc.reshape(sub_rows, LANES)

    def gate_matmuls(pb):
        for k in range(RNN_BLOCKS):
            rows = pl.ds(pb * LC * TMJ_ROWS + k, LC * PAIR, stride=RNN_BLOCKS)
            g = jnp.dot(xc_s[rows, :].astype(BF16), wg_ref[k], preferred_element_type=F32)
            r_s[rows, :] = g[:, :LANES]
            i_s[rows, :] = g[:, LANES:]

    def gates(pb):
        for t0 in range(0, LC, TS):
            rows = rows_of(pb, t0, TS)
            tile = lambda ref: ref[rows, :].reshape(TS, TMJ_ROWS, LANES)
            tr = jnp.tanh(tile(r_s) + bg_ref[0])
            ti = jnp.tanh(tile(i_s) + bg_ref[1])
            log2_a = tr * half_decay + half_decay
            a = jnp.exp2(log2_a)
            q = jnp.tanh(log2_a * (-math.log(2.0))) * (a * a + 1.0)
            b = jnp.where(q > 0.0, q * lax.rsqrt(q), 0.0) * ((ti + 1.0) * tile(xc_s))
            a_s[rows, :] = a.reshape(sub_rows, LANES)
            b_s[rows, :] = b.reshape(sub_rows, LANES)

    for stage in (conv, gate_matmuls, gates):
        for pb in range(PB):
            stage(pb)

    def step(j, hs):
        t = LC - 1 - j if reverse else j
        out = []
        for pb in range(PB):
            rows = rows_of(pb, t, 1)
            h = a_s[rows, :] * hs[pb] + b_s[rows, :]
            if reverse:
                y_s[rows, :] = (hf_ref[pb, t] + h) * gg_ref[pb, t]
            else:
                hf_ref[pb, t] = h
            out.append(h)
        return tuple(out)

    hs = lax.fori_loop(0, LC, step, tuple(hcar[pb] for pb in range(PB)), unroll=8)
    for pb in range(PB):
        hcar[pb] = hs[pb]
        fs_ref[pb] = hs[pb]

    if reverse:
        for pb in range(PB):
            for s in range(PAIR):
                cols = [y_s[pl.ds(pb * LC * TMJ_ROWS + s * RNN_BLOCKS + k, LC, stride=TMJ_ROWS), :]
                        for k in range(RNN_BLOCKS)]
                y_ref[pb * PAIR + s] = jnp.concatenate(cols, axis=1).astype(BF16)


def _scan(xr, gg, hf, h0, conv_w, conv_b, w_gates, b_gates, lam, direction):
    n_pairs, seq_len = xr.shape[:2]
    n_chunks = seq_len // LC
    reverse = direction == 1
    pos = (lambda c: n_chunks - 1 - c) if reverse else (lambda c: c)
    tmj_blk = pl.BlockSpec((PB, LC, TMJ_ROWS, LANES), lambda i, c: (i, pos(c), 0, 0))
    state_blk = pl.BlockSpec((PB, TMJ_ROWS, LANES), lambda i, c: (i, 0, 0))
    per_dir = lambda *shape: pl.BlockSpec((None,) + shape, lambda i, c: (direction,) + (0,) * len(shape))
    in_specs = [
        pl.BlockSpec((PB, CONV_LEFT, TMJ_ROWS, LANES),
                     lambda i, c: (i, jnp.maximum(pos(c) * (LC // CONV_LEFT) - 1, 0), 0, 0)),
        tmj_blk,
        pl.BlockSpec((PB, 1, TMJ_ROWS, LANES), lambda i, c: (i, jnp.minimum((pos(c) + 1) * LC, seq_len - 1), 0, 0)),
    ]
    args = [xr, xr, xr]
    if reverse:
        in_specs += [tmj_blk, tmj_blk]
        args += [gg, hf]
    in_specs += [state_blk,
                 pl.BlockSpec((CONV_W, TMJ_ROWS, LANES), lambda i, c: (0, 0, 0)),
                 pl.BlockSpec((TMJ_ROWS, LANES), lambda i, c: (0, 0)),
                 per_dir(RNN_BLOCKS, LANES, 2 * LANES),
                 per_dir(2, TMJ_ROWS, LANES),
                 per_dir(TMJ_ROWS, LANES)]
    args += [h0, conv_w, conv_b, w_gates, b_gates, lam]
    flat = pltpu.VMEM((PB * LC * TMJ_ROWS, LANES), F32)
    scratch = [pltpu.VMEM((PB, LC + CONV_W - 1, TMJ_ROWS, LANES), F32)] + [flat] * (6 if reverse else 5)
    scratch += [pltpu.VMEM((PB, TMJ_ROWS, LANES), F32)]
    state = jax.ShapeDtypeStruct((n_pairs, TMJ_ROWS, LANES), F32)
    if reverse:
        out_shape = (jax.ShapeDtypeStruct((n_pairs * PAIR, seq_len, D_RNN), BF16), state)
        out_specs = (pl.BlockSpec((PB * PAIR, LC, D_RNN), lambda i, c: (i, pos(c), 0)), state_blk)
    else:
        out_shape = (jax.ShapeDtypeStruct(xr.shape, F32), state)
        out_specs = (tmj_blk, state_blk)
    return pl.pallas_call(
        functools.partial(_scan_kernel, reverse=reverse, n_chunks=n_chunks),
        out_shape=out_shape,
        grid=(n_pairs // PB, n_chunks),
        in_specs=in_specs,
        out_specs=out_specs,
        scratch_shapes=scratch,
        compiler_params=_params(("parallel", "arbitrary")),
        name="scan_bwd" if reverse else "scan_fwd",
    )(*args)


def _route(lt):
    n = lt.shape[1]
    row = lax.broadcasted_iota(jnp.int32, (EXPERTS_PER_GROUP, n), 0)
    neg = jnp.float32(-jnp.inf)

    def arg_max(v):
        m = jnp.max(v, axis=0, keepdims=True)
        return jnp.min(jnp.where(v == m, row, EXPERTS_PER_GROUP), axis=0, keepdims=True)

    g_idx = arg_max(lt[0:N_GROUPS])
    el = lt[E_ROW0:E_ROW0 + EXPERTS_PER_GROUP]
    for g in range(1, N_GROUPS):
        first = E_ROW0 + g * EXPERTS_PER_GROUP
        el = jnp.where(g_idx == g, lt[first:first + EXPERTS_PER_GROUP], el)
    i1 = arg_max(el)
    i2 = arg_max(jnp.where(row == i1, neg, el))
    ja = jnp.minimum(i1, i2)
    jb = jnp.maximum(i1, i2)
    pair = (ja * (2 * EXPERTS_PER_GROUP - 1 - ja)) // 2 + (jb - ja - 1)
    return g_idx * PAIRS_PER_GROUP + pair


def _store_token_major(ref, x, t0=0):
    n = x.shape[0]
    for k in range(ROW_TILES):
        ref[pl.ds(t0 * ROW_TILES + k, n, stride=ROW_TILES), :] = x[:, k * LANES:(k + 1) * LANES]


def _load_token_major(ref, n):
    return jnp.concatenate([ref[pl.ds(k, n, stride=ROW_TILES), :] for k in range(ROW_TILES)], axis=1)


def _postmix_kernel(*refs, add_pos):
    refs = list(refs)
    x_ref = refs.pop(0)
    pos_refs = (refs.pop(0), refs.pop(0)) if add_pos else None
    (yr_ref, ys_ref, mod_ref, gpost_ref, gpre_ref, wout_ref, rw_ref, rb_ref, earlier_ref, cnt0_ref,
     x1_ref, hn_ref, rt_ref, cnt_ref, run_ref) = refs

    @pl.when(pl.program_id(0) == 0)
    def _():
        run_ref[...] = cnt0_ref[...]

    gate1 = mod_ref[0, :, 2 * D_MODEL:3 * D_MODEL]
    shift2 = mod_ref[0, :, 3 * D_MODEL:4 * D_MODEL]
    scale2 = mod_ref[0, :, 4 * D_MODEL:5 * D_MODEL]
    y = (jnp.dot(yr_ref[...], wout_ref[0:D_RNN, :], preferred_element_type=F32)
         + jnp.dot(ys_ref[...], wout_ref[D_RNN:, :], preferred_element_type=F32))
    x1 = _load_x(x_ref, pos_refs, 0, TP) + _rms(y) * (gate1 * gpost_ref[...])
    x1_ref[...] = x1
    hn = _rms(x1) * (gpre_ref[...] * (1.0 + scale2)) + shift2
    _store_token_major(hn_ref, hn)
    lt = lax.dot_general(rw_ref[...], hn.astype(BF16), (((1,), (1,)), ((), ())),
                         preferred_element_type=F32) + rb_ref[:, 0:1]
    bucket = _route(lt)
    onehot = lax.broadcasted_iota(jnp.int32, (ROUTER_ROWS, TP), 0) == bucket
    before = jnp.dot(onehot.astype(BF16), earlier_ref[...], preferred_element_type=F32) + run_ref[:, 0:1]
    rank = jnp.sum(jnp.where(onehot, before, 0.0), axis=0, keepdims=True).astype(jnp.int32)
    row = lax.broadcasted_iota(jnp.int32, (SUBLANES, TP), 0)
    rt_ref[...] = jnp.where(row == 0, bucket, jnp.where(row == 1, rank, 0))
    run_ref[...] += jnp.sum(onehot.astype(F32), axis=1, keepdims=True)
    cnt_ref[...] = run_ref[...]


def _postmix(x, y_rnn, y_sgu, mod3, cond_of_tile, g_post, g_pre, w_out_b, router_wt, router_bt, earlier, pos_tab,
             counts0):
    n_tok = x.shape[0]
    n_tiles = n_tok // TP
    add_pos = pos_tab is not None
    tok = lambda i: (i, 0)
    const2 = lambda i: (0, 0)
    in_specs = [pl.BlockSpec((TP, D_MODEL), tok)]
    args = [x]
    if add_pos:
        reps = TP // GRID_W
        tiles_per_seq = GRID_W // reps
        in_specs += [pl.BlockSpec((None, reps, D_MODEL // 2), lambda i: (i % tiles_per_seq, 0, 0)),
                     pl.BlockSpec((GRID_W, D_MODEL // 2), const2)]
        args += [pos_tab.reshape(tiles_per_seq, reps, D_MODEL // 2), pos_tab]
    in_specs += [pl.BlockSpec((TP, D_RNN), tok),
                 pl.BlockSpec((TP, D_SGU), tok),
                 pl.BlockSpec((1, 1, 6 * D_MODEL), lambda i: (cond_of_tile(i), 0, 0)),
                 pl.BlockSpec((1, D_MODEL), const2),
                 pl.BlockSpec((1, D_MODEL), const2),
                 pl.BlockSpec((D_MODEL, D_MODEL), const2),
                 pl.BlockSpec((ROUTER_ROWS, D_MODEL), const2),
                 pl.BlockSpec((ROUTER_ROWS, LANES), const2),
                 pl.BlockSpec((TP, TP), const2),
                 pl.BlockSpec((ROUTER_ROWS, LANES), const2)]
    args += [y_rnn, y_sgu, mod3, g_post, g_pre, w_out_b, router_wt, router_bt, earlier, counts0]
    counts_spec = pl.BlockSpec((ROUTER_ROWS, LANES), const2)
    x1, hn, route, counts = pl.pallas_call(
        functools.partial(_postmix_kernel, add_pos=add_pos),
        out_shape=(jax.ShapeDtypeStruct((n_tok, D_MODEL), F32),
                   jax.ShapeDtypeStruct((n_tok * ROW_TILES, LANES), F32),
                   jax.ShapeDtypeStruct((n_tiles * SUBLANES, TP), jnp.int32),
                   jax.ShapeDtypeStruct((ROUTER_ROWS, LANES), F32)),
        grid=(n_tiles,),
        in_specs=in_specs,
        out_specs=(pl.BlockSpec((TP, D_MODEL), tok),
                   pl.BlockSpec((TP * ROW_TILES, LANES), tok),
                   pl.BlockSpec((SUBLANES, TP), tok),
                   counts_spec),
        scratch_shapes=[pltpu.VMEM((ROUTER_ROWS, LANES), F32)],
        compiler_params=_params(("arbitrary",)),
        name="postmix",
    )(*args)
    route = route.reshape(n_tiles, SUBLANES, TP)
    return x1, hn, route[:, 0].reshape(n_tok), route[:, 1].reshape(n_tok), counts


def _token_rows(ref, t):
    return ref.at[pl.ds(pl.multiple_of(t * ROW_TILES, ROW_TILES), ROW_TILES), :]


def _dispatch_kernel(dest_ref, hc_ref, hs_ref, xs_ref, sem, *, n_ctx_steps):
    i = pl.program_id(0)
    base = i * TD

    def scatter(src_ref):
        def start(g, carry):
            for u in range(DMA_UNROLL):
                r = g * DMA_UNROLL + u
                pltpu.make_async_copy(_token_rows(src_ref, r), _token_rows(xs_ref, dest_ref[base + r]),
                                      sem).start(priority=u % 2)
            return carry

        lax.fori_loop(0, TD // DMA_UNROLL, start, 0)
        pltpu.make_async_copy(src_ref, xs_ref.at[pl.ds(0, TD * ROW_TILES), :], sem).wait()

    @pl.when(i < n_ctx_steps)
    def _():
        scatter(hc_ref)

    @pl.when(i >= n_ctx_steps)
    def _():
        scatter(hs_ref)


def _dispatch(dest, hn_ctx, hn_dec, n_slots):
    n_ctx_steps = hn_ctx.shape[0] // (TD * ROW_TILES)
    n_dec_steps = hn_dec.shape[0] // (TD * ROW_TILES)
    return pl.pallas_call(
        functools.partial(_dispatch_kernel, n_ctx_steps=n_ctx_steps),
        out_shape=jax.ShapeDtypeStruct((n_slots * ROW_TILES, LANES), F32),
        grid_spec=pltpu.PrefetchScalarGridSpec(
            num_scalar_prefetch=1,
            grid=(n_ctx_steps + n_dec_steps,),
            in_specs=[pl.BlockSpec((TD * ROW_TILES, LANES), lambda i, d: (jnp.minimum(i, n_ctx_steps - 1), 0)),
                      pl.BlockSpec((TD * ROW_TILES, LANES), lambda i, d: (jnp.maximum(i - n_ctx_steps, 0), 0))],
            out_specs=pl.BlockSpec(memory_space=pl.ANY),
            scratch_shapes=[pltpu.SemaphoreType.DMA(())]),
        compiler_params=_params(("arbitrary",)),
        name="dispatch",
    )(dest, hn_ctx, hn_dec)


def _experts_kernel(ea_ref, eb_ref, nv_ref, blk_ref, xs_ref, rw_ref, rb_ref, *refs):
    w32_refs, ys_ref, w_refs = refs[:6], refs[6], refs[7:]
    wga_ref, wua_ref, wda_ref, wgb_ref, wub_ref, wdb_ref = w_refs
    i = pl.program_id(0)
    nv = nv_ref[i]
    prev = jnp.maximum(i - 1, 0)

    @pl.when((i == 0) | (ea_ref[i] != ea_ref[prev]) | (eb_ref[i] != eb_ref[prev]))
    def _():
        for w32_ref, w_ref in zip(w32_refs, w_refs):
            w_ref[...] = w32_ref[0].astype(BF16)

    @pl.when(nv > 0)
    def _():
        row = lax.broadcasted_iota(jnp.int32, (TMX, 1), 0)
        xb = jnp.where(row < nv, _load_token_major(xs_ref, TMX), 0.0).astype(BF16)
        logits = jnp.dot(xb, rw_ref[...], preferred_element_type=F32) + rb_ref[...]
        lane = lax.broadcasted_iota(jnp.int32, logits.shape, 1)
        ea = ea_ref[i]
        eb = eb_ref[i]
        gmask = lane < N_GROUPS
        gl = jnp.where(gmask, logits, -jnp.inf)
        gmax = jnp.max(gl, axis=-1, keepdims=True)
        gexp = jnp.where(gmask, jnp.exp(gl - gmax), 0.0)
        g_own = jnp.sum(jnp.where(lane == ea // EXPERTS_PER_GROUP, gexp, 0.0), axis=-1, keepdims=True)
        g_w = g_own / jnp.sum(gexp, axis=-1, keepdims=True)
        la = jnp.sum(jnp.where(lane == ea + E_LANE0, logits, 0.0), axis=-1, keepdims=True)
        lb = jnp.sum(jnp.where(lane == eb + E_LANE0, logits, 0.0), axis=-1, keepdims=True)
        m = jnp.maximum(la, lb)
        pa = jnp.exp(la - m)
        pb = jnp.exp(lb - m)
        inv = g_w / (pa + pb)

        def hidden(x, wg_ref, wu_ref, w):
            g = jnp.dot(x, wg_ref[...], preferred_element_type=F32)
            u = jnp.dot(x, wu_ref[...], preferred_element_type=F32)
            return ((g * _sigmoid(g)) * u * w).astype(BF16)

        part = TMX // EXPERT_ROW_PARTS
        ys = []
        for h in range(EXPERT_ROW_PARTS):
            rows = slice(h * part, (h + 1) * part)
            act_a = hidden(xb[rows], wga_ref, wua_ref, (pa * inv)[rows])
            act_b = hidden(xb[rows], wgb_ref, wub_ref, (pb * inv)[rows])
            y = (jnp.dot(act_a, wda_ref[...], preferred_element_type=F32)
                 + jnp.dot(act_b, wdb_ref[...], preferred_element_type=F32))
            ys.append(y)
        _store_token_major(ys_ref, jnp.concatenate(ys, axis=0))


def _experts(sched, xs, router_w, router_b, wg, wu, wd):
    ea, eb, nv, blk = sched
    n_tiles = ea.shape[0]
    rows = lambda i, ea, eb, nv, blk: (blk[i], 0)
    const2 = lambda i, ea, eb, nv, blk: (0, 0)
    exp_a = lambda i, ea, eb, nv, blk: (ea[i], 0, 0)
    exp_b = lambda i, ea, eb, nv, blk: (eb[i], 0, 0)
    w_in_spec = lambda m: pl.BlockSpec((1, D_MODEL, D_EXPERT), m)
    w_out_spec = lambda m: pl.BlockSpec((1, D_EXPERT, D_MODEL), m)
    return pl.pallas_call(
        _experts_kernel,
        out_shape=jax.ShapeDtypeStruct(xs.shape, F32),
        grid_spec=pltpu.PrefetchScalarGridSpec(
            num_scalar_prefetch=4,
            grid=(n_tiles,),
            in_specs=[pl.BlockSpec((TMX * ROW_TILES, LANES), rows),
                      pl.BlockSpec((D_MODEL, ROUTER_LANES), const2),
                      pl.BlockSpec((1, ROUTER_LANES), const2),
                      w_in_spec(exp_a), w_in_spec(exp_a), w_out_spec(exp_a),
                      w_in_spec(exp_b), w_in_spec(exp_b), w_out_spec(exp_b)],
            out_specs=pl.BlockSpec((TMX * ROW_TILES, LANES), rows),
            scratch_shapes=[pltpu.VMEM((D_MODEL, D_EXPERT), BF16), pltpu.VMEM((D_MODEL, D_EXPERT), BF16),
                            pltpu.VMEM((D_EXPERT, D_MODEL), BF16)] * 2),
        compiler_params=_params(("arbitrary",)),
        name="experts",
    )(ea, eb, nv, blk, xs, router_w, router_b, wg, wu, wd, wg, wu, wd)


def _combine_kernel(dest_ref, ys_ref, x1_ref, mod_ref, gpost_ref, o_ref, ybuf, sems, *, tile0):
    i = pl.program_id(0)
    n = pl.num_programs(0)

    def fetch(tile, slot):
        base = (tile + tile0) * TC

        def start(g, carry):
            for u in range(DMA_UNROLL):
                r = g * DMA_UNROLL + u
                pltpu.make_async_copy(_token_rows(ys_ref, dest_ref[base + r]), _token_rows(ybuf.at[slot], r),
                                      sems.at[slot]).start(priority=u % 2)
            return carry

        lax.fori_loop(0, TC // DMA_UNROLL, start, 0)

    @pl.when(i == 0)
    def _():
        fetch(0, 0)

    @pl.when(i + 1 < n)
    def _():
        fetch(i + 1, (i + 1) % 2)

    slot = i % 2
    pltpu.make_async_copy(ys_ref.at[pl.ds(0, TC * ROW_TILES), :], ybuf.at[slot], sems.at[slot]).wait()
    gate2 = mod_ref[0, :, 5 * D_MODEL:6 * D_MODEL]
    o_ref[...] = x1_ref[...] + _rms(_load_token_major(ybuf.at[slot], TC)) * (gate2 * gpost_ref[...])


def _combine(dest, ys, x1, mod3, g_post, cond_of_tile, tile0):
    n_tok = x1.shape[0]
    return pl.pallas_call(
        functools.partial(_combine_kernel, tile0=tile0),
        out_shape=jax.ShapeDtypeStruct((n_tok, D_MODEL), F32),
        grid_spec=pltpu.PrefetchScalarGridSpec(
            num_scalar_prefetch=1,
            grid=(n_tok // TC,),
            in_specs=[pl.BlockSpec(memory_space=pl.ANY),
                      pl.BlockSpec((TC, D_MODEL), lambda i, d: (i, 0)),
                      pl.BlockSpec((1, 1, 6 * D_MODEL), lambda i, d: (cond_of_tile(i), 0, 0)),
                      pl.BlockSpec((1, D_MODEL), lambda i, d: (0, 0))],
            out_specs=pl.BlockSpec((TC, D_MODEL), lambda i, d: (i, 0)),
            scratch_shapes=[pltpu.VMEM((2, TC * ROW_TILES, LANES), F32), pltpu.SemaphoreType.DMA((2,))]),
        compiler_params=_params(("arbitrary",)),
        name="combine",
    )(dest, ys, x1, mod3, g_post)


def _schedule(bucket, rank, counts):
    n_tok = bucket.shape[0]
    n_max = n_tok // TMX + N_BUCKETS
    cnt = counts[:N_BUCKETS, 0].astype(jnp.int32)
    tiles = (cnt + TMX - 1) // TMX
    tile_end = jnp.cumsum(tiles)
    tile_start = tile_end - tiles
    ids = jnp.arange(N_BUCKETS, dtype=jnp.int32)
    slot0 = jnp.sum(jnp.where(bucket[:, None] == ids[None, :], (tile_start * TMX)[None, :], 0), axis=1)
    dest = slot0 + rank
    i = jnp.arange(n_max, dtype=jnp.int32)
    total = tile_end[-1]
    valid = i < total
    tb = jnp.sum((jnp.minimum(i, total - 1)[:, None] >= tile_end[None, :]).astype(jnp.int32), axis=1)
    pairs = [(a, b) for a in range(EXPERTS_PER_GROUP) for b in range(a + 1, EXPERTS_PER_GROUP)]
    ea_tab = jnp.array([g * EXPERTS_PER_GROUP + a for g in range(N_GROUPS) for a, _ in pairs], jnp.int32)
    eb_tab = jnp.array([g * EXPERTS_PER_GROUP + b for g in range(N_GROUPS) for _, b in pairs], jnp.int32)
    ea = ea_tab[tb]
    eb = eb_tab[tb]
    nv = jnp.where(valid, jnp.clip(cnt[tb] - (i - tile_start[tb]) * TMX, 0, TMX), 0)
    blk = jnp.minimum(i, total - 1)
    return dest, (ea, eb, nv, blk), n_max * TMX


def _block_diag_gates(rg_wa, rg_wx):
    heads = LANES // HEAD_RNN

    def bd(w):
        w = w.reshape(2, RNN_BLOCKS, heads, HEAD_RNN, HEAD_RNN)
        eye = jnp.eye(heads, dtype=w.dtype)
        full = jnp.einsum('dghij,hk->dghikj', w, eye)
        return full.reshape(2, RNN_BLOCKS, LANES, LANES)

    return jnp.concatenate([bd(rg_wa), bd(rg_wx)], axis=-1).astype(BF16)


def _row_tile(v):
    blocks = v.reshape(v.shape[:-1] + (RNN_BLOCKS, LANES))
    return jnp.concatenate([blocks] * PAIR, axis=-2)


def _to_time_major_state(h):
    return h.reshape(h.shape[0] // PAIR, TMJ_ROWS, LANES)


def kernel(x_prompt, x_sample, state_rglru, c, c_ctx, w_mod, b_mod, g_pre_mix, g_post_mix, g_pre_ffn,
           g_post_ffn, w_in, conv_w, conv_b, rg_wa, rg_ba, rg_wx, rg_bx, rg_lambda, sgu_g, sgu_w, sgu_b,
           w_out, router_g_w, router_g_b, router_e_w, router_e_b, exp_w_gate, exp_w_up, exp_w_down):
    assert w_mod.shape[0] == 1, "single-layer trunk"
    n_ctx, ctx_len, _ = x_prompt.shape
    n_dec, dec_len, _ = x_sample.shape
    l = 0

    n_cond = SUBLANES
    assert n_dec % PAIR == 0 and n_dec + PAIR <= n_cond
    cond = jnp.concatenate([c, jnp.broadcast_to(c_ctx, (PAIR, D_MODEL)),
                            jnp.zeros((n_cond - n_dec - PAIR, D_MODEL), F32)], axis=0)
    mod3 = _modulation(cond, w_mod[l], b_mod[l]).reshape(n_cond, 1, 6 * D_MODEL)
    pos_tab = _pos_table()

    w_rx, w_gate, w_u, w_v = jnp.split(w_in[l], [D_RNN, 2 * D_RNN, 2 * D_RNN + D_SGU], axis=1)
    w_in_b = jnp.concatenate([w_v, w_u, w_gate, w_rx], axis=1).astype(BF16)
    w_out_b = w_out[l].astype(BF16)
    sgu_w_b = sgu_w[l].reshape(2, 4 * CHUNK, CHUNK).astype(BF16)
    sgu_bias_tile = jnp.repeat(sgu_b[l].T, HEAD_SGU, axis=1)
    w_gates = _block_diag_gates(rg_wa[l], rg_wx[l])
    b_gates = 0.5 * jnp.stack([_row_tile(rg_ba[l]), _row_tile(rg_bx[l])], axis=1)
    lam = _row_tile(rg_lambda[l])
    conv_w_t = 0.5 * _row_tile(conv_w[l])
    conv_b_t = 0.5 * _row_tile(conv_b[l])
    lane_pad = ROUTER_LANES - E_LANE0 - N_EXPERTS
    router_w = jnp.pad(jnp.concatenate([router_g_w[l], router_e_w[l]], axis=1), ((0, 0), (0, lane_pad))).astype(BF16)
    router_b = jnp.pad(jnp.concatenate([router_g_b[l], router_e_b[l]]), (0, lane_pad)).reshape(1, ROUTER_LANES)
    gap, tail = E_ROW0 - N_GROUPS, ROUTER_ROWS - E_ROW0 - N_EXPERTS
    router_wt = jnp.concatenate([router_g_w[l].T, jnp.zeros((gap, D_MODEL), F32), router_e_w[l].T,
                                 jnp.zeros((tail, D_MODEL), F32)], axis=0).astype(BF16)
    router_bt = jnp.concatenate([router_g_b[l], jnp.zeros((gap,), F32), router_e_b[l], jnp.zeros((tail,), F32)])
    router_bt = jnp.broadcast_to(router_bt[:, None], (ROUTER_ROWS, LANES))
    earlier = jnp.triu(jnp.ones((TP, TP), BF16), k=1)
    row = lambda v: v.reshape(1, -1)

    n_ctx_tok = n_ctx * ctx_len

    def mixer(x, h0, cond_of_tile, cond_block, use_pos, counts0):
        n_seq, seq_len, _ = x.shape
        xf = x.reshape(n_seq * seq_len, D_MODEL)
        tab = pos_tab if use_pos else None
        xr, gg, y_sgu = _premix(x, mod3, cond_block, row(g_pre_mix[l]), w_in_b, row(sgu_g[l]),
                                sgu_w_b, sgu_bias_tile, tab)
        scan_params = (conv_w_t, conv_b_t, w_gates, b_gates, lam)
        hf, hf_last = _scan(xr, None, None, _to_time_major_state(h0[:, 0]), *scan_params, direction=0)
        y_rnn, hb_first = _scan(xr, gg, hf, _to_time_major_state(h0[:, 1]), *scan_params, direction=1)
        fstate = jnp.stack([hf_last.reshape(n_seq, D_RNN), hb_first.reshape(n_seq, D_RNN)], axis=1)
        x1, hn, bucket, rank, counts = _postmix(
            xf, y_rnn.reshape(n_seq * seq_len, D_RNN), y_sgu.reshape(n_seq * seq_len, D_SGU), mod3,
            cond_of_tile, row(g_post_mix[l]), row(g_pre_ffn[l]), w_out_b, router_wt, router_bt, earlier, tab,
            counts0)
        return x1, hn, bucket, rank, counts, fstate

    ctx_cond = lambda i: n_dec
    dec_cond = lambda i: i // (dec_len // TP)
    h0_ctx = jnp.zeros((n_ctx, 2, D_RNN), F32)
    counts0 = jnp.zeros((ROUTER_ROWS, LANES), F32)
    x1_ctx, hn_ctx, bucket_ctx, rank_ctx, counts, st = mixer(x_prompt, h0_ctx, ctx_cond, lambda p: n_dec // PAIR,
                                                             False, counts0)
    new_state = st.astype(state_rglru.dtype)[:, None]
    x1_dec, hn_dec, bucket_dec, rank_dec, counts, _ = mixer(x_sample, state_rglru[:, l].astype(F32), dec_cond,
                                                            lambda p: p, True, counts)

    dest, sched, n_slots = _schedule(jnp.concatenate([bucket_ctx, bucket_dec]),
                                     jnp.concatenate([rank_ctx, rank_dec]), counts)
    xs = _dispatch(dest, hn_ctx, hn_dec, n_slots)
    ys = _experts(sched, xs, router_w, router_b, exp_w_gate[l], exp_w_up[l], exp_w_down[l])
    y_prompt = _combine(dest, ys, x1_ctx, mod3, row(g_post_ffn[l]), ctx_cond, 0)
    y_sample = _combine(dest, ys, x1_dec, mod3, row(g_post_ffn[l]), lambda i: i // (dec_len // TC), n_ctx_tok // TC)
    return (y_prompt.reshape(x_prompt.shape), y_sample.reshape(x_sample.shape), new_state)
```

```python
import functools
import math

import jax
import jax.numpy as jnp
from jax import lax
from jax.experimental import pallas as pl
from jax.experimental.pallas import tpu as pltpu

D_MODEL = 1024
D_RNN = 512
D_SGU = 512
N_HEADS_RNN = 8
HEAD_RNN = D_RNN // N_HEADS_RNN
N_HEADS_SGU = 8
HEAD_SGU = D_SGU // N_HEADS_SGU
CHUNK = 128
GRID_W = 64
RG_C = 8.0
N_GROUPS = 4
EXPERTS_PER_GROUP = 4
N_EXPERTS = N_GROUPS * EXPERTS_PER_GROUP
D_EXPERT = 512
EPS = 1e-6
POS_BASE = 10000.0

LANES = 128
SUBLANES = 8
CONV_W = 4
CONV_LEFT = 2
PAIR = 2
RNN_BLOCKS = D_RNN // LANES
TMJ_ROWS = PAIR * RNN_BLOCKS
ROUTER_LANES = LANES
E_LANE0 = N_GROUPS
ROUTER_ROWS = 32
E_ROW0 = SUBLANES

PAIRS_PER_GROUP = EXPERTS_PER_GROUP * (EXPERTS_PER_GROUP - 1) // 2
N_BUCKETS = N_GROUPS * PAIRS_PER_GROUP

ROW_TILES = D_MODEL // LANES

MOD_COLS = 1024
TM = 512
TP = 1024
TD = 2048
TC = 512
DMA_UNROLL = 8
EXPERT_ROW_PARTS = 2
TMX = 512
TT = TM // PAIR
LC = 256
TS = 16
PB = 2
VMEM_LIMIT = 56 * 1024 * 1024

F32 = jnp.float32
BF16 = jnp.bfloat16


def _params(sem):
    return pltpu.CompilerParams(dimension_semantics=sem, vmem_limit_bytes=VMEM_LIMIT)


def _rms(x):
    return x * lax.rsqrt(jnp.mean(x * x, axis=-1, keepdims=True) + EPS)


def _sigmoid(x):
    return 0.5 * jnp.tanh(0.5 * x) + 0.5


def _mod_kernel(cond_ref, w_ref, b_ref, o_ref):
    c = cond_ref[...]
    s = c * _sigmoid(c)
    o_ref[...] = jnp.dot(s.astype(BF16), w_ref[...].astype(BF16),
                         preferred_element_type=F32) + b_ref[...]


def _modulation(cond, w_mod, b_mod):
    n = w_mod.shape[1]
    return pl.pallas_call(
        _mod_kernel,
        out_shape=jax.ShapeDtypeStruct((cond.shape[0], n), F32),
        grid=(n // MOD_COLS,),
        in_specs=[pl.BlockSpec(cond.shape, lambda j: (0, 0)),
                  pl.BlockSpec((D_MODEL, MOD_COLS), lambda j: (0, j)),
                  pl.BlockSpec((1, MOD_COLS), lambda j: (0, j))],
        out_specs=pl.BlockSpec((cond.shape[0], MOD_COLS), lambda j: (0, j)),
        compiler_params=_params(("arbitrary",)),
        name="modulation",
    )(cond, w_mod, b_mod.reshape(1, n))


def _pos_kernel(o_ref):
    n_freq = D_MODEL // 4
    k = lax.broadcasted_iota(jnp.int32, (GRID_W, n_freq), 1).astype(F32)
    p = lax.broadcasted_iota(jnp.int32, (GRID_W, n_freq), 0).astype(F32)
    freq = jnp.exp(-math.log(POS_BASE) * k / n_freq)
    ang = p * freq
    o_ref[:, 0:n_freq] = jnp.sin(ang)
    o_ref[:, n_freq:2 * n_freq] = jnp.cos(ang)


def _pos_table():
    return pl.pallas_call(
        _pos_kernel,
        out_shape=jax.ShapeDtypeStruct((GRID_W, D_MODEL // 2), F32),
        name="pos_table",
    )()


def _add_pos(x, pos_refs, q0):
    if pos_refs is None:
        return x
    rows_ref, cols_ref = pos_refs
    reps = x.shape[0] // GRID_W
    rpart = jnp.concatenate(
        [jnp.broadcast_to(rows_ref[q:q + 1, :], (GRID_W, D_MODEL // 2)) for q in range(q0, q0 + reps)], axis=0)
    cpart = jnp.concatenate([cols_ref[...]] * reps, axis=0)
    return jnp.concatenate([x[:, :D_MODEL // 2] + rpart, x[:, D_MODEL // 2:] + cpart], axis=1)


def _load_x(x_ref, pos_refs, r0, n):
    return _add_pos(x_ref[r0:r0 + n, :], pos_refs, r0 // GRID_W)


def _premix_kernel(*refs, add_pos):
    refs = list(refs)
    x_ref = refs.pop(0)
    pos_refs = (refs.pop(0), refs.pop(0)) if add_pos else None
    mod_ref, g_ref, win_ref, sgug_ref, sguw_ref, sgub_ref, xr_ref, gg_ref, ys_ref = refs
    hn = []
    for s in range(PAIR):
        shift = mod_ref[s, :, 0:D_MODEL]
        scale = mod_ref[s, :, D_MODEL:2 * D_MODEL]
        hn.append(_rms(_add_pos(x_ref[s], pos_refs, 0)) * (g_ref[...] * (1.0 + scale)) + shift)
    z = jnp.dot(jnp.concatenate(hn, axis=0).astype(BF16), win_ref[...],
                preferred_element_type=F32)
    half = D_SGU // 2
    heads_per_half = N_HEADS_SGU // 2
    lane_head = lax.broadcasted_iota(jnp.int32, (CHUNK, half), 1) // HEAD_SGU
    for s in range(PAIR):
        zs = z[s * TT:(s + 1) * TT]
        vn = (_rms(zs[:, 0:D_SGU]) * sgug_ref[...]).astype(BF16)
        u = zs[:, D_SGU:2 * D_SGU]
        for c in range(TT // CHUNK):
            rows = slice(c * CHUNK, (c + 1) * CHUNK)
            halves = []
            for hf in range(2):
                r = jnp.dot(sguw_ref[hf], vn[rows, hf * half:(hf + 1) * half],
                            preferred_element_type=F32)
                sel = jnp.zeros((CHUNK, half), F32)
                for h in range(heads_per_half):
                    sel = jnp.where(lane_head == h, r[h * CHUNK:(h + 1) * CHUNK], sel)
                halves.append(sel)
            gatev = jnp.concatenate(halves, axis=1) + sgub_ref[...]
            ys_ref[s, rows, :] = (u[rows] * gatev).astype(BF16)
    for s in range(PAIR):
        zs = z[s * TT:(s + 1) * TT]
        gg = jax.nn.gelu(zs[:, 2 * D_SGU:2 * D_SGU + D_RNN])
        xr = zs[:, 2 * D_SGU + D_RNN:]
        for k in range(RNN_BLOCKS):
            rows = pl.ds(s * RNN_BLOCKS + k, TT, stride=TMJ_ROWS)
            xr_ref[rows, :] = xr[:, k * LANES:(k + 1) * LANES]
            gg_ref[rows, :] = gg[:, k * LANES:(k + 1) * LANES]


def _premix(x, mod3, cond_block, g_pre, w_in_b, sgu_g, sgu_w_b, sgu_bias_tile, pos_tab):
    n_seq, seq_len, _ = x.shape
    n_pairs, n_tiles = n_seq // PAIR, seq_len // TT
    add_pos = pos_tab is not None
    const2 = lambda p, j: (0, 0)
    in_specs = [pl.BlockSpec((PAIR, TT, D_MODEL), lambda p, j: (p, j, 0))]
    args = [x]
    if add_pos:
        reps = TT // GRID_W
        in_specs += [pl.BlockSpec((None, reps, D_MODEL // 2), lambda p, j: (j, 0, 0)),
                     pl.BlockSpec((GRID_W, D_MODEL // 2), const2)]
        args += [pos_tab.reshape(GRID_W // reps, reps, D_MODEL // 2), pos_tab]
    in_specs += [pl.BlockSpec((PAIR, 1, 6 * D_MODEL), lambda p, j: (cond_block(p), 0, 0)),
                 pl.BlockSpec((1, D_MODEL), const2),
                 pl.BlockSpec((D_MODEL, 2 * D_RNN + 2 * D_SGU), const2),
                 pl.BlockSpec((1, D_SGU), const2),
                 pl.BlockSpec((2, 4 * CHUNK, CHUNK), lambda p, j: (0, 0, 0)),
                 pl.BlockSpec((CHUNK, D_SGU), const2)]
    args += [mod3, g_pre, w_in_b, sgu_g, sgu_w_b, sgu_bias_tile]
    tmj = jax.ShapeDtypeStruct((n_pairs * seq_len * TMJ_ROWS, LANES), F32)
    tmj_spec = pl.BlockSpec((TT * TMJ_ROWS, LANES), lambda p, j: (p * n_tiles + j, 0))
    xr, gg, y_sgu = pl.pallas_call(
        functools.partial(_premix_kernel, add_pos=add_pos),
        out_shape=(tmj, tmj, jax.ShapeDtypeStruct((n_seq, seq_len, D_SGU), BF16)),
        grid=(n_pairs, n_tiles),
        in_specs=in_specs,
        out_specs=(tmj_spec, tmj_spec, pl.BlockSpec((PAIR, TT, D_SGU), lambda p, j: (p, j, 0))),
        compiler_params=_params(("parallel", "parallel")),
        name="premix",
    )(*args)
    shape4 = (n_pairs, seq_len, TMJ_ROWS, LANES)
    return xr.reshape(shape4), gg.reshape(shape4), y_sgu


def _scan_kernel(*refs, reverse, n_chunks):
    if reverse:
        (xprev_ref, x_ref, xnext_ref, gg_ref, hf_ref, h0_ref, cw_ref, cb_ref, wg_ref, bg_ref, lam_ref,
         y_ref, fs_ref, xwin, xc_s, r_s, i_s, a_s, b_s, y_s, hcar) = refs
    else:
        (xprev_ref, x_ref, xnext_ref, h0_ref, cw_ref, cb_ref, wg_ref, bg_ref, lam_ref,
         hf_ref, fs_ref, xwin, xc_s, r_s, i_s, a_s, b_s, hcar) = refs
    c = pl.program_id(1)
    chunk = n_chunks - 1 - c if reverse else c
    sub_rows = TS * TMJ_ROWS

    def rows_of(pb, t0, n_steps):
        first = (pb * LC + t0) * TMJ_ROWS
        if not isinstance(first, int):
            first = pl.multiple_of(first, TMJ_ROWS)
        return pl.ds(first, n_steps * TMJ_ROWS)

    @pl.when(c == 0)
    def _():
        hcar[...] = h0_ref[...]

    xwin[:, 0:CONV_LEFT] = jnp.where(chunk > 0, xprev_ref[...], 0.0)
    xwin[:, LC + CONV_LEFT:LC + CONV_W - 1] = jnp.where(chunk < n_chunks - 1, xnext_ref[...], 0.0)

    xwin[:, CONV_LEFT:CONV_LEFT + LC] = x_ref[...]

    neg_lam = -lam_ref[...]
    softplus = jnp.maximum(neg_lam, 0.0) + jnp.log(1.0 + jnp.exp(-jnp.abs(neg_lam)))
    half_decay = (-0.5 * RG_C * math.log2(math.e)) * softplus

    def conv(pb):
        for t0 in range(0, LC, TS):
            xc = cb_ref[...] + cw_ref[0] * xwin[pb, t0:t0 + TS]
            for k in range(1, CONV_W):
                xc = xc + cw_ref[k] * xwin[pb, t0 + k:t0 + k + TS]
            xc_s[rows_of(pb, t0, TS), :] = xc.reshape(sub_rows, LANES)

    def gate_matmuls(pb):
        for k in range(RNN_BLOCKS):
            rows = pl.ds(pb * LC * TMJ_ROWS + k, LC * PAIR, stride=RNN_BLOCKS)
            g = jnp.dot(xc_s[rows, :].astype(BF16), wg_ref[k], preferred_element_type=F32)
            r_s[rows, :] = g[:, :LANES]
            i_s[rows, :] = g[:, LANES:]

    def gates(pb):
        for t0 in range(0, LC, TS):
            rows = rows_of(pb, t0, TS)
            tile = lambda ref: ref[rows, :].reshape(TS, TMJ_ROWS, LANES)
            tr = jnp.tanh(tile(r_s) + bg_ref[0])
            ti = jnp.tanh(tile(i_s) + bg_ref[1])
            log2_a = tr * half_decay + half_decay
            a = jnp.exp2(log2_a)
            q = jnp.tanh(log2_a * (-math.log(2.0))) * (a * a + 1.0)
            b = jnp.where(q > 0.0, q * lax.rsqrt(q), 0.0) * ((ti + 1.0) * tile(xc_s))
            a_s[rows, :] = a.reshape(sub_rows, LANES)
            b_s[rows, :] = b.reshape(sub_rows, LANES)

    for stage in (conv, gate_matmuls, gates):
        for pb in range(PB):
            stage(pb)

    def step(j, hs):
        t = LC - 1 - j if reverse else j
        out = []
        for pb in range(PB):
            rows = rows_of(pb, t, 1)
            h = a_s[rows, :] * hs[pb] + b_s[rows, :]
            if reverse:
                y_s[rows, :] = h
            else:
                hf_ref[pb, t] = h
            out.append(h)
        return tuple(out)

    hs = lax.fori_loop(0, LC, step, tuple(hcar[pb] for pb in range(PB)), unroll=8)
    for pb in range(PB):
        hcar[pb] = hs[pb]
        fs_ref[pb] = hs[pb]

    if reverse:
        for pb in range(PB):
            for t0 in range(0, LC, TS):
                rows = rows_of(pb, t0, TS)
                both = hf_ref[pb, t0:t0 + TS].reshape(sub_rows, LANES) + y_s[rows, :]
                y_s[rows, :] = both * gg_ref[pb, t0:t0 + TS].reshape(sub_rows, LANES)
            for s in range(PAIR):
                cols = [y_s[pl.ds(pb * LC * TMJ_ROWS + s * RNN_BLOCKS + k, LC, stride=TMJ_ROWS), :]
                        for k in range(RNN_BLOCKS)]
                y_ref[pb * PAIR + s] = jnp.concatenate(cols, axis=1).astype(BF16)


def _scan(xr, gg, hf, h0, conv_w, conv_b, w_gates, b_gates, lam, direction):
    n_pairs, seq_len = xr.shape[:2]
    n_chunks = seq_len // LC
    reverse = direction == 1
    pos = (lambda c: n_chunks - 1 - c) if reverse else (lambda c: c)
    tmj_blk = pl.BlockSpec((PB, LC, TMJ_ROWS, LANES), lambda i, c: (i, pos(c), 0, 0))
    state_blk = pl.BlockSpec((PB, TMJ_ROWS, LANES), lambda i, c: (i, 0, 0))
    per_dir = lambda *shape: pl.BlockSpec((None,) + shape, lambda i, c: (direction,) + (0,) * len(shape))
    in_specs = [
        pl.BlockSpec((PB, CONV_LEFT, TMJ_ROWS, LANES),
                     lambda i, c: (i, jnp.maximum(pos(c) * (LC // CONV_LEFT) - 1, 0), 0, 0)),
        tmj_blk,
        pl.BlockSpec((PB, 1, TMJ_ROWS, LANES), lambda i, c: (i, jnp.minimum((pos(c) + 1) * LC, seq_len - 1), 0, 0)),
    ]
    args = [xr, xr, xr]
    if reverse:
        in_specs += [tmj_blk, tmj_blk]
        args += [gg, hf]
    in_specs += [state_blk,
                 pl.BlockSpec((CONV_W, TMJ_ROWS, LANES), lambda i, c: (0, 0, 0)),
                 pl.BlockSpec((TMJ_ROWS, LANES), lambda i, c: (0, 0)),
                 per_dir(RNN_BLOCKS, LANES, 2 * LANES),
                 per_dir(2, TMJ_ROWS, LANES),
                 per_dir(TMJ_ROWS, LANES)]
    args += [h0, conv_w, conv_b, w_gates, b_gates, lam]
    flat = pltpu.VMEM((PB * LC * TMJ_ROWS, LANES), F32)
    scratch = [pltpu.VMEM((PB, LC + CONV_W - 1, TMJ_ROWS, LANES), F32)] + [flat] * (6 if reverse else 5)
    scratch += [pltpu.VMEM((PB, TMJ_ROWS, LANES), F32)]
    state = jax.ShapeDtypeStruct((n_pairs, TMJ_ROWS, LANES), F32)
    if reverse:
        out_shape = (jax.ShapeDtypeStruct((n_pairs * PAIR, seq_len, D_RNN), BF16), state)
        out_specs = (pl.BlockSpec((PB * PAIR, LC, D_RNN), lambda i, c: (i, pos(c), 0)), state_blk)
    else:
        out_shape = (jax.ShapeDtypeStruct(xr.shape, F32), state)
        out_specs = (tmj_blk, state_blk)
    return pl.pallas_call(
        functools.partial(_scan_kernel, reverse=reverse, n_chunks=n_chunks),
        out_shape=out_shape,
        grid=(n_pairs // PB, n_chunks),
        in_specs=in_specs,
        out_specs=out_specs,
        scratch_shapes=scratch,
        compiler_params=_params(("parallel", "arbitrary")),
        name="scan_bwd" if reverse else "scan_fwd",
    )(*args)


def _route(lt):
    n = lt.shape[1]
    row = lax.broadcasted_iota(jnp.int32, (EXPERTS_PER_GROUP, n), 0)
    neg = jnp.float32(-jnp.inf)

    def arg_max(v):
        m = jnp.max(v, axis=0, keepdims=True)
        return jnp.min(jnp.where(v == m, row, EXPERTS_PER_GROUP), axis=0, keepdims=True)

    g_idx = arg_max(lt[0:N_GROUPS])
    el = lt[E_ROW0:E_ROW0 + EXPERTS_PER_GROUP]
    for g in range(1, N_GROUPS):
        first = E_ROW0 + g * EXPERTS_PER_GROUP
        el = jnp.where(g_idx == g, lt[first:first + EXPERTS_PER_GROUP], el)
    i1 = arg_max(el)
    i2 = arg_max(jnp.where(row == i1, neg, el))
    ja = jnp.minimum(i1, i2)
    jb = jnp.maximum(i1, i2)
    pair = (ja * (2 * EXPERTS_PER_GROUP - 1 - ja)) // 2 + (jb - ja - 1)
    return g_idx * PAIRS_PER_GROUP + pair


def _store_token_major(ref, x, t0=0):
    n = x.shape[0]
    for k in range(ROW_TILES):
        ref[pl.ds(t0 * ROW_TILES + k, n, stride=ROW_TILES), :] = x[:, k * LANES:(k + 1) * LANES]


def _load_token_major(ref, n):
    return jnp.concatenate([ref[pl.ds(k, n, stride=ROW_TILES), :] for k in range(ROW_TILES)], axis=1)


def _postmix_kernel(*refs, add_pos):
    refs = list(refs)
    x_ref = refs.pop(0)
    pos_refs = (refs.pop(0), refs.pop(0)) if add_pos else None
    (yr_ref, ys_ref, mod_ref, gpost_ref, gpre_ref, wout_ref, rw_ref, rb_ref, earlier_ref, cnt0_ref,
     x1_ref, hn_ref, rt_ref, cnt_ref, run_ref) = refs

    @pl.when(pl.program_id(0) == 0)
    def _():
        run_ref[...] = cnt0_ref[...]

    gate1 = mod_ref[0, :, 2 * D_MODEL:3 * D_MODEL]
    shift2 = mod_ref[0, :, 3 * D_MODEL:4 * D_MODEL]
    scale2 = mod_ref[0, :, 4 * D_MODEL:5 * D_MODEL]
    y = (jnp.dot(yr_ref[...], wout_ref[0:D_RNN, :], preferred_element_type=F32)
         + jnp.dot(ys_ref[...], wout_ref[D_RNN:, :], preferred_element_type=F32))
    x1 = _load_x(x_ref, pos_refs, 0, TP) + _rms(y) * (gate1 * gpost_ref[...])
    x1_ref[...] = x1
    hn = _rms(x1) * (gpre_ref[...] * (1.0 + scale2)) + shift2
    _store_token_major(hn_ref, hn)
    lt = lax.dot_general(rw_ref[...], hn.astype(BF16), (((1,), (1,)), ((), ())),
                         preferred_element_type=F32) + rb_ref[:, 0:1]
    bucket = _route(lt)
    onehot = lax.broadcasted_iota(jnp.int32, (ROUTER_ROWS, TP), 0) == bucket
    before = jnp.dot(onehot.astype(BF16), earlier_ref[...], preferred_element_type=F32) + run_ref[:, 0:1]
    rank = jnp.sum(jnp.where(onehot, before, 0.0), axis=0, keepdims=True).astype(jnp.int32)
    row = lax.broadcasted_iota(jnp.int32, (SUBLANES, TP), 0)
    rt_ref[...] = jnp.where(row == 0, bucket, jnp.where(row == 1, rank, 0))
    run_ref[...] += jnp.sum(onehot.astype(F32), axis=1, keepdims=True)
    cnt_ref[...] = run_ref[...]


def _postmix(x, y_rnn, y_sgu, mod3, cond_of_tile, g_post, g_pre, w_out_b, router_wt, router_bt, earlier, pos_tab,
             counts0):
    n_tok = x.shape[0]
    n_tiles = n_tok // TP
    add_pos = pos_tab is not None
    tok = lambda i: (i, 0)
    const2 = lambda i: (0, 0)
    in_specs = [pl.BlockSpec((TP, D_MODEL), tok)]
    args = [x]
    if add_pos:
        reps = TP // GRID_W
        tiles_per_seq = GRID_W // reps
        in_specs += [pl.BlockSpec((None, reps, D_MODEL // 2), lambda i: (i % tiles_per_seq, 0, 0)),
                     pl.BlockSpec((GRID_W, D_MODEL // 2), const2)]
        args += [pos_tab.reshape(tiles_per_seq, reps, D_MODEL // 2), pos_tab]
    in_specs += [pl.BlockSpec((TP, D_RNN), tok),
                 pl.BlockSpec((TP, D_SGU), tok),
                 pl.BlockSpec((1, 1, 6 * D_MODEL), lambda i: (cond_of_tile(i), 0, 0)),
                 pl.BlockSpec((1, D_MODEL), const2),
                 pl.BlockSpec((1, D_MODEL), const2),
                 pl.BlockSpec((D_MODEL, D_MODEL), const2),
                 pl.BlockSpec((ROUTER_ROWS, D_MODEL), const2),
                 pl.BlockSpec((ROUTER_ROWS, LANES), const2),
                 pl.BlockSpec((TP, TP), const2),
                 pl.BlockSpec((ROUTER_ROWS, LANES), const2)]
    args += [y_rnn, y_sgu, mod3, g_post, g_pre, w_out_b, router_wt, router_bt, earlier, counts0]
    counts_spec = pl.BlockSpec((ROUTER_ROWS, LANES), const2)
    x1, hn, route, counts = pl.pallas_call(
        functools.partial(_postmix_kernel, add_pos=add_pos),
        out_shape=(jax.ShapeDtypeStruct((n_tok, D_MODEL), F32),
                   jax.ShapeDtypeStruct((n_tok * ROW_TILES, LANES), F32),
                   jax.ShapeDtypeStruct((n_tiles * SUBLANES, TP), jnp.int32),
                   jax.ShapeDtypeStruct((ROUTER_ROWS, LANES), F32)),
        grid=(n_tiles,),
        in_specs=in_specs,
        out_specs=(pl.BlockSpec((TP, D_MODEL), tok),
                   pl.BlockSpec((TP * ROW_TILES, LANES), tok),
                   pl.BlockSpec((SUBLANES, TP), tok),
                   counts_spec),
        scratch_shapes=[pltpu.VMEM((ROUTER_ROWS, LANES), F32)],
        compiler_params=_params(("arbitrary",)),
        name="postmix",
    )(*args)
    route = route.reshape(n_tiles, SUBLANES, TP)
    return x1, hn, route[:, 0].reshape(n_tok), route[:, 1].reshape(n_tok), counts


def _token_rows(ref, t):
    return ref.at[pl.ds(pl.multiple_of(t * ROW_TILES, ROW_TILES), ROW_TILES), :]


def _dispatch_kernel(dest_ref, hc_ref, hs_ref, xs_ref, sem, *, n_ctx_steps):
    i = pl.program_id(0)
    base = i * TD

    def scatter(src_ref):
        def start(g, carry):
            for u in range(DMA_UNROLL):
                r = g * DMA_UNROLL + u
                pltpu.make_async_copy(_token_rows(src_ref, r), _token_rows(xs_ref, dest_ref[base + r]),
                                      sem).start(priority=u % 2)
            return carry

        lax.fori_loop(0, TD // DMA_UNROLL, start, 0)
        pltpu.make_async_copy(src_ref, xs_ref.at[pl.ds(0, TD * ROW_TILES), :], sem).wait()

    @pl.when(i < n_ctx_steps)
    def _():
        scatter(hc_ref)

    @pl.when(i >= n_ctx_steps)
    def _():
        scatter(hs_ref)


def _dispatch(dest, hn_ctx, hn_dec, n_slots):
    n_ctx_steps = hn_ctx.shape[0] // (TD * ROW_TILES)
    n_dec_steps = hn_dec.shape[0] // (TD * ROW_TILES)
    return pl.pallas_call(
        functools.partial(_dispatch_kernel, n_ctx_steps=n_ctx_steps),
        out_shape=jax.ShapeDtypeStruct((n_slots * ROW_TILES, LANES), F32),
        grid_spec=pltpu.PrefetchScalarGridSpec(
            num_scalar_prefetch=1,
            grid=(n_ctx_steps + n_dec_steps,),
            in_specs=[pl.BlockSpec((TD * ROW_TILES, LANES), lambda i, d: (jnp.minimum(i, n_ctx_steps - 1), 0)),
                      pl.BlockSpec((TD * ROW_TILES, LANES), lambda i, d: (jnp.maximum(i - n_ctx_steps, 0), 0))],
            out_specs=pl.BlockSpec(memory_space=pl.ANY),
            scratch_shapes=[pltpu.SemaphoreType.DMA(())]),
        compiler_params=_params(("arbitrary",)),
        name="dispatch",
    )(dest, hn_ctx, hn_dec)


def _experts_kernel(ea_ref, eb_ref, nv_ref, blk_ref, xs_ref, rw_ref, rb_ref, *refs):
    w32_refs, ys_ref, w_refs = refs[:6], refs[6], refs[7:]
    wga_ref, wua_ref, wda_ref, wgb_ref, wub_ref, wdb_ref = w_refs
    i = pl.program_id(0)
    nv = nv_ref[i]
    prev = jnp.maximum(i - 1, 0)

    @pl.when((i == 0) | (ea_ref[i] != ea_ref[prev]) | (eb_ref[i] != eb_ref[prev]))
    def _():
        for w32_ref, w_ref in zip(w32_refs, w_refs):
            w_ref[...] = w32_ref[0].astype(BF16)

    @pl.when(nv > 0)
    def _():
        row = lax.broadcasted_iota(jnp.int32, (TMX, 1), 0)
        xb = jnp.where(row < nv, _load_token_major(xs_ref, TMX), 0.0).astype(BF16)
        logits = jnp.dot(xb, rw_ref[...], preferred_element_type=F32) + rb_ref[...]
        lane = lax.broadcasted_iota(jnp.int32, logits.shape, 1)
        ea = ea_ref[i]
        eb = eb_ref[i]
        gmask = lane < N_GROUPS
        gl = jnp.where(gmask, logits, -jnp.inf)
        gmax = jnp.max(gl, axis=-1, keepdims=True)
        gexp = jnp.where(gmask, jnp.exp(gl - gmax), 0.0)
        g_own = jnp.sum(jnp.where(lane == ea // EXPERTS_PER_GROUP, gexp, 0.0), axis=-1, keepdims=True)
        g_w = g_own / jnp.sum(gexp, axis=-1, keepdims=True)
        la = jnp.sum(jnp.where(lane == ea + E_LANE0, logits, 0.0), axis=-1, keepdims=True)
        lb = jnp.sum(jnp.where(lane == eb + E_LANE0, logits, 0.0), axis=-1, keepdims=True)
        m = jnp.maximum(la, lb)
        pa = jnp.exp(la - m)
        pb = jnp.exp(lb - m)
        inv = g_w / (pa + pb)

        def hidden(x, wg_ref, wu_ref, w):
            g = jnp.dot(x, wg_ref[...], preferred_element_type=F32)
            u = jnp.dot(x, wu_ref[...], preferred_element_type=F32)
            return ((g * _sigmoid(g)) * u * w).astype(BF16)

        part = TMX // EXPERT_ROW_PARTS
        ys = []
        for h in range(EXPERT_ROW_PARTS):
            rows = slice(h * part, (h + 1) * part)
            act_a = hidden(xb[rows], wga_ref, wua_ref, (pa * inv)[rows])
            act_b = hidden(xb[rows], wgb_ref, wub_ref, (pb * inv)[rows])
            y = (jnp.dot(act_a, wda_ref[...], preferred_element_type=F32)
                 + jnp.dot(act_b, wdb_ref[...], preferred_element_type=F32))
            ys.append(y)
        _store_token_major(ys_ref, jnp.concatenate(ys, axis=0))


def _experts(sched, xs, router_w, router_b, wg, wu, wd):
    ea, eb, nv, blk = sched
    n_tiles = ea.shape[0]
    rows = lambda i, ea, eb, nv, blk: (blk[i], 0)
    const2 = lambda i, ea, eb, nv, blk: (0, 0)
    exp_a = lambda i, ea, eb, nv, blk: (ea[i], 0, 0)
    exp_b = lambda i, ea, eb, nv, blk: (eb[i], 0, 0)
    w_in_spec = lambda m: pl.BlockSpec((1, D_MODEL, D_EXPERT), m)
    w_out_spec = lambda m: pl.BlockSpec((1, D_EXPERT, D_MODEL), m)
    return pl.pallas_call(
        _experts_kernel,
        out_shape=jax.ShapeDtypeStruct(xs.shape, F32),
        grid_spec=pltpu.PrefetchScalarGridSpec(
            num_scalar_prefetch=4,
            grid=(n_tiles,),
            in_specs=[pl.BlockSpec((TMX * ROW_TILES, LANES), rows),
                      pl.BlockSpec((D_MODEL, ROUTER_LANES), const2),
                      pl.BlockSpec((1, ROUTER_LANES), const2),
                      w_in_spec(exp_a), w_in_spec(exp_a), w_out_spec(exp_a),
                      w_in_spec(exp_b), w_in_spec(exp_b), w_out_spec(exp_b)],
            out_specs=pl.BlockSpec((TMX * ROW_TILES, LANES), rows),
            scratch_shapes=[pltpu.VMEM((D_MODEL, D_EXPERT), BF16), pltpu.VMEM((D_MODEL, D_EXPERT), BF16),
                            pltpu.VMEM((D_EXPERT, D_MODEL), BF16)] * 2),
        compiler_params=_params(("arbitrary",)),
        name="experts",
    )(ea, eb, nv, blk, xs, router_w, router_b, wg, wu, wd, wg, wu, wd)


def _combine_kernel(dest_ref, ys_ref, x1_ref, mod_ref, gpost_ref, o_ref, ybuf, sems, *, tile0):
    i = pl.program_id(0)
    n = pl.num_programs(0)

    def fetch(tile, slot):
        base = (tile + tile0) * TC

        def start(g, carry):
            for u in range(DMA_UNROLL):
                r = g * DMA_UNROLL + u
                pltpu.make_async_copy(_token_rows(ys_ref, dest_ref[base + r]), _token_rows(ybuf.at[slot], r),
                                      sems.at[slot]).start(priority=u % 2)
            return carry

        lax.fori_loop(0, TC // DMA_UNROLL, start, 0)

    @pl.when(i == 0)
    def _():
        fetch(0, 0)

    @pl.when(i + 1 < n)
    def _():
        fetch(i + 1, (i + 1) % 2)

    slot = i % 2
    pltpu.make_async_copy(ys_ref.at[pl.ds(0, TC * ROW_TILES), :], ybuf.at[slot], sems.at[slot]).wait()
    gate2 = mod_ref[0, :, 5 * D_MODEL:6 * D_MODEL]
    o_ref[...] = x1_ref[...] + _rms(_load_token_major(ybuf.at[slot], TC)) * (gate2 * gpost_ref[...])


def _combine(dest, ys, x1, mod3, g_post, cond_of_tile, tile0):
    n_tok = x1.shape[0]
    return pl.pallas_call(
        functools.partial(_combine_kernel, tile0=tile0),
        out_shape=jax.ShapeDtypeStruct((n_tok, D_MODEL), F32),
        grid_spec=pltpu.PrefetchScalarGridSpec(
            num_scalar_prefetch=1,
            grid=(n_tok // TC,),
            in_specs=[pl.BlockSpec(memory_space=pl.ANY),
                      pl.BlockSpec((TC, D_MODEL), lambda i, d: (i, 0)),
                      pl.BlockSpec((1, 1, 6 * D_MODEL), lambda i, d: (cond_of_tile(i), 0, 0)),
                      pl.BlockSpec((1, D_MODEL), lambda i, d: (0, 0))],
            out_specs=pl.BlockSpec((TC, D_MODEL), lambda i, d: (i, 0)),
            scratch_shapes=[pltpu.VMEM((2, TC * ROW_TILES, LANES), F32), pltpu.SemaphoreType.DMA((2,))]),
        compiler_params=_params(("arbitrary",)),
        name="combine",
    )(dest, ys, x1, mod3, g_post)


def _schedule(bucket, rank, counts):
    n_tok = bucket.shape[0]
    n_max = n_tok // TMX + N_BUCKETS
    cnt = counts[:N_BUCKETS, 0].astype(jnp.int32)
    tiles = (cnt + TMX - 1) // TMX
    tile_end = jnp.cumsum(tiles)
    tile_start = tile_end - tiles
    ids = jnp.arange(N_BUCKETS, dtype=jnp.int32)
    slot0 = jnp.sum(jnp.where(bucket[:, None] == ids[None, :], (tile_start * TMX)[None, :], 0), axis=1)
    dest = slot0 + rank
    i = jnp.arange(n_max, dtype=jnp.int32)
    total = tile_end[-1]
    valid = i < total
    tb = jnp.sum((jnp.minimum(i, total - 1)[:, None] >= tile_end[None, :]).astype(jnp.int32), axis=1)
    pairs = [(a, b) for a in range(EXPERTS_PER_GROUP) for b in range(a + 1, EXPERTS_PER_GROUP)]
    ea_tab = jnp.array([g * EXPERTS_PER_GROUP + a for g in range(N_GROUPS) for a, _ in pairs], jnp.int32)
    eb_tab = jnp.array([g * EXPERTS_PER_GROUP + b for g in range(N_GROUPS) for _, b in pairs], jnp.int32)
    hit = tb[:, None] == ids[None, :]
    look = lambda tab: jnp.sum(jnp.where(hit, tab[None, :], 0), axis=1)
    ea, eb = look(ea_tab), look(eb_tab)
    nv = jnp.where(valid, jnp.clip(look(cnt) - (i - look(tile_start)) * TMX, 0, TMX), 0)
    blk = jnp.minimum(i, total - 1)
    return dest, (ea, eb, nv, blk), n_max * TMX


def _block_diag_gates(rg_wa, rg_wx):
    heads = LANES // HEAD_RNN

    def bd(w):
        w = w.reshape(2, RNN_BLOCKS, heads, HEAD_RNN, HEAD_RNN)
        eye = jnp.eye(heads, dtype=w.dtype)
        full = jnp.einsum('dghij,hk->dghikj', w, eye)
        return full.reshape(2, RNN_BLOCKS, LANES, LANES)

    return jnp.concatenate([bd(rg_wa), bd(rg_wx)], axis=-1).astype(BF16)


def _row_tile(v):
    blocks = v.reshape(v.shape[:-1] + (RNN_BLOCKS, LANES))
    return jnp.concatenate([blocks] * PAIR, axis=-2)


def _to_time_major_state(h):
    return h.reshape(h.shape[0] // PAIR, TMJ_ROWS, LANES)


def kernel(x_prompt, x_sample, state_rglru, c, c_ctx, w_mod, b_mod, g_pre_mix, g_post_mix, g_pre_ffn,
           g_post_ffn, w_in, conv_w, conv_b, rg_wa, rg_ba, rg_wx, rg_bx, rg_lambda, sgu_g, sgu_w, sgu_b,
           w_out, router_g_w, router_g_b, router_e_w, router_e_b, exp_w_gate, exp_w_up, exp_w_down):
    assert w_mod.shape[0] == 1, "single-layer trunk"
    n_ctx, ctx_len, _ = x_prompt.shape
    n_dec, dec_len, _ = x_sample.shape
    l = 0

    n_cond = SUBLANES
    assert n_dec % PAIR == 0 and n_dec + PAIR <= n_cond
    cond = jnp.concatenate([c, jnp.broadcast_to(c_ctx, (PAIR, D_MODEL)),
                            jnp.zeros((n_cond - n_dec - PAIR, D_MODEL), F32)], axis=0)
    mod3 = _modulation(cond, w_mod[l], b_mod[l]).reshape(n_cond, 1, 6 * D_MODEL)
    pos_tab = _pos_table()

    w_rx, w_gate, w_u, w_v = jnp.split(w_in[l], [D_RNN, 2 * D_RNN, 2 * D_RNN + D_SGU], axis=1)
    w_in_b = jnp.concatenate([w_v, w_u, w_gate, w_rx], axis=1).astype(BF16)
    w_out_b = w_out[l].astype(BF16)
    sgu_w_b = sgu_w[l].reshape(2, 4 * CHUNK, CHUNK).astype(BF16)
    sgu_bias_tile = jnp.repeat(sgu_b[l].T, HEAD_SGU, axis=1)
    w_gates = _block_diag_gates(rg_wa[l], rg_wx[l])
    b_gates = 0.5 * jnp.stack([_row_tile(rg_ba[l]), _row_tile(rg_bx[l])], axis=1)
    lam = _row_tile(rg_lambda[l])
    conv_w_t = 0.5 * _row_tile(conv_w[l])
    conv_b_t = 0.5 * _row_tile(conv_b[l])
    lane_pad = ROUTER_LANES - E_LANE0 - N_EXPERTS
    router_w = jnp.pad(jnp.concatenate([router_g_w[l], router_e_w[l]], axis=1), ((0, 0), (0, lane_pad))).astype(BF16)
    router_b = jnp.pad(jnp.concatenate([router_g_b[l], router_e_b[l]]), (0, lane_pad)).reshape(1, ROUTER_LANES)
    gap, tail = E_ROW0 - N_GROUPS, ROUTER_ROWS - E_ROW0 - N_EXPERTS
    router_wt = jnp.concatenate([router_g_w[l].T, jnp.zeros((gap, D_MODEL), F32), router_e_w[l].T,
                                 jnp.zeros((tail, D_MODEL), F32)], axis=0).astype(BF16)
    router_bt = jnp.concatenate([router_g_b[l], jnp.zeros((gap,), F32), router_e_b[l], jnp.zeros((tail,), F32)])
    router_bt = jnp.broadcast_to(router_bt[:, None], (ROUTER_ROWS, LANES))
    earlier = jnp.triu(jnp.ones((TP, TP), BF16), k=1)
    row = lambda v: v.reshape(1, -1)

    n_ctx_tok = n_ctx * ctx_len

    def mixer(x, h0, cond_of_tile, cond_block, use_pos, counts0):
        n_seq, seq_len, _ = x.shape
        xf = x.reshape(n_seq * seq_len, D_MODEL)
        tab = pos_tab if use_pos else None
        xr, gg, y_sgu = _premix(x, mod3, cond_block, row(g_pre_mix[l]), w_in_b, row(sgu_g[l]),
                                sgu_w_b, sgu_bias_tile, tab)
        scan_params = (conv_w_t, conv_b_t, w_gates, b_gates, lam)
        hf, hf_last = _scan(xr, None, None, _to_time_major_state(h0[:, 0]), *scan_params, direction=0)
        y_rnn, hb_first = _scan(xr, gg, hf, _to_time_major_state(h0[:, 1]), *scan_params, direction=1)
        fstate = jnp.stack([hf_last.reshape(n_seq, D_RNN), hb_first.reshape(n_seq, D_RNN)], axis=1)
        x1, hn, bucket, rank, counts = _postmix(
            xf, y_rnn.reshape(n_seq * seq_len, D_RNN), y_sgu.reshape(n_seq * seq_len, D_SGU), mod3,
            cond_of_tile, row(g_post_mix[l]), row(g_pre_ffn[l]), w_out_b, router_wt, router_bt, earlier, tab,
            counts0)
        return x1, hn, bucket, rank, counts, fstate

    ctx_cond = lambda i: n_dec
    dec_cond = lambda i: i // (dec_len // TP)
    h0_ctx = jnp.zeros((n_ctx, 2, D_RNN), F32)
    counts0 = jnp.zeros((ROUTER_ROWS, LANES), F32)
    x1_ctx, hn_ctx, bucket_ctx, rank_ctx, counts, st = mixer(x_prompt, h0_ctx, ctx_cond, lambda p: n_dec // PAIR,
                                                             False, counts0)
    new_state = st.astype(state_rglru.dtype)[:, None]
    x1_dec, hn_dec, bucket_dec, rank_dec, counts, _ = mixer(x_sample, state_rglru[:, l].astype(F32), dec_cond,
                                                            lambda p: p, True, counts)

    dest, sched, n_slots = _schedule(jnp.concatenate([bucket_ctx, bucket_dec]),
                                     jnp.concatenate([rank_ctx, rank_dec]), counts)
    xs = _dispatch(dest, hn_ctx, hn_dec, n_slots)
    ys = _experts(sched, xs, router_w, router_b, exp_w_gate[l], exp_w_up[l], exp_w_down[l])
    y_prompt = _combine(dest, ys, x1_ctx, mod3, row(g_post_ffn[l]), ctx_cond, 0)
    y_sample = _combine(dest, ys, x1_dec, mod3, row(g_post_ffn[l]), lambda i: i // (dec_len // TC), n_ctx_tok // TC)
    return (y_prompt.reshape(x_prompt.shape), y_sample.reshape(x_sample.shape), new_state)
```

```python
import functools
import math

import jax
import jax.numpy as jnp
from jax import lax
from jax.experimental import pallas as pl
from jax.experimental.pallas import tpu as pltpu

D_MODEL = 1024
D_RNN = 512
D_SGU = 512
N_HEADS_RNN = 8
HEAD_RNN = D_RNN // N_HEADS_RNN
N_HEADS_SGU = 8
HEAD_SGU = D_SGU // N_HEADS_SGU
CHUNK = 128
GRID_W = 64
RG_C = 8.0
N_GROUPS = 4
EXPERTS_PER_GROUP = 4
N_EXPERTS = N_GROUPS * EXPERTS_PER_GROUP
D_EXPERT = 512
EPS = 1e-6
POS_BASE = 10000.0

LANES = 128
SUBLANES = 8
CONV_W = 4
CONV_LEFT = 2
PAIR = 2
RNN_BLOCKS = D_RNN // LANES
TMJ_ROWS = PAIR * RNN_BLOCKS
ROUTER_LANES = LANES
E_LANE0 = N_GROUPS
ROUTER_ROWS = 32
E_ROW0 = SUBLANES

PAIRS_PER_GROUP = EXPERTS_PER_GROUP * (EXPERTS_PER_GROUP - 1) // 2
N_BUCKETS = N_GROUPS * PAIRS_PER_GROUP

ROW_TILES = D_MODEL // LANES

MOD_COLS = 1024
TP = 1024
TD = 2048
TC = 512
DMA_UNROLL = 8
EXPERT_ROW_PARTS = 2
TMX = 512
TT = 512
LC = 256
TS = 16
PB = 2
VMEM_LIMIT = 56 * 1024 * 1024

F32 = jnp.float32
BF16 = jnp.bfloat16


def _params(sem):
    return pltpu.CompilerParams(dimension_semantics=sem, vmem_limit_bytes=VMEM_LIMIT)


def _rms(x):
    return x * lax.rsqrt(jnp.mean(x * x, axis=-1, keepdims=True) + EPS)


def _sigmoid(x):
    return 0.5 * jnp.tanh(0.5 * x) + 0.5


def _mod_kernel(cond_ref, w_ref, b_ref, o_ref):
    c = cond_ref[...]
    s = c * _sigmoid(c)
    o_ref[...] = jnp.dot(s.astype(BF16), w_ref[...].astype(BF16),
                         preferred_element_type=F32) + b_ref[...]


def _modulation(cond, w_mod, b_mod):
    n = w_mod.shape[1]
    return pl.pallas_call(
        _mod_kernel,
        out_shape=jax.ShapeDtypeStruct((cond.shape[0], n), F32),
        grid=(n // MOD_COLS,),
        in_specs=[pl.BlockSpec(cond.shape, lambda j: (0, 0)),
                  pl.BlockSpec((D_MODEL, MOD_COLS), lambda j: (0, j)),
                  pl.BlockSpec((1, MOD_COLS), lambda j: (0, j))],
        out_specs=pl.BlockSpec((cond.shape[0], MOD_COLS), lambda j: (0, j)),
        compiler_params=_params(("arbitrary",)),
        name="modulation",
    )(cond, w_mod, b_mod.reshape(1, n))


def _pos_kernel(o_ref):
    n_freq = D_MODEL // 4
    k = lax.broadcasted_iota(jnp.int32, (GRID_W, n_freq), 1).astype(F32)
    p = lax.broadcasted_iota(jnp.int32, (GRID_W, n_freq), 0).astype(F32)
    freq = jnp.exp(-math.log(POS_BASE) * k / n_freq)
    ang = p * freq
    o_ref[:, 0:n_freq] = jnp.sin(ang)
    o_ref[:, n_freq:2 * n_freq] = jnp.cos(ang)


def _pos_table():
    return pl.pallas_call(
        _pos_kernel,
        out_shape=jax.ShapeDtypeStruct((GRID_W, D_MODEL // 2), F32),
        name="pos_table",
    )()


def _add_pos(x, pos_refs, q0):
    if pos_refs is None:
        return x
    rows_ref, cols_ref = pos_refs
    reps = x.shape[0] // GRID_W
    rpart = jnp.concatenate(
        [jnp.broadcast_to(rows_ref[q:q + 1, :], (GRID_W, D_MODEL // 2)) for q in range(q0, q0 + reps)], axis=0)
    cpart = jnp.concatenate([cols_ref[...]] * reps, axis=0)
    return jnp.concatenate([x[:, :D_MODEL // 2] + rpart, x[:, D_MODEL // 2:] + cpart], axis=1)


def _load_x(x_ref, pos_refs, r0, n):
    return _add_pos(x_ref[r0:r0 + n, :], pos_refs, r0 // GRID_W)


def _premix_kernel(*refs, add_pos):
    refs = list(refs)
    x_ref = refs.pop(0)
    tt = x_ref.shape[1]
    pos_refs = (refs.pop(0), refs.pop(0)) if add_pos else None
    mod_ref, g_ref, win_ref, sgug_ref, sguw_ref, sgub_ref, xr_ref, gg_ref, ys_ref = refs
    hn = []
    for s in range(PAIR):
        shift = mod_ref[s, :, 0:D_MODEL]
        scale = mod_ref[s, :, D_MODEL:2 * D_MODEL]
        hn.append(_rms(_add_pos(x_ref[s], pos_refs, 0)) * (g_ref[...] * (1.0 + scale)) + shift)
    z = jnp.dot(jnp.concatenate(hn, axis=0).astype(BF16), win_ref[...],
                preferred_element_type=F32)
    half = D_SGU // 2
    heads_per_half = N_HEADS_SGU // 2
    lane_head = lax.broadcasted_iota(jnp.int32, (CHUNK, half), 1) // HEAD_SGU
    for s in range(PAIR):
        zs = z[s * tt:(s + 1) * tt]
        vn = (_rms(zs[:, 0:D_SGU]) * sgug_ref[...]).astype(BF16)
        u = zs[:, D_SGU:2 * D_SGU]
        for c in range(tt // CHUNK):
            rows = slice(c * CHUNK, (c + 1) * CHUNK)
            halves = []
            for hf in range(2):
                r = jnp.dot(sguw_ref[hf], vn[rows, hf * half:(hf + 1) * half],
                            preferred_element_type=F32)
                sel = jnp.zeros((CHUNK, half), F32)
                for h in range(heads_per_half):
                    sel = jnp.where(lane_head == h, r[h * CHUNK:(h + 1) * CHUNK], sel)
                halves.append(sel)
            gatev = jnp.concatenate(halves, axis=1) + sgub_ref[...]
            ys_ref[s, rows, :] = (u[rows] * gatev).astype(BF16)
    for s in range(PAIR):
        zs = z[s * tt:(s + 1) * tt]
        gg = jax.nn.gelu(zs[:, 2 * D_SGU:2 * D_SGU + D_RNN])
        xr = zs[:, 2 * D_SGU + D_RNN:]
        for k in range(RNN_BLOCKS):
            rows = pl.ds(s * RNN_BLOCKS + k, tt, stride=TMJ_ROWS)
            xr_ref[rows, :] = xr[:, k * LANES:(k + 1) * LANES]
            gg_ref[rows, :] = gg[:, k * LANES:(k + 1) * LANES]


def _premix(x, mod3, cond_block, g_pre, w_in_b, sgu_g, sgu_w_b, sgu_bias_tile, pos_tab):
    n_seq, seq_len, _ = x.shape
    tt = min(seq_len, TT)
    n_pairs, n_tiles = n_seq // PAIR, seq_len // tt
    add_pos = pos_tab is not None
    const2 = lambda p, j: (0, 0)
    in_specs = [pl.BlockSpec((PAIR, tt, D_MODEL), lambda p, j: (p, j, 0))]
    args = [x]
    if add_pos:
        reps = tt // GRID_W
        in_specs += [pl.BlockSpec((None, reps, D_MODEL // 2), lambda p, j: (j, 0, 0)),
                     pl.BlockSpec((GRID_W, D_MODEL // 2), const2)]
        args += [pos_tab.reshape(GRID_W // reps, reps, D_MODEL // 2), pos_tab]
    in_specs += [pl.BlockSpec((PAIR, 1, 6 * D_MODEL), lambda p, j: (cond_block(p), 0, 0)),
                 pl.BlockSpec((1, D_MODEL), const2),
                 pl.BlockSpec((D_MODEL, 2 * D_RNN + 2 * D_SGU), const2),
                 pl.BlockSpec((1, D_SGU), const2),
                 pl.BlockSpec((2, 4 * CHUNK, CHUNK), lambda p, j: (0, 0, 0)),
                 pl.BlockSpec((CHUNK, D_SGU), const2)]
    args += [mod3, g_pre, w_in_b, sgu_g, sgu_w_b, sgu_bias_tile]
    tmj = jax.ShapeDtypeStruct((n_pairs * seq_len * TMJ_ROWS, LANES), F32)
    tmj_spec = pl.BlockSpec((tt * TMJ_ROWS, LANES), lambda p, j: (p * n_tiles + j, 0))
    xr, gg, y_sgu = pl.pallas_call(
        functools.partial(_premix_kernel, add_pos=add_pos),
        out_shape=(tmj, tmj, jax.ShapeDtypeStruct((n_seq, seq_len, D_SGU), BF16)),
        grid=(n_pairs, n_tiles),
        in_specs=in_specs,
        out_specs=(tmj_spec, tmj_spec, pl.BlockSpec((PAIR, tt, D_SGU), lambda p, j: (p, j, 0))),
        compiler_params=_params(("parallel", "parallel")),
        name="premix",
    )(*args)
    shape4 = (n_pairs, seq_len, TMJ_ROWS, LANES)
    return xr.reshape(shape4), gg.reshape(shape4), y_sgu


def _scan_kernel(*refs, reverse, n_chunks):
    if reverse:
        (xprev_ref, x_ref, xnext_ref, gg_ref, hf_ref, h0_ref, cw_ref, cb_ref, wg_ref, bg_ref, lam_ref,
         y_ref, fs_ref, xwin, xc_s, r_s, i_s, a_s, b_s, y_s, hcar) = refs
    else:
        (xprev_ref, x_ref, xnext_ref, h0_ref, cw_ref, cb_ref, wg_ref, bg_ref, lam_ref,
         hf_ref, fs_ref, xwin, xc_s, r_s, i_s, a_s, b_s, hcar) = refs
    c = pl.program_id(1)
    chunk = n_chunks - 1 - c if reverse else c
    sub_rows = TS * TMJ_ROWS

    def rows_of(pb, t0, n_steps):
        first = (pb * LC + t0) * TMJ_ROWS
        if not isinstance(first, int):
            first = pl.multiple_of(first, TMJ_ROWS)
        return pl.ds(first, n_steps * TMJ_ROWS)

    @pl.when(c == 0)
    def _():
        hcar[...] = h0_ref[...]

    xwin[:, 0:CONV_LEFT] = jnp.where(chunk > 0, xprev_ref[...], 0.0)
    xwin[:, LC + CONV_LEFT:LC + CONV_W - 1] = jnp.where(chunk < n_chunks - 1, xnext_ref[...], 0.0)

    xwin[:, CONV_LEFT:CONV_LEFT + LC] = x_ref[...]

    neg_lam = -lam_ref[...]
    softplus = jnp.maximum(neg_lam, 0.0) + jnp.log(1.0 + jnp.exp(-jnp.abs(neg_lam)))
    half_decay = (-0.5 * RG_C * math.log2(math.e)) * softplus

    def conv(pb):
        for t0 in range(0, LC, TS):
            xc = cb_ref[...] + cw_ref[0] * xwin[pb, t0:t0 + TS]
            for k in range(1, CONV_W):
                xc = xc + cw_ref[k] * xwin[pb, t0 + k:t0 + k + TS]
            xc_s[rows_of(pb, t0, TS), :] = xc.reshape(sub_rows, LANES)

    def gate_matmuls(pb):
        for k in range(RNN_BLOCKS):
            rows = pl.ds(pb * LC * TMJ_ROWS + k, LC * PAIR, stride=RNN_BLOCKS)
            g = jnp.dot(xc_s[rows, :].astype(BF16), wg_ref[k], preferred_element_type=F32)
            r_s[rows, :] = g[:, :LANES]
            i_s[rows, :] = g[:, LANES:]

    def gates(pb):
        for t0 in range(0, LC, TS):
            rows = rows_of(pb, t0, TS)
            tile = lambda ref: ref[rows, :].reshape(TS, TMJ_ROWS, LANES)
            tr = jnp.tanh(tile(r_s) + bg_ref[0])
            ti = jnp.tanh(tile(i_s) + bg_ref[1])
            log2_a = tr * half_decay + half_decay
            a = jnp.exp2(log2_a)
            q = jnp.tanh(log2_a * (-math.log(2.0))) * (a * a + 1.0)
            b = jnp.where(q > 0.0, q * lax.rsqrt(q), 0.0) * ((ti + 1.0) * tile(xc_s))
            a_s[rows, :] = a.reshape(sub_rows, LANES)
            b_s[rows, :] = b.reshape(sub_rows, LANES)

    for stage in (conv, gate_matmuls, gates):
        for pb in range(PB):
            stage(pb)

    def step(j, hs):
        t = LC - 1 - j if reverse else j
        out = []
        for pb in range(PB):
            rows = rows_of(pb, t, 1)
            h = a_s[rows, :] * hs[pb] + b_s[rows, :]
            if reverse:
                y_s[rows, :] = h
            else:
                hf_ref[pb, t] = h
            out.append(h)
        return tuple(out)

    hs = lax.fori_loop(0, LC, step, tuple(hcar[pb] for pb in range(PB)), unroll=8)
    for pb in range(PB):
        hcar[pb] = hs[pb]
        fs_ref[pb] = hs[pb]

    if reverse:
        for pb in range(PB):
            for t0 in range(0, LC, TS):
                rows = rows_of(pb, t0, TS)
                both = hf_ref[pb, t0:t0 + TS].reshape(sub_rows, LANES) + y_s[rows, :]
                y_s[rows, :] = both * gg_ref[pb, t0:t0 + TS].reshape(sub_rows, LANES)
            for s in range(PAIR):
                cols = [y_s[pl.ds(pb * LC * TMJ_ROWS + s * RNN_BLOCKS + k, LC, stride=TMJ_ROWS), :]
                        for k in range(RNN_BLOCKS)]
                y_ref[pb * PAIR + s] = jnp.concatenate(cols, axis=1).astype(BF16)


def _scan(xr, gg, hf, h0, conv_w, conv_b, w_gates, b_gates, lam, direction):
    n_pairs, seq_len = xr.shape[:2]
    n_chunks = seq_len // LC
    reverse = direction == 1
    pos = (lambda c: n_chunks - 1 - c) if reverse else (lambda c: c)
    tmj_blk = pl.BlockSpec((PB, LC, TMJ_ROWS, LANES), lambda i, c: (i, pos(c), 0, 0))
    state_blk = pl.BlockSpec((PB, TMJ_ROWS, LANES), lambda i, c: (i, 0, 0))
    per_dir = lambda *shape: pl.BlockSpec((None,) + shape, lambda i, c: (direction,) + (0,) * len(shape))
    in_specs = [
        pl.BlockSpec((PB, CONV_LEFT, TMJ_ROWS, LANES),
                     lambda i, c: (i, jnp.maximum(pos(c) * (LC // CONV_LEFT) - 1, 0), 0, 0)),
        tmj_blk,
        pl.BlockSpec((PB, 1, TMJ_ROWS, LANES), lambda i, c: (i, jnp.minimum((pos(c) + 1) * LC, seq_len - 1), 0, 0)),
    ]
    args = [xr, xr, xr]
    if reverse:
        in_specs += [tmj_blk, tmj_blk]
        args += [gg, hf]
    in_specs += [state_blk,
                 pl.BlockSpec((CONV_W, TMJ_ROWS, LANES), lambda i, c: (0, 0, 0)),
                 pl.BlockSpec((TMJ_ROWS, LANES), lambda i, c: (0, 0)),
                 per_dir(RNN_BLOCKS, LANES, 2 * LANES),
                 per_dir(2, TMJ_ROWS, LANES),
                 per_dir(TMJ_ROWS, LANES)]
    args += [h0, conv_w, conv_b, w_gates, b_gates, lam]
    flat = pltpu.VMEM((PB * LC * TMJ_ROWS, LANES), F32)
    scratch = [pltpu.VMEM((PB, LC + CONV_W - 1, TMJ_ROWS, LANES), F32)] + [flat] * (6 if reverse else 5)
    scratch += [pltpu.VMEM((PB, TMJ_ROWS, LANES), F32)]
    state = jax.ShapeDtypeStruct((n_pairs, TMJ_ROWS, LANES), F32)
    if reverse:
        out_shape = (jax.ShapeDtypeStruct((n_pairs * PAIR, seq_len, D_RNN), BF16), state)
        out_specs = (pl.BlockSpec((PB * PAIR, LC, D_RNN), lambda i, c: (i, pos(c), 0)), state_blk)
    else:
        out_shape = (jax.ShapeDtypeStruct(xr.shape, F32), state)
        out_specs = (tmj_blk, state_blk)
    return pl.pallas_call(
        functools.partial(_scan_kernel, reverse=reverse, n_chunks=n_chunks),
        out_shape=out_shape,
        grid=(n_pairs // PB, n_chunks),
        in_specs=in_specs,
        out_specs=out_specs,
        scratch_shapes=scratch,
        compiler_params=_params(("parallel", "arbitrary")),
        name="scan_bwd" if reverse else "scan_fwd",
    )(*args)


def _route(lt):
    n = lt.shape[1]
    row = lax.broadcasted_iota(jnp.int32, (EXPERTS_PER_GROUP, n), 0)
    neg = jnp.float32(-jnp.inf)

    def arg_max(v):
        m = jnp.max(v, axis=0, keepdims=True)
        return jnp.min(jnp.where(v == m, row, EXPERTS_PER_GROUP), axis=0, keepdims=True)

    g_idx = arg_max(lt[0:N_GROUPS])
    el = lt[E_ROW0:E_ROW0 + EXPERTS_PER_GROUP]
    for g in range(1, N_GROUPS):
        first = E_ROW0 + g * EXPERTS_PER_GROUP
        el = jnp.where(g_idx == g, lt[first:first + EXPERTS_PER_GROUP], el)
    i1 = arg_max(el)
    i2 = arg_max(jnp.where(row == i1, neg, el))
    ja = jnp.minimum(i1, i2)
    jb = jnp.maximum(i1, i2)
    pair = (ja * (2 * EXPERTS_PER_GROUP - 1 - ja)) // 2 + (jb - ja - 1)
    return g_idx * PAIRS_PER_GROUP + pair


def _store_token_major(ref, x, t0=0):
    n = x.shape[0]
    for k in range(ROW_TILES):
        ref[pl.ds(t0 * ROW_TILES + k, n, stride=ROW_TILES), :] = x[:, k * LANES:(k + 1) * LANES]


def _load_token_major(ref, n):
    return jnp.concatenate([ref[pl.ds(k, n, stride=ROW_TILES), :] for k in range(ROW_TILES)], axis=1)


def _postmix_kernel(*refs, add_pos):
    refs = list(refs)
    x_ref = refs.pop(0)
    pos_refs = (refs.pop(0), refs.pop(0)) if add_pos else None
    (yr_ref, ys_ref, mod_ref, gpost_ref, gpre_ref, wout_ref, rw_ref, rb_ref, earlier_ref, cnt0_ref,
     x1_ref, hn_ref, rt_ref, cnt_ref, run_ref) = refs

    @pl.when(pl.program_id(0) == 0)
    def _():
        run_ref[...] = cnt0_ref[...]

    gate1 = mod_ref[0, :, 2 * D_MODEL:3 * D_MODEL]
    shift2 = mod_ref[0, :, 3 * D_MODEL:4 * D_MODEL]
    scale2 = mod_ref[0, :, 4 * D_MODEL:5 * D_MODEL]
    y = (jnp.dot(yr_ref[...], wout_ref[0:D_RNN, :], preferred_element_type=F32)
         + jnp.dot(ys_ref[...], wout_ref[D_RNN:, :], preferred_element_type=F32))
    x1 = _load_x(x_ref, pos_refs, 0, TP) + _rms(y) * (gate1 * gpost_ref[...])
    x1_ref[...] = x1
    hn = _rms(x1) * (gpre_ref[...] * (1.0 + scale2)) + shift2
    _store_token_major(hn_ref, hn)
    lt = lax.dot_general(rw_ref[...], hn.astype(BF16), (((1,), (1,)), ((), ())),
                         preferred_element_type=F32) + rb_ref[:, 0:1]
    bucket = _route(lt)
    onehot = lax.broadcasted_iota(jnp.int32, (ROUTER_ROWS, TP), 0) == bucket
    before = jnp.dot(onehot.astype(BF16), earlier_ref[...], preferred_element_type=F32) + run_ref[:, 0:1]
    rank = jnp.sum(jnp.where(onehot, before, 0.0), axis=0, keepdims=True).astype(jnp.int32)
    row = lax.broadcasted_iota(jnp.int32, (SUBLANES, TP), 0)
    rt_ref[...] = jnp.where(row == 0, bucket, jnp.where(row == 1, rank, 0))
    run_ref[...] += jnp.sum(onehot.astype(F32), axis=1, keepdims=True)
    cnt_ref[...] = run_ref[...]


def _postmix(x, y_rnn, y_sgu, mod3, cond_of_tile, g_post, g_pre, w_out_b, router_wt, router_bt, earlier, pos_tab,
             counts0):
    n_tok = x.shape[0]
    n_tiles = n_tok // TP
    add_pos = pos_tab is not None
    tok = lambda i: (i, 0)
    const2 = lambda i: (0, 0)
    in_specs = [pl.BlockSpec((TP, D_MODEL), tok)]
    args = [x]
    if add_pos:
        reps = TP // GRID_W
        tiles_per_seq = GRID_W // reps
        in_specs += [pl.BlockSpec((None, reps, D_MODEL // 2), lambda i: (i % tiles_per_seq, 0, 0)),
                     pl.BlockSpec((GRID_W, D_MODEL // 2), const2)]
        args += [pos_tab.reshape(tiles_per_seq, reps, D_MODEL // 2), pos_tab]
    in_specs += [pl.BlockSpec((TP, D_RNN), tok),
                 pl.BlockSpec((TP, D_SGU), tok),
                 pl.BlockSpec((1, 1, 6 * D_MODEL), lambda i: (cond_of_tile(i), 0, 0)),
                 pl.BlockSpec((1, D_MODEL), const2),
                 pl.BlockSpec((1, D_MODEL), const2),
                 pl.BlockSpec((D_MODEL, D_MODEL), const2),
                 pl.BlockSpec((ROUTER_ROWS, D_MODEL), const2),
                 pl.BlockSpec((ROUTER_ROWS, LANES), const2),
                 pl.BlockSpec((TP, TP), const2),
                 pl.BlockSpec((ROUTER_ROWS, LANES), const2)]
    args += [y_rnn, y_sgu, mod3, g_post, g_pre, w_out_b, router_wt, router_bt, earlier, counts0]
    counts_spec = pl.BlockSpec((ROUTER_ROWS, LANES), const2)
    x1, hn, route, counts = pl.pallas_call(
        functools.partial(_postmix_kernel, add_pos=add_pos),
        out_shape=(jax.ShapeDtypeStruct((n_tok, D_MODEL), F32),
                   jax.ShapeDtypeStruct((n_tok * ROW_TILES, LANES), F32),
                   jax.ShapeDtypeStruct((n_tiles * SUBLANES, TP), jnp.int32),
                   jax.ShapeDtypeStruct((ROUTER_ROWS, LANES), F32)),
        grid=(n_tiles,),
        in_specs=in_specs,
        out_specs=(pl.BlockSpec((TP, D_MODEL), tok),
                   pl.BlockSpec((TP * ROW_TILES, LANES), tok),
                   pl.BlockSpec((SUBLANES, TP), tok),
                   counts_spec),
        scratch_shapes=[pltpu.VMEM((ROUTER_ROWS, LANES), F32)],
        compiler_params=_params(("arbitrary",)),
        name="postmix",
    )(*args)
    route = route.reshape(n_tiles, SUBLANES, TP)
    return x1, hn, route[:, 0].reshape(n_tok), route[:, 1].reshape(n_tok), counts


def _token_rows(ref, t):
    return ref.at[pl.ds(pl.multiple_of(t * ROW_TILES, ROW_TILES), ROW_TILES), :]


def _dispatch_kernel(dest_ref, hc_ref, hs_ref, xs_ref, sem, *, n_ctx_steps):
    i = pl.program_id(0)
    base = i * TD

    def scatter(src_ref):
        def start(g, carry):
            for u in range(DMA_UNROLL):
                r = g * DMA_UNROLL + u
                pltpu.make_async_copy(_token_rows(src_ref, r), _token_rows(xs_ref, dest_ref[base + r]),
                                      sem).start(priority=u % 2)
            return carry

        lax.fori_loop(0, TD // DMA_UNROLL, start, 0)
        pltpu.make_async_copy(src_ref, xs_ref.at[pl.ds(0, TD * ROW_TILES), :], sem).wait()

    @pl.when(i < n_ctx_steps)
    def _():
        scatter(hc_ref)

    @pl.when(i >= n_ctx_steps)
    def _():
        scatter(hs_ref)


def _dispatch(dest, hn_ctx, hn_dec, n_slots):
    n_ctx_steps = hn_ctx.shape[0] // (TD * ROW_TILES)
    n_dec_steps = hn_dec.shape[0] // (TD * ROW_TILES)
    return pl.pallas_call(
        functools.partial(_dispatch_kernel, n_ctx_steps=n_ctx_steps),
        out_shape=jax.ShapeDtypeStruct((n_slots * ROW_TILES, LANES), F32),
        grid_spec=pltpu.PrefetchScalarGridSpec(
            num_scalar_prefetch=1,
            grid=(n_ctx_steps + n_dec_steps,),
            in_specs=[pl.BlockSpec((TD * ROW_TILES, LANES), lambda i, d: (jnp.minimum(i, n_ctx_steps - 1), 0)),
                      pl.BlockSpec((TD * ROW_TILES, LANES), lambda i, d: (jnp.maximum(i - n_ctx_steps, 0), 0))],
            out_specs=pl.BlockSpec(memory_space=pl.ANY),
            scratch_shapes=[pltpu.SemaphoreType.DMA(())]),
        compiler_params=_params(("arbitrary",)),
        name="dispatch",
    )(dest, hn_ctx, hn_dec)


def _experts_kernel(ea_ref, eb_ref, nv_ref, blk_ref, xs_ref, rw_ref, rb_ref, *refs):
    w32_refs, ys_ref, w_refs = refs[:6], refs[6], refs[7:]
    wga_ref, wua_ref, wda_ref, wgb_ref, wub_ref, wdb_ref = w_refs
    i = pl.program_id(0)
    nv = nv_ref[i]
    prev = jnp.maximum(i - 1, 0)

    @pl.when((i == 0) | (ea_ref[i] != ea_ref[prev]) | (eb_ref[i] != eb_ref[prev]))
    def _():
        for w32_ref, w_ref in zip(w32_refs, w_refs):
            w_ref[...] = w32_ref[0].astype(BF16)

    @pl.when(nv > 0)
    def _():
        row = lax.broadcasted_iota(jnp.int32, (TMX, 1), 0)
        xb = jnp.where(row < nv, _load_token_major(xs_ref, TMX), 0.0).astype(BF16)
        logits = jnp.dot(xb, rw_ref[...], preferred_element_type=F32) + rb_ref[...]
        lane = lax.broadcasted_iota(jnp.int32, logits.shape, 1)
        ea = ea_ref[i]
        eb = eb_ref[i]
        gmask = lane < N_GROUPS
        gl = jnp.where(gmask, logits, -jnp.inf)
        gmax = jnp.max(gl, axis=-1, keepdims=True)
        gexp = jnp.where(gmask, jnp.exp(gl - gmax), 0.0)
        g_own = jnp.sum(jnp.where(lane == ea // EXPERTS_PER_GROUP, gexp, 0.0), axis=-1, keepdims=True)
        g_w = g_own / jnp.sum(gexp, axis=-1, keepdims=True)
        la = jnp.sum(jnp.where(lane == ea + E_LANE0, logits, 0.0), axis=-1, keepdims=True)
        lb = jnp.sum(jnp.where(lane == eb + E_LANE0, logits, 0.0), axis=-1, keepdims=True)
        m = jnp.maximum(la, lb)
        pa = jnp.exp(la - m)
        pb = jnp.exp(lb - m)
        inv = g_w / (pa + pb)

        def hidden(x, wg_ref, wu_ref, w):
            g = jnp.dot(x, wg_ref[...], preferred_element_type=F32)
            u = jnp.dot(x, wu_ref[...], preferred_element_type=F32)
            return ((g * _sigmoid(g)) * u * w).astype(BF16)

        part = TMX // EXPERT_ROW_PARTS
        ys = []
        for h in range(EXPERT_ROW_PARTS):
            rows = slice(h * part, (h + 1) * part)
            act_a = hidden(xb[rows], wga_ref, wua_ref, (pa * inv)[rows])
            act_b = hidden(xb[rows], wgb_ref, wub_ref, (pb * inv)[rows])
            y = (jnp.dot(act_a, wda_ref[...], preferred_element_type=F32)
                 + jnp.dot(act_b, wdb_ref[...], preferred_element_type=F32))
            ys.append(y)
        _store_token_major(ys_ref, jnp.concatenate(ys, axis=0))


def _experts(sched, xs, router_w, router_b, wg, wu, wd):
    ea, eb, nv, blk = sched
    n_tiles = ea.shape[0]
    rows = lambda i, ea, eb, nv, blk: (blk[i], 0)
    const2 = lambda i, ea, eb, nv, blk: (0, 0)
    exp_a = lambda i, ea, eb, nv, blk: (ea[i], 0, 0)
    exp_b = lambda i, ea, eb, nv, blk: (eb[i], 0, 0)
    w_in_spec = lambda m: pl.BlockSpec((1, D_MODEL, D_EXPERT), m)
    w_out_spec = lambda m: pl.BlockSpec((1, D_EXPERT, D_MODEL), m)
    return pl.pallas_call(
        _experts_kernel,
        out_shape=jax.ShapeDtypeStruct(xs.shape, F32),
        grid_spec=pltpu.PrefetchScalarGridSpec(
            num_scalar_prefetch=4,
            grid=(n_tiles,),
            in_specs=[pl.BlockSpec((TMX * ROW_TILES, LANES), rows),
                      pl.BlockSpec((D_MODEL, ROUTER_LANES), const2),
                      pl.BlockSpec((1, ROUTER_LANES), const2),
                      w_in_spec(exp_a), w_in_spec(exp_a), w_out_spec(exp_a),
                      w_in_spec(exp_b), w_in_spec(exp_b), w_out_spec(exp_b)],
            out_specs=pl.BlockSpec((TMX * ROW_TILES, LANES), rows),
            scratch_shapes=[pltpu.VMEM((D_MODEL, D_EXPERT), BF16), pltpu.VMEM((D_MODEL, D_EXPERT), BF16),
                            pltpu.VMEM((D_EXPERT, D_MODEL), BF16)] * 2),
        compiler_params=_params(("arbitrary",)),
        name="experts",
    )(ea, eb, nv, blk, xs, router_w, router_b, wg, wu, wd, wg, wu, wd)


def _combine_kernel(dest_ref, ys_ref, x1_ref, mod_ref, gpost_ref, o_ref, ybuf, sems, *, tile0):
    i = pl.program_id(0)
    n = pl.num_programs(0)

    def fetch(tile, slot):
        base = (tile + tile0) * TC

        def start(g, carry):
            for u in range(DMA_UNROLL):
                r = g * DMA_UNROLL + u
                pltpu.make_async_copy(_token_rows(ys_ref, dest_ref[base + r]), _token_rows(ybuf.at[slot], r),
                                      sems.at[slot]).start(priority=u % 2)
            return carry

        lax.fori_loop(0, TC // DMA_UNROLL, start, 0)

    @pl.when(i == 0)
    def _():
        fetch(0, 0)

    @pl.when(i + 1 < n)
    def _():
        fetch(i + 1, (i + 1) % 2)

    slot = i % 2
    pltpu.make_async_copy(ys_ref.at[pl.ds(0, TC * ROW_TILES), :], ybuf.at[slot], sems.at[slot]).wait()
    gate2 = mod_ref[0, :, 5 * D_MODEL:6 * D_MODEL]
    o_ref[...] = x1_ref[...] + _rms(_load_token_major(ybuf.at[slot], TC)) * (gate2 * gpost_ref[...])


def _combine(dest, ys, x1, mod3, g_post, cond_of_tile, tile0):
    n_tok = x1.shape[0]
    return pl.pallas_call(
        functools.partial(_combine_kernel, tile0=tile0),
        out_shape=jax.ShapeDtypeStruct((n_tok, D_MODEL), F32),
        grid_spec=pltpu.PrefetchScalarGridSpec(
            num_scalar_prefetch=1,
            grid=(n_tok // TC,),
            in_specs=[pl.BlockSpec(memory_space=pl.ANY),
                      pl.BlockSpec((TC, D_MODEL), lambda i, d: (i, 0)),
                      pl.BlockSpec((1, 1, 6 * D_MODEL), lambda i, d: (cond_of_tile(i), 0, 0)),
                      pl.BlockSpec((1, D_MODEL), lambda i, d: (0, 0))],
            out_specs=pl.BlockSpec((TC, D_MODEL), lambda i, d: (i, 0)),
            scratch_shapes=[pltpu.VMEM((2, TC * ROW_TILES, LANES), F32), pltpu.SemaphoreType.DMA((2,))]),
        compiler_params=_params(("arbitrary",)),
        name="combine",
    )(dest, ys, x1, mod3, g_post)


def _schedule(bucket, rank, counts):
    n_tok = bucket.shape[0]
    n_max = n_tok // TMX + N_BUCKETS
    cnt = counts[:N_BUCKETS, 0].astype(jnp.int32)
    tiles = (cnt + TMX - 1) // TMX
    tile_end = jnp.cumsum(tiles)
    tile_start = tile_end - tiles
    ids = jnp.arange(N_BUCKETS, dtype=jnp.int32)
    slot0 = jnp.sum(jnp.where(bucket[:, None] == ids[None, :], (tile_start * TMX)[None, :], 0), axis=1)
    dest = slot0 + rank
    i = jnp.arange(n_max, dtype=jnp.int32)
    total = tile_end[-1]
    valid = i < total
    tb = jnp.sum((jnp.minimum(i, total - 1)[:, None] >= tile_end[None, :]).astype(jnp.int32), axis=1)
    pairs = [(a, b) for a in range(EXPERTS_PER_GROUP) for b in range(a + 1, EXPERTS_PER_GROUP)]
    ea_tab = jnp.array([g * EXPERTS_PER_GROUP + a for g in range(N_GROUPS) for a, _ in pairs], jnp.int32)
    eb_tab = jnp.array([g * EXPERTS_PER_GROUP + b for g in range(N_GROUPS) for _, b in pairs], jnp.int32)
    hit = tb[:, None] == ids[None, :]
    look = lambda tab: jnp.sum(jnp.where(hit, tab[None, :], 0), axis=1)
    ea, eb = look(ea_tab), look(eb_tab)
    nv = jnp.where(valid, jnp.clip(look(cnt) - (i - look(tile_start)) * TMX, 0, TMX), 0)
    blk = jnp.minimum(i, total - 1)
    return dest, (ea, eb, nv, blk), n_max * TMX


def _block_diag_gates(rg_wa, rg_wx):
    heads = LANES // HEAD_RNN

    def bd(w):
        w = w.reshape(2, RNN_BLOCKS, heads, HEAD_RNN, HEAD_RNN)
        eye = jnp.eye(heads, dtype=w.dtype)
        full = jnp.einsum('dghij,hk->dghikj', w, eye)
        return full.reshape(2, RNN_BLOCKS, LANES, LANES)

    return jnp.concatenate([bd(rg_wa), bd(rg_wx)], axis=-1).astype(BF16)


def _row_tile(v):
    blocks = v.reshape(v.shape[:-1] + (RNN_BLOCKS, LANES))
    return jnp.concatenate([blocks] * PAIR, axis=-2)


def _to_time_major_state(h):
    return h.reshape(h.shape[0] // PAIR, TMJ_ROWS, LANES)


def kernel(x_prompt, x_sample, state_rglru, c, c_ctx, w_mod, b_mod, g_pre_mix, g_post_mix, g_pre_ffn,
           g_post_ffn, w_in, conv_w, conv_b, rg_wa, rg_ba, rg_wx, rg_bx, rg_lambda, sgu_g, sgu_w, sgu_b,
           w_out, router_g_w, router_g_b, router_e_w, router_e_b, exp_w_gate, exp_w_up, exp_w_down):
    assert w_mod.shape[0] == 1, "single-layer trunk"
    n_ctx, ctx_len, _ = x_prompt.shape
    n_dec, dec_len, _ = x_sample.shape
    l = 0

    n_cond = SUBLANES
    assert n_dec % PAIR == 0 and n_dec + PAIR <= n_cond
    cond = jnp.concatenate([c, jnp.broadcast_to(c_ctx, (PAIR, D_MODEL)),
                            jnp.zeros((n_cond - n_dec - PAIR, D_MODEL), F32)], axis=0)
    mod3 = _modulation(cond, w_mod[l], b_mod[l]).reshape(n_cond, 1, 6 * D_MODEL)
    pos_tab = _pos_table()

    w_rx, w_gate, w_u, w_v = jnp.split(w_in[l], [D_RNN, 2 * D_RNN, 2 * D_RNN + D_SGU], axis=1)
    w_in_b = jnp.concatenate([w_v, w_u, w_gate, w_rx], axis=1).astype(BF16)
    w_out_b = w_out[l].astype(BF16)
    sgu_w_b = sgu_w[l].reshape(2, 4 * CHUNK, CHUNK).astype(BF16)
    sgu_bias_tile = jnp.repeat(sgu_b[l].T, HEAD_SGU, axis=1)
    w_gates = _block_diag_gates(rg_wa[l], rg_wx[l])
    b_gates = 0.5 * jnp.stack([_row_tile(rg_ba[l]), _row_tile(rg_bx[l])], axis=1)
    lam = _row_tile(rg_lambda[l])
    conv_w_t = 0.5 * _row_tile(conv_w[l])
    conv_b_t = 0.5 * _row_tile(conv_b[l])
    lane_pad = ROUTER_LANES - E_LANE0 - N_EXPERTS
    router_w = jnp.pad(jnp.concatenate([router_g_w[l], router_e_w[l]], axis=1), ((0, 0), (0, lane_pad))).astype(BF16)
    router_b = jnp.pad(jnp.concatenate([router_g_b[l], router_e_b[l]]), (0, lane_pad)).reshape(1, ROUTER_LANES)
    gap, tail = E_ROW0 - N_GROUPS, ROUTER_ROWS - E_ROW0 - N_EXPERTS
    router_wt = jnp.concatenate([router_g_w[l].T, jnp.zeros((gap, D_MODEL), F32), router_e_w[l].T,
                                 jnp.zeros((tail, D_MODEL), F32)], axis=0).astype(BF16)
    router_bt = jnp.concatenate([router_g_b[l], jnp.zeros((gap,), F32), router_e_b[l], jnp.zeros((tail,), F32)])
    router_bt = jnp.broadcast_to(router_bt[:, None], (ROUTER_ROWS, LANES))
    earlier = jnp.triu(jnp.ones((TP, TP), BF16), k=1)
    row = lambda v: v.reshape(1, -1)

    n_ctx_tok = n_ctx * ctx_len

    def mixer(x, h0, cond_of_tile, cond_block, use_pos, counts0):
        n_seq, seq_len, _ = x.shape
        xf = x.reshape(n_seq * seq_len, D_MODEL)
        tab = pos_tab if use_pos else None
        xr, gg, y_sgu = _premix(x, mod3, cond_block, row(g_pre_mix[l]), w_in_b, row(sgu_g[l]),
                                sgu_w_b, sgu_bias_tile, tab)
        scan_params = (conv_w_t, conv_b_t, w_gates, b_gates, lam)
        hf, hf_last = _scan(xr, None, None, _to_time_major_state(h0[:, 0]), *scan_params, direction=0)
        y_rnn, hb_first = _scan(xr, gg, hf, _to_time_major_state(h0[:, 1]), *scan_params, direction=1)
        fstate = jnp.stack([hf_last.reshape(n_seq, D_RNN), hb_first.reshape(n_seq, D_RNN)], axis=1)
        x1, hn, bucket, rank, counts = _postmix(
            xf, y_rnn.reshape(n_seq * seq_len, D_RNN), y_sgu.reshape(n_seq * seq_len, D_SGU), mod3,
            cond_of_tile, row(g_post_mix[l]), row(g_pre_ffn[l]), w_out_b, router_wt, router_bt, earlier, tab,
            counts0)
        return x1, hn, bucket, rank, counts, fstate

    ctx_cond = lambda i: n_dec
    dec_cond = lambda i: i // (dec_len // TP)
    h0_ctx = jnp.zeros((n_ctx, 2, D_RNN), F32)
    counts0 = jnp.zeros((ROUTER_ROWS, LANES), F32)
    x1_ctx, hn_ctx, bucket_ctx, rank_ctx, counts, st = mixer(x_prompt, h0_ctx, ctx_cond, lambda p: n_dec // PAIR,
                                                             False, counts0)
    new_state = st.astype(state_rglru.dtype)[:, None]
    x1_dec, hn_dec, bucket_dec, rank_dec, counts, _ = mixer(x_sample, state_rglru[:, l].astype(F32), dec_cond,
                                                            lambda p: p, True, counts)

    dest, sched, n_slots = _schedule(jnp.concatenate([bucket_ctx, bucket_dec]),
                                     jnp.concatenate([rank_ctx, rank_dec]), counts)
    xs = _dispatch(dest, hn_ctx, hn_dec, n_slots)
    ys = _experts(sched, xs, router_w, router_b, exp_w_gate[l], exp_w_up[l], exp_w_down[l])
    y_prompt = _combine(dest, ys, x1_ctx, mod3, row(g_post_ffn[l]), ctx_cond, 0)
    y_sample = _combine(dest, ys, x1_dec, mod3, row(g_post_ffn[l]), lambda i: i // (dec_len // TC), n_ctx_tok // TC)
    return (y_prompt.reshape(x_prompt.shape), y_sample.reshape(x_sample.shape), new_state)
```

```python
import functools
import math

import jax
import jax.numpy as jnp
from jax import lax
from jax.experimental import pallas as pl
from jax.experimental.pallas import tpu as pltpu

D_MODEL = 1024
D_RNN = 512
D_SGU = 512
N_HEADS_RNN = 8
HEAD_RNN = D_RNN // N_HEADS_RNN
N_HEADS_SGU = 8
HEAD_SGU = D_SGU // N_HEADS_SGU
CHUNK = 128
GRID_W = 64
RG_C = 8.0
N_GROUPS = 4
EXPERTS_PER_GROUP = 4
N_EXPERTS = N_GROUPS * EXPERTS_PER_GROUP
D_EXPERT = 512
EPS = 1e-6
POS_BASE = 10000.0

LANES = 128
SUBLANES = 8
CONV_W = 4
CONV_LEFT = 2
PAIR = 2
RNN_BLOCKS = D_RNN // LANES
TMJ_ROWS = PAIR * RNN_BLOCKS
ROUTER_LANES = LANES
E_LANE0 = N_GROUPS
ROUTER_ROWS = 32
E_ROW0 = SUBLANES

PAIRS_PER_GROUP = EXPERTS_PER_GROUP * (EXPERTS_PER_GROUP - 1) // 2
N_BUCKETS = N_GROUPS * PAIRS_PER_GROUP

ROW_TILES = D_MODEL // LANES

MOD_COLS = 2048
TP = 1024
TD = 2048
TC = 512
DMA_UNROLL = 8
EXPERT_ROW_PARTS = 2
TMX = 512
TT = 512
PREMIX_ROWS = 1024
LC = 256
TS = 16
PB = 2
VMEM_LIMIT = 56 * 1024 * 1024

F32 = jnp.float32
BF16 = jnp.bfloat16


def _params(sem):
    return pltpu.CompilerParams(dimension_semantics=sem, vmem_limit_bytes=VMEM_LIMIT)


def _rms(x):
    return x * lax.rsqrt(jnp.mean(x * x, axis=-1, keepdims=True) + EPS)


def _sigmoid(x):
    return 0.5 * jnp.tanh(0.5 * x) + 0.5


def _mod_kernel(cond_ref, w_ref, b_ref, o_ref):
    c = cond_ref[...]
    s = c * _sigmoid(c)
    o_ref[...] = jnp.dot(s.astype(BF16), w_ref[...].astype(BF16),
                         preferred_element_type=F32) + b_ref[...]


def _modulation(cond, w_mod, b_mod):
    n = w_mod.shape[1]
    return pl.pallas_call(
        _mod_kernel,
        out_shape=jax.ShapeDtypeStruct((cond.shape[0], n), F32),
        grid=(n // MOD_COLS,),
        in_specs=[pl.BlockSpec(cond.shape, lambda j: (0, 0)),
                  pl.BlockSpec((D_MODEL, MOD_COLS), lambda j: (0, j)),
                  pl.BlockSpec((1, MOD_COLS), lambda j: (0, j))],
        out_specs=pl.BlockSpec((cond.shape[0], MOD_COLS), lambda j: (0, j)),
        compiler_params=_params(("arbitrary",)),
        name="modulation",
    )(cond, w_mod, b_mod.reshape(1, n))


def _pos_kernel(o_ref):
    n_freq = D_MODEL // 4
    k = lax.broadcasted_iota(jnp.int32, (GRID_W, n_freq), 1).astype(F32)
    p = lax.broadcasted_iota(jnp.int32, (GRID_W, n_freq), 0).astype(F32)
    freq = jnp.exp(-math.log(POS_BASE) * k / n_freq)
    ang = p * freq
    o_ref[:, 0:n_freq] = jnp.sin(ang)
    o_ref[:, n_freq:2 * n_freq] = jnp.cos(ang)


def _pos_table():
    return pl.pallas_call(
        _pos_kernel,
        out_shape=jax.ShapeDtypeStruct((GRID_W, D_MODEL // 2), F32),
        name="pos_table",
    )()


def _add_pos(x, pos_refs, q0):
    if pos_refs is None:
        return x
    rows_ref, cols_ref = pos_refs
    reps = x.shape[0] // GRID_W
    rpart = jnp.concatenate(
        [jnp.broadcast_to(rows_ref[q:q + 1, :], (GRID_W, D_MODEL // 2)) for q in range(q0, q0 + reps)], axis=0)
    cpart = jnp.concatenate([cols_ref[...]] * reps, axis=0)
    return jnp.concatenate([x[:, :D_MODEL // 2] + rpart, x[:, D_MODEL // 2:] + cpart], axis=1)


def _load_x(x_ref, pos_refs, r0, n):
    return _add_pos(x_ref[r0:r0 + n, :], pos_refs, r0 // GRID_W)


def _premix_kernel(*refs, add_pos):
    refs = list(refs)
    x_ref = refs.pop(0)
    nseq, tt = x_ref.shape[:2]
    pos_refs = (refs.pop(0), refs.pop(0)) if add_pos else None
    mod_ref, g_ref, win_ref, sgug_ref, sguw_ref, sgub_ref, xr_ref, gg_ref, ys_ref = refs
    hn = []
    for s in range(nseq):
        shift = mod_ref[s, :, 0:D_MODEL]
        scale = mod_ref[s, :, D_MODEL:2 * D_MODEL]
        hn.append(_rms(_add_pos(x_ref[s], pos_refs, 0)) * (g_ref[...] * (1.0 + scale)) + shift)
    z = jnp.dot(jnp.concatenate(hn, axis=0).astype(BF16), win_ref[...],
                preferred_element_type=F32)
    half = D_SGU // 2
    heads_per_half = N_HEADS_SGU // 2
    lane_head = lax.broadcasted_iota(jnp.int32, (CHUNK, half), 1) // HEAD_SGU
    for s in range(nseq):
        zs = z[s * tt:(s + 1) * tt]
        vn = (_rms(zs[:, 0:D_SGU]) * sgug_ref[...]).astype(BF16)
        u = zs[:, D_SGU:2 * D_SGU]
        for c in range(tt // CHUNK):
            rows = slice(c * CHUNK, (c + 1) * CHUNK)
            halves = []
            for hf in range(2):
                r = jnp.dot(sguw_ref[hf], vn[rows, hf * half:(hf + 1) * half],
                            preferred_element_type=F32)
                sel = jnp.zeros((CHUNK, half), F32)
                for h in range(heads_per_half):
                    sel = jnp.where(lane_head == h, r[h * CHUNK:(h + 1) * CHUNK], sel)
                halves.append(sel)
            gatev = jnp.concatenate(halves, axis=1) + sgub_ref[...]
            ys_ref[s, rows, :] = (u[rows] * gatev).astype(BF16)
    for s in range(nseq):
        zs = z[s * tt:(s + 1) * tt]
        gg = jax.nn.gelu(zs[:, 2 * D_SGU:2 * D_SGU + D_RNN])
        xr = zs[:, 2 * D_SGU + D_RNN:]
        for k in range(RNN_BLOCKS):
            first = (s // PAIR) * tt * TMJ_ROWS + (s % PAIR) * RNN_BLOCKS + k
            rows = pl.ds(first, tt, stride=TMJ_ROWS)
            xr_ref[rows, :] = xr[:, k * LANES:(k + 1) * LANES]
            gg_ref[rows, :] = gg[:, k * LANES:(k + 1) * LANES]


def _premix(x, mod3, cond_block, g_pre, w_in_b, sgu_g, sgu_w_b, sgu_bias_tile, pos_tab):
    n_seq, seq_len, _ = x.shape
    tt = min(seq_len, TT)
    nseq = max(PAIR, PREMIX_ROWS // tt)
    assert nseq == PAIR or tt == seq_len
    n_pairs, n_tiles = n_seq // PAIR, seq_len // tt
    add_pos = pos_tab is not None
    const2 = lambda p, j: (0, 0)
    in_specs = [pl.BlockSpec((nseq, tt, D_MODEL), lambda p, j: (p, j, 0))]
    args = [x]
    if add_pos:
        reps = tt // GRID_W
        in_specs += [pl.BlockSpec((None, reps, D_MODEL // 2), lambda p, j: (j, 0, 0)),
                     pl.BlockSpec((GRID_W, D_MODEL // 2), const2)]
        args += [pos_tab.reshape(GRID_W // reps, reps, D_MODEL // 2), pos_tab]
    in_specs += [pl.BlockSpec((nseq, 1, 6 * D_MODEL), lambda p, j: (cond_block(p), 0, 0)),
                 pl.BlockSpec((1, D_MODEL), const2),
                 pl.BlockSpec((D_MODEL, 2 * D_RNN + 2 * D_SGU), const2),
                 pl.BlockSpec((1, D_SGU), const2),
                 pl.BlockSpec((2, 4 * CHUNK, CHUNK), lambda p, j: (0, 0, 0)),
                 pl.BlockSpec((CHUNK, D_SGU), const2)]
    args += [mod3, g_pre, w_in_b, sgu_g, sgu_w_b, sgu_bias_tile]
    tmj = jax.ShapeDtypeStruct((n_pairs * seq_len * TMJ_ROWS, LANES), F32)
    tmj_spec = pl.BlockSpec((nseq // PAIR * tt * TMJ_ROWS, LANES), lambda p, j: (p * n_tiles + j, 0))
    xr, gg, y_sgu = pl.pallas_call(
        functools.partial(_premix_kernel, add_pos=add_pos),
        out_shape=(tmj, tmj, jax.ShapeDtypeStruct((n_seq, seq_len, D_SGU), BF16)),
        grid=(n_seq // nseq, n_tiles),
        in_specs=in_specs,
        out_specs=(tmj_spec, tmj_spec, pl.BlockSpec((nseq, tt, D_SGU), lambda p, j: (p, j, 0))),
        compiler_params=_params(("parallel", "parallel")),
        name="premix",
    )(*args)
    shape4 = (n_pairs, seq_len, TMJ_ROWS, LANES)
    return xr.reshape(shape4), gg.reshape(shape4), y_sgu


def _scan_kernel(*refs, reverse, n_chunks):
    if reverse:
        (xprev_ref, x_ref, xnext_ref, gg_ref, hf_ref, h0_ref, cw_ref, cb_ref, wg_ref, bg_ref, lam_ref,
         y_ref, fs_ref, xwin, xc_s, r_s, i_s, a_s, b_s, y_s, hcar) = refs
    else:
        (xprev_ref, x_ref, xnext_ref, h0_ref, cw_ref, cb_ref, wg_ref, bg_ref, lam_ref,
         hf_ref, fs_ref, xwin, xc_s, r_s, i_s, a_s, b_s, hcar) = refs
    c = pl.program_id(1)
    chunk = n_chunks - 1 - c if reverse else c
    sub_rows = TS * TMJ_ROWS

    def rows_of(pb, t0, n_steps):
        first = (pb * LC + t0) * TMJ_ROWS
        if not isinstance(first, int):
            first = pl.multiple_of(first, TMJ_ROWS)
        return pl.ds(first, n_steps * TMJ_ROWS)

    @pl.when(c == 0)
    def _():
        hcar[...] = h0_ref[...]

    xwin[:, 0:CONV_LEFT] = jnp.where(chunk > 0, xprev_ref[...], 0.0)
    xwin[:, LC + CONV_LEFT:LC + CONV_W - 1] = jnp.where(chunk < n_chunks - 1, xnext_ref[...], 0.0)

    xwin[:, CONV_LEFT:CONV_LEFT + LC] = x_ref[...]

    neg_lam = -lam_ref[...]
    softplus = jnp.maximum(neg_lam, 0.0) + jnp.log(1.0 + jnp.exp(-jnp.abs(neg_lam)))
    half_decay = (-0.5 * RG_C * math.log2(math.e)) * softplus

    def conv(pb):
        for t0 in range(0, LC, TS):
            xc = cb_ref[...] + cw_ref[0] * xwin[pb, t0:t0 + TS]
            for k in range(1, CONV_W):
                xc = xc + cw_ref[k] * xwin[pb, t0 + k:t0 + k + TS]
            xc_s[rows_of(pb, t0, TS), :] = xc.reshape(sub_rows, LANES)

    def gate_matmuls(pb):
        for k in range(RNN_BLOCKS):
            rows = pl.ds(pb * LC * TMJ_ROWS + k, LC * PAIR, stride=RNN_BLOCKS)
            g = jnp.dot(xc_s[rows, :].astype(BF16), wg_ref[k], preferred_element_type=F32)
            r_s[rows, :] = g[:, :LANES]
            i_s[rows, :] = g[:, LANES:]

    def gates(pb):
        for t0 in range(0, LC, TS):
            rows = rows_of(pb, t0, TS)
            tile = lambda ref: ref[rows, :].reshape(TS, TMJ_ROWS, LANES)
            tr = jnp.tanh(tile(r_s) + bg_ref[0])
            ti = jnp.tanh(tile(i_s) + bg_ref[1])
            log2_a = tr * half_decay + half_decay
            a = jnp.exp2(log2_a)
            q = jnp.tanh(log2_a * (-math.log(2.0))) * (a * a + 1.0)
            b = jnp.where(q > 0.0, q * lax.rsqrt(q), 0.0) * ((ti + 1.0) * tile(xc_s))
            a_s[rows, :] = a.reshape(sub_rows, LANES)
            b_s[rows, :] = b.reshape(sub_rows, LANES)

    for stage in (conv, gate_matmuls, gates):
        for pb in range(PB):
            stage(pb)

    def step(j, hs):
        t = LC - 1 - j if reverse else j
        out = []
        for pb in range(PB):
            rows = rows_of(pb, t, 1)
            h = a_s[rows, :] * hs[pb] + b_s[rows, :]
            if reverse:
                y_s[rows, :] = h
            else:
                hf_ref[pb, t] = h
            out.append(h)
        return tuple(out)

    hs = lax.fori_loop(0, LC, step, tuple(hcar[pb] for pb in range(PB)), unroll=8)
    for pb in range(PB):
        hcar[pb] = hs[pb]
        fs_ref[pb] = hs[pb]

    if reverse:
        for pb in range(PB):
            for t0 in range(0, LC, TS):
                rows = rows_of(pb, t0, TS)
                both = hf_ref[pb, t0:t0 + TS].reshape(sub_rows, LANES) + y_s[rows, :]
                y_s[rows, :] = both * gg_ref[pb, t0:t0 + TS].reshape(sub_rows, LANES)
            for s in range(PAIR):
                cols = [y_s[pl.ds(pb * LC * TMJ_ROWS + s * RNN_BLOCKS + k, LC, stride=TMJ_ROWS), :]
                        for k in range(RNN_BLOCKS)]
                y_ref[pb * PAIR + s] = jnp.concatenate(cols, axis=1).astype(BF16)


def _scan(xr, gg, hf, h0, conv_w, conv_b, w_gates, b_gates, lam, direction):
    n_pairs, seq_len = xr.shape[:2]
    n_chunks = seq_len // LC
    reverse = direction == 1
    pos = (lambda c: n_chunks - 1 - c) if reverse else (lambda c: c)
    tmj_blk = pl.BlockSpec((PB, LC, TMJ_ROWS, LANES), lambda i, c: (i, pos(c), 0, 0))
    state_blk = pl.BlockSpec((PB, TMJ_ROWS, LANES), lambda i, c: (i, 0, 0))
    per_dir = lambda *shape: pl.BlockSpec((None,) + shape, lambda i, c: (direction,) + (0,) * len(shape))
    in_specs = [
        pl.BlockSpec((PB, CONV_LEFT, TMJ_ROWS, LANES),
                     lambda i, c: (i, jnp.maximum(pos(c) * (LC // CONV_LEFT) - 1, 0), 0, 0)),
        tmj_blk,
        pl.BlockSpec((PB, 1, TMJ_ROWS, LANES), lambda i, c: (i, jnp.minimum((pos(c) + 1) * LC, seq_len - 1), 0, 0)),
    ]
    args = [xr, xr, xr]
    if reverse:
        in_specs += [tmj_blk, tmj_blk]
        args += [gg, hf]
    in_specs += [state_blk,
                 pl.BlockSpec((CONV_W, TMJ_ROWS, LANES), lambda i, c: (0, 0, 0)),
                 pl.BlockSpec((TMJ_ROWS, LANES), lambda i, c: (0, 0)),
                 per_dir(RNN_BLOCKS, LANES, 2 * LANES),
                 per_dir(2, TMJ_ROWS, LANES),
                 per_dir(TMJ_ROWS, LANES)]
    args += [h0, conv_w, conv_b, w_gates, b_gates, lam]
    flat = pltpu.VMEM((PB * LC * TMJ_ROWS, LANES), F32)
    scratch = [pltpu.VMEM((PB, LC + CONV_W - 1, TMJ_ROWS, LANES), F32)] + [flat] * (6 if reverse else 5)
    scratch += [pltpu.VMEM((PB, TMJ_ROWS, LANES), F32)]
    state = jax.ShapeDtypeStruct((n_pairs, TMJ_ROWS, LANES), F32)
    if reverse:
        out_shape = (jax.ShapeDtypeStruct((n_pairs * PAIR, seq_len, D_RNN), BF16), state)
        out_specs = (pl.BlockSpec((PB * PAIR, LC, D_RNN), lambda i, c: (i, pos(c), 0)), state_blk)
    else:
        out_shape = (jax.ShapeDtypeStruct(xr.shape, F32), state)
        out_specs = (tmj_blk, state_blk)
    return pl.pallas_call(
        functools.partial(_scan_kernel, reverse=reverse, n_chunks=n_chunks),
        out_shape=out_shape,
        grid=(n_pairs // PB, n_chunks),
        in_specs=in_specs,
        out_specs=out_specs,
        scratch_shapes=scratch,
        compiler_params=_params(("parallel", "arbitrary")),
        name="scan_bwd" if reverse else "scan_fwd",
    )(*args)


def _route(lt):
    n = lt.shape[1]
    row = lax.broadcasted_iota(jnp.int32, (EXPERTS_PER_GROUP, n), 0)
    neg = jnp.float32(-jnp.inf)

    def arg_max(v):
        m = jnp.max(v, axis=0, keepdims=True)
        return jnp.min(jnp.where(v == m, row, EXPERTS_PER_GROUP), axis=0, keepdims=True)

    g_idx = arg_max(lt[0:N_GROUPS])
    el = lt[E_ROW0:E_ROW0 + EXPERTS_PER_GROUP]
    for g in range(1, N_GROUPS):
        first = E_ROW0 + g * EXPERTS_PER_GROUP
        el = jnp.where(g_idx == g, lt[first:first + EXPERTS_PER_GROUP], el)
    i1 = arg_max(el)
    i2 = arg_max(jnp.where(row == i1, neg, el))
    ja = jnp.minimum(i1, i2)
    jb = jnp.maximum(i1, i2)
    pair = (ja * (2 * EXPERTS_PER_GROUP - 1 - ja)) // 2 + (jb - ja - 1)
    return g_idx * PAIRS_PER_GROUP + pair


def _store_token_major(ref, x, t0=0):
    n = x.shape[0]
    for k in range(ROW_TILES):
        ref[pl.ds(t0 * ROW_TILES + k, n, stride=ROW_TILES), :] = x[:, k * LANES:(k + 1) * LANES]


def _load_token_major(ref, n):
    return jnp.concatenate([ref[pl.ds(k, n, stride=ROW_TILES), :] for k in range(ROW_TILES)], axis=1)


def _postmix_kernel(*refs, add_pos):
    refs = list(refs)
    x_ref = refs.pop(0)
    pos_refs = (refs.pop(0), refs.pop(0)) if add_pos else None
    (yr_ref, ys_ref, mod_ref, gpost_ref, gpre_ref, wout_ref, rw_ref, rb_ref, earlier_ref, cnt0_ref,
     x1_ref, hn_ref, rt_ref, cnt_ref, run_ref) = refs

    @pl.when(pl.program_id(0) == 0)
    def _():
        run_ref[...] = cnt0_ref[...]

    gate1 = mod_ref[0, :, 2 * D_MODEL:3 * D_MODEL]
    shift2 = mod_ref[0, :, 3 * D_MODEL:4 * D_MODEL]
    scale2 = mod_ref[0, :, 4 * D_MODEL:5 * D_MODEL]
    y = (jnp.dot(yr_ref[...], wout_ref[0:D_RNN, :], preferred_element_type=F32)
         + jnp.dot(ys_ref[...], wout_ref[D_RNN:, :], preferred_element_type=F32))
    x1 = _load_x(x_ref, pos_refs, 0, TP) + _rms(y) * (gate1 * gpost_ref[...])
    x1_ref[...] = x1
    hn = _rms(x1) * (gpre_ref[...] * (1.0 + scale2)) + shift2
    _store_token_major(hn_ref, hn)
    lt = lax.dot_general(rw_ref[...], hn.astype(BF16), (((1,), (1,)), ((), ())),
                         preferred_element_type=F32) + rb_ref[:, 0:1]
    bucket = _route(lt)
    onehot = lax.broadcasted_iota(jnp.int32, (ROUTER_ROWS, TP), 0) == bucket
    before = jnp.dot(onehot.astype(BF16), earlier_ref[...], preferred_element_type=F32) + run_ref[:, 0:1]
    rank = jnp.sum(jnp.where(onehot, before, 0.0), axis=0, keepdims=True).astype(jnp.int32)
    row = lax.broadcasted_iota(jnp.int32, (SUBLANES, TP), 0)
    rt_ref[...] = jnp.where(row == 0, bucket, jnp.where(row == 1, rank, 0))
    run_ref[...] += jnp.sum(onehot.astype(F32), axis=1, keepdims=True)
    cnt_ref[...] = run_ref[...]


def _postmix(x, y_rnn, y_sgu, mod3, cond_of_tile, g_post, g_pre, w_out_b, router_wt, router_bt, earlier, pos_tab,
             counts0):
    n_tok = x.shape[0]
    n_tiles = n_tok // TP
    add_pos = pos_tab is not None
    tok = lambda i: (i, 0)
    const2 = lambda i: (0, 0)
    in_specs = [pl.BlockSpec((TP, D_MODEL), tok)]
    args = [x]
    if add_pos:
        reps = TP // GRID_W
        tiles_per_seq = GRID_W // reps
        in_specs += [pl.BlockSpec((None, reps, D_MODEL // 2), lambda i: (i % tiles_per_seq, 0, 0)),
                     pl.BlockSpec((GRID_W, D_MODEL // 2), const2)]
        args += [pos_tab.reshape(tiles_per_seq, reps, D_MODEL // 2), pos_tab]
    in_specs += [pl.BlockSpec((TP, D_RNN), tok),
                 pl.BlockSpec((TP, D_SGU), tok),
                 pl.BlockSpec((1, 1, 6 * D_MODEL), lambda i: (cond_of_tile(i), 0, 0)),
                 pl.BlockSpec((1, D_MODEL), const2),
                 pl.BlockSpec((1, D_MODEL), const2),
                 pl.BlockSpec((D_MODEL, D_MODEL), const2),
                 pl.BlockSpec((ROUTER_ROWS, D_MODEL), const2),
                 pl.BlockSpec((ROUTER_ROWS, LANES), const2),
                 pl.BlockSpec((TP, TP), const2),
                 pl.BlockSpec((ROUTER_ROWS, LANES), const2)]
    args += [y_rnn, y_sgu, mod3, g_post, g_pre, w_out_b, router_wt, router_bt, earlier, counts0]
    counts_spec = pl.BlockSpec((ROUTER_ROWS, LANES), const2)
    x1, hn, route, counts = pl.pallas_call(
        functools.partial(_postmix_kernel, add_pos=add_pos),
        out_shape=(jax.ShapeDtypeStruct((n_tok, D_MODEL), F32),
                   jax.ShapeDtypeStruct((n_tok * ROW_TILES, LANES), F32),
                   jax.ShapeDtypeStruct((n_tiles * SUBLANES, TP), jnp.int32),
                   jax.ShapeDtypeStruct((ROUTER_ROWS, LANES), F32)),
        grid=(n_tiles,),
        in_specs=in_specs,
        out_specs=(pl.BlockSpec((TP, D_MODEL), tok),
                   pl.BlockSpec((TP * ROW_TILES, LANES), tok),
                   pl.BlockSpec((SUBLANES, TP), tok),
                   counts_spec),
        scratch_shapes=[pltpu.VMEM((ROUTER_ROWS, LANES), F32)],
        compiler_params=_params(("arbitrary",)),
        name="postmix",
    )(*args)
    route = route.reshape(n_tiles, SUBLANES, TP)
    return x1, hn, route[:, 0].reshape(n_tok), route[:, 1].reshape(n_tok), counts


def _token_rows(ref, t):
    return ref.at[pl.ds(pl.multiple_of(t * ROW_TILES, ROW_TILES), ROW_TILES), :]


def _dispatch_kernel(dest_ref, hc_ref, hs_ref, xs_ref, sem, *, n_ctx_steps):
    i = pl.program_id(0)
    base = i * TD

    def scatter(src_ref):
        def start(g, carry):
            for u in range(DMA_UNROLL):
                r = g * DMA_UNROLL + u
                pltpu.make_async_copy(_token_rows(src_ref, r), _token_rows(xs_ref, dest_ref[base + r]),
                                      sem).start(priority=u % 2)
            return carry

        lax.fori_loop(0, TD // DMA_UNROLL, start, 0)
        pltpu.make_async_copy(src_ref, xs_ref.at[pl.ds(0, TD * ROW_TILES), :], sem).wait()

    @pl.when(i < n_ctx_steps)
    def _():
        scatter(hc_ref)

    @pl.when(i >= n_ctx_steps)
    def _():
        scatter(hs_ref)


def _dispatch(dest, hn_ctx, hn_dec, n_slots):
    n_ctx_steps = hn_ctx.shape[0] // (TD * ROW_TILES)
    n_dec_steps = hn_dec.shape[0] // (TD * ROW_TILES)
    return pl.pallas_call(
        functools.partial(_dispatch_kernel, n_ctx_steps=n_ctx_steps),
        out_shape=jax.ShapeDtypeStruct((n_slots * ROW_TILES, LANES), F32),
        grid_spec=pltpu.PrefetchScalarGridSpec(
            num_scalar_prefetch=1,
            grid=(n_ctx_steps + n_dec_steps,),
            in_specs=[pl.BlockSpec((TD * ROW_TILES, LANES), lambda i, d: (jnp.minimum(i, n_ctx_steps - 1), 0)),
                      pl.BlockSpec((TD * ROW_TILES, LANES), lambda i, d: (jnp.maximum(i - n_ctx_steps, 0), 0))],
            out_specs=pl.BlockSpec(memory_space=pl.ANY),
            scratch_shapes=[pltpu.SemaphoreType.DMA(())]),
        compiler_params=_params(("arbitrary",)),
        name="dispatch",
    )(dest, hn_ctx, hn_dec)


def _experts_kernel(ea_ref, eb_ref, nv_ref, blk_ref, xs_ref, rw_ref, rb_ref, *refs):
    w32_refs, ys_ref, w_refs = refs[:6], refs[6], refs[7:]
    wga_ref, wua_ref, wda_ref, wgb_ref, wub_ref, wdb_ref = w_refs
    i = pl.program_id(0)
    nv = nv_ref[i]
    prev = jnp.maximum(i - 1, 0)

    @pl.when((i == 0) | (ea_ref[i] != ea_ref[prev]) | (eb_ref[i] != eb_ref[prev]))
    def _():
        for w32_ref, w_ref in zip(w32_refs, w_refs):
            w_ref[...] = w32_ref[0].astype(BF16)

    @pl.when(nv > 0)
    def _():
        row = lax.broadcasted_iota(jnp.int32, (TMX, 1), 0)
        xb = jnp.where(row < nv, _load_token_major(xs_ref, TMX), 0.0).astype(BF16)
        logits = jnp.dot(xb, rw_ref[...], preferred_element_type=F32) + rb_ref[...]
        lane = lax.broadcasted_iota(jnp.int32, logits.shape, 1)
        ea = ea_ref[i]
        eb = eb_ref[i]
        gmask = lane < N_GROUPS
        gl = jnp.where(gmask, logits, -jnp.inf)
        gmax = jnp.max(gl, axis=-1, keepdims=True)
        gexp = jnp.where(gmask, jnp.exp(gl - gmax), 0.0)
        g_own = jnp.sum(jnp.where(lane == ea // EXPERTS_PER_GROUP, gexp, 0.0), axis=-1, keepdims=True)
        g_w = g_own / jnp.sum(gexp, axis=-1, keepdims=True)
        la = jnp.sum(jnp.where(lane == ea + E_LANE0, logits, 0.0), axis=-1, keepdims=True)
        lb = jnp.sum(jnp.where(lane == eb + E_LANE0, logits, 0.0), axis=-1, keepdims=True)
        m = jnp.maximum(la, lb)
        pa = jnp.exp(la - m)
        pb = jnp.exp(lb - m)
        inv = g_w / (pa + pb)

        def hidden(x, wg_ref, wu_ref, w):
            g = jnp.dot(x, wg_ref[...], preferred_element_type=F32)
            u = jnp.dot(x, wu_ref[...], preferred_element_type=F32)
            return ((g * _sigmoid(g)) * u * w).astype(BF16)

        part = TMX // EXPERT_ROW_PARTS
        ys = []
        for h in range(EXPERT_ROW_PARTS):
            rows = slice(h * part, (h + 1) * part)
            act_a = hidden(xb[rows], wga_ref, wua_ref, (pa * inv)[rows])
            act_b = hidden(xb[rows], wgb_ref, wub_ref, (pb * inv)[rows])
            y = (jnp.dot(act_a, wda_ref[...], preferred_element_type=F32)
                 + jnp.dot(act_b, wdb_ref[...], preferred_element_type=F32))
            ys.append(y)
        _store_token_major(ys_ref, jnp.concatenate(ys, axis=0))


def _experts(sched, xs, router_w, router_b, wg, wu, wd):
    ea, eb, nv, blk = sched
    n_tiles = ea.shape[0]
    rows = lambda i, ea, eb, nv, blk: (blk[i], 0)
    const2 = lambda i, ea, eb, nv, blk: (0, 0)
    exp_a = lambda i, ea, eb, nv, blk: (ea[i], 0, 0)
    exp_b = lambda i, ea, eb, nv, blk: (eb[i], 0, 0)
    w_in_spec = lambda m: pl.BlockSpec((1, D_MODEL, D_EXPERT), m)
    w_out_spec = lambda m: pl.BlockSpec((1, D_EXPERT, D_MODEL), m)
    return pl.pallas_call(
        _experts_kernel,
        out_shape=jax.ShapeDtypeStruct(xs.shape, F32),
        grid_spec=pltpu.PrefetchScalarGridSpec(
            num_scalar_prefetch=4,
            grid=(n_tiles,),
            in_specs=[pl.BlockSpec((TMX * ROW_TILES, LANES), rows),
                      pl.BlockSpec((D_MODEL, ROUTER_LANES), const2),
                      pl.BlockSpec((1, ROUTER_LANES), const2),
                      w_in_spec(exp_a), w_in_spec(exp_a), w_out_spec(exp_a),
                      w_in_spec(exp_b), w_in_spec(exp_b), w_out_spec(exp_b)],
            out_specs=pl.BlockSpec((TMX * ROW_TILES, LANES), rows),
            scratch_shapes=[pltpu.VMEM((D_MODEL, D_EXPERT), BF16), pltpu.VMEM((D_MODEL, D_EXPERT), BF16),
                            pltpu.VMEM((D_EXPERT, D_MODEL), BF16)] * 2),
        compiler_params=_params(("arbitrary",)),
        name="experts",
    )(ea, eb, nv, blk, xs, router_w, router_b, wg, wu, wd, wg, wu, wd)


def _combine_kernel(dest_ref, ys_ref, x1_ref, mod_ref, gpost_ref, o_ref, ybuf, sems, *, tile0):
    i = pl.program_id(0)
    n = pl.num_programs(0)

    def fetch(tile, slot):
        base = (tile + tile0) * TC

        def start(g, carry):
            for u in range(DMA_UNROLL):
                r = g * DMA_UNROLL + u
                pltpu.make_async_copy(_token_rows(ys_ref, dest_ref[base + r]), _token_rows(ybuf.at[slot], r),
                                      sems.at[slot]).start(priority=u % 2)
            return carry

        lax.fori_loop(0, TC // DMA_UNROLL, start, 0)

    @pl.when(i == 0)
    def _():
        fetch(0, 0)

    @pl.when(i + 1 < n)
    def _():
        fetch(i + 1, (i + 1) % 2)

    slot = i % 2
    pltpu.make_async_copy(ys_ref.at[pl.ds(0, TC * ROW_TILES), :], ybuf.at[slot], sems.at[slot]).wait()
    gate2 = mod_ref[0, :, 5 * D_MODEL:6 * D_MODEL]
    o_ref[...] = x1_ref[...] + _rms(_load_token_major(ybuf.at[slot], TC)) * (gate2 * gpost_ref[...])


def _combine(dest, ys, x1, mod3, g_post, cond_of_tile, tile0):
    n_tok = x1.shape[0]
    return pl.pallas_call(
        functools.partial(_combine_kernel, tile0=tile0),
        out_shape=jax.ShapeDtypeStruct((n_tok, D_MODEL), F32),
        grid_spec=pltpu.PrefetchScalarGridSpec(
            num_scalar_prefetch=1,
            grid=(n_tok // TC,),
            in_specs=[pl.BlockSpec(memory_space=pl.ANY),
                      pl.BlockSpec((TC, D_MODEL), lambda i, d: (i, 0)),
                      pl.BlockSpec((1, 1, 6 * D_MODEL), lambda i, d: (cond_of_tile(i), 0, 0)),
                      pl.BlockSpec((1, D_MODEL), lambda i, d: (0, 0))],
            out_specs=pl.BlockSpec((TC, D_MODEL), lambda i, d: (i, 0)),
            scratch_shapes=[pltpu.VMEM((2, TC * ROW_TILES, LANES), F32), pltpu.SemaphoreType.DMA((2,))]),
        compiler_params=_params(("arbitrary",)),
        name="combine",
    )(dest, ys, x1, mod3, g_post)


def _schedule(bucket, rank, counts):
    n_tok = bucket.shape[0]
    n_max = n_tok // TMX + N_BUCKETS
    cnt = counts[:N_BUCKETS, 0].astype(jnp.int32)
    tiles = (cnt + TMX - 1) // TMX
    tile_end = jnp.cumsum(tiles)
    tile_start = tile_end - tiles
    ids = jnp.arange(N_BUCKETS, dtype=jnp.int32)
    slot0 = jnp.sum(jnp.where(bucket[:, None] == ids[None, :], (tile_start * TMX)[None, :], 0), axis=1)
    dest = slot0 + rank
    i = jnp.arange(n_max, dtype=jnp.int32)
    total = tile_end[-1]
    valid = i < total
    tb = jnp.sum((jnp.minimum(i, total - 1)[:, None] >= tile_end[None, :]).astype(jnp.int32), axis=1)
    pairs = [(a, b) for a in range(EXPERTS_PER_GROUP) for b in range(a + 1, EXPERTS_PER_GROUP)]
    ea_tab = jnp.array([g * EXPERTS_PER_GROUP + a for g in range(N_GROUPS) for a, _ in pairs], jnp.int32)
    eb_tab = jnp.array([g * EXPERTS_PER_GROUP + b for g in range(N_GROUPS) for _, b in pairs], jnp.int32)
    hit = tb[:, None] == ids[None, :]
    look = lambda tab: jnp.sum(jnp.where(hit, tab[None, :], 0), axis=1)
    ea, eb = look(ea_tab), look(eb_tab)
    nv = jnp.where(valid, jnp.clip(look(cnt) - (i - look(tile_start)) * TMX, 0, TMX), 0)
    blk = jnp.minimum(i, total - 1)
    return dest, (ea, eb, nv, blk), n_max * TMX


def _block_diag_gates(rg_wa, rg_wx):
    heads = LANES // HEAD_RNN

    def bd(w):
        w = w.reshape(2, RNN_BLOCKS, heads, HEAD_RNN, HEAD_RNN)
        eye = jnp.eye(heads, dtype=w.dtype)
        full = jnp.einsum('dghij,hk->dghikj', w, eye)
        return full.reshape(2, RNN_BLOCKS, LANES, LANES)

    return jnp.concatenate([bd(rg_wa), bd(rg_wx)], axis=-1).astype(BF16)


def _row_tile(v):
    blocks = v.reshape(v.shape[:-1] + (RNN_BLOCKS, LANES))
    return jnp.concatenate([blocks] * PAIR, axis=-2)


def _to_time_major_state(h):
    return h.reshape(h.shape[0] // PAIR, TMJ_ROWS, LANES)


def kernel(x_prompt, x_sample, state_rglru, c, c_ctx, w_mod, b_mod, g_pre_mix, g_post_mix, g_pre_ffn,
           g_post_ffn, w_in, conv_w, conv_b, rg_wa, rg_ba, rg_wx, rg_bx, rg_lambda, sgu_g, sgu_w, sgu_b,
           w_out, router_g_w, router_g_b, router_e_w, router_e_b, exp_w_gate, exp_w_up, exp_w_down):
    assert w_mod.shape[0] == 1, "single-layer trunk"
    n_ctx, ctx_len, _ = x_prompt.shape
    n_dec, dec_len, _ = x_sample.shape
    l = 0

    n_cond = SUBLANES
    ctx_rows = n_cond - n_dec
    ctx_per_step = max(PAIR, PREMIX_ROWS // min(ctx_len, TT))
    assert n_dec % PAIR == 0 and ctx_rows == ctx_per_step and n_dec % ctx_per_step == 0
    cond = jnp.concatenate([c, jnp.broadcast_to(c_ctx, (ctx_rows, D_MODEL))], axis=0)
    mod3 = _modulation(cond, w_mod[l], b_mod[l]).reshape(n_cond, 1, 6 * D_MODEL)
    pos_tab = _pos_table()

    w_rx, w_gate, w_u, w_v = jnp.split(w_in[l], [D_RNN, 2 * D_RNN, 2 * D_RNN + D_SGU], axis=1)
    w_in_b = jnp.concatenate([w_v, w_u, w_gate, w_rx], axis=1).astype(BF16)
    w_out_b = w_out[l].astype(BF16)
    sgu_w_b = sgu_w[l].reshape(2, 4 * CHUNK, CHUNK).astype(BF16)
    sgu_bias_tile = jnp.repeat(sgu_b[l].T, HEAD_SGU, axis=1)
    w_gates = _block_diag_gates(rg_wa[l], rg_wx[l])
    b_gates = 0.5 * jnp.stack([_row_tile(rg_ba[l]), _row_tile(rg_bx[l])], axis=1)
    lam = _row_tile(rg_lambda[l])
    conv_w_t = 0.5 * _row_tile(conv_w[l])
    conv_b_t = 0.5 * _row_tile(conv_b[l])
    lane_pad = ROUTER_LANES - E_LANE0 - N_EXPERTS
    router_w = jnp.pad(jnp.concatenate([router_g_w[l], router_e_w[l]], axis=1), ((0, 0), (0, lane_pad))).astype(BF16)
    router_b = jnp.pad(jnp.concatenate([router_g_b[l], router_e_b[l]]), (0, lane_pad)).reshape(1, ROUTER_LANES)
    gap, tail = E_ROW0 - N_GROUPS, ROUTER_ROWS - E_ROW0 - N_EXPERTS
    router_wt = jnp.concatenate([router_g_w[l].T, jnp.zeros((gap, D_MODEL), F32), router_e_w[l].T,
                                 jnp.zeros((tail, D_MODEL), F32)], axis=0).astype(BF16)
    router_bt = jnp.concatenate([router_g_b[l], jnp.zeros((gap,), F32), router_e_b[l], jnp.zeros((tail,), F32)])
    router_bt = jnp.broadcast_to(router_bt[:, None], (ROUTER_ROWS, LANES))
    earlier = jnp.triu(jnp.ones((TP, TP), BF16), k=1)
    row = lambda v: v.reshape(1, -1)

    n_ctx_tok = n_ctx * ctx_len

    def mixer(x, h0, cond_of_tile, cond_block, use_pos, counts0):
        n_seq, seq_len, _ = x.shape
        xf = x.reshape(n_seq * seq_len, D_MODEL)
        tab = pos_tab if use_pos else None
        xr, gg, y_sgu = _premix(x, mod3, cond_block, row(g_pre_mix[l]), w_in_b, row(sgu_g[l]),
                                sgu_w_b, sgu_bias_tile, tab)
        scan_params = (conv_w_t, conv_b_t, w_gates, b_gates, lam)
        hf, hf_last = _scan(xr, None, None, _to_time_major_state(h0[:, 0]), *scan_params, direction=0)
        y_rnn, hb_first = _scan(xr, gg, hf, _to_time_major_state(h0[:, 1]), *scan_params, direction=1)
        fstate = jnp.stack([hf_last.reshape(n_seq, D_RNN), hb_first.reshape(n_seq, D_RNN)], axis=1)
        x1, hn, bucket, rank, counts = _postmix(
            xf, y_rnn.reshape(n_seq * seq_len, D_RNN), y_sgu.reshape(n_seq * seq_len, D_SGU), mod3,
            cond_of_tile, row(g_post_mix[l]), row(g_pre_ffn[l]), w_out_b, router_wt, router_bt, earlier, tab,
            counts0)
        return x1, hn, bucket, rank, counts, fstate

    ctx_cond = lambda i: n_dec
    dec_cond = lambda i: i // (dec_len // TP)
    h0_ctx = jnp.zeros((n_ctx, 2, D_RNN), F32)
    counts0 = jnp.zeros((ROUTER_ROWS, LANES), F32)
    x1_ctx, hn_ctx, bucket_ctx, rank_ctx, counts, st = mixer(x_prompt, h0_ctx, ctx_cond, lambda p: n_dec // ctx_per_step,
                                                             False, counts0)
    new_state = st.astype(state_rglru.dtype)[:, None]
    x1_dec, hn_dec, bucket_dec, rank_dec, counts, _ = mixer(x_sample, state_rglru[:, l].astype(F32), dec_cond,
                                                            lambda p: p, True, counts)

    dest, sched, n_slots = _schedule(jnp.concatenate([bucket_ctx, bucket_dec]),
                                     jnp.concatenate([rank_ctx, rank_dec]), counts)
    xs = _dispatch(dest, hn_ctx, hn_dec, n_slots)
    ys = _experts(sched, xs, router_w, router_b, exp_w_gate[l], exp_w_up[l], exp_w_down[l])
    y_prompt = _combine(dest, ys, x1_ctx, mod3, row(g_post_ffn[l]), ctx_cond, 0)
    y_sample = _combine(dest, ys, x1_dec, mod3, row(g_post_ffn[l]), lambda i: i // (dec_len // TC), n_ctx_tok // TC)
    return (y_prompt.reshape(x_prompt.shape), y_sample.reshape(x_sample.shape), new_state)
```

```python
import functools
import math

import jax
import jax.numpy as jnp
from jax import lax
from jax.experimental import pallas as pl
from jax.experimental.pallas import tpu as pltpu

D_MODEL = 1024
D_RNN = 512
D_SGU = 512
N_HEADS_RNN = 8
HEAD_RNN = D_RNN // N_HEADS_RNN
N_HEADS_SGU = 8
HEAD_SGU = D_SGU // N_HEADS_SGU
CHUNK = 128
GRID_W = 64
RG_C = 8.0
N_GROUPS = 4
EXPERTS_PER_GROUP = 4
N_EXPERTS = N_GROUPS * EXPERTS_PER_GROUP
D_EXPERT = 512
EPS = 1e-6
POS_BASE = 10000.0

LANES = 128
SUBLANES = 8
CONV_W = 4
CONV_LEFT = 2
PAIR = 2
RNN_BLOCKS = D_RNN // LANES
TMJ_ROWS = PAIR * RNN_BLOCKS
ROUTER_LANES = LANES
E_LANE0 = N_GROUPS
ROUTER_ROWS = 32
E_ROW0 = SUBLANES

PAIRS_PER_GROUP = EXPERTS_PER_GROUP * (EXPERTS_PER_GROUP - 1) // 2
N_BUCKETS = N_GROUPS * PAIRS_PER_GROUP

ROW_TILES = D_MODEL // LANES

MOD_COLS = 2048
TP = 1024
TD = 2048
TC = 512
DMA_UNROLL = 8
EXPERT_ROW_PARTS = 2
TMX = 512
TT = 512
PREMIX_ROWS = 1024
LC = 256
TS = 16
PB = 2
VMEM_LIMIT = 56 * 1024 * 1024

F32 = jnp.float32
BF16 = jnp.bfloat16


def _params(sem):
    return pltpu.CompilerParams(dimension_semantics=sem, vmem_limit_bytes=VMEM_LIMIT)


def _rms(x):
    return x * lax.rsqrt(jnp.mean(x * x, axis=-1, keepdims=True) + EPS)


def _sigmoid(x):
    return 0.5 * jnp.tanh(0.5 * x) + 0.5


def _mod_kernel(cond_ref, w_ref, b_ref, o_ref):
    c = cond_ref[...]
    s = c * _sigmoid(c)
    o_ref[...] = jnp.dot(s.astype(BF16), w_ref[...].astype(BF16),
                         preferred_element_type=F32) + b_ref[...]


def _modulation(cond, w_mod, b_mod):
    n = w_mod.shape[1]
    return pl.pallas_call(
        _mod_kernel,
        out_shape=jax.ShapeDtypeStruct((cond.shape[0], n), F32),
        grid=(n // MOD_COLS,),
        in_specs=[pl.BlockSpec(cond.shape, lambda j: (0, 0)),
                  pl.BlockSpec((D_MODEL, MOD_COLS), lambda j: (0, j)),
                  pl.BlockSpec((1, MOD_COLS), lambda j: (0, j))],
        out_specs=pl.BlockSpec((cond.shape[0], MOD_COLS), lambda j: (0, j)),
        compiler_params=_params(("arbitrary",)),
        name="modulation",
    )(cond, w_mod, b_mod.reshape(1, n))


def _pos_kernel(o_ref):
    n_freq = D_MODEL // 4
    k = lax.broadcasted_iota(jnp.int32, (GRID_W, n_freq), 1).astype(F32)
    p = lax.broadcasted_iota(jnp.int32, (GRID_W, n_freq), 0).astype(F32)
    freq = jnp.exp(-math.log(POS_BASE) * k / n_freq)
    ang = p * freq
    o_ref[:, 0:n_freq] = jnp.sin(ang)
    o_ref[:, n_freq:2 * n_freq] = jnp.cos(ang)


def _pos_table():
    return pl.pallas_call(
        _pos_kernel,
        out_shape=jax.ShapeDtypeStruct((GRID_W, D_MODEL // 2), F32),
        name="pos_table",
    )()


def _add_pos(x, pos_refs, q0):
    if pos_refs is None:
        return x
    rows_ref, cols_ref = pos_refs
    reps = x.shape[0] // GRID_W
    rpart = jnp.concatenate(
        [jnp.broadcast_to(rows_ref[q:q + 1, :], (GRID_W, D_MODEL // 2)) for q in range(q0, q0 + reps)], axis=0)
    cpart = jnp.concatenate([cols_ref[...]] * reps, axis=0)
    return jnp.concatenate([x[:, :D_MODEL // 2] + rpart, x[:, D_MODEL // 2:] + cpart], axis=1)


def _load_x(x_ref, pos_refs, r0, n):
    return _add_pos(x_ref[r0:r0 + n, :], pos_refs, r0 // GRID_W)


def _premix_kernel(*refs, add_pos):
    refs = list(refs)
    x_ref = refs.pop(0)
    nseq, tt = x_ref.shape[:2]
    pos_refs = (refs.pop(0), refs.pop(0)) if add_pos else None
    mod_ref, g_ref, win_ref, sgug_ref, sguw_ref, sgub_ref, xr_ref, gg_ref, ys_ref = refs
    hn = []
    for s in range(nseq):
        shift = mod_ref[s, :, 0:D_MODEL]
        scale = mod_ref[s, :, D_MODEL:2 * D_MODEL]
        hn.append(_rms(_add_pos(x_ref[s], pos_refs, 0)) * (g_ref[...] * (1.0 + scale)) + shift)
    z = jnp.dot(jnp.concatenate(hn, axis=0).astype(BF16), win_ref[...],
                preferred_element_type=F32)
    half = D_SGU // 2
    heads_per_half = N_HEADS_SGU // 2
    lane_head = lax.broadcasted_iota(jnp.int32, (CHUNK, half), 1) // HEAD_SGU
    for s in range(nseq):
        zs = z[s * tt:(s + 1) * tt]
        vn = (_rms(zs[:, 0:D_SGU]) * sgug_ref[...]).astype(BF16)
        u = zs[:, D_SGU:2 * D_SGU]
        for c in range(tt // CHUNK):
            rows = slice(c * CHUNK, (c + 1) * CHUNK)
            halves = []
            for hf in range(2):
                r = jnp.dot(sguw_ref[hf], vn[rows, hf * half:(hf + 1) * half],
                            preferred_element_type=F32)
                sel = jnp.zeros((CHUNK, half), F32)
                for h in range(heads_per_half):
                    sel = jnp.where(lane_head == h, r[h * CHUNK:(h + 1) * CHUNK], sel)
                halves.append(sel)
            gatev = jnp.concatenate(halves, axis=1) + sgub_ref[...]
            ys_ref[s, rows, :] = (u[rows] * gatev).astype(BF16)
    for s in range(nseq):
        zs = z[s * tt:(s + 1) * tt]
        gg = jax.nn.gelu(zs[:, 2 * D_SGU:2 * D_SGU + D_RNN])
        xr = zs[:, 2 * D_SGU + D_RNN:]
        for k in range(RNN_BLOCKS):
            first = (s // PAIR) * tt * TMJ_ROWS + (s % PAIR) * RNN_BLOCKS + k
            rows = pl.ds(first, tt, stride=TMJ_ROWS)
            xr_ref[rows, :] = xr[:, k * LANES:(k + 1) * LANES]
            gg_ref[rows, :] = gg[:, k * LANES:(k + 1) * LANES]


def _premix(x, mod3, cond_block, g_pre, w_in_b, sgu_g, sgu_w_b, sgu_bias_tile, pos_tab):
    n_seq, seq_len, _ = x.shape
    tt = min(seq_len, TT)
    nseq = max(PAIR, PREMIX_ROWS // tt)
    assert nseq == PAIR or tt == seq_len
    n_pairs, n_tiles = n_seq // PAIR, seq_len // tt
    add_pos = pos_tab is not None
    const2 = lambda p, j: (0, 0)
    in_specs = [pl.BlockSpec((nseq, tt, D_MODEL), lambda p, j: (p, j, 0))]
    args = [x]
    if add_pos:
        reps = tt // GRID_W
        in_specs += [pl.BlockSpec((None, reps, D_MODEL // 2), lambda p, j: (j, 0, 0)),
                     pl.BlockSpec((GRID_W, D_MODEL // 2), const2)]
        args += [pos_tab.reshape(GRID_W // reps, reps, D_MODEL // 2), pos_tab]
    in_specs += [pl.BlockSpec((nseq, 1, 6 * D_MODEL), lambda p, j: (cond_block(p), 0, 0)),
                 pl.BlockSpec((1, D_MODEL), const2),
                 pl.BlockSpec((D_MODEL, 2 * D_RNN + 2 * D_SGU), const2),
                 pl.BlockSpec((1, D_SGU), const2),
                 pl.BlockSpec((2, 4 * CHUNK, CHUNK), lambda p, j: (0, 0, 0)),
                 pl.BlockSpec((CHUNK, D_SGU), const2)]
    args += [mod3, g_pre, w_in_b, sgu_g, sgu_w_b, sgu_bias_tile]
    tmj = jax.ShapeDtypeStruct((n_pairs * seq_len * TMJ_ROWS, LANES), F32)
    tmj_spec = pl.BlockSpec((nseq // PAIR * tt * TMJ_ROWS, LANES), lambda p, j: (p * n_tiles + j, 0))
    xr, gg, y_sgu = pl.pallas_call(
        functools.partial(_premix_kernel, add_pos=add_pos),
        out_shape=(tmj, tmj, jax.ShapeDtypeStruct((n_seq, seq_len, D_SGU), BF16)),
        grid=(n_seq // nseq, n_tiles),
        in_specs=in_specs,
        out_specs=(tmj_spec, tmj_spec, pl.BlockSpec((nseq, tt, D_SGU), lambda p, j: (p, j, 0))),
        compiler_params=_params(("parallel", "parallel")),
        name="premix",
    )(*args)
    shape4 = (n_pairs, seq_len, TMJ_ROWS, LANES)
    return xr.reshape(shape4), gg.reshape(shape4), y_sgu


def _scan_kernel(*refs, reverse, n_chunks):
    if reverse:
        (xprev_ref, x_ref, xnext_ref, gg_ref, hf_ref, h0_ref, cw_ref, cb_ref, wg_ref, bg_ref, lam_ref,
         y_ref, fs_ref, xwin, xc_s, r_s, i_s, a_s, b_s, y_s, hcar) = refs
    else:
        (xprev_ref, x_ref, xnext_ref, h0_ref, cw_ref, cb_ref, wg_ref, bg_ref, lam_ref,
         hf_ref, fs_ref, xwin, xc_s, r_s, i_s, a_s, b_s, hcar) = refs
    c = pl.program_id(1)
    chunk = n_chunks - 1 - c if reverse else c
    sub_rows = TS * TMJ_ROWS

    def rows_of(pb, t0, n_steps):
        first = (pb * LC + t0) * TMJ_ROWS
        if not isinstance(first, int):
            first = pl.multiple_of(first, TMJ_ROWS)
        return pl.ds(first, n_steps * TMJ_ROWS)

    @pl.when(c == 0)
    def _():
        hcar[...] = h0_ref[...]

    xwin[:, 0:CONV_LEFT] = jnp.where(chunk > 0, xprev_ref[...], 0.0)
    xwin[:, LC + CONV_LEFT:LC + CONV_W - 1] = jnp.where(chunk < n_chunks - 1, xnext_ref[...], 0.0)

    xwin[:, CONV_LEFT:CONV_LEFT + LC] = x_ref[...]

    neg_lam = -lam_ref[...]
    softplus = jnp.maximum(neg_lam, 0.0) + jnp.log(1.0 + jnp.exp(-jnp.abs(neg_lam)))
    half_decay = (-0.5 * RG_C * math.log2(math.e)) * softplus

    def conv(pb):
        for t0 in range(0, LC, TS):
            xc = cb_ref[...] + cw_ref[0] * xwin[pb, t0:t0 + TS]
            for k in range(1, CONV_W):
                xc = xc + cw_ref[k] * xwin[pb, t0 + k:t0 + k + TS]
            xc_s[rows_of(pb, t0, TS), :] = xc.reshape(sub_rows, LANES)

    def gate_matmuls(pb):
        for k in range(RNN_BLOCKS):
            rows = pl.ds(pb * LC * TMJ_ROWS + k, LC * PAIR, stride=RNN_BLOCKS)
            g = jnp.dot(xc_s[rows, :].astype(BF16), wg_ref[k], preferred_element_type=F32)
            r_s[rows, :] = g[:, :LANES]
            i_s[rows, :] = g[:, LANES:]

    def gates(pb):
        for t0 in range(0, LC, TS):
            rows = rows_of(pb, t0, TS)
            tile = lambda ref: ref[rows, :].reshape(TS, TMJ_ROWS, LANES)
            tr = jnp.tanh(tile(r_s) + bg_ref[0])
            ti = jnp.tanh(tile(i_s) + bg_ref[1])
            log2_a = tr * half_decay + half_decay
            a = jnp.exp2(log2_a)
            q = jnp.tanh(log2_a * (-math.log(2.0))) * (a * a + 1.0)
            b = jnp.where(q > 0.0, q * lax.rsqrt(q), 0.0) * ((ti + 1.0) * tile(xc_s))
            a_s[rows, :] = a.reshape(sub_rows, LANES)
            b_s[rows, :] = b.reshape(sub_rows, LANES)

    for stage in (conv, gate_matmuls, gates):
        for pb in range(PB):
            stage(pb)

    def step(j, hs):
        t = LC - 1 - j if reverse else j
        out = []
        for pb in range(PB):
            rows = rows_of(pb, t, 1)
            h = a_s[rows, :] * hs[pb] + b_s[rows, :]
            if reverse:
                y_s[rows, :] = h
            else:
                hf_ref[pb, t] = h
            out.append(h)
        return tuple(out)

    hs = lax.fori_loop(0, LC, step, tuple(hcar[pb] for pb in range(PB)), unroll=8)
    for pb in range(PB):
        hcar[pb] = hs[pb]
        fs_ref[pb] = hs[pb]

    if reverse:
        for pb in range(PB):
            for t0 in range(0, LC, TS):
                rows = rows_of(pb, t0, TS)
                both = hf_ref[pb, t0:t0 + TS].reshape(sub_rows, LANES) + y_s[rows, :]
                y_s[rows, :] = both * gg_ref[pb, t0:t0 + TS].reshape(sub_rows, LANES)
            for s in range(PAIR):
                cols = [y_s[pl.ds(pb * LC * TMJ_ROWS + s * RNN_BLOCKS + k, LC, stride=TMJ_ROWS), :]
                        for k in range(RNN_BLOCKS)]
                y_ref[pb * PAIR + s] = jnp.concatenate(cols, axis=1).astype(BF16)


def _scan(xr, gg, hf, h0, conv_w, conv_b, w_gates, b_gates, lam, direction):
    n_pairs, seq_len = xr.shape[:2]
    n_chunks = seq_len // LC
    reverse = direction == 1
    pos = (lambda c: n_chunks - 1 - c) if reverse else (lambda c: c)
    tmj_blk = pl.BlockSpec((PB, LC, TMJ_ROWS, LANES), lambda i, c: (i, pos(c), 0, 0))
    state_blk = pl.BlockSpec((PB, TMJ_ROWS, LANES), lambda i, c: (i, 0, 0))
    per_dir = lambda *shape: pl.BlockSpec((None,) + shape, lambda i, c: (direction,) + (0,) * len(shape))
    in_specs = [
        pl.BlockSpec((PB, CONV_LEFT, TMJ_ROWS, LANES),
                     lambda i, c: (i, jnp.maximum(pos(c) * (LC // CONV_LEFT) - 1, 0), 0, 0)),
        tmj_blk,
        pl.BlockSpec((PB, 1, TMJ_ROWS, LANES), lambda i, c: (i, jnp.minimum((pos(c) + 1) * LC, seq_len - 1), 0, 0)),
    ]
    args = [xr, xr, xr]
    if reverse:
        in_specs += [tmj_blk, tmj_blk]
        args += [gg, hf]
    in_specs += [state_blk,
                 pl.BlockSpec((CONV_W, TMJ_ROWS, LANES), lambda i, c: (0, 0, 0)),
                 pl.BlockSpec((TMJ_ROWS, LANES), lambda i, c: (0, 0)),
                 per_dir(RNN_BLOCKS, LANES, 2 * LANES),
                 per_dir(2, TMJ_ROWS, LANES),
                 per_dir(TMJ_ROWS, LANES)]
    args += [h0, conv_w, conv_b, w_gates, b_gates, lam]
    flat = pltpu.VMEM((PB * LC * TMJ_ROWS, LANES), F32)
    scratch = [pltpu.VMEM((PB, LC + CONV_W - 1, TMJ_ROWS, LANES), F32)] + [flat] * (6 if reverse else 5)
    scratch += [pltpu.VMEM((PB, TMJ_ROWS, LANES), F32)]
    state = jax.ShapeDtypeStruct((n_pairs, TMJ_ROWS, LANES), F32)
    if reverse:
        out_shape = (jax.ShapeDtypeStruct((n_pairs * PAIR, seq_len, D_RNN), BF16), state)
        out_specs = (pl.BlockSpec((PB * PAIR, LC, D_RNN), lambda i, c: (i, pos(c), 0)), state_blk)
    else:
        out_shape = (jax.ShapeDtypeStruct(xr.shape, F32), state)
        out_specs = (tmj_blk, state_blk)
    return pl.pallas_call(
        functools.partial(_scan_kernel, reverse=reverse, n_chunks=n_chunks),
        out_shape=out_shape,
        grid=(n_pairs // PB, n_chunks),
        in_specs=in_specs,
        out_specs=out_specs,
        scratch_shapes=scratch,
        compiler_params=_params(("parallel", "arbitrary")),
        name="scan_bwd" if reverse else "scan_fwd",
    )(*args)


def _route(lt):
    n = lt.shape[1]
    row = lax.broadcasted_iota(jnp.int32, (EXPERTS_PER_GROUP, n), 0)
    neg = jnp.float32(-jnp.inf)

    def arg_max(v):
        m = jnp.max(v, axis=0, keepdims=True)
        return jnp.min(jnp.where(v == m, row, EXPERTS_PER_GROUP), axis=0, keepdims=True)

    g_idx = arg_max(lt[0:N_GROUPS])
    el = lt[E_ROW0:E_ROW0 + EXPERTS_PER_GROUP]
    for g in range(1, N_GROUPS):
        first = E_ROW0 + g * EXPERTS_PER_GROUP
        el = jnp.where(g_idx == g, lt[first:first + EXPERTS_PER_GROUP], el)
    i1 = arg_max(el)
    i2 = arg_max(jnp.where(row == i1, neg, el))
    ja = jnp.minimum(i1, i2)
    jb = jnp.maximum(i1, i2)
    pair = (ja * (2 * EXPERTS_PER_GROUP - 1 - ja)) // 2 + (jb - ja - 1)
    return g_idx * PAIRS_PER_GROUP + pair


def _store_token_major(ref, x, t0=0):
    n = x.shape[0]
    for k in range(ROW_TILES):
        ref[pl.ds(t0 * ROW_TILES + k, n, stride=ROW_TILES), :] = x[:, k * LANES:(k + 1) * LANES]


def _load_token_major(ref, n):
    return jnp.concatenate([ref[pl.ds(k, n, stride=ROW_TILES), :] for k in range(ROW_TILES)], axis=1)


def _postmix_kernel(*refs, add_pos):
    refs = list(refs)
    x_ref = refs.pop(0)
    pos_refs = (refs.pop(0), refs.pop(0)) if add_pos else None
    (yr_ref, ys_ref, mod_ref, gpost_ref, gpre_ref, wout_ref, rw_ref, rb_ref, earlier_ref, cnt0_ref,
     x1_ref, hn_ref, rt_ref, cnt_ref, run_ref) = refs

    @pl.when(pl.program_id(0) == 0)
    def _():
        run_ref[...] = cnt0_ref[...]

    gate1 = mod_ref[0, :, 2 * D_MODEL:3 * D_MODEL]
    shift2 = mod_ref[0, :, 3 * D_MODEL:4 * D_MODEL]
    scale2 = mod_ref[0, :, 4 * D_MODEL:5 * D_MODEL]
    y = (jnp.dot(yr_ref[...], wout_ref[0:D_RNN, :], preferred_element_type=F32)
         + jnp.dot(ys_ref[...], wout_ref[D_RNN:, :], preferred_element_type=F32))
    x1 = _load_x(x_ref, pos_refs, 0, TP) + _rms(y) * (gate1 * gpost_ref[...])
    x1_ref[...] = x1
    hn = _rms(x1) * (gpre_ref[...] * (1.0 + scale2)) + shift2
    _store_token_major(hn_ref, hn)
    lt = lax.dot_general(rw_ref[...], hn.astype(BF16), (((1,), (1,)), ((), ())),
                         preferred_element_type=F32) + rb_ref[:, 0:1]
    bucket = _route(lt)
    onehot = lax.broadcasted_iota(jnp.int32, (ROUTER_ROWS, TP), 0) == bucket
    before = jnp.dot(onehot.astype(BF16), earlier_ref[...], preferred_element_type=F32) + run_ref[:, 0:1]
    rank = jnp.sum(jnp.where(onehot, before, 0.0), axis=0, keepdims=True).astype(jnp.int32)
    row = lax.broadcasted_iota(jnp.int32, (SUBLANES, TP), 0)
    rt_ref[...] = jnp.where(row == 0, bucket, jnp.where(row == 1, rank, 0))
    run_ref[...] += jnp.sum(onehot.astype(F32), axis=1, keepdims=True)
    cnt_ref[...] = run_ref[...]


def _postmix(x, y_rnn, y_sgu, mod3, cond_of_tile, g_post, g_pre, w_out_b, router_wt, router_bt, earlier, pos_tab,
             counts0):
    n_tok = x.shape[0]
    n_tiles = n_tok // TP
    add_pos = pos_tab is not None
    tok = lambda i: (i, 0)
    const2 = lambda i: (0, 0)
    in_specs = [pl.BlockSpec((TP, D_MODEL), tok)]
    args = [x]
    if add_pos:
        reps = TP // GRID_W
        tiles_per_seq = GRID_W // reps
        in_specs += [pl.BlockSpec((None, reps, D_MODEL // 2), lambda i: (i % tiles_per_seq, 0, 0)),
                     pl.BlockSpec((GRID_W, D_MODEL // 2), const2)]
        args += [pos_tab.reshape(tiles_per_seq, reps, D_MODEL // 2), pos_tab]
    in_specs += [pl.BlockSpec((TP, D_RNN), tok),
                 pl.BlockSpec((TP, D_SGU), tok),
                 pl.BlockSpec((1, 1, 6 * D_MODEL), lambda i: (cond_of_tile(i), 0, 0)),
                 pl.BlockSpec((1, D_MODEL), const2),
                 pl.BlockSpec((1, D_MODEL), const2),
                 pl.BlockSpec((D_MODEL, D_MODEL), const2),
                 pl.BlockSpec((ROUTER_ROWS, D_MODEL), const2),
                 pl.BlockSpec((ROUTER_ROWS, LANES), const2),
                 pl.BlockSpec((TP, TP), const2),
                 pl.BlockSpec((ROUTER_ROWS, LANES), const2)]
    args += [y_rnn, y_sgu, mod3, g_post, g_pre, w_out_b, router_wt, router_bt, earlier, counts0]
    counts_spec = pl.BlockSpec((ROUTER_ROWS, LANES), const2)
    x1, hn, route, counts = pl.pallas_call(
        functools.partial(_postmix_kernel, add_pos=add_pos),
        out_shape=(jax.ShapeDtypeStruct((n_tok, D_MODEL), F32),
                   jax.ShapeDtypeStruct((n_tok * ROW_TILES, LANES), F32),
                   jax.ShapeDtypeStruct((n_tiles * SUBLANES, TP), jnp.int32),
                   jax.ShapeDtypeStruct((ROUTER_ROWS, LANES), F32)),
        grid=(n_tiles,),
        in_specs=in_specs,
        out_specs=(pl.BlockSpec((TP, D_MODEL), tok),
                   pl.BlockSpec((TP * ROW_TILES, LANES), tok),
                   pl.BlockSpec((SUBLANES, TP), tok),
                   counts_spec),
        scratch_shapes=[pltpu.VMEM((ROUTER_ROWS, LANES), F32)],
        compiler_params=_params(("arbitrary",)),
        name="postmix",
    )(*args)
    route = route.reshape(n_tiles, SUBLANES, TP)
    return x1, hn, route[:, 0].reshape(n_tok), route[:, 1].reshape(n_tok), counts


def _token_rows(ref, t):
    return ref.at[pl.ds(pl.multiple_of(t * ROW_TILES, ROW_TILES), ROW_TILES), :]


def _dispatch_kernel(dest_ref, hc_ref, hs_ref, xs_ref, sem, *, n_ctx_steps):
    i = pl.program_id(0)
    base = i * TD

    def scatter(src_ref):
        def start(g, carry):
            for u in range(DMA_UNROLL):
                r = g * DMA_UNROLL + u
                pltpu.make_async_copy(_token_rows(src_ref, r), _token_rows(xs_ref, dest_ref[base + r]),
                                      sem).start(priority=u % 2)
            return carry

        lax.fori_loop(0, TD // DMA_UNROLL, start, 0)
        pltpu.make_async_copy(src_ref, xs_ref.at[pl.ds(0, TD * ROW_TILES), :], sem).wait()

    @pl.when(i < n_ctx_steps)
    def _():
        scatter(hc_ref)

    @pl.when(i >= n_ctx_steps)
    def _():
        scatter(hs_ref)


def _dispatch(dest, hn_ctx, hn_dec, n_slots):
    n_ctx_steps = hn_ctx.shape[0] // (TD * ROW_TILES)
    n_dec_steps = hn_dec.shape[0] // (TD * ROW_TILES)
    return pl.pallas_call(
        functools.partial(_dispatch_kernel, n_ctx_steps=n_ctx_steps),
        out_shape=jax.ShapeDtypeStruct((n_slots * ROW_TILES, LANES), F32),
        grid_spec=pltpu.PrefetchScalarGridSpec(
            num_scalar_prefetch=1,
            grid=(n_ctx_steps + n_dec_steps,),
            in_specs=[pl.BlockSpec((TD * ROW_TILES, LANES), lambda i, d: (jnp.minimum(i, n_ctx_steps - 1), 0)),
                      pl.BlockSpec((TD * ROW_TILES, LANES), lambda i, d: (jnp.maximum(i - n_ctx_steps, 0), 0))],
            out_specs=pl.BlockSpec(memory_space=pl.ANY),
            scratch_shapes=[pltpu.SemaphoreType.DMA(())]),
        compiler_params=_params(("arbitrary",)),
        name="dispatch",
    )(dest, hn_ctx, hn_dec)


def _experts_kernel(ea_ref, eb_ref, nv_ref, blk_ref, xs_ref, rw_ref, rb_ref, *refs):
    w32_refs, ys_ref, w_refs = refs[:6], refs[6], refs[7:]
    wga_ref, wua_ref, wda_ref, wgb_ref, wub_ref, wdb_ref = w_refs
    i = pl.program_id(0)
    nv = nv_ref[i]
    prev = jnp.maximum(i - 1, 0)

    @pl.when((i == 0) | (ea_ref[i] != ea_ref[prev]) | (eb_ref[i] != eb_ref[prev]))
    def _():
        for w32_ref, w_ref in zip(w32_refs, w_refs):
            w_ref[...] = w32_ref[0].astype(BF16)

    part = TMX // EXPERT_ROW_PARTS

    def process(n_parts):
        n = n_parts * part
        row = lax.broadcasted_iota(jnp.int32, (n, 1), 0)
        xb = jnp.where(row < nv, _load_token_major(xs_ref, n), 0.0).astype(BF16)
        logits = jnp.dot(xb, rw_ref[...], preferred_element_type=F32) + rb_ref[...]
        lane = lax.broadcasted_iota(jnp.int32, logits.shape, 1)
        ea = ea_ref[i]
        eb = eb_ref[i]
        gmask = lane < N_GROUPS
        gl = jnp.where(gmask, logits, -jnp.inf)
        gmax = jnp.max(gl, axis=-1, keepdims=True)
        gexp = jnp.where(gmask, jnp.exp(gl - gmax), 0.0)
        g_own = jnp.sum(jnp.where(lane == ea // EXPERTS_PER_GROUP, gexp, 0.0), axis=-1, keepdims=True)
        g_w = g_own / jnp.sum(gexp, axis=-1, keepdims=True)
        la = jnp.sum(jnp.where(lane == ea + E_LANE0, logits, 0.0), axis=-1, keepdims=True)
        lb = jnp.sum(jnp.where(lane == eb + E_LANE0, logits, 0.0), axis=-1, keepdims=True)
        m = jnp.maximum(la, lb)
        pa = jnp.exp(la - m)
        pb = jnp.exp(lb - m)
        inv = g_w / (pa + pb)

        def hidden(x, wg_ref, wu_ref, w):
            g = jnp.dot(x, wg_ref[...], preferred_element_type=F32)
            u = jnp.dot(x, wu_ref[...], preferred_element_type=F32)
            return ((g * _sigmoid(g)) * u * w).astype(BF16)

        ys = []
        for h in range(n_parts):
            rows = slice(h * part, (h + 1) * part)
            act_a = hidden(xb[rows], wga_ref, wua_ref, (pa * inv)[rows])
            act_b = hidden(xb[rows], wgb_ref, wub_ref, (pb * inv)[rows])
            y = (jnp.dot(act_a, wda_ref[...], preferred_element_type=F32)
                 + jnp.dot(act_b, wdb_ref[...], preferred_element_type=F32))
            ys.append(y)
        _store_token_major(ys_ref, jnp.concatenate(ys, axis=0))

    for n_parts in range(1, EXPERT_ROW_PARTS + 1):
        pl.when((nv > (n_parts - 1) * part) & (nv <= n_parts * part))(functools.partial(process, n_parts))


def _experts(sched, xs, router_w, router_b, wg, wu, wd):
    ea, eb, nv, blk = sched
    n_tiles = ea.shape[0]
    rows = lambda i, ea, eb, nv, blk: (blk[i], 0)
    const2 = lambda i, ea, eb, nv, blk: (0, 0)
    exp_a = lambda i, ea, eb, nv, blk: (ea[i], 0, 0)
    exp_b = lambda i, ea, eb, nv, blk: (eb[i], 0, 0)
    w_in_spec = lambda m: pl.BlockSpec((1, D_MODEL, D_EXPERT), m)
    w_out_spec = lambda m: pl.BlockSpec((1, D_EXPERT, D_MODEL), m)
    return pl.pallas_call(
        _experts_kernel,
        out_shape=jax.ShapeDtypeStruct(xs.shape, F32),
        grid_spec=pltpu.PrefetchScalarGridSpec(
            num_scalar_prefetch=4,
            grid=(n_tiles,),
            in_specs=[pl.BlockSpec((TMX * ROW_TILES, LANES), rows),
                      pl.BlockSpec((D_MODEL, ROUTER_LANES), const2),
                      pl.BlockSpec((1, ROUTER_LANES), const2),
                      w_in_spec(exp_a), w_in_spec(exp_a), w_out_spec(exp_a),
                      w_in_spec(exp_b), w_in_spec(exp_b), w_out_spec(exp_b)],
            out_specs=pl.BlockSpec((TMX * ROW_TILES, LANES), rows),
            scratch_shapes=[pltpu.VMEM((D_MODEL, D_EXPERT), BF16), pltpu.VMEM((D_MODEL, D_EXPERT), BF16),
                            pltpu.VMEM((D_EXPERT, D_MODEL), BF16)] * 2),
        compiler_params=_params(("arbitrary",)),
        name="experts",
    )(ea, eb, nv, blk, xs, router_w, router_b, wg, wu, wd, wg, wu, wd)


def _combine_kernel(dest_ref, ys_ref, x1_ref, mod_ref, gpost_ref, o_ref, ybuf, sems, *, tile0):
    i = pl.program_id(0)
    n = pl.num_programs(0)

    def fetch(tile, slot):
        base = (tile + tile0) * TC

        def start(g, carry):
            for u in range(DMA_UNROLL):
                r = g * DMA_UNROLL + u
                pltpu.make_async_copy(_token_rows(ys_ref, dest_ref[base + r]), _token_rows(ybuf.at[slot], r),
                                      sems.at[slot]).start(priority=u % 2)
            return carry

        lax.fori_loop(0, TC // DMA_UNROLL, start, 0)

    @pl.when(i == 0)
    def _():
        fetch(0, 0)

    @pl.when(i + 1 < n)
    def _():
        fetch(i + 1, (i + 1) % 2)

    slot = i % 2
    pltpu.make_async_copy(ys_ref.at[pl.ds(0, TC * ROW_TILES), :], ybuf.at[slot], sems.at[slot]).wait()
    gate2 = mod_ref[0, :, 5 * D_MODEL:6 * D_MODEL]
    o_ref[...] = x1_ref[...] + _rms(_load_token_major(ybuf.at[slot], TC)) * (gate2 * gpost_ref[...])


def _combine(dest, ys, x1, mod3, g_post, cond_of_tile, tile0):
    n_tok = x1.shape[0]
    return pl.pallas_call(
        functools.partial(_combine_kernel, tile0=tile0),
        out_shape=jax.ShapeDtypeStruct((n_tok, D_MODEL), F32),
        grid_spec=pltpu.PrefetchScalarGridSpec(
            num_scalar_prefetch=1,
            grid=(n_tok // TC,),
            in_specs=[pl.BlockSpec(memory_space=pl.ANY),
                      pl.BlockSpec((TC, D_MODEL), lambda i, d: (i, 0)),
                      pl.BlockSpec((1, 1, 6 * D_MODEL), lambda i, d: (cond_of_tile(i), 0, 0)),
                      pl.BlockSpec((1, D_MODEL), lambda i, d: (0, 0))],
            out_specs=pl.BlockSpec((TC, D_MODEL), lambda i, d: (i, 0)),
            scratch_shapes=[pltpu.VMEM((2, TC * ROW_TILES, LANES), F32), pltpu.SemaphoreType.DMA((2,))]),
        compiler_params=_params(("arbitrary",)),
        name="combine",
    )(dest, ys, x1, mod3, g_post)


def _schedule(bucket, rank, counts):
    n_tok = bucket.shape[0]
    n_max = n_tok // TMX + N_BUCKETS
    cnt = counts[:N_BUCKETS, 0].astype(jnp.int32)
    tiles = (cnt + TMX - 1) // TMX
    tile_end = jnp.cumsum(tiles)
    tile_start = tile_end - tiles
    ids = jnp.arange(N_BUCKETS, dtype=jnp.int32)
    slot0 = jnp.sum(jnp.where(bucket[:, None] == ids[None, :], (tile_start * TMX)[None, :], 0), axis=1)
    dest = slot0 + rank
    i = jnp.arange(n_max, dtype=jnp.int32)
    total = tile_end[-1]
    valid = i < total
    tb = jnp.sum((jnp.minimum(i, total - 1)[:, None] >= tile_end[None, :]).astype(jnp.int32), axis=1)
    pairs = [(a, b) for a in range(EXPERTS_PER_GROUP) for b in range(a + 1, EXPERTS_PER_GROUP)]
    ea_tab = jnp.array([g * EXPERTS_PER_GROUP + a for g in range(N_GROUPS) for a, _ in pairs], jnp.int32)
    eb_tab = jnp.array([g * EXPERTS_PER_GROUP + b for g in range(N_GROUPS) for _, b in pairs], jnp.int32)
    hit = tb[:, None] == ids[None, :]
    look = lambda tab: jnp.sum(jnp.where(hit, tab[None, :], 0), axis=1)
    ea, eb = look(ea_tab), look(eb_tab)
    nv = jnp.where(valid, jnp.clip(look(cnt) - (i - look(tile_start)) * TMX, 0, TMX), 0)
    blk = jnp.minimum(i, total - 1)
    return dest, (ea, eb, nv, blk), n_max * TMX


def _block_diag_gates(rg_wa, rg_wx):
    heads = LANES // HEAD_RNN

    def bd(w):
        w = w.reshape(2, RNN_BLOCKS, heads, HEAD_RNN, HEAD_RNN)
        eye = jnp.eye(heads, dtype=w.dtype)
        full = jnp.einsum('dghij,hk->dghikj', w, eye)
        return full.reshape(2, RNN_BLOCKS, LANES, LANES)

    return jnp.concatenate([bd(rg_wa), bd(rg_wx)], axis=-1).astype(BF16)


def _row_tile(v):
    blocks = v.reshape(v.shape[:-1] + (RNN_BLOCKS, LANES))
    return jnp.concatenate([blocks] * PAIR, axis=-2)


def _to_time_major_state(h):
    return h.reshape(h.shape[0] // PAIR, TMJ_ROWS, LANES)


def kernel(x_prompt, x_sample, state_rglru, c, c_ctx, w_mod, b_mod, g_pre_mix, g_post_mix, g_pre_ffn,
           g_post_ffn, w_in, conv_w, conv_b, rg_wa, rg_ba, rg_wx, rg_bx, rg_lambda, sgu_g, sgu_w, sgu_b,
           w_out, router_g_w, router_g_b, router_e_w, router_e_b, exp_w_gate, exp_w_up, exp_w_down):
    assert w_mod.shape[0] == 1, "single-layer trunk"
    n_ctx, ctx_len, _ = x_prompt.shape
    n_dec, dec_len, _ = x_sample.shape
    l = 0

    n_cond = SUBLANES
    ctx_rows = n_cond - n_dec
    ctx_per_step = max(PAIR, PREMIX_ROWS // min(ctx_len, TT))
    assert n_dec % PAIR == 0 and ctx_rows == ctx_per_step and n_dec % ctx_per_step == 0
    cond = jnp.concatenate([c, jnp.broadcast_to(c_ctx, (ctx_rows, D_MODEL))], axis=0)
    mod3 = _modulation(cond, w_mod[l], b_mod[l]).reshape(n_cond, 1, 6 * D_MODEL)
    pos_tab = _pos_table()

    w_rx, w_gate, w_u, w_v = jnp.split(w_in[l], [D_RNN, 2 * D_RNN, 2 * D_RNN + D_SGU], axis=1)
    w_in_b = jnp.concatenate([w_v, w_u, w_gate, w_rx], axis=1).astype(BF16)
    w_out_b = w_out[l].astype(BF16)
    sgu_w_b = sgu_w[l].reshape(2, 4 * CHUNK, CHUNK).astype(BF16)
    sgu_bias_tile = jnp.repeat(sgu_b[l].T, HEAD_SGU, axis=1)
    w_gates = _block_diag_gates(rg_wa[l], rg_wx[l])
    b_gates = 0.5 * jnp.stack([_row_tile(rg_ba[l]), _row_tile(rg_bx[l])], axis=1)
    lam = _row_tile(rg_lambda[l])
    conv_w_t = 0.5 * _row_tile(conv_w[l])
    conv_b_t = 0.5 * _row_tile(conv_b[l])
    lane_pad = ROUTER_LANES - E_LANE0 - N_EXPERTS
    router_w = jnp.pad(jnp.concatenate([router_g_w[l], router_e_w[l]], axis=1), ((0, 0), (0, lane_pad))).astype(BF16)
    router_b = jnp.pad(jnp.concatenate([router_g_b[l], router_e_b[l]]), (0, lane_pad)).reshape(1, ROUTER_LANES)
    gap, tail = E_ROW0 - N_GROUPS, ROUTER_ROWS - E_ROW0 - N_EXPERTS
    router_wt = jnp.concatenate([router_g_w[l].T, jnp.zeros((gap, D_MODEL), F32), router_e_w[l].T,
                                 jnp.zeros((tail, D_MODEL), F32)], axis=0).astype(BF16)
    router_bt = jnp.concatenate([router_g_b[l], jnp.zeros((gap,), F32), router_e_b[l], jnp.zeros((tail,), F32)])
    router_bt = jnp.broadcast_to(router_bt[:, None], (ROUTER_ROWS, LANES))
    earlier = jnp.triu(jnp.ones((TP, TP), BF16), k=1)
    row = lambda v: v.reshape(1, -1)

    n_ctx_tok = n_ctx * ctx_len

    def mixer(x, h0, cond_of_tile, cond_block, use_pos, counts0):
        n_seq, seq_len, _ = x.shape
        xf = x.reshape(n_seq * seq_len, D_MODEL)
        tab = pos_tab if use_pos else None
        xr, gg, y_sgu = _premix(x, mod3, cond_block, row(g_pre_mix[l]), w_in_b, row(sgu_g[l]),
                                sgu_w_b, sgu_bias_tile, tab)
        scan_params = (conv_w_t, conv_b_t, w_gates, b_gates, lam)
        hf, hf_last = _scan(xr, None, None, _to_time_major_state(h0[:, 0]), *scan_params, direction=0)
        y_rnn, hb_first = _scan(xr, gg, hf, _to_time_major_state(h0[:, 1]), *scan_params, direction=1)
        fstate = jnp.stack([hf_last.reshape(n_seq, D_RNN), hb_first.reshape(n_seq, D_RNN)], axis=1)
        x1, hn, bucket, rank, counts = _postmix(
            xf, y_rnn.reshape(n_seq * seq_len, D_RNN), y_sgu.reshape(n_seq * seq_len, D_SGU), mod3,
            cond_of_tile, row(g_post_mix[l]), row(g_pre_ffn[l]), w_out_b, router_wt, router_bt, earlier, tab,
            counts0)
        return x1, hn, bucket, rank, counts, fstate

    ctx_cond = lambda i: n_dec
    dec_cond = lambda i: i // (dec_len // TP)
    h0_ctx = jnp.zeros((n_ctx, 2, D_RNN), F32)
    counts0 = jnp.zeros((ROUTER_ROWS, LANES), F32)
    x1_ctx, hn_ctx, bucket_ctx, rank_ctx, counts, st = mixer(x_prompt, h0_ctx, ctx_cond, lambda p: n_dec // ctx_per_step,
                                                             False, counts0)
    new_state = st.astype(state_rglru.dtype)[:, None]
    x1_dec, hn_dec, bucket_dec, rank_dec, counts, _ = mixer(x_sample, state_rglru[:, l].astype(F32), dec_cond,
                                                            lambda p: p, True, counts)

    dest, sched, n_slots = _schedule(jnp.concatenate([bucket_ctx, bucket_dec]),
                                     jnp.concatenate([rank_ctx, rank_dec]), counts)
    xs = _dispatch(dest, hn_ctx, hn_dec, n_slots)
    ys = _experts(sched, xs, router_w, router_b, exp_w_gate[l], exp_w_up[l], exp_w_down[l])
    y_prompt = _combine(dest, ys, x1_ctx, mod3, row(g_post_ffn[l]), ctx_cond, 0)
    y_sample = _combine(dest, ys, x1_dec, mod3, row(g_post_ffn[l]), lambda i: i // (dec_len // TC), n_ctx_tok // TC)
    return (y_prompt.reshape(x_prompt.shape), y_sample.reshape(x_sample.shape), new_state)
```

```python
import functools
import math

import jax
import jax.numpy as jnp
from jax import lax
from jax.experimental import pallas as pl
from jax.experimental.pallas import tpu as pltpu

D_MODEL = 1024
D_RNN = 512
D_SGU = 512
N_HEADS_RNN = 8
HEAD_RNN = D_RNN // N_HEADS_RNN
N_HEADS_SGU = 8
HEAD_SGU = D_SGU // N_HEADS_SGU
CHUNK = 128
GRID_W = 64
RG_C = 8.0
N_GROUPS = 4
EXPERTS_PER_GROUP = 4
N_EXPERTS = N_GROUPS * EXPERTS_PER_GROUP
D_EXPERT = 512
EPS = 1e-6
POS_BASE = 10000.0

LANES = 128
SUBLANES = 8
CONV_W = 4
CONV_LEFT = 2
PAIR = 2
RNN_BLOCKS = D_RNN // LANES
TMJ_ROWS = PAIR * RNN_BLOCKS
ROUTER_LANES = LANES
E_LANE0 = N_GROUPS
ROUTER_ROWS = 32
E_ROW0 = SUBLANES

PAIRS_PER_GROUP = EXPERTS_PER_GROUP * (EXPERTS_PER_GROUP - 1) // 2
N_BUCKETS = N_GROUPS * PAIRS_PER_GROUP

ROW_TILES = D_MODEL // LANES

MOD_COLS = 2048
TP = 1024
TD = 2048
TC = 512
DMA_UNROLL = 8
EXPERT_ROW_PARTS = 2
TMX = 512
TT = 512
PREMIX_ROWS = 1024
LC = 256
TS = 16
PB = 2
VMEM_LIMIT = 56 * 1024 * 1024

F32 = jnp.float32
BF16 = jnp.bfloat16


def _params(sem):
    return pltpu.CompilerParams(dimension_semantics=sem, vmem_limit_bytes=VMEM_LIMIT)


def _rms(x):
    return x * lax.rsqrt(jnp.mean(x * x, axis=-1, keepdims=True) + EPS)


def _sigmoid(x):
    return 0.5 * jnp.tanh(0.5 * x) + 0.5


def _mod_kernel(cond_ref, w_ref, b_ref, o_ref):
    c = cond_ref[...]
    s = c * _sigmoid(c)
    o_ref[...] = jnp.dot(s.astype(BF16), w_ref[...].astype(BF16),
                         preferred_element_type=F32) + b_ref[...]


def _modulation(cond, w_mod, b_mod):
    n = w_mod.shape[1]
    return pl.pallas_call(
        _mod_kernel,
        out_shape=jax.ShapeDtypeStruct((cond.shape[0], n), F32),
        grid=(n // MOD_COLS,),
        in_specs=[pl.BlockSpec(cond.shape, lambda j: (0, 0)),
                  pl.BlockSpec((D_MODEL, MOD_COLS), lambda j: (0, j)),
                  pl.BlockSpec((1, MOD_COLS), lambda j: (0, j))],
        out_specs=pl.BlockSpec((cond.shape[0], MOD_COLS), lambda j: (0, j)),
        compiler_params=_params(("arbitrary",)),
        name="modulation",
    )(cond, w_mod, b_mod.reshape(1, n))


def _pos_kernel(o_ref):
    n_freq = D_MODEL // 4
    k = lax.broadcasted_iota(jnp.int32, (GRID_W, n_freq), 1).astype(F32)
    p = lax.broadcasted_iota(jnp.int32, (GRID_W, n_freq), 0).astype(F32)
    freq = jnp.exp(-math.log(POS_BASE) * k / n_freq)
    ang = p * freq
    o_ref[:, 0:n_freq] = jnp.sin(ang)
    o_ref[:, n_freq:2 * n_freq] = jnp.cos(ang)


def _pos_table():
    return pl.pallas_call(
        _pos_kernel,
        out_shape=jax.ShapeDtypeStruct((GRID_W, D_MODEL // 2), F32),
        name="pos_table",
    )()


def _add_pos(x, pos_refs, q0):
    if pos_refs is None:
        return x
    rows_ref, cols_ref = pos_refs
    reps = x.shape[0] // GRID_W
    rpart = jnp.concatenate(
        [jnp.broadcast_to(rows_ref[q:q + 1, :], (GRID_W, D_MODEL // 2)) for q in range(q0, q0 + reps)], axis=0)
    cpart = jnp.concatenate([cols_ref[...]] * reps, axis=0)
    return jnp.concatenate([x[:, :D_MODEL // 2] + rpart, x[:, D_MODEL // 2:] + cpart], axis=1)


def _load_x(x_ref, pos_refs, r0, n):
    return _add_pos(x_ref[r0:r0 + n, :], pos_refs, r0 // GRID_W)


def _premix_kernel(*refs, add_pos):
    refs = list(refs)
    x_ref = refs.pop(0)
    nseq, tt = x_ref.shape[:2]
    pos_refs = (refs.pop(0), refs.pop(0)) if add_pos else None
    mod_ref, g_ref, win_ref, sgug_ref, sguw_ref, sgub_ref, xr_ref, gg_ref, ys_ref = refs
    hn = []
    for s in range(nseq):
        shift = mod_ref[s, :, 0:D_MODEL]
        scale = mod_ref[s, :, D_MODEL:2 * D_MODEL]
        hn.append(_rms(_add_pos(x_ref[s], pos_refs, 0)) * (g_ref[...] * (1.0 + scale)) + shift)
    z = jnp.dot(jnp.concatenate(hn, axis=0).astype(BF16), win_ref[...],
                preferred_element_type=F32)
    half = D_SGU // 2
    heads_per_half = N_HEADS_SGU // 2
    lane_head = lax.broadcasted_iota(jnp.int32, (CHUNK, half), 1) // HEAD_SGU
    for s in range(nseq):
        zs = z[s * tt:(s + 1) * tt]
        vn = (_rms(zs[:, 0:D_SGU]) * sgug_ref[...]).astype(BF16)
        u = zs[:, D_SGU:2 * D_SGU]
        for c in range(tt // CHUNK):
            rows = slice(c * CHUNK, (c + 1) * CHUNK)
            halves = []
            for hf in range(2):
                r = jnp.dot(sguw_ref[hf], vn[rows, hf * half:(hf + 1) * half],
                            preferred_element_type=F32)
                sel = jnp.zeros((CHUNK, half), F32)
                for h in range(heads_per_half):
                    sel = jnp.where(lane_head == h, r[h * CHUNK:(h + 1) * CHUNK], sel)
                halves.append(sel)
            gatev = jnp.concatenate(halves, axis=1) + sgub_ref[...]
            ys_ref[s, rows, :] = (u[rows] * gatev).astype(BF16)
    for s in range(nseq):
        zs = z[s * tt:(s + 1) * tt]
        gg = jax.nn.gelu(zs[:, 2 * D_SGU:2 * D_SGU + D_RNN])
        xr = zs[:, 2 * D_SGU + D_RNN:]
        for k in range(RNN_BLOCKS):
            first = (s // PAIR) * tt * TMJ_ROWS + (s % PAIR) * RNN_BLOCKS + k
            rows = pl.ds(first, tt, stride=TMJ_ROWS)
            xr_ref[rows, :] = xr[:, k * LANES:(k + 1) * LANES]
            gg_ref[rows, :] = gg[:, k * LANES:(k + 1) * LANES]


def _premix(x, mod3, cond_block, g_pre, w_in_b, sgu_g, sgu_w_b, sgu_bias_tile, pos_tab):
    n_seq, seq_len, _ = x.shape
    tt = min(seq_len, TT)
    nseq = max(PAIR, PREMIX_ROWS // tt)
    assert nseq == PAIR or tt == seq_len
    n_pairs, n_tiles = n_seq // PAIR, seq_len // tt
    add_pos = pos_tab is not None
    const2 = lambda p, j: (0, 0)
    in_specs = [pl.BlockSpec((nseq, tt, D_MODEL), lambda p, j: (p, j, 0))]
    args = [x]
    if add_pos:
        reps = tt // GRID_W
        in_specs += [pl.BlockSpec((None, reps, D_MODEL // 2), lambda p, j: (j, 0, 0)),
                     pl.BlockSpec((GRID_W, D_MODEL // 2), const2)]
        args += [pos_tab.reshape(GRID_W // reps, reps, D_MODEL // 2), pos_tab]
    in_specs += [pl.BlockSpec((nseq, 1, 6 * D_MODEL), lambda p, j: (cond_block(p), 0, 0)),
                 pl.BlockSpec((1, D_MODEL), const2),
                 pl.BlockSpec((D_MODEL, 2 * D_RNN + 2 * D_SGU), const2),
                 pl.BlockSpec((1, D_SGU), const2),
                 pl.BlockSpec((2, 4 * CHUNK, CHUNK), lambda p, j: (0, 0, 0)),
                 pl.BlockSpec((CHUNK, D_SGU), const2)]
    args += [mod3, g_pre, w_in_b, sgu_g, sgu_w_b, sgu_bias_tile]
    tmj = jax.ShapeDtypeStruct((n_pairs * seq_len * TMJ_ROWS, LANES), F32)
    tmj_spec = pl.BlockSpec((nseq // PAIR * tt * TMJ_ROWS, LANES), lambda p, j: (p * n_tiles + j, 0))
    xr, gg, y_sgu = pl.pallas_call(
        functools.partial(_premix_kernel, add_pos=add_pos),
        out_shape=(tmj, tmj, jax.ShapeDtypeStruct((n_seq, seq_len, D_SGU), BF16)),
        grid=(n_seq // nseq, n_tiles),
        in_specs=in_specs,
        out_specs=(tmj_spec, tmj_spec, pl.BlockSpec((nseq, tt, D_SGU), lambda p, j: (p, j, 0))),
        compiler_params=_params(("parallel", "parallel")),
        name="premix",
    )(*args)
    shape4 = (n_pairs, seq_len, TMJ_ROWS, LANES)
    return xr.reshape(shape4), gg.reshape(shape4), y_sgu


def _scan_kernel(*refs, reverse, n_chunks):
    if reverse:
        (xprev_ref, x_ref, xnext_ref, gg_ref, hf_ref, h0_ref, cw_ref, cb_ref, wg_ref, bg_ref, lam_ref,
         y_ref, fs_ref, xwin, xc_s, r_s, i_s, a_s, b_s, y_s, hcar) = refs
    else:
        (xprev_ref, x_ref, xnext_ref, h0_ref, cw_ref, cb_ref, wg_ref, bg_ref, lam_ref,
         hf_ref, fs_ref, xwin, xc_s, r_s, i_s, a_s, b_s, hcar) = refs
    c = pl.program_id(1)
    chunk = n_chunks - 1 - c if reverse else c
    sub_rows = TS * TMJ_ROWS

    def rows_of(pb, t0, n_steps):
        first = (pb * LC + t0) * TMJ_ROWS
        if not isinstance(first, int):
            first = pl.multiple_of(first, TMJ_ROWS)
        return pl.ds(first, n_steps * TMJ_ROWS)

    @pl.when(c == 0)
    def _():
        hcar[...] = h0_ref[...]

    xwin[:, 0:CONV_LEFT] = jnp.where(chunk > 0, xprev_ref[...], 0.0)
    xwin[:, LC + CONV_LEFT:LC + CONV_W - 1] = jnp.where(chunk < n_chunks - 1, xnext_ref[...], 0.0)

    xwin[:, CONV_LEFT:CONV_LEFT + LC] = x_ref[...]

    neg_lam = -lam_ref[...]
    softplus = jnp.maximum(neg_lam, 0.0) + jnp.log(1.0 + jnp.exp(-jnp.abs(neg_lam)))
    half_decay = (-0.5 * RG_C * math.log2(math.e)) * softplus

    def conv(pb):
        for t0 in range(0, LC, TS):
            xc = cb_ref[...] + cw_ref[0] * xwin[pb, t0:t0 + TS]
            for k in range(1, CONV_W):
                xc = xc + cw_ref[k] * xwin[pb, t0 + k:t0 + k + TS]
            xc_s[rows_of(pb, t0, TS), :] = xc.reshape(sub_rows, LANES)

    def gate_matmuls(pb):
        for k in range(RNN_BLOCKS):
            rows = pl.ds(pb * LC * TMJ_ROWS + k, LC * PAIR, stride=RNN_BLOCKS)
            g = jnp.dot(xc_s[rows, :].astype(BF16), wg_ref[k], preferred_element_type=F32)
            r_s[rows, :] = g[:, :LANES]
            i_s[rows, :] = g[:, LANES:]

    def gates(pb):
        for t0 in range(0, LC, TS):
            rows = rows_of(pb, t0, TS)
            tile = lambda ref: ref[rows, :].reshape(TS, TMJ_ROWS, LANES)
            tr = jnp.tanh(tile(r_s) + bg_ref[0])
            ti = jnp.tanh(tile(i_s) + bg_ref[1])
            log2_a = tr * half_decay + half_decay
            a = jnp.exp2(log2_a)
            q = jnp.tanh(log2_a * (-math.log(2.0))) * (a * a + 1.0)
            b = jnp.where(q > 0.0, q * lax.rsqrt(q), 0.0) * ((ti + 1.0) * tile(xc_s))
            a_s[rows, :] = a.reshape(sub_rows, LANES)
            b_s[rows, :] = b.reshape(sub_rows, LANES)

    for stage in (conv, gate_matmuls, gates):
        for pb in range(PB):
            stage(pb)

    def step(j, hs):
        t = LC - 1 - j if reverse else j
        out = []
        for pb in range(PB):
            rows = rows_of(pb, t, 1)
            h = a_s[rows, :] * hs[pb] + b_s[rows, :]
            if reverse:
                y_s[rows, :] = h
            else:
                hf_ref[pb, t] = h
            out.append(h)
        return tuple(out)

    hs = lax.fori_loop(0, LC, step, tuple(hcar[pb] for pb in range(PB)), unroll=8)
    for pb in range(PB):
        hcar[pb] = hs[pb]
        fs_ref[pb] = hs[pb]

    if reverse:
        for pb in range(PB):
            for t0 in range(0, LC, TS):
                rows = rows_of(pb, t0, TS)
                both = hf_ref[pb, t0:t0 + TS].reshape(sub_rows, LANES) + y_s[rows, :]
                y_s[rows, :] = both * gg_ref[pb, t0:t0 + TS].reshape(sub_rows, LANES)
            for s in range(PAIR):
                cols = [y_s[pl.ds(pb * LC * TMJ_ROWS + s * RNN_BLOCKS + k, LC, stride=TMJ_ROWS), :]
                        for k in range(RNN_BLOCKS)]
                y_ref[pb * PAIR + s] = jnp.concatenate(cols, axis=1).astype(BF16)


def _scan(xr, gg, hf, h0, conv_w, conv_b, w_gates, b_gates, lam, direction):
    n_pairs, seq_len = xr.shape[:2]
    n_chunks = seq_len // LC
    reverse = direction == 1
    pos = (lambda c: n_chunks - 1 - c) if reverse else (lambda c: c)
    tmj_blk = pl.BlockSpec((PB, LC, TMJ_ROWS, LANES), lambda i, c: (i, pos(c), 0, 0))
    state_blk = pl.BlockSpec((PB, TMJ_ROWS, LANES), lambda i, c: (i, 0, 0))
    per_dir = lambda *shape: pl.BlockSpec((None,) + shape, lambda i, c: (direction,) + (0,) * len(shape))
    in_specs = [
        pl.BlockSpec((PB, CONV_LEFT, TMJ_ROWS, LANES),
                     lambda i, c: (i, jnp.maximum(pos(c) * (LC // CONV_LEFT) - 1, 0), 0, 0)),
        tmj_blk,
        pl.BlockSpec((PB, 1, TMJ_ROWS, LANES), lambda i, c: (i, jnp.minimum((pos(c) + 1) * LC, seq_len - 1), 0, 0)),
    ]
    args = [xr, xr, xr]
    if reverse:
        in_specs += [tmj_blk, tmj_blk]
        args += [gg, hf]
    in_specs += [state_blk,
                 pl.BlockSpec((CONV_W, TMJ_ROWS, LANES), lambda i, c: (0, 0, 0)),
                 pl.BlockSpec((TMJ_ROWS, LANES), lambda i, c: (0, 0)),
                 per_dir(RNN_BLOCKS, LANES, 2 * LANES),
                 per_dir(2, TMJ_ROWS, LANES),
                 per_dir(TMJ_ROWS, LANES)]
    args += [h0, conv_w, conv_b, w_gates, b_gates, lam]
    flat = pltpu.VMEM((PB * LC * TMJ_ROWS, LANES), F32)
    scratch = [pltpu.VMEM((PB, LC + CONV_W - 1, TMJ_ROWS, LANES), F32)] + [flat] * (6 if reverse else 5)
    scratch += [pltpu.VMEM((PB, TMJ_ROWS, LANES), F32)]
    state = jax.ShapeDtypeStruct((n_pairs, TMJ_ROWS, LANES), F32)
    if reverse:
        out_shape = (jax.ShapeDtypeStruct((n_pairs * PAIR, seq_len, D_RNN), BF16), state)
        out_specs = (pl.BlockSpec((PB * PAIR, LC, D_RNN), lambda i, c: (i, pos(c), 0)), state_blk)
    else:
        out_shape = (jax.ShapeDtypeStruct(xr.shape, F32), state)
        out_specs = (tmj_blk, state_blk)
    return pl.pallas_call(
        functools.partial(_scan_kernel, reverse=reverse, n_chunks=n_chunks),
        out_shape=out_shape,
        grid=(n_pairs // PB, n_chunks),
        in_specs=in_specs,
        out_specs=out_specs,
        scratch_shapes=scratch,
        compiler_params=_params(("parallel", "arbitrary")),
        name="scan_bwd" if reverse else "scan_fwd",
    )(*args)


def _route(lt):
    n = lt.shape[1]
    row = lax.broadcasted_iota(jnp.int32, (EXPERTS_PER_GROUP, n), 0)
    neg = jnp.float32(-jnp.inf)

    def arg_max(v):
        m = jnp.max(v, axis=0, keepdims=True)
        return jnp.min(jnp.where(v == m, row, EXPERTS_PER_GROUP), axis=0, keepdims=True)

    g_idx = arg_max(lt[0:N_GROUPS])
    el = lt[E_ROW0:E_ROW0 + EXPERTS_PER_GROUP]
    for g in range(1, N_GROUPS):
        first = E_ROW0 + g * EXPERTS_PER_GROUP
        el = jnp.where(g_idx == g, lt[first:first + EXPERTS_PER_GROUP], el)
    i1 = arg_max(el)
    i2 = arg_max(jnp.where(row == i1, neg, el))
    ja = jnp.minimum(i1, i2)
    jb = jnp.maximum(i1, i2)
    pair = (ja * (2 * EXPERTS_PER_GROUP - 1 - ja)) // 2 + (jb - ja - 1)
    return g_idx * PAIRS_PER_GROUP + pair


def _store_token_major(ref, x, t0=0):
    n = x.shape[0]
    for k in range(ROW_TILES):
        ref[pl.ds(t0 * ROW_TILES + k, n, stride=ROW_TILES), :] = x[:, k * LANES:(k + 1) * LANES]


def _load_token_major(ref, n):
    return jnp.concatenate([ref[pl.ds(k, n, stride=ROW_TILES), :] for k in range(ROW_TILES)], axis=1)


def _postmix_kernel(*refs, add_pos):
    refs = list(refs)
    x_ref = refs.pop(0)
    pos_refs = (refs.pop(0), refs.pop(0)) if add_pos else None
    (yr_ref, ys_ref, mod_ref, gpost_ref, gpre_ref, wout_ref, rw_ref, rb_ref, earlier_ref, cnt0_ref,
     x1_ref, hn_ref, rt_ref, cnt_ref, run_ref) = refs

    @pl.when(pl.program_id(0) == 0)
    def _():
        run_ref[...] = cnt0_ref[...]

    gate1 = mod_ref[0, :, 2 * D_MODEL:3 * D_MODEL]
    shift2 = mod_ref[0, :, 3 * D_MODEL:4 * D_MODEL]
    scale2 = mod_ref[0, :, 4 * D_MODEL:5 * D_MODEL]
    y = (jnp.dot(yr_ref[...], wout_ref[0:D_RNN, :], preferred_element_type=F32)
         + jnp.dot(ys_ref[...], wout_ref[D_RNN:, :], preferred_element_type=F32))
    x1 = _load_x(x_ref, pos_refs, 0, TP) + _rms(y) * (gate1 * gpost_ref[...])
    x1_ref[...] = x1
    hn = _rms(x1) * (gpre_ref[...] * (1.0 + scale2)) + shift2
    _store_token_major(hn_ref, hn)
    lt = lax.dot_general(rw_ref[...], hn.astype(BF16), (((1,), (1,)), ((), ())),
                         preferred_element_type=F32) + rb_ref[:, 0:1]
    bucket = _route(lt)
    onehot = lax.broadcasted_iota(jnp.int32, (ROUTER_ROWS, TP), 0) == bucket
    before = jnp.dot(onehot.astype(BF16), earlier_ref[...], preferred_element_type=F32) + run_ref[:, 0:1]
    rank = jnp.sum(jnp.where(onehot, before, 0.0), axis=0, keepdims=True).astype(jnp.int32)
    row = lax.broadcasted_iota(jnp.int32, (SUBLANES, TP), 0)
    rt_ref[...] = jnp.where(row == 0, bucket, jnp.where(row == 1, rank, 0))
    run_ref[...] += jnp.sum(onehot.astype(F32), axis=1, keepdims=True)
    cnt_ref[...] = run_ref[...]


def _postmix(x, y_rnn, y_sgu, mod3, cond_of_tile, g_post, g_pre, w_out_b, router_wt, router_bt, earlier, pos_tab,
             counts0):
    n_tok = x.shape[0]
    n_tiles = n_tok // TP
    add_pos = pos_tab is not None
    tok = lambda i: (i, 0)
    const2 = lambda i: (0, 0)
    in_specs = [pl.BlockSpec((TP, D_MODEL), tok)]
    args = [x]
    if add_pos:
        reps = TP // GRID_W
        tiles_per_seq = GRID_W // reps
        in_specs += [pl.BlockSpec((None, reps, D_MODEL // 2), lambda i: (i % tiles_per_seq, 0, 0)),
                     pl.BlockSpec((GRID_W, D_MODEL // 2), const2)]
        args += [pos_tab.reshape(tiles_per_seq, reps, D_MODEL // 2), pos_tab]
    in_specs += [pl.BlockSpec((TP, D_RNN), tok),
                 pl.BlockSpec((TP, D_SGU), tok),
                 pl.BlockSpec((1, 1, 6 * D_MODEL), lambda i: (cond_of_tile(i), 0, 0)),
                 pl.BlockSpec((1, D_MODEL), const2),
                 pl.BlockSpec((1, D_MODEL), const2),
                 pl.BlockSpec((D_MODEL, D_MODEL), const2),
                 pl.BlockSpec((ROUTER_ROWS, D_MODEL), const2),
                 pl.BlockSpec((ROUTER_ROWS, LANES), const2),
                 pl.BlockSpec((TP, TP), const2),
                 pl.BlockSpec((ROUTER_ROWS, LANES), const2)]
    args += [y_rnn, y_sgu, mod3, g_post, g_pre, w_out_b, router_wt, router_bt, earlier, counts0]
    counts_spec = pl.BlockSpec((ROUTER_ROWS, LANES), const2)
    x1, hn, route, counts = pl.pallas_call(
        functools.partial(_postmix_kernel, add_pos=add_pos),
        out_shape=(jax.ShapeDtypeStruct((n_tok, D_MODEL), F32),
                   jax.ShapeDtypeStruct((n_tok * ROW_TILES, LANES), F32),
                   jax.ShapeDtypeStruct((n_tiles * SUBLANES, TP), jnp.int32),
                   jax.ShapeDtypeStruct((ROUTER_ROWS, LANES), F32)),
        grid=(n_tiles,),
        in_specs=in_specs,
        out_specs=(pl.BlockSpec((TP, D_MODEL), tok),
                   pl.BlockSpec((TP * ROW_TILES, LANES), tok),
                   pl.BlockSpec((SUBLANES, TP), tok),
                   counts_spec),
        scratch_shapes=[pltpu.VMEM((ROUTER_ROWS, LANES), F32)],
        compiler_params=_params(("arbitrary",)),
        name="postmix",
    )(*args)
    route = route.reshape(n_tiles, SUBLANES, TP)
    return x1, hn, route[:, 0].reshape(n_tok), route[:, 1].reshape(n_tok), counts


def _token_rows(ref, t):
    return ref.at[pl.ds(pl.multiple_of(t * ROW_TILES, ROW_TILES), ROW_TILES), :]


def _dispatch_kernel(dest_ref, hc_ref, hs_ref, xs_ref, sem, *, n_ctx_steps):
    i = pl.program_id(0)
    base = i * TD

    def scatter(src_ref):
        def start(g, carry):
            for u in range(DMA_UNROLL):
                r = g * DMA_UNROLL + u
                pltpu.make_async_copy(_token_rows(src_ref, r), _token_rows(xs_ref, dest_ref[base + r]),
                                      sem).start(priority=u % 2)
            return carry

        lax.fori_loop(0, TD // DMA_UNROLL, start, 0)
        pltpu.make_async_copy(src_ref, xs_ref.at[pl.ds(0, TD * ROW_TILES), :], sem).wait()

    @pl.when(i < n_ctx_steps)
    def _():
        scatter(hc_ref)

    @pl.when(i >= n_ctx_steps)
    def _():
        scatter(hs_ref)


def _dispatch(dest, hn_ctx, hn_dec, n_slots):
    n_ctx_steps = hn_ctx.shape[0] // (TD * ROW_TILES)
    n_dec_steps = hn_dec.shape[0] // (TD * ROW_TILES)
    return pl.pallas_call(
        functools.partial(_dispatch_kernel, n_ctx_steps=n_ctx_steps),
        out_shape=jax.ShapeDtypeStruct((n_slots * ROW_TILES, LANES), F32),
        grid_spec=pltpu.PrefetchScalarGridSpec(
            num_scalar_prefetch=1,
            grid=(n_ctx_steps + n_dec_steps,),
            in_specs=[pl.BlockSpec((TD * ROW_TILES, LANES), lambda i, d: (jnp.minimum(i, n_ctx_steps - 1), 0)),
                      pl.BlockSpec((TD * ROW_TILES, LANES), lambda i, d: (jnp.maximum(i - n_ctx_steps, 0), 0))],
            out_specs=pl.BlockSpec(memory_space=pl.ANY),
            scratch_shapes=[pltpu.SemaphoreType.DMA(())]),
        compiler_params=_params(("arbitrary",)),
        name="dispatch",
    )(dest, hn_ctx, hn_dec)


def _experts_kernel(ea_ref, eb_ref, nv_ref, blk_ref, xs_ref, rw_ref, rb_ref, *refs):
    w32_refs, ys_ref, w_refs = refs[:6], refs[6], refs[7:]
    wga_ref, wua_ref, wda_ref, wgb_ref, wub_ref, wdb_ref = w_refs
    i = pl.program_id(0)
    nv = nv_ref[i]
    prev = jnp.maximum(i - 1, 0)

    for e_ref, first in ((ea_ref, 0), (eb_ref, 3)):
        @pl.when((i == 0) | (e_ref[i] != e_ref[prev]))
        def _(first=first):
            for w32_ref, w_ref in zip(w32_refs[first:first + 3], w_refs[first:first + 3]):
                w_ref[...] = w32_ref[0].astype(BF16)

    @pl.when(nv > 0)
    def _():
        row = lax.broadcasted_iota(jnp.int32, (TMX, 1), 0)
        xb = jnp.where(row < nv, _load_token_major(xs_ref, TMX), 0.0).astype(BF16)
        logits = jnp.dot(xb, rw_ref[...], preferred_element_type=F32) + rb_ref[...]
        lane = lax.broadcasted_iota(jnp.int32, logits.shape, 1)
        ea = ea_ref[i]
        eb = eb_ref[i]
        gmask = lane < N_GROUPS
        gl = jnp.where(gmask, logits, -jnp.inf)
        gmax = jnp.max(gl, axis=-1, keepdims=True)
        gexp = jnp.where(gmask, jnp.exp(gl - gmax), 0.0)
        g_own = jnp.sum(jnp.where(lane == ea // EXPERTS_PER_GROUP, gexp, 0.0), axis=-1, keepdims=True)
        g_w = g_own / jnp.sum(gexp, axis=-1, keepdims=True)
        la = jnp.sum(jnp.where(lane == ea + E_LANE0, logits, 0.0), axis=-1, keepdims=True)
        lb = jnp.sum(jnp.where(lane == eb + E_LANE0, logits, 0.0), axis=-1, keepdims=True)
        m = jnp.maximum(la, lb)
        pa = jnp.exp(la - m)
        pb = jnp.exp(lb - m)
        inv = g_w / (pa + pb)

        def hidden(x, wg_ref, wu_ref, w):
            g = jnp.dot(x, wg_ref[...], preferred_element_type=F32)
            u = jnp.dot(x, wu_ref[...], preferred_element_type=F32)
            return ((g * _sigmoid(g)) * u * w).astype(BF16)

        part = TMX // EXPERT_ROW_PARTS
        ys = []
        for h in range(EXPERT_ROW_PARTS):
            rows = slice(h * part, (h + 1) * part)
            act_a = hidden(xb[rows], wga_ref, wua_ref, (pa * inv)[rows])
            act_b = hidden(xb[rows], wgb_ref, wub_ref, (pb * inv)[rows])
            y = (jnp.dot(act_a, wda_ref[...], preferred_element_type=F32)
                 + jnp.dot(act_b, wdb_ref[...], preferred_element_type=F32))
            ys.append(y)
        _store_token_major(ys_ref, jnp.concatenate(ys, axis=0))


def _experts(sched, xs, router_w, router_b, wg, wu, wd):
    ea, eb, nv, blk = sched
    n_tiles = ea.shape[0]
    rows = lambda i, ea, eb, nv, blk: (blk[i], 0)
    const2 = lambda i, ea, eb, nv, blk: (0, 0)
    exp_a = lambda i, ea, eb, nv, blk: (ea[i], 0, 0)
    exp_b = lambda i, ea, eb, nv, blk: (eb[i], 0, 0)
    w_in_spec = lambda m: pl.BlockSpec((1, D_MODEL, D_EXPERT), m)
    w_out_spec = lambda m: pl.BlockSpec((1, D_EXPERT, D_MODEL), m)
    return pl.pallas_call(
        _experts_kernel,
        out_shape=jax.ShapeDtypeStruct(xs.shape, F32),
        grid_spec=pltpu.PrefetchScalarGridSpec(
            num_scalar_prefetch=4,
            grid=(n_tiles,),
            in_specs=[pl.BlockSpec((TMX * ROW_TILES, LANES), rows),
                      pl.BlockSpec((D_MODEL, ROUTER_LANES), const2),
                      pl.BlockSpec((1, ROUTER_LANES), const2),
                      w_in_spec(exp_a), w_in_spec(exp_a), w_out_spec(exp_a),
                      w_in_spec(exp_b), w_in_spec(exp_b), w_out_spec(exp_b)],
            out_specs=pl.BlockSpec((TMX * ROW_TILES, LANES), rows),
            scratch_shapes=[pltpu.VMEM((D_MODEL, D_EXPERT), BF16), pltpu.VMEM((D_MODEL, D_EXPERT), BF16),
                            pltpu.VMEM((D_EXPERT, D_MODEL), BF16)] * 2),
        compiler_params=_params(("arbitrary",)),
        name="experts",
    )(ea, eb, nv, blk, xs, router_w, router_b, wg, wu, wd, wg, wu, wd)


def _combine_kernel(dest_ref, ys_ref, x1_ref, mod_ref, gpost_ref, o_ref, ybuf, sems, *, tile0):
    i = pl.program_id(0)
    n = pl.num_programs(0)

    def fetch(tile, slot):
        base = (tile + tile0) * TC

        def start(g, carry):
            for u in range(DMA_UNROLL):
                r = g * DMA_UNROLL + u
                pltpu.make_async_copy(_token_rows(ys_ref, dest_ref[base + r]), _token_rows(ybuf.at[slot], r),
                                      sems.at[slot]).start(priority=u % 2)
            return carry

        lax.fori_loop(0, TC // DMA_UNROLL, start, 0)

    @pl.when(i == 0)
    def _():
        fetch(0, 0)

    @pl.when(i + 1 < n)
    def _():
        fetch(i + 1, (i + 1) % 2)

    slot = i % 2
    pltpu.make_async_copy(ys_ref.at[pl.ds(0, TC * ROW_TILES), :], ybuf.at[slot], sems.at[slot]).wait()
    gate2 = mod_ref[0, :, 5 * D_MODEL:6 * D_MODEL]
    o_ref[...] = x1_ref[...] + _rms(_load_token_major(ybuf.at[slot], TC)) * (gate2 * gpost_ref[...])


def _combine(dest, ys, x1, mod3, g_post, cond_of_tile, tile0):
    n_tok = x1.shape[0]
    return pl.pallas_call(
        functools.partial(_combine_kernel, tile0=tile0),
        out_shape=jax.ShapeDtypeStruct((n_tok, D_MODEL), F32),
        grid_spec=pltpu.PrefetchScalarGridSpec(
            num_scalar_prefetch=1,
            grid=(n_tok // TC,),
            in_specs=[pl.BlockSpec(memory_space=pl.ANY),
                      pl.BlockSpec((TC, D_MODEL), lambda i, d: (i, 0)),
                      pl.BlockSpec((1, 1, 6 * D_MODEL), lambda i, d: (cond_of_tile(i), 0, 0)),
                      pl.BlockSpec((1, D_MODEL), lambda i, d: (0, 0))],
            out_specs=pl.BlockSpec((TC, D_MODEL), lambda i, d: (i, 0)),
            scratch_shapes=[pltpu.VMEM((2, TC * ROW_TILES, LANES), F32), pltpu.SemaphoreType.DMA((2,))]),
        compiler_params=_params(("arbitrary",)),
        name="combine",
    )(dest, ys, x1, mod3, g_post)


def _schedule(bucket, rank, counts):
    n_tok = bucket.shape[0]
    n_max = n_tok // TMX + N_BUCKETS
    cnt = counts[:N_BUCKETS, 0].astype(jnp.int32)
    tiles = (cnt + TMX - 1) // TMX
    tile_end = jnp.cumsum(tiles)
    tile_start = tile_end - tiles
    ids = jnp.arange(N_BUCKETS, dtype=jnp.int32)
    slot0 = jnp.sum(jnp.where(bucket[:, None] == ids[None, :], (tile_start * TMX)[None, :], 0), axis=1)
    dest = slot0 + rank
    i = jnp.arange(n_max, dtype=jnp.int32)
    total = tile_end[-1]
    valid = i < total
    tb = jnp.sum((jnp.minimum(i, total - 1)[:, None] >= tile_end[None, :]).astype(jnp.int32), axis=1)
    pairs = [(a, b) for a in range(EXPERTS_PER_GROUP) for b in range(a + 1, EXPERTS_PER_GROUP)]
    ea_tab = jnp.array([g * EXPERTS_PER_GROUP + a for g in range(N_GROUPS) for a, _ in pairs], jnp.int32)
    eb_tab = jnp.array([g * EXPERTS_PER_GROUP + b for g in range(N_GROUPS) for _, b in pairs], jnp.int32)
    hit = tb[:, None] == ids[None, :]
    look = lambda tab: jnp.sum(jnp.where(hit, tab[None, :], 0), axis=1)
    ea, eb = look(ea_tab), look(eb_tab)
    nv = jnp.where(valid, jnp.clip(look(cnt) - (i - look(tile_start)) * TMX, 0, TMX), 0)
    blk = jnp.minimum(i, total - 1)
    return dest, (ea, eb, nv, blk), n_max * TMX


def _block_diag_gates(rg_wa, rg_wx):
    heads = LANES // HEAD_RNN

    def bd(w):
        w = w.reshape(2, RNN_BLOCKS, heads, HEAD_RNN, HEAD_RNN)
        eye = jnp.eye(heads, dtype=w.dtype)
        full = jnp.einsum('dghij,hk->dghikj', w, eye)
        return full.reshape(2, RNN_BLOCKS, LANES, LANES)

    return jnp.concatenate([bd(rg_wa), bd(rg_wx)], axis=-1).astype(BF16)


def _row_tile(v):
    blocks = v.reshape(v.shape[:-1] + (RNN_BLOCKS, LANES))
    return jnp.concatenate([blocks] * PAIR, axis=-2)


def _to_time_major_state(h):
    return h.reshape(h.shape[0] // PAIR, TMJ_ROWS, LANES)


def kernel(x_prompt, x_sample, state_rglru, c, c_ctx, w_mod, b_mod, g_pre_mix, g_post_mix, g_pre_ffn,
           g_post_ffn, w_in, conv_w, conv_b, rg_wa, rg_ba, rg_wx, rg_bx, rg_lambda, sgu_g, sgu_w, sgu_b,
           w_out, router_g_w, router_g_b, router_e_w, router_e_b, exp_w_gate, exp_w_up, exp_w_down):
    assert w_mod.shape[0] == 1, "single-layer trunk"
    n_ctx, ctx_len, _ = x_prompt.shape
    n_dec, dec_len, _ = x_sample.shape
    l = 0

    n_cond = SUBLANES
    ctx_rows = n_cond - n_dec
    ctx_per_step = max(PAIR, PREMIX_ROWS // min(ctx_len, TT))
    assert n_dec % PAIR == 0 and ctx_rows == ctx_per_step and n_dec % ctx_per_step == 0
    cond = jnp.concatenate([c, jnp.broadcast_to(c_ctx, (ctx_rows, D_MODEL))], axis=0)
    mod3 = _modulation(cond, w_mod[l], b_mod[l]).reshape(n_cond, 1, 6 * D_MODEL)
    pos_tab = _pos_table()

    w_rx, w_gate, w_u, w_v = jnp.split(w_in[l], [D_RNN, 2 * D_RNN, 2 * D_RNN + D_SGU], axis=1)
    w_in_b = jnp.concatenate([w_v, w_u, w_gate, w_rx], axis=1).astype(BF16)
    w_out_b = w_out[l].astype(BF16)
    sgu_w_b = sgu_w[l].reshape(2, 4 * CHUNK, CHUNK).astype(BF16)
    sgu_bias_tile = jnp.repeat(sgu_b[l].T, HEAD_SGU, axis=1)
    w_gates = _block_diag_gates(rg_wa[l], rg_wx[l])
    b_gates = 0.5 * jnp.stack([_row_tile(rg_ba[l]), _row_tile(rg_bx[l])], axis=1)
    lam = _row_tile(rg_lambda[l])
    conv_w_t = 0.5 * _row_tile(conv_w[l])
    conv_b_t = 0.5 * _row_tile(conv_b[l])
    lane_pad = ROUTER_LANES - E_LANE0 - N_EXPERTS
    router_w = jnp.pad(jnp.concatenate([router_g_w[l], router_e_w[l]], axis=1), ((0, 0), (0, lane_pad))).astype(BF16)
    router_b = jnp.pad(jnp.concatenate([router_g_b[l], router_e_b[l]]), (0, lane_pad)).reshape(1, ROUTER_LANES)
    gap, tail = E_ROW0 - N_GROUPS, ROUTER_ROWS - E_ROW0 - N_EXPERTS
    router_wt = jnp.concatenate([router_g_w[l].T, jnp.zeros((gap, D_MODEL), F32), router_e_w[l].T,
                                 jnp.zeros((tail, D_MODEL), F32)], axis=0).astype(BF16)
    router_bt = jnp.concatenate([router_g_b[l], jnp.zeros((gap,), F32), router_e_b[l], jnp.zeros((tail,), F32)])
    router_bt = jnp.broadcast_to(router_bt[:, None], (ROUTER_ROWS, LANES))
    earlier = jnp.triu(jnp.ones((TP, TP), BF16), k=1)
    row = lambda v: v.reshape(1, -1)

    n_ctx_tok = n_ctx * ctx_len

    def mixer(x, h0, cond_of_tile, cond_block, use_pos, counts0):
        n_seq, seq_len, _ = x.shape
        xf = x.reshape(n_seq * seq_len, D_MODEL)
        tab = pos_tab if use_pos else None
        xr, gg, y_sgu = _premix(x, mod3, cond_block, row(g_pre_mix[l]), w_in_b, row(sgu_g[l]),
                                sgu_w_b, sgu_bias_tile, tab)
        scan_params = (conv_w_t, conv_b_t, w_gates, b_gates, lam)
        hf, hf_last = _scan(xr, None, None, _to_time_major_state(h0[:, 0]), *scan_params, direction=0)
        y_rnn, hb_first = _scan(xr, gg, hf, _to_time_major_state(h0[:, 1]), *scan_params, direction=1)
        fstate = jnp.stack([hf_last.reshape(n_seq, D_RNN), hb_first.reshape(n_seq, D_RNN)], axis=1)
        x1, hn, bucket, rank, counts = _postmix(
            xf, y_rnn.reshape(n_seq * seq_len, D_RNN), y_sgu.reshape(n_seq * seq_len, D_SGU), mod3,
            cond_of_tile, row(g_post_mix[l]), row(g_pre_ffn[l]), w_out_b, router_wt, router_bt, earlier, tab,
            counts0)
        return x1, hn, bucket, rank, counts, fstate

    ctx_cond = lambda i: n_dec
    dec_cond = lambda i: i // (dec_len // TP)
    h0_ctx = jnp.zeros((n_ctx, 2, D_RNN), F32)
    counts0 = jnp.zeros((ROUTER_ROWS, LANES), F32)
    x1_ctx, hn_ctx, bucket_ctx, rank_ctx, counts, st = mixer(x_prompt, h0_ctx, ctx_cond, lambda p: n_dec // ctx_per_step,
                                                             False, counts0)
    new_state = st.astype(state_rglru.dtype)[:, None]
    x1_dec, hn_dec, bucket_dec, rank_dec, counts, _ = mixer(x_sample, state_rglru[:, l].astype(F32), dec_cond,
                                                            lambda p: p, True, counts)

    dest, sched, n_slots = _schedule(jnp.concatenate([bucket_ctx, bucket_dec]),
                                     jnp.concatenate([rank_ctx, rank_dec]), counts)
    xs = _dispatch(dest, hn_ctx, hn_dec, n_slots)
    ys = _experts(sched, xs, router_w, router_b, exp_w_gate[l], exp_w_up[l], exp_w_down[l])
    y_prompt = _combine(dest, ys, x1_ctx, mod3, row(g_post_ffn[l]), ctx_cond, 0)
    y_sample = _combine(dest, ys, x1_dec, mod3, row(g_post_ffn[l]), lambda i: i // (dec_len // TC), n_ctx_tok // TC)
    return (y_prompt.reshape(x_prompt.shape), y_sample.reshape(x_sample.shape), new_state)
```

```python
import functools
import math

import jax
import jax.numpy as jnp
from jax import lax
from jax.experimental import pallas as pl
from jax.experimental.pallas import tpu as pltpu

D_MODEL = 1024
D_RNN = 512
D_SGU = 512
N_HEADS_RNN = 8
HEAD_RNN = D_RNN // N_HEADS_RNN
N_HEADS_SGU = 8
HEAD_SGU = D_SGU // N_HEADS_SGU
CHUNK = 128
GRID_W = 64
RG_C = 8.0
N_GROUPS = 4
EXPERTS_PER_GROUP = 4
N_EXPERTS = N_GROUPS * EXPERTS_PER_GROUP
D_EXPERT = 512
EPS = 1e-6
POS_BASE = 10000.0

LANES = 128
SUBLANES = 8
CONV_W = 4
CONV_LEFT = 2
PAIR = 2
RNN_BLOCKS = D_RNN // LANES
TMJ_ROWS = PAIR * RNN_BLOCKS
ROUTER_LANES = LANES
E_LANE0 = N_GROUPS
ROUTER_ROWS = 32
E_ROW0 = SUBLANES

PAIRS_PER_GROUP = EXPERTS_PER_GROUP * (EXPERTS_PER_GROUP - 1) // 2
N_BUCKETS = N_GROUPS * PAIRS_PER_GROUP

ROW_TILES = D_MODEL // LANES

MOD_COLS = 2048
TP = 1024
TD = 2048
TC = 512
DMA_UNROLL = 8
EXPERT_ROW_PARTS = 2
TMX = 512
TT = 512
PREMIX_ROWS = 1024
LC = 256
TS = 16
PB = 2
VMEM_LIMIT = 56 * 1024 * 1024

F32 = jnp.float32
BF16 = jnp.bfloat16


def _params(sem):
    return pltpu.CompilerParams(dimension_semantics=sem, vmem_limit_bytes=VMEM_LIMIT)


def _rms(x):
    return x * lax.rsqrt(jnp.mean(x * x, axis=-1, keepdims=True) + EPS)


def _sigmoid(x):
    return 0.5 * jnp.tanh(0.5 * x) + 0.5


def _mod_kernel(cond_ref, w_ref, b_ref, o_ref):
    c = cond_ref[...]
    s = c * _sigmoid(c)
    o_ref[...] = jnp.dot(s.astype(BF16), w_ref[...].astype(BF16),
                         preferred_element_type=F32) + b_ref[...]


def _modulation(cond, w_mod, b_mod):
    n = w_mod.shape[1]
    return pl.pallas_call(
        _mod_kernel,
        out_shape=jax.ShapeDtypeStruct((cond.shape[0], n), F32),
        grid=(n // MOD_COLS,),
        in_specs=[pl.BlockSpec(cond.shape, lambda j: (0, 0)),
                  pl.BlockSpec((D_MODEL, MOD_COLS), lambda j: (0, j)),
                  pl.BlockSpec((1, MOD_COLS), lambda j: (0, j))],
        out_specs=pl.BlockSpec((cond.shape[0], MOD_COLS), lambda j: (0, j)),
        compiler_params=_params(("arbitrary",)),
        name="modulation",
    )(cond, w_mod, b_mod.reshape(1, n))


def _pos_kernel(o_ref):
    n_freq = D_MODEL // 4
    k = lax.broadcasted_iota(jnp.int32, (GRID_W, n_freq), 1).astype(F32)
    p = lax.broadcasted_iota(jnp.int32, (GRID_W, n_freq), 0).astype(F32)
    freq = jnp.exp(-math.log(POS_BASE) * k / n_freq)
    ang = p * freq
    o_ref[:, 0:n_freq] = jnp.sin(ang)
    o_ref[:, n_freq:2 * n_freq] = jnp.cos(ang)


def _pos_table():
    return pl.pallas_call(
        _pos_kernel,
        out_shape=jax.ShapeDtypeStruct((GRID_W, D_MODEL // 2), F32),
        name="pos_table",
    )()


def _add_pos(x, pos_refs, q0):
    if pos_refs is None:
        return x
    rows_ref, cols_ref = pos_refs
    reps = x.shape[0] // GRID_W
    rpart = jnp.concatenate(
        [jnp.broadcast_to(rows_ref[q:q + 1, :], (GRID_W, D_MODEL // 2)) for q in range(q0, q0 + reps)], axis=0)
    cpart = jnp.concatenate([cols_ref[...]] * reps, axis=0)
    return jnp.concatenate([x[:, :D_MODEL // 2] + rpart, x[:, D_MODEL // 2:] + cpart], axis=1)


def _load_x(x_ref, pos_refs, r0, n):
    return _add_pos(x_ref[r0:r0 + n, :], pos_refs, r0 // GRID_W)


def _premix_kernel(*refs, add_pos):
    refs = list(refs)
    x_ref = refs.pop(0)
    nseq, tt = x_ref.shape[:2]
    pos_refs = (refs.pop(0), refs.pop(0)) if add_pos else None
    mod_ref, g_ref, win_ref, sgug_ref, sguw_ref, sgub_ref, xr_ref, gg_ref, ys_ref = refs
    hn = []
    for s in range(nseq):
        shift = mod_ref[s, :, 0:D_MODEL]
        scale = mod_ref[s, :, D_MODEL:2 * D_MODEL]
        hn.append(_rms(_add_pos(x_ref[s], pos_refs, 0)) * (g_ref[...] * (1.0 + scale)) + shift)
    z = jnp.dot(jnp.concatenate(hn, axis=0).astype(BF16), win_ref[...],
                preferred_element_type=F32)
    half = D_SGU // 2
    heads_per_half = N_HEADS_SGU // 2
    lane_head = lax.broadcasted_iota(jnp.int32, (CHUNK, half), 1) // HEAD_SGU
    for s in range(nseq):
        zs = z[s * tt:(s + 1) * tt]
        vn = (_rms(zs[:, 0:D_SGU]) * sgug_ref[...]).astype(BF16)
        u = zs[:, D_SGU:2 * D_SGU]
        for c in range(tt // CHUNK):
            rows = slice(c * CHUNK, (c + 1) * CHUNK)
            halves = []
            for hf in range(2):
                r = jnp.dot(sguw_ref[hf], vn[rows, hf * half:(hf + 1) * half],
                            preferred_element_type=F32)
                sel = jnp.zeros((CHUNK, half), F32)
                for h in range(heads_per_half):
                    sel = jnp.where(lane_head == h, r[h * CHUNK:(h + 1) * CHUNK], sel)
                halves.append(sel)
            gatev = jnp.concatenate(halves, axis=1) + sgub_ref[...]
            ys_ref[s, rows, :] = (u[rows] * gatev).astype(BF16)
    for s in range(nseq):
        zs = z[s * tt:(s + 1) * tt]
        gg = jax.nn.gelu(zs[:, 2 * D_SGU:2 * D_SGU + D_RNN])
        xr = zs[:, 2 * D_SGU + D_RNN:]
        for k in range(RNN_BLOCKS):
            first = (s // PAIR) * tt * TMJ_ROWS + (s % PAIR) * RNN_BLOCKS + k
            rows = pl.ds(first, tt, stride=TMJ_ROWS)
            xr_ref[rows, :] = xr[:, k * LANES:(k + 1) * LANES]
            gg_ref[rows, :] = gg[:, k * LANES:(k + 1) * LANES]


def _premix(x, mod3, cond_block, g_pre, w_in_b, sgu_g, sgu_w_b, sgu_bias_tile, pos_tab):
    n_seq, seq_len, _ = x.shape
    tt = min(seq_len, TT)
    nseq = max(PAIR, PREMIX_ROWS // tt)
    assert nseq == PAIR or tt == seq_len
    n_pairs, n_tiles = n_seq // PAIR, seq_len // tt
    add_pos = pos_tab is not None
    const2 = lambda p, j: (0, 0)
    in_specs = [pl.BlockSpec((nseq, tt, D_MODEL), lambda p, j: (p, j, 0))]
    args = [x]
    if add_pos:
        reps = tt // GRID_W
        in_specs += [pl.BlockSpec((None, reps, D_MODEL // 2), lambda p, j: (j, 0, 0)),
                     pl.BlockSpec((GRID_W, D_MODEL // 2), const2)]
        args += [pos_tab.reshape(GRID_W // reps, reps, D_MODEL // 2), pos_tab]
    in_specs += [pl.BlockSpec((nseq, 1, 6 * D_MODEL), lambda p, j: (cond_block(p), 0, 0)),
                 pl.BlockSpec((1, D_MODEL), const2),
                 pl.BlockSpec((D_MODEL, 2 * D_RNN + 2 * D_SGU), const2),
                 pl.BlockSpec((1, D_SGU), const2),
                 pl.BlockSpec((2, 4 * CHUNK, CHUNK), lambda p, j: (0, 0, 0)),
                 pl.BlockSpec((CHUNK, D_SGU), const2)]
    args += [mod3, g_pre, w_in_b, sgu_g, sgu_w_b, sgu_bias_tile]
    tmj = jax.ShapeDtypeStruct((n_pairs * seq_len * TMJ_ROWS, LANES), F32)
    tmj_spec = pl.BlockSpec((nseq // PAIR * tt * TMJ_ROWS, LANES), lambda p, j: (p * n_tiles + j, 0))
    xr, gg, y_sgu = pl.pallas_call(
        functools.partial(_premix_kernel, add_pos=add_pos),
        out_shape=(tmj, tmj, jax.ShapeDtypeStruct((n_seq, seq_len, D_SGU), BF16)),
        grid=(n_seq // nseq, n_tiles),
        in_specs=in_specs,
        out_specs=(tmj_spec, tmj_spec, pl.BlockSpec((nseq, tt, D_SGU), lambda p, j: (p, j, 0))),
        compiler_params=_params(("parallel", "parallel")),
        name="premix",
    )(*args)
    shape4 = (n_pairs, seq_len, TMJ_ROWS, LANES)
    return xr.reshape(shape4), gg.reshape(shape4), y_sgu


def _scan_kernel(*refs, reverse, n_chunks):
    if reverse:
        (xprev_ref, x_ref, xnext_ref, gg_ref, hf_ref, h0_ref, cw_ref, cb_ref, wg_ref, bg_ref, lam_ref,
         y_ref, fs_ref, xwin, xc_s, r_s, i_s, a_s, b_s, y_s, hcar) = refs
    else:
        (xprev_ref, x_ref, xnext_ref, h0_ref, cw_ref, cb_ref, wg_ref, bg_ref, lam_ref,
         hf_ref, fs_ref, xwin, xc_s, r_s, i_s, a_s, b_s, hcar) = refs
    c = pl.program_id(1)
    chunk = n_chunks - 1 - c if reverse else c
    sub_rows = TS * TMJ_ROWS

    def rows_of(pb, t0, n_steps):
        first = (pb * LC + t0) * TMJ_ROWS
        if not isinstance(first, int):
            first = pl.multiple_of(first, TMJ_ROWS)
        return pl.ds(first, n_steps * TMJ_ROWS)

    @pl.when(c == 0)
    def _():
        hcar[...] = h0_ref[...]

    xwin[:, 0:CONV_LEFT] = jnp.where(chunk > 0, xprev_ref[...], 0.0)
    xwin[:, LC + CONV_LEFT:LC + CONV_W - 1] = jnp.where(chunk < n_chunks - 1, xnext_ref[...], 0.0)

    xwin[:, CONV_LEFT:CONV_LEFT + LC] = x_ref[...]

    neg_lam = -lam_ref[...]
    softplus = jnp.maximum(neg_lam, 0.0) + jnp.log(1.0 + jnp.exp(-jnp.abs(neg_lam)))
    half_decay = (-0.5 * RG_C * math.log2(math.e)) * softplus

    def conv(pb):
        for t0 in range(0, LC, TS):
            xc = cb_ref[...] + cw_ref[0] * xwin[pb, t0:t0 + TS]
            for k in range(1, CONV_W):
                xc = xc + cw_ref[k] * xwin[pb, t0 + k:t0 + k + TS]
            xc_s[rows_of(pb, t0, TS), :] = xc.reshape(sub_rows, LANES)

    def gate_matmuls(pb):
        for k in range(RNN_BLOCKS):
            rows = pl.ds(pb * LC * TMJ_ROWS + k, LC * PAIR, stride=RNN_BLOCKS)
            g = jnp.dot(xc_s[rows, :].astype(BF16), wg_ref[k], preferred_element_type=F32)
            r_s[rows, :] = g[:, :LANES]
            i_s[rows, :] = g[:, LANES:]

    def gates(pb):
        for t0 in range(0, LC, TS):
            rows = rows_of(pb, t0, TS)
            tile = lambda ref: ref[rows, :].reshape(TS, TMJ_ROWS, LANES)
            tr = jnp.tanh(tile(r_s) + bg_ref[0])
            ti = jnp.tanh(tile(i_s) + bg_ref[1])
            log2_a = tr * half_decay + half_decay
            a = jnp.exp2(log2_a)
            q = jnp.tanh(log2_a * (-math.log(2.0))) * (a * a + 1.0)
            b = jnp.where(q > 0.0, q * lax.rsqrt(q), 0.0) * ((ti + 1.0) * tile(xc_s))
            a_s[rows, :] = a.reshape(sub_rows, LANES)
            b_s[rows, :] = b.reshape(sub_rows, LANES)

    for stage in (conv, gate_matmuls, gates):
        for pb in range(PB):
            stage(pb)

    half_steps = LC // 2

    def put(pb, t, rows, h):
        if reverse:
            y_s[rows, :] = h
        else:
            hf_ref[pb, t] = h

    def step(j, carry):
        lead, lag, prod = carry
        t_lead = LC - 1 - j if reverse else j
        t_lag = half_steps - 1 - j if reverse else half_steps + j
        out = ([], [], [])
        for pb in range(PB):
            rows_lead, rows_lag = rows_of(pb, t_lead, 1), rows_of(pb, t_lag, 1)
            a_lag = a_s[rows_lag, :]
            h_lead = a_s[rows_lead, :] * lead[pb] + b_s[rows_lead, :]
            h_lag = a_lag * lag[pb] + b_s[rows_lag, :]
            p_lag = a_lag * prod[pb]
            put(pb, t_lead, rows_lead, h_lead)
            put(pb, t_lag, rows_lag, h_lag)
            r_s[rows_lag, :] = p_lag
            for acc, v in zip(out, (h_lead, h_lag, p_lag)):
                acc.append(v)
        return tuple(tuple(acc) for acc in out)

    zeros = tuple(jnp.zeros((TMJ_ROWS, LANES), F32) for _ in range(PB))
    ones = tuple(jnp.ones((TMJ_ROWS, LANES), F32) for _ in range(PB))
    lead, lag, prod = lax.fori_loop(0, half_steps, step, (tuple(hcar[pb] for pb in range(PB)), zeros, ones),
                                    unroll=8)
    lag_first = 0 if reverse else half_steps
    for pb in range(PB):
        final = lag[pb] + prod[pb] * lead[pb]
        hcar[pb] = final
        fs_ref[pb] = final
        for t0 in range(lag_first, lag_first + half_steps, TS):
            rows = rows_of(pb, t0, TS)
            fix = (r_s[rows, :].reshape(TS, TMJ_ROWS, LANES) * lead[pb]).reshape(sub_rows, LANES)
            if reverse:
                y_s[rows, :] = y_s[rows, :] + fix
            else:
                hf_ref[pb, t0:t0 + TS] = hf_ref[pb, t0:t0 + TS] + fix.reshape(TS, TMJ_ROWS, LANES)

    if reverse:
        for pb in range(PB):
            for t0 in range(0, LC, TS):
                rows = rows_of(pb, t0, TS)
                both = hf_ref[pb, t0:t0 + TS].reshape(sub_rows, LANES) + y_s[rows, :]
                y_s[rows, :] = both * gg_ref[pb, t0:t0 + TS].reshape(sub_rows, LANES)
            for s in range(PAIR):
                cols = [y_s[pl.ds(pb * LC * TMJ_ROWS + s * RNN_BLOCKS + k, LC, stride=TMJ_ROWS), :]
                        for k in range(RNN_BLOCKS)]
                y_ref[pb * PAIR + s] = jnp.concatenate(cols, axis=1).astype(BF16)


def _scan(xr, gg, hf, h0, conv_w, conv_b, w_gates, b_gates, lam, direction):
    n_pairs, seq_len = xr.shape[:2]
    n_chunks = seq_len // LC
    reverse = direction == 1
    pos = (lambda c: n_chunks - 1 - c) if reverse else (lambda c: c)
    tmj_blk = pl.BlockSpec((PB, LC, TMJ_ROWS, LANES), lambda i, c: (i, pos(c), 0, 0))
    state_blk = pl.BlockSpec((PB, TMJ_ROWS, LANES), lambda i, c: (i, 0, 0))
    per_dir = lambda *shape: pl.BlockSpec((None,) + shape, lambda i, c: (direction,) + (0,) * len(shape))
    in_specs = [
        pl.BlockSpec((PB, CONV_LEFT, TMJ_ROWS, LANES),
                     lambda i, c: (i, jnp.maximum(pos(c) * (LC // CONV_LEFT) - 1, 0), 0, 0)),
        tmj_blk,
        pl.BlockSpec((PB, 1, TMJ_ROWS, LANES), lambda i, c: (i, jnp.minimum((pos(c) + 1) * LC, seq_len - 1), 0, 0)),
    ]
    args = [xr, xr, xr]
    if reverse:
        in_specs += [tmj_blk, tmj_blk]
        args += [gg, hf]
    in_specs += [state_blk,
                 pl.BlockSpec((CONV_W, TMJ_ROWS, LANES), lambda i, c: (0, 0, 0)),
                 pl.BlockSpec((TMJ_ROWS, LANES), lambda i, c: (0, 0)),
                 per_dir(RNN_BLOCKS, LANES, 2 * LANES),
                 per_dir(2, TMJ_ROWS, LANES),
                 per_dir(TMJ_ROWS, LANES)]
    args += [h0, conv_w, conv_b, w_gates, b_gates, lam]
    flat = pltpu.VMEM((PB * LC * TMJ_ROWS, LANES), F32)
    scratch = [pltpu.VMEM((PB, LC + CONV_W - 1, TMJ_ROWS, LANES), F32)] + [flat] * (6 if reverse else 5)
    scratch += [pltpu.VMEM((PB, TMJ_ROWS, LANES), F32)]
    state = jax.ShapeDtypeStruct((n_pairs, TMJ_ROWS, LANES), F32)
    if reverse:
        out_shape = (jax.ShapeDtypeStruct((n_pairs * PAIR, seq_len, D_RNN), BF16), state)
        out_specs = (pl.BlockSpec((PB * PAIR, LC, D_RNN), lambda i, c: (i, pos(c), 0)), state_blk)
    else:
        out_shape = (jax.ShapeDtypeStruct(xr.shape, F32), state)
        out_specs = (tmj_blk, state_blk)
    return pl.pallas_call(
        functools.partial(_scan_kernel, reverse=reverse, n_chunks=n_chunks),
        out_shape=out_shape,
        grid=(n_pairs // PB, n_chunks),
        in_specs=in_specs,
        out_specs=out_specs,
        scratch_shapes=scratch,
        compiler_params=_params(("parallel", "arbitrary")),
        name="scan_bwd" if reverse else "scan_fwd",
    )(*args)


def _route(lt):
    n = lt.shape[1]
    row = lax.broadcasted_iota(jnp.int32, (EXPERTS_PER_GROUP, n), 0)
    neg = jnp.float32(-jnp.inf)

    def arg_max(v):
        m = jnp.max(v, axis=0, keepdims=True)
        return jnp.min(jnp.where(v == m, row, EXPERTS_PER_GROUP), axis=0, keepdims=True)

    g_idx = arg_max(lt[0:N_GROUPS])
    el = lt[E_ROW0:E_ROW0 + EXPERTS_PER_GROUP]
    for g in range(1, N_GROUPS):
        first = E_ROW0 + g * EXPERTS_PER_GROUP
        el = jnp.where(g_idx == g, lt[first:first + EXPERTS_PER_GROUP], el)
    i1 = arg_max(el)
    i2 = arg_max(jnp.where(row == i1, neg, el))
    ja = jnp.minimum(i1, i2)
    jb = jnp.maximum(i1, i2)
    pair = (ja * (2 * EXPERTS_PER_GROUP - 1 - ja)) // 2 + (jb - ja - 1)
    return g_idx * PAIRS_PER_GROUP + pair


def _store_token_major(ref, x, t0=0):
    n = x.shape[0]
    for k in range(ROW_TILES):
        ref[pl.ds(t0 * ROW_TILES + k, n, stride=ROW_TILES), :] = x[:, k * LANES:(k + 1) * LANES]


def _load_token_major(ref, n):
    return jnp.concatenate([ref[pl.ds(k, n, stride=ROW_TILES), :] for k in range(ROW_TILES)], axis=1)


def _postmix_kernel(*refs, add_pos):
    refs = list(refs)
    x_ref = refs.pop(0)
    pos_refs = (refs.pop(0), refs.pop(0)) if add_pos else None
    (yr_ref, ys_ref, mod_ref, gpost_ref, gpre_ref, wout_ref, rw_ref, rb_ref, earlier_ref, cnt0_ref,
     x1_ref, hn_ref, rt_ref, cnt_ref, run_ref) = refs

    @pl.when(pl.program_id(0) == 0)
    def _():
        run_ref[...] = cnt0_ref[...]

    gate1 = mod_ref[0, :, 2 * D_MODEL:3 * D_MODEL]
    shift2 = mod_ref[0, :, 3 * D_MODEL:4 * D_MODEL]
    scale2 = mod_ref[0, :, 4 * D_MODEL:5 * D_MODEL]
    y = (jnp.dot(yr_ref[...], wout_ref[0:D_RNN, :], preferred_element_type=F32)
         + jnp.dot(ys_ref[...], wout_ref[D_RNN:, :], preferred_element_type=F32))
    x1 = _load_x(x_ref, pos_refs, 0, TP) + _rms(y) * (gate1 * gpost_ref[...])
    x1_ref[...] = x1
    hn = _rms(x1) * (gpre_ref[...] * (1.0 + scale2)) + shift2
    _store_token_major(hn_ref, hn)
    lt = lax.dot_general(rw_ref[...], hn.astype(BF16), (((1,), (1,)), ((), ())),
                         preferred_element_type=F32) + rb_ref[:, 0:1]
    bucket = _route(lt)
    onehot = lax.broadcasted_iota(jnp.int32, (ROUTER_ROWS, TP), 0) == bucket
    before = jnp.dot(onehot.astype(BF16), earlier_ref[...], preferred_element_type=F32) + run_ref[:, 0:1]
    rank = jnp.sum(jnp.where(onehot, before, 0.0), axis=0, keepdims=True).astype(jnp.int32)
    row = lax.broadcasted_iota(jnp.int32, (SUBLANES, TP), 0)
    rt_ref[...] = jnp.where(row == 0, bucket, jnp.where(row == 1, rank, 0))
    run_ref[...] += jnp.sum(onehot.astype(F32), axis=1, keepdims=True)
    cnt_ref[...] = run_ref[...]


def _postmix(x, y_rnn, y_sgu, mod3, cond_of_tile, g_post, g_pre, w_out_b, router_wt, router_bt, earlier, pos_tab,
             counts0):
    n_tok = x.shape[0]
    n_tiles = n_tok // TP
    add_pos = pos_tab is not None
    tok = lambda i: (i, 0)
    const2 = lambda i: (0, 0)
    in_specs = [pl.BlockSpec((TP, D_MODEL), tok)]
    args = [x]
    if add_pos:
        reps = TP // GRID_W
        tiles_per_seq = GRID_W // reps
        in_specs += [pl.BlockSpec((None, reps, D_MODEL // 2), lambda i: (i % tiles_per_seq, 0, 0)),
                     pl.BlockSpec((GRID_W, D_MODEL // 2), const2)]
        args += [pos_tab.reshape(tiles_per_seq, reps, D_MODEL // 2), pos_tab]
    in_specs += [pl.BlockSpec((TP, D_RNN), tok),
                 pl.BlockSpec((TP, D_SGU), tok),
                 pl.BlockSpec((1, 1, 6 * D_MODEL), lambda i: (cond_of_tile(i), 0, 0)),
                 pl.BlockSpec((1, D_MODEL), const2),
                 pl.BlockSpec((1, D_MODEL), const2),
                 pl.BlockSpec((D_MODEL, D_MODEL), const2),
                 pl.BlockSpec((ROUTER_ROWS, D_MODEL), const2),
                 pl.BlockSpec((ROUTER_ROWS, LANES), const2),
                 pl.BlockSpec((TP, TP), const2),
                 pl.BlockSpec((ROUTER_ROWS, LANES), const2)]
    args += [y_rnn, y_sgu, mod3, g_post, g_pre, w_out_b, router_wt, router_bt, earlier, counts0]
    counts_spec = pl.BlockSpec((ROUTER_ROWS, LANES), const2)
    x1, hn, route, counts = pl.pallas_call(
        functools.partial(_postmix_kernel, add_pos=add_pos),
        out_shape=(jax.ShapeDtypeStruct((n_tok, D_MODEL), F32),
                   jax.ShapeDtypeStruct((n_tok * ROW_TILES, LANES), F32),
                   jax.ShapeDtypeStruct((n_tiles * SUBLANES, TP), jnp.int32),
                   jax.ShapeDtypeStruct((ROUTER_ROWS, LANES), F32)),
        grid=(n_tiles,),
        in_specs=in_specs,
        out_specs=(pl.BlockSpec((TP, D_MODEL), tok),
                   pl.BlockSpec((TP * ROW_TILES, LANES), tok),
                   pl.BlockSpec((SUBLANES, TP), tok),
                   counts_spec),
        scratch_shapes=[pltpu.VMEM((ROUTER_ROWS, LANES), F32)],
        compiler_params=_params(("arbitrary",)),
        name="postmix",
    )(*args)
    route = route.reshape(n_tiles, SUBLANES, TP)
    return x1, hn, route[:, 0].reshape(n_tok), route[:, 1].reshape(n_tok), counts


def _token_rows(ref, t):
    return ref.at[pl.ds(pl.multiple_of(t * ROW_TILES, ROW_TILES), ROW_TILES), :]


def _dispatch_kernel(dest_ref, hc_ref, hs_ref, xs_ref, sem, *, n_ctx_steps):
    i = pl.program_id(0)
    base = i * TD

    def scatter(src_ref):
        def start(g, carry):
            for u in range(DMA_UNROLL):
                r = g * DMA_UNROLL + u
                pltpu.make_async_copy(_token_rows(src_ref, r), _token_rows(xs_ref, dest_ref[base + r]),
                                      sem).start(priority=u % 2)
            return carry

        lax.fori_loop(0, TD // DMA_UNROLL, start, 0)
        pltpu.make_async_copy(src_ref, xs_ref.at[pl.ds(0, TD * ROW_TILES), :], sem).wait()

    @pl.when(i < n_ctx_steps)
    def _():
        scatter(hc_ref)

    @pl.when(i >= n_ctx_steps)
    def _():
        scatter(hs_ref)


def _dispatch(dest, hn_ctx, hn_dec, n_slots):
    n_ctx_steps = hn_ctx.shape[0] // (TD * ROW_TILES)
    n_dec_steps = hn_dec.shape[0] // (TD * ROW_TILES)
    return pl.pallas_call(
        functools.partial(_dispatch_kernel, n_ctx_steps=n_ctx_steps),
        out_shape=jax.ShapeDtypeStruct((n_slots * ROW_TILES, LANES), F32),
        grid_spec=pltpu.PrefetchScalarGridSpec(
            num_scalar_prefetch=1,
            grid=(n_ctx_steps + n_dec_steps,),
            in_specs=[pl.BlockSpec((TD * ROW_TILES, LANES), lambda i, d: (jnp.minimum(i, n_ctx_steps - 1), 0)),
                      pl.BlockSpec((TD * ROW_TILES, LANES), lambda i, d: (jnp.maximum(i - n_ctx_steps, 0), 0))],
            out_specs=pl.BlockSpec(memory_space=pl.ANY),
            scratch_shapes=[pltpu.SemaphoreType.DMA(())]),
        compiler_params=_params(("arbitrary",)),
        name="dispatch",
    )(dest, hn_ctx, hn_dec)


def _experts_kernel(ea_ref, eb_ref, nv_ref, blk_ref, xs_ref, rw_ref, rb_ref, *refs):
    w32_refs, ys_ref, w_refs = refs[:6], refs[6], refs[7:]
    wga_ref, wua_ref, wda_ref, wgb_ref, wub_ref, wdb_ref = w_refs
    i = pl.program_id(0)
    nv = nv_ref[i]
    prev = jnp.maximum(i - 1, 0)

    for e_ref, first in ((ea_ref, 0), (eb_ref, 3)):
        @pl.when((i == 0) | (e_ref[i] != e_ref[prev]))
        def _(first=first):
            for w32_ref, w_ref in zip(w32_refs[first:first + 3], w_refs[first:first + 3]):
                w_ref[...] = w32_ref[0].astype(BF16)

    @pl.when(nv > 0)
    def _():
        row = lax.broadcasted_iota(jnp.int32, (TMX, 1), 0)
        xb = jnp.where(row < nv, _load_token_major(xs_ref, TMX), 0.0).astype(BF16)
        logits = jnp.dot(xb, rw_ref[...], preferred_element_type=F32) + rb_ref[...]
        lane = lax.broadcasted_iota(jnp.int32, logits.shape, 1)
        ea = ea_ref[i]
        eb = eb_ref[i]
        gmask = lane < N_GROUPS
        gl = jnp.where(gmask, logits, -jnp.inf)
        gmax = jnp.max(gl, axis=-1, keepdims=True)
        gexp = jnp.where(gmask, jnp.exp(gl - gmax), 0.0)
        g_own = jnp.sum(jnp.where(lane == ea // EXPERTS_PER_GROUP, gexp, 0.0), axis=-1, keepdims=True)
        g_w = g_own / jnp.sum(gexp, axis=-1, keepdims=True)
        la = jnp.sum(jnp.where(lane == ea + E_LANE0, logits, 0.0), axis=-1, keepdims=True)
        lb = jnp.sum(jnp.where(lane == eb + E_LANE0, logits, 0.0), axis=-1, keepdims=True)
        m = jnp.maximum(la, lb)
        pa = jnp.exp(la - m)
        pb = jnp.exp(lb - m)
        inv = g_w / (pa + pb)

        def hidden(x, wg_ref, wu_ref, w):
            g = jnp.dot(x, wg_ref[...], preferred_element_type=F32)
            u = jnp.dot(x, wu_ref[...], preferred_element_type=F32)
            return ((g * _sigmoid(g)) * u * w).astype(BF16)

        part = TMX // EXPERT_ROW_PARTS
        ys = []
        for h in range(EXPERT_ROW_PARTS):
            rows = slice(h * part, (h + 1) * part)
            act_a = hidden(xb[rows], wga_ref, wua_ref, (pa * inv)[rows])
            act_b = hidden(xb[rows], wgb_ref, wub_ref, (pb * inv)[rows])
            y = (jnp.dot(act_a, wda_ref[...], preferred_element_type=F32)
                 + jnp.dot(act_b, wdb_ref[...], preferred_element_type=F32))
            ys.append(y)
        _store_token_major(ys_ref, jnp.concatenate(ys, axis=0))


def _experts(sched, xs, router_w, router_b, wg, wu, wd):
    ea, eb, nv, blk = sched
    n_tiles = ea.shape[0]
    rows = lambda i, ea, eb, nv, blk: (blk[i], 0)
    const2 = lambda i, ea, eb, nv, blk: (0, 0)
    exp_a = lambda i, ea, eb, nv, blk: (ea[i], 0, 0)
    exp_b = lambda i, ea, eb, nv, blk: (eb[i], 0, 0)
    w_in_spec = lambda m: pl.BlockSpec((1, D_MODEL, D_EXPERT), m)
    w_out_spec = lambda m: pl.BlockSpec((1, D_EXPERT, D_MODEL), m)
    return pl.pallas_call(
        _experts_kernel,
        out_shape=jax.ShapeDtypeStruct(xs.shape, F32),
        grid_spec=pltpu.PrefetchScalarGridSpec(
            num_scalar_prefetch=4,
            grid=(n_tiles,),
            in_specs=[pl.BlockSpec((TMX * ROW_TILES, LANES), rows),
                      pl.BlockSpec((D_MODEL, ROUTER_LANES), const2),
                      pl.BlockSpec((1, ROUTER_LANES), const2),
                      w_in_spec(exp_a), w_in_spec(exp_a), w_out_spec(exp_a),
                      w_in_spec(exp_b), w_in_spec(exp_b), w_out_spec(exp_b)],
            out_specs=pl.BlockSpec((TMX * ROW_TILES, LANES), rows),
            scratch_shapes=[pltpu.VMEM((D_MODEL, D_EXPERT), BF16), pltpu.VMEM((D_MODEL, D_EXPERT), BF16),
                            pltpu.VMEM((D_EXPERT, D_MODEL), BF16)] * 2),
        compiler_params=_params(("arbitrary",)),
        name="experts",
    )(ea, eb, nv, blk, xs, router_w, router_b, wg, wu, wd, wg, wu, wd)


def _combine_kernel(dest_ref, ys_ref, x1_ref, mod_ref, gpost_ref, o_ref, ybuf, sems, *, tile0):
    i = pl.program_id(0)
    n = pl.num_programs(0)

    def fetch(tile, slot):
        base = (tile + tile0) * TC

        def start(g, carry):
            for u in range(DMA_UNROLL):
                r = g * DMA_UNROLL + u
                pltpu.make_async_copy(_token_rows(ys_ref, dest_ref[base + r]), _token_rows(ybuf.at[slot], r),
                                      sems.at[slot]).start(priority=u % 2)
            return carry

        lax.fori_loop(0, TC // DMA_UNROLL, start, 0)

    @pl.when(i == 0)
    def _():
        fetch(0, 0)

    @pl.when(i + 1 < n)
    def _():
        fetch(i + 1, (i + 1) % 2)

    slot = i % 2
    pltpu.make_async_copy(ys_ref.at[pl.ds(0, TC * ROW_TILES), :], ybuf.at[slot], sems.at[slot]).wait()
    gate2 = mod_ref[0, :, 5 * D_MODEL:6 * D_MODEL]
    o_ref[...] = x1_ref[...] + _rms(_load_token_major(ybuf.at[slot], TC)) * (gate2 * gpost_ref[...])


def _combine(dest, ys, x1, mod3, g_post, cond_of_tile, tile0):
    n_tok = x1.shape[0]
    return pl.pallas_call(
        functools.partial(_combine_kernel, tile0=tile0),
        out_shape=jax.ShapeDtypeStruct((n_tok, D_MODEL), F32),
        grid_spec=pltpu.PrefetchScalarGridSpec(
            num_scalar_prefetch=1,
            grid=(n_tok // TC,),
            in_specs=[pl.BlockSpec(memory_space=pl.ANY),
                      pl.BlockSpec((TC, D_MODEL), lambda i, d: (i, 0)),
                      pl.BlockSpec((1, 1, 6 * D_MODEL), lambda i, d: (cond_of_tile(i), 0, 0)),
                      pl.BlockSpec((1, D_MODEL), lambda i, d: (0, 0))],
            out_specs=pl.BlockSpec((TC, D_MODEL), lambda i, d: (i, 0)),
            scratch_shapes=[pltpu.VMEM((2, TC * ROW_TILES, LANES), F32), pltpu.SemaphoreType.DMA((2,))]),
        compiler_params=_params(("arbitrary",)),
        name="combine",
    )(dest, ys, x1, mod3, g_post)


def _schedule(bucket, rank, counts):
    n_tok = bucket.shape[0]
    n_max = n_tok // TMX + N_BUCKETS
    cnt = counts[:N_BUCKETS, 0].astype(jnp.int32)
    tiles = (cnt + TMX - 1) // TMX
    tile_end = jnp.cumsum(tiles)
    tile_start = tile_end - tiles
    ids = jnp.arange(N_BUCKETS, dtype=jnp.int32)
    slot0 = jnp.sum(jnp.where(bucket[:, None] == ids[None, :], (tile_start * TMX)[None, :], 0), axis=1)
    dest = slot0 + rank
    i = jnp.arange(n_max, dtype=jnp.int32)
    total = tile_end[-1]
    valid = i < total
    tb = jnp.sum((jnp.minimum(i, total - 1)[:, None] >= tile_end[None, :]).astype(jnp.int32), axis=1)
    pairs = [(a, b) for a in range(EXPERTS_PER_GROUP) for b in range(a + 1, EXPERTS_PER_GROUP)]
    ea_tab = jnp.array([g * EXPERTS_PER_GROUP + a for g in range(N_GROUPS) for a, _ in pairs], jnp.int32)
    eb_tab = jnp.array([g * EXPERTS_PER_GROUP + b for g in range(N_GROUPS) for _, b in pairs], jnp.int32)
    hit = tb[:, None] == ids[None, :]
    look = lambda tab: jnp.sum(jnp.where(hit, tab[None, :], 0), axis=1)
    ea, eb = look(ea_tab), look(eb_tab)
    nv = jnp.where(valid, jnp.clip(look(cnt) - (i - look(tile_start)) * TMX, 0, TMX), 0)
    blk = jnp.minimum(i, total - 1)
    return dest, (ea, eb, nv, blk), n_max * TMX


def _block_diag_gates(rg_wa, rg_wx):
    heads = LANES // HEAD_RNN

    def bd(w):
        w = w.reshape(2, RNN_BLOCKS, heads, HEAD_RNN, HEAD_RNN)
        eye = jnp.eye(heads, dtype=w.dtype)
        full = jnp.einsum('dghij,hk->dghikj', w, eye)
        return full.reshape(2, RNN_BLOCKS, LANES, LANES)

    return jnp.concatenate([bd(rg_wa), bd(rg_wx)], axis=-1).astype(BF16)


def _row_tile(v):
    blocks = v.reshape(v.shape[:-1] + (RNN_BLOCKS, LANES))
    return jnp.concatenate([blocks] * PAIR, axis=-2)


def _to_time_major_state(h):
    return h.reshape(h.shape[0] // PAIR, TMJ_ROWS, LANES)


def kernel(x_prompt, x_sample, state_rglru, c, c_ctx, w_mod, b_mod, g_pre_mix, g_post_mix, g_pre_ffn,
           g_post_ffn, w_in, conv_w, conv_b, rg_wa, rg_ba, rg_wx, rg_bx, rg_lambda, sgu_g, sgu_w, sgu_b,
           w_out, router_g_w, router_g_b, router_e_w, router_e_b, exp_w_gate, exp_w_up, exp_w_down):
    assert w_mod.shape[0] == 1, "single-layer trunk"
    n_ctx, ctx_len, _ = x_prompt.shape
    n_dec, dec_len, _ = x_sample.shape
    l = 0

    n_cond = SUBLANES
    ctx_rows = n_cond - n_dec
    ctx_per_step = max(PAIR, PREMIX_ROWS // min(ctx_len, TT))
    assert n_dec % PAIR == 0 and ctx_rows == ctx_per_step and n_dec % ctx_per_step == 0
    cond = jnp.concatenate([c, jnp.broadcast_to(c_ctx, (ctx_rows, D_MODEL))], axis=0)
    mod3 = _modulation(cond, w_mod[l], b_mod[l]).reshape(n_cond, 1, 6 * D_MODEL)
    pos_tab = _pos_table()

    w_rx, w_gate, w_u, w_v = jnp.split(w_in[l], [D_RNN, 2 * D_RNN, 2 * D_RNN + D_SGU], axis=1)
    w_in_b = jnp.concatenate([w_v, w_u, w_gate, w_rx], axis=1).astype(BF16)
    w_out_b = w_out[l].astype(BF16)
    sgu_w_b = sgu_w[l].reshape(2, 4 * CHUNK, CHUNK).astype(BF16)
    sgu_bias_tile = jnp.repeat(sgu_b[l].T, HEAD_SGU, axis=1)
    w_gates = _block_diag_gates(rg_wa[l], rg_wx[l])
    b_gates = 0.5 * jnp.stack([_row_tile(rg_ba[l]), _row_tile(rg_bx[l])], axis=1)
    lam = _row_tile(rg_lambda[l])
    conv_w_t = 0.5 * _row_tile(conv_w[l])
    conv_b_t = 0.5 * _row_tile(conv_b[l])
    lane_pad = ROUTER_LANES - E_LANE0 - N_EXPERTS
    router_w = jnp.pad(jnp.concatenate([router_g_w[l], router_e_w[l]], axis=1), ((0, 0), (0, lane_pad))).astype(BF16)
    router_b = jnp.pad(jnp.concatenate([router_g_b[l], router_e_b[l]]), (0, lane_pad)).reshape(1, ROUTER_LANES)
    gap, tail = E_ROW0 - N_GROUPS, ROUTER_ROWS - E_ROW0 - N_EXPERTS
    router_wt = jnp.concatenate([router_g_w[l].T, jnp.zeros((gap, D_MODEL), F32), router_e_w[l].T,
                                 jnp.zeros((tail, D_MODEL), F32)], axis=0).astype(BF16)
    router_bt = jnp.concatenate([router_g_b[l], jnp.zeros((gap,), F32), router_e_b[l], jnp.zeros((tail,), F32)])
    router_bt = jnp.broadcast_to(router_bt[:, None], (ROUTER_ROWS, LANES))
    earlier = jnp.triu(jnp.ones((TP, TP), BF16), k=1)
    row = lambda v: v.reshape(1, -1)

    n_ctx_tok = n_ctx * ctx_len

    def mixer(x, h0, cond_of_tile, cond_block, use_pos, counts0):
        n_seq, seq_len, _ = x.shape
        xf = x.reshape(n_seq * seq_len, D_MODEL)
        tab = pos_tab if use_pos else None
        xr, gg, y_sgu = _premix(x, mod3, cond_block, row(g_pre_mix[l]), w_in_b, row(sgu_g[l]),
                                sgu_w_b, sgu_bias_tile, tab)
        scan_params = (conv_w_t, conv_b_t, w_gates, b_gates, lam)
        hf, hf_last = _scan(xr, None, None, _to_time_major_state(h0[:, 0]), *scan_params, direction=0)
        y_rnn, hb_first = _scan(xr, gg, hf, _to_time_major_state(h0[:, 1]), *scan_params, direction=1)
        fstate = jnp.stack([hf_last.reshape(n_seq, D_RNN), hb_first.reshape(n_seq, D_RNN)], axis=1)
        x1, hn, bucket, rank, counts = _postmix(
            xf, y_rnn.reshape(n_seq * seq_len, D_RNN), y_sgu.reshape(n_seq * seq_len, D_SGU), mod3,
            cond_of_tile, row(g_post_mix[l]), row(g_pre_ffn[l]), w_out_b, router_wt, router_bt, earlier, tab,
            counts0)
        return x1, hn, bucket, rank, counts, fstate

    ctx_cond = lambda i: n_dec
    dec_cond = lambda i: i // (dec_len // TP)
    h0_ctx = jnp.zeros((n_ctx, 2, D_RNN), F32)
    counts0 = jnp.zeros((ROUTER_ROWS, LANES), F32)
    x1_ctx, hn_ctx, bucket_ctx, rank_ctx, counts, st = mixer(x_prompt, h0_ctx, ctx_cond, lambda p: n_dec // ctx_per_step,
                                                             False, counts0)
    new_state = st.astype(state_rglru.dtype)[:, None]
    x1_dec, hn_dec, bucket_dec, rank_dec, counts, _ = mixer(x_sample, state_rglru[:, l].astype(F32), dec_cond,
                                                            lambda p: p, True, counts)

    dest, sched, n_slots = _schedule(jnp.concatenate([bucket_ctx, bucket_dec]),
                                     jnp.concatenate([rank_ctx, rank_dec]), counts)
    xs = _dispatch(dest, hn_ctx, hn_dec, n_slots)
    ys = _experts(sched, xs, router_w, router_b, exp_w_gate[l], exp_w_up[l], exp_w_down[l])
    y_prompt = _combine(dest, ys, x1_ctx, mod3, row(g_post_ffn[l]), ctx_cond, 0)
    y_sample = _combine(dest, ys, x1_dec, mod3, row(g_post_ffn[l]), lambda i: i // (dec_len // TC), n_ctx_tok // TC)
    return (y_prompt.reshape(x_prompt.shape), y_sample.reshape(x_sample.shape), new_state)
```

```python
import functools
import math

import jax
import jax.numpy as jnp
from jax import lax
from jax.experimental import pallas as pl
from jax.experimental.pallas import tpu as pltpu

D_MODEL = 1024
D_RNN = 512
D_SGU = 512
N_HEADS_RNN = 8
HEAD_RNN = D_RNN // N_HEADS_RNN
N_HEADS_SGU = 8
HEAD_SGU = D_SGU // N_HEADS_SGU
CHUNK = 128
GRID_W = 64
RG_C = 8.0
N_GROUPS = 4
EXPERTS_PER_GROUP = 4
N_EXPERTS = N_GROUPS * EXPERTS_PER_GROUP
D_EXPERT = 512
EPS = 1e-6
POS_BASE = 10000.0

LANES = 128
SUBLANES = 8
CONV_W = 4
CONV_LEFT = 2
PAIR = 2
RNN_BLOCKS = D_RNN // LANES
TMJ_ROWS = PAIR * RNN_BLOCKS
ROUTER_LANES = LANES
E_LANE0 = N_GROUPS
ROUTER_ROWS = 32
E_ROW0 = SUBLANES

PAIRS_PER_GROUP = EXPERTS_PER_GROUP * (EXPERTS_PER_GROUP - 1) // 2
N_BUCKETS = N_GROUPS * PAIRS_PER_GROUP

ROW_TILES = D_MODEL // LANES

MOD_COLS = 2048
TP = 1024
TD = 2048
TC = 512
DMA_UNROLL = 8
EXPERT_ROW_PARTS = 2
TMX = 512
TT = 512
PREMIX_ROWS = 1024
LC = 256
TS = 16
PB = 2
VMEM_LIMIT = 56 * 1024 * 1024

F32 = jnp.float32
BF16 = jnp.bfloat16


def _params(sem):
    return pltpu.CompilerParams(dimension_semantics=sem, vmem_limit_bytes=VMEM_LIMIT)


def _rms(x):
    return x * lax.rsqrt(jnp.mean(x * x, axis=-1, keepdims=True) + EPS)


def _sigmoid(x):
    return 0.5 * jnp.tanh(0.5 * x) + 0.5


def _mod_kernel(cond_ref, w_ref, b_ref, o_ref):
    c = cond_ref[...]
    s = c * _sigmoid(c)
    o_ref[...] = jnp.dot(s.astype(BF16), w_ref[...].astype(BF16),
                         preferred_element_type=F32) + b_ref[...]


def _modulation(cond, w_mod, b_mod):
    n = w_mod.shape[1]
    return pl.pallas_call(
        _mod_kernel,
        out_shape=jax.ShapeDtypeStruct((cond.shape[0], n), F32),
        grid=(n // MOD_COLS,),
        in_specs=[pl.BlockSpec(cond.shape, lambda j: (0, 0)),
                  pl.BlockSpec((D_MODEL, MOD_COLS), lambda j: (0, j)),
                  pl.BlockSpec((1, MOD_COLS), lambda j: (0, j))],
        out_specs=pl.BlockSpec((cond.shape[0], MOD_COLS), lambda j: (0, j)),
        compiler_params=_params(("arbitrary",)),
        name="modulation",
    )(cond, w_mod, b_mod.reshape(1, n))


def _pos_kernel(o_ref):
    n_freq = D_MODEL // 4
    k = lax.broadcasted_iota(jnp.int32, (GRID_W, n_freq), 1).astype(F32)
    p = lax.broadcasted_iota(jnp.int32, (GRID_W, n_freq), 0).astype(F32)
    freq = jnp.exp(-math.log(POS_BASE) * k / n_freq)
    ang = p * freq
    o_ref[:, 0:n_freq] = jnp.sin(ang)
    o_ref[:, n_freq:2 * n_freq] = jnp.cos(ang)


def _pos_table():
    return pl.pallas_call(
        _pos_kernel,
        out_shape=jax.ShapeDtypeStruct((GRID_W, D_MODEL // 2), F32),
        name="pos_table",
    )()


def _add_pos(x, pos_refs, q0):
    if pos_refs is None:
        return x
    rows_ref, cols_ref = pos_refs
    reps = x.shape[0] // GRID_W
    rpart = jnp.concatenate(
        [jnp.broadcast_to(rows_ref[q:q + 1, :], (GRID_W, D_MODEL // 2)) for q in range(q0, q0 + reps)], axis=0)
    cpart = jnp.concatenate([cols_ref[...]] * reps, axis=0)
    return jnp.concatenate([x[:, :D_MODEL // 2] + rpart, x[:, D_MODEL // 2:] + cpart], axis=1)


def _load_x(x_ref, pos_refs, r0, n):
    return _add_pos(x_ref[r0:r0 + n, :], pos_refs, r0 // GRID_W)


def _premix_kernel(*refs, add_pos):
    refs = list(refs)
    x_ref = refs.pop(0)
    nseq, tt = x_ref.shape[:2]
    pos_refs = (refs.pop(0), refs.pop(0)) if add_pos else None
    mod_ref, g_ref, win_ref, sgug_ref, sguw_ref, sgub_ref, xr_ref, gg_ref, ys_ref = refs
    hn = []
    for s in range(nseq):
        shift = mod_ref[s, :, 0:D_MODEL]
        scale = mod_ref[s, :, D_MODEL:2 * D_MODEL]
        hn.append(_rms(_add_pos(x_ref[s], pos_refs, 0)) * (g_ref[...] * (1.0 + scale)) + shift)
    z = jnp.dot(jnp.concatenate(hn, axis=0).astype(BF16), win_ref[...],
                preferred_element_type=F32)
    half = D_SGU // 2
    heads_per_half = N_HEADS_SGU // 2
    lane_head = lax.broadcasted_iota(jnp.int32, (CHUNK, half), 1) // HEAD_SGU
    for s in range(nseq):
        zs = z[s * tt:(s + 1) * tt]
        vn = (_rms(zs[:, 0:D_SGU]) * sgug_ref[...]).astype(BF16)
        u = zs[:, D_SGU:2 * D_SGU]
        for c in range(tt // CHUNK):
            rows = slice(c * CHUNK, (c + 1) * CHUNK)
            halves = []
            for hf in range(2):
                r = jnp.dot(sguw_ref[hf], vn[rows, hf * half:(hf + 1) * half],
                            preferred_element_type=F32)
                sel = jnp.zeros((CHUNK, half), F32)
                for h in range(heads_per_half):
                    sel = jnp.where(lane_head == h, r[h * CHUNK:(h + 1) * CHUNK], sel)
                halves.append(sel)
            gatev = jnp.concatenate(halves, axis=1) + sgub_ref[...]
            ys_ref[s, rows, :] = (u[rows] * gatev).astype(BF16)
    for s in range(nseq):
        zs = z[s * tt:(s + 1) * tt]
        gg = jax.nn.gelu(zs[:, 2 * D_SGU:2 * D_SGU + D_RNN])
        xr = zs[:, 2 * D_SGU + D_RNN:]
        for k in range(RNN_BLOCKS):
            first = (s // PAIR) * tt * TMJ_ROWS + (s % PAIR) * RNN_BLOCKS + k
            rows = pl.ds(first, tt, stride=TMJ_ROWS)
            xr_ref[rows, :] = xr[:, k * LANES:(k + 1) * LANES]
            gg_ref[rows, :] = gg[:, k * LANES:(k + 1) * LANES]


def _premix(x, mod3, cond_block, g_pre, w_in_b, sgu_g, sgu_w_b, sgu_bias_tile, pos_tab):
    n_seq, seq_len, _ = x.shape
    tt = min(seq_len, TT)
    nseq = max(PAIR, PREMIX_ROWS // tt)
    assert nseq == PAIR or tt == seq_len
    n_pairs, n_tiles = n_seq // PAIR, seq_len // tt
    add_pos = pos_tab is not None
    const2 = lambda p, j: (0, 0)
    in_specs = [pl.BlockSpec((nseq, tt, D_MODEL), lambda p, j: (p, j, 0))]
    args = [x]
    if add_pos:
        reps = tt // GRID_W
        in_specs += [pl.BlockSpec((None, reps, D_MODEL // 2), lambda p, j: (j, 0, 0)),
                     pl.BlockSpec((GRID_W, D_MODEL // 2), const2)]
        args += [pos_tab.reshape(GRID_W // reps, reps, D_MODEL // 2), pos_tab]
    in_specs += [pl.BlockSpec((nseq, 1, 6 * D_MODEL), lambda p, j: (cond_block(p), 0, 0)),
                 pl.BlockSpec((1, D_MODEL), const2),
                 pl.BlockSpec((D_MODEL, 2 * D_RNN + 2 * D_SGU), const2),
                 pl.BlockSpec((1, D_SGU), const2),
                 pl.BlockSpec((2, 4 * CHUNK, CHUNK), lambda p, j: (0, 0, 0)),
                 pl.BlockSpec((CHUNK, D_SGU), const2)]
    args += [mod3, g_pre, w_in_b, sgu_g, sgu_w_b, sgu_bias_tile]
    tmj = jax.ShapeDtypeStruct((n_pairs * seq_len * TMJ_ROWS, LANES), F32)
    tmj_spec = pl.BlockSpec((nseq // PAIR * tt * TMJ_ROWS, LANES), lambda p, j: (p * n_tiles + j, 0))
    xr, gg, y_sgu = pl.pallas_call(
        functools.partial(_premix_kernel, add_pos=add_pos),
        out_shape=(tmj, tmj, jax.ShapeDtypeStruct((n_seq, seq_len, D_SGU), BF16)),
        grid=(n_seq // nseq, n_tiles),
        in_specs=in_specs,
        out_specs=(tmj_spec, tmj_spec, pl.BlockSpec((nseq, tt, D_SGU), lambda p, j: (p, j, 0))),
        compiler_params=_params(("parallel", "parallel")),
        name="premix",
    )(*args)
    shape4 = (n_pairs, seq_len, TMJ_ROWS, LANES)
    return xr.reshape(shape4), gg.reshape(shape4), y_sgu


def _scan_kernel(*refs, reverse, n_chunks):
    if reverse:
        (xprev_ref, x_ref, xnext_ref, gg_ref, hf_ref, h0_ref, cw_ref, cb_ref, wg_ref, bg_ref, lam_ref,
         y_ref, fs_ref, xwin, xc_s, r_s, i_s, a_s, b_s, y_s, hcar) = refs
    else:
        (xprev_ref, x_ref, xnext_ref, h0_ref, cw_ref, cb_ref, wg_ref, bg_ref, lam_ref,
         hf_ref, fs_ref, xwin, xc_s, r_s, i_s, a_s, b_s, hcar) = refs
    c = pl.program_id(1)
    chunk = n_chunks - 1 - c if reverse else c
    sub_rows = TS * TMJ_ROWS

    def rows_of(pb, t0, n_steps):
        first = (pb * LC + t0) * TMJ_ROWS
        if not isinstance(first, int):
            first = pl.multiple_of(first, TMJ_ROWS)
        return pl.ds(first, n_steps * TMJ_ROWS)

    @pl.when(c == 0)
    def _():
        hcar[...] = h0_ref[...]

    xwin[:, 0:CONV_LEFT] = jnp.where(chunk > 0, xprev_ref[...], 0.0)
    xwin[:, LC + CONV_LEFT:LC + CONV_W - 1] = jnp.where(chunk < n_chunks - 1, xnext_ref[...], 0.0)

    xwin[:, CONV_LEFT:CONV_LEFT + LC] = x_ref[...]

    neg_lam = -lam_ref[...]
    softplus = jnp.maximum(neg_lam, 0.0) + jnp.log(1.0 + jnp.exp(-jnp.abs(neg_lam)))
    half_decay = (-0.5 * RG_C * math.log2(math.e)) * softplus

    def conv(pb):
        for t0 in range(0, LC, TS):
            xc = cb_ref[...] + cw_ref[0] * xwin[pb, t0:t0 + TS]
            for k in range(1, CONV_W):
                xc = xc + cw_ref[k] * xwin[pb, t0 + k:t0 + k + TS]
            xc_s[rows_of(pb, t0, TS), :] = xc.reshape(sub_rows, LANES)

    def gate_matmuls(pb):
        for k in range(RNN_BLOCKS):
            rows = pl.ds(pb * LC * TMJ_ROWS + k, LC * PAIR, stride=RNN_BLOCKS)
            g = jnp.dot(xc_s[rows, :].astype(BF16), wg_ref[k], preferred_element_type=F32)
            r_s[rows, :] = g[:, :LANES]
            i_s[rows, :] = g[:, LANES:]

    def gates(pb):
        for t0 in range(0, LC, TS):
            rows = rows_of(pb, t0, TS)
            tile = lambda ref: ref[rows, :].reshape(TS, TMJ_ROWS, LANES)
            tr = jnp.tanh(tile(r_s) + bg_ref[0])
            ti = jnp.tanh(tile(i_s) + bg_ref[1])
            log2_a = tr * half_decay + half_decay
            a = jnp.exp2(log2_a)
            q = jnp.tanh(log2_a * (-math.log(2.0))) * (a * a + 1.0)
            b = jnp.where(q > 0.0, q * lax.rsqrt(q), 0.0) * ((ti + 1.0) * tile(xc_s))
            a_s[rows, :] = a.reshape(sub_rows, LANES)
            b_s[rows, :] = b.reshape(sub_rows, LANES)

    for stage in (conv, gate_matmuls, gates):
        for pb in range(PB):
            stage(pb)

    half_steps = LC // 2

    def put(pb, t, rows, h):
        if reverse:
            y_s[rows, :] = h
        else:
            hf_ref[pb, t] = h

    def step(j, carry):
        lead, lag, prod = carry
        t_lead = LC - 1 - j if reverse else j
        t_lag = half_steps - 1 - j if reverse else half_steps + j
        out = ([], [], [])
        for pb in range(PB):
            rows_lead, rows_lag = rows_of(pb, t_lead, 1), rows_of(pb, t_lag, 1)
            a_lag = a_s[rows_lag, :]
            h_lead = a_s[rows_lead, :] * lead[pb] + b_s[rows_lead, :]
            h_lag = a_lag * lag[pb] + b_s[rows_lag, :]
            p_lag = a_lag * prod[pb]
            put(pb, t_lead, rows_lead, h_lead)
            put(pb, t_lag, rows_lag, h_lag)
            r_s[rows_lag, :] = p_lag
            for acc, v in zip(out, (h_lead, h_lag, p_lag)):
                acc.append(v)
        return tuple(tuple(acc) for acc in out)

    zeros = tuple(jnp.zeros((TMJ_ROWS, LANES), F32) for _ in range(PB))
    ones = tuple(jnp.ones((TMJ_ROWS, LANES), F32) for _ in range(PB))
    lead, lag, prod = lax.fori_loop(0, half_steps, step, (tuple(hcar[pb] for pb in range(PB)), zeros, ones),
                                    unroll=8)
    lag_first = 0 if reverse else half_steps

    def completion(pb, t0):
        prods = r_s[rows_of(pb, t0, TS), :].reshape(TS, TMJ_ROWS, LANES)
        return (prods * lead[pb]).reshape(sub_rows, LANES)

    for pb in range(PB):
        final = lag[pb] + prod[pb] * lead[pb]
        hcar[pb] = final
        fs_ref[pb] = final
        if not reverse:
            for t0 in range(lag_first, lag_first + half_steps, TS):
                hf_ref[pb, t0:t0 + TS] = hf_ref[pb, t0:t0 + TS] + completion(pb, t0).reshape(TS, TMJ_ROWS, LANES)

    if reverse:
        for pb in range(PB):
            for t0 in range(0, LC, TS):
                rows = rows_of(pb, t0, TS)
                both = hf_ref[pb, t0:t0 + TS].reshape(sub_rows, LANES) + y_s[rows, :]
                if lag_first <= t0 < lag_first + half_steps:
                    both = both + completion(pb, t0)
                y_s[rows, :] = both * gg_ref[pb, t0:t0 + TS].reshape(sub_rows, LANES)
            for s in range(PAIR):
                cols = [y_s[pl.ds(pb * LC * TMJ_ROWS + s * RNN_BLOCKS + k, LC, stride=TMJ_ROWS), :]
                        for k in range(RNN_BLOCKS)]
                y_ref[pb * PAIR + s] = jnp.concatenate(cols, axis=1).astype(BF16)


def _scan(xr, gg, hf, h0, conv_w, conv_b, w_gates, b_gates, lam, direction):
    n_pairs, seq_len = xr.shape[:2]
    n_chunks = seq_len // LC
    reverse = direction == 1
    pos = (lambda c: n_chunks - 1 - c) if reverse else (lambda c: c)
    tmj_blk = pl.BlockSpec((PB, LC, TMJ_ROWS, LANES), lambda i, c: (i, pos(c), 0, 0))
    state_blk = pl.BlockSpec((PB, TMJ_ROWS, LANES), lambda i, c: (i, 0, 0))
    per_dir = lambda *shape: pl.BlockSpec((None,) + shape, lambda i, c: (direction,) + (0,) * len(shape))
    in_specs = [
        pl.BlockSpec((PB, CONV_LEFT, TMJ_ROWS, LANES),
                     lambda i, c: (i, jnp.maximum(pos(c) * (LC // CONV_LEFT) - 1, 0), 0, 0)),
        tmj_blk,
        pl.BlockSpec((PB, 1, TMJ_ROWS, LANES), lambda i, c: (i, jnp.minimum((pos(c) + 1) * LC, seq_len - 1), 0, 0)),
    ]
    args = [xr, xr, xr]
    if reverse:
        in_specs += [tmj_blk, tmj_blk]
        args += [gg, hf]
    in_specs += [state_blk,
                 pl.BlockSpec((CONV_W, TMJ_ROWS, LANES), lambda i, c: (0, 0, 0)),
                 pl.BlockSpec((TMJ_ROWS, LANES), lambda i, c: (0, 0)),
                 per_dir(RNN_BLOCKS, LANES, 2 * LANES),
                 per_dir(2, TMJ_ROWS, LANES),
                 per_dir(TMJ_ROWS, LANES)]
    args += [h0, conv_w, conv_b, w_gates, b_gates, lam]
    flat = pltpu.VMEM((PB * LC * TMJ_ROWS, LANES), F32)
    scratch = [pltpu.VMEM((PB, LC + CONV_W - 1, TMJ_ROWS, LANES), F32)] + [flat] * (6 if reverse else 5)
    scratch += [pltpu.VMEM((PB, TMJ_ROWS, LANES), F32)]
    state = jax.ShapeDtypeStruct((n_pairs, TMJ_ROWS, LANES), F32)
    if reverse:
        out_shape = (jax.ShapeDtypeStruct((n_pairs * PAIR, seq_len, D_RNN), BF16), state)
        out_specs = (pl.BlockSpec((PB * PAIR, LC, D_RNN), lambda i, c: (i, pos(c), 0)), state_blk)
    else:
        out_shape = (jax.ShapeDtypeStruct(xr.shape, F32), state)
        out_specs = (tmj_blk, state_blk)
    return pl.pallas_call(
        functools.partial(_scan_kernel, reverse=reverse, n_chunks=n_chunks),
        out_shape=out_shape,
        grid=(n_pairs // PB, n_chunks),
        in_specs=in_specs,
        out_specs=out_specs,
        scratch_shapes=scratch,
        compiler_params=_params(("parallel", "arbitrary")),
        name="scan_bwd" if reverse else "scan_fwd",
    )(*args)


def _route(lt):
    n = lt.shape[1]
    row = lax.broadcasted_iota(jnp.int32, (EXPERTS_PER_GROUP, n), 0)
    neg = jnp.float32(-jnp.inf)

    def arg_max(v):
        m = jnp.max(v, axis=0, keepdims=True)
        return jnp.min(jnp.where(v == m, row, EXPERTS_PER_GROUP), axis=0, keepdims=True)

    g_idx = arg_max(lt[0:N_GROUPS])
    el = lt[E_ROW0:E_ROW0 + EXPERTS_PER_GROUP]
    for g in range(1, N_GROUPS):
        first = E_ROW0 + g * EXPERTS_PER_GROUP
        el = jnp.where(g_idx == g, lt[first:first + EXPERTS_PER_GROUP], el)
    i1 = arg_max(el)
    i2 = arg_max(jnp.where(row == i1, neg, el))
    ja = jnp.minimum(i1, i2)
    jb = jnp.maximum(i1, i2)
    pair = (ja * (2 * EXPERTS_PER_GROUP - 1 - ja)) // 2 + (jb - ja - 1)
    return g_idx * PAIRS_PER_GROUP + pair


def _store_token_major(ref, x, t0=0):
    n = x.shape[0]
    for k in range(ROW_TILES):
        ref[pl.ds(t0 * ROW_TILES + k, n, stride=ROW_TILES), :] = x[:, k * LANES:(k + 1) * LANES]


def _load_token_major(ref, n):
    return jnp.concatenate([ref[pl.ds(k, n, stride=ROW_TILES), :] for k in range(ROW_TILES)], axis=1)


def _postmix_kernel(*refs, add_pos):
    refs = list(refs)
    x_ref = refs.pop(0)
    pos_refs = (refs.pop(0), refs.pop(0)) if add_pos else None
    (yr_ref, ys_ref, mod_ref, gpost_ref, gpre_ref, wout_ref, rw_ref, rb_ref, earlier_ref, cnt0_ref,
     x1_ref, hn_ref, rt_ref, cnt_ref, run_ref) = refs

    @pl.when(pl.program_id(0) == 0)
    def _():
        run_ref[...] = cnt0_ref[...]

    gate1 = mod_ref[0, :, 2 * D_MODEL:3 * D_MODEL]
    shift2 = mod_ref[0, :, 3 * D_MODEL:4 * D_MODEL]
    scale2 = mod_ref[0, :, 4 * D_MODEL:5 * D_MODEL]
    y = (jnp.dot(yr_ref[...], wout_ref[0:D_RNN, :], preferred_element_type=F32)
         + jnp.dot(ys_ref[...], wout_ref[D_RNN:, :], preferred_element_type=F32))
    x1 = _load_x(x_ref, pos_refs, 0, TP) + _rms(y) * (gate1 * gpost_ref[...])
    x1_ref[...] = x1
    hn = _rms(x1) * (gpre_ref[...] * (1.0 + scale2)) + shift2
    _store_token_major(hn_ref, hn)
    lt = lax.dot_general(rw_ref[...], hn.astype(BF16), (((1,), (1,)), ((), ())),
                         preferred_element_type=F32) + rb_ref[:, 0:1]
    bucket = _route(lt)
    onehot = lax.broadcasted_iota(jnp.int32, (ROUTER_ROWS, TP), 0) == bucket
    before = jnp.dot(onehot.astype(BF16), earlier_ref[...], preferred_element_type=F32) + run_ref[:, 0:1]
    rank = jnp.sum(jnp.where(onehot, before, 0.0), axis=0, keepdims=True).astype(jnp.int32)
    row = lax.broadcasted_iota(jnp.int32, (SUBLANES, TP), 0)
    rt_ref[...] = jnp.where(row == 0, bucket, jnp.where(row == 1, rank, 0))
    run_ref[...] += jnp.sum(onehot.astype(F32), axis=1, keepdims=True)
    cnt_ref[...] = run_ref[...]


def _postmix(x, y_rnn, y_sgu, mod3, cond_of_tile, g_post, g_pre, w_out_b, router_wt, router_bt, earlier, pos_tab,
             counts0):
    n_tok = x.shape[0]
    n_tiles = n_tok // TP
    add_pos = pos_tab is not None
    tok = lambda i: (i, 0)
    const2 = lambda i: (0, 0)
    in_specs = [pl.BlockSpec((TP, D_MODEL), tok)]
    args = [x]
    if add_pos:
        reps = TP // GRID_W
        tiles_per_seq = GRID_W // reps
        in_specs += [pl.BlockSpec((None, reps, D_MODEL // 2), lambda i: (i % tiles_per_seq, 0, 0)),
                     pl.BlockSpec((GRID_W, D_MODEL // 2), const2)]
        args += [pos_tab.reshape(tiles_per_seq, reps, D_MODEL // 2), pos_tab]
    in_specs += [pl.BlockSpec((TP, D_RNN), tok),
                 pl.BlockSpec((TP, D_SGU), tok),
                 pl.BlockSpec((1, 1, 6 * D_MODEL), lambda i: (cond_of_tile(i), 0, 0)),
                 pl.BlockSpec((1, D_MODEL), const2),
                 pl.BlockSpec((1, D_MODEL), const2),
                 pl.BlockSpec((D_MODEL, D_MODEL), const2),
                 pl.BlockSpec((ROUTER_ROWS, D_MODEL), const2),
                 pl.BlockSpec((ROUTER_ROWS, LANES), const2),
                 pl.BlockSpec((TP, TP), const2),
                 pl.BlockSpec((ROUTER_ROWS, LANES), const2)]
    args += [y_rnn, y_sgu, mod3, g_post, g_pre, w_out_b, router_wt, router_bt, earlier, counts0]
    counts_spec = pl.BlockSpec((ROUTER_ROWS, LANES), const2)
    x1, hn, route, counts = pl.pallas_call(
        functools.partial(_postmix_kernel, add_pos=add_pos),
        out_shape=(jax.ShapeDtypeStruct((n_tok, D_MODEL), F32),
                   jax.ShapeDtypeStruct((n_tok * ROW_TILES, LANES), F32),
                   jax.ShapeDtypeStruct((n_tiles * SUBLANES, TP), jnp.int32),
                   jax.ShapeDtypeStruct((ROUTER_ROWS, LANES), F32)),
        grid=(n_tiles,),
        in_specs=in_specs,
        out_specs=(pl.BlockSpec((TP, D_MODEL), tok),
                   pl.BlockSpec((TP * ROW_TILES, LANES), tok),
                   pl.BlockSpec((SUBLANES, TP), tok),
                   counts_spec),
        scratch_shapes=[pltpu.VMEM((ROUTER_ROWS, LANES), F32)],
        compiler_params=_params(("arbitrary",)),
        name="postmix",
    )(*args)
    route = route.reshape(n_tiles, SUBLANES, TP)
    return x1, hn, route[:, 0].reshape(n_tok), route[:, 1].reshape(n_tok), counts


def _token_rows(ref, t):
    return ref.at[pl.ds(pl.multiple_of(t * ROW_TILES, ROW_TILES), ROW_TILES), :]


def _dispatch_kernel(dest_ref, hc_ref, hs_ref, xs_ref, sem, *, n_ctx_steps):
    i = pl.program_id(0)
    base = i * TD

    def scatter(src_ref):
        def start(g, carry):
            for u in range(DMA_UNROLL):
                r = g * DMA_UNROLL + u
                pltpu.make_async_copy(_token_rows(src_ref, r), _token_rows(xs_ref, dest_ref[base + r]),
                                      sem).start(priority=u % 2)
            return carry

        lax.fori_loop(0, TD // DMA_UNROLL, start, 0)
        pltpu.make_async_copy(src_ref, xs_ref.at[pl.ds(0, TD * ROW_TILES), :], sem).wait()

    @pl.when(i < n_ctx_steps)
    def _():
        scatter(hc_ref)

    @pl.when(i >= n_ctx_steps)
    def _():
        scatter(hs_ref)


def _dispatch(dest, hn_ctx, hn_dec, n_slots):
    n_ctx_steps = hn_ctx.shape[0] // (TD * ROW_TILES)
    n_dec_steps = hn_dec.shape[0] // (TD * ROW_TILES)
    return pl.pallas_call(
        functools.partial(_dispatch_kernel, n_ctx_steps=n_ctx_steps),
        out_shape=jax.ShapeDtypeStruct((n_slots * ROW_TILES, LANES), F32),
        grid_spec=pltpu.PrefetchScalarGridSpec(
            num_scalar_prefetch=1,
            grid=(n_ctx_steps + n_dec_steps,),
            in_specs=[pl.BlockSpec((TD * ROW_TILES, LANES), lambda i, d: (jnp.minimum(i, n_ctx_steps - 1), 0)),
                      pl.BlockSpec((TD * ROW_TILES, LANES), lambda i, d: (jnp.maximum(i - n_ctx_steps, 0), 0))],
            out_specs=pl.BlockSpec(memory_space=pl.ANY),
            scratch_shapes=[pltpu.SemaphoreType.DMA(())]),
        compiler_params=_params(("arbitrary",)),
        name="dispatch",
    )(dest, hn_ctx, hn_dec)


def _experts_kernel(ea_ref, eb_ref, nv_ref, blk_ref, xs_ref, rw_ref, rb_ref, *refs):
    w32_refs, ys_ref, w_refs = refs[:6], refs[6], refs[7:]
    wga_ref, wua_ref, wda_ref, wgb_ref, wub_ref, wdb_ref = w_refs
    i = pl.program_id(0)
    nv = nv_ref[i]
    prev = jnp.maximum(i - 1, 0)

    for e_ref, first in ((ea_ref, 0), (eb_ref, 3)):
        @pl.when((i == 0) | (e_ref[i] != e_ref[prev]))
        def _(first=first):
            for w32_ref, w_ref in zip(w32_refs[first:first + 3], w_refs[first:first + 3]):
                w_ref[...] = w32_ref[0].astype(BF16)

    @pl.when(nv > 0)
    def _():
        row = lax.broadcasted_iota(jnp.int32, (TMX, 1), 0)
        xb = jnp.where(row < nv, _load_token_major(xs_ref, TMX), 0.0).astype(BF16)
        logits = jnp.dot(xb, rw_ref[...], preferred_element_type=F32) + rb_ref[...]
        lane = lax.broadcasted_iota(jnp.int32, logits.shape, 1)
        ea = ea_ref[i]
        eb = eb_ref[i]
        gmask = lane < N_GROUPS
        gl = jnp.where(gmask, logits, -jnp.inf)
        gmax = jnp.max(gl, axis=-1, keepdims=True)
        gexp = jnp.where(gmask, jnp.exp(gl - gmax), 0.0)
        g_own = jnp.sum(jnp.where(lane == ea // EXPERTS_PER_GROUP, gexp, 0.0), axis=-1, keepdims=True)
        g_w = g_own / jnp.sum(gexp, axis=-1, keepdims=True)
        la = jnp.sum(jnp.where(lane == ea + E_LANE0, logits, 0.0), axis=-1, keepdims=True)
        lb = jnp.sum(jnp.where(lane == eb + E_LANE0, logits, 0.0), axis=-1, keepdims=True)
        m = jnp.maximum(la, lb)
        pa = jnp.exp(la - m)
        pb = jnp.exp(lb - m)
        inv = g_w / (pa + pb)

        def hidden(x, wg_ref, wu_ref, w):
            g = jnp.dot(x, wg_ref[...], preferred_element_type=F32)
            u = jnp.dot(x, wu_ref[...], preferred_element_type=F32)
            return ((g * _sigmoid(g)) * u * w).astype(BF16)

        part = TMX // EXPERT_ROW_PARTS
        ys = []
        for h in range(EXPERT_ROW_PARTS):
            rows = slice(h * part, (h + 1) * part)
            act_a = hidden(xb[rows], wga_ref, wua_ref, (pa * inv)[rows])
            act_b = hidden(xb[rows], wgb_ref, wub_ref, (pb * inv)[rows])
            y = (jnp.dot(act_a, wda_ref[...], preferred_element_type=F32)
                 + jnp.dot(act_b, wdb_ref[...], preferred_element_type=F32))
            ys.append(y)
        _store_token_major(ys_ref, jnp.concatenate(ys, axis=0))


def _experts(sched, xs, router_w, router_b, wg, wu, wd):
    ea, eb, nv, blk = sched
    n_tiles = ea.shape[0]
    rows = lambda i, ea, eb, nv, blk: (blk[i], 0)
    const2 = lambda i, ea, eb, nv, blk: (0, 0)
    exp_a = lambda i, ea, eb, nv, blk: (ea[i], 0, 0)
    exp_b = lambda i, ea, eb, nv, blk: (eb[i], 0, 0)
    w_in_spec = lambda m: pl.BlockSpec((1, D_MODEL, D_EXPERT), m)
    w_out_spec = lambda m: pl.BlockSpec((1, D_EXPERT, D_MODEL), m)
    return pl.pallas_call(
        _experts_kernel,
        out_shape=jax.ShapeDtypeStruct(xs.shape, F32),
        grid_spec=pltpu.PrefetchScalarGridSpec(
            num_scalar_prefetch=4,
            grid=(n_tiles,),
            in_specs=[pl.BlockSpec((TMX * ROW_TILES, LANES), rows),
                      pl.BlockSpec((D_MODEL, ROUTER_LANES), const2),
                      pl.BlockSpec((1, ROUTER_LANES), const2),
                      w_in_spec(exp_a), w_in_spec(exp_a), w_out_spec(exp_a),
                      w_in_spec(exp_b), w_in_spec(exp_b), w_out_spec(exp_b)],
            out_specs=pl.BlockSpec((TMX * ROW_TILES, LANES), rows),
            scratch_shapes=[pltpu.VMEM((D_MODEL, D_EXPERT), BF16), pltpu.VMEM((D_MODEL, D_EXPERT), BF16),
                            pltpu.VMEM((D_EXPERT, D_MODEL), BF16)] * 2),
        compiler_params=_params(("arbitrary",)),
        name="experts",
    )(ea, eb, nv, blk, xs, router_w, router_b, wg, wu, wd, wg, wu, wd)


def _combine_kernel(dest_ref, ys_ref, x1_ref, mod_ref, gpost_ref, o_ref, ybuf, sems, *, tile0):
    i = pl.program_id(0)
    n = pl.num_programs(0)

    def fetch(tile, slot):
        base = (tile + tile0) * TC

        def start(g, carry):
            for u in range(DMA_UNROLL):
                r = g * DMA_UNROLL + u
                pltpu.make_async_copy(_token_rows(ys_ref, dest_ref[base + r]), _token_rows(ybuf.at[slot], r),
                                      sems.at[slot]).start(priority=u % 2)
            return carry

        lax.fori_loop(0, TC // DMA_UNROLL, start, 0)

    @pl.when(i == 0)
    def _():
        fetch(0, 0)

    @pl.when(i + 1 < n)
    def _():
        fetch(i + 1, (i + 1) % 2)

    slot = i % 2
    pltpu.make_async_copy(ys_ref.at[pl.ds(0, TC * ROW_TILES), :], ybuf.at[slot], sems.at[slot]).wait()
    gate2 = mod_ref[0, :, 5 * D_MODEL:6 * D_MODEL]
    o_ref[...] = x1_ref[...] + _rms(_load_token_major(ybuf.at[slot], TC)) * (gate2 * gpost_ref[...])


def _combine(dest, ys, x1, mod3, g_post, cond_of_tile, tile0):
    n_tok = x1.shape[0]
    return pl.pallas_call(
        functools.partial(_combine_kernel, tile0=tile0),
        out_shape=jax.ShapeDtypeStruct((n_tok, D_MODEL), F32),
        grid_spec=pltpu.PrefetchScalarGridSpec(
            num_scalar_prefetch=1,
            grid=(n_tok // TC,),
            in_specs=[pl.BlockSpec(memory_space=pl.ANY),
                      pl.BlockSpec((TC, D_MODEL), lambda i, d: (i, 0)),
                      pl.BlockSpec((1, 1, 6 * D_MODEL), lambda i, d: (cond_of_tile(i), 0, 0)),
                      pl.BlockSpec((1, D_MODEL), lambda i, d: (0, 0))],
            out_specs=pl.BlockSpec((TC, D_MODEL), lambda i, d: (i, 0)),
            scratch_shapes=[pltpu.VMEM((2, TC * ROW_TILES, LANES), F32), pltpu.SemaphoreType.DMA((2,))]),
        compiler_params=_params(("arbitrary",)),
        name="combine",
    )(dest, ys, x1, mod3, g_post)


def _schedule(bucket, rank, counts):
    n_tok = bucket.shape[0]
    n_max = n_tok // TMX + N_BUCKETS
    cnt = counts[:N_BUCKETS, 0].astype(jnp.int32)
    tiles = (cnt + TMX - 1) // TMX
    tile_end = jnp.cumsum(tiles)
    tile_start = tile_end - tiles
    ids = jnp.arange(N_BUCKETS, dtype=jnp.int32)
    slot0 = jnp.sum(jnp.where(bucket[:, None] == ids[None, :], (tile_start * TMX)[None, :], 0), axis=1)
    dest = slot0 + rank
    i = jnp.arange(n_max, dtype=jnp.int32)
    total = tile_end[-1]
    valid = i < total
    tb = jnp.sum((jnp.minimum(i, total - 1)[:, None] >= tile_end[None, :]).astype(jnp.int32), axis=1)
    pairs = [(a, b) for a in range(EXPERTS_PER_GROUP) for b in range(a + 1, EXPERTS_PER_GROUP)]
    ea_tab = jnp.array([g * EXPERTS_PER_GROUP + a for g in range(N_GROUPS) for a, _ in pairs], jnp.int32)
    eb_tab = jnp.array([g * EXPERTS_PER_GROUP + b for g in range(N_GROUPS) for _, b in pairs], jnp.int32)
    hit = tb[:, None] == ids[None, :]
    look = lambda tab: jnp.sum(jnp.where(hit, tab[None, :], 0), axis=1)
    ea, eb = look(ea_tab), look(eb_tab)
    nv = jnp.where(valid, jnp.clip(look(cnt) - (i - look(tile_start)) * TMX, 0, TMX), 0)
    blk = jnp.minimum(i, total - 1)
    return dest, (ea, eb, nv, blk), n_max * TMX


def _block_diag_gates(rg_wa, rg_wx):
    heads = LANES // HEAD_RNN

    def bd(w):
        w = w.reshape(2, RNN_BLOCKS, heads, HEAD_RNN, HEAD_RNN)
        eye = jnp.eye(heads, dtype=w.dtype)
        full = jnp.einsum('dghij,hk->dghikj', w, eye)
        return full.reshape(2, RNN_BLOCKS, LANES, LANES)

    return jnp.concatenate([bd(rg_wa), bd(rg_wx)], axis=-1).astype(BF16)


def _row_tile(v):
    blocks = v.reshape(v.shape[:-1] + (RNN_BLOCKS, LANES))
    return jnp.concatenate([blocks] * PAIR, axis=-2)


def _to_time_major_state(h):
    return h.reshape(h.shape[0] // PAIR, TMJ_ROWS, LANES)


def kernel(x_prompt, x_sample, state_rglru, c, c_ctx, w_mod, b_mod, g_pre_mix, g_post_mix, g_pre_ffn,
           g_post_ffn, w_in, conv_w, conv_b, rg_wa, rg_ba, rg_wx, rg_bx, rg_lambda, sgu_g, sgu_w, sgu_b,
           w_out, router_g_w, router_g_b, router_e_w, router_e_b, exp_w_gate, exp_w_up, exp_w_down):
    assert w_mod.shape[0] == 1, "single-layer trunk"
    n_ctx, ctx_len, _ = x_prompt.shape
    n_dec, dec_len, _ = x_sample.shape
    l = 0

    n_cond = SUBLANES
    ctx_rows = n_cond - n_dec
    ctx_per_step = max(PAIR, PREMIX_ROWS // min(ctx_len, TT))
    assert n_dec % PAIR == 0 and ctx_rows == ctx_per_step and n_dec % ctx_per_step == 0
    cond = jnp.concatenate([c, jnp.broadcast_to(c_ctx, (ctx_rows, D_MODEL))], axis=0)
    mod3 = _modulation(cond, w_mod[l], b_mod[l]).reshape(n_cond, 1, 6 * D_MODEL)
    pos_tab = _pos_table()

    w_rx, w_gate, w_u, w_v = jnp.split(w_in[l], [D_RNN, 2 * D_RNN, 2 * D_RNN + D_SGU], axis=1)
    w_in_b = jnp.concatenate([w_v, w_u, w_gate, w_rx], axis=1).astype(BF16)
    w_out_b = w_out[l].astype(BF16)
    sgu_w_b = sgu_w[l].reshape(2, 4 * CHUNK, CHUNK).astype(BF16)
    sgu_bias_tile = jnp.repeat(sgu_b[l].T, HEAD_SGU, axis=1)
    w_gates = _block_diag_gates(rg_wa[l], rg_wx[l])
    b_gates = 0.5 * jnp.stack([_row_tile(rg_ba[l]), _row_tile(rg_bx[l])], axis=1)
    lam = _row_tile(rg_lambda[l])
    conv_w_t = 0.5 * _row_tile(conv_w[l])
    conv_b_t = 0.5 * _row_tile(conv_b[l])
    lane_pad = ROUTER_LANES - E_LANE0 - N_EXPERTS
    router_w = jnp.pad(jnp.concatenate([router_g_w[l], router_e_w[l]], axis=1), ((0, 0), (0, lane_pad))).astype(BF16)
    router_b = jnp.pad(jnp.concatenate([router_g_b[l], router_e_b[l]]), (0, lane_pad)).reshape(1, ROUTER_LANES)
    gap, tail = E_ROW0 - N_GROUPS, ROUTER_ROWS - E_ROW0 - N_EXPERTS
    router_wt = jnp.concatenate([router_g_w[l].T, jnp.zeros((gap, D_MODEL), F32), router_e_w[l].T,
                                 jnp.zeros((tail, D_MODEL), F32)], axis=0).astype(BF16)
    router_bt = jnp.concatenate([router_g_b[l], jnp.zeros((gap,), F32), router_e_b[l], jnp.zeros((tail,), F32)])
    router_bt = jnp.broadcast_to(router_bt[:, None], (ROUTER_ROWS, LANES))
    earlier = jnp.triu(jnp.ones((TP, TP), BF16), k=1)
    row = lambda v: v.reshape(1, -1)

    n_ctx_tok = n_ctx * ctx_len

    def mixer(x, h0, cond_of_tile, cond_block, use_pos, counts0):
        n_seq, seq_len, _ = x.shape
        xf = x.reshape(n_seq * seq_len, D_MODEL)
        tab = pos_tab if use_pos else None
        xr, gg, y_sgu = _premix(x, mod3, cond_block, row(g_pre_mix[l]), w_in_b, row(sgu_g[l]),
                                sgu_w_b, sgu_bias_tile, tab)
        scan_params = (conv_w_t, conv_b_t, w_gates, b_gates, lam)
        hf, hf_last = _scan(xr, None, None, _to_time_major_state(h0[:, 0]), *scan_params, direction=0)
        y_rnn, hb_first = _scan(xr, gg, hf, _to_time_major_state(h0[:, 1]), *scan_params, direction=1)
        fstate = jnp.stack([hf_last.reshape(n_seq, D_RNN), hb_first.reshape(n_seq, D_RNN)], axis=1)
        x1, hn, bucket, rank, counts = _postmix(
            xf, y_rnn.reshape(n_seq * seq_len, D_RNN), y_sgu.reshape(n_seq * seq_len, D_SGU), mod3,
            cond_of_tile, row(g_post_mix[l]), row(g_pre_ffn[l]), w_out_b, router_wt, router_bt, earlier, tab,
            counts0)
        return x1, hn, bucket, rank, counts, fstate

    ctx_cond = lambda i: n_dec
    dec_cond = lambda i: i // (dec_len // TP)
    h0_ctx = jnp.zeros((n_ctx, 2, D_RNN), F32)
    counts0 = jnp.zeros((ROUTER_ROWS, LANES), F32)
    x1_ctx, hn_ctx, bucket_ctx, rank_ctx, counts, st = mixer(x_prompt, h0_ctx, ctx_cond, lambda p: n_dec // ctx_per_step,
                                                             False, counts0)
    new_state = st.astype(state_rglru.dtype)[:, None]
    x1_dec, hn_dec, bucket_dec, rank_dec, counts, _ = mixer(x_sample, state_rglru[:, l].astype(F32), dec_cond,
                                                            lambda p: p, True, counts)

    dest, sched, n_slots = _schedule(jnp.concatenate([bucket_ctx, bucket_dec]),
                                     jnp.concatenate([rank_ctx, rank_dec]), counts)
    xs = _dispatch(dest, hn_ctx, hn_dec, n_slots)
    ys = _experts(sched, xs, router_w, router_b, exp_w_gate[l], exp_w_up[l], exp_w_down[l])
    y_prompt = _combine(dest, ys, x1_ctx, mod3, row(g_post_ffn[l]), ctx_cond, 0)
    y_sample = _combine(dest, ys, x1_dec, mod3, row(g_post_ffn[l]), lambda i: i // (dec_len // TC), n_ctx_tok // TC)
    return (y_prompt.reshape(x_prompt.shape), y_sample.reshape(x_sample.shape), new_state)
```

```python
import functools
import math

import jax
import jax.numpy as jnp
from jax import lax
from jax.experimental import pallas as pl
from jax.experimental.pallas import tpu as pltpu

D_MODEL = 1024
D_RNN = 512
D_SGU = 512
N_HEADS_RNN = 8
HEAD_RNN = D_RNN // N_HEADS_RNN
N_HEADS_SGU = 8
HEAD_SGU = D_SGU // N_HEADS_SGU
CHUNK = 128
GRID_W = 64
RG_C = 8.0
N_GROUPS = 4
EXPERTS_PER_GROUP = 4
N_EXPERTS = N_GROUPS * EXPERTS_PER_GROUP
D_EXPERT = 512
EPS = 1e-6
POS_BASE = 10000.0

LANES = 128
SUBLANES = 8
CONV_W = 4
CONV_LEFT = 2
PAIR = 2
RNN_BLOCKS = D_RNN // LANES
TMJ_ROWS = PAIR * RNN_BLOCKS
ROUTER_LANES = LANES
E_LANE0 = N_GROUPS
ROUTER_ROWS = 32
E_ROW0 = SUBLANES

PAIRS_PER_GROUP = EXPERTS_PER_GROUP * (EXPERTS_PER_GROUP - 1) // 2
N_BUCKETS = N_GROUPS * PAIRS_PER_GROUP

ROW_TILES = D_MODEL // LANES

MOD_COLS = 2048
TP = 1024
TD = 2048
TC = 512
DMA_UNROLL = 8
EXPERT_ROW_PARTS = 2
TMX = 512
TT = 512
PREMIX_ROWS = 1024
LC = 256
TS = 16
PB = 2
VMEM_LIMIT = 56 * 1024 * 1024

F32 = jnp.float32
BF16 = jnp.bfloat16


def _params(sem):
    return pltpu.CompilerParams(dimension_semantics=sem, vmem_limit_bytes=VMEM_LIMIT)


def _rms(x):
    return x * lax.rsqrt(jnp.mean(x * x, axis=-1, keepdims=True) + EPS)


def _sigmoid(x):
    return 0.5 * jnp.tanh(0.5 * x) + 0.5


def _mod_kernel(cond_ref, w_ref, b_ref, o_ref):
    c = cond_ref[...]
    s = c * _sigmoid(c)
    o_ref[...] = jnp.dot(s.astype(BF16), w_ref[...].astype(BF16),
                         preferred_element_type=F32) + b_ref[...]


def _modulation(cond, w_mod, b_mod):
    n = w_mod.shape[1]
    return pl.pallas_call(
        _mod_kernel,
        out_shape=jax.ShapeDtypeStruct((cond.shape[0], n), F32),
        grid=(n // MOD_COLS,),
        in_specs=[pl.BlockSpec(cond.shape, lambda j: (0, 0)),
                  pl.BlockSpec((D_MODEL, MOD_COLS), lambda j: (0, j)),
                  pl.BlockSpec((1, MOD_COLS), lambda j: (0, j))],
        out_specs=pl.BlockSpec((cond.shape[0], MOD_COLS), lambda j: (0, j)),
        compiler_params=_params(("arbitrary",)),
        name="modulation",
    )(cond, w_mod, b_mod.reshape(1, n))


def _pos_kernel(o_ref):
    n_freq = D_MODEL // 4
    k = lax.broadcasted_iota(jnp.int32, (GRID_W, n_freq), 1).astype(F32)
    p = lax.broadcasted_iota(jnp.int32, (GRID_W, n_freq), 0).astype(F32)
    freq = jnp.exp(-math.log(POS_BASE) * k / n_freq)
    ang = p * freq
    o_ref[:, 0:n_freq] = jnp.sin(ang)
    o_ref[:, n_freq:2 * n_freq] = jnp.cos(ang)


def _pos_table():
    return pl.pallas_call(
        _pos_kernel,
        out_shape=jax.ShapeDtypeStruct((GRID_W, D_MODEL // 2), F32),
        name="pos_table",
    )()


def _add_pos(x, pos_refs, q0):
    if pos_refs is None:
        return x
    rows_ref, cols_ref = pos_refs
    reps = x.shape[0] // GRID_W
    rpart = jnp.concatenate(
        [jnp.broadcast_to(rows_ref[q:q + 1, :], (GRID_W, D_MODEL // 2)) for q in range(q0, q0 + reps)], axis=0)
    cpart = jnp.concatenate([cols_ref[...]] * reps, axis=0)
    return jnp.concatenate([x[:, :D_MODEL // 2] + rpart, x[:, D_MODEL // 2:] + cpart], axis=1)


def _load_x(x_ref, pos_refs, r0, n):
    return _add_pos(x_ref[r0:r0 + n, :], pos_refs, r0 // GRID_W)


def _premix_kernel(*refs, add_pos):
    refs = list(refs)
    x_ref = refs.pop(0)
    nseq, tt = x_ref.shape[:2]
    pos_refs = (refs.pop(0), refs.pop(0)) if add_pos else None
    mod_ref, g_ref, win_ref, sgug_ref, sguw_ref, sgub_ref, xr_ref, gg_ref, ys_ref = refs
    hn = []
    for s in range(nseq):
        shift = mod_ref[s, :, 0:D_MODEL]
        scale = mod_ref[s, :, D_MODEL:2 * D_MODEL]
        hn.append(_rms(_add_pos(x_ref[s], pos_refs, 0)) * (g_ref[...] * (1.0 + scale)) + shift)
    z = jnp.dot(jnp.concatenate(hn, axis=0).astype(BF16), win_ref[...],
                preferred_element_type=F32)
    half = D_SGU // 2
    heads_per_half = N_HEADS_SGU // 2
    lane_head = lax.broadcasted_iota(jnp.int32, (CHUNK, half), 1) // HEAD_SGU
    for s in range(nseq):
        zs = z[s * tt:(s + 1) * tt]
        vn = (_rms(zs[:, 0:D_SGU]) * sgug_ref[...]).astype(BF16)
        u = zs[:, D_SGU:2 * D_SGU]
        for c in range(tt // CHUNK):
            rows = slice(c * CHUNK, (c + 1) * CHUNK)
            halves = []
            for hf in range(2):
                r = jnp.dot(sguw_ref[hf], vn[rows, hf * half:(hf + 1) * half],
                            preferred_element_type=F32)
                sel = jnp.zeros((CHUNK, half), F32)
                for h in range(heads_per_half):
                    sel = jnp.where(lane_head == h, r[h * CHUNK:(h + 1) * CHUNK], sel)
                halves.append(sel)
            gatev = jnp.concatenate(halves, axis=1) + sgub_ref[...]
            ys_ref[s, rows, :] = (u[rows] * gatev).astype(BF16)
    for s in range(nseq):
        zs = z[s * tt:(s + 1) * tt]
        gg = jax.nn.gelu(zs[:, 2 * D_SGU:2 * D_SGU + D_RNN])
        xr = zs[:, 2 * D_SGU + D_RNN:]
        for k in range(RNN_BLOCKS):
            first = (s // PAIR) * tt * TMJ_ROWS + (s % PAIR) * RNN_BLOCKS + k
            rows = pl.ds(first, tt, stride=TMJ_ROWS)
            xr_ref[rows, :] = xr[:, k * LANES:(k + 1) * LANES]
            gg_ref[rows, :] = gg[:, k * LANES:(k + 1) * LANES]


def _premix(x, mod3, cond_block, g_pre, w_in_b, sgu_g, sgu_w_b, sgu_bias_tile, pos_tab):
    n_seq, seq_len, _ = x.shape
    tt = min(seq_len, TT)
    nseq = max(PAIR, PREMIX_ROWS // tt)
    assert nseq == PAIR or tt == seq_len
    n_pairs, n_tiles = n_seq // PAIR, seq_len // tt
    add_pos = pos_tab is not None
    const2 = lambda p, j: (0, 0)
    in_specs = [pl.BlockSpec((nseq, tt, D_MODEL), lambda p, j: (p, j, 0))]
    args = [x]
    if add_pos:
        reps = tt // GRID_W
        in_specs += [pl.BlockSpec((None, reps, D_MODEL // 2), lambda p, j: (j, 0, 0)),
                     pl.BlockSpec((GRID_W, D_MODEL // 2), const2)]
        args += [pos_tab.reshape(GRID_W // reps, reps, D_MODEL // 2), pos_tab]
    in_specs += [pl.BlockSpec((nseq, 1, 6 * D_MODEL), lambda p, j: (cond_block(p), 0, 0)),
                 pl.BlockSpec((1, D_MODEL), const2),
                 pl.BlockSpec((D_MODEL, 2 * D_RNN + 2 * D_SGU), const2),
                 pl.BlockSpec((1, D_SGU), const2),
                 pl.BlockSpec((2, 4 * CHUNK, CHUNK), lambda p, j: (0, 0, 0)),
                 pl.BlockSpec((CHUNK, D_SGU), const2)]
    args += [mod3, g_pre, w_in_b, sgu_g, sgu_w_b, sgu_bias_tile]
    tmj = jax.ShapeDtypeStruct((n_pairs * seq_len * TMJ_ROWS, LANES), F32)
    tmj_spec = pl.BlockSpec((nseq // PAIR * tt * TMJ_ROWS, LANES), lambda p, j: (p * n_tiles + j, 0))
    xr, gg, y_sgu = pl.pallas_call(
        functools.partial(_premix_kernel, add_pos=add_pos),
        out_shape=(tmj, tmj, jax.ShapeDtypeStruct((n_seq, seq_len, D_SGU), BF16)),
        grid=(n_seq // nseq, n_tiles),
        in_specs=in_specs,
        out_specs=(tmj_spec, tmj_spec, pl.BlockSpec((nseq, tt, D_SGU), lambda p, j: (p, j, 0))),
        compiler_params=_params(("parallel", "parallel")),
        name="premix",
    )(*args)
    shape4 = (n_pairs, seq_len, TMJ_ROWS, LANES)
    return xr.reshape(shape4), gg.reshape(shape4), y_sgu


def _scan_kernel(*refs, reverse, n_chunks):
    if reverse:
        (xprev_ref, x_ref, xnext_ref, gg_ref, hf_ref, h0_ref, pt_ref, wg_ref,
         y_ref, fs_ref, xwin, xc_s, r_s, i_s, a_s, b_s, y_s, hcar) = refs
    else:
        (xprev_ref, x_ref, xnext_ref, h0_ref, pt_ref, wg_ref,
         hf_ref, fs_ref, xwin, xc_s, r_s, i_s, a_s, b_s, hcar) = refs
    cw = [pt_ref[k] for k in range(CONV_W)]
    cb, bias_a, bias_x, lam = (pt_ref[CONV_W + k] for k in range(4))
    c = pl.program_id(1)
    chunk = n_chunks - 1 - c if reverse else c
    sub_rows = TS * TMJ_ROWS

    def rows_of(pb, t0, n_steps):
        first = (pb * LC + t0) * TMJ_ROWS
        if not isinstance(first, int):
            first = pl.multiple_of(first, TMJ_ROWS)
        return pl.ds(first, n_steps * TMJ_ROWS)

    @pl.when(c == 0)
    def _():
        hcar[...] = h0_ref[...]

    xwin[:, 0:CONV_LEFT] = jnp.where(chunk > 0, xprev_ref[...], 0.0)
    xwin[:, LC + CONV_LEFT:LC + CONV_W - 1] = jnp.where(chunk < n_chunks - 1, xnext_ref[...], 0.0)

    xwin[:, CONV_LEFT:CONV_LEFT + LC] = x_ref[...]

    neg_lam = -lam
    softplus = jnp.maximum(neg_lam, 0.0) + jnp.log(1.0 + jnp.exp(-jnp.abs(neg_lam)))
    half_decay = (-0.5 * RG_C * math.log2(math.e)) * softplus

    def conv(pb):
        for t0 in range(0, LC, TS):
            xc = cb + cw[0] * xwin[pb, t0:t0 + TS]
            for k in range(1, CONV_W):
                xc = xc + cw[k] * xwin[pb, t0 + k:t0 + k + TS]
            xc_s[rows_of(pb, t0, TS), :] = xc.reshape(sub_rows, LANES)

    def gate_matmuls(pb):
        for k in range(RNN_BLOCKS):
            rows = pl.ds(pb * LC * TMJ_ROWS + k, LC * PAIR, stride=RNN_BLOCKS)
            g = jnp.dot(xc_s[rows, :].astype(BF16), wg_ref[k], preferred_element_type=F32)
            r_s[rows, :] = g[:, :LANES]
            i_s[rows, :] = g[:, LANES:]

    def gates(pb):
        for t0 in range(0, LC, TS):
            rows = rows_of(pb, t0, TS)
            tile = lambda ref: ref[rows, :].reshape(TS, TMJ_ROWS, LANES)
            tr = jnp.tanh(tile(r_s) + bias_a)
            ti = jnp.tanh(tile(i_s) + bias_x)
            log2_a = tr * half_decay + half_decay
            a = jnp.exp2(log2_a)
            q = jnp.tanh(log2_a * (-math.log(2.0))) * (a * a + 1.0)
            b = jnp.where(q > 0.0, q * lax.rsqrt(q), 0.0) * ((ti + 1.0) * tile(xc_s))
            a_s[rows, :] = a.reshape(sub_rows, LANES)
            b_s[rows, :] = b.reshape(sub_rows, LANES)

    for stage in (conv, gate_matmuls, gates):
        for pb in range(PB):
            stage(pb)

    half_steps = LC // 2

    def put(pb, t, rows, h):
        if reverse:
            y_s[rows, :] = h
        else:
            hf_ref[pb, t] = h

    def step(j, carry):
        lead, lag, prod = carry
        t_lead = LC - 1 - j if reverse else j
        t_lag = half_steps - 1 - j if reverse else half_steps + j
        out = ([], [], [])
        for pb in range(PB):
            rows_lead, rows_lag = rows_of(pb, t_lead, 1), rows_of(pb, t_lag, 1)
            a_lag = a_s[rows_lag, :]
            h_lead = a_s[rows_lead, :] * lead[pb] + b_s[rows_lead, :]
            h_lag = a_lag * lag[pb] + b_s[rows_lag, :]
            p_lag = a_lag * prod[pb]
            put(pb, t_lead, rows_lead, h_lead)
            put(pb, t_lag, rows_lag, h_lag)
            r_s[rows_lag, :] = p_lag
            for acc, v in zip(out, (h_lead, h_lag, p_lag)):
                acc.append(v)
        return tuple(tuple(acc) for acc in out)

    zeros = tuple(jnp.zeros((TMJ_ROWS, LANES), F32) for _ in range(PB))
    ones = tuple(jnp.ones((TMJ_ROWS, LANES), F32) for _ in range(PB))
    lead, lag, prod = lax.fori_loop(0, half_steps, step, (tuple(hcar[pb] for pb in range(PB)), zeros, ones),
                                    unroll=8)
    lag_first = 0 if reverse else half_steps

    def completion(pb, t0):
        prods = r_s[rows_of(pb, t0, TS), :].reshape(TS, TMJ_ROWS, LANES)
        return (prods * lead[pb]).reshape(sub_rows, LANES)

    for pb in range(PB):
        final = lag[pb] + prod[pb] * lead[pb]
        hcar[pb] = final
        fs_ref[pb] = final
        if not reverse:
            for t0 in range(lag_first, lag_first + half_steps, TS):
                hf_ref[pb, t0:t0 + TS] = hf_ref[pb, t0:t0 + TS] + completion(pb, t0).reshape(TS, TMJ_ROWS, LANES)

    if reverse:
        for pb in range(PB):
            for t0 in range(0, LC, TS):
                rows = rows_of(pb, t0, TS)
                both = hf_ref[pb, t0:t0 + TS].reshape(sub_rows, LANES) + y_s[rows, :]
                if lag_first <= t0 < lag_first + half_steps:
                    both = both + completion(pb, t0)
                y_s[rows, :] = both * gg_ref[pb, t0:t0 + TS].reshape(sub_rows, LANES)
            for s in range(PAIR):
                cols = [y_s[pl.ds(pb * LC * TMJ_ROWS + s * RNN_BLOCKS + k, LC, stride=TMJ_ROWS), :]
                        for k in range(RNN_BLOCKS)]
                y_ref[pb * PAIR + s] = jnp.concatenate(cols, axis=1).astype(BF16)


def _scan(xr, gg, hf, h0, param_tiles, w_gates, direction):
    n_pairs, seq_len = xr.shape[:2]
    n_chunks = seq_len // LC
    reverse = direction == 1
    pos = (lambda c: n_chunks - 1 - c) if reverse else (lambda c: c)
    tmj_blk = pl.BlockSpec((PB, LC, TMJ_ROWS, LANES), lambda i, c: (i, pos(c), 0, 0))
    state_blk = pl.BlockSpec((PB, TMJ_ROWS, LANES), lambda i, c: (i, 0, 0))
    per_dir = lambda *shape: pl.BlockSpec((None,) + shape, lambda i, c: (direction,) + (0,) * len(shape))
    in_specs = [
        pl.BlockSpec((PB, CONV_LEFT, TMJ_ROWS, LANES),
                     lambda i, c: (i, jnp.maximum(pos(c) * (LC // CONV_LEFT) - 1, 0), 0, 0)),
        tmj_blk,
        pl.BlockSpec((PB, 1, TMJ_ROWS, LANES), lambda i, c: (i, jnp.minimum((pos(c) + 1) * LC, seq_len - 1), 0, 0)),
    ]
    args = [xr, xr, xr]
    if reverse:
        in_specs += [tmj_blk, tmj_blk]
        args += [gg, hf]
    in_specs += [state_blk,
                 per_dir(CONV_W + 4, TMJ_ROWS, LANES),
                 per_dir(RNN_BLOCKS, LANES, 2 * LANES)]
    args += [h0, param_tiles, w_gates]
    flat = pltpu.VMEM((PB * LC * TMJ_ROWS, LANES), F32)
    scratch = [pltpu.VMEM((PB, LC + CONV_W - 1, TMJ_ROWS, LANES), F32)] + [flat] * (6 if reverse else 5)
    scratch += [pltpu.VMEM((PB, TMJ_ROWS, LANES), F32)]
    state = jax.ShapeDtypeStruct((n_pairs, TMJ_ROWS, LANES), F32)
    if reverse:
        out_shape = (jax.ShapeDtypeStruct((n_pairs * PAIR, seq_len, D_RNN), BF16), state)
        out_specs = (pl.BlockSpec((PB * PAIR, LC, D_RNN), lambda i, c: (i, pos(c), 0)), state_blk)
    else:
        out_shape = (jax.ShapeDtypeStruct(xr.shape, F32), state)
        out_specs = (tmj_blk, state_blk)
    return pl.pallas_call(
        functools.partial(_scan_kernel, reverse=reverse, n_chunks=n_chunks),
        out_shape=out_shape,
        grid=(n_pairs // PB, n_chunks),
        in_specs=in_specs,
        out_specs=out_specs,
        scratch_shapes=scratch,
        compiler_params=_params(("parallel", "arbitrary")),
        name="scan_bwd" if reverse else "scan_fwd",
    )(*args)


def _route(lt):
    n = lt.shape[1]
    row = lax.broadcasted_iota(jnp.int32, (EXPERTS_PER_GROUP, n), 0)
    neg = jnp.float32(-jnp.inf)

    def arg_max(v):
        m = jnp.max(v, axis=0, keepdims=True)
        return jnp.min(jnp.where(v == m, row, EXPERTS_PER_GROUP), axis=0, keepdims=True)

    g_idx = arg_max(lt[0:N_GROUPS])
    el = lt[E_ROW0:E_ROW0 + EXPERTS_PER_GROUP]
    for g in range(1, N_GROUPS):
        first = E_ROW0 + g * EXPERTS_PER_GROUP
        el = jnp.where(g_idx == g, lt[first:first + EXPERTS_PER_GROUP], el)
    i1 = arg_max(el)
    i2 = arg_max(jnp.where(row == i1, neg, el))
    ja = jnp.minimum(i1, i2)
    jb = jnp.maximum(i1, i2)
    pair = (ja * (2 * EXPERTS_PER_GROUP - 1 - ja)) // 2 + (jb - ja - 1)
    return g_idx * PAIRS_PER_GROUP + pair


def _store_token_major(ref, x, t0=0):
    n = x.shape[0]
    for k in range(ROW_TILES):
        ref[pl.ds(t0 * ROW_TILES + k, n, stride=ROW_TILES), :] = x[:, k * LANES:(k + 1) * LANES]


def _load_token_major(ref, n):
    return jnp.concatenate([ref[pl.ds(k, n, stride=ROW_TILES), :] for k in range(ROW_TILES)], axis=1)


def _postmix_kernel(*refs, add_pos):
    refs = list(refs)
    x_ref = refs.pop(0)
    pos_refs = (refs.pop(0), refs.pop(0)) if add_pos else None
    (yr_ref, ys_ref, mod_ref, gpost_ref, gpre_ref, wout_ref, rw_ref, rb_ref, earlier_ref, cnt0_ref,
     x1_ref, hn_ref, rt_ref, cnt_ref, run_ref) = refs

    @pl.when(pl.program_id(0) == 0)
    def _():
        run_ref[...] = cnt0_ref[...]

    gate1 = mod_ref[0, :, 2 * D_MODEL:3 * D_MODEL]
    shift2 = mod_ref[0, :, 3 * D_MODEL:4 * D_MODEL]
    scale2 = mod_ref[0, :, 4 * D_MODEL:5 * D_MODEL]
    y = (jnp.dot(yr_ref[...], wout_ref[0:D_RNN, :], preferred_element_type=F32)
         + jnp.dot(ys_ref[...], wout_ref[D_RNN:, :], preferred_element_type=F32))
    x1 = _load_x(x_ref, pos_refs, 0, TP) + _rms(y) * (gate1 * gpost_ref[...])
    x1_ref[...] = x1
    hn = _rms(x1) * (gpre_ref[...] * (1.0 + scale2)) + shift2
    _store_token_major(hn_ref, hn)
    lt = lax.dot_general(rw_ref[...], hn.astype(BF16), (((1,), (1,)), ((), ())),
                         preferred_element_type=F32) + rb_ref[:, 0:1]
    bucket = _route(lt)
    onehot = lax.broadcasted_iota(jnp.int32, (ROUTER_ROWS, TP), 0) == bucket
    before = jnp.dot(onehot.astype(BF16), earlier_ref[...], preferred_element_type=F32) + run_ref[:, 0:1]
    rank = jnp.sum(jnp.where(onehot, before, 0.0), axis=0, keepdims=True).astype(jnp.int32)
    row = lax.broadcasted_iota(jnp.int32, (SUBLANES, TP), 0)
    rt_ref[...] = jnp.where(row == 0, bucket, jnp.where(row == 1, rank, 0))
    run_ref[...] += jnp.sum(onehot.astype(F32), axis=1, keepdims=True)
    cnt_ref[...] = run_ref[...]


def _postmix(x, y_rnn, y_sgu, mod3, cond_of_tile, g_post, g_pre, w_out_b, router_wt, router_bt, earlier, pos_tab,
             counts0):
    n_tok = x.shape[0]
    n_tiles = n_tok // TP
    add_pos = pos_tab is not None
    tok = lambda i: (i, 0)
    const2 = lambda i: (0, 0)
    in_specs = [pl.BlockSpec((TP, D_MODEL), tok)]
    args = [x]
    if add_pos:
        reps = TP // GRID_W
        tiles_per_seq = GRID_W // reps
        in_specs += [pl.BlockSpec((None, reps, D_MODEL // 2), lambda i: (i % tiles_per_seq, 0, 0)),
                     pl.BlockSpec((GRID_W, D_MODEL // 2), const2)]
        args += [pos_tab.reshape(tiles_per_seq, reps, D_MODEL // 2), pos_tab]
    in_specs += [pl.BlockSpec((TP, D_RNN), tok),
                 pl.BlockSpec((TP, D_SGU), tok),
                 pl.BlockSpec((1, 1, 6 * D_MODEL), lambda i: (cond_of_tile(i), 0, 0)),
                 pl.BlockSpec((1, D_MODEL), const2),
                 pl.BlockSpec((1, D_MODEL), const2),
                 pl.BlockSpec((D_MODEL, D_MODEL), const2),
                 pl.BlockSpec((ROUTER_ROWS, D_MODEL), const2),
                 pl.BlockSpec((ROUTER_ROWS, LANES), const2),
                 pl.BlockSpec((TP, TP), const2),
                 pl.BlockSpec((ROUTER_ROWS, LANES), const2)]
    args += [y_rnn, y_sgu, mod3, g_post, g_pre, w_out_b, router_wt, router_bt, earlier, counts0]
    counts_spec = pl.BlockSpec((ROUTER_ROWS, LANES), const2)
    x1, hn, route, counts = pl.pallas_call(
        functools.partial(_postmix_kernel, add_pos=add_pos),
        out_shape=(jax.ShapeDtypeStruct((n_tok, D_MODEL), F32),
                   jax.ShapeDtypeStruct((n_tok * ROW_TILES, LANES), F32),
                   jax.ShapeDtypeStruct((n_tiles * SUBLANES, TP), jnp.int32),
                   jax.ShapeDtypeStruct((ROUTER_ROWS, LANES), F32)),
        grid=(n_tiles,),
        in_specs=in_specs,
        out_specs=(pl.BlockSpec((TP, D_MODEL), tok),
                   pl.BlockSpec((TP * ROW_TILES, LANES), tok),
                   pl.BlockSpec((SUBLANES, TP), tok),
                   counts_spec),
        scratch_shapes=[pltpu.VMEM((ROUTER_ROWS, LANES), F32)],
        compiler_params=_params(("arbitrary",)),
        name="postmix",
    )(*args)
    route = route.reshape(n_tiles, SUBLANES, TP)
    return x1, hn, route[:, 0].reshape(n_tok), route[:, 1].reshape(n_tok), counts


def _token_rows(ref, t):
    return ref.at[pl.ds(pl.multiple_of(t * ROW_TILES, ROW_TILES), ROW_TILES), :]


def _dispatch_kernel(dest_ref, hc_ref, hs_ref, xs_ref, sem, *, n_ctx_steps):
    i = pl.program_id(0)
    base = i * TD

    def scatter(src_ref):
        def start(g, carry):
            for u in range(DMA_UNROLL):
                r = g * DMA_UNROLL + u
                pltpu.make_async_copy(_token_rows(src_ref, r), _token_rows(xs_ref, dest_ref[base + r]),
                                      sem).start(priority=u % 2)
            return carry

        lax.fori_loop(0, TD // DMA_UNROLL, start, 0)
        pltpu.make_async_copy(src_ref, xs_ref.at[pl.ds(0, TD * ROW_TILES), :], sem).wait()

    @pl.when(i < n_ctx_steps)
    def _():
        scatter(hc_ref)

    @pl.when(i >= n_ctx_steps)
    def _():
        scatter(hs_ref)


def _dispatch(dest, hn_ctx, hn_dec, n_slots):
    n_ctx_steps = hn_ctx.shape[0] // (TD * ROW_TILES)
    n_dec_steps = hn_dec.shape[0] // (TD * ROW_TILES)
    return pl.pallas_call(
        functools.partial(_dispatch_kernel, n_ctx_steps=n_ctx_steps),
        out_shape=jax.ShapeDtypeStruct((n_slots * ROW_TILES, LANES), F32),
        grid_spec=pltpu.PrefetchScalarGridSpec(
            num_scalar_prefetch=1,
            grid=(n_ctx_steps + n_dec_steps,),
            in_specs=[pl.BlockSpec((TD * ROW_TILES, LANES), lambda i, d: (jnp.minimum(i, n_ctx_steps - 1), 0)),
                      pl.BlockSpec((TD * ROW_TILES, LANES), lambda i, d: (jnp.maximum(i - n_ctx_steps, 0), 0))],
            out_specs=pl.BlockSpec(memory_space=pl.ANY),
            scratch_shapes=[pltpu.SemaphoreType.DMA(())]),
        compiler_params=_params(("arbitrary",)),
        name="dispatch",
    )(dest, hn_ctx, hn_dec)


def _experts_kernel(ea_ref, eb_ref, nv_ref, blk_ref, xs_ref, rw_ref, rb_ref, *refs):
    w32_refs, ys_ref, w_refs = refs[:6], refs[6], refs[7:]
    wga_ref, wua_ref, wda_ref, wgb_ref, wub_ref, wdb_ref = w_refs
    i = pl.program_id(0)
    nv = nv_ref[i]
    prev = jnp.maximum(i - 1, 0)

    for e_ref, first in ((ea_ref, 0), (eb_ref, 3)):
        @pl.when((i == 0) | (e_ref[i] != e_ref[prev]))
        def _(first=first):
            for w32_ref, w_ref in zip(w32_refs[first:first + 3], w_refs[first:first + 3]):
                w_ref[...] = w32_ref[0].astype(BF16)

    @pl.when(nv > 0)
    def _():
        row = lax.broadcasted_iota(jnp.int32, (TMX, 1), 0)
        xb = jnp.where(row < nv, _load_token_major(xs_ref, TMX), 0.0).astype(BF16)
        logits = jnp.dot(xb, rw_ref[...], preferred_element_type=F32) + rb_ref[...]
        lane = lax.broadcasted_iota(jnp.int32, logits.shape, 1)
        ea = ea_ref[i]
        eb = eb_ref[i]
        gmask = lane < N_GROUPS
        gl = jnp.where(gmask, logits, -jnp.inf)
        gmax = jnp.max(gl, axis=-1, keepdims=True)
        gexp = jnp.where(gmask, jnp.exp(gl - gmax), 0.0)
        g_own = jnp.sum(jnp.where(lane == ea // EXPERTS_PER_GROUP, gexp, 0.0), axis=-1, keepdims=True)
        g_w = g_own / jnp.sum(gexp, axis=-1, keepdims=True)
        la = jnp.sum(jnp.where(lane == ea + E_LANE0, logits, 0.0), axis=-1, keepdims=True)
        lb = jnp.sum(jnp.where(lane == eb + E_LANE0, logits, 0.0), axis=-1, keepdims=True)
        m = jnp.maximum(la, lb)
        pa = jnp.exp(la - m)
        pb = jnp.exp(lb - m)
        inv = g_w / (pa + pb)

        def hidden(x, wg_ref, wu_ref, w):
            g = jnp.dot(x, wg_ref[...], preferred_element_type=F32)
            u = jnp.dot(x, wu_ref[...], preferred_element_type=F32)
            return ((g * _sigmoid(g)) * u * w).astype(BF16)

        part = TMX // EXPERT_ROW_PARTS
        ys = []
        for h in range(EXPERT_ROW_PARTS):
            rows = slice(h * part, (h + 1) * part)
            act_a = hidden(xb[rows], wga_ref, wua_ref, (pa * inv)[rows])
            act_b = hidden(xb[rows], wgb_ref, wub_ref, (pb * inv)[rows])
            y = (jnp.dot(act_a, wda_ref[...], preferred_element_type=F32)
                 + jnp.dot(act_b, wdb_ref[...], preferred_element_type=F32))
            ys.append(y)
        _store_token_major(ys_ref, jnp.concatenate(ys, axis=0))


def _experts(sched, xs, router_w, router_b, wg, wu, wd):
    ea, eb, nv, blk = sched
    n_tiles = ea.shape[0]
    rows = lambda i, ea, eb, nv, blk: (blk[i], 0)
    const2 = lambda i, ea, eb, nv, blk: (0, 0)
    exp_a = lambda i, ea, eb, nv, blk: (ea[i], 0, 0)
    exp_b = lambda i, ea, eb, nv, blk: (eb[i], 0, 0)
    w_in_spec = lambda m: pl.BlockSpec((1, D_MODEL, D_EXPERT), m)
    w_out_spec = lambda m: pl.BlockSpec((1, D_EXPERT, D_MODEL), m)
    return pl.pallas_call(
        _experts_kernel,
        out_shape=jax.ShapeDtypeStruct(xs.shape, F32),
        grid_spec=pltpu.PrefetchScalarGridSpec(
            num_scalar_prefetch=4,
            grid=(n_tiles,),
            in_specs=[pl.BlockSpec((TMX * ROW_TILES, LANES), rows),
                      pl.BlockSpec((D_MODEL, ROUTER_LANES), const2),
                      pl.BlockSpec((1, ROUTER_LANES), const2),
                      w_in_spec(exp_a), w_in_spec(exp_a), w_out_spec(exp_a),
                      w_in_spec(exp_b), w_in_spec(exp_b), w_out_spec(exp_b)],
            out_specs=pl.BlockSpec((TMX * ROW_TILES, LANES), rows),
            scratch_shapes=[pltpu.VMEM((D_MODEL, D_EXPERT), BF16), pltpu.VMEM((D_MODEL, D_EXPERT), BF16),
                            pltpu.VMEM((D_EXPERT, D_MODEL), BF16)] * 2),
        compiler_params=_params(("arbitrary",)),
        name="experts",
    )(ea, eb, nv, blk, xs, router_w, router_b, wg, wu, wd, wg, wu, wd)


def _combine_kernel(dest_ref, ys_ref, x1_ref, mod_ref, gpost_ref, o_ref, ybuf, sems, *, tile0):
    i = pl.program_id(0)
    n = pl.num_programs(0)

    def fetch(tile, slot):
        base = (tile + tile0) * TC

        def start(g, carry):
            for u in range(DMA_UNROLL):
                r = g * DMA_UNROLL + u
                pltpu.make_async_copy(_token_rows(ys_ref, dest_ref[base + r]), _token_rows(ybuf.at[slot], r),
                                      sems.at[slot]).start(priority=u % 2)
            return carry

        lax.fori_loop(0, TC // DMA_UNROLL, start, 0)

    @pl.when(i == 0)
    def _():
        fetch(0, 0)

    @pl.when(i + 1 < n)
    def _():
        fetch(i + 1, (i + 1) % 2)

    slot = i % 2
    pltpu.make_async_copy(ys_ref.at[pl.ds(0, TC * ROW_TILES), :], ybuf.at[slot], sems.at[slot]).wait()
    gate2 = mod_ref[0, :, 5 * D_MODEL:6 * D_MODEL]
    o_ref[...] = x1_ref[...] + _rms(_load_token_major(ybuf.at[slot], TC)) * (gate2 * gpost_ref[...])


def _combine(dest, ys, x1, mod3, g_post, cond_of_tile, tile0):
    n_tok = x1.shape[0]
    return pl.pallas_call(
        functools.partial(_combine_kernel, tile0=tile0),
        out_shape=jax.ShapeDtypeStruct((n_tok, D_MODEL), F32),
        grid_spec=pltpu.PrefetchScalarGridSpec(
            num_scalar_prefetch=1,
            grid=(n_tok // TC,),
            in_specs=[pl.BlockSpec(memory_space=pl.ANY),
                      pl.BlockSpec((TC, D_MODEL), lambda i, d: (i, 0)),
                      pl.BlockSpec((1, 1, 6 * D_MODEL), lambda i, d: (cond_of_tile(i), 0, 0)),
                      pl.BlockSpec((1, D_MODEL), lambda i, d: (0, 0))],
            out_specs=pl.BlockSpec((TC, D_MODEL), lambda i, d: (i, 0)),
            scratch_shapes=[pltpu.VMEM((2, TC * ROW_TILES, LANES), F32), pltpu.SemaphoreType.DMA((2,))]),
        compiler_params=_params(("arbitrary",)),
        name="combine",
    )(dest, ys, x1, mod3, g_post)


def _schedule(bucket, rank, counts):
    n_tok = bucket.shape[0]
    n_max = n_tok // TMX + N_BUCKETS
    cnt = counts[:N_BUCKETS, 0].astype(jnp.int32)
    tiles = (cnt + TMX - 1) // TMX
    tile_end = jnp.cumsum(tiles)
    tile_start = tile_end - tiles
    ids = jnp.arange(N_BUCKETS, dtype=jnp.int32)
    slot0 = jnp.sum(jnp.where(bucket[:, None] == ids[None, :], (tile_start * TMX)[None, :], 0), axis=1)
    dest = slot0 + rank
    i = jnp.arange(n_max, dtype=jnp.int32)
    total = tile_end[-1]
    valid = i < total
    tb = jnp.sum((jnp.minimum(i, total - 1)[:, None] >= tile_end[None, :]).astype(jnp.int32), axis=1)
    pairs = [(a, b) for a in range(EXPERTS_PER_GROUP) for b in range(a + 1, EXPERTS_PER_GROUP)]
    ea_tab = jnp.array([g * EXPERTS_PER_GROUP + a for g in range(N_GROUPS) for a, _ in pairs], jnp.int32)
    eb_tab = jnp.array([g * EXPERTS_PER_GROUP + b for g in range(N_GROUPS) for _, b in pairs], jnp.int32)
    hit = tb[:, None] == ids[None, :]
    look = lambda tab: jnp.sum(jnp.where(hit, tab[None, :], 0), axis=1)
    ea, eb = look(ea_tab), look(eb_tab)
    nv = jnp.where(valid, jnp.clip(look(cnt) - (i - look(tile_start)) * TMX, 0, TMX), 0)
    blk = jnp.minimum(i, total - 1)
    return dest, (ea, eb, nv, blk), n_max * TMX


def _block_diag_gates(rg_wa, rg_wx):
    heads = LANES // HEAD_RNN

    def bd(w):
        w = w.reshape(2, RNN_BLOCKS, heads, HEAD_RNN, HEAD_RNN)
        eye = jnp.eye(heads, dtype=w.dtype)
        full = jnp.einsum('dghij,hk->dghikj', w, eye)
        return full.reshape(2, RNN_BLOCKS, LANES, LANES)

    return jnp.concatenate([bd(rg_wa), bd(rg_wx)], axis=-1).astype(BF16)


def _row_tile(v):
    blocks = v.reshape(v.shape[:-1] + (RNN_BLOCKS, LANES))
    return jnp.concatenate([blocks] * PAIR, axis=-2)


def _to_time_major_state(h):
    return h.reshape(h.shape[0] // PAIR, TMJ_ROWS, LANES)


def kernel(x_prompt, x_sample, state_rglru, c, c_ctx, w_mod, b_mod, g_pre_mix, g_post_mix, g_pre_ffn,
           g_post_ffn, w_in, conv_w, conv_b, rg_wa, rg_ba, rg_wx, rg_bx, rg_lambda, sgu_g, sgu_w, sgu_b,
           w_out, router_g_w, router_g_b, router_e_w, router_e_b, exp_w_gate, exp_w_up, exp_w_down):
    assert w_mod.shape[0] == 1, "single-layer trunk"
    n_ctx, ctx_len, _ = x_prompt.shape
    n_dec, dec_len, _ = x_sample.shape
    l = 0

    n_cond = SUBLANES
    ctx_rows = n_cond - n_dec
    ctx_per_step = max(PAIR, PREMIX_ROWS // min(ctx_len, TT))
    assert n_dec % PAIR == 0 and ctx_rows == ctx_per_step and n_dec % ctx_per_step == 0
    cond = jnp.concatenate([c, jnp.broadcast_to(c_ctx, (ctx_rows, D_MODEL))], axis=0)
    mod3 = _modulation(cond, w_mod[l], b_mod[l]).reshape(n_cond, 1, 6 * D_MODEL)
    pos_tab = _pos_table()

    w_rx, w_gate, w_u, w_v = jnp.split(w_in[l], [D_RNN, 2 * D_RNN, 2 * D_RNN + D_SGU], axis=1)
    w_in_b = jnp.concatenate([w_v, w_u, w_gate, w_rx], axis=1).astype(BF16)
    w_out_b = w_out[l].astype(BF16)
    sgu_w_b = sgu_w[l].reshape(2, 4 * CHUNK, CHUNK).astype(BF16)
    sgu_bias_tile = jnp.repeat(sgu_b[l].T, HEAD_SGU, axis=1)
    w_gates = _block_diag_gates(rg_wa[l], rg_wx[l])
    conv_rows = 0.5 * jnp.concatenate([conv_w[l], conv_b[l][None]], axis=0)
    per_dir_rows = jnp.concatenate([jnp.broadcast_to(conv_rows, (2,) + conv_rows.shape), 0.5 * rg_ba[l][:, None],
                                    0.5 * rg_bx[l][:, None], rg_lambda[l][:, None]], axis=1)
    scan_tiles = _row_tile(per_dir_rows)
    lane_pad = ROUTER_LANES - E_LANE0 - N_EXPERTS
    router_w = jnp.pad(jnp.concatenate([router_g_w[l], router_e_w[l]], axis=1), ((0, 0), (0, lane_pad))).astype(BF16)
    router_b = jnp.pad(jnp.concatenate([router_g_b[l], router_e_b[l]]), (0, lane_pad)).reshape(1, ROUTER_LANES)
    gap, tail = E_ROW0 - N_GROUPS, ROUTER_ROWS - E_ROW0 - N_EXPERTS
    router_wt = jnp.concatenate([router_g_w[l].T, jnp.zeros((gap, D_MODEL), F32), router_e_w[l].T,
                                 jnp.zeros((tail, D_MODEL), F32)], axis=0).astype(BF16)
    router_bt = jnp.concatenate([router_g_b[l], jnp.zeros((gap,), F32), router_e_b[l], jnp.zeros((tail,), F32)])
    router_bt = jnp.broadcast_to(router_bt[:, None], (ROUTER_ROWS, LANES))
    earlier = jnp.triu(jnp.ones((TP, TP), BF16), k=1)
    row = lambda v: v.reshape(1, -1)

    n_ctx_tok = n_ctx * ctx_len

    def mixer(x, h0, cond_of_tile, cond_block, use_pos, counts0):
        n_seq, seq_len, _ = x.shape
        xf = x.reshape(n_seq * seq_len, D_MODEL)
        tab = pos_tab if use_pos else None
        xr, gg, y_sgu = _premix(x, mod3, cond_block, row(g_pre_mix[l]), w_in_b, row(sgu_g[l]),
                                sgu_w_b, sgu_bias_tile, tab)
        scan_params = (scan_tiles, w_gates)
        hf, hf_last = _scan(xr, None, None, _to_time_major_state(h0[:, 0]), *scan_params, direction=0)
        y_rnn, hb_first = _scan(xr, gg, hf, _to_time_major_state(h0[:, 1]), *scan_params, direction=1)
        fstate = jnp.stack([hf_last.reshape(n_seq, D_RNN), hb_first.reshape(n_seq, D_RNN)], axis=1)
        x1, hn, bucket, rank, counts = _postmix(
            xf, y_rnn.reshape(n_seq * seq_len, D_RNN), y_sgu.reshape(n_seq * seq_len, D_SGU), mod3,
            cond_of_tile, row(g_post_mix[l]), row(g_pre_ffn[l]), w_out_b, router_wt, router_bt, earlier, tab,
            counts0)
        return x1, hn, bucket, rank, counts, fstate

    ctx_cond = lambda i: n_dec
    dec_cond = lambda i: i // (dec_len // TP)
    h0_ctx = jnp.zeros((n_ctx, 2, D_RNN), F32)
    counts0 = jnp.zeros((ROUTER_ROWS, LANES), F32)
    x1_ctx, hn_ctx, bucket_ctx, rank_ctx, counts, st = mixer(x_prompt, h0_ctx, ctx_cond, lambda p: n_dec // ctx_per_step,
                                                             False, counts0)
    new_state = st.astype(state_rglru.dtype)[:, None]
    x1_dec, hn_dec, bucket_dec, rank_dec, counts, _ = mixer(x_sample, state_rglru[:, l].astype(F32), dec_cond,
                                                            lambda p: p, True, counts)

    dest, sched, n_slots = _schedule(jnp.concatenate([bucket_ctx, bucket_dec]),
                                     jnp.concatenate([rank_ctx, rank_dec]), counts)
    xs = _dispatch(dest, hn_ctx, hn_dec, n_slots)
    ys = _experts(sched, xs, router_w, router_b, exp_w_gate[l], exp_w_up[l], exp_w_down[l])
    y_prompt = _combine(dest, ys, x1_ctx, mod3, row(g_post_ffn[l]), ctx_cond, 0)
    y_sample = _combine(dest, ys, x1_dec, mod3, row(g_post_ffn[l]), lambda i: i // (dec_len // TC), n_ctx_tok // TC)
    return (y_prompt.reshape(x_prompt.shape), y_sample.reshape(x_sample.shape), new_state)
```

```python
import functools
import math

import jax
import jax.numpy as jnp
from jax import lax
from jax.experimental import pallas as pl
from jax.experimental.pallas import tpu as pltpu

D_MODEL = 1024
D_RNN = 512
D_SGU = 512
N_HEADS_RNN = 8
HEAD_RNN = D_RNN // N_HEADS_RNN
N_HEADS_SGU = 8
HEAD_SGU = D_SGU // N_HEADS_SGU
CHUNK = 128
GRID_W = 64
RG_C = 8.0
N_GROUPS = 4
EXPERTS_PER_GROUP = 4
N_EXPERTS = N_GROUPS * EXPERTS_PER_GROUP
D_EXPERT = 512
EPS = 1e-6
POS_BASE = 10000.0

LANES = 128
SUBLANES = 8
CONV_W = 4
CONV_LEFT = 2
PAIR = 2
RNN_BLOCKS = D_RNN // LANES
TMJ_ROWS = PAIR * RNN_BLOCKS
ROUTER_LANES = LANES
E_LANE0 = N_GROUPS
ROUTER_ROWS = 32
E_ROW0 = SUBLANES

PAIRS_PER_GROUP = EXPERTS_PER_GROUP * (EXPERTS_PER_GROUP - 1) // 2
N_BUCKETS = N_GROUPS * PAIRS_PER_GROUP

ROW_TILES = D_MODEL // LANES

MOD_COLS = 2048
TP = 1024
TD = 2048
TC = 512
DMA_UNROLL = 8
EXPERT_ROW_PARTS = 2
TMX = 512
TT = 512
PREMIX_ROWS = 1024
LC = 256
TS = 16
PB = 2
SCAN_UNROLL = 8
VMEM_LIMIT = 56 * 1024 * 1024

F32 = jnp.float32
BF16 = jnp.bfloat16


def _params(sem):
    return pltpu.CompilerParams(dimension_semantics=sem, vmem_limit_bytes=VMEM_LIMIT)


def _rms(x):
    return x * lax.rsqrt(jnp.mean(x * x, axis=-1, keepdims=True) + EPS)


def _sigmoid(x):
    return 0.5 * jnp.tanh(0.5 * x) + 0.5


def _mod_kernel(cond_ref, w_ref, b_ref, o_ref):
    c = cond_ref[...]
    s = c * _sigmoid(c)
    o_ref[...] = jnp.dot(s.astype(BF16), w_ref[...].astype(BF16),
                         preferred_element_type=F32) + b_ref[...]


def _modulation(cond, w_mod, b_mod):
    n = w_mod.shape[1]
    return pl.pallas_call(
        _mod_kernel,
        out_shape=jax.ShapeDtypeStruct((cond.shape[0], n), F32),
        grid=(n // MOD_COLS,),
        in_specs=[pl.BlockSpec(cond.shape, lambda j: (0, 0)),
                  pl.BlockSpec((D_MODEL, MOD_COLS), lambda j: (0, j)),
                  pl.BlockSpec((1, MOD_COLS), lambda j: (0, j))],
        out_specs=pl.BlockSpec((cond.shape[0], MOD_COLS), lambda j: (0, j)),
        compiler_params=_params(("arbitrary",)),
        name="modulation",
    )(cond, w_mod, b_mod.reshape(1, n))


def _pos_kernel(o_ref):
    n_freq = D_MODEL // 4
    k = lax.broadcasted_iota(jnp.int32, (GRID_W, n_freq), 1).astype(F32)
    p = lax.broadcasted_iota(jnp.int32, (GRID_W, n_freq), 0).astype(F32)
    freq = jnp.exp(-math.log(POS_BASE) * k / n_freq)
    ang = p * freq
    o_ref[:, 0:n_freq] = jnp.sin(ang)
    o_ref[:, n_freq:2 * n_freq] = jnp.cos(ang)


def _pos_table():
    return pl.pallas_call(
        _pos_kernel,
        out_shape=jax.ShapeDtypeStruct((GRID_W, D_MODEL // 2), F32),
        name="pos_table",
    )()


def _add_pos(x, pos_refs, q0):
    if pos_refs is None:
        return x
    rows_ref, cols_ref = pos_refs
    reps = x.shape[0] // GRID_W
    rpart = jnp.concatenate(
        [jnp.broadcast_to(rows_ref[q:q + 1, :], (GRID_W, D_MODEL // 2)) for q in range(q0, q0 + reps)], axis=0)
    cpart = jnp.concatenate([cols_ref[...]] * reps, axis=0)
    return jnp.concatenate([x[:, :D_MODEL // 2] + rpart, x[:, D_MODEL // 2:] + cpart], axis=1)


def _load_x(x_ref, pos_refs, r0, n):
    return _add_pos(x_ref[r0:r0 + n, :], pos_refs, r0 // GRID_W)


def _premix_kernel(*refs, add_pos):
    refs = list(refs)
    x_ref = refs.pop(0)
    nseq, tt = x_ref.shape[:2]
    pos_refs = (refs.pop(0), refs.pop(0)) if add_pos else None
    mod_ref, g_ref, win_ref, sgug_ref, sguw_ref, sgub_ref, xr_ref, gg_ref, ys_ref = refs
    hn = []
    for s in range(nseq):
        shift = mod_ref[s, :, 0:D_MODEL]
        scale = mod_ref[s, :, D_MODEL:2 * D_MODEL]
        hn.append(_rms(_add_pos(x_ref[s], pos_refs, 0)) * (g_ref[...] * (1.0 + scale)) + shift)
    z = jnp.dot(jnp.concatenate(hn, axis=0).astype(BF16), win_ref[...],
                preferred_element_type=F32)
    half = D_SGU // 2
    heads_per_half = N_HEADS_SGU // 2
    lane_head = lax.broadcasted_iota(jnp.int32, (CHUNK, half), 1) // HEAD_SGU
    for s in range(nseq):
        zs = z[s * tt:(s + 1) * tt]
        vn = (_rms(zs[:, 0:D_SGU]) * sgug_ref[...]).astype(BF16)
        u = zs[:, D_SGU:2 * D_SGU]
        for c in range(tt // CHUNK):
            rows = slice(c * CHUNK, (c + 1) * CHUNK)
            halves = []
            for hf in range(2):
                r = jnp.dot(sguw_ref[hf], vn[rows, hf * half:(hf + 1) * half],
                            preferred_element_type=F32)
                sel = jnp.zeros((CHUNK, half), F32)
                for h in range(heads_per_half):
                    sel = jnp.where(lane_head == h, r[h * CHUNK:(h + 1) * CHUNK], sel)
                halves.append(sel)
            gatev = jnp.concatenate(halves, axis=1) + sgub_ref[...]
            ys_ref[s, rows, :] = (u[rows] * gatev).astype(BF16)
    for s in range(nseq):
        zs = z[s * tt:(s + 1) * tt]
        gg = jax.nn.gelu(zs[:, 2 * D_SGU:2 * D_SGU + D_RNN])
        xr = zs[:, 2 * D_SGU + D_RNN:]
        for k in range(RNN_BLOCKS):
            first = (s // PAIR) * tt * TMJ_ROWS + (s % PAIR) * RNN_BLOCKS + k
            rows = pl.ds(first, tt, stride=TMJ_ROWS)
            xr_ref[rows, :] = xr[:, k * LANES:(k + 1) * LANES]
            gg_ref[rows, :] = gg[:, k * LANES:(k + 1) * LANES]


def _premix(x, mod3, cond_block, g_pre, w_in_b, sgu_g, sgu_w_b, sgu_bias_tile, pos_tab):
    n_seq, seq_len, _ = x.shape
    tt = min(seq_len, TT)
    nseq = max(PAIR, PREMIX_ROWS // tt)
    assert nseq == PAIR or tt == seq_len
    n_pairs, n_tiles = n_seq // PAIR, seq_len // tt
    add_pos = pos_tab is not None
    const2 = lambda p, j: (0, 0)
    in_specs = [pl.BlockSpec((nseq, tt, D_MODEL), lambda p, j: (p, j, 0))]
    args = [x]
    if add_pos:
        reps = tt // GRID_W
        in_specs += [pl.BlockSpec((None, reps, D_MODEL // 2), lambda p, j: (j, 0, 0)),
                     pl.BlockSpec((GRID_W, D_MODEL // 2), const2)]
        args += [pos_tab.reshape(GRID_W // reps, reps, D_MODEL // 2), pos_tab]
    in_specs += [pl.BlockSpec((nseq, 1, 6 * D_MODEL), lambda p, j: (cond_block(p), 0, 0)),
                 pl.BlockSpec((1, D_MODEL), const2),
                 pl.BlockSpec((D_MODEL, 2 * D_RNN + 2 * D_SGU), const2),
                 pl.BlockSpec((1, D_SGU), const2),
                 pl.BlockSpec((2, 4 * CHUNK, CHUNK), lambda p, j: (0, 0, 0)),
                 pl.BlockSpec((CHUNK, D_SGU), const2)]
    args += [mod3, g_pre, w_in_b, sgu_g, sgu_w_b, sgu_bias_tile]
    tmj = jax.ShapeDtypeStruct((n_pairs * seq_len * TMJ_ROWS, LANES), F32)
    tmj_spec = pl.BlockSpec((nseq // PAIR * tt * TMJ_ROWS, LANES), lambda p, j: (p * n_tiles + j, 0))
    xr, gg, y_sgu = pl.pallas_call(
        functools.partial(_premix_kernel, add_pos=add_pos),
        out_shape=(tmj, tmj, jax.ShapeDtypeStruct((n_seq, seq_len, D_SGU), BF16)),
        grid=(n_seq // nseq, n_tiles),
        in_specs=in_specs,
        out_specs=(tmj_spec, tmj_spec, pl.BlockSpec((nseq, tt, D_SGU), lambda p, j: (p, j, 0))),
        compiler_params=_params(("parallel", "parallel")),
        name="premix",
    )(*args)
    shape4 = (n_pairs, seq_len, TMJ_ROWS, LANES)
    return xr.reshape(shape4), gg.reshape(shape4), y_sgu


def _scan_kernel(*refs, reverse, n_chunks):
    if reverse:
        (xprev_ref, x_ref, xnext_ref, gg_ref, hf_ref, h0_ref, pt_ref, wg_ref,
         y_ref, fs_ref, xwin, xc_s, r_s, i_s, a_s, b_s, y_s, hcar) = refs
    else:
        (xprev_ref, x_ref, xnext_ref, h0_ref, pt_ref, wg_ref,
         hf_ref, fs_ref, xwin, xc_s, r_s, i_s, a_s, b_s, hcar) = refs
    cw = [pt_ref[k] for k in range(CONV_W)]
    cb, bias_a, bias_x, lam = (pt_ref[CONV_W + k] for k in range(4))
    c = pl.program_id(1)
    chunk = n_chunks - 1 - c if reverse else c
    sub_rows = TS * TMJ_ROWS

    def rows_of(pb, t0, n_steps):
        first = (pb * LC + t0) * TMJ_ROWS
        if not isinstance(first, int):
            first = pl.multiple_of(first, TMJ_ROWS)
        return pl.ds(first, n_steps * TMJ_ROWS)

    @pl.when(c == 0)
    def _():
        hcar[...] = h0_ref[...]

    xwin[:, 0:CONV_LEFT] = jnp.where(chunk > 0, xprev_ref[...], 0.0)
    xwin[:, LC + CONV_LEFT:LC + CONV_W - 1] = jnp.where(chunk < n_chunks - 1, xnext_ref[...], 0.0)

    xwin[:, CONV_LEFT:CONV_LEFT + LC] = x_ref[...]

    neg_lam = -lam
    softplus = jnp.maximum(neg_lam, 0.0) + jnp.log(1.0 + jnp.exp(-jnp.abs(neg_lam)))
    half_decay = (-0.5 * RG_C * math.log2(math.e)) * softplus

    def conv(pb):
        for t0 in range(0, LC, TS):
            xc = cb + cw[0] * xwin[pb, t0:t0 + TS]
            for k in range(1, CONV_W):
                xc = xc + cw[k] * xwin[pb, t0 + k:t0 + k + TS]
            xc_s[rows_of(pb, t0, TS), :] = xc.reshape(sub_rows, LANES)

    def gate_matmuls(pb):
        for k in range(RNN_BLOCKS):
            rows = pl.ds(pb * LC * TMJ_ROWS + k, LC * PAIR, stride=RNN_BLOCKS)
            g = jnp.dot(xc_s[rows, :].astype(BF16), wg_ref[k], preferred_element_type=F32)
            r_s[rows, :] = g[:, :LANES]
            i_s[rows, :] = g[:, LANES:]

    def gates(pb):
        for t0 in range(0, LC, TS):
            rows = rows_of(pb, t0, TS)
            tile = lambda ref: ref[rows, :].reshape(TS, TMJ_ROWS, LANES)
            tr = jnp.tanh(tile(r_s) + bias_a)
            ti = jnp.tanh(tile(i_s) + bias_x)
            log2_a = tr * half_decay + half_decay
            a = jnp.exp2(log2_a)
            q = jnp.tanh(log2_a * (-math.log(2.0))) * (a * a + 1.0)
            b = jnp.where(q > 0.0, q * lax.rsqrt(q), 0.0) * ((ti + 1.0) * tile(xc_s))
            a_s[rows, :] = a.reshape(sub_rows, LANES)
            b_s[rows, :] = b.reshape(sub_rows, LANES)

    for stage in (conv, gate_matmuls, gates):
        for pb in range(PB):
            stage(pb)

    half_steps = LC // 2

    def put(pb, t, rows, h):
        if reverse:
            y_s[rows, :] = h
        else:
            hf_ref[pb, t] = h

    def steps(jb, carry):
        lead, lag, prod = (list(c) for c in carry)
        sign = -1 if reverse else 1
        t_base = sign * jb * SCAN_UNROLL
        row_base = pl.multiple_of(t_base * TMJ_ROWS, TMJ_ROWS)
        for u in range(SCAN_UNROLL):
            lead_0 = LC - 1 - u if reverse else u
            lag_0 = half_steps - 1 - u if reverse else half_steps + u
            for pb in range(PB):
                rows_lead = pl.ds(row_base + (pb * LC + lead_0) * TMJ_ROWS, TMJ_ROWS)
                rows_lag = pl.ds(row_base + (pb * LC + lag_0) * TMJ_ROWS, TMJ_ROWS)
                a_lag = a_s[rows_lag, :]
                lead[pb] = a_s[rows_lead, :] * lead[pb] + b_s[rows_lead, :]
                lag[pb] = a_lag * lag[pb] + b_s[rows_lag, :]
                prod[pb] = a_lag * prod[pb]
                put(pb, t_base + lead_0, rows_lead, lead[pb])
                put(pb, t_base + lag_0, rows_lag, lag[pb])
                r_s[rows_lag, :] = prod[pb]
        return tuple(lead), tuple(lag), tuple(prod)

    zeros = tuple(jnp.zeros((TMJ_ROWS, LANES), F32) for _ in range(PB))
    ones = tuple(jnp.ones((TMJ_ROWS, LANES), F32) for _ in range(PB))
    lead, lag, prod = lax.fori_loop(0, half_steps // SCAN_UNROLL, steps,
                                    (tuple(hcar[pb] for pb in range(PB)), zeros, ones))
    lag_first = 0 if reverse else half_steps

    def completion(pb, t0):
        prods = r_s[rows_of(pb, t0, TS), :].reshape(TS, TMJ_ROWS, LANES)
        return (prods * lead[pb]).reshape(sub_rows, LANES)

    for pb in range(PB):
        final = lag[pb] + prod[pb] * lead[pb]
        hcar[pb] = final
        fs_ref[pb] = final
        if not reverse:
            for t0 in range(lag_first, lag_first + half_steps, TS):
                hf_ref[pb, t0:t0 + TS] = hf_ref[pb, t0:t0 + TS] + completion(pb, t0).reshape(TS, TMJ_ROWS, LANES)

    if reverse:
        for pb in range(PB):
            for t0 in range(0, LC, TS):
                rows = rows_of(pb, t0, TS)
                both = hf_ref[pb, t0:t0 + TS].reshape(sub_rows, LANES) + y_s[rows, :]
                if lag_first <= t0 < lag_first + half_steps:
                    both = both + completion(pb, t0)
                y_s[rows, :] = both * gg_ref[pb, t0:t0 + TS].reshape(sub_rows, LANES)
            for s in range(PAIR):
                cols = [y_s[pl.ds(pb * LC * TMJ_ROWS + s * RNN_BLOCKS + k, LC, stride=TMJ_ROWS), :]
                        for k in range(RNN_BLOCKS)]
                y_ref[pb * PAIR + s] = jnp.concatenate(cols, axis=1).astype(BF16)


def _scan(xr, gg, hf, h0, param_tiles, w_gates, direction):
    n_pairs, seq_len = xr.shape[:2]
    n_chunks = seq_len // LC
    reverse = direction == 1
    pos = (lambda c: n_chunks - 1 - c) if reverse else (lambda c: c)
    tmj_blk = pl.BlockSpec((PB, LC, TMJ_ROWS, LANES), lambda i, c: (i, pos(c), 0, 0))
    state_blk = pl.BlockSpec((PB, TMJ_ROWS, LANES), lambda i, c: (i, 0, 0))
    per_dir = lambda *shape: pl.BlockSpec((None,) + shape, lambda i, c: (direction,) + (0,) * len(shape))
    in_specs = [
        pl.BlockSpec((PB, CONV_LEFT, TMJ_ROWS, LANES),
                     lambda i, c: (i, jnp.maximum(pos(c) * (LC // CONV_LEFT) - 1, 0), 0, 0)),
        tmj_blk,
        pl.BlockSpec((PB, 1, TMJ_ROWS, LANES), lambda i, c: (i, jnp.minimum((pos(c) + 1) * LC, seq_len - 1), 0, 0)),
    ]
    args = [xr, xr, xr]
    if reverse:
        in_specs += [tmj_blk, tmj_blk]
        args += [gg, hf]
    in_specs += [state_blk,
                 per_dir(CONV_W + 4, TMJ_ROWS, LANES),
                 per_dir(RNN_BLOCKS, LANES, 2 * LANES)]
    args += [h0, param_tiles, w_gates]
    flat = pltpu.VMEM((PB * LC * TMJ_ROWS, LANES), F32)
    scratch = [pltpu.VMEM((PB, LC + CONV_W - 1, TMJ_ROWS, LANES), F32)] + [flat] * (6 if reverse else 5)
    scratch += [pltpu.VMEM((PB, TMJ_ROWS, LANES), F32)]
    state = jax.ShapeDtypeStruct((n_pairs, TMJ_ROWS, LANES), F32)
    if reverse:
        out_shape = (jax.ShapeDtypeStruct((n_pairs * PAIR, seq_len, D_RNN), BF16), state)
        out_specs = (pl.BlockSpec((PB * PAIR, LC, D_RNN), lambda i, c: (i, pos(c), 0)), state_blk)
    else:
        out_shape = (jax.ShapeDtypeStruct(xr.shape, F32), state)
        out_specs = (tmj_blk, state_blk)
    return pl.pallas_call(
        functools.partial(_scan_kernel, reverse=reverse, n_chunks=n_chunks),
        out_shape=out_shape,
        grid=(n_pairs // PB, n_chunks),
        in_specs=in_specs,
        out_specs=out_specs,
        scratch_shapes=scratch,
        compiler_params=_params(("parallel", "arbitrary")),
        name="scan_bwd" if reverse else "scan_fwd",
    )(*args)


def _route(lt):
    n = lt.shape[1]
    row = lax.broadcasted_iota(jnp.int32, (EXPERTS_PER_GROUP, n), 0)
    neg = jnp.float32(-jnp.inf)

    def arg_max(v):
        m = jnp.max(v, axis=0, keepdims=True)
        return jnp.min(jnp.where(v == m, row, EXPERTS_PER_GROUP), axis=0, keepdims=True)

    g_idx = arg_max(lt[0:N_GROUPS])
    el = lt[E_ROW0:E_ROW0 + EXPERTS_PER_GROUP]
    for g in range(1, N_GROUPS):
        first = E_ROW0 + g * EXPERTS_PER_GROUP
        el = jnp.where(g_idx == g, lt[first:first + EXPERTS_PER_GROUP], el)
    i1 = arg_max(el)
    i2 = arg_max(jnp.where(row == i1, neg, el))
    ja = jnp.minimum(i1, i2)
    jb = jnp.maximum(i1, i2)
    pair = (ja * (2 * EXPERTS_PER_GROUP - 1 - ja)) // 2 + (jb - ja - 1)
    return g_idx * PAIRS_PER_GROUP + pair


def _store_token_major(ref, x, t0=0):
    n = x.shape[0]
    for k in range(ROW_TILES):
        ref[pl.ds(t0 * ROW_TILES + k, n, stride=ROW_TILES), :] = x[:, k * LANES:(k + 1) * LANES]


def _load_token_major(ref, n):
    return jnp.concatenate([ref[pl.ds(k, n, stride=ROW_TILES), :] for k in range(ROW_TILES)], axis=1)


def _postmix_kernel(*refs, add_pos):
    refs = list(refs)
    x_ref = refs.pop(0)
    pos_refs = (refs.pop(0), refs.pop(0)) if add_pos else None
    (yr_ref, ys_ref, mod_ref, gpost_ref, gpre_ref, wout_ref, rw_ref, rb_ref, earlier_ref, cnt0_ref,
     x1_ref, hn_ref, rt_ref, cnt_ref, run_ref) = refs

    @pl.when(pl.program_id(0) == 0)
    def _():
        run_ref[...] = cnt0_ref[...]

    gate1 = mod_ref[0, :, 2 * D_MODEL:3 * D_MODEL]
    shift2 = mod_ref[0, :, 3 * D_MODEL:4 * D_MODEL]
    scale2 = mod_ref[0, :, 4 * D_MODEL:5 * D_MODEL]
    y = (jnp.dot(yr_ref[...], wout_ref[0:D_RNN, :], preferred_element_type=F32)
         + jnp.dot(ys_ref[...], wout_ref[D_RNN:, :], preferred_element_type=F32))
    x1 = _load_x(x_ref, pos_refs, 0, TP) + _rms(y) * (gate1 * gpost_ref[...])
    x1_ref[...] = x1
    hn = _rms(x1) * (gpre_ref[...] * (1.0 + scale2)) + shift2
    _store_token_major(hn_ref, hn)
    lt = lax.dot_general(rw_ref[...], hn.astype(BF16), (((1,), (1,)), ((), ())),
                         preferred_element_type=F32) + rb_ref[:, 0:1]
    bucket = _route(lt)
    onehot = lax.broadcasted_iota(jnp.int32, (ROUTER_ROWS, TP), 0) == bucket
    before = jnp.dot(onehot.astype(BF16), earlier_ref[...], preferred_element_type=F32) + run_ref[:, 0:1]
    rank = jnp.sum(jnp.where(onehot, before, 0.0), axis=0, keepdims=True).astype(jnp.int32)
    row = lax.broadcasted_iota(jnp.int32, (SUBLANES, TP), 0)
    rt_ref[...] = jnp.where(row == 0, bucket, jnp.where(row == 1, rank, 0))
    run_ref[...] += jnp.sum(onehot.astype(F32), axis=1, keepdims=True)
    cnt_ref[...] = run_ref[...]


def _postmix(x, y_rnn, y_sgu, mod3, cond_of_tile, g_post, g_pre, w_out_b, router_wt, router_bt, earlier, pos_tab,
             counts0):
    n_tok = x.shape[0]
    n_tiles = n_tok // TP
    add_pos = pos_tab is not None
    tok = lambda i: (i, 0)
    const2 = lambda i: (0, 0)
    in_specs = [pl.BlockSpec((TP, D_MODEL), tok)]
    args = [x]
    if add_pos:
        reps = TP // GRID_W
        tiles_per_seq = GRID_W // reps
        in_specs += [pl.BlockSpec((None, reps, D_MODEL // 2), lambda i: (i % tiles_per_seq, 0, 0)),
                     pl.BlockSpec((GRID_W, D_MODEL // 2), const2)]
        args += [pos_tab.reshape(tiles_per_seq, reps, D_MODEL // 2), pos_tab]
    in_specs += [pl.BlockSpec((TP, D_RNN), tok),
                 pl.BlockSpec((TP, D_SGU), tok),
                 pl.BlockSpec((1, 1, 6 * D_MODEL), lambda i: (cond_of_tile(i), 0, 0)),
                 pl.BlockSpec((1, D_MODEL), const2),
                 pl.BlockSpec((1, D_MODEL), const2),
                 pl.BlockSpec((D_MODEL, D_MODEL), const2),
                 pl.BlockSpec((ROUTER_ROWS, D_MODEL), const2),
                 pl.BlockSpec((ROUTER_ROWS, LANES), const2),
                 pl.BlockSpec((TP, TP), const2),
                 pl.BlockSpec((ROUTER_ROWS, LANES), const2)]
    args += [y_rnn, y_sgu, mod3, g_post, g_pre, w_out_b, router_wt, router_bt, earlier, counts0]
    counts_spec = pl.BlockSpec((ROUTER_ROWS, LANES), const2)
    x1, hn, route, counts = pl.pallas_call(
        functools.partial(_postmix_kernel, add_pos=add_pos),
        out_shape=(jax.ShapeDtypeStruct((n_tok, D_MODEL), F32),
                   jax.ShapeDtypeStruct((n_tok * ROW_TILES, LANES), F32),
                   jax.ShapeDtypeStruct((n_tiles * SUBLANES, TP), jnp.int32),
                   jax.ShapeDtypeStruct((ROUTER_ROWS, LANES), F32)),
        grid=(n_tiles,),
        in_specs=in_specs,
        out_specs=(pl.BlockSpec((TP, D_MODEL), tok),
                   pl.BlockSpec((TP * ROW_TILES, LANES), tok),
                   pl.BlockSpec((SUBLANES, TP), tok),
                   counts_spec),
        scratch_shapes=[pltpu.VMEM((ROUTER_ROWS, LANES), F32)],
        compiler_params=_params(("arbitrary",)),
        name="postmix",
    )(*args)
    route = route.reshape(n_tiles, SUBLANES, TP)
    return x1, hn, route[:, 0].reshape(n_tok), route[:, 1].reshape(n_tok), counts


def _token_rows(ref, t):
    return ref.at[pl.ds(pl.multiple_of(t * ROW_TILES, ROW_TILES), ROW_TILES), :]


def _dispatch_kernel(dest_ref, hc_ref, hs_ref, xs_ref, sem, *, n_ctx_steps):
    i = pl.program_id(0)
    base = i * TD

    def scatter(src_ref):
        def start(g, carry):
            for u in range(DMA_UNROLL):
                r = g * DMA_UNROLL + u
                pltpu.make_async_copy(_token_rows(src_ref, r), _token_rows(xs_ref, dest_ref[base + r]),
                                      sem).start(priority=u % 2)
            return carry

        lax.fori_loop(0, TD // DMA_UNROLL, start, 0)
        pltpu.make_async_copy(src_ref, xs_ref.at[pl.ds(0, TD * ROW_TILES), :], sem).wait()

    @pl.when(i < n_ctx_steps)
    def _():
        scatter(hc_ref)

    @pl.when(i >= n_ctx_steps)
    def _():
        scatter(hs_ref)


def _dispatch(dest, hn_ctx, hn_dec, n_slots):
    n_ctx_steps = hn_ctx.shape[0] // (TD * ROW_TILES)
    n_dec_steps = hn_dec.shape[0] // (TD * ROW_TILES)
    return pl.pallas_call(
        functools.partial(_dispatch_kernel, n_ctx_steps=n_ctx_steps),
        out_shape=jax.ShapeDtypeStruct((n_slots * ROW_TILES, LANES), F32),
        grid_spec=pltpu.PrefetchScalarGridSpec(
            num_scalar_prefetch=1,
            grid=(n_ctx_steps + n_dec_steps,),
            in_specs=[pl.BlockSpec((TD * ROW_TILES, LANES), lambda i, d: (jnp.minimum(i, n_ctx_steps - 1), 0)),
                      pl.BlockSpec((TD * ROW_TILES, LANES), lambda i, d: (jnp.maximum(i - n_ctx_steps, 0), 0))],
            out_specs=pl.BlockSpec(memory_space=pl.ANY),
            scratch_shapes=[pltpu.SemaphoreType.DMA(())]),
        compiler_params=_params(("arbitrary",)),
        name="dispatch",
    )(dest, hn_ctx, hn_dec)


def _experts_kernel(ea_ref, eb_ref, nv_ref, blk_ref, xs_ref, rw_ref, rb_ref, *refs):
    w32_refs, ys_ref, w_refs = refs[:6], refs[6], refs[7:]
    wga_ref, wua_ref, wda_ref, wgb_ref, wub_ref, wdb_ref = w_refs
    i = pl.program_id(0)
    nv = nv_ref[i]
    prev = jnp.maximum(i - 1, 0)

    for e_ref, first in ((ea_ref, 0), (eb_ref, 3)):
        @pl.when((i == 0) | (e_ref[i] != e_ref[prev]))
        def _(first=first):
            for w32_ref, w_ref in zip(w32_refs[first:first + 3], w_refs[first:first + 3]):
                w_ref[...] = w32_ref[0].astype(BF16)

    @pl.when(nv > 0)
    def _():
        row = lax.broadcasted_iota(jnp.int32, (TMX, 1), 0)
        xb = jnp.where(row < nv, _load_token_major(xs_ref, TMX), 0.0).astype(BF16)
        logits = jnp.dot(xb, rw_ref[...], preferred_element_type=F32) + rb_ref[...]
        lane = lax.broadcasted_iota(jnp.int32, logits.shape, 1)
        ea = ea_ref[i]
        eb = eb_ref[i]
        gmask = lane < N_GROUPS
        gl = jnp.where(gmask, logits, -jnp.inf)
        gmax = jnp.max(gl, axis=-1, keepdims=True)
        gexp = jnp.where(gmask, jnp.exp(gl - gmax), 0.0)
        g_own = jnp.sum(jnp.where(lane == ea // EXPERTS_PER_GROUP, gexp, 0.0), axis=-1, keepdims=True)
        g_w = g_own / jnp.sum(gexp, axis=-1, keepdims=True)
        la = jnp.sum(jnp.where(lane == ea + E_LANE0, logits, 0.0), axis=-1, keepdims=True)
        lb = jnp.sum(jnp.where(lane == eb + E_LANE0, logits, 0.0), axis=-1, keepdims=True)
        m = jnp.maximum(la, lb)
        pa = jnp.exp(la - m)
        pb = jnp.exp(lb - m)
        inv = g_w / (pa + pb)

        def hidden(x, wg_ref, wu_ref, w):
            g = jnp.dot(x, wg_ref[...], preferred_element_type=F32)
            u = jnp.dot(x, wu_ref[...], preferred_element_type=F32)
            return ((g * _sigmoid(g)) * u * w).astype(BF16)

        part = TMX // EXPERT_ROW_PARTS
        ys = []
        for h in range(EXPERT_ROW_PARTS):
            rows = slice(h * part, (h + 1) * part)
            act_a = hidden(xb[rows], wga_ref, wua_ref, (pa * inv)[rows])
            act_b = hidden(xb[rows], wgb_ref, wub_ref, (pb * inv)[rows])
            y = (jnp.dot(act_a, wda_ref[...], preferred_element_type=F32)
                 + jnp.dot(act_b, wdb_ref[...], preferred_element_type=F32))
            ys.append(y)
        _store_token_major(ys_ref, jnp.concatenate(ys, axis=0))


def _experts(sched, xs, router_w, router_b, wg, wu, wd):
    ea, eb, nv, blk = sched
    n_tiles = ea.shape[0]
    rows = lambda i, ea, eb, nv, blk: (blk[i], 0)
    const2 = lambda i, ea, eb, nv, blk: (0, 0)
    exp_a = lambda i, ea, eb, nv, blk: (ea[i], 0, 0)
    exp_b = lambda i, ea, eb, nv, blk: (eb[i], 0, 0)
    w_in_spec = lambda m: pl.BlockSpec((1, D_MODEL, D_EXPERT), m)
    w_out_spec = lambda m: pl.BlockSpec((1, D_EXPERT, D_MODEL), m)
    return pl.pallas_call(
        _experts_kernel,
        out_shape=jax.ShapeDtypeStruct(xs.shape, F32),
        grid_spec=pltpu.PrefetchScalarGridSpec(
            num_scalar_prefetch=4,
            grid=(n_tiles,),
            in_specs=[pl.BlockSpec((TMX * ROW_TILES, LANES), rows),
                      pl.BlockSpec((D_MODEL, ROUTER_LANES), const2),
                      pl.BlockSpec((1, ROUTER_LANES), const2),
                      w_in_spec(exp_a), w_in_spec(exp_a), w_out_spec(exp_a),
                      w_in_spec(exp_b), w_in_spec(exp_b), w_out_spec(exp_b)],
            out_specs=pl.BlockSpec((TMX * ROW_TILES, LANES), rows),
            scratch_shapes=[pltpu.VMEM((D_MODEL, D_EXPERT), BF16), pltpu.VMEM((D_MODEL, D_EXPERT), BF16),
                            pltpu.VMEM((D_EXPERT, D_MODEL), BF16)] * 2),
        compiler_params=_params(("arbitrary",)),
        name="experts",
    )(ea, eb, nv, blk, xs, router_w, router_b, wg, wu, wd, wg, wu, wd)


def _combine_kernel(dest_ref, ys_ref, x1_ref, mod_ref, gpost_ref, o_ref, ybuf, sems, *, tile0):
    i = pl.program_id(0)
    n = pl.num_programs(0)

    def fetch(tile, slot):
        base = (tile + tile0) * TC

        def start(g, carry):
            for u in range(DMA_UNROLL):
                r = g * DMA_UNROLL + u
                pltpu.make_async_copy(_token_rows(ys_ref, dest_ref[base + r]), _token_rows(ybuf.at[slot], r),
                                      sems.at[slot]).start(priority=u % 2)
            return carry

        lax.fori_loop(0, TC // DMA_UNROLL, start, 0)

    @pl.when(i == 0)
    def _():
        fetch(0, 0)

    @pl.when(i + 1 < n)
    def _():
        fetch(i + 1, (i + 1) % 2)

    slot = i % 2
    pltpu.make_async_copy(ys_ref.at[pl.ds(0, TC * ROW_TILES), :], ybuf.at[slot], sems.at[slot]).wait()
    gate2 = mod_ref[0, :, 5 * D_MODEL:6 * D_MODEL]
    o_ref[...] = x1_ref[...] + _rms(_load_token_major(ybuf.at[slot], TC)) * (gate2 * gpost_ref[...])


def _combine(dest, ys, x1, mod3, g_post, cond_of_tile, tile0):
    n_tok = x1.shape[0]
    return pl.pallas_call(
        functools.partial(_combine_kernel, tile0=tile0),
        out_shape=jax.ShapeDtypeStruct((n_tok, D_MODEL), F32),
        grid_spec=pltpu.PrefetchScalarGridSpec(
            num_scalar_prefetch=1,
            grid=(n_tok // TC,),
            in_specs=[pl.BlockSpec(memory_space=pl.ANY),
                      pl.BlockSpec((TC, D_MODEL), lambda i, d: (i, 0)),
                      pl.BlockSpec((1, 1, 6 * D_MODEL), lambda i, d: (cond_of_tile(i), 0, 0)),
                      pl.BlockSpec((1, D_MODEL), lambda i, d: (0, 0))],
            out_specs=pl.BlockSpec((TC, D_MODEL), lambda i, d: (i, 0)),
            scratch_shapes=[pltpu.VMEM((2, TC * ROW_TILES, LANES), F32), pltpu.SemaphoreType.DMA((2,))]),
        compiler_params=_params(("arbitrary",)),
        name="combine",
    )(dest, ys, x1, mod3, g_post)


def _schedule(bucket, rank, counts):
    n_tok = bucket.shape[0]
    n_max = n_tok // TMX + N_BUCKETS
    cnt = counts[:N_BUCKETS, 0].astype(jnp.int32)
    tiles = (cnt + TMX - 1) // TMX
    tile_end = jnp.cumsum(tiles)
    tile_start = tile_end - tiles
    ids = jnp.arange(N_BUCKETS, dtype=jnp.int32)
    slot0 = jnp.sum(jnp.where(bucket[:, None] == ids[None, :], (tile_start * TMX)[None, :], 0), axis=1)
    dest = slot0 + rank
    i = jnp.arange(n_max, dtype=jnp.int32)
    total = tile_end[-1]
    valid = i < total
    tb = jnp.sum((jnp.minimum(i, total - 1)[:, None] >= tile_end[None, :]).astype(jnp.int32), axis=1)
    pairs = [(a, b) for a in range(EXPERTS_PER_GROUP) for b in range(a + 1, EXPERTS_PER_GROUP)]
    ea_tab = jnp.array([g * EXPERTS_PER_GROUP + a for g in range(N_GROUPS) for a, _ in pairs], jnp.int32)
    eb_tab = jnp.array([g * EXPERTS_PER_GROUP + b for g in range(N_GROUPS) for _, b in pairs], jnp.int32)
    hit = tb[:, None] == ids[None, :]
    look = lambda tab: jnp.sum(jnp.where(hit, tab[None, :], 0), axis=1)
    ea, eb = look(ea_tab), look(eb_tab)
    nv = jnp.where(valid, jnp.clip(look(cnt) - (i - look(tile_start)) * TMX, 0, TMX), 0)
    blk = jnp.minimum(i, total - 1)
    return dest, (ea, eb, nv, blk), n_max * TMX


def _block_diag_gates(rg_wa, rg_wx):
    heads = LANES // HEAD_RNN

    def bd(w):
        w = w.reshape(2, RNN_BLOCKS, heads, HEAD_RNN, HEAD_RNN)
        eye = jnp.eye(heads, dtype=w.dtype)
        full = jnp.einsum('dghij,hk->dghikj', w, eye)
        return full.reshape(2, RNN_BLOCKS, LANES, LANES)

    return jnp.concatenate([bd(rg_wa), bd(rg_wx)], axis=-1).astype(BF16)


def _row_tile(v):
    blocks = v.reshape(v.shape[:-1] + (RNN_BLOCKS, LANES))
    return jnp.concatenate([blocks] * PAIR, axis=-2)


def _to_time_major_state(h):
    return h.reshape(h.shape[0] // PAIR, TMJ_ROWS, LANES)


def kernel(x_prompt, x_sample, state_rglru, c, c_ctx, w_mod, b_mod, g_pre_mix, g_post_mix, g_pre_ffn,
           g_post_ffn, w_in, conv_w, conv_b, rg_wa, rg_ba, rg_wx, rg_bx, rg_lambda, sgu_g, sgu_w, sgu_b,
           w_out, router_g_w, router_g_b, router_e_w, router_e_b, exp_w_gate, exp_w_up, exp_w_down):
    assert w_mod.shape[0] == 1, "single-layer trunk"
    n_ctx, ctx_len, _ = x_prompt.shape
    n_dec, dec_len, _ = x_sample.shape
    l = 0

    n_cond = SUBLANES
    ctx_rows = n_cond - n_dec
    ctx_per_step = max(PAIR, PREMIX_ROWS // min(ctx_len, TT))
    assert n_dec % PAIR == 0 and ctx_rows == ctx_per_step and n_dec % ctx_per_step == 0
    cond = jnp.concatenate([c, jnp.broadcast_to(c_ctx, (ctx_rows, D_MODEL))], axis=0)
    mod3 = _modulation(cond, w_mod[l], b_mod[l]).reshape(n_cond, 1, 6 * D_MODEL)
    pos_tab = _pos_table()

    w_rx, w_gate, w_u, w_v = jnp.split(w_in[l], [D_RNN, 2 * D_RNN, 2 * D_RNN + D_SGU], axis=1)
    w_in_b = jnp.concatenate([w_v, w_u, w_gate, w_rx], axis=1).astype(BF16)
    w_out_b = w_out[l].astype(BF16)
    sgu_w_b = sgu_w[l].reshape(2, 4 * CHUNK, CHUNK).astype(BF16)
    sgu_bias_tile = jnp.repeat(sgu_b[l].T, HEAD_SGU, axis=1)
    w_gates = _block_diag_gates(rg_wa[l], rg_wx[l])
    conv_rows = 0.5 * jnp.concatenate([conv_w[l], conv_b[l][None]], axis=0)
    per_dir_rows = jnp.concatenate([jnp.broadcast_to(conv_rows, (2,) + conv_rows.shape), 0.5 * rg_ba[l][:, None],
                                    0.5 * rg_bx[l][:, None], rg_lambda[l][:, None]], axis=1)
    scan_tiles = _row_tile(per_dir_rows)
    lane_pad = ROUTER_LANES - E_LANE0 - N_EXPERTS
    router_w = jnp.pad(jnp.concatenate([router_g_w[l], router_e_w[l]], axis=1), ((0, 0), (0, lane_pad))).astype(BF16)
    router_b = jnp.pad(jnp.concatenate([router_g_b[l], router_e_b[l]]), (0, lane_pad)).reshape(1, ROUTER_LANES)
    gap, tail = E_ROW0 - N_GROUPS, ROUTER_ROWS - E_ROW0 - N_EXPERTS
    router_wt = jnp.concatenate([router_g_w[l].T, jnp.zeros((gap, D_MODEL), F32), router_e_w[l].T,
                                 jnp.zeros((tail, D_MODEL), F32)], axis=0).astype(BF16)
    router_bt = jnp.concatenate([router_g_b[l], jnp.zeros((gap,), F32), router_e_b[l], jnp.zeros((tail,), F32)])
    router_bt = jnp.broadcast_to(router_bt[:, None], (ROUTER_ROWS, LANES))
    earlier = jnp.triu(jnp.ones((TP, TP), BF16), k=1)
    row = lambda v: v.reshape(1, -1)

    n_ctx_tok = n_ctx * ctx_len

    def mixer(x, h0, cond_of_tile, cond_block, use_pos, counts0):
        n_seq, seq_len, _ = x.shape
        xf = x.reshape(n_seq * seq_len, D_MODEL)
        tab = pos_tab if use_pos else None
        xr, gg, y_sgu = _premix(x, mod3, cond_block, row(g_pre_mix[l]), w_in_b, row(sgu_g[l]),
                                sgu_w_b, sgu_bias_tile, tab)
        scan_params = (scan_tiles, w_gates)
        hf, hf_last = _scan(xr, None, None, _to_time_major_state(h0[:, 0]), *scan_params, direction=0)
        y_rnn, hb_first = _scan(xr, gg, hf, _to_time_major_state(h0[:, 1]), *scan_params, direction=1)
        fstate = jnp.stack([hf_last.reshape(n_seq, D_RNN), hb_first.reshape(n_seq, D_RNN)], axis=1)
        x1, hn, bucket, rank, counts = _postmix(
            xf, y_rnn.reshape(n_seq * seq_len, D_RNN), y_sgu.reshape(n_seq * seq_len, D_SGU), mod3,
            cond_of_tile, row(g_post_mix[l]), row(g_pre_ffn[l]), w_out_b, router_wt, router_bt, earlier, tab,
            counts0)
        return x1, hn, bucket, rank, counts, fstate

    ctx_cond = lambda i: n_dec
    dec_cond = lambda i: i // (dec_len // TP)
    h0_ctx = jnp.zeros((n_ctx, 2, D_RNN), F32)
    counts0 = jnp.zeros((ROUTER_ROWS, LANES), F32)
    x1_ctx, hn_ctx, bucket_ctx, rank_ctx, counts, st = mixer(x_prompt, h0_ctx, ctx_cond, lambda p: n_dec // ctx_per_step,
                                                             False, counts0)
    new_state = st.astype(state_rglru.dtype)[:, None]
    x1_dec, hn_dec, bucket_dec, rank_dec, counts, _ = mixer(x_sample, state_rglru[:, l].astype(F32), dec_cond,
                                                            lambda p: p, True, counts)

    dest, sched, n_slots = _schedule(jnp.concatenate([bucket_ctx, bucket_dec]),
                                     jnp.concatenate([rank_ctx, rank_dec]), counts)
    xs = _dispatch(dest, hn_ctx, hn_dec, n_slots)
    ys = _experts(sched, xs, router_w, router_b, exp_w_gate[l], exp_w_up[l], exp_w_down[l])
    y_prompt = _combine(dest, ys, x1_ctx, mod3, row(g_post_ffn[l]), ctx_cond, 0)
    y_sample = _combine(dest, ys, x1_dec, mod3, row(g_post_ffn[l]), lambda i: i // (dec_len // TC), n_ctx_tok // TC)
    return (y_prompt.reshape(x_prompt.shape), y_sample.reshape(x_sample.shape), new_state)
```

```python
import functools
import math

import jax
import jax.numpy as jnp
import numpy as np
from jax import lax
from jax.experimental import pallas as pl
from jax.experimental.pallas import tpu as pltpu

D_MODEL = 1024
D_RNN = 512
D_SGU = 512
N_HEADS_RNN = 8
HEAD_RNN = D_RNN // N_HEADS_RNN
N_HEADS_SGU = 8
HEAD_SGU = D_SGU // N_HEADS_SGU
CHUNK = 128
GRID_W = 64
RG_C = 8.0
N_GROUPS = 4
EXPERTS_PER_GROUP = 4
N_EXPERTS = N_GROUPS * EXPERTS_PER_GROUP
D_EXPERT = 512
EPS = 1e-6
POS_BASE = 10000.0

LANES = 128
SUBLANES = 8
CONV_W = 4
CONV_LEFT = 2
PAIR = 2
RNN_BLOCKS = D_RNN // LANES
TMJ_ROWS = PAIR * RNN_BLOCKS
ROUTER_LANES = LANES
E_LANE0 = N_GROUPS
ROUTER_ROWS = 32
E_ROW0 = SUBLANES

PAIRS_PER_GROUP = EXPERTS_PER_GROUP * (EXPERTS_PER_GROUP - 1) // 2
N_BUCKETS = N_GROUPS * PAIRS_PER_GROUP

ROW_TILES = D_MODEL // LANES

MOD_COLS = 2048
TP = 1024
TD = 2048
TC = 512
DMA_UNROLL = 8
EXPERT_ROW_PARTS = 2
TMX = 512
TT = 512
PREMIX_ROWS = 1024
LC = 256
TS = 16
PB = 2
SCAN_UNROLL = 8
VMEM_LIMIT = 56 * 1024 * 1024

F32 = jnp.float32
BF16 = jnp.bfloat16


def _params(sem):
    return pltpu.CompilerParams(dimension_semantics=sem, vmem_limit_bytes=VMEM_LIMIT)


def _rms(x):
    return x * lax.rsqrt(jnp.mean(x * x, axis=-1, keepdims=True) + EPS)


def _sigmoid(x):
    return 0.5 * jnp.tanh(0.5 * x) + 0.5


def _mod_kernel(cond_ref, w_ref, b_ref, o_ref):
    c = cond_ref[...]
    s = c * _sigmoid(c)
    out = jnp.dot(s.astype(BF16), w_ref[...].astype(BF16), preferred_element_type=F32) + b_ref[...]
    for r in range(o_ref.shape[0]):
        o_ref[r] = out[r:r + 1, :]


def _modulation(cond, w_mod, b_mod):
    n = w_mod.shape[1]
    return pl.pallas_call(
        _mod_kernel,
        out_shape=jax.ShapeDtypeStruct((cond.shape[0], 1, n), F32),
        grid=(n // MOD_COLS,),
        in_specs=[pl.BlockSpec(cond.shape, lambda j: (0, 0)),
                  pl.BlockSpec((D_MODEL, MOD_COLS), lambda j: (0, j)),
                  pl.BlockSpec((1, MOD_COLS), lambda j: (0, j))],
        out_specs=pl.BlockSpec((cond.shape[0], 1, MOD_COLS), lambda j: (0, 0, j)),
        compiler_params=_params(("arbitrary",)),
        name="modulation",
    )(cond, w_mod, b_mod.reshape(1, n))


def _pos_kernel(o_ref):
    n_freq = D_MODEL // 4
    k = lax.broadcasted_iota(jnp.int32, (GRID_W, n_freq), 1).astype(F32)
    p = lax.broadcasted_iota(jnp.int32, (GRID_W, n_freq), 0).astype(F32)
    freq = jnp.exp(-math.log(POS_BASE) * k / n_freq)
    ang = p * freq
    o_ref[:, 0:n_freq] = jnp.sin(ang)
    o_ref[:, n_freq:2 * n_freq] = jnp.cos(ang)


def _pos_table():
    return pl.pallas_call(
        _pos_kernel,
        out_shape=jax.ShapeDtypeStruct((GRID_W, D_MODEL // 2), F32),
        name="pos_table",
    )()


def _add_pos(x, pos_refs, q0):
    if pos_refs is None:
        return x
    rows_ref, cols_ref = pos_refs
    reps = x.shape[0] // GRID_W
    rpart = jnp.concatenate(
        [jnp.broadcast_to(rows_ref[q:q + 1, :], (GRID_W, D_MODEL // 2)) for q in range(q0, q0 + reps)], axis=0)
    cpart = jnp.concatenate([cols_ref[...]] * reps, axis=0)
    return jnp.concatenate([x[:, :D_MODEL // 2] + rpart, x[:, D_MODEL // 2:] + cpart], axis=1)


def _load_x(x_ref, pos_refs, r0, n):
    return _add_pos(x_ref[r0:r0 + n, :], pos_refs, r0 // GRID_W)


def _premix_kernel(*refs, add_pos):
    refs = list(refs)
    x_ref = refs.pop(0)
    nseq, tt = x_ref.shape[:2]
    pos_refs = (refs.pop(0), refs.pop(0)) if add_pos else None
    mod_ref, g_ref, win_ref, sgug_ref, sguw_ref, sgub_ref, xr_ref, gg_ref, ys_ref = refs
    hn = []
    for s in range(nseq):
        shift = mod_ref[s, :, 0:D_MODEL]
        scale = mod_ref[s, :, D_MODEL:2 * D_MODEL]
        hn.append(_rms(_add_pos(x_ref[s], pos_refs, 0)) * (g_ref[...] * (1.0 + scale)) + shift)
    z = jnp.dot(jnp.concatenate(hn, axis=0).astype(BF16), win_ref[...],
                preferred_element_type=F32)
    half = D_SGU // 2
    heads_per_half = N_HEADS_SGU // 2
    lane_head = lax.broadcasted_iota(jnp.int32, (CHUNK, half), 1) // HEAD_SGU
    for s in range(nseq):
        zs = z[s * tt:(s + 1) * tt]
        vn = (_rms(zs[:, 0:D_SGU]) * sgug_ref[...]).astype(BF16)
        u = zs[:, D_SGU:2 * D_SGU]
        for c in range(tt // CHUNK):
            rows = slice(c * CHUNK, (c + 1) * CHUNK)
            halves = []
            for hf in range(2):
                r = jnp.dot(sguw_ref[hf], vn[rows, hf * half:(hf + 1) * half],
                            preferred_element_type=F32)
                sel = jnp.zeros((CHUNK, half), F32)
                for h in range(heads_per_half):
                    sel = jnp.where(lane_head == h, r[h * CHUNK:(h + 1) * CHUNK], sel)
                halves.append(sel)
            gatev = jnp.concatenate(halves, axis=1) + sgub_ref[...]
            ys_ref[s, rows, :] = (u[rows] * gatev).astype(BF16)
    for s in range(nseq):
        zs = z[s * tt:(s + 1) * tt]
        gg = jax.nn.gelu(zs[:, 2 * D_SGU:2 * D_SGU + D_RNN])
        xr = zs[:, 2 * D_SGU + D_RNN:]
        for k in range(RNN_BLOCKS):
            first = (s // PAIR) * tt * TMJ_ROWS + (s % PAIR) * RNN_BLOCKS + k
            rows = pl.ds(first, tt, stride=TMJ_ROWS)
            xr_ref[rows, :] = xr[:, k * LANES:(k + 1) * LANES]
            gg_ref[rows, :] = gg[:, k * LANES:(k + 1) * LANES]


def _premix(x, mod3, cond_block, g_pre, w_in_b, sgu_g, sgu_w_b, sgu_bias_tile, pos_tab):
    n_seq, seq_len, _ = x.shape
    tt = min(seq_len, TT)
    nseq = max(PAIR, PREMIX_ROWS // tt)
    assert nseq == PAIR or tt == seq_len
    n_pairs, n_tiles = n_seq // PAIR, seq_len // tt
    add_pos = pos_tab is not None
    const2 = lambda p, j: (0, 0)
    in_specs = [pl.BlockSpec((nseq, tt, D_MODEL), lambda p, j: (p, j, 0))]
    args = [x]
    if add_pos:
        reps = tt // GRID_W
        in_specs += [pl.BlockSpec((None, reps, D_MODEL // 2), lambda p, j: (j, 0, 0)),
                     pl.BlockSpec((GRID_W, D_MODEL // 2), const2)]
        args += [pos_tab.reshape(GRID_W // reps, reps, D_MODEL // 2), pos_tab]
    in_specs += [pl.BlockSpec((nseq, 1, 6 * D_MODEL), lambda p, j: (cond_block(p), 0, 0)),
                 pl.BlockSpec((1, D_MODEL), const2),
                 pl.BlockSpec((D_MODEL, 2 * D_RNN + 2 * D_SGU), const2),
                 pl.BlockSpec((1, D_SGU), const2),
                 pl.BlockSpec((2, 4 * CHUNK, CHUNK), lambda p, j: (0, 0, 0)),
                 pl.BlockSpec((CHUNK, D_SGU), const2)]
    args += [mod3, g_pre, w_in_b, sgu_g, sgu_w_b, sgu_bias_tile]
    tmj = jax.ShapeDtypeStruct((n_pairs * seq_len * TMJ_ROWS, LANES), F32)
    tmj_spec = pl.BlockSpec((nseq // PAIR * tt * TMJ_ROWS, LANES), lambda p, j: (p * n_tiles + j, 0))
    xr, gg, y_sgu = pl.pallas_call(
        functools.partial(_premix_kernel, add_pos=add_pos),
        out_shape=(tmj, tmj, jax.ShapeDtypeStruct((n_seq, seq_len, D_SGU), BF16)),
        grid=(n_seq // nseq, n_tiles),
        in_specs=in_specs,
        out_specs=(tmj_spec, tmj_spec, pl.BlockSpec((nseq, tt, D_SGU), lambda p, j: (p, j, 0))),
        compiler_params=_params(("parallel", "parallel")),
        name="premix",
    )(*args)
    shape4 = (n_pairs, seq_len, TMJ_ROWS, LANES)
    return xr.reshape(shape4), gg.reshape(shape4), y_sgu


def _scan_kernel(*refs, reverse, n_chunks):
    if reverse:
        (xprev_ref, x_ref, xnext_ref, gg_ref, hf_ref, h0_ref, pt_ref, wg_ref,
         y_ref, fs_ref, xwin, xc_s, r_s, i_s, a_s, b_s, y_s, hcar) = refs
    else:
        (xprev_ref, x_ref, xnext_ref, h0_ref, pt_ref, wg_ref,
         hf_ref, fs_ref, xwin, xc_s, r_s, i_s, a_s, b_s, hcar) = refs
    cw = [pt_ref[k] for k in range(CONV_W)]
    cb, bias_a, bias_x, lam = (pt_ref[CONV_W + k] for k in range(4))
    c = pl.program_id(1)
    chunk = n_chunks - 1 - c if reverse else c
    sub_rows = TS * TMJ_ROWS

    def rows_of(pb, t0, n_steps):
        first = (pb * LC + t0) * TMJ_ROWS
        if not isinstance(first, int):
            first = pl.multiple_of(first, TMJ_ROWS)
        return pl.ds(first, n_steps * TMJ_ROWS)

    @pl.when(c == 0)
    def _():
        hcar[...] = h0_ref[...]

    xwin[:, 0:CONV_LEFT] = jnp.where(chunk > 0, xprev_ref[...], 0.0)
    xwin[:, LC + CONV_LEFT:LC + CONV_W - 1] = jnp.where(chunk < n_chunks - 1, xnext_ref[...], 0.0)

    xwin[:, CONV_LEFT:CONV_LEFT + LC] = x_ref[...]

    neg_lam = -lam
    softplus = jnp.maximum(neg_lam, 0.0) + jnp.log(1.0 + jnp.exp(-jnp.abs(neg_lam)))
    half_decay = (-0.5 * RG_C * math.log2(math.e)) * softplus

    def conv(pb):
        for t0 in range(0, LC, TS):
            xc = cb + cw[0] * xwin[pb, t0:t0 + TS]
            for k in range(1, CONV_W):
                xc = xc + cw[k] * xwin[pb, t0 + k:t0 + k + TS]
            xc_s[rows_of(pb, t0, TS), :] = xc.reshape(sub_rows, LANES)

    def gate_matmuls(pb):
        for k in range(RNN_BLOCKS):
            rows = pl.ds(pb * LC * TMJ_ROWS + k, LC * PAIR, stride=RNN_BLOCKS)
            g = jnp.dot(xc_s[rows, :].astype(BF16), wg_ref[k], preferred_element_type=F32)
            r_s[rows, :] = g[:, :LANES]
            i_s[rows, :] = g[:, LANES:]

    def gates(pb):
        for t0 in range(0, LC, TS):
            rows = rows_of(pb, t0, TS)
            tile = lambda ref: ref[rows, :].reshape(TS, TMJ_ROWS, LANES)
            tr = jnp.tanh(tile(r_s) + bias_a)
            ti = jnp.tanh(tile(i_s) + bias_x)
            log2_a = tr * half_decay + half_decay
            a = jnp.exp2(log2_a)
            q = jnp.tanh(log2_a * (-math.log(2.0))) * (a * a + 1.0)
            b = jnp.where(q > 0.0, q * lax.rsqrt(q), 0.0) * ((ti + 1.0) * tile(xc_s))
            a_s[rows, :] = a.reshape(sub_rows, LANES)
            b_s[rows, :] = b.reshape(sub_rows, LANES)

    for stage in (conv, gate_matmuls, gates):
        for pb in range(PB):
            stage(pb)

    half_steps = LC // 2

    def put(pb, t, rows, h):
        if reverse:
            y_s[rows, :] = h
        else:
            hf_ref[pb, t] = h

    def steps(jb, carry):
        lead, lag, prod = (list(c) for c in carry)
        sign = -1 if reverse else 1
        t_base = sign * jb * SCAN_UNROLL
        row_base = pl.multiple_of(t_base * TMJ_ROWS, TMJ_ROWS)
        for u in range(SCAN_UNROLL):
            lead_0 = LC - 1 - u if reverse else u
            lag_0 = half_steps - 1 - u if reverse else half_steps + u
            for pb in range(PB):
                rows_lead = pl.ds(row_base + (pb * LC + lead_0) * TMJ_ROWS, TMJ_ROWS)
                rows_lag = pl.ds(row_base + (pb * LC + lag_0) * TMJ_ROWS, TMJ_ROWS)
                a_lag = a_s[rows_lag, :]
                lead[pb] = a_s[rows_lead, :] * lead[pb] + b_s[rows_lead, :]
                lag[pb] = a_lag * lag[pb] + b_s[rows_lag, :]
                prod[pb] = a_lag * prod[pb]
                put(pb, t_base + lead_0, rows_lead, lead[pb])
                put(pb, t_base + lag_0, rows_lag, lag[pb])
                r_s[rows_lag, :] = prod[pb]
        return tuple(lead), tuple(lag), tuple(prod)

    zeros = tuple(jnp.zeros((TMJ_ROWS, LANES), F32) for _ in range(PB))
    ones = tuple(jnp.ones((TMJ_ROWS, LANES), F32) for _ in range(PB))
    lead, lag, prod = lax.fori_loop(0, half_steps // SCAN_UNROLL, steps,
                                    (tuple(hcar[pb] for pb in range(PB)), zeros, ones))
    lag_first = 0 if reverse else half_steps

    def completion(pb, t0):
        prods = r_s[rows_of(pb, t0, TS), :].reshape(TS, TMJ_ROWS, LANES)
        return (prods * lead[pb]).reshape(sub_rows, LANES)

    for pb in range(PB):
        final = lag[pb] + prod[pb] * lead[pb]
        hcar[pb] = final
        fs_ref[pb] = final
        if not reverse:
            for t0 in range(lag_first, lag_first + half_steps, TS):
                hf_ref[pb, t0:t0 + TS] = hf_ref[pb, t0:t0 + TS] + completion(pb, t0).reshape(TS, TMJ_ROWS, LANES)

    if reverse:
        for pb in range(PB):
            for t0 in range(0, LC, TS):
                rows = rows_of(pb, t0, TS)
                both = hf_ref[pb, t0:t0 + TS].reshape(sub_rows, LANES) + y_s[rows, :]
                if lag_first <= t0 < lag_first + half_steps:
                    both = both + completion(pb, t0)
                y_s[rows, :] = both * gg_ref[pb, t0:t0 + TS].reshape(sub_rows, LANES)
            for s in range(PAIR):
                cols = [y_s[pl.ds(pb * LC * TMJ_ROWS + s * RNN_BLOCKS + k, LC, stride=TMJ_ROWS), :]
                        for k in range(RNN_BLOCKS)]
                y_ref[pb * PAIR + s] = jnp.concatenate(cols, axis=1).astype(BF16)


def _scan(xr, gg, hf, h0, param_tiles, w_gates, direction):
    n_pairs, seq_len = xr.shape[:2]
    n_chunks = seq_len // LC
    reverse = direction == 1
    pos = (lambda c: n_chunks - 1 - c) if reverse else (lambda c: c)
    tmj_blk = pl.BlockSpec((PB, LC, TMJ_ROWS, LANES), lambda i, c: (i, pos(c), 0, 0))
    state_blk = pl.BlockSpec((PB, TMJ_ROWS, LANES), lambda i, c: (i, 0, 0))
    per_dir = lambda *shape: pl.BlockSpec((None,) + shape, lambda i, c: (direction,) + (0,) * len(shape))
    in_specs = [
        pl.BlockSpec((PB, CONV_LEFT, TMJ_ROWS, LANES),
                     lambda i, c: (i, jnp.maximum(pos(c) * (LC // CONV_LEFT) - 1, 0), 0, 0)),
        tmj_blk,
        pl.BlockSpec((PB, 1, TMJ_ROWS, LANES), lambda i, c: (i, jnp.minimum((pos(c) + 1) * LC, seq_len - 1), 0, 0)),
    ]
    args = [xr, xr, xr]
    if reverse:
        in_specs += [tmj_blk, tmj_blk]
        args += [gg, hf]
    in_specs += [state_blk,
                 per_dir(CONV_W + 4, TMJ_ROWS, LANES),
                 per_dir(RNN_BLOCKS, LANES, 2 * LANES)]
    args += [h0, param_tiles, w_gates]
    flat = pltpu.VMEM((PB * LC * TMJ_ROWS, LANES), F32)
    scratch = [pltpu.VMEM((PB, LC + CONV_W - 1, TMJ_ROWS, LANES), F32)] + [flat] * (6 if reverse else 5)
    scratch += [pltpu.VMEM((PB, TMJ_ROWS, LANES), F32)]
    state = jax.ShapeDtypeStruct((n_pairs, TMJ_ROWS, LANES), F32)
    if reverse:
        out_shape = (jax.ShapeDtypeStruct((n_pairs * PAIR, seq_len, D_RNN), BF16), state)
        out_specs = (pl.BlockSpec((PB * PAIR, LC, D_RNN), lambda i, c: (i, pos(c), 0)), state_blk)
    else:
        out_shape = (jax.ShapeDtypeStruct(xr.shape, F32), state)
        out_specs = (tmj_blk, state_blk)
    return pl.pallas_call(
        functools.partial(_scan_kernel, reverse=reverse, n_chunks=n_chunks),
        out_shape=out_shape,
        grid=(n_pairs // PB, n_chunks),
        in_specs=in_specs,
        out_specs=out_specs,
        scratch_shapes=scratch,
        compiler_params=_params(("parallel", "arbitrary")),
        name="scan_bwd" if reverse else "scan_fwd",
    )(*args)


def _route(lt):
    n = lt.shape[1]
    row = lax.broadcasted_iota(jnp.int32, (EXPERTS_PER_GROUP, n), 0)
    neg = jnp.float32(-jnp.inf)

    def arg_max(v):
        m = jnp.max(v, axis=0, keepdims=True)
        return jnp.min(jnp.where(v == m, row, EXPERTS_PER_GROUP), axis=0, keepdims=True)

    g_idx = arg_max(lt[0:N_GROUPS])
    el = lt[E_ROW0:E_ROW0 + EXPERTS_PER_GROUP]
    for g in range(1, N_GROUPS):
        first = E_ROW0 + g * EXPERTS_PER_GROUP
        el = jnp.where(g_idx == g, lt[first:first + EXPERTS_PER_GROUP], el)
    i1 = arg_max(el)
    i2 = arg_max(jnp.where(row == i1, neg, el))
    ja = jnp.minimum(i1, i2)
    jb = jnp.maximum(i1, i2)
    pair = (ja * (2 * EXPERTS_PER_GROUP - 1 - ja)) // 2 + (jb - ja - 1)
    return g_idx * PAIRS_PER_GROUP + pair


def _store_token_major(ref, x, t0=0):
    n = x.shape[0]
    for k in range(ROW_TILES):
        ref[pl.ds(t0 * ROW_TILES + k, n, stride=ROW_TILES), :] = x[:, k * LANES:(k + 1) * LANES]


def _load_token_major(ref, n):
    return jnp.concatenate([ref[pl.ds(k, n, stride=ROW_TILES), :] for k in range(ROW_TILES)], axis=1)


def _postmix_kernel(*refs, add_pos):
    refs = list(refs)
    x_ref = refs.pop(0)
    pos_refs = (refs.pop(0), refs.pop(0)) if add_pos else None
    (yr_ref, ys_ref, mod_ref, gpost_ref, gpre_ref, wout_ref, rw_ref, rb_ref, earlier_ref, cnt0_ref,
     x1_ref, hn_ref, rt_ref, cnt_ref, run_ref) = refs

    @pl.when(pl.program_id(0) == 0)
    def _():
        run_ref[...] = cnt0_ref[...]

    gate1 = mod_ref[0, :, 2 * D_MODEL:3 * D_MODEL]
    shift2 = mod_ref[0, :, 3 * D_MODEL:4 * D_MODEL]
    scale2 = mod_ref[0, :, 4 * D_MODEL:5 * D_MODEL]
    y = (jnp.dot(yr_ref[...], wout_ref[0:D_RNN, :], preferred_element_type=F32)
         + jnp.dot(ys_ref[...], wout_ref[D_RNN:, :], preferred_element_type=F32))
    x1 = _load_x(x_ref, pos_refs, 0, TP) + _rms(y) * (gate1 * gpost_ref[...])
    x1_ref[...] = x1
    hn = _rms(x1) * (gpre_ref[...] * (1.0 + scale2)) + shift2
    _store_token_major(hn_ref, hn)
    lt = lax.dot_general(rw_ref[...], hn.astype(BF16), (((1,), (1,)), ((), ())),
                         preferred_element_type=F32) + rb_ref[:, 0:1]
    bucket = _route(lt)
    onehot = lax.broadcasted_iota(jnp.int32, (ROUTER_ROWS, TP), 0) == bucket
    before = jnp.dot(onehot.astype(BF16), earlier_ref[...], preferred_element_type=F32) + run_ref[:, 0:1]
    rank = jnp.sum(jnp.where(onehot, before, 0.0), axis=0, keepdims=True).astype(jnp.int32)
    row = lax.broadcasted_iota(jnp.int32, (SUBLANES, TP), 0)
    rt_ref[...] = jnp.where(row == 0, bucket, jnp.where(row == 1, rank, 0))
    run_ref[...] += jnp.sum(onehot.astype(F32), axis=1, keepdims=True)
    cnt_ref[...] = run_ref[...]


def _postmix(x, y_rnn, y_sgu, mod3, cond_of_tile, g_post, g_pre, w_out_b, router_wt, router_bt, earlier, pos_tab,
             counts0):
    n_tok = x.shape[0]
    n_tiles = n_tok // TP
    add_pos = pos_tab is not None
    tok = lambda i: (i, 0)
    const2 = lambda i: (0, 0)
    in_specs = [pl.BlockSpec((TP, D_MODEL), tok)]
    args = [x]
    if add_pos:
        reps = TP // GRID_W
        tiles_per_seq = GRID_W // reps
        in_specs += [pl.BlockSpec((None, reps, D_MODEL // 2), lambda i: (i % tiles_per_seq, 0, 0)),
                     pl.BlockSpec((GRID_W, D_MODEL // 2), const2)]
        args += [pos_tab.reshape(tiles_per_seq, reps, D_MODEL // 2), pos_tab]
    in_specs += [pl.BlockSpec((TP, D_RNN), tok),
                 pl.BlockSpec((TP, D_SGU), tok),
                 pl.BlockSpec((1, 1, 6 * D_MODEL), lambda i: (cond_of_tile(i), 0, 0)),
                 pl.BlockSpec((1, D_MODEL), const2),
                 pl.BlockSpec((1, D_MODEL), const2),
                 pl.BlockSpec((D_MODEL, D_MODEL), const2),
                 pl.BlockSpec((ROUTER_ROWS, D_MODEL), const2),
                 pl.BlockSpec((ROUTER_ROWS, LANES), const2),
                 pl.BlockSpec((TP, TP), const2),
                 pl.BlockSpec((ROUTER_ROWS, LANES), const2)]
    args += [y_rnn, y_sgu, mod3, g_post, g_pre, w_out_b, router_wt, router_bt, earlier, counts0]
    counts_spec = pl.BlockSpec((ROUTER_ROWS, LANES), const2)
    x1, hn, route, counts = pl.pallas_call(
        functools.partial(_postmix_kernel, add_pos=add_pos),
        out_shape=(jax.ShapeDtypeStruct((n_tok, D_MODEL), F32),
                   jax.ShapeDtypeStruct((n_tok * ROW_TILES, LANES), F32),
                   jax.ShapeDtypeStruct((n_tiles * SUBLANES, TP), jnp.int32),
                   jax.ShapeDtypeStruct((ROUTER_ROWS, LANES), F32)),
        grid=(n_tiles,),
        in_specs=in_specs,
        out_specs=(pl.BlockSpec((TP, D_MODEL), tok),
                   pl.BlockSpec((TP * ROW_TILES, LANES), tok),
                   pl.BlockSpec((SUBLANES, TP), tok),
                   counts_spec),
        scratch_shapes=[pltpu.VMEM((ROUTER_ROWS, LANES), F32)],
        compiler_params=_params(("arbitrary",)),
        name="postmix",
    )(*args)
    route = route.reshape(n_tiles, SUBLANES, TP)
    return x1, hn, route[:, 0].reshape(n_tok), route[:, 1].reshape(n_tok), counts


def _token_rows(ref, t):
    return ref.at[pl.ds(pl.multiple_of(t * ROW_TILES, ROW_TILES), ROW_TILES), :]


def _dispatch_kernel(dest_ref, hc_ref, hs_ref, xs_ref, sem, *, n_ctx_steps):
    i = pl.program_id(0)
    base = i * TD

    def scatter(src_ref):
        def start(g, carry):
            for u in range(DMA_UNROLL):
                r = g * DMA_UNROLL + u
                pltpu.make_async_copy(_token_rows(src_ref, r), _token_rows(xs_ref, dest_ref[base + r]),
                                      sem).start(priority=u % 2)
            return carry

        lax.fori_loop(0, TD // DMA_UNROLL, start, 0)
        pltpu.make_async_copy(src_ref, xs_ref.at[pl.ds(0, TD * ROW_TILES), :], sem).wait()

    @pl.when(i < n_ctx_steps)
    def _():
        scatter(hc_ref)

    @pl.when(i >= n_ctx_steps)
    def _():
        scatter(hs_ref)


def _dispatch(dest, hn_ctx, hn_dec, n_slots):
    n_ctx_steps = hn_ctx.shape[0] // (TD * ROW_TILES)
    n_dec_steps = hn_dec.shape[0] // (TD * ROW_TILES)
    return pl.pallas_call(
        functools.partial(_dispatch_kernel, n_ctx_steps=n_ctx_steps),
        out_shape=jax.ShapeDtypeStruct((n_slots * ROW_TILES, LANES), F32),
        grid_spec=pltpu.PrefetchScalarGridSpec(
            num_scalar_prefetch=1,
            grid=(n_ctx_steps + n_dec_steps,),
            in_specs=[pl.BlockSpec((TD * ROW_TILES, LANES), lambda i, d: (jnp.minimum(i, n_ctx_steps - 1), 0)),
                      pl.BlockSpec((TD * ROW_TILES, LANES), lambda i, d: (jnp.maximum(i - n_ctx_steps, 0), 0))],
            out_specs=pl.BlockSpec(memory_space=pl.ANY),
            scratch_shapes=[pltpu.SemaphoreType.DMA(())]),
        compiler_params=_params(("arbitrary",)),
        name="dispatch",
    )(dest, hn_ctx, hn_dec)


def _experts_kernel(ea_ref, eb_ref, nv_ref, blk_ref, xs_ref, rw_ref, rb_ref, *refs):
    w32_refs, ys_ref, w_refs = refs[:6], refs[6], refs[7:]
    wga_ref, wua_ref, wda_ref, wgb_ref, wub_ref, wdb_ref = w_refs
    i = pl.program_id(0)
    nv = nv_ref[i]
    prev = jnp.maximum(i - 1, 0)

    for e_ref, first in ((ea_ref, 0), (eb_ref, 3)):
        @pl.when((i == 0) | (e_ref[i] != e_ref[prev]))
        def _(first=first):
            for w32_ref, w_ref in zip(w32_refs[first:first + 3], w_refs[first:first + 3]):
                w_ref[...] = w32_ref[0].astype(BF16)

    @pl.when(nv > 0)
    def _():
        row = lax.broadcasted_iota(jnp.int32, (TMX, 1), 0)
        xb = jnp.where(row < nv, _load_token_major(xs_ref, TMX), 0.0).astype(BF16)
        logits = jnp.dot(xb, rw_ref[...], preferred_element_type=F32) + rb_ref[...]
        lane = lax.broadcasted_iota(jnp.int32, logits.shape, 1)
        ea = ea_ref[i]
        eb = eb_ref[i]
        gmask = lane < N_GROUPS
        gl = jnp.where(gmask, logits, -jnp.inf)
        gmax = jnp.max(gl, axis=-1, keepdims=True)
        gexp = jnp.where(gmask, jnp.exp(gl - gmax), 0.0)
        g_own = jnp.sum(jnp.where(lane == ea // EXPERTS_PER_GROUP, gexp, 0.0), axis=-1, keepdims=True)
        g_w = g_own / jnp.sum(gexp, axis=-1, keepdims=True)
        la = jnp.sum(jnp.where(lane == ea + E_LANE0, logits, 0.0), axis=-1, keepdims=True)
        lb = jnp.sum(jnp.where(lane == eb + E_LANE0, logits, 0.0), axis=-1, keepdims=True)
        m = jnp.maximum(la, lb)
        pa = jnp.exp(la - m)
        pb = jnp.exp(lb - m)
        inv = g_w / (pa + pb)

        def hidden(x, wg_ref, wu_ref, w):
            g = jnp.dot(x, wg_ref[...], preferred_element_type=F32)
            u = jnp.dot(x, wu_ref[...], preferred_element_type=F32)
            return ((g * _sigmoid(g)) * u * w).astype(BF16)

        part = TMX // EXPERT_ROW_PARTS
        ys = []
        for h in range(EXPERT_ROW_PARTS):
            rows = slice(h * part, (h + 1) * part)
            act_a = hidden(xb[rows], wga_ref, wua_ref, (pa * inv)[rows])
            act_b = hidden(xb[rows], wgb_ref, wub_ref, (pb * inv)[rows])
            y = (jnp.dot(act_a, wda_ref[...], preferred_element_type=F32)
                 + jnp.dot(act_b, wdb_ref[...], preferred_element_type=F32))
            ys.append(y)
        _store_token_major(ys_ref, jnp.concatenate(ys, axis=0))


def _experts(sched, xs, router_w, router_b, wg, wu, wd):
    ea, eb, nv, blk = sched
    n_tiles = ea.shape[0]
    rows = lambda i, ea, eb, nv, blk: (blk[i], 0)
    const2 = lambda i, ea, eb, nv, blk: (0, 0)
    exp_a = lambda i, ea, eb, nv, blk: (ea[i], 0, 0)
    exp_b = lambda i, ea, eb, nv, blk: (eb[i], 0, 0)
    w_in_spec = lambda m: pl.BlockSpec((1, D_MODEL, D_EXPERT), m)
    w_out_spec = lambda m: pl.BlockSpec((1, D_EXPERT, D_MODEL), m)
    return pl.pallas_call(
        _experts_kernel,
        out_shape=jax.ShapeDtypeStruct(xs.shape, F32),
        grid_spec=pltpu.PrefetchScalarGridSpec(
            num_scalar_prefetch=4,
            grid=(n_tiles,),
            in_specs=[pl.BlockSpec((TMX * ROW_TILES, LANES), rows),
                      pl.BlockSpec((D_MODEL, ROUTER_LANES), const2),
                      pl.BlockSpec((1, ROUTER_LANES), const2),
                      w_in_spec(exp_a), w_in_spec(exp_a), w_out_spec(exp_a),
                      w_in_spec(exp_b), w_in_spec(exp_b), w_out_spec(exp_b)],
            out_specs=pl.BlockSpec((TMX * ROW_TILES, LANES), rows),
            scratch_shapes=[pltpu.VMEM((D_MODEL, D_EXPERT), BF16), pltpu.VMEM((D_MODEL, D_EXPERT), BF16),
                            pltpu.VMEM((D_EXPERT, D_MODEL), BF16)] * 2),
        compiler_params=_params(("arbitrary",)),
        name="experts",
    )(ea, eb, nv, blk, xs, router_w, router_b, wg, wu, wd, wg, wu, wd)


def _combine_kernel(dest_ref, ys_ref, x1_ref, mod_ref, gpost_ref, o_ref, ybuf, sems, *, tile0):
    i = pl.program_id(0)
    n = pl.num_programs(0)

    def fetch(tile, slot):
        base = (tile + tile0) * TC

        def start(g, carry):
            for u in range(DMA_UNROLL):
                r = g * DMA_UNROLL + u
                pltpu.make_async_copy(_token_rows(ys_ref, dest_ref[base + r]), _token_rows(ybuf.at[slot], r),
                                      sems.at[slot]).start(priority=u % 2)
            return carry

        lax.fori_loop(0, TC // DMA_UNROLL, start, 0)

    @pl.when(i == 0)
    def _():
        fetch(0, 0)

    @pl.when(i + 1 < n)
    def _():
        fetch(i + 1, (i + 1) % 2)

    slot = i % 2
    pltpu.make_async_copy(ys_ref.at[pl.ds(0, TC * ROW_TILES), :], ybuf.at[slot], sems.at[slot]).wait()
    gate2 = mod_ref[0, :, 5 * D_MODEL:6 * D_MODEL]
    o_ref[...] = x1_ref[...] + _rms(_load_token_major(ybuf.at[slot], TC)) * (gate2 * gpost_ref[...])


def _combine(dest, ys, x1, mod3, g_post, cond_of_tile, tile0):
    n_tok = x1.shape[0]
    return pl.pallas_call(
        functools.partial(_combine_kernel, tile0=tile0),
        out_shape=jax.ShapeDtypeStruct((n_tok, D_MODEL), F32),
        grid_spec=pltpu.PrefetchScalarGridSpec(
            num_scalar_prefetch=1,
            grid=(n_tok // TC,),
            in_specs=[pl.BlockSpec(memory_space=pl.ANY),
                      pl.BlockSpec((TC, D_MODEL), lambda i, d: (i, 0)),
                      pl.BlockSpec((1, 1, 6 * D_MODEL), lambda i, d: (cond_of_tile(i), 0, 0)),
                      pl.BlockSpec((1, D_MODEL), lambda i, d: (0, 0))],
            out_specs=pl.BlockSpec((TC, D_MODEL), lambda i, d: (i, 0)),
            scratch_shapes=[pltpu.VMEM((2, TC * ROW_TILES, LANES), F32), pltpu.SemaphoreType.DMA((2,))]),
        compiler_params=_params(("arbitrary",)),
        name="combine",
    )(dest, ys, x1, mod3, g_post)


def _schedule(bucket, rank, counts):
    n_tok = bucket.shape[0]
    n_max = n_tok // TMX + N_BUCKETS
    cnt = counts[:N_BUCKETS, 0].astype(jnp.int32)
    tiles = (cnt + TMX - 1) // TMX
    tile_end = jnp.cumsum(tiles)
    tile_start = tile_end - tiles
    ids = jnp.arange(N_BUCKETS, dtype=jnp.int32)
    slot0 = jnp.sum(jnp.where(bucket[:, None] == ids[None, :], (tile_start * TMX)[None, :], 0), axis=1)
    dest = slot0 + rank
    i = jnp.arange(n_max, dtype=jnp.int32)
    total = tile_end[-1]
    valid = i < total
    tb = jnp.sum((jnp.minimum(i, total - 1)[:, None] >= tile_end[None, :]).astype(jnp.int32), axis=1)
    pairs = [(a, b) for a in range(EXPERTS_PER_GROUP) for b in range(a + 1, EXPERTS_PER_GROUP)]
    ea_tab = jnp.array([g * EXPERTS_PER_GROUP + a for g in range(N_GROUPS) for a, _ in pairs], jnp.int32)
    eb_tab = jnp.array([g * EXPERTS_PER_GROUP + b for g in range(N_GROUPS) for _, b in pairs], jnp.int32)
    hit = tb[:, None] == ids[None, :]
    look = lambda tab: jnp.sum(jnp.where(hit, tab[None, :], 0), axis=1)
    ea, eb = look(ea_tab), look(eb_tab)
    nv = jnp.where(valid, jnp.clip(look(cnt) - (i - look(tile_start)) * TMX, 0, TMX), 0)
    blk = jnp.minimum(i, total - 1)
    return dest, (ea, eb, nv, blk), n_max * TMX


def _block_diag_gates(rg_wa, rg_wx):
    heads = LANES // HEAD_RNN

    def bd(w):
        w = w.reshape(2, RNN_BLOCKS, heads, HEAD_RNN, HEAD_RNN)
        eye = jnp.eye(heads, dtype=w.dtype)
        full = jnp.einsum('dghij,hk->dghikj', w, eye)
        return full.reshape(2, RNN_BLOCKS, LANES, LANES)

    return jnp.concatenate([bd(rg_wa), bd(rg_wx)], axis=-1).astype(BF16)


def _row_tile(v):
    blocks = v.reshape(v.shape[:-1] + (RNN_BLOCKS, LANES))
    return jnp.concatenate([blocks] * PAIR, axis=-2)


def _to_time_major_state(h):
    return h.reshape(h.shape[0] // PAIR, TMJ_ROWS, LANES)


def kernel(x_prompt, x_sample, state_rglru, c, c_ctx, w_mod, b_mod, g_pre_mix, g_post_mix, g_pre_ffn,
           g_post_ffn, w_in, conv_w, conv_b, rg_wa, rg_ba, rg_wx, rg_bx, rg_lambda, sgu_g, sgu_w, sgu_b,
           w_out, router_g_w, router_g_b, router_e_w, router_e_b, exp_w_gate, exp_w_up, exp_w_down):
    assert w_mod.shape[0] == 1, "single-layer trunk"
    n_ctx, ctx_len, _ = x_prompt.shape
    n_dec, dec_len, _ = x_sample.shape
    l = 0

    n_cond = SUBLANES
    ctx_rows = n_cond - n_dec
    ctx_per_step = max(PAIR, PREMIX_ROWS // min(ctx_len, TT))
    assert n_dec % PAIR == 0 and ctx_rows == ctx_per_step and n_dec % ctx_per_step == 0
    cond = jnp.concatenate([c, jnp.broadcast_to(c_ctx, (ctx_rows, D_MODEL))], axis=0)
    mod3 = _modulation(cond, w_mod[l], b_mod[l])
    pos_tab = _pos_table()

    w_rx, w_gate, w_u, w_v = jnp.split(w_in[l], [D_RNN, 2 * D_RNN, 2 * D_RNN + D_SGU], axis=1)
    w_in_b = jnp.concatenate([w_v, w_u, w_gate, w_rx], axis=1).astype(BF16)
    w_out_b = w_out[l].astype(BF16)
    sgu_w_b = sgu_w[l].reshape(2, 4 * CHUNK, CHUNK).astype(BF16)
    sgu_bias_tile = jnp.repeat(sgu_b[l].T, HEAD_SGU, axis=1)
    w_gates = _block_diag_gates(rg_wa[l], rg_wx[l])
    conv_rows = 0.5 * jnp.concatenate([conv_w[l], conv_b[l][None]], axis=0)
    per_dir_rows = jnp.concatenate([jnp.broadcast_to(conv_rows, (2,) + conv_rows.shape), 0.5 * rg_ba[l][:, None],
                                    0.5 * rg_bx[l][:, None], rg_lambda[l][:, None]], axis=1)
    scan_tiles = _row_tile(per_dir_rows)
    lane_pad = ROUTER_LANES - E_LANE0 - N_EXPERTS
    router_w = jnp.pad(jnp.concatenate([router_g_w[l], router_e_w[l]], axis=1), ((0, 0), (0, lane_pad))).astype(BF16)
    router_b = jnp.pad(jnp.concatenate([router_g_b[l], router_e_b[l]]), (0, lane_pad)).reshape(1, ROUTER_LANES)
    gap, tail = E_ROW0 - N_GROUPS, ROUTER_ROWS - E_ROW0 - N_EXPERTS
    router_wt = jnp.concatenate([router_g_w[l].T, jnp.zeros((gap, D_MODEL), F32), router_e_w[l].T,
                                 jnp.zeros((tail, D_MODEL), F32)], axis=0).astype(BF16)
    router_bt = jnp.concatenate([router_g_b[l], jnp.zeros((gap,), F32), router_e_b[l], jnp.zeros((tail,), F32)])
    router_bt = jnp.broadcast_to(router_bt[:, None], (ROUTER_ROWS, LANES))
    earlier = jnp.asarray(np.triu(np.ones((TP, TP), BF16), k=1))
    row = lambda v: v.reshape(1, -1)

    n_ctx_tok = n_ctx * ctx_len

    def mixer(x, h0, cond_of_tile, cond_block, use_pos, counts0):
        n_seq, seq_len, _ = x.shape
        xf = x.reshape(n_seq * seq_len, D_MODEL)
        tab = pos_tab if use_pos else None
        xr, gg, y_sgu = _premix(x, mod3, cond_block, row(g_pre_mix[l]), w_in_b, row(sgu_g[l]),
                                sgu_w_b, sgu_bias_tile, tab)
        scan_params = (scan_tiles, w_gates)
        hf, hf_last = _scan(xr, None, None, _to_time_major_state(h0[:, 0]), *scan_params, direction=0)
        y_rnn, hb_first = _scan(xr, gg, hf, _to_time_major_state(h0[:, 1]), *scan_params, direction=1)
        fstate = jnp.stack([hf_last.reshape(n_seq, D_RNN), hb_first.reshape(n_seq, D_RNN)], axis=1)
        x1, hn, bucket, rank, counts = _postmix(
            xf, y_rnn.reshape(n_seq * seq_len, D_RNN), y_sgu.reshape(n_seq * seq_len, D_SGU), mod3,
            cond_of_tile, row(g_post_mix[l]), row(g_pre_ffn[l]), w_out_b, router_wt, router_bt, earlier, tab,
            counts0)
        return x1, hn, bucket, rank, counts, fstate

    ctx_cond = lambda i: n_dec
    dec_cond = lambda i: i // (dec_len // TP)
    h0_ctx = jnp.zeros((n_ctx, 2, D_RNN), F32)
    counts0 = jnp.zeros((ROUTER_ROWS, LANES), F32)
    x1_ctx, hn_ctx, bucket_ctx, rank_ctx, counts, st = mixer(x_prompt, h0_ctx, ctx_cond, lambda p: n_dec // ctx_per_step,
                                                             False, counts0)
    new_state = st.astype(state_rglru.dtype)[:, None]
    x1_dec, hn_dec, bucket_dec, rank_dec, counts, _ = mixer(x_sample, state_rglru[:, l].astype(F32), dec_cond,
                                                            lambda p: p, True, counts)

    dest, sched, n_slots = _schedule(jnp.concatenate([bucket_ctx, bucket_dec]),
                                     jnp.concatenate([rank_ctx, rank_dec]), counts)
    xs = _dispatch(dest, hn_ctx, hn_dec, n_slots)
    ys = _experts(sched, xs, router_w, router_b, exp_w_gate[l], exp_w_up[l], exp_w_down[l])
    y_prompt = _combine(dest, ys, x1_ctx, mod3, row(g_post_ffn[l]), ctx_cond, 0)
    y_sample = _combine(dest, ys, x1_dec, mod3, row(g_post_ffn[l]), lambda i: i // (dec_len // TC), n_ctx_tok // TC)
    return (y_prompt.reshape(x_prompt.shape), y_sample.reshape(x_sample.shape), new_state)
```

```python
import functools
import math

import jax
import jax.numpy as jnp
import numpy as np
from jax import lax
from jax.experimental import pallas as pl
from jax.experimental.pallas import tpu as pltpu

D_MODEL = 1024
D_RNN = 512
D_SGU = 512
N_HEADS_RNN = 8
HEAD_RNN = D_RNN // N_HEADS_RNN
N_HEADS_SGU = 8
HEAD_SGU = D_SGU // N_HEADS_SGU
CHUNK = 128
GRID_W = 64
RG_C = 8.0
N_GROUPS = 4
EXPERTS_PER_GROUP = 4
N_EXPERTS = N_GROUPS * EXPERTS_PER_GROUP
D_EXPERT = 512
EPS = 1e-6
POS_BASE = 10000.0

LANES = 128
SUBLANES = 8
CONV_W = 4
CONV_LEFT = 2
PAIR = 2
RNN_BLOCKS = D_RNN // LANES
TMJ_ROWS = PAIR * RNN_BLOCKS
ROUTER_LANES = LANES
E_LANE0 = N_GROUPS
ROUTER_ROWS = 32
E_ROW0 = SUBLANES

PAIRS_PER_GROUP = EXPERTS_PER_GROUP * (EXPERTS_PER_GROUP - 1) // 2
N_BUCKETS = N_GROUPS * PAIRS_PER_GROUP

ROW_TILES = D_MODEL // LANES

MOD_COLS = 2048
TP = 1024
TD = 2048
TC = 512
DMA_UNROLL = 8
EXPERT_ROW_PARTS = 2
TMX = 512
TT = 512
PREMIX_ROWS = 1024
LC = 256
TS = 16
PB = 2
SCAN_UNROLL = 8
VMEM_LIMIT = 56 * 1024 * 1024

F32 = jnp.float32
BF16 = jnp.bfloat16


def _params(sem):
    return pltpu.CompilerParams(dimension_semantics=sem, vmem_limit_bytes=VMEM_LIMIT)


def _rms(x):
    return x * lax.rsqrt(jnp.mean(x * x, axis=-1, keepdims=True) + EPS)


def _sigmoid(x):
    return 0.5 * jnp.tanh(0.5 * x) + 0.5


def _mod_kernel(cond_ref, w_ref, b_ref, o_ref):
    c = cond_ref[...]
    s = c * _sigmoid(c)
    out = jnp.dot(s.astype(BF16), w_ref[...].astype(BF16), preferred_element_type=F32) + b_ref[...]
    for r in range(o_ref.shape[0]):
        o_ref[r] = out[r:r + 1, :]


def _modulation(cond, w_mod, b_mod):
    n = w_mod.shape[1]
    return pl.pallas_call(
        _mod_kernel,
        out_shape=jax.ShapeDtypeStruct((cond.shape[0], 1, n), F32),
        grid=(n // MOD_COLS,),
        in_specs=[pl.BlockSpec(cond.shape, lambda j: (0, 0)),
                  pl.BlockSpec((D_MODEL, MOD_COLS), lambda j: (0, j)),
                  pl.BlockSpec((1, MOD_COLS), lambda j: (0, j))],
        out_specs=pl.BlockSpec((cond.shape[0], 1, MOD_COLS), lambda j: (0, 0, j)),
        compiler_params=_params(("arbitrary",)),
        name="modulation",
    )(cond, w_mod, b_mod.reshape(1, n))


def _pos_kernel(o_ref):
    n_freq = D_MODEL // 4
    k = lax.broadcasted_iota(jnp.int32, (GRID_W, n_freq), 1).astype(F32)
    p = lax.broadcasted_iota(jnp.int32, (GRID_W, n_freq), 0).astype(F32)
    freq = jnp.exp(-math.log(POS_BASE) * k / n_freq)
    ang = p * freq
    o_ref[:, 0:n_freq] = jnp.sin(ang)
    o_ref[:, n_freq:2 * n_freq] = jnp.cos(ang)


def _pos_table():
    return pl.pallas_call(
        _pos_kernel,
        out_shape=jax.ShapeDtypeStruct((GRID_W, D_MODEL // 2), F32),
        name="pos_table",
    )()


def _add_pos(x, pos_refs, q0):
    if pos_refs is None:
        return x
    rows_ref, cols_ref = pos_refs
    reps = x.shape[0] // GRID_W
    rpart = jnp.concatenate(
        [jnp.broadcast_to(rows_ref[q:q + 1, :], (GRID_W, D_MODEL // 2)) for q in range(q0, q0 + reps)], axis=0)
    cpart = jnp.concatenate([cols_ref[...]] * reps, axis=0)
    return jnp.concatenate([x[:, :D_MODEL // 2] + rpart, x[:, D_MODEL // 2:] + cpart], axis=1)


def _load_x(x_ref, pos_refs, r0, n):
    return _add_pos(x_ref[r0:r0 + n, :], pos_refs, r0 // GRID_W)


def _premix_kernel(*refs, add_pos):
    refs = list(refs)
    x_ref = refs.pop(0)
    nseq, tt = x_ref.shape[:2]
    pos_refs = (refs.pop(0), refs.pop(0)) if add_pos else None
    mod_ref, g_ref, win_ref, sgug_ref, sguw_ref, sgub_ref, xr_ref, gg_ref, ys_ref = refs
    hn = []
    for s in range(nseq):
        shift = mod_ref[s, :, 0:D_MODEL]
        scale = mod_ref[s, :, D_MODEL:2 * D_MODEL]
        hn.append(_rms(_add_pos(x_ref[s], pos_refs, 0)) * (g_ref[...] * (1.0 + scale)) + shift)
    z = jnp.dot(jnp.concatenate(hn, axis=0).astype(BF16), win_ref[...],
                preferred_element_type=F32)
    half = D_SGU // 2
    heads_per_half = N_HEADS_SGU // 2
    lane_head = lax.broadcasted_iota(jnp.int32, (CHUNK, half), 1) // HEAD_SGU
    for s in range(nseq):
        zs = z[s * tt:(s + 1) * tt]
        vn = (_rms(zs[:, 0:D_SGU]) * sgug_ref[...]).astype(BF16)
        u = zs[:, D_SGU:2 * D_SGU]
        for c in range(tt // CHUNK):
            rows = slice(c * CHUNK, (c + 1) * CHUNK)
            halves = []
            for hf in range(2):
                r = jnp.dot(sguw_ref[hf], vn[rows, hf * half:(hf + 1) * half],
                            preferred_element_type=F32)
                sel = jnp.zeros((CHUNK, half), F32)
                for h in range(heads_per_half):
                    sel = jnp.where(lane_head == h, r[h * CHUNK:(h + 1) * CHUNK], sel)
                halves.append(sel)
            gatev = jnp.concatenate(halves, axis=1) + sgub_ref[...]
            ys_ref[s, rows, :] = (u[rows] * gatev).astype(BF16)
    for s in range(nseq):
        zs = z[s * tt:(s + 1) * tt]
        gg = jax.nn.gelu(zs[:, 2 * D_SGU:2 * D_SGU + D_RNN])
        xr = zs[:, 2 * D_SGU + D_RNN:]
        for k in range(RNN_BLOCKS):
            first = (s // PAIR) * tt * TMJ_ROWS + (s % PAIR) * RNN_BLOCKS + k
            rows = pl.ds(first, tt, stride=TMJ_ROWS)
            xr_ref[rows, :] = xr[:, k * LANES:(k + 1) * LANES]
            gg_ref[rows, :] = gg[:, k * LANES:(k + 1) * LANES]


def _premix(x, mod3, cond_block, g_pre, w_in_b, sgu_g, sgu_w_b, sgu_bias_tile, pos_tab):
    n_seq, seq_len, _ = x.shape
    tt = min(seq_len, TT)
    nseq = max(PAIR, PREMIX_ROWS // tt)
    assert nseq == PAIR or tt == seq_len
    n_pairs, n_tiles = n_seq // PAIR, seq_len // tt
    add_pos = pos_tab is not None
    const2 = lambda p, j: (0, 0)
    in_specs = [pl.BlockSpec((nseq, tt, D_MODEL), lambda p, j: (p, j, 0))]
    args = [x]
    if add_pos:
        reps = tt // GRID_W
        in_specs += [pl.BlockSpec((None, reps, D_MODEL // 2), lambda p, j: (j, 0, 0)),
                     pl.BlockSpec((GRID_W, D_MODEL // 2), const2)]
        args += [pos_tab.reshape(GRID_W // reps, reps, D_MODEL // 2), pos_tab]
    in_specs += [pl.BlockSpec((nseq, 1, 6 * D_MODEL), lambda p, j: (cond_block(p), 0, 0)),
                 pl.BlockSpec((1, D_MODEL), const2),
                 pl.BlockSpec((D_MODEL, 2 * D_RNN + 2 * D_SGU), const2),
                 pl.BlockSpec((1, D_SGU), const2),
                 pl.BlockSpec((2, 4 * CHUNK, CHUNK), lambda p, j: (0, 0, 0)),
                 pl.BlockSpec((CHUNK, D_SGU), const2)]
    args += [mod3, g_pre, w_in_b, sgu_g, sgu_w_b, sgu_bias_tile]
    tmj = jax.ShapeDtypeStruct((n_pairs * seq_len * TMJ_ROWS, LANES), F32)
    tmj_spec = pl.BlockSpec((nseq // PAIR * tt * TMJ_ROWS, LANES), lambda p, j: (p * n_tiles + j, 0))
    xr, gg, y_sgu = pl.pallas_call(
        functools.partial(_premix_kernel, add_pos=add_pos),
        out_shape=(tmj, tmj, jax.ShapeDtypeStruct((n_seq, seq_len, D_SGU), BF16)),
        grid=(n_seq // nseq, n_tiles),
        in_specs=in_specs,
        out_specs=(tmj_spec, tmj_spec, pl.BlockSpec((nseq, tt, D_SGU), lambda p, j: (p, j, 0))),
        compiler_params=_params(("parallel", "parallel")),
        name="premix",
    )(*args)
    shape4 = (n_pairs, seq_len, TMJ_ROWS, LANES)
    return xr.reshape(shape4), gg.reshape(shape4), y_sgu


def _scan_kernel(*refs, reverse, n_chunks):
    if reverse:
        (xprev_ref, x_ref, xnext_ref, gg_ref, hf_ref, h0_ref, pt_ref, wg_ref,
         y_ref, fs_ref, xwin, xc_s, r_s, i_s, a_s, b_s, y_s, hcar) = refs
    else:
        (xprev_ref, x_ref, xnext_ref, h0_ref, pt_ref, wg_ref,
         hf_ref, fs_ref, xwin, xc_s, r_s, i_s, a_s, b_s, hcar) = refs
    cw = [pt_ref[k] for k in range(CONV_W)]
    cb, bias_a, bias_x, lam = (pt_ref[CONV_W + k] for k in range(4))
    c = pl.program_id(1)
    chunk = n_chunks - 1 - c if reverse else c
    sub_rows = TS * TMJ_ROWS

    def rows_of(pb, t0, n_steps):
        first = (pb * LC + t0) * TMJ_ROWS
        if not isinstance(first, int):
            first = pl.multiple_of(first, TMJ_ROWS)
        return pl.ds(first, n_steps * TMJ_ROWS)

    @pl.when(c == 0)
    def _():
        hcar[...] = h0_ref[...]

    xwin[:, 0:CONV_LEFT] = jnp.where(chunk > 0, xprev_ref[...], 0.0)
    xwin[:, LC + CONV_LEFT:LC + CONV_W - 1] = jnp.where(chunk < n_chunks - 1, xnext_ref[...], 0.0)

    xwin[:, CONV_LEFT:CONV_LEFT + LC] = x_ref[...]

    neg_lam = -lam
    softplus = jnp.maximum(neg_lam, 0.0) + jnp.log(1.0 + jnp.exp(-jnp.abs(neg_lam)))
    half_decay = (-0.5 * RG_C * math.log2(math.e)) * softplus

    def conv(pb):
        for t0 in range(0, LC, TS):
            xc = cb + cw[0] * xwin[pb, t0:t0 + TS]
            for k in range(1, CONV_W):
                xc = xc + cw[k] * xwin[pb, t0 + k:t0 + k + TS]
            xc_s[rows_of(pb, t0, TS), :] = xc.reshape(sub_rows, LANES)

    def gate_matmuls(pb):
        for k in range(RNN_BLOCKS):
            rows = pl.ds(pb * LC * TMJ_ROWS + k, LC * PAIR, stride=RNN_BLOCKS)
            g = jnp.dot(xc_s[rows, :].astype(BF16), wg_ref[k], preferred_element_type=F32)
            r_s[rows, :] = g[:, :LANES]
            i_s[rows, :] = g[:, LANES:]

    def gates(pb):
        for t0 in range(0, LC, TS):
            rows = rows_of(pb, t0, TS)
            tile = lambda ref: ref[rows, :].reshape(TS, TMJ_ROWS, LANES)
            tr = jnp.tanh(tile(r_s) + bias_a)
            ti = jnp.tanh(tile(i_s) + bias_x)
            log2_a = tr * half_decay + half_decay
            a = jnp.exp2(log2_a)
            q = jnp.tanh(log2_a * (-math.log(2.0))) * (a * a + 1.0)
            b = jnp.where(q > 0.0, q * lax.rsqrt(q), 0.0) * ((ti + 1.0) * tile(xc_s))
            a_s[rows, :] = a.reshape(sub_rows, LANES)
            b_s[rows, :] = b.reshape(sub_rows, LANES)

    for stage in (conv, gate_matmuls, gates):
        for pb in range(PB):
            stage(pb)

    half_steps = LC // 2

    def put(pb, t, rows, h):
        if reverse:
            y_s[rows, :] = h
        else:
            hf_ref[pb, t] = h

    def steps(jb, carry):
        lead, lag, prod = (list(c) for c in carry)
        sign = -1 if reverse else 1
        t_base = sign * jb * SCAN_UNROLL
        row_base = pl.multiple_of(t_base * TMJ_ROWS, TMJ_ROWS)
        for u in range(SCAN_UNROLL):
            lead_0 = LC - 1 - u if reverse else u
            lag_0 = half_steps - 1 - u if reverse else half_steps + u
            for pb in range(PB):
                rows_lead = pl.ds(row_base + (pb * LC + lead_0) * TMJ_ROWS, TMJ_ROWS)
                rows_lag = pl.ds(row_base + (pb * LC + lag_0) * TMJ_ROWS, TMJ_ROWS)
                a_lag = a_s[rows_lag, :]
                lead[pb] = a_s[rows_lead, :] * lead[pb] + b_s[rows_lead, :]
                lag[pb] = a_lag * lag[pb] + b_s[rows_lag, :]
                prod[pb] = a_lag * prod[pb]
                put(pb, t_base + lead_0, rows_lead, lead[pb])
                put(pb, t_base + lag_0, rows_lag, lag[pb])
                r_s[rows_lag, :] = prod[pb]
        return tuple(lead), tuple(lag), tuple(prod)

    zeros = tuple(jnp.zeros((TMJ_ROWS, LANES), F32) for _ in range(PB))
    ones = tuple(jnp.ones((TMJ_ROWS, LANES), F32) for _ in range(PB))
    lead, lag, prod = lax.fori_loop(0, half_steps // SCAN_UNROLL, steps,
                                    (tuple(hcar[pb] for pb in range(PB)), zeros, ones))
    lag_first = 0 if reverse else half_steps

    def completion(pb, t0):
        prods = r_s[rows_of(pb, t0, TS), :].reshape(TS, TMJ_ROWS, LANES)
        return (prods * lead[pb]).reshape(sub_rows, LANES)

    for pb in range(PB):
        final = lag[pb] + prod[pb] * lead[pb]
        hcar[pb] = final
        fs_ref[pb] = final
        if not reverse:
            for t0 in range(lag_first, lag_first + half_steps, TS):
                hf_ref[pb, t0:t0 + TS] = hf_ref[pb, t0:t0 + TS] + completion(pb, t0).reshape(TS, TMJ_ROWS, LANES)

    if reverse:
        for pb in range(PB):
            for t0 in range(0, LC, TS):
                rows = rows_of(pb, t0, TS)
                both = hf_ref[pb, t0:t0 + TS].reshape(sub_rows, LANES) + y_s[rows, :]
                if lag_first <= t0 < lag_first + half_steps:
                    both = both + completion(pb, t0)
                y_s[rows, :] = both * gg_ref[pb, t0:t0 + TS].reshape(sub_rows, LANES)
            for s in range(PAIR):
                cols = [y_s[pl.ds(pb * LC * TMJ_ROWS + s * RNN_BLOCKS + k, LC, stride=TMJ_ROWS), :]
                        for k in range(RNN_BLOCKS)]
                y_ref[pb * PAIR + s] = jnp.concatenate(cols, axis=1).astype(BF16)


def _scan(xr, gg, hf, h0, param_tiles, w_gates, direction):
    n_pairs, seq_len = xr.shape[:2]
    n_chunks = seq_len // LC
    reverse = direction == 1
    pos = (lambda c: n_chunks - 1 - c) if reverse else (lambda c: c)
    tmj_blk = pl.BlockSpec((PB, LC, TMJ_ROWS, LANES), lambda i, c: (i, pos(c), 0, 0))
    state_blk = pl.BlockSpec((PB, TMJ_ROWS, LANES), lambda i, c: (i, 0, 0))
    per_dir = lambda *shape: pl.BlockSpec((None,) + shape, lambda i, c: (direction,) + (0,) * len(shape))
    in_specs = [
        pl.BlockSpec((PB, CONV_LEFT, TMJ_ROWS, LANES),
                     lambda i, c: (i, jnp.maximum(pos(c) * (LC // CONV_LEFT) - 1, 0), 0, 0)),
        tmj_blk,
        pl.BlockSpec((PB, 1, TMJ_ROWS, LANES), lambda i, c: (i, jnp.minimum((pos(c) + 1) * LC, seq_len - 1), 0, 0)),
    ]
    args = [xr, xr, xr]
    if reverse:
        in_specs += [tmj_blk, tmj_blk]
        args += [gg, hf]
    in_specs += [state_blk,
                 per_dir(CONV_W + 4, TMJ_ROWS, LANES),
                 per_dir(RNN_BLOCKS, LANES, 2 * LANES)]
    args += [h0, param_tiles, w_gates]
    flat = pltpu.VMEM((PB * LC * TMJ_ROWS, LANES), F32)
    scratch = [pltpu.VMEM((PB, LC + CONV_W - 1, TMJ_ROWS, LANES), F32)] + [flat] * (6 if reverse else 5)
    scratch += [pltpu.VMEM((PB, TMJ_ROWS, LANES), F32)]
    state = jax.ShapeDtypeStruct((n_pairs, TMJ_ROWS, LANES), F32)
    if reverse:
        out_shape = (jax.ShapeDtypeStruct((n_pairs * PAIR, seq_len, D_RNN), BF16), state)
        out_specs = (pl.BlockSpec((PB * PAIR, LC, D_RNN), lambda i, c: (i, pos(c), 0)), state_blk)
    else:
        out_shape = (jax.ShapeDtypeStruct(xr.shape, F32), state)
        out_specs = (tmj_blk, state_blk)
    return pl.pallas_call(
        functools.partial(_scan_kernel, reverse=reverse, n_chunks=n_chunks),
        out_shape=out_shape,
        grid=(n_pairs // PB, n_chunks),
        in_specs=in_specs,
        out_specs=out_specs,
        scratch_shapes=scratch,
        compiler_params=_params(("parallel", "arbitrary")),
        name="scan_bwd" if reverse else "scan_fwd",
    )(*args)


def _route(lt):
    n = lt.shape[1]
    row = lax.broadcasted_iota(jnp.int32, (EXPERTS_PER_GROUP, n), 0)
    neg = jnp.float32(-jnp.inf)

    def arg_max(v):
        m = jnp.max(v, axis=0, keepdims=True)
        return jnp.min(jnp.where(v == m, row, EXPERTS_PER_GROUP), axis=0, keepdims=True)

    g_idx = arg_max(lt[0:N_GROUPS])
    el = lt[E_ROW0:E_ROW0 + EXPERTS_PER_GROUP]
    for g in range(1, N_GROUPS):
        first = E_ROW0 + g * EXPERTS_PER_GROUP
        el = jnp.where(g_idx == g, lt[first:first + EXPERTS_PER_GROUP], el)
    i1 = arg_max(el)
    i2 = arg_max(jnp.where(row == i1, neg, el))
    ja = jnp.minimum(i1, i2)
    jb = jnp.maximum(i1, i2)
    pair = (ja * (2 * EXPERTS_PER_GROUP - 1 - ja)) // 2 + (jb - ja - 1)
    return g_idx * PAIRS_PER_GROUP + pair


def _store_token_major(ref, x, t0=0):
    n = x.shape[0]
    for k in range(ROW_TILES):
        ref[pl.ds(t0 * ROW_TILES + k, n, stride=ROW_TILES), :] = x[:, k * LANES:(k + 1) * LANES]


def _load_token_major(ref, n):
    return jnp.concatenate([ref[pl.ds(k, n, stride=ROW_TILES), :] for k in range(ROW_TILES)], axis=1)


def _postmix_kernel(*refs, add_pos):
    refs = list(refs)
    x_ref = refs.pop(0)
    pos_refs = (refs.pop(0), refs.pop(0)) if add_pos else None
    (yr_ref, ys_ref, mod_ref, gpost_ref, gpre_ref, wout_ref, rw_ref, rb_ref, earlier_ref, cnt0_ref,
     x1_ref, hn_ref, bucket_ref, rank_ref, cnt_ref, run_ref) = refs

    @pl.when(pl.program_id(0) == 0)
    def _():
        run_ref[...] = cnt0_ref[...]

    gate1 = mod_ref[0, :, 2 * D_MODEL:3 * D_MODEL]
    shift2 = mod_ref[0, :, 3 * D_MODEL:4 * D_MODEL]
    scale2 = mod_ref[0, :, 4 * D_MODEL:5 * D_MODEL]
    y = (jnp.dot(yr_ref[...], wout_ref[0:D_RNN, :], preferred_element_type=F32)
         + jnp.dot(ys_ref[...], wout_ref[D_RNN:, :], preferred_element_type=F32))
    x1 = _load_x(x_ref, pos_refs, 0, TP) + _rms(y) * (gate1 * gpost_ref[...])
    x1_ref[...] = x1
    hn = _rms(x1) * (gpre_ref[...] * (1.0 + scale2)) + shift2
    _store_token_major(hn_ref, hn)
    lt = lax.dot_general(rw_ref[...], hn.astype(BF16), (((1,), (1,)), ((), ())),
                         preferred_element_type=F32) + rb_ref[:, 0:1]
    bucket = _route(lt)
    onehot = lax.broadcasted_iota(jnp.int32, (ROUTER_ROWS, TP), 0) == bucket
    before = jnp.dot(onehot.astype(BF16), earlier_ref[...], preferred_element_type=F32) + run_ref[:, 0:1]
    rank = jnp.sum(jnp.where(onehot, before, 0.0), axis=0, keepdims=True).astype(jnp.int32)
    bucket_ref[...] = bucket
    rank_ref[...] = rank
    run_ref[...] += jnp.sum(onehot.astype(F32), axis=1, keepdims=True)
    cnt_ref[...] = run_ref[...]


def _postmix(x, y_rnn, y_sgu, mod3, cond_of_tile, g_post, g_pre, w_out_b, router_wt, router_bt, earlier, pos_tab,
             counts0):
    n_tok = x.shape[0]
    n_tiles = n_tok // TP
    add_pos = pos_tab is not None
    tok = lambda i: (i, 0)
    const2 = lambda i: (0, 0)
    in_specs = [pl.BlockSpec((TP, D_MODEL), tok)]
    args = [x]
    if add_pos:
        reps = TP // GRID_W
        tiles_per_seq = GRID_W // reps
        in_specs += [pl.BlockSpec((None, reps, D_MODEL // 2), lambda i: (i % tiles_per_seq, 0, 0)),
                     pl.BlockSpec((GRID_W, D_MODEL // 2), const2)]
        args += [pos_tab.reshape(tiles_per_seq, reps, D_MODEL // 2), pos_tab]
    in_specs += [pl.BlockSpec((TP, D_RNN), tok),
                 pl.BlockSpec((TP, D_SGU), tok),
                 pl.BlockSpec((1, 1, 6 * D_MODEL), lambda i: (cond_of_tile(i), 0, 0)),
                 pl.BlockSpec((1, D_MODEL), const2),
                 pl.BlockSpec((1, D_MODEL), const2),
                 pl.BlockSpec((D_MODEL, D_MODEL), const2),
                 pl.BlockSpec((ROUTER_ROWS, D_MODEL), const2),
                 pl.BlockSpec((ROUTER_ROWS, LANES), const2),
                 pl.BlockSpec((TP, TP), const2),
                 pl.BlockSpec((ROUTER_ROWS, LANES), const2)]
    args += [y_rnn, y_sgu, mod3, g_post, g_pre, w_out_b, router_wt, router_bt, earlier, counts0]
    counts_spec = pl.BlockSpec((ROUTER_ROWS, LANES), const2)
    route_shape = jax.ShapeDtypeStruct((n_tiles, 1, TP), jnp.int32)
    route_spec = pl.BlockSpec((None, 1, TP), lambda i: (i, 0, 0))
    x1, hn, bucket, rank, counts = pl.pallas_call(
        functools.partial(_postmix_kernel, add_pos=add_pos),
        out_shape=(jax.ShapeDtypeStruct((n_tok, D_MODEL), F32),
                   jax.ShapeDtypeStruct((n_tok * ROW_TILES, LANES), F32),
                   route_shape, route_shape,
                   jax.ShapeDtypeStruct((ROUTER_ROWS, LANES), F32)),
        grid=(n_tiles,),
        in_specs=in_specs,
        out_specs=(pl.BlockSpec((TP, D_MODEL), tok),
                   pl.BlockSpec((TP * ROW_TILES, LANES), tok),
                   route_spec, route_spec,
                   counts_spec),
        scratch_shapes=[pltpu.VMEM((ROUTER_ROWS, LANES), F32)],
        compiler_params=_params(("arbitrary",)),
        name="postmix",
    )(*args)
    return x1, hn, bucket.reshape(n_tok), rank.reshape(n_tok), counts


def _token_rows(ref, t):
    return ref.at[pl.ds(pl.multiple_of(t * ROW_TILES, ROW_TILES), ROW_TILES), :]


def _dispatch_kernel(dest_ref, hc_ref, hs_ref, xs_ref, sem, *, n_ctx_steps):
    i = pl.program_id(0)
    base = i * TD

    def scatter(src_ref):
        def start(g, carry):
            for u in range(DMA_UNROLL):
                r = g * DMA_UNROLL + u
                pltpu.make_async_copy(_token_rows(src_ref, r), _token_rows(xs_ref, dest_ref[base + r]),
                                      sem).start(priority=u % 2)
            return carry

        lax.fori_loop(0, TD // DMA_UNROLL, start, 0)
        pltpu.make_async_copy(src_ref, xs_ref.at[pl.ds(0, TD * ROW_TILES), :], sem).wait()

    @pl.when(i < n_ctx_steps)
    def _():
        scatter(hc_ref)

    @pl.when(i >= n_ctx_steps)
    def _():
        scatter(hs_ref)


def _dispatch(dest, hn_ctx, hn_dec, n_slots):
    n_ctx_steps = hn_ctx.shape[0] // (TD * ROW_TILES)
    n_dec_steps = hn_dec.shape[0] // (TD * ROW_TILES)
    return pl.pallas_call(
        functools.partial(_dispatch_kernel, n_ctx_steps=n_ctx_steps),
        out_shape=jax.ShapeDtypeStruct((n_slots * ROW_TILES, LANES), F32),
        grid_spec=pltpu.PrefetchScalarGridSpec(
            num_scalar_prefetch=1,
            grid=(n_ctx_steps + n_dec_steps,),
            in_specs=[pl.BlockSpec((TD * ROW_TILES, LANES), lambda i, d: (jnp.minimum(i, n_ctx_steps - 1), 0)),
                      pl.BlockSpec((TD * ROW_TILES, LANES), lambda i, d: (jnp.maximum(i - n_ctx_steps, 0), 0))],
            out_specs=pl.BlockSpec(memory_space=pl.ANY),
            scratch_shapes=[pltpu.SemaphoreType.DMA(())]),
        compiler_params=_params(("arbitrary",)),
        name="dispatch",
    )(dest, hn_ctx, hn_dec)


def _experts_kernel(ea_ref, eb_ref, nv_ref, blk_ref, xs_ref, rw_ref, rb_ref, *refs):
    w32_refs, ys_ref, w_refs = refs[:6], refs[6], refs[7:]
    wga_ref, wua_ref, wda_ref, wgb_ref, wub_ref, wdb_ref = w_refs
    i = pl.program_id(0)
    nv = nv_ref[i]
    prev = jnp.maximum(i - 1, 0)

    for e_ref, first in ((ea_ref, 0), (eb_ref, 3)):
        @pl.when((i == 0) | (e_ref[i] != e_ref[prev]))
        def _(first=first):
            for w32_ref, w_ref in zip(w32_refs[first:first + 3], w_refs[first:first + 3]):
                w_ref[...] = w32_ref[0].astype(BF16)

    @pl.when(nv > 0)
    def _():
        row = lax.broadcasted_iota(jnp.int32, (TMX, 1), 0)
        xb = jnp.where(row < nv, _load_token_major(xs_ref, TMX), 0.0).astype(BF16)
        logits = jnp.dot(xb, rw_ref[...], preferred_element_type=F32) + rb_ref[...]
        lane = lax.broadcasted_iota(jnp.int32, logits.shape, 1)
        ea = ea_ref[i]
        eb = eb_ref[i]
        gmask = lane < N_GROUPS
        gl = jnp.where(gmask, logits, -jnp.inf)
        gmax = jnp.max(gl, axis=-1, keepdims=True)
        gexp = jnp.where(gmask, jnp.exp(gl - gmax), 0.0)
        g_own = jnp.sum(jnp.where(lane == ea // EXPERTS_PER_GROUP, gexp, 0.0), axis=-1, keepdims=True)
        g_w = g_own / jnp.sum(gexp, axis=-1, keepdims=True)
        la = jnp.sum(jnp.where(lane == ea + E_LANE0, logits, 0.0), axis=-1, keepdims=True)
        lb = jnp.sum(jnp.where(lane == eb + E_LANE0, logits, 0.0), axis=-1, keepdims=True)
        m = jnp.maximum(la, lb)
        pa = jnp.exp(la - m)
        pb = jnp.exp(lb - m)
        inv = g_w / (pa + pb)

        def hidden(x, wg_ref, wu_ref, w):
            g = jnp.dot(x, wg_ref[...], preferred_element_type=F32)
            u = jnp.dot(x, wu_ref[...], preferred_element_type=F32)
            return ((g * _sigmoid(g)) * u * w).astype(BF16)

        part = TMX // EXPERT_ROW_PARTS
        ys = []
        for h in range(EXPERT_ROW_PARTS):
            rows = slice(h * part, (h + 1) * part)
            act_a = hidden(xb[rows], wga_ref, wua_ref, (pa * inv)[rows])
            act_b = hidden(xb[rows], wgb_ref, wub_ref, (pb * inv)[rows])
            y = (jnp.dot(act_a, wda_ref[...], preferred_element_type=F32)
                 + jnp.dot(act_b, wdb_ref[...], preferred_element_type=F32))
            ys.append(y)
        _store_token_major(ys_ref, jnp.concatenate(ys, axis=0))


def _experts(sched, xs, router_w, router_b, wg, wu, wd):
    ea, eb, nv, blk = sched
    n_tiles = ea.shape[0]
    rows = lambda i, ea, eb, nv, blk: (blk[i], 0)
    const2 = lambda i, ea, eb, nv, blk: (0, 0)
    exp_a = lambda i, ea, eb, nv, blk: (ea[i], 0, 0)
    exp_b = lambda i, ea, eb, nv, blk: (eb[i], 0, 0)
    w_in_spec = lambda m: pl.BlockSpec((1, D_MODEL, D_EXPERT), m)
    w_out_spec = lambda m: pl.BlockSpec((1, D_EXPERT, D_MODEL), m)
    return pl.pallas_call(
        _experts_kernel,
        out_shape=jax.ShapeDtypeStruct(xs.shape, F32),
        grid_spec=pltpu.PrefetchScalarGridSpec(
            num_scalar_prefetch=4,
            grid=(n_tiles,),
            in_specs=[pl.BlockSpec((TMX * ROW_TILES, LANES), rows),
                      pl.BlockSpec((D_MODEL, ROUTER_LANES), const2),
                      pl.BlockSpec((1, ROUTER_LANES), const2),
                      w_in_spec(exp_a), w_in_spec(exp_a), w_out_spec(exp_a),
                      w_in_spec(exp_b), w_in_spec(exp_b), w_out_spec(exp_b)],
            out_specs=pl.BlockSpec((TMX * ROW_TILES, LANES), rows),
            scratch_shapes=[pltpu.VMEM((D_MODEL, D_EXPERT), BF16), pltpu.VMEM((D_MODEL, D_EXPERT), BF16),
                            pltpu.VMEM((D_EXPERT, D_MODEL), BF16)] * 2),
        compiler_params=_params(("arbitrary",)),
        name="experts",
    )(ea, eb, nv, blk, xs, router_w, router_b, wg, wu, wd, wg, wu, wd)


def _combine_kernel(dest_ref, ys_ref, x1_ref, mod_ref, gpost_ref, o_ref, ybuf, sems, *, tile0):
    i = pl.program_id(0)
    n = pl.num_programs(0)

    def fetch(tile, slot):
        base = (tile + tile0) * TC

        def start(g, carry):
            for u in range(DMA_UNROLL):
                r = g * DMA_UNROLL + u
                pltpu.make_async_copy(_token_rows(ys_ref, dest_ref[base + r]), _token_rows(ybuf.at[slot], r),
                                      sems.at[slot]).start(priority=u % 2)
            return carry

        lax.fori_loop(0, TC // DMA_UNROLL, start, 0)

    @pl.when(i == 0)
    def _():
        fetch(0, 0)

    @pl.when(i + 1 < n)
    def _():
        fetch(i + 1, (i + 1) % 2)

    slot = i % 2
    pltpu.make_async_copy(ys_ref.at[pl.ds(0, TC * ROW_TILES), :], ybuf.at[slot], sems.at[slot]).wait()
    gate2 = mod_ref[0, :, 5 * D_MODEL:6 * D_MODEL]
    o_ref[...] = x1_ref[...] + _rms(_load_token_major(ybuf.at[slot], TC)) * (gate2 * gpost_ref[...])


def _combine(dest, ys, x1, mod3, g_post, cond_of_tile, tile0):
    n_tok = x1.shape[0]
    return pl.pallas_call(
        functools.partial(_combine_kernel, tile0=tile0),
        out_shape=jax.ShapeDtypeStruct((n_tok, D_MODEL), F32),
        grid_spec=pltpu.PrefetchScalarGridSpec(
            num_scalar_prefetch=1,
            grid=(n_tok // TC,),
            in_specs=[pl.BlockSpec(memory_space=pl.ANY),
                      pl.BlockSpec((TC, D_MODEL), lambda i, d: (i, 0)),
                      pl.BlockSpec((1, 1, 6 * D_MODEL), lambda i, d: (cond_of_tile(i), 0, 0)),
                      pl.BlockSpec((1, D_MODEL), lambda i, d: (0, 0))],
            out_specs=pl.BlockSpec((TC, D_MODEL), lambda i, d: (i, 0)),
            scratch_shapes=[pltpu.VMEM((2, TC * ROW_TILES, LANES), F32), pltpu.SemaphoreType.DMA((2,))]),
        compiler_params=_params(("arbitrary",)),
        name="combine",
    )(dest, ys, x1, mod3, g_post)


def _schedule(bucket, rank, counts):
    n_tok = bucket.shape[0]
    n_max = n_tok // TMX + N_BUCKETS
    cnt = counts[:N_BUCKETS, 0].astype(jnp.int32)
    tiles = (cnt + TMX - 1) // TMX
    tile_end = jnp.cumsum(tiles)
    tile_start = tile_end - tiles
    ids = jnp.arange(N_BUCKETS, dtype=jnp.int32)
    slot0 = jnp.sum(jnp.where(bucket[:, None] == ids[None, :], (tile_start * TMX)[None, :], 0), axis=1)
    dest = slot0 + rank
    i = jnp.arange(n_max, dtype=jnp.int32)
    total = tile_end[-1]
    valid = i < total
    tb = jnp.sum((jnp.minimum(i, total - 1)[:, None] >= tile_end[None, :]).astype(jnp.int32), axis=1)
    pairs = [(a, b) for a in range(EXPERTS_PER_GROUP) for b in range(a + 1, EXPERTS_PER_GROUP)]
    ea_tab = jnp.array([g * EXPERTS_PER_GROUP + a for g in range(N_GROUPS) for a, _ in pairs], jnp.int32)
    eb_tab = jnp.array([g * EXPERTS_PER_GROUP + b for g in range(N_GROUPS) for _, b in pairs], jnp.int32)
    hit = tb[:, None] == ids[None, :]
    look = lambda tab: jnp.sum(jnp.where(hit, tab[None, :], 0), axis=1)
    ea, eb = look(ea_tab), look(eb_tab)
    nv = jnp.where(valid, jnp.clip(look(cnt) - (i - look(tile_start)) * TMX, 0, TMX), 0)
    blk = jnp.minimum(i, total - 1)
    return dest, (ea, eb, nv, blk), n_max * TMX


def _block_diag_gates(rg_wa, rg_wx):
    heads = LANES // HEAD_RNN

    def bd(w):
        w = w.reshape(2, RNN_BLOCKS, heads, HEAD_RNN, HEAD_RNN)
        eye = jnp.eye(heads, dtype=w.dtype)
        full = jnp.einsum('dghij,hk->dghikj', w, eye)
        return full.reshape(2, RNN_BLOCKS, LANES, LANES)

    return jnp.concatenate([bd(rg_wa), bd(rg_wx)], axis=-1).astype(BF16)


def _row_tile(v):
    blocks = v.reshape(v.shape[:-1] + (RNN_BLOCKS, LANES))
    return jnp.concatenate([blocks] * PAIR, axis=-2)


def _to_time_major_state(h):
    return h.reshape(h.shape[0] // PAIR, TMJ_ROWS, LANES)


def kernel(x_prompt, x_sample, state_rglru, c, c_ctx, w_mod, b_mod, g_pre_mix, g_post_mix, g_pre_ffn,
           g_post_ffn, w_in, conv_w, conv_b, rg_wa, rg_ba, rg_wx, rg_bx, rg_lambda, sgu_g, sgu_w, sgu_b,
           w_out, router_g_w, router_g_b, router_e_w, router_e_b, exp_w_gate, exp_w_up, exp_w_down):
    assert w_mod.shape[0] == 1, "single-layer trunk"
    n_ctx, ctx_len, _ = x_prompt.shape
    n_dec, dec_len, _ = x_sample.shape
    l = 0

    n_cond = SUBLANES
    ctx_rows = n_cond - n_dec
    ctx_per_step = max(PAIR, PREMIX_ROWS // min(ctx_len, TT))
    assert n_dec % PAIR == 0 and ctx_rows == ctx_per_step and n_dec % ctx_per_step == 0
    cond = jnp.concatenate([c, jnp.broadcast_to(c_ctx, (ctx_rows, D_MODEL))], axis=0)
    mod3 = _modulation(cond, w_mod[l], b_mod[l])
    pos_tab = _pos_table()

    w_rx, w_gate, w_u, w_v = jnp.split(w_in[l], [D_RNN, 2 * D_RNN, 2 * D_RNN + D_SGU], axis=1)
    w_in_b = jnp.concatenate([w_v, w_u, w_gate, w_rx], axis=1).astype(BF16)
    w_out_b = w_out[l].astype(BF16)
    sgu_w_b = sgu_w[l].reshape(2, 4 * CHUNK, CHUNK).astype(BF16)
    sgu_bias_tile = jnp.repeat(sgu_b[l].T, HEAD_SGU, axis=1)
    w_gates = _block_diag_gates(rg_wa[l], rg_wx[l])
    conv_rows = 0.5 * jnp.concatenate([conv_w[l], conv_b[l][None]], axis=0)
    per_dir_rows = jnp.concatenate([jnp.broadcast_to(conv_rows, (2,) + conv_rows.shape), 0.5 * rg_ba[l][:, None],
                                    0.5 * rg_bx[l][:, None], rg_lambda[l][:, None]], axis=1)
    scan_tiles = _row_tile(per_dir_rows)
    lane_pad = ROUTER_LANES - E_LANE0 - N_EXPERTS
    router_w = jnp.pad(jnp.concatenate([router_g_w[l], router_e_w[l]], axis=1), ((0, 0), (0, lane_pad))).astype(BF16)
    router_b = jnp.pad(jnp.concatenate([router_g_b[l], router_e_b[l]]), (0, lane_pad)).reshape(1, ROUTER_LANES)
    gap, tail = E_ROW0 - N_GROUPS, ROUTER_ROWS - E_ROW0 - N_EXPERTS
    router_wt = jnp.concatenate([router_g_w[l].T, jnp.zeros((gap, D_MODEL), F32), router_e_w[l].T,
                                 jnp.zeros((tail, D_MODEL), F32)], axis=0).astype(BF16)
    router_bt = jnp.concatenate([router_g_b[l], jnp.zeros((gap,), F32), router_e_b[l], jnp.zeros((tail,), F32)])
    router_bt = jnp.broadcast_to(router_bt[:, None], (ROUTER_ROWS, LANES))
    earlier = jnp.asarray(np.triu(np.ones((TP, TP), BF16), k=1))
    row = lambda v: v.reshape(1, -1)

    n_ctx_tok = n_ctx * ctx_len

    def mixer(x, h0, cond_of_tile, cond_block, use_pos, counts0):
        n_seq, seq_len, _ = x.shape
        xf = x.reshape(n_seq * seq_len, D_MODEL)
        tab = pos_tab if use_pos else None
        xr, gg, y_sgu = _premix(x, mod3, cond_block, row(g_pre_mix[l]), w_in_b, row(sgu_g[l]),
                                sgu_w_b, sgu_bias_tile, tab)
        scan_params = (scan_tiles, w_gates)
        hf, hf_last = _scan(xr, None, None, _to_time_major_state(h0[:, 0]), *scan_params, direction=0)
        y_rnn, hb_first = _scan(xr, gg, hf, _to_time_major_state(h0[:, 1]), *scan_params, direction=1)
        fstate = jnp.stack([hf_last.reshape(n_seq, D_RNN), hb_first.reshape(n_seq, D_RNN)], axis=1)
        x1, hn, bucket, rank, counts = _postmix(
            xf, y_rnn.reshape(n_seq * seq_len, D_RNN), y_sgu.reshape(n_seq * seq_len, D_SGU), mod3,
            cond_of_tile, row(g_post_mix[l]), row(g_pre_ffn[l]), w_out_b, router_wt, router_bt, earlier, tab,
            counts0)
        return x1, hn, bucket, rank, counts, fstate

    ctx_cond = lambda i: n_dec
    dec_cond = lambda i: i // (dec_len // TP)
    h0_ctx = jnp.zeros((n_ctx, 2, D_RNN), F32)
    counts0 = jnp.zeros((ROUTER_ROWS, LANES), F32)
    x1_ctx, hn_ctx, bucket_ctx, rank_ctx, counts, st = mixer(x_prompt, h0_ctx, ctx_cond, lambda p: n_dec // ctx_per_step,
                                                             False, counts0)
    new_state = st.astype(state_rglru.dtype)[:, None]
    x1_dec, hn_dec, bucket_dec, rank_dec, counts, _ = mixer(x_sample, state_rglru[:, l].astype(F32), dec_cond,
                                                            lambda p: p, True, counts)

    dest, sched, n_slots = _schedule(jnp.concatenate([bucket_ctx, bucket_dec]),
                                     jnp.concatenate([rank_ctx, rank_dec]), counts)
    xs = _dispatch(dest, hn_ctx, hn_dec, n_slots)
    ys = _experts(sched, xs, router_w, router_b, exp_w_gate[l], exp_w_up[l], exp_w_down[l])
    y_prompt = _combine(dest, ys, x1_ctx, mod3, row(g_post_ffn[l]), ctx_cond, 0)
    y_sample = _combine(dest, ys, x1_dec, mod3, row(g_post_ffn[l]), lambda i: i // (dec_len // TC), n_ctx_tok // TC)
    return (y_prompt.reshape(x_prompt.shape), y_sample.reshape(x_sample.shape), new_state)
```

```python
import functools
import math

import jax
import jax.numpy as jnp
import numpy as np
from jax import lax
from jax.experimental import pallas as pl
from jax.experimental.pallas import tpu as pltpu

D_MODEL = 1024
D_RNN = 512
D_SGU = 512
N_HEADS_RNN = 8
HEAD_RNN = D_RNN // N_HEADS_RNN
N_HEADS_SGU = 8
HEAD_SGU = D_SGU // N_HEADS_SGU
CHUNK = 128
GRID_W = 64
RG_C = 8.0
N_GROUPS = 4
EXPERTS_PER_GROUP = 4
N_EXPERTS = N_GROUPS * EXPERTS_PER_GROUP
D_EXPERT = 512
EPS = 1e-6
POS_BASE = 10000.0

LANES = 128
SUBLANES = 8
CONV_W = 4
CONV_LEFT = 2
PAIR = 2
RNN_BLOCKS = D_RNN // LANES
TMJ_ROWS = PAIR * RNN_BLOCKS
ROUTER_LANES = LANES
E_LANE0 = N_GROUPS
ROUTER_ROWS = 32
E_ROW0 = SUBLANES

PAIRS_PER_GROUP = EXPERTS_PER_GROUP * (EXPERTS_PER_GROUP - 1) // 2
N_BUCKETS = N_GROUPS * PAIRS_PER_GROUP

ROW_TILES = D_MODEL // LANES

MOD_COLS = 2048
TP = 1024
X_SLOTS = 3
TD = 2048
TC = 512
DMA_UNROLL = 8
EXPERT_ROW_PARTS = 2
TMX = 512
TT = 512
PREMIX_ROWS = 1024
LC = 256
TS = 16
PB = 2
SCAN_UNROLL = 8
VMEM_LIMIT = 56 * 1024 * 1024

F32 = jnp.float32
BF16 = jnp.bfloat16


def _params(sem):
    return pltpu.CompilerParams(dimension_semantics=sem, vmem_limit_bytes=VMEM_LIMIT)


def _rms(x):
    return x * lax.rsqrt(jnp.mean(x * x, axis=-1, keepdims=True) + EPS)


def _sigmoid(x):
    return 0.5 * jnp.tanh(0.5 * x) + 0.5


def _mod_kernel(cond_ref, w_ref, b_ref, o_ref):
    c = cond_ref[...]
    s = c * _sigmoid(c)
    out = jnp.dot(s.astype(BF16), w_ref[...].astype(BF16), preferred_element_type=F32) + b_ref[...]
    for r in range(o_ref.shape[0]):
        o_ref[r] = out[r:r + 1, :]


def _modulation(cond, w_mod, b_mod):
    n = w_mod.shape[1]
    return pl.pallas_call(
        _mod_kernel,
        out_shape=jax.ShapeDtypeStruct((cond.shape[0], 1, n), F32),
        grid=(n // MOD_COLS,),
        in_specs=[pl.BlockSpec(cond.shape, lambda j: (0, 0)),
                  pl.BlockSpec((D_MODEL, MOD_COLS), lambda j: (0, j)),
                  pl.BlockSpec((1, MOD_COLS), lambda j: (0, j))],
        out_specs=pl.BlockSpec((cond.shape[0], 1, MOD_COLS), lambda j: (0, 0, j)),
        compiler_params=_params(("arbitrary",)),
        name="modulation",
    )(cond, w_mod, b_mod.reshape(1, n))


def _pos_kernel(o_ref):
    n_freq = D_MODEL // 4
    k = lax.broadcasted_iota(jnp.int32, (GRID_W, n_freq), 1).astype(F32)
    p = lax.broadcasted_iota(jnp.int32, (GRID_W, n_freq), 0).astype(F32)
    freq = jnp.exp(-math.log(POS_BASE) * k / n_freq)
    ang = p * freq
    o_ref[:, 0:n_freq] = jnp.sin(ang)
    o_ref[:, n_freq:2 * n_freq] = jnp.cos(ang)


def _pos_table():
    return pl.pallas_call(
        _pos_kernel,
        out_shape=jax.ShapeDtypeStruct((GRID_W, D_MODEL // 2), F32),
        name="pos_table",
    )()


def _add_pos(x, pos_refs, q0):
    if pos_refs is None:
        return x
    rows_ref, cols_ref = pos_refs
    reps = x.shape[0] // GRID_W
    rpart = jnp.concatenate(
        [jnp.broadcast_to(rows_ref[q:q + 1, :], (GRID_W, D_MODEL // 2)) for q in range(q0, q0 + reps)], axis=0)
    cpart = jnp.concatenate([cols_ref[...]] * reps, axis=0)
    return jnp.concatenate([x[:, :D_MODEL // 2] + rpart, x[:, D_MODEL // 2:] + cpart], axis=1)


def _load_x(x_ref, pos_refs, r0, n):
    return _add_pos(x_ref[r0:r0 + n, :], pos_refs, r0 // GRID_W)


def _premix_kernel(*refs, add_pos):
    refs = list(refs)
    x_ref = refs.pop(0)
    nseq, tt = x_ref.shape[:2]
    pos_refs = (refs.pop(0), refs.pop(0)) if add_pos else None
    mod_ref, g_ref, win_ref, sgug_ref, sguw_ref, sgub_ref, xr_ref, gg_ref, ys_ref = refs
    hn = []
    for s in range(nseq):
        shift = mod_ref[s, :, 0:D_MODEL]
        scale = mod_ref[s, :, D_MODEL:2 * D_MODEL]
        hn.append(_rms(_add_pos(x_ref[s], pos_refs, 0)) * (g_ref[...] * (1.0 + scale)) + shift)
    z = jnp.dot(jnp.concatenate(hn, axis=0).astype(BF16), win_ref[...],
                preferred_element_type=F32)
    half = D_SGU // 2
    heads_per_half = N_HEADS_SGU // 2
    lane_head = lax.broadcasted_iota(jnp.int32, (CHUNK, half), 1) // HEAD_SGU
    for s in range(nseq):
        zs = z[s * tt:(s + 1) * tt]
        vn = (_rms(zs[:, 0:D_SGU]) * sgug_ref[...]).astype(BF16)
        u = zs[:, D_SGU:2 * D_SGU]
        for c in range(tt // CHUNK):
            rows = slice(c * CHUNK, (c + 1) * CHUNK)
            halves = []
            for hf in range(2):
                r = jnp.dot(sguw_ref[hf], vn[rows, hf * half:(hf + 1) * half],
                            preferred_element_type=F32)
                sel = jnp.zeros((CHUNK, half), F32)
                for h in range(heads_per_half):
                    sel = jnp.where(lane_head == h, r[h * CHUNK:(h + 1) * CHUNK], sel)
                halves.append(sel)
            gatev = jnp.concatenate(halves, axis=1) + sgub_ref[...]
            ys_ref[s, rows, :] = (u[rows] * gatev).astype(BF16)
    for s in range(nseq):
        zs = z[s * tt:(s + 1) * tt]
        gg = jax.nn.gelu(zs[:, 2 * D_SGU:2 * D_SGU + D_RNN])
        xr = zs[:, 2 * D_SGU + D_RNN:]
        for k in range(RNN_BLOCKS):
            first = (s // PAIR) * tt * TMJ_ROWS + (s % PAIR) * RNN_BLOCKS + k
            rows = pl.ds(first, tt, stride=TMJ_ROWS)
            xr_ref[rows, :] = xr[:, k * LANES:(k + 1) * LANES]
            gg_ref[rows, :] = gg[:, k * LANES:(k + 1) * LANES]


def _premix(x, mod3, cond_block, g_pre, w_in_b, sgu_g, sgu_w_b, sgu_bias_tile, pos_tab):
    n_seq, seq_len, _ = x.shape
    tt = min(seq_len, TT)
    nseq = max(PAIR, PREMIX_ROWS // tt)
    assert nseq == PAIR or tt == seq_len
    n_pairs, n_tiles = n_seq // PAIR, seq_len // tt
    add_pos = pos_tab is not None
    const2 = lambda p, j: (0, 0)
    in_specs = [pl.BlockSpec((nseq, tt, D_MODEL), lambda p, j: (p, j, 0))]
    args = [x]
    if add_pos:
        reps = tt // GRID_W
        in_specs += [pl.BlockSpec((None, reps, D_MODEL // 2), lambda p, j: (j, 0, 0)),
                     pl.BlockSpec((GRID_W, D_MODEL // 2), const2)]
        args += [pos_tab.reshape(GRID_W // reps, reps, D_MODEL // 2), pos_tab]
    in_specs += [pl.BlockSpec((nseq, 1, 6 * D_MODEL), lambda p, j: (cond_block(p), 0, 0)),
                 pl.BlockSpec((1, D_MODEL), const2),
                 pl.BlockSpec((D_MODEL, 2 * D_RNN + 2 * D_SGU), const2),
                 pl.BlockSpec((1, D_SGU), const2),
                 pl.BlockSpec((2, 4 * CHUNK, CHUNK), lambda p, j: (0, 0, 0)),
                 pl.BlockSpec((CHUNK, D_SGU), const2)]
    args += [mod3, g_pre, w_in_b, sgu_g, sgu_w_b, sgu_bias_tile]
    tmj = jax.ShapeDtypeStruct((n_pairs * seq_len * TMJ_ROWS, LANES), F32)
    tmj_spec = pl.BlockSpec((nseq // PAIR * tt * TMJ_ROWS, LANES), lambda p, j: (p * n_tiles + j, 0))
    xr, gg, y_sgu = pl.pallas_call(
        functools.partial(_premix_kernel, add_pos=add_pos),
        out_shape=(tmj, tmj, jax.ShapeDtypeStruct((n_seq, seq_len, D_SGU), BF16)),
        grid=(n_seq // nseq, n_tiles),
        in_specs=in_specs,
        out_specs=(tmj_spec, tmj_spec, pl.BlockSpec((nseq, tt, D_SGU), lambda p, j: (p, j, 0))),
        compiler_params=_params(("parallel", "parallel")),
        name="premix",
    )(*args)
    shape4 = (n_pairs, seq_len, TMJ_ROWS, LANES)
    return xr.reshape(shape4), gg.reshape(shape4), y_sgu


def _scan_kernel(*refs, reverse, n_chunks):
    if reverse:
        (xprev_ref, x_ref, xnext_ref, gg_ref, hf_ref, h0_ref, pt_ref, wg_ref,
         y_ref, fs_ref, xwin, xc_s, r_s, i_s, a_s, b_s, y_s, hcar) = refs
    else:
        (xprev_ref, x_ref, xnext_ref, h0_ref, pt_ref, wg_ref,
         hf_ref, fs_ref, xwin, xc_s, r_s, i_s, a_s, b_s, hcar) = refs
    cw = [pt_ref[k] for k in range(CONV_W)]
    cb, bias_a, bias_x, lam = (pt_ref[CONV_W + k] for k in range(4))
    c = pl.program_id(1)
    chunk = n_chunks - 1 - c if reverse else c
    sub_rows = TS * TMJ_ROWS

    def rows_of(pb, t0, n_steps):
        first = (pb * LC + t0) * TMJ_ROWS
        if not isinstance(first, int):
            first = pl.multiple_of(first, TMJ_ROWS)
        return pl.ds(first, n_steps * TMJ_ROWS)

    @pl.when(c == 0)
    def _():
        hcar[...] = h0_ref[...]

    xwin[:, 0:CONV_LEFT] = jnp.where(chunk > 0, xprev_ref[...], 0.0)
    xwin[:, LC + CONV_LEFT:LC + CONV_W - 1] = jnp.where(chunk < n_chunks - 1, xnext_ref[...], 0.0)

    xwin[:, CONV_LEFT:CONV_LEFT + LC] = x_ref[...]

    neg_lam = -lam
    softplus = jnp.maximum(neg_lam, 0.0) + jnp.log(1.0 + jnp.exp(-jnp.abs(neg_lam)))
    half_decay = (-0.5 * RG_C * math.log2(math.e)) * softplus

    def conv(pb):
        for t0 in range(0, LC, TS):
            xc = cb + cw[0] * xwin[pb, t0:t0 + TS]
            for k in range(1, CONV_W):
                xc = xc + cw[k] * xwin[pb, t0 + k:t0 + k + TS]
            xc_s[rows_of(pb, t0, TS), :] = xc.reshape(sub_rows, LANES)

    def gate_matmuls(pb):
        for k in range(RNN_BLOCKS):
            rows = pl.ds(pb * LC * TMJ_ROWS + k, LC * PAIR, stride=RNN_BLOCKS)
            g = jnp.dot(xc_s[rows, :].astype(BF16), wg_ref[k], preferred_element_type=F32)
            r_s[rows, :] = g[:, :LANES]
            i_s[rows, :] = g[:, LANES:]

    def gates(pb):
        for t0 in range(0, LC, TS):
            rows = rows_of(pb, t0, TS)
            tile = lambda ref: ref[rows, :].reshape(TS, TMJ_ROWS, LANES)
            tr = jnp.tanh(tile(r_s) + bias_a)
            ti = jnp.tanh(tile(i_s) + bias_x)
            log2_a = tr * half_decay + half_decay
            a = jnp.exp2(log2_a)
            q = jnp.tanh(log2_a * (-math.log(2.0))) * (a * a + 1.0)
            b = jnp.where(q > 0.0, q * lax.rsqrt(q), 0.0) * ((ti + 1.0) * tile(xc_s))
            a_s[rows, :] = a.reshape(sub_rows, LANES)
            b_s[rows, :] = b.reshape(sub_rows, LANES)

    for stage in (conv, gate_matmuls, gates):
        for pb in range(PB):
            stage(pb)

    half_steps = LC // 2

    def put(pb, t, rows, h):
        if reverse:
            y_s[rows, :] = h
        else:
            hf_ref[pb, t] = h

    def steps(jb, carry):
        lead, lag, prod = (list(c) for c in carry)
        sign = -1 if reverse else 1
        t_base = sign * jb * SCAN_UNROLL
        row_base = pl.multiple_of(t_base * TMJ_ROWS, TMJ_ROWS)
        for u in range(SCAN_UNROLL):
            lead_0 = LC - 1 - u if reverse else u
            lag_0 = half_steps - 1 - u if reverse else half_steps + u
            for pb in range(PB):
                rows_lead = pl.ds(row_base + (pb * LC + lead_0) * TMJ_ROWS, TMJ_ROWS)
                rows_lag = pl.ds(row_base + (pb * LC + lag_0) * TMJ_ROWS, TMJ_ROWS)
                a_lag = a_s[rows_lag, :]
                lead[pb] = a_s[rows_lead, :] * lead[pb] + b_s[rows_lead, :]
                lag[pb] = a_lag * lag[pb] + b_s[rows_lag, :]
                prod[pb] = a_lag * prod[pb]
                put(pb, t_base + lead_0, rows_lead, lead[pb])
                put(pb, t_base + lag_0, rows_lag, lag[pb])
                r_s[rows_lag, :] = prod[pb]
        return tuple(lead), tuple(lag), tuple(prod)

    zeros = tuple(jnp.zeros((TMJ_ROWS, LANES), F32) for _ in range(PB))
    ones = tuple(jnp.ones((TMJ_ROWS, LANES), F32) for _ in range(PB))
    lead, lag, prod = lax.fori_loop(0, half_steps // SCAN_UNROLL, steps,
                                    (tuple(hcar[pb] for pb in range(PB)), zeros, ones))
    lag_first = 0 if reverse else half_steps

    def completion(pb, t0):
        prods = r_s[rows_of(pb, t0, TS), :].reshape(TS, TMJ_ROWS, LANES)
        return (prods * lead[pb]).reshape(sub_rows, LANES)

    for pb in range(PB):
        final = lag[pb] + prod[pb] * lead[pb]
        hcar[pb] = final
        fs_ref[pb] = final
        if not reverse:
            for t0 in range(lag_first, lag_first + half_steps, TS):
                hf_ref[pb, t0:t0 + TS] = hf_ref[pb, t0:t0 + TS] + completion(pb, t0).reshape(TS, TMJ_ROWS, LANES)

    if reverse:
        for pb in range(PB):
            for t0 in range(0, LC, TS):
                rows = rows_of(pb, t0, TS)
                both = hf_ref[pb, t0:t0 + TS].reshape(sub_rows, LANES) + y_s[rows, :]
                if lag_first <= t0 < lag_first + half_steps:
                    both = both + completion(pb, t0)
                y_s[rows, :] = both * gg_ref[pb, t0:t0 + TS].reshape(sub_rows, LANES)
            for s in range(PAIR):
                cols = [y_s[pl.ds(pb * LC * TMJ_ROWS + s * RNN_BLOCKS + k, LC, stride=TMJ_ROWS), :]
                        for k in range(RNN_BLOCKS)]
                y_ref[pb * PAIR + s] = jnp.concatenate(cols, axis=1).astype(BF16)


def _scan(xr, gg, hf, h0, param_tiles, w_gates, direction):
    n_pairs, seq_len = xr.shape[:2]
    n_chunks = seq_len // LC
    reverse = direction == 1
    pos = (lambda c: n_chunks - 1 - c) if reverse else (lambda c: c)
    tmj_blk = pl.BlockSpec((PB, LC, TMJ_ROWS, LANES), lambda i, c: (i, pos(c), 0, 0))
    state_blk = pl.BlockSpec((PB, TMJ_ROWS, LANES), lambda i, c: (i, 0, 0))
    per_dir = lambda *shape: pl.BlockSpec((None,) + shape, lambda i, c: (direction,) + (0,) * len(shape))
    in_specs = [
        pl.BlockSpec((PB, CONV_LEFT, TMJ_ROWS, LANES),
                     lambda i, c: (i, jnp.maximum(pos(c) * (LC // CONV_LEFT) - 1, 0), 0, 0)),
        tmj_blk,
        pl.BlockSpec((PB, 1, TMJ_ROWS, LANES), lambda i, c: (i, jnp.minimum((pos(c) + 1) * LC, seq_len - 1), 0, 0)),
    ]
    args = [xr, xr, xr]
    if reverse:
        in_specs += [tmj_blk, tmj_blk]
        args += [gg, hf]
    in_specs += [state_blk,
                 per_dir(CONV_W + 4, TMJ_ROWS, LANES),
                 per_dir(RNN_BLOCKS, LANES, 2 * LANES)]
    args += [h0, param_tiles, w_gates]
    flat = pltpu.VMEM((PB * LC * TMJ_ROWS, LANES), F32)
    scratch = [pltpu.VMEM((PB, LC + CONV_W - 1, TMJ_ROWS, LANES), F32)] + [flat] * (6 if reverse else 5)
    scratch += [pltpu.VMEM((PB, TMJ_ROWS, LANES), F32)]
    state = jax.ShapeDtypeStruct((n_pairs, TMJ_ROWS, LANES), F32)
    if reverse:
        out_shape = (jax.ShapeDtypeStruct((n_pairs * PAIR, seq_len, D_RNN), BF16), state)
        out_specs = (pl.BlockSpec((PB * PAIR, LC, D_RNN), lambda i, c: (i, pos(c), 0)), state_blk)
    else:
        out_shape = (jax.ShapeDtypeStruct(xr.shape, F32), state)
        out_specs = (tmj_blk, state_blk)
    return pl.pallas_call(
        functools.partial(_scan_kernel, reverse=reverse, n_chunks=n_chunks),
        out_shape=out_shape,
        grid=(n_pairs // PB, n_chunks),
        in_specs=in_specs,
        out_specs=out_specs,
        scratch_shapes=scratch,
        compiler_params=_params(("parallel", "arbitrary")),
        name="scan_bwd" if reverse else "scan_fwd",
    )(*args)


def _route(lt):
    n = lt.shape[1]
    row = lax.broadcasted_iota(jnp.int32, (EXPERTS_PER_GROUP, n), 0)
    neg = jnp.float32(-jnp.inf)

    def arg_max(v):
        m = jnp.max(v, axis=0, keepdims=True)
        return jnp.min(jnp.where(v == m, row, EXPERTS_PER_GROUP), axis=0, keepdims=True)

    g_idx = arg_max(lt[0:N_GROUPS])
    el = lt[E_ROW0:E_ROW0 + EXPERTS_PER_GROUP]
    for g in range(1, N_GROUPS):
        first = E_ROW0 + g * EXPERTS_PER_GROUP
        el = jnp.where(g_idx == g, lt[first:first + EXPERTS_PER_GROUP], el)
    i1 = arg_max(el)
    i2 = arg_max(jnp.where(row == i1, neg, el))
    ja = jnp.minimum(i1, i2)
    jb = jnp.maximum(i1, i2)
    pair = (ja * (2 * EXPERTS_PER_GROUP - 1 - ja)) // 2 + (jb - ja - 1)
    return g_idx * PAIRS_PER_GROUP + pair


def _store_token_major(ref, x, t0=0):
    n = x.shape[0]
    for k in range(ROW_TILES):
        ref[pl.ds(t0 * ROW_TILES + k, n, stride=ROW_TILES), :] = x[:, k * LANES:(k + 1) * LANES]


def _load_token_major(ref, n):
    return jnp.concatenate([ref[pl.ds(k, n, stride=ROW_TILES), :] for k in range(ROW_TILES)], axis=1)


def _postmix_kernel(*refs, add_pos, n_tiles):
    refs = list(refs)
    x_hbm = refs.pop(0)
    pos_refs = (refs.pop(0), refs.pop(0)) if add_pos else None
    (yr_ref, ys_ref, mod_ref, gpost_ref, gpre_ref, wout_ref, rw_ref, rb_ref, earlier_ref, cnt0_ref,
     x1_ref, hn_ref, rt_ref, cnt_ref, run_ref, x_ring, x_sem) = refs
    i = pl.program_id(0)

    def x_copy(tile):
        slot = tile % X_SLOTS
        first = tile * TP if isinstance(tile, int) else pl.multiple_of(tile * TP, TP)
        return pltpu.make_async_copy(x_hbm.at[pl.ds(first, TP), :], x_ring.at[slot], x_sem.at[slot])

    @pl.when(i == 0)
    def _():
        run_ref[...] = cnt0_ref[...]
        for tile in range(min(X_SLOTS - 1, n_tiles)):
            x_copy(tile).start()

    @pl.when(i + X_SLOTS - 1 < n_tiles)
    def _():
        x_copy(i + X_SLOTS - 1).start()

    x_copy(i).wait()
    x_ref = x_ring.at[i % X_SLOTS]

    gate1 = mod_ref[0, :, 2 * D_MODEL:3 * D_MODEL]
    shift2 = mod_ref[0, :, 3 * D_MODEL:4 * D_MODEL]
    scale2 = mod_ref[0, :, 4 * D_MODEL:5 * D_MODEL]
    y = (jnp.dot(yr_ref[...], wout_ref[0:D_RNN, :], preferred_element_type=F32)
         + jnp.dot(ys_ref[...], wout_ref[D_RNN:, :], preferred_element_type=F32))
    x1 = _load_x(x_ref, pos_refs, 0, TP) + _rms(y) * (gate1 * gpost_ref[...])
    x1_ref[...] = x1
    hn = _rms(x1) * (gpre_ref[...] * (1.0 + scale2)) + shift2
    _store_token_major(hn_ref, hn)
    lt = lax.dot_general(rw_ref[...], hn.astype(BF16), (((1,), (1,)), ((), ())),
                         preferred_element_type=F32) + rb_ref[:, 0:1]
    bucket = _route(lt)
    onehot = lax.broadcasted_iota(jnp.int32, (ROUTER_ROWS, TP), 0) == bucket
    before = jnp.dot(onehot.astype(BF16), earlier_ref[...], preferred_element_type=F32) + run_ref[:, 0:1]
    rank = jnp.sum(jnp.where(onehot, before, 0.0), axis=0, keepdims=True).astype(jnp.int32)
    row = lax.broadcasted_iota(jnp.int32, (SUBLANES, TP), 0)
    rt_ref[...] = jnp.where(row == 0, bucket, jnp.where(row == 1, rank, 0))
    run_ref[...] += jnp.sum(onehot.astype(F32), axis=1, keepdims=True)
    cnt_ref[...] = run_ref[...]


def _postmix(x, y_rnn, y_sgu, mod3, cond_of_tile, g_post, g_pre, w_out_b, router_wt, router_bt, earlier, pos_tab,
             counts0):
    n_tok = x.shape[0]
    n_tiles = n_tok // TP
    add_pos = pos_tab is not None
    tok = lambda i: (i, 0)
    const2 = lambda i: (0, 0)
    in_specs = [pl.BlockSpec(memory_space=pl.ANY)]
    args = [x]
    if add_pos:
        reps = TP // GRID_W
        tiles_per_seq = GRID_W // reps
        in_specs += [pl.BlockSpec((None, reps, D_MODEL // 2), lambda i: (i % tiles_per_seq, 0, 0)),
                     pl.BlockSpec((GRID_W, D_MODEL // 2), const2)]
        args += [pos_tab.reshape(tiles_per_seq, reps, D_MODEL // 2), pos_tab]
    in_specs += [pl.BlockSpec((TP, D_RNN), tok),
                 pl.BlockSpec((TP, D_SGU), tok),
                 pl.BlockSpec((1, 1, 6 * D_MODEL), lambda i: (cond_of_tile(i), 0, 0)),
                 pl.BlockSpec((1, D_MODEL), const2),
                 pl.BlockSpec((1, D_MODEL), const2),
                 pl.BlockSpec((D_MODEL, D_MODEL), const2),
                 pl.BlockSpec((ROUTER_ROWS, D_MODEL), const2),
                 pl.BlockSpec((ROUTER_ROWS, LANES), const2),
                 pl.BlockSpec((TP, TP), const2),
                 pl.BlockSpec((ROUTER_ROWS, LANES), const2)]
    args += [y_rnn, y_sgu, mod3, g_post, g_pre, w_out_b, router_wt, router_bt, earlier, counts0]
    counts_spec = pl.BlockSpec((ROUTER_ROWS, LANES), const2)
    x1, hn, route, counts = pl.pallas_call(
        functools.partial(_postmix_kernel, add_pos=add_pos, n_tiles=n_tiles),
        out_shape=(jax.ShapeDtypeStruct((n_tok, D_MODEL), F32),
                   jax.ShapeDtypeStruct((n_tok * ROW_TILES, LANES), F32),
                   jax.ShapeDtypeStruct((n_tiles * SUBLANES, TP), jnp.int32),
                   jax.ShapeDtypeStruct((ROUTER_ROWS, LANES), F32)),
        grid=(n_tiles,),
        in_specs=in_specs,
        out_specs=(pl.BlockSpec((TP, D_MODEL), tok),
                   pl.BlockSpec((TP * ROW_TILES, LANES), tok),
                   pl.BlockSpec((SUBLANES, TP), tok),
                   counts_spec),
        scratch_shapes=[pltpu.VMEM((ROUTER_ROWS, LANES), F32), pltpu.VMEM((X_SLOTS, TP, D_MODEL), F32),
                        pltpu.SemaphoreType.DMA((X_SLOTS,))],
        compiler_params=_params(("arbitrary",)),
        name="postmix",
    )(*args)
    route = route.reshape(n_tiles, SUBLANES, TP)
    return x1, hn, route[:, 0].reshape(n_tok), route[:, 1].reshape(n_tok), counts


def _token_rows(ref, t):
    return ref.at[pl.ds(pl.multiple_of(t * ROW_TILES, ROW_TILES), ROW_TILES), :]


def _dispatch_kernel(dest_ref, hc_ref, hs_ref, xs_ref, sem, *, n_ctx_steps):
    i = pl.program_id(0)
    base = i * TD

    def scatter(src_ref):
        def start(g, carry):
            for u in range(DMA_UNROLL):
                r = g * DMA_UNROLL + u
                pltpu.make_async_copy(_token_rows(src_ref, r), _token_rows(xs_ref, dest_ref[base + r]),
                                      sem).start(priority=u % 2)
            return carry

        lax.fori_loop(0, TD // DMA_UNROLL, start, 0)
        pltpu.make_async_copy(src_ref, xs_ref.at[pl.ds(0, TD * ROW_TILES), :], sem).wait()

    @pl.when(i < n_ctx_steps)
    def _():
        scatter(hc_ref)

    @pl.when(i >= n_ctx_steps)
    def _():
        scatter(hs_ref)


def _dispatch(dest, hn_ctx, hn_dec, n_slots):
    n_ctx_steps = hn_ctx.shape[0] // (TD * ROW_TILES)
    n_dec_steps = hn_dec.shape[0] // (TD * ROW_TILES)
    return pl.pallas_call(
        functools.partial(_dispatch_kernel, n_ctx_steps=n_ctx_steps),
        out_shape=jax.ShapeDtypeStruct((n_slots * ROW_TILES, LANES), F32),
        grid_spec=pltpu.PrefetchScalarGridSpec(
            num_scalar_prefetch=1,
            grid=(n_ctx_steps + n_dec_steps,),
            in_specs=[pl.BlockSpec((TD * ROW_TILES, LANES), lambda i, d: (jnp.minimum(i, n_ctx_steps - 1), 0)),
                      pl.BlockSpec((TD * ROW_TILES, LANES), lambda i, d: (jnp.maximum(i - n_ctx_steps, 0), 0))],
            out_specs=pl.BlockSpec(memory_space=pl.ANY),
            scratch_shapes=[pltpu.SemaphoreType.DMA(())]),
        compiler_params=_params(("arbitrary",)),
        name="dispatch",
    )(dest, hn_ctx, hn_dec)


def _experts_kernel(ea_ref, eb_ref, nv_ref, blk_ref, xs_ref, rw_ref, rb_ref, *refs):
    w32_refs, ys_ref, w_refs = refs[:6], refs[6], refs[7:]
    wga_ref, wua_ref, wda_ref, wgb_ref, wub_ref, wdb_ref = w_refs
    i = pl.program_id(0)
    nv = nv_ref[i]
    prev = jnp.maximum(i - 1, 0)

    for e_ref, first in ((ea_ref, 0), (eb_ref, 3)):
        @pl.when((i == 0) | (e_ref[i] != e_ref[prev]))
        def _(first=first):
            for w32_ref, w_ref in zip(w32_refs[first:first + 3], w_refs[first:first + 3]):
                w_ref[...] = w32_ref[0].astype(BF16)

    @pl.when(nv > 0)
    def _():
        row = lax.broadcasted_iota(jnp.int32, (TMX, 1), 0)
        xb = jnp.where(row < nv, _load_token_major(xs_ref, TMX), 0.0).astype(BF16)
        logits = jnp.dot(xb, rw_ref[...], preferred_element_type=F32) + rb_ref[...]
        lane = lax.broadcasted_iota(jnp.int32, logits.shape, 1)
        ea = ea_ref[i]
        eb = eb_ref[i]
        gmask = lane < N_GROUPS
        gl = jnp.where(gmask, logits, -jnp.inf)
        gmax = jnp.max(gl, axis=-1, keepdims=True)
        gexp = jnp.where(gmask, jnp.exp(gl - gmax), 0.0)
        g_own = jnp.sum(jnp.where(lane == ea // EXPERTS_PER_GROUP, gexp, 0.0), axis=-1, keepdims=True)
        g_w = g_own / jnp.sum(gexp, axis=-1, keepdims=True)
        la = jnp.sum(jnp.where(lane == ea + E_LANE0, logits, 0.0), axis=-1, keepdims=True)
        lb = jnp.sum(jnp.where(lane == eb + E_LANE0, logits, 0.0), axis=-1, keepdims=True)
        m = jnp.maximum(la, lb)
        pa = jnp.exp(la - m)
        pb = jnp.exp(lb - m)
        inv = g_w / (pa + pb)

        def hidden(x, wg_ref, wu_ref, w):
            g = jnp.dot(x, wg_ref[...], preferred_element_type=F32)
            u = jnp.dot(x, wu_ref[...], preferred_element_type=F32)
            return ((g * _sigmoid(g)) * u * w).astype(BF16)

        part = TMX // EXPERT_ROW_PARTS
        ys = []
        for h in range(EXPERT_ROW_PARTS):
            rows = slice(h * part, (h + 1) * part)
            act_a = hidden(xb[rows], wga_ref, wua_ref, (pa * inv)[rows])
            act_b = hidden(xb[rows], wgb_ref, wub_ref, (pb * inv)[rows])
            y = (jnp.dot(act_a, wda_ref[...], preferred_element_type=F32)
                 + jnp.dot(act_b, wdb_ref[...], preferred_element_type=F32))
            ys.append(y)
        _store_token_major(ys_ref, jnp.concatenate(ys, axis=0))


def _experts(sched, xs, router_w, router_b, wg, wu, wd):
    ea, eb, nv, blk = sched
    n_tiles = ea.shape[0]
    rows = lambda i, ea, eb, nv, blk: (blk[i], 0)
    const2 = lambda i, ea, eb, nv, blk: (0, 0)
    exp_a = lambda i, ea, eb, nv, blk: (ea[i], 0, 0)
    exp_b = lambda i, ea, eb, nv, blk: (eb[i], 0, 0)
    w_in_spec = lambda m: pl.BlockSpec((1, D_MODEL, D_EXPERT), m)
    w_out_spec = lambda m: pl.BlockSpec((1, D_EXPERT, D_MODEL), m)
    return pl.pallas_call(
        _experts_kernel,
        out_shape=jax.ShapeDtypeStruct(xs.shape, F32),
        grid_spec=pltpu.PrefetchScalarGridSpec(
            num_scalar_prefetch=4,
            grid=(n_tiles,),
            in_specs=[pl.BlockSpec((TMX * ROW_TILES, LANES), rows),
                      pl.BlockSpec((D_MODEL, ROUTER_LANES), const2),
                      pl.BlockSpec((1, ROUTER_LANES), const2),
                      w_in_spec(exp_a), w_in_spec(exp_a), w_out_spec(exp_a),
                      w_in_spec(exp_b), w_in_spec(exp_b), w_out_spec(exp_b)],
            out_specs=pl.BlockSpec((TMX * ROW_TILES, LANES), rows),
            scratch_shapes=[pltpu.VMEM((D_MODEL, D_EXPERT), BF16), pltpu.VMEM((D_MODEL, D_EXPERT), BF16),
                            pltpu.VMEM((D_EXPERT, D_MODEL), BF16)] * 2),
        compiler_params=_params(("arbitrary",)),
        name="experts",
    )(ea, eb, nv, blk, xs, router_w, router_b, wg, wu, wd, wg, wu, wd)


def _combine_kernel(dest_ref, ys_ref, x1_ref, mod_ref, gpost_ref, o_ref, ybuf, sems, *, tile0):
    i = pl.program_id(0)
    n = pl.num_programs(0)

    def fetch(tile, slot):
        base = (tile + tile0) * TC

        def start(g, carry):
            for u in range(DMA_UNROLL):
                r = g * DMA_UNROLL + u
                pltpu.make_async_copy(_token_rows(ys_ref, dest_ref[base + r]), _token_rows(ybuf.at[slot], r),
                                      sems.at[slot]).start(priority=u % 2)
            return carry

        lax.fori_loop(0, TC // DMA_UNROLL, start, 0)

    @pl.when(i == 0)
    def _():
        fetch(0, 0)

    @pl.when(i + 1 < n)
    def _():
        fetch(i + 1, (i + 1) % 2)

    slot = i % 2
    pltpu.make_async_copy(ys_ref.at[pl.ds(0, TC * ROW_TILES), :], ybuf.at[slot], sems.at[slot]).wait()
    gate2 = mod_ref[0, :, 5 * D_MODEL:6 * D_MODEL]
    o_ref[...] = x1_ref[...] + _rms(_load_token_major(ybuf.at[slot], TC)) * (gate2 * gpost_ref[...])


def _combine(dest, ys, x1, mod3, g_post, cond_of_tile, tile0):
    n_tok = x1.shape[0]
    return pl.pallas_call(
        functools.partial(_combine_kernel, tile0=tile0),
        out_shape=jax.ShapeDtypeStruct((n_tok, D_MODEL), F32),
        grid_spec=pltpu.PrefetchScalarGridSpec(
            num_scalar_prefetch=1,
            grid=(n_tok // TC,),
            in_specs=[pl.BlockSpec(memory_space=pl.ANY),
                      pl.BlockSpec((TC, D_MODEL), lambda i, d: (i, 0)),
                      pl.BlockSpec((1, 1, 6 * D_MODEL), lambda i, d: (cond_of_tile(i), 0, 0)),
                      pl.BlockSpec((1, D_MODEL), lambda i, d: (0, 0))],
            out_specs=pl.BlockSpec((TC, D_MODEL), lambda i, d: (i, 0)),
            scratch_shapes=[pltpu.VMEM((2, TC * ROW_TILES, LANES), F32), pltpu.SemaphoreType.DMA((2,))]),
        compiler_params=_params(("arbitrary",)),
        name="combine",
    )(dest, ys, x1, mod3, g_post)


def _schedule(bucket, rank, counts):
    n_tok = bucket.shape[0]
    n_max = n_tok // TMX + N_BUCKETS
    cnt = counts[:N_BUCKETS, 0].astype(jnp.int32)
    tiles = (cnt + TMX - 1) // TMX
    tile_end = jnp.cumsum(tiles)
    tile_start = tile_end - tiles
    ids = jnp.arange(N_BUCKETS, dtype=jnp.int32)
    slot0 = jnp.sum(jnp.where(bucket[:, None] == ids[None, :], (tile_start * TMX)[None, :], 0), axis=1)
    dest = slot0 + rank
    i = jnp.arange(n_max, dtype=jnp.int32)
    total = tile_end[-1]
    valid = i < total
    tb = jnp.sum((jnp.minimum(i, total - 1)[:, None] >= tile_end[None, :]).astype(jnp.int32), axis=1)
    pairs = [(a, b) for a in range(EXPERTS_PER_GROUP) for b in range(a + 1, EXPERTS_PER_GROUP)]
    ea_tab = jnp.array([g * EXPERTS_PER_GROUP + a for g in range(N_GROUPS) for a, _ in pairs], jnp.int32)
    eb_tab = jnp.array([g * EXPERTS_PER_GROUP + b for g in range(N_GROUPS) for _, b in pairs], jnp.int32)
    hit = tb[:, None] == ids[None, :]
    look = lambda tab: jnp.sum(jnp.where(hit, tab[None, :], 0), axis=1)
    ea, eb = look(ea_tab), look(eb_tab)
    nv = jnp.where(valid, jnp.clip(look(cnt) - (i - look(tile_start)) * TMX, 0, TMX), 0)
    blk = jnp.minimum(i, total - 1)
    return dest, (ea, eb, nv, blk), n_max * TMX


def _block_diag_gates(rg_wa, rg_wx):
    heads = LANES // HEAD_RNN

    def bd(w):
        w = w.reshape(2, RNN_BLOCKS, heads, HEAD_RNN, HEAD_RNN)
        eye = jnp.eye(heads, dtype=w.dtype)
        full = jnp.einsum('dghij,hk->dghikj', w, eye)
        return full.reshape(2, RNN_BLOCKS, LANES, LANES)

    return jnp.concatenate([bd(rg_wa), bd(rg_wx)], axis=-1).astype(BF16)


def _row_tile(v):
    blocks = v.reshape(v.shape[:-1] + (RNN_BLOCKS, LANES))
    return jnp.concatenate([blocks] * PAIR, axis=-2)


def _to_time_major_state(h):
    return h.reshape(h.shape[0] // PAIR, TMJ_ROWS, LANES)


def kernel(x_prompt, x_sample, state_rglru, c, c_ctx, w_mod, b_mod, g_pre_mix, g_post_mix, g_pre_ffn,
           g_post_ffn, w_in, conv_w, conv_b, rg_wa, rg_ba, rg_wx, rg_bx, rg_lambda, sgu_g, sgu_w, sgu_b,
           w_out, router_g_w, router_g_b, router_e_w, router_e_b, exp_w_gate, exp_w_up, exp_w_down):
    assert w_mod.shape[0] == 1, "single-layer trunk"
    n_ctx, ctx_len, _ = x_prompt.shape
    n_dec, dec_len, _ = x_sample.shape
    l = 0

    n_cond = SUBLANES
    ctx_rows = n_cond - n_dec
    ctx_per_step = max(PAIR, PREMIX_ROWS // min(ctx_len, TT))
    assert n_dec % PAIR == 0 and ctx_rows == ctx_per_step and n_dec % ctx_per_step == 0
    cond = jnp.concatenate([c, jnp.broadcast_to(c_ctx, (ctx_rows, D_MODEL))], axis=0)
    mod3 = _modulation(cond, w_mod[l], b_mod[l])
    pos_tab = _pos_table()

    w_rx, w_gate, w_u, w_v = jnp.split(w_in[l], [D_RNN, 2 * D_RNN, 2 * D_RNN + D_SGU], axis=1)
    w_in_b = jnp.concatenate([w_v, w_u, w_gate, w_rx], axis=1).astype(BF16)
    w_out_b = w_out[l].astype(BF16)
    sgu_w_b = sgu_w[l].reshape(2, 4 * CHUNK, CHUNK).astype(BF16)
    sgu_bias_tile = jnp.repeat(sgu_b[l].T, HEAD_SGU, axis=1)
    w_gates = _block_diag_gates(rg_wa[l], rg_wx[l])
    conv_rows = 0.5 * jnp.concatenate([conv_w[l], conv_b[l][None]], axis=0)
    per_dir_rows = jnp.concatenate([jnp.broadcast_to(conv_rows, (2,) + conv_rows.shape), 0.5 * rg_ba[l][:, None],
                                    0.5 * rg_bx[l][:, None], rg_lambda[l][:, None]], axis=1)
    scan_tiles = _row_tile(per_dir_rows)
    lane_pad = ROUTER_LANES - E_LANE0 - N_EXPERTS
    router_w = jnp.pad(jnp.concatenate([router_g_w[l], router_e_w[l]], axis=1), ((0, 0), (0, lane_pad))).astype(BF16)
    router_b = jnp.pad(jnp.concatenate([router_g_b[l], router_e_b[l]]), (0, lane_pad)).reshape(1, ROUTER_LANES)
    gap, tail = E_ROW0 - N_GROUPS, ROUTER_ROWS - E_ROW0 - N_EXPERTS
    router_wt = jnp.concatenate([router_g_w[l].T, jnp.zeros((gap, D_MODEL), F32), router_e_w[l].T,
                                 jnp.zeros((tail, D_MODEL), F32)], axis=0).astype(BF16)
    router_bt = jnp.concatenate([router_g_b[l], jnp.zeros((gap,), F32), router_e_b[l], jnp.zeros((tail,), F32)])
    router_bt = jnp.broadcast_to(router_bt[:, None], (ROUTER_ROWS, LANES))
    earlier = jnp.asarray(np.triu(np.ones((TP, TP), BF16), k=1))
    row = lambda v: v.reshape(1, -1)

    n_ctx_tok = n_ctx * ctx_len

    def mixer(x, h0, cond_of_tile, cond_block, use_pos, counts0):
        n_seq, seq_len, _ = x.shape
        xf = x.reshape(n_seq * seq_len, D_MODEL)
        tab = pos_tab if use_pos else None
        xr, gg, y_sgu = _premix(x, mod3, cond_block, row(g_pre_mix[l]), w_in_b, row(sgu_g[l]),
                                sgu_w_b, sgu_bias_tile, tab)
        scan_params = (scan_tiles, w_gates)
        hf, hf_last = _scan(xr, None, None, _to_time_major_state(h0[:, 0]), *scan_params, direction=0)
        y_rnn, hb_first = _scan(xr, gg, hf, _to_time_major_state(h0[:, 1]), *scan_params, direction=1)
        fstate = jnp.stack([hf_last.reshape(n_seq, D_RNN), hb_first.reshape(n_seq, D_RNN)], axis=1)
        x1, hn, bucket, rank, counts = _postmix(
            xf, y_rnn.reshape(n_seq * seq_len, D_RNN), y_sgu.reshape(n_seq * seq_len, D_SGU), mod3,
            cond_of_tile, row(g_post_mix[l]), row(g_pre_ffn[l]), w_out_b, router_wt, router_bt, earlier, tab,
            counts0)
        return x1, hn, bucket, rank, counts, fstate

    ctx_cond = lambda i: n_dec
    dec_cond = lambda i: i // (dec_len // TP)
    h0_ctx = jnp.zeros((n_ctx, 2, D_RNN), F32)
    counts0 = jnp.zeros((ROUTER_ROWS, LANES), F32)
    x1_ctx, hn_ctx, bucket_ctx, rank_ctx, counts, st = mixer(x_prompt, h0_ctx, ctx_cond, lambda p: n_dec // ctx_per_step,
                                                             False, counts0)
    new_state = st.astype(state_rglru.dtype)[:, None]
    x1_dec, hn_dec, bucket_dec, rank_dec, counts, _ = mixer(x_sample, state_rglru[:, l].astype(F32), dec_cond,
                                                            lambda p: p, True, counts)

    dest, sched, n_slots = _schedule(jnp.concatenate([bucket_ctx, bucket_dec]),
                                     jnp.concatenate([rank_ctx, rank_dec]), counts)
    xs = _dispatch(dest, hn_ctx, hn_dec, n_slots)
    ys = _experts(sched, xs, router_w, router_b, exp_w_gate[l], exp_w_up[l], exp_w_down[l])
    y_prompt = _combine(dest, ys, x1_ctx, mod3, row(g_post_ffn[l]), ctx_cond, 0)
    y_sample = _combine(dest, ys, x1_dec, mod3, row(g_post_ffn[l]), lambda i: i // (dec_len // TC), n_ctx_tok // TC)
    return (y_prompt.reshape(x_prompt.shape), y_sample.reshape(x_sample.shape), new_state)
```
